```python
import jax
import jax.numpy as jnp
from jax import lax
import numpy as np

D_MODEL = 1024
BATCH = 32
SEQ = 2048
DEPTH = 2

CTX_LEN = 256
GRID_W = 64
HEAD_DIM = 64
NA_HEADS = 4
NA_ROWS = 8
NA_COLS = 16
GQA_HEADS = 8
GQA_KV_HEADS = 2
MLA_HEADS = 4
MLA_Q_RANK = 256
MLA_KV_RANK = 128
MLA_NOPE = 64
MLA_ROPE = 32
MLA_V = 64
N_BRANCH = 3
N_MOD = 9
D_FF = ((8 * D_MODEL // 3 + 127) // 128) * 128
Q_BLOCK = 128
ROPE_THETA = 10000.0
EPS = 1e-6
NEG_BIG = -1e30

NA_WIDTH = NA_HEADS * HEAD_DIM
GQA_Q_WIDTH = GQA_HEADS * HEAD_DIM
GQA_KV_WIDTH = GQA_KV_HEADS * HEAD_DIM
MLA_OUT_WIDTH = MLA_HEADS * MLA_V
IN_SIZES = (NA_WIDTH, NA_WIDTH, NA_WIDTH, GQA_Q_WIDTH, GQA_KV_WIDTH, GQA_KV_WIDTH,
            MLA_Q_RANK, MLA_KV_RANK, MLA_ROPE, N_BRANCH * D_MODEL)
IN_SPLITS = tuple(int(s) for s in np.cumsum(IN_SIZES)[:-1])
IN_WIDTH = sum(IN_SIZES)

kernel_name = "hybrid_na_gqa_mla_macaron_dit"


def rmsnorm(x, g):
    xf = x.astype(jnp.float32)
    y = xf * lax.rsqrt(jnp.mean(xf * xf, axis=-1, keepdims=True) + EPS)
    return (y * g.astype(jnp.float32)).astype(x.dtype)


def modulate(h, shift, scale):
    return h * (1.0 + scale) + shift


def swiglu(h, w_gate, w_up, w_down):
    return (jax.nn.silu(h @ w_gate) * (h @ w_up)) @ w_down


def half_ffn(h, shift, scale, gate, g_norm, w_gate, w_up, w_down):
    n = modulate(rmsnorm(h, g_norm), shift, scale)
    return h + 0.5 * gate * swiglu(n, w_gate, w_up, w_down)


def axial_rope(n_tok, d_rot):
    half = d_rot // 2
    freqs = ROPE_THETA ** (-jnp.arange(0, half, 2, dtype=jnp.float32) / half)
    t = jnp.arange(n_tok)
    row = (t // GRID_W).astype(jnp.float32)[:, None] * freqs
    col = (t % GRID_W).astype(jnp.float32)[:, None] * freqs
    ang = jnp.concatenate([row, row, col, col], axis=-1)
    return jnp.cos(ang), jnp.sin(ang)


def apply_rope(x, cos, sin):
    xr = x.reshape(x.shape[:-1] + (2, 2, -1))
    rot = jnp.concatenate([-xr[..., 1:, :], xr[..., :1, :]], axis=-2).reshape(x.shape)
    return x * cos.astype(x.dtype) + rot * sin.astype(x.dtype)


def project_heads(n, w_in, q_norm, k_norm, mq_norm, mkv_norm, w_uq, w_ukv, rope):
    bn, t_len, _ = n.shape
    a_q, a_k, a_v, b_q, b_k, b_v, c_q, c_kv, c_kr, gates = jnp.split(n @ w_in, IN_SPLITS, axis=-1)

    def heads(t, h):
        return t.reshape(bn, t_len, h, -1).transpose(0, 2, 1, 3)

    attn_a = (heads(a_q, NA_HEADS), heads(a_k, NA_HEADS), heads(a_v, NA_HEADS))
    b_q = rmsnorm(heads(b_q, GQA_HEADS), q_norm)
    b_k = rmsnorm(heads(b_k, GQA_KV_HEADS), k_norm)
    q_lat = heads(rmsnorm(c_q, mq_norm) @ w_uq, MLA_HEADS)
    kv_lat = heads(rmsnorm(c_kv, mkv_norm) @ w_ukv, MLA_HEADS)
    q_nope, q_rope = q_lat[..., :MLA_NOPE], q_lat[..., MLA_NOPE:]
    k_nope, m_v = kv_lat[..., :MLA_NOPE], kv_lat[..., MLA_NOPE:]
    k_rope = c_kr[:, None]
    if rope is not None:
        cos_b, sin_b, cos_m, sin_m = rope
        b_q = apply_rope(b_q, cos_b, sin_b)
        b_k = apply_rope(b_k, cos_b, sin_b)
        q_rope = apply_rope(q_rope, cos_m, sin_m)
        k_rope = apply_rope(k_rope, cos_m, sin_m)
    m_q = jnp.concatenate([q_nope, q_rope], axis=-1)
    m_k = jnp.concatenate([k_nope, jnp.broadcast_to(k_rope, k_nope.shape[:-1] + (MLA_ROPE,))], axis=-1)
    return attn_a, (b_q, b_k, heads(b_v, GQA_KV_HEADS)), (m_q, m_k, m_v), gates


def block_attention(q, k, v, scale):
    bn, hq, t_len, dk = q.shape
    hkv = k.shape[1]
    grp = hq // hkv
    nblk = t_len // Q_BLOCK
    qb = q.reshape(bn, hkv, grp, nblk, Q_BLOCK, dk).transpose(3, 0, 1, 2, 4, 5)

    def one_block(qi):
        s = jnp.einsum('bhgqd,bhkd->bhgqk', qi, k).astype(jnp.float32) * scale
        p = jax.nn.softmax(s, axis=-1).astype(v.dtype)
        return jnp.einsum('bhgqk,bhkd->bhgqd', p, v)

    o = lax.map(one_block, qb)
    return o.transpose(1, 2, 3, 0, 4, 5).reshape(bn, hq, t_len, v.shape[-1])


def neighbourhood_attention(q, k, v, k_ctx, v_ctx, rel_bias):
    bn, h, s_len, d = q.shape
    rows = s_len // GRID_W
    wr = min(NA_ROWS, rows)
    scale = d ** -0.5

    def grid(t):
        return t.reshape(bn, h, rows, GRID_W, d)

    kg, vg = grid(k), grid(v)
    q_rows = jnp.moveaxis(grid(q), 2, 0)
    cols = jnp.arange(GRID_W)
    c0 = jnp.clip(cols - NA_COLS // 2, 0, GRID_W - NA_COLS)
    col_in = (cols[None, :] >= c0[:, None]) & (cols[None, :] < c0[:, None] + NA_COLS)
    col_idx = jnp.clip(cols[None, :] - cols[:, None] + NA_COLS - 1, 0, 2 * NA_COLS - 2)
    mask = jnp.broadcast_to(col_in[:, None, :], (GRID_W, wr, GRID_W)).reshape(GRID_W, wr * GRID_W)
    n_loc = wr * GRID_W

    def row_block(args):
        qr, r = args
        r0 = jnp.clip(r - wr // 2, 0, rows - wr)
        kr = lax.dynamic_slice_in_dim(kg, r0, wr, axis=2).reshape(bn, h, n_loc, d)
        vr = lax.dynamic_slice_in_dim(vg, r0, wr, axis=2).reshape(bn, h, n_loc, d)
        dr_idx = r0 + jnp.arange(wr) - r + (NA_ROWS - 1)
        bias = rel_bias[:, dr_idx[None, :, None], col_idx[:, None, :]].reshape(h, GRID_W, n_loc)
        s_loc = jnp.einsum('bhqd,bhkd->bhqk', qr, kr).astype(jnp.float32) * scale + bias.astype(jnp.float32)
        s_loc = jnp.where(mask, s_loc, NEG_BIG)
        s_ctx = jnp.einsum('bhqd,bhkd->bhqk', qr, k_ctx).astype(jnp.float32) * scale
        p = jax.nn.softmax(jnp.concatenate([s_loc, s_ctx], axis=-1), axis=-1).astype(v.dtype)
        return (jnp.einsum('bhqk,bhkd->bhqd', p[..., :n_loc], vr)
                + jnp.einsum('bhqk,bhkd->bhqd', p[..., n_loc:], v_ctx))

    o = lax.map(row_block, (q_rows, jnp.arange(rows)))
    return jnp.moveaxis(o, 0, 2).reshape(bn, h, s_len, d)


def merge_branches(o_a, o_b, o_m, gate_logits, w_a, w_b, w_m, w_o):
    def flat(o):
        return o.transpose(0, 2, 1, 3).reshape(o.shape[0], o.shape[2], -1)

    g_a, g_b, g_m = jnp.split(jax.nn.sigmoid(gate_logits), N_BRANCH, axis=-1)
    y = g_a * (flat(o_a) @ w_a) + g_b * (flat(o_b) @ w_b) + g_m * (flat(o_m) @ w_m)
    return y @ w_o


def _fwd_setup_inputs(seed: int = 0) -> dict:
    key = jax.random.key(seed)
    ks = iter(jax.random.split(key, 32))
    d = D_MODEL

    def nrm(shape, s):
        return jax.random.normal(next(ks), shape, jnp.float32) * s

    def lin(shape):
        return nrm(shape, shape[-2] ** -0.5)

    def gain(shape):
        return 1.0 + nrm(shape, 0.1)

    return {
        "x": nrm((BATCH, SEQ, d), 1.0),
        "c": nrm((BATCH, d), 1.0),
        "ctx": nrm((BATCH, CTX_LEN, d), 1.0),
        "c_ctx": nrm((d,), 1.0),
        "w_ada": nrm((DEPTH, d, N_MOD * d), 0.5 * d ** -0.5),
        "b_ada": nrm((DEPTH, N_MOD * d), 0.02),
        "ffn1_norm": gain((DEPTH, d)),
        "ffn1_w_gate": lin((DEPTH, d, D_FF)),
        "ffn1_w_up": lin((DEPTH, d, D_FF)),
        "ffn1_w_down": lin((DEPTH, D_FF, d)),
        "mix_norm": gain((DEPTH, d)),
        "w_in": lin((DEPTH, d, IN_WIDTH)),
        "na_rel_bias": nrm((DEPTH, NA_HEADS, 2 * NA_ROWS - 1, 2 * NA_COLS - 1), 0.2),
        "gqa_q_norm": gain((DEPTH, HEAD_DIM)),
        "gqa_k_norm": gain((DEPTH, HEAD_DIM)),
        "mla_q_norm": gain((DEPTH, MLA_Q_RANK)),
        "mla_kv_norm": gain((DEPTH, MLA_KV_RANK)),
        "mla_w_uq": lin((DEPTH, MLA_Q_RANK, MLA_HEADS * (MLA_NOPE + MLA_ROPE))),
        "mla_w_ukv": lin((DEPTH, MLA_KV_RANK, MLA_HEADS * (MLA_NOPE + MLA_V))),
        "w_branch_a": lin((DEPTH, NA_WIDTH, d)),
        "w_branch_b": lin((DEPTH, GQA_Q_WIDTH, d)),
        "w_branch_c": lin((DEPTH, MLA_OUT_WIDTH, d)),
        "w_out": lin((DEPTH, d, d)),
        "ffn2_norm": gain((DEPTH, d)),
        "ffn2_w_gate": lin((DEPTH, d, D_FF)),
        "ffn2_w_up": lin((DEPTH, d, D_FF)),
        "ffn2_w_down": lin((DEPTH, D_FF, d)),
        "final_norm": gain((d,)),
    }


def _fwd_reference(x, c, ctx, c_ctx, w_ada, b_ada, ffn1_norm, ffn1_w_gate, ffn1_w_up, ffn1_w_down,
              mix_norm, w_in, na_rel_bias, gqa_q_norm, gqa_k_norm, mla_q_norm, mla_kv_norm,
              mla_w_uq, mla_w_ukv, w_branch_a, w_branch_b, w_branch_c, w_out,
              ffn2_norm, ffn2_w_gate, ffn2_w_up, ffn2_w_down, final_norm):
    n_tok = x.shape[1]
    cos_b, sin_b = axial_rope(n_tok, HEAD_DIM)
    cos_m, sin_m = axial_rope(n_tok, MLA_ROPE)
    rope = (cos_b, sin_b, cos_m, sin_m)
    na_scale = HEAD_DIM ** -0.5
    gqa_scale = HEAD_DIM ** -0.5
    mla_scale = (MLA_NOPE + MLA_ROPE) ** -0.5

    for i in range(DEPTH):
        mod_x = [m[:, None, :] for m in jnp.split(jax.nn.silu(c) @ w_ada[i] + b_ada[i], N_MOD, axis=-1)]
        mod_c = jnp.split(jax.nn.silu(c_ctx) @ w_ada[i] + b_ada[i], N_MOD, axis=-1)
        ffn1 = (ffn1_norm[i], ffn1_w_gate[i], ffn1_w_up[i], ffn1_w_down[i])
        ffn2 = (ffn2_norm[i], ffn2_w_gate[i], ffn2_w_up[i], ffn2_w_down[i])
        proj = (w_in[i], gqa_q_norm[i], gqa_k_norm[i], mla_q_norm[i], mla_kv_norm[i], mla_w_uq[i], mla_w_ukv[i])
        merge = (w_branch_a[i], w_branch_b[i], w_branch_c[i], w_out[i])

        x = half_ffn(x, mod_x[0], mod_x[1], mod_x[2], *ffn1)
        ctx = half_ffn(ctx, mod_c[0], mod_c[1], mod_c[2], *ffn1)

        nx = modulate(rmsnorm(x, mix_norm[i]), mod_x[3], mod_x[4])
        nc = modulate(rmsnorm(ctx, mix_norm[i]), mod_c[3], mod_c[4])
        (aq, ak, av), (bq, bk, bv), (mq, mk, mv), gates = project_heads(nx, *proj, rope)
        (caq, cak, cav), (cbq, cbk, cbv), (cmq, cmk, cmv), cgates = project_heads(nc, *proj, None)

        o_a = neighbourhood_attention(aq, ak, av, cak, cav, na_rel_bias[i])
        o_b = block_attention(bq, jnp.concatenate([bk, cbk], axis=2), jnp.concatenate([bv, cbv], axis=2), gqa_scale)
        o_m = block_attention(mq, jnp.concatenate([mk, cmk], axis=2), jnp.concatenate([mv, cmv], axis=2), mla_scale)
        x = x + mod_x[5] * merge_branches(o_a, o_b, o_m, gates, *merge)
        x = half_ffn(x, mod_x[6], mod_x[7], mod_x[8], *ffn2)

        if i < DEPTH - 1:
            oc_a = block_attention(caq, cak, cav, na_scale)
            oc_b = block_attention(cbq, cbk, cbv, gqa_scale)
            oc_m = block_attention(cmq, cmk, cmv, mla_scale)
            ctx = ctx + mod_c[5] * merge_branches(oc_a, oc_b, oc_m, cgates, *merge)
            ctx = half_ffn(ctx, mod_c[6], mod_c[7], mod_c[8], *ffn2)

    return rmsnorm(x, final_norm)


import jax as _jax
import jax.numpy as _jnp

TWIN_FORMAT = 'train_step'
FWD_PARAMS = ['x', 'c', 'ctx', 'c_ctx', 'w_ada', 'b_ada', 'ffn1_norm', 'ffn1_w_gate', 'ffn1_w_up', 'ffn1_w_down', 'mix_norm', 'w_in', 'na_rel_bias', 'gqa_q_norm', 'gqa_k_norm', 'mla_q_norm', 'mla_kv_norm', 'mla_w_uq', 'mla_w_ukv', 'w_branch_a', 'w_branch_b', 'w_branch_c', 'w_out', 'ffn2_norm', 'ffn2_w_gate', 'ffn2_w_up', 'ffn2_w_down', 'final_norm']
TWIN_WEIGHTS = ['c_ctx', 'w_ada', 'b_ada', 'ffn1_norm', 'ffn1_w_gate', 'ffn1_w_up', 'ffn1_w_down', 'mix_norm', 'w_in', 'na_rel_bias', 'gqa_q_norm', 'gqa_k_norm', 'mla_q_norm', 'mla_kv_norm', 'mla_w_uq', 'mla_w_ukv', 'w_branch_a', 'w_branch_b', 'w_branch_c', 'w_out', 'ffn2_norm', 'ffn2_w_gate', 'ffn2_w_up', 'ffn2_w_down', 'final_norm']
TWIN_DIFF_INPUT = 'x'
TWIN_INPUTS = ['x', 'c', 'ctx', 'c_ctx', 'w_ada', 'b_ada', 'ffn1_norm', 'ffn1_w_gate', 'ffn1_w_up', 'ffn1_w_down', 'mix_norm', 'w_in', 'na_rel_bias', 'gqa_q_norm', 'gqa_k_norm', 'mla_q_norm', 'mla_kv_norm', 'mla_w_uq', 'mla_w_ukv', 'w_branch_a', 'w_branch_b', 'w_branch_c', 'w_out', 'ffn2_norm', 'ffn2_w_gate', 'ffn2_w_up', 'ffn2_w_down', 'final_norm', 'loss_target', 'm_c_ctx', 'm_w_ada', 'm_b_ada', 'm_ffn1_norm', 'm_ffn1_w_gate', 'm_ffn1_w_up', 'm_ffn1_w_down', 'm_mix_norm', 'm_w_in', 'm_na_rel_bias', 'm_gqa_q_norm', 'm_gqa_k_norm', 'm_mla_q_norm', 'm_mla_kv_norm', 'm_mla_w_uq', 'm_mla_w_ukv', 'm_w_branch_a', 'm_w_branch_b', 'm_w_branch_c', 'm_w_out', 'm_ffn2_norm', 'm_ffn2_w_gate', 'm_ffn2_w_up', 'm_ffn2_w_down', 'm_final_norm', 'v_c_ctx', 'v_w_ada', 'v_b_ada', 'v_ffn1_norm', 'v_ffn1_w_gate', 'v_ffn1_w_up', 'v_ffn1_w_down', 'v_mix_norm', 'v_w_in', 'v_na_rel_bias', 'v_gqa_q_norm', 'v_gqa_k_norm', 'v_mla_q_norm', 'v_mla_kv_norm', 'v_mla_w_uq', 'v_mla_w_ukv', 'v_w_branch_a', 'v_w_branch_b', 'v_w_branch_c', 'v_w_out', 'v_ffn2_norm', 'v_ffn2_w_gate', 'v_ffn2_w_up', 'v_ffn2_w_down', 'v_final_norm']
TWIN_OUTPUTS = ['loss', 'grad_x', 'grad_c_ctx', 'grad_w_ada', 'grad_b_ada', 'grad_ffn1_norm', 'grad_ffn1_w_gate', 'grad_ffn1_w_up', 'grad_ffn1_w_down', 'grad_mix_norm', 'grad_w_in', 'grad_na_rel_bias', 'grad_gqa_q_norm', 'grad_gqa_k_norm', 'grad_mla_q_norm', 'grad_mla_kv_norm', 'grad_mla_w_uq', 'grad_mla_w_ukv', 'grad_w_branch_a', 'grad_w_branch_b', 'grad_w_branch_c', 'grad_w_out', 'grad_ffn2_norm', 'grad_ffn2_w_gate', 'grad_ffn2_w_up', 'grad_ffn2_w_down', 'grad_final_norm', 'delta_c_ctx', 'delta_w_ada', 'delta_b_ada', 'delta_ffn1_norm', 'delta_ffn1_w_gate', 'delta_ffn1_w_up', 'delta_ffn1_w_down', 'delta_mix_norm', 'delta_w_in', 'delta_na_rel_bias', 'delta_gqa_q_norm', 'delta_gqa_k_norm', 'delta_mla_q_norm', 'delta_mla_kv_norm', 'delta_mla_w_uq', 'delta_mla_w_ukv', 'delta_w_branch_a', 'delta_w_branch_b', 'delta_w_branch_c', 'delta_w_out', 'delta_ffn2_norm', 'delta_ffn2_w_gate', 'delta_ffn2_w_up', 'delta_ffn2_w_down', 'delta_final_norm', 'new_m_c_ctx', 'new_m_w_ada', 'new_m_b_ada', 'new_m_ffn1_norm', 'new_m_ffn1_w_gate', 'new_m_ffn1_w_up', 'new_m_ffn1_w_down', 'new_m_mix_norm', 'new_m_w_in', 'new_m_na_rel_bias', 'new_m_gqa_q_norm', 'new_m_gqa_k_norm', 'new_m_mla_q_norm', 'new_m_mla_kv_norm', 'new_m_mla_w_uq', 'new_m_mla_w_ukv', 'new_m_w_branch_a', 'new_m_w_branch_b', 'new_m_w_branch_c', 'new_m_w_out', 'new_m_ffn2_norm', 'new_m_ffn2_w_gate', 'new_m_ffn2_w_up', 'new_m_ffn2_w_down', 'new_m_final_norm', 'new_v_c_ctx', 'new_v_w_ada', 'new_v_b_ada', 'new_v_ffn1_norm', 'new_v_ffn1_w_gate', 'new_v_ffn1_w_up', 'new_v_ffn1_w_down', 'new_v_mix_norm', 'new_v_w_in', 'new_v_na_rel_bias', 'new_v_gqa_q_norm', 'new_v_gqa_k_norm', 'new_v_mla_q_norm', 'new_v_mla_kv_norm', 'new_v_mla_w_uq', 'new_v_mla_w_ukv', 'new_v_w_branch_a', 'new_v_w_branch_b', 'new_v_w_branch_c', 'new_v_w_out', 'new_v_ffn2_norm', 'new_v_ffn2_w_gate', 'new_v_ffn2_w_up', 'new_v_ffn2_w_down', 'new_v_final_norm']
TWIN_LEAF_KINDS = {'loss': 'loss', 'grad_x': 'grad_x', 'grad_c_ctx': 'grad_w', 'grad_w_ada': 'grad_w', 'grad_b_ada': 'grad_w', 'grad_ffn1_norm': 'grad_w', 'grad_ffn1_w_gate': 'grad_w', 'grad_ffn1_w_up': 'grad_w', 'grad_ffn1_w_down': 'grad_w', 'grad_mix_norm': 'grad_w', 'grad_w_in': 'grad_w', 'grad_na_rel_bias': 'grad_w', 'grad_gqa_q_norm': 'grad_w', 'grad_gqa_k_norm': 'grad_w', 'grad_mla_q_norm': 'grad_w', 'grad_mla_kv_norm': 'grad_w', 'grad_mla_w_uq': 'grad_w', 'grad_mla_w_ukv': 'grad_w', 'grad_w_branch_a': 'grad_w', 'grad_w_branch_b': 'grad_w', 'grad_w_branch_c': 'grad_w', 'grad_w_out': 'grad_w', 'grad_ffn2_norm': 'grad_w', 'grad_ffn2_w_gate': 'grad_w', 'grad_ffn2_w_up': 'grad_w', 'grad_ffn2_w_down': 'grad_w', 'grad_final_norm': 'grad_w', 'delta_c_ctx': 'delta_w', 'delta_w_ada': 'delta_w', 'delta_b_ada': 'delta_w', 'delta_ffn1_norm': 'delta_w', 'delta_ffn1_w_gate': 'delta_w', 'delta_ffn1_w_up': 'delta_w', 'delta_ffn1_w_down': 'delta_w', 'delta_mix_norm': 'delta_w', 'delta_w_in': 'delta_w', 'delta_na_rel_bias': 'delta_w', 'delta_gqa_q_norm': 'delta_w', 'delta_gqa_k_norm': 'delta_w', 'delta_mla_q_norm': 'delta_w', 'delta_mla_kv_norm': 'delta_w', 'delta_mla_w_uq': 'delta_w', 'delta_mla_w_ukv': 'delta_w', 'delta_w_branch_a': 'delta_w', 'delta_w_branch_b': 'delta_w', 'delta_w_branch_c': 'delta_w', 'delta_w_out': 'delta_w', 'delta_ffn2_norm': 'delta_w', 'delta_ffn2_w_gate': 'delta_w', 'delta_ffn2_w_up': 'delta_w', 'delta_ffn2_w_down': 'delta_w', 'delta_final_norm': 'delta_w', 'new_m_c_ctx': 'new_m', 'new_m_w_ada': 'new_m', 'new_m_b_ada': 'new_m', 'new_m_ffn1_norm': 'new_m', 'new_m_ffn1_w_gate': 'new_m', 'new_m_ffn1_w_up': 'new_m', 'new_m_ffn1_w_down': 'new_m', 'new_m_mix_norm': 'new_m', 'new_m_w_in': 'new_m', 'new_m_na_rel_bias': 'new_m', 'new_m_gqa_q_norm': 'new_m', 'new_m_gqa_k_norm': 'new_m', 'new_m_mla_q_norm': 'new_m', 'new_m_mla_kv_norm': 'new_m', 'new_m_mla_w_uq': 'new_m', 'new_m_mla_w_ukv': 'new_m', 'new_m_w_branch_a': 'new_m', 'new_m_w_branch_b': 'new_m', 'new_m_w_branch_c': 'new_m', 'new_m_w_out': 'new_m', 'new_m_ffn2_norm': 'new_m', 'new_m_ffn2_w_gate': 'new_m', 'new_m_ffn2_w_up': 'new_m', 'new_m_ffn2_w_down': 'new_m', 'new_m_final_norm': 'new_m', 'new_v_c_ctx': 'new_v', 'new_v_w_ada': 'new_v', 'new_v_b_ada': 'new_v', 'new_v_ffn1_norm': 'new_v', 'new_v_ffn1_w_gate': 'new_v', 'new_v_ffn1_w_up': 'new_v', 'new_v_ffn1_w_down': 'new_v', 'new_v_mix_norm': 'new_v', 'new_v_w_in': 'new_v', 'new_v_na_rel_bias': 'new_v', 'new_v_gqa_q_norm': 'new_v', 'new_v_gqa_k_norm': 'new_v', 'new_v_mla_q_norm': 'new_v', 'new_v_mla_kv_norm': 'new_v', 'new_v_mla_w_uq': 'new_v', 'new_v_mla_w_ukv': 'new_v', 'new_v_w_branch_a': 'new_v', 'new_v_w_branch_b': 'new_v', 'new_v_w_branch_c': 'new_v', 'new_v_w_out': 'new_v', 'new_v_ffn2_norm': 'new_v', 'new_v_ffn2_w_gate': 'new_v', 'new_v_ffn2_w_up': 'new_v', 'new_v_ffn2_w_down': 'new_v', 'new_v_final_norm': 'new_v'}


def _forward(args):
    return _fwd_reference(*[args[k] for k in FWD_PARAMS])


def _output_shape():
    out = _jax.eval_shape(lambda: _forward(_fwd_setup_inputs(0)))
    return out.shape, out.dtype

N_MICROBATCH = 1
ADAM_LR = 0.001
ADAM_B1 = 0.9
ADAM_B2 = 0.999
ADAM_EPS = 1e-08
ADAM_WD = 0.01
ADAM_STEP = 10
PER_EXAMPLE_BATCH_AXIS = {'x': 0, 'c': 0, 'ctx': 0, 'loss_target': 0}
SHARED_INPUTS = []
_WEIGHT_DTYPES = {'c_ctx': _jnp.float32, 'w_ada': _jnp.float32, 'b_ada': _jnp.float32, 'ffn1_norm': _jnp.float32, 'ffn1_w_gate': _jnp.float32, 'ffn1_w_up': _jnp.float32, 'ffn1_w_down': _jnp.float32, 'mix_norm': _jnp.float32, 'w_in': _jnp.float32, 'na_rel_bias': _jnp.float32, 'gqa_q_norm': _jnp.float32, 'gqa_k_norm': _jnp.float32, 'mla_q_norm': _jnp.float32, 'mla_kv_norm': _jnp.float32, 'mla_w_uq': _jnp.float32, 'mla_w_ukv': _jnp.float32, 'w_branch_a': _jnp.float32, 'w_branch_b': _jnp.float32, 'w_branch_c': _jnp.float32, 'w_out': _jnp.float32, 'ffn2_norm': _jnp.float32, 'ffn2_w_gate': _jnp.float32, 'ffn2_w_up': _jnp.float32, 'ffn2_w_down': _jnp.float32, 'final_norm': _jnp.float32}
MOMENT_SCALE = {'c_ctx': 2.107610e-02, 'w_ada': 5.448755e-02, 'b_ada': 9.876930e-02, 'ffn1_norm': 4.031581e-02, 'ffn1_w_gate': 1.797737e-02, 'ffn1_w_up': 1.756484e-02, 'ffn1_w_down': 2.906833e-02, 'mix_norm': 1.573259e-02, 'w_in': 1.561090e-02, 'na_rel_bias': 2.458107e-03, 'gqa_q_norm': 1.584535e-02, 'gqa_k_norm': 1.480219e-02, 'mla_q_norm': 9.952199e-03, 'mla_kv_norm': 4.624239e-02, 'mla_w_uq': 7.775084e-03, 'mla_w_ukv': 2.640962e-02, 'w_branch_a': 1.673397e-02, 'w_branch_b': 2.548235e-02, 'w_branch_c': 2.067966e-02, 'w_out': 3.421053e-02, 'ffn2_norm': 3.817323e-02, 'ffn2_w_gate': 1.730113e-02, 'ffn2_w_up': 1.700829e-02, 'ffn2_w_down': 2.813741e-02, 'final_norm': 6.415085e+01}


def _to_microbatches(a, axis):
    t = _jnp.moveaxis(a, axis, 0)
    t = t.reshape((N_MICROBATCH, t.shape[0] // N_MICROBATCH) + t.shape[1:])
    return _jnp.moveaxis(t, 1, axis + 1)


def setup_inputs(seed: int = 0) -> dict:
    inp = _fwd_setup_inputs(seed)
    key = _jax.random.fold_in(_jax.random.key(seed), 7919)
    shape, _ = _output_shape()
    out = dict(inp)
    out["loss_target"] = _jax.random.normal(_jax.random.fold_in(key, 0), shape, _jnp.float32)
    for i, name in enumerate(TWIN_WEIGHTS):
        w = inp[name].astype(_jnp.float32)
        if MOMENT_SCALE is None:
            s = _jnp.sqrt(_jnp.mean(_jnp.square(w)) + 1e-30)
        else:
            s = MOMENT_SCALE[name]
        km, kv = _jax.random.split(_jax.random.fold_in(key, i + 1))
        out[name] = w
        out["m_" + name] = s * _jax.random.normal(km, w.shape, _jnp.float32)
        out["v_" + name] = (s * s) * _jax.random.uniform(kv, w.shape, _jnp.float32, 0.5, 1.5)
    if N_MICROBATCH > 1:
        for name, axis in PER_EXAMPLE_BATCH_AXIS.items():
            out[name] = _to_microbatches(out[name], axis)
    return {'x': out['x'], 'c': out['c'], 'ctx': out['ctx'], 'c_ctx': out['c_ctx'], 'w_ada': out['w_ada'], 'b_ada': out['b_ada'], 'ffn1_norm': out['ffn1_norm'], 'ffn1_w_gate': out['ffn1_w_gate'], 'ffn1_w_up': out['ffn1_w_up'], 'ffn1_w_down': out['ffn1_w_down'], 'mix_norm': out['mix_norm'], 'w_in': out['w_in'], 'na_rel_bias': out['na_rel_bias'], 'gqa_q_norm': out['gqa_q_norm'], 'gqa_k_norm': out['gqa_k_norm'], 'mla_q_norm': out['mla_q_norm'], 'mla_kv_norm': out['mla_kv_norm'], 'mla_w_uq': out['mla_w_uq'], 'mla_w_ukv': out['mla_w_ukv'], 'w_branch_a': out['w_branch_a'], 'w_branch_b': out['w_branch_b'], 'w_branch_c': out['w_branch_c'], 'w_out': out['w_out'], 'ffn2_norm': out['ffn2_norm'], 'ffn2_w_gate': out['ffn2_w_gate'], 'ffn2_w_up': out['ffn2_w_up'], 'ffn2_w_down': out['ffn2_w_down'], 'final_norm': out['final_norm'], 'loss_target': out['loss_target'], 'm_c_ctx': out['m_c_ctx'], 'm_w_ada': out['m_w_ada'], 'm_b_ada': out['m_b_ada'], 'm_ffn1_norm': out['m_ffn1_norm'], 'm_ffn1_w_gate': out['m_ffn1_w_gate'], 'm_ffn1_w_up': out['m_ffn1_w_up'], 'm_ffn1_w_down': out['m_ffn1_w_down'], 'm_mix_norm': out['m_mix_norm'], 'm_w_in': out['m_w_in'], 'm_na_rel_bias': out['m_na_rel_bias'], 'm_gqa_q_norm': out['m_gqa_q_norm'], 'm_gqa_k_norm': out['m_gqa_k_norm'], 'm_mla_q_norm': out['m_mla_q_norm'], 'm_mla_kv_norm': out['m_mla_kv_norm'], 'm_mla_w_uq': out['m_mla_w_uq'], 'm_mla_w_ukv': out['m_mla_w_ukv'], 'm_w_branch_a': out['m_w_branch_a'], 'm_w_branch_b': out['m_w_branch_b'], 'm_w_branch_c': out['m_w_branch_c'], 'm_w_out': out['m_w_out'], 'm_ffn2_norm': out['m_ffn2_norm'], 'm_ffn2_w_gate': out['m_ffn2_w_gate'], 'm_ffn2_w_up': out['m_ffn2_w_up'], 'm_ffn2_w_down': out['m_ffn2_w_down'], 'm_final_norm': out['m_final_norm'], 'v_c_ctx': out['v_c_ctx'], 'v_w_ada': out['v_w_ada'], 'v_b_ada': out['v_b_ada'], 'v_ffn1_norm': out['v_ffn1_norm'], 'v_ffn1_w_gate': out['v_ffn1_w_gate'], 'v_ffn1_w_up': out['v_ffn1_w_up'], 'v_ffn1_w_down': out['v_ffn1_w_down'], 'v_mix_norm': out['v_mix_norm'], 'v_w_in': out['v_w_in'], 'v_na_rel_bias': out['v_na_rel_bias'], 'v_gqa_q_norm': out['v_gqa_q_norm'], 'v_gqa_k_norm': out['v_gqa_k_norm'], 'v_mla_q_norm': out['v_mla_q_norm'], 'v_mla_kv_norm': out['v_mla_kv_norm'], 'v_mla_w_uq': out['v_mla_w_uq'], 'v_mla_w_ukv': out['v_mla_w_ukv'], 'v_w_branch_a': out['v_w_branch_a'], 'v_w_branch_b': out['v_w_branch_b'], 'v_w_branch_c': out['v_w_branch_c'], 'v_w_out': out['v_w_out'], 'v_ffn2_norm': out['v_ffn2_norm'], 'v_ffn2_w_gate': out['v_ffn2_w_gate'], 'v_ffn2_w_up': out['v_ffn2_w_up'], 'v_ffn2_w_down': out['v_ffn2_w_down'], 'v_final_norm': out['v_final_norm']}


def _loss(weights, diff, rest, loss_target):
    with _jax.named_scope("forward"):
        args = {**rest, TWIN_DIFF_INPUT: diff, **{k: w.astype(_WEIGHT_DTYPES[k]) for k, w in weights.items()}}
        y = _forward(args)
    with _jax.named_scope("loss_head"):
        err = _jnp.square(y.astype(_jnp.float32) - loss_target)
        return 0.5 * _jnp.sum(_jnp.mean(err, axis=-1)) if err.ndim else 0.5 * err


def _adamw(w, g, m, v):
    m = ADAM_B1 * m + (1.0 - ADAM_B1) * g
    v = ADAM_B2 * v + (1.0 - ADAM_B2) * _jnp.square(g)
    m_hat = m / (1.0 - ADAM_B1 ** ADAM_STEP)
    v_hat = v / (1.0 - ADAM_B2 ** ADAM_STEP)
    delta = -ADAM_LR * (m_hat / (_jnp.sqrt(v_hat) + ADAM_EPS) + ADAM_WD * w)
    return delta, m, v


def reference(x, c, ctx, c_ctx, w_ada, b_ada, ffn1_norm, ffn1_w_gate, ffn1_w_up, ffn1_w_down, mix_norm, w_in, na_rel_bias, gqa_q_norm, gqa_k_norm, mla_q_norm, mla_kv_norm, mla_w_uq, mla_w_ukv, w_branch_a, w_branch_b, w_branch_c, w_out, ffn2_norm, ffn2_w_gate, ffn2_w_up, ffn2_w_down, final_norm, loss_target, m_c_ctx, m_w_ada, m_b_ada, m_ffn1_norm, m_ffn1_w_gate, m_ffn1_w_up, m_ffn1_w_down, m_mix_norm, m_w_in, m_na_rel_bias, m_gqa_q_norm, m_gqa_k_norm, m_mla_q_norm, m_mla_kv_norm, m_mla_w_uq, m_mla_w_ukv, m_w_branch_a, m_w_branch_b, m_w_branch_c, m_w_out, m_ffn2_norm, m_ffn2_w_gate, m_ffn2_w_up, m_ffn2_w_down, m_final_norm, v_c_ctx, v_w_ada, v_b_ada, v_ffn1_norm, v_ffn1_w_gate, v_ffn1_w_up, v_ffn1_w_down, v_mix_norm, v_w_in, v_na_rel_bias, v_gqa_q_norm, v_gqa_k_norm, v_mla_q_norm, v_mla_kv_norm, v_mla_w_uq, v_mla_w_ukv, v_w_branch_a, v_w_branch_b, v_w_branch_c, v_w_out, v_ffn2_norm, v_ffn2_w_gate, v_ffn2_w_up, v_ffn2_w_down, v_final_norm):
    given = dict(x=x, c=c, ctx=ctx, c_ctx=c_ctx, w_ada=w_ada, b_ada=b_ada, ffn1_norm=ffn1_norm, ffn1_w_gate=ffn1_w_gate, ffn1_w_up=ffn1_w_up, ffn1_w_down=ffn1_w_down, mix_norm=mix_norm, w_in=w_in, na_rel_bias=na_rel_bias, gqa_q_norm=gqa_q_norm, gqa_k_norm=gqa_k_norm, mla_q_norm=mla_q_norm, mla_kv_norm=mla_kv_norm, mla_w_uq=mla_w_uq, mla_w_ukv=mla_w_ukv, w_branch_a=w_branch_a, w_branch_b=w_branch_b, w_branch_c=w_branch_c, w_out=w_out, ffn2_norm=ffn2_norm, ffn2_w_gate=ffn2_w_gate, ffn2_w_up=ffn2_w_up, ffn2_w_down=ffn2_w_down, final_norm=final_norm, loss_target=loss_target, m_c_ctx=m_c_ctx, m_w_ada=m_w_ada, m_b_ada=m_b_ada, m_ffn1_norm=m_ffn1_norm, m_ffn1_w_gate=m_ffn1_w_gate, m_ffn1_w_up=m_ffn1_w_up, m_ffn1_w_down=m_ffn1_w_down, m_mix_norm=m_mix_norm, m_w_in=m_w_in, m_na_rel_bias=m_na_rel_bias, m_gqa_q_norm=m_gqa_q_norm, m_gqa_k_norm=m_gqa_k_norm, m_mla_q_norm=m_mla_q_norm, m_mla_kv_norm=m_mla_kv_norm, m_mla_w_uq=m_mla_w_uq, m_mla_w_ukv=m_mla_w_ukv, m_w_branch_a=m_w_branch_a, m_w_branch_b=m_w_branch_b, m_w_branch_c=m_w_branch_c, m_w_out=m_w_out, m_ffn2_norm=m_ffn2_norm, m_ffn2_w_gate=m_ffn2_w_gate, m_ffn2_w_up=m_ffn2_w_up, m_ffn2_w_down=m_ffn2_w_down, m_final_norm=m_final_norm, v_c_ctx=v_c_ctx, v_w_ada=v_w_ada, v_b_ada=v_b_ada, v_ffn1_norm=v_ffn1_norm, v_ffn1_w_gate=v_ffn1_w_gate, v_ffn1_w_up=v_ffn1_w_up, v_ffn1_w_down=v_ffn1_w_down, v_mix_norm=v_mix_norm, v_w_in=v_w_in, v_na_rel_bias=v_na_rel_bias, v_gqa_q_norm=v_gqa_q_norm, v_gqa_k_norm=v_gqa_k_norm, v_mla_q_norm=v_mla_q_norm, v_mla_kv_norm=v_mla_kv_norm, v_mla_w_uq=v_mla_w_uq, v_mla_w_ukv=v_mla_w_ukv, v_w_branch_a=v_w_branch_a, v_w_branch_b=v_w_branch_b, v_w_branch_c=v_w_branch_c, v_w_out=v_w_out, v_ffn2_norm=v_ffn2_norm, v_ffn2_w_gate=v_ffn2_w_gate, v_ffn2_w_up=v_ffn2_w_up, v_ffn2_w_down=v_ffn2_w_down, v_final_norm=v_final_norm)
    weights = {n: given[n] for n in TWIN_WEIGHTS}
    shared = {n: given[n] for n in SHARED_INPUTS}
    per_example = {n: given[n] for n in ['x', 'c', 'ctx']}
    grad_fn = _jax.value_and_grad(_loss, argnums=(0, 1))

    def one_microbatch(ex, loss_target):
        ex = dict(ex)
        diff = ex.pop(TWIN_DIFF_INPUT)
        return grad_fn(weights, diff, {**shared, **ex}, loss_target)

    if N_MICROBATCH == 1:
        loss, (grad_w, grad_x) = one_microbatch(per_example, given["loss_target"])
    else:
        def body(carry, xs):
            loss_sum, grad_sum = carry
            l_k, (gw_k, gx_k) = one_microbatch(xs[0], xs[1])
            with _jax.named_scope("update"):
                return (loss_sum + l_k, _jax.tree.map(_jnp.add, grad_sum, gw_k)), gx_k

        init = (_jnp.zeros((), _jnp.float32), _jax.tree.map(_jnp.zeros_like, weights))
        (loss, grad_w), grad_x = _jax.lax.scan(body, init, (per_example, given["loss_target"]))
    with _jax.named_scope("update"):
        delta_w, new_m, new_v = {}, {}, {}
        for n in TWIN_WEIGHTS:
            delta_w[n], new_m[n], new_v[n] = _adamw(weights[n], grad_w[n], given["m_" + n], given["v_" + n])
    return (loss, grad_x, *[grad_w[n] for n in TWIN_WEIGHTS], *[delta_w[n] for n in TWIN_WEIGHTS],
            *[new_m[n] for n in TWIN_WEIGHTS], *[new_v[n] for n in TWIN_WEIGHTS])
```

```python
import functools

import jax
import jax.numpy as jnp
import numpy as np
from jax import lax
from jax.experimental import pallas as pl
from jax.experimental.pallas import tpu as pltpu

F32 = jnp.float32
BF16 = jnp.bfloat16
HI = lax.Precision.HIGHEST
MESH = pl.DeviceIdType.MESH
ANY = pl.BlockSpec(memory_space=pl.ANY)

V7X_VMEM_BYTES = 64 * 1024 * 1024
VMEM_LIMIT = V7X_VMEM_BYTES - 8 * 1024 * 1024
LANE = 128
PACK_W = 1024

GRID_W = 64
HEAD_DIM = 64
NA_HEADS, NA_ROWS, NA_COLS = 4, 8, 16
GQA_HEADS, GQA_KV_HEADS = 8, 2
MLA_HEADS, MLA_Q_RANK, MLA_KV_RANK, MLA_NOPE, MLA_ROPE, MLA_V = 4, 256, 128, 64, 32, 64
N_MOD = 9
ROPE_THETA = 10000.0
EPS = 1e-6
NEG_BIG = -1e30
NA_W = NA_HEADS * HEAD_DIM
GQ_W = GQA_HEADS * HEAD_DIM
GK_W = GQA_KV_HEADS * HEAD_DIM
MAIN_W = 3 * NA_W + GQ_W + 2 * GK_W + MLA_Q_RANK + MLA_KV_RANK + MLA_ROPE
MAIN_PAD = 2048
TQ = 256
TM = 256

ADAM_LR, ADAM_B1, ADAM_B2, ADAM_EPS, ADAM_WD, ADAM_STEP = 0.001, 0.9, 0.999, 1e-08, 0.01, 10

ARG_NAMES = ['x', 'c', 'ctx', 'c_ctx', 'w_ada', 'b_ada', 'ffn1_norm', 'ffn1_w_gate', 'ffn1_w_up', 'ffn1_w_down', 'mix_norm', 'w_in',
             'na_rel_bias', 'gqa_q_norm', 'gqa_k_norm', 'mla_q_norm', 'mla_kv_norm', 'mla_w_uq', 'mla_w_ukv', 'w_branch_a',
             'w_branch_b', 'w_branch_c', 'w_out', 'ffn2_norm', 'ffn2_w_gate', 'ffn2_w_up', 'ffn2_w_down', 'final_norm']
WEIGHTS = ARG_NAMES[3:]
BIG = [('ffn1_w_gate', 1), ('ffn1_w_up', 1), ('ffn1_w_down', 0), ('w_in', 1), ('mla_w_uq', 1), ('mla_w_ukv', 1),
       ('w_branch_a', 1), ('w_branch_b', 1), ('w_branch_c', 1), ('w_out', 0), ('ffn2_w_gate', 1), ('ffn2_w_up', 1),
       ('ffn2_w_down', 0)]
SMALL_LAYER = ['ffn1_norm', 'mix_norm', 'na_rel_bias', 'gqa_q_norm', 'gqa_k_norm', 'mla_q_norm', 'mla_kv_norm', 'ffn2_norm']


def _cp(*sem):
    return pltpu.CompilerParams(dimension_semantics=sem, vmem_limit_bytes=VMEM_LIMIT)


def _tile(dim, cands):
    for t in cands:
        if dim % t == 0:
            return t
    return dim


def _row_tile(rows, cap=512):
    best = None
    for t in range(8, min(rows, cap) + 1, 8):
        if rows % t == 0:
            best = t
    return best or rows


def _ceil_to(n, m):
    return -(-n // m) * m


def mm(a, b, *, name, ta=False, tb=False, out_dtype=F32, precise=False):
    m, k = (a.shape[1], a.shape[0]) if ta else a.shape
    n = b.shape[0] if tb else b.shape[1]
    tm = _tile(m, (512, 256, 128))
    tn = _tile(n, (1024, 1408, 512, 256, 128))
    tk = _tile(k, (1024, 1408, 512, 256, 128))
    nk = k // tk
    dims = (((0 if ta else 1,), (1 if tb else 0,)), ((), ()))

    def body(a_ref, b_ref, o_ref, *acc):
        if precise:
            part = lax.dot_general(a_ref[...].astype(F32), b_ref[...].astype(F32), dims, precision=HI, preferred_element_type=F32)
        else:
            part = lax.dot_general(a_ref[...].astype(BF16), b_ref[...].astype(BF16), dims, preferred_element_type=F32)
        if nk == 1:
            o_ref[...] = part.astype(o_ref.dtype)
        else:
            acc_ref, = acc
            kk = pl.program_id(2)

            @pl.when(kk == 0)
            def _():
                acc_ref[...] = part

            @pl.when(kk > 0)
            def _():
                acc_ref[...] += part

            @pl.when(kk == nk - 1)
            def _():
                o_ref[...] = acc_ref[...].astype(o_ref.dtype)

    a_spec = pl.BlockSpec((tk, tm), lambda i, j, kk: (kk, i)) if ta else pl.BlockSpec((tm, tk), lambda i, j, kk: (i, kk))
    b_spec = pl.BlockSpec((tn, tk), lambda i, j, kk: (j, kk)) if tb else pl.BlockSpec((tk, tn), lambda i, j, kk: (kk, j))
    return pl.pallas_call(
        body, name=name, grid=(m // tm, n // tn, nk), in_specs=[a_spec, b_spec],
        out_specs=pl.BlockSpec((tm, tn), lambda i, j, kk: (i, j)),
        out_shape=jax.ShapeDtypeStruct((m, n), out_dtype),
        scratch_shapes=[pltpu.VMEM((tm, tn), F32)] if nk > 1 else [],
        compiler_params=_cp("parallel", "parallel", "arbitrary"),
    )(a, b)


def rowcall(name, fn, ins, outs, *, nb, nt, nct):
    in_specs, arrays = [], []
    for arr, kind in ins:
        arrays.append(arr)
        if kind == 'tok':
            in_specs.append(pl.BlockSpec((None, TM, arr.shape[-1]), lambda b, t: (b, t, 0)))
        elif kind == 'lat':
            in_specs.append(pl.BlockSpec((None, TM, arr.shape[-1]), lambda b, t: (b, jnp.maximum(t - nct, 0), 0)))
        elif kind == 'pos':
            in_specs.append(pl.BlockSpec((TM, arr.shape[-1]), lambda b, t: (t, 0)))
        elif kind == 'mod':
            in_specs.append(pl.BlockSpec((None, None, 1, arr.shape[-1]), lambda b, t: (b, jnp.where(t >= nct, 1, 0), 0, 0)))
        elif kind == 'full':
            in_specs.append(pl.BlockSpec(arr.shape, lambda b, t, nd=arr.ndim: (0,) * nd))
        else:
            _, w, j = kind
            in_specs.append(pl.BlockSpec((None, TM, w), lambda b, t, j=j: (b, t, j)))
    out_specs, out_shape = [], []
    for o in outs:
        if o[0] == 'tok':
            out_specs.append(pl.BlockSpec((None, TM, o[1]), lambda b, t: (b, t, 0)))
            out_shape.append(jax.ShapeDtypeStruct((nb, nt * TM, o[1]), o[2]))
        elif o[0] == 'mod':
            out_specs.append(pl.BlockSpec((None, None, 1, o[1]), lambda b, t: (b, jnp.where(t >= nct, 1, 0), 0, 0)))
            out_shape.append(jax.ShapeDtypeStruct((nb, 2, 1, o[1]), F32))
        else:
            out_specs.append(pl.BlockSpec(o[1], lambda b, t, nd=len(o[1]): (0,) * nd))
            out_shape.append(jax.ShapeDtypeStruct(o[1], F32))
    n_in = len(ins)

    def body(*refs):
        b, t = pl.program_id(0), pl.program_id(1)
        res = fn(t < nct, *[r[...] for r in refs[:n_in]])
        for ref, o, val in zip(refs[n_in:], outs, res, strict=True):
            if o[0] == 'tok':
                ref[...] = val.astype(ref.dtype)
                continue
            first = ((t == 0) | (t == nct)) if o[0] == 'mod' else ((b == 0) & (t == 0))

            @pl.when(first)
            def _(ref=ref, val=val):
                ref[...] = val

            @pl.when(jnp.logical_not(first))
            def _(ref=ref, val=val):
                ref[...] += val

    return pl.pallas_call(body, name=name, grid=(nb, nt), in_specs=in_specs, out_specs=out_specs, out_shape=out_shape,
                          compiler_params=_cp("arbitrary", "arbitrary"))(*arrays)


def _rms(x, g):
    return x * lax.rsqrt(jnp.mean(x * x, axis=-1, keepdims=True) + EPS) * g


def f_normmod(h, g, shift, scale):
    return _rms(h, g) * (1.0 + scale) + shift


def f_act(gu):
    f = gu.shape[-1] // 2
    return jax.nn.silu(gu[:, :f]) * gu[:, f:]


def f_merge(ga, gb, gm, ya, yb, ym):
    return jax.nn.sigmoid(ga) * ya + jax.nn.sigmoid(gb) * yb + jax.nn.sigmoid(gm) * ym


def f_post(p, cb, sb, cm, sm, qn, kn, mqn, mkvn, wuq, wukv, s_b, r_b, t_b, r_m, rep):
    def hnorm(x, g, w):
        ms = jnp.dot(x * x, s_b[:w, :w], precision=HI, preferred_element_type=F32)
        gw = jnp.dot(g, t_b[:, :w], precision=HI, preferred_element_type=F32)
        return x * lax.rsqrt(ms + EPS) * gw

    def rope(x, cos, sin, rot):
        return x * cos + jnp.dot(x, rot, precision=HI, preferred_element_type=F32) * sin

    o = 3 * NA_W
    a_q, a_k, a_v = p[:, 0:NA_W], p[:, NA_W:2 * NA_W], p[:, 2 * NA_W:o]
    b_q = rope(hnorm(p[:, o:o + GQ_W], qn, GQ_W), cb, sb, r_b)
    o += GQ_W
    b_k = rope(hnorm(p[:, o:o + GK_W], kn, GK_W), cb[:, :GK_W], sb[:, :GK_W], r_b[:GK_W, :GK_W])
    b_v = p[:, o + GK_W:o + 2 * GK_W]
    o += 2 * GK_W
    q_lat = jnp.dot(_rms(p[:, o:o + MLA_Q_RANK], mqn).astype(BF16), wuq.astype(BF16), preferred_element_type=F32)
    o += MLA_Q_RANK
    kv_lat = jnp.dot(_rms(p[:, o:o + MLA_KV_RANK], mkvn).astype(BF16), wukv.astype(BF16), preferred_element_type=F32)
    o += MLA_KV_RANK
    nw = MLA_HEADS * MLA_NOPE
    mq_nope, mq_rope = q_lat[:, :nw], rope(q_lat[:, nw:], cm, sm, r_m)
    mk_nope, m_v = kv_lat[:, :nw], kv_lat[:, nw:]
    mk_rope = jnp.dot(rope(p[:, o:o + LANE], cm, sm, r_m), rep, precision=HI, preferred_element_type=F32)
    return (a_q, a_k, a_v, b_q, b_k, b_v, mq_nope, mq_rope, mk_nope, mk_rope, m_v)


POST_WIDTHS = (NA_W, NA_W, NA_W, GQ_W, GK_W, GK_W, MLA_HEADS * MLA_NOPE, MLA_HEADS * MLA_ROPE, MLA_HEADS * MLA_NOPE,
               MLA_HEADS * MLA_ROPE, MLA_HEADS * MLA_V)


_NT = (((1,), (1,)), ((), ()))
_TN = (((0,), (0,)), ((), ()))


def _dot(a, b, dims=None):
    if dims is None:
        return jnp.dot(a, b, preferred_element_type=F32)
    return lax.dot_general(a, b, dims, preferred_element_type=F32)


def att_fwd(q, k, v, *, scale, lc, name):
    nb, hq, t, dk = q.shape
    hkv, dv = k.shape[1], v.shape[-1]
    grp = hq // hkv
    nctb = lc // TQ

    def body(q_ref, k_ref, v_ref, o_ref, lse_ref):
        i = pl.program_id(2)

        def run(kk, vv):
            s = _dot(q_ref[...], kk, _NT) * scale
            m = jnp.max(s, axis=-1, keepdims=True)
            p = jnp.exp(s - m)
            l = jnp.sum(p, axis=-1, keepdims=True)
            o_ref[...] = (_dot(p.astype(BF16), vv) / l).astype(o_ref.dtype)
            lse_ref[...] = m + jnp.log(l)

        @pl.when(i < nctb)
        def _():
            run(k_ref[0:lc, :], v_ref[0:lc, :])

        @pl.when(i >= nctb)
        def _():
            run(k_ref[...], v_ref[...])

    return pl.pallas_call(
        body, name=name, grid=(nb, hq, t // TQ),
        in_specs=[pl.BlockSpec((None, None, TQ, dk), lambda b, h, i: (b, h, i, 0)),
                  pl.BlockSpec((None, None, t, dk), lambda b, h, i: (b, h // grp, 0, 0)),
                  pl.BlockSpec((None, None, t, dv), lambda b, h, i: (b, h // grp, 0, 0))],
        out_specs=[pl.BlockSpec((None, None, TQ, dv), lambda b, h, i: (b, h, i, 0)),
                   pl.BlockSpec((None, None, TQ, 1), lambda b, h, i: (b, h, i, 0))],
        out_shape=[jax.ShapeDtypeStruct((nb, hq, t, dv), BF16), jax.ShapeDtypeStruct((nb, hq, t, 1), F32)],
        compiler_params=_cp("parallel", "parallel", "arbitrary"),
    )(q, k, v)


def att_bwd(q, k, v, lse, do, *, scale, lc, name):
    nb, hq, t, dk = q.shape
    hkv, dv = k.shape[1], v.shape[-1]
    grp = hq // hkv
    nctb = lc // TQ

    def body(q_ref, k_ref, v_ref, lse_ref, do_ref, dq_ref, dk_ref, dv_ref):
        g, i = pl.program_id(2), pl.program_id(3)

        @pl.when((g == 0) & (i == 0))
        def _():
            dk_ref[...] = jnp.zeros_like(dk_ref)
            dv_ref[...] = jnp.zeros_like(dv_ref)

        def run(rows):
            qq, dd = q_ref[...], do_ref[...]
            kk, vv = k_ref[rows, :], v_ref[rows, :]
            p = jnp.exp(_dot(qq, kk, _NT) * scale - lse_ref[...])
            dp = _dot(dd, vv, _NT)
            delta = jnp.sum(p * dp, axis=-1, keepdims=True)
            ds = (p * (dp - delta) * scale).astype(BF16)
            dq_ref[...] = _dot(ds, kk)
            dk_ref[rows, :] += _dot(ds, qq, _TN)
            dv_ref[rows, :] += _dot(p.astype(BF16), dd, _TN)

        @pl.when(i < nctb)
        def _():
            run(pl.ds(0, lc))

        @pl.when(i >= nctb)
        def _():
            run(pl.ds(0, t))

    qmap = lambda b, hk, g, i: (b, hk * grp + g, i, 0)
    kmap = lambda b, hk, g, i: (b, hk, 0, 0)
    return pl.pallas_call(
        body, name=name, grid=(nb, hkv, grp, t // TQ),
        in_specs=[pl.BlockSpec((None, None, TQ, dk), qmap), pl.BlockSpec((None, None, t, dk), kmap),
                  pl.BlockSpec((None, None, t, dv), kmap), pl.BlockSpec((None, None, TQ, 1), qmap),
                  pl.BlockSpec((None, None, TQ, dv), qmap)],
        out_specs=[pl.BlockSpec((None, None, TQ, dk), qmap), pl.BlockSpec((None, None, t, dk), kmap),
                   pl.BlockSpec((None, None, t, dv), kmap)],
        out_shape=[jax.ShapeDtypeStruct((nb, hq, t, dk), F32), jax.ShapeDtypeStruct((nb, hkv, t, dk), F32),
                   jax.ShapeDtypeStruct((nb, hkv, t, dv), F32)],
        compiler_params=_cp("arbitrary", "arbitrary", "arbitrary", "arbitrary"),
    )(q, k, v, lse, do)


def _na_window(st, nc, rows):
    r = jnp.maximum(st - nc, 0)
    r0 = jnp.clip(r - NA_ROWS // 2, 0, rows - NA_ROWS)
    return r, r0, r - r0


def na_fwd(q, k, v, bias, *, lc, name):
    nb, nh, t, d = q.shape
    nc, rows = lc // GRID_W, (t - lc) // GRID_W
    nwin = NA_ROWS * GRID_W
    scale = d ** -0.5

    def body(q_ref, k_ref, v_ref, bias_ref, o_ref, lse_ref):
        st = pl.program_id(2)
        qq = q_ref[...]
        kc, vc = k_ref[0:lc, :], v_ref[0:lc, :]
        s_ctx = _dot(qq, kc, _NT) * scale

        @pl.when(st < nc)
        def _():
            m = jnp.max(s_ctx, axis=-1, keepdims=True)
            p = jnp.exp(s_ctx - m)
            l = jnp.sum(p, axis=-1, keepdims=True)
            o_ref[...] = (_dot(p.astype(BF16), vc) / l).astype(o_ref.dtype)
            lse_ref[...] = m + jnp.log(l)

        @pl.when(st >= nc)
        def _():
            _, r0, _ = _na_window(st, nc, rows)
            win = pl.ds(pl.multiple_of(lc + r0 * GRID_W, GRID_W), nwin)
            kw, vw = k_ref[win, :], v_ref[win, :]
            s_loc = _dot(qq, kw, _NT) * scale + bias_ref[...]
            m = jnp.maximum(jnp.max(s_loc, axis=-1, keepdims=True), jnp.max(s_ctx, axis=-1, keepdims=True))
            p_loc, p_ctx = jnp.exp(s_loc - m), jnp.exp(s_ctx - m)
            l = jnp.sum(p_loc, axis=-1, keepdims=True) + jnp.sum(p_ctx, axis=-1, keepdims=True)
            o_ref[...] = ((_dot(p_loc.astype(BF16), vw) + _dot(p_ctx.astype(BF16), vc)) / l).astype(o_ref.dtype)
            lse_ref[...] = m + jnp.log(l)

    qmap = lambda h, b, st: (b, h, st, 0)
    kmap = lambda h, b, st: (b, h, 0, 0)
    return pl.pallas_call(
        body, name=name, grid=(nh, nb, nc + rows),
        in_specs=[pl.BlockSpec((None, None, GRID_W, d), qmap), pl.BlockSpec((None, None, t, d), kmap),
                  pl.BlockSpec((None, None, t, d), kmap),
                  pl.BlockSpec((None, None, GRID_W, nwin), lambda h, b, st: (h, _na_window(st, nc, rows)[2], 0, 0))],
        out_specs=[pl.BlockSpec((None, None, GRID_W, d), qmap), pl.BlockSpec((None, None, GRID_W, 1), qmap)],
        out_shape=[jax.ShapeDtypeStruct((nb, nh, t, d), BF16), jax.ShapeDtypeStruct((nb, nh, t, 1), F32)],
        compiler_params=_cp("parallel", "parallel", "arbitrary"),
    )(q, k, v, bias)


def na_bwd(q, k, v, bias, lse, do, *, lc, name):
    nb, nh, t, d = q.shape
    nc, rows = lc // GRID_W, (t - lc) // GRID_W
    nwin = NA_ROWS * GRID_W
    scale = d ** -0.5

    def body(q_ref, k_ref, v_ref, bias_ref, lse_ref, do_ref, dq_ref, dk_ref, dv_ref, db_ref):
        b, st = pl.program_id(1), pl.program_id(2)

        @pl.when(st == 0)
        def _():
            dk_ref[...] = jnp.zeros_like(dk_ref)
            dv_ref[...] = jnp.zeros_like(dv_ref)

        @pl.when((st == 0) & (b == 0))
        def _():
            db_ref[...] = jnp.zeros_like(db_ref)

        qq, dd, lse_v = q_ref[...], do_ref[...], lse_ref[...]
        ctx = pl.ds(0, lc)
        kc, vc = k_ref[ctx, :], v_ref[ctx, :]
        p_ctx = jnp.exp(_dot(qq, kc, _NT) * scale - lse_v)
        dp_ctx = _dot(dd, vc, _NT)
        dsum_ctx = jnp.sum(p_ctx * dp_ctx, axis=-1, keepdims=True)

        @pl.when(st < nc)
        def _():
            ds = (p_ctx * (dp_ctx - dsum_ctx) * scale).astype(BF16)
            dq_ref[...] = _dot(ds, kc)
            dk_ref[ctx, :] += _dot(ds, qq, _TN)
            dv_ref[ctx, :] += _dot(p_ctx.astype(BF16), dd, _TN)

        @pl.when(st >= nc)
        def _():
            _, r0, case = _na_window(st, nc, rows)
            win = pl.ds(pl.multiple_of(lc + r0 * GRID_W, GRID_W), nwin)
            kw, vw = k_ref[win, :], v_ref[win, :]
            p_loc = jnp.exp(_dot(qq, kw, _NT) * scale + bias_ref[...] - lse_v)
            dp_loc = _dot(dd, vw, _NT)
            delta = dsum_ctx + jnp.sum(p_loc * dp_loc, axis=-1, keepdims=True)
            ds_loc = p_loc * (dp_loc - delta)
            db_ref[case] += ds_loc
            ds_loc = (ds_loc * scale).astype(BF16)
            ds_ctx = (p_ctx * (dp_ctx - delta) * scale).astype(BF16)
            dq_ref[...] = _dot(ds_loc, kw) + _dot(ds_ctx, kc)
            dk_ref[win, :] += _dot(ds_loc, qq, _TN)
            dk_ref[ctx, :] += _dot(ds_ctx, qq, _TN)
            dv_ref[win, :] += _dot(p_loc.astype(BF16), dd, _TN)
            dv_ref[ctx, :] += _dot(p_ctx.astype(BF16), dd, _TN)

    qmap = lambda h, b, st: (b, h, st, 0)
    kmap = lambda h, b, st: (b, h, 0, 0)
    return pl.pallas_call(
        body, name=name, grid=(nh, nb, nc + rows),
        in_specs=[pl.BlockSpec((None, None, GRID_W, d), qmap), pl.BlockSpec((None, None, t, d), kmap),
                  pl.BlockSpec((None, None, t, d), kmap),
                  pl.BlockSpec((None, None, GRID_W, nwin), lambda h, b, st: (h, _na_window(st, nc, rows)[2], 0, 0)),
                  pl.BlockSpec((None, None, GRID_W, 1), qmap), pl.BlockSpec((None, None, GRID_W, d), qmap)],
        out_specs=[pl.BlockSpec((None, None, GRID_W, d), qmap), pl.BlockSpec((None, None, t, d), kmap),
                   pl.BlockSpec((None, None, t, d), kmap),
                   pl.BlockSpec((None, NA_ROWS, GRID_W, nwin), lambda h, b, st: (h, 0, 0, 0))],
        out_shape=[jax.ShapeDtypeStruct((nb, nh, t, d), F32), jax.ShapeDtypeStruct((nb, nh, t, d), F32),
                   jax.ShapeDtypeStruct((nb, nh, t, d), F32), jax.ShapeDtypeStruct((nh, NA_ROWS, GRID_W, nwin), F32)],
        compiler_params=_cp("arbitrary", "arbitrary", "arbitrary"),
    )(q, k, v, bias, lse, do)


def _na_tables():
    cols = np.arange(GRID_W)
    c0 = np.clip(cols - NA_COLS // 2, 0, GRID_W - NA_COLS)
    col_in = (cols[None, :] >= c0[:, None]) & (cols[None, :] < c0[:, None] + NA_COLS)
    dc = np.clip(cols[None, :] - cols[:, None] + NA_COLS - 1, 0, 2 * NA_COLS - 2)
    dr = np.arange(NA_ROWS)[None, :] + (NA_ROWS - 1) - np.arange(NA_ROWS)[:, None]
    return col_in, dc, dr


def na_expand_bias(rel_bias):
    col_in, dc, dr = _na_tables()
    g = rel_bias[:, dr[:, None, :, None], dc[None, :, None, :]]
    g = jnp.where(col_in[None, None, :, None, :], g, NEG_BIG)
    return g.reshape(rel_bias.shape[0], NA_ROWS, GRID_W, NA_ROWS * GRID_W)


def na_reduce_bias(dexp, name):
    col_in, dc, dr = _na_tables()
    nh = dexp.shape[0]
    e1 = np.zeros((GRID_W, GRID_W, LANE), np.float32)
    qi, ki = np.nonzero(col_in)
    e1[qi, ki, dc[qi, ki]] = 1.0
    e2 = np.zeros((2 * NA_ROWS, NA_ROWS, NA_ROWS), np.float32)
    ci, ji = np.meshgrid(np.arange(NA_ROWS), np.arange(NA_ROWS), indexing='ij')
    e2[dr[ci, ji], ci, ji] = 1.0
    x = dexp.reshape(nh, NA_ROWS, GRID_W, NA_ROWS, GRID_W).transpose(0, 1, 3, 2, 4).reshape(nh * NA_ROWS * NA_ROWS, GRID_W * GRID_W)
    y = mm(x, jnp.asarray(e1.reshape(GRID_W * GRID_W, LANE)), name=name + "_cols", precise=True)
    y = y.reshape(nh, NA_ROWS * NA_ROWS, LANE).transpose(1, 0, 2).reshape(NA_ROWS * NA_ROWS, nh * LANE)
    z = mm(jnp.asarray(e2.reshape(2 * NA_ROWS, NA_ROWS * NA_ROWS)), y, name=name + "_rows", precise=True)
    return z.reshape(2 * NA_ROWS, nh, LANE).transpose(1, 0, 2)[:, :2 * NA_ROWS - 1, :2 * NA_COLS - 1]


def _rot_matrix(width, d_rot):
    f = d_rot // 4
    r = np.zeros((width, width), np.float32)
    for base in range(0, width, d_rot // 2):
        for j in range(f):
            r[base + f + j, base + j] = -1.0
            r[base + j, base + f + j] = 1.0
    return r


def _rope_tables(s_len, lc, d_rot, reps):
    half = d_rot // 2
    freqs = ROPE_THETA ** (-jnp.arange(0, half, 2, dtype=F32) / half)
    tpos = jnp.arange(s_len)
    row = (tpos // GRID_W).astype(F32)[:, None] * freqs
    col = (tpos % GRID_W).astype(F32)[:, None] * freqs
    ang = jnp.concatenate([row, row, col, col], axis=-1)
    cos = jnp.concatenate([jnp.ones((lc, d_rot), F32), jnp.cos(ang)], axis=0)
    sin = jnp.concatenate([jnp.zeros((lc, d_rot), F32), jnp.sin(ang)], axis=0)
    return jnp.tile(cos, (1, reps)), jnp.tile(sin, (1, reps))


def _post_consts():
    s_b = np.kron(np.eye(GQA_HEADS, dtype=np.float32), np.full((HEAD_DIM, HEAD_DIM), 1.0 / HEAD_DIM, np.float32))
    t_b = np.tile(np.eye(HEAD_DIM, dtype=np.float32), (1, GQA_HEADS))
    r_b = _rot_matrix(GQ_W, HEAD_DIM)
    r_m = _rot_matrix(LANE, MLA_ROPE)
    rep = np.zeros((LANE, LANE), np.float32)
    for h in range(MLA_HEADS):
        rep[np.arange(MLA_ROPE), h * MLA_ROPE + np.arange(MLA_ROPE)] = 1.0
    return tuple(jnp.asarray(a) for a in (s_b, r_b, t_b, r_m, rep))


def _mla_perms():
    uq = [h * (MLA_NOPE + MLA_ROPE) + j for h in range(MLA_HEADS) for j in range(MLA_NOPE)] + \
         [h * (MLA_NOPE + MLA_ROPE) + MLA_NOPE + j for h in range(MLA_HEADS) for j in range(MLA_ROPE)]
    ukv = [h * (MLA_NOPE + MLA_V) + j for h in range(MLA_HEADS) for j in range(MLA_NOPE)] + \
          [h * (MLA_NOPE + MLA_V) + MLA_NOPE + j for h in range(MLA_HEADS) for j in range(MLA_V)]
    return np.array(uq), np.array(ukv)


def _place():
    return lax.axis_index("x"), lax.axis_index("y"), lax.axis_index("c")


def all_gather(v, *, name, with_c):
    flips = [(dx, dy, dc) for dx in (0, 1) for dy in (0, 1) for dc in ((0, 1) if with_c else (0,))][1:]
    n = len(flips) + 1

    def body(v_ref, out_ref, send_sems, recv_sems, local_sem):
        mx, my, mc = _place()

        def slot(px, py, pc):
            return 4 * px + 2 * py + pc if with_c else 2 * px + py

        mine = pltpu.make_async_copy(v_ref, out_ref.at[slot(mx, my, mc)], local_sem)
        mine.start()
        sends = []
        for j, (dx, dy, dc) in enumerate(flips):
            peer = (mx ^ dx, my ^ dy, mc ^ dc)
            cp = pltpu.make_async_remote_copy(src_ref=v_ref, dst_ref=out_ref.at[slot(mx, my, mc)], send_sem=send_sems.at[j],
                                              recv_sem=recv_sems.at[j], device_id=peer, device_id_type=MESH)
            cp.start()
            sends.append(cp)
        for j, (dx, dy, dc) in enumerate(flips):
            peer = (mx ^ dx, my ^ dy, mc ^ dc)
            pltpu.make_async_remote_copy(src_ref=v_ref, dst_ref=out_ref.at[slot(*peer)], send_sem=send_sems.at[j],
                                         recv_sem=recv_sems.at[j], device_id=peer, device_id_type=MESH).wait_recv()
        for cp in sends:
            cp.wait_send()
        mine.wait()

    return pl.pallas_call(
        body, name=name, in_specs=[ANY], out_specs=ANY, out_shape=jax.ShapeDtypeStruct((n,) + v.shape, v.dtype),
        scratch_shapes=[pltpu.SemaphoreType.DMA((n - 1,)), pltpu.SemaphoreType.DMA((n - 1,)), pltpu.SemaphoreType.DMA(())],
    )(v)


def pair_exchange_halves(g, *, name):
    n, _, h, w = g.shape

    def body(g_ref, out_ref, send_sems, recv_sems):
        mx, my, mc = _place()
        sib = (mx, my, 1 - mc)
        cps = [pltpu.make_async_remote_copy(src_ref=g_ref.at[s, 1 - mc], dst_ref=out_ref.at[s], send_sem=send_sems.at[s],
                                            recv_sem=recv_sems.at[s], device_id=sib, device_id_type=MESH) for s in range(n)]
        for cp in cps:
            cp.start()
        for cp in cps:
            cp.wait_recv()
        for cp in cps:
            cp.wait_send()

    return pl.pallas_call(
        body, name=name, in_specs=[ANY], out_specs=ANY, out_shape=jax.ShapeDtypeStruct((n, h, w), g.dtype),
        scratch_shapes=[pltpu.SemaphoreType.DMA((n,)), pltpu.SemaphoreType.DMA((n,))],
    )(g)


def all_to_all_xy(v, *, name):
    def body(v_ref, out_ref, send_sems, recv_sems, local_sem):
        mx, my, mc = _place()
        me = 2 * mx + my
        mine = pltpu.make_async_copy(v_ref.at[me], out_ref.at[me], local_sem)
        mine.start()
        flips = [(1, 0), (0, 1), (1, 1)]
        sends = []
        for j, (dx, dy) in enumerate(flips):
            px, py = mx ^ dx, my ^ dy
            cp = pltpu.make_async_remote_copy(src_ref=v_ref.at[2 * px + py], dst_ref=out_ref.at[me], send_sem=send_sems.at[j],
                                              recv_sem=recv_sems.at[j], device_id=(px, py, mc), device_id_type=MESH)
            cp.start()
            sends.append(cp)
        for j, (dx, dy) in enumerate(flips):
            px, py = mx ^ dx, my ^ dy
            pltpu.make_async_remote_copy(src_ref=v_ref.at[me], dst_ref=out_ref.at[2 * px + py], send_sem=send_sems.at[j],
                                         recv_sem=recv_sems.at[j], device_id=(px, py, mc), device_id_type=MESH).wait_recv()
        for cp in sends:
            cp.wait_send()
        mine.wait()

    return pl.pallas_call(
        body, name=name, in_specs=[ANY], out_specs=ANY, out_shape=jax.ShapeDtypeStruct(v.shape, v.dtype),
        scratch_shapes=[pltpu.SemaphoreType.DMA((3,)), pltpu.SemaphoreType.DMA((3,)), pltpu.SemaphoreType.DMA(())],
    )(v)


def pair_all_gather(v, *, name):
    def body(v_ref, out_ref, send_sem, recv_sem, local_sem):
        mx, my, mc = _place()
        mine = pltpu.make_async_copy(v_ref, out_ref.at[mc], local_sem)
        mine.start()
        cp = pltpu.make_async_remote_copy(src_ref=v_ref, dst_ref=out_ref.at[mc], send_sem=send_sem, recv_sem=recv_sem,
                                          device_id=(mx, my, 1 - mc), device_id_type=MESH)
        cp.start()
        pltpu.make_async_remote_copy(src_ref=v_ref, dst_ref=out_ref.at[1 - mc], send_sem=send_sem, recv_sem=recv_sem,
                                     device_id=(mx, my, 1 - mc), device_id_type=MESH).wait_recv()
        cp.wait_send()
        mine.wait()

    return pl.pallas_call(
        body, name=name, in_specs=[ANY], out_specs=ANY, out_shape=jax.ShapeDtypeStruct((2,) + v.shape, v.dtype),
        scratch_shapes=[pltpu.SemaphoreType.DMA(()), pltpu.SemaphoreType.DMA(()), pltpu.SemaphoreType.DMA(())],
    )(v)


def add_kept_half(g, r, c_idx, *, name):
    n, _, h, w = g.shape
    th = _row_tile(h)

    def body(c_ref, g_ref, r_ref, o_ref):
        o_ref[...] = g_ref[...] + r_ref[...]

    return pl.pallas_call(
        body, name=name,
        grid_spec=pltpu.PrefetchScalarGridSpec(
            num_scalar_prefetch=1, grid=(n, h // th),
            in_specs=[pl.BlockSpec((None, None, th, w), lambda s, i, c_ref: (s, c_ref[0], i, 0)),
                      pl.BlockSpec((None, th, w), lambda s, i, c_ref: (s, i, 0))],
            out_specs=pl.BlockSpec((None, th, w), lambda s, i, c_ref: (s, i, 0))),
        out_shape=jax.ShapeDtypeStruct((n, h, w), F32), compiler_params=_cp("parallel", "parallel"),
    )(c_idx, g, r)


def sum_slots(v, *, name):
    n, rows, w = v.shape
    tr = _row_tile(rows, 256)

    def body(v_ref, o_ref):
        acc = v_ref[0]
        for s in range(1, n):
            acc = acc + v_ref[s]
        o_ref[...] = acc

    return pl.pallas_call(body, name=name, grid=(rows // tr,), in_specs=[pl.BlockSpec((n, tr, w), lambda i: (0, i, 0))],
                          out_specs=pl.BlockSpec((tr, w), lambda i: (i, 0)), out_shape=jax.ShapeDtypeStruct((rows, w), F32),
                          compiler_params=_cp("parallel"))(v)


def ada_fwd(c_rows, w_ada, b_shard, *, name):
    nl, d, ncol = w_ada.shape
    rows = c_rows.shape[0]
    tn = _tile(ncol, (768, 512, 256, 128))

    def body(c_ref, w_ref, b_ref, o_ref):
        o_ref[...] = jnp.dot(jax.nn.silu(c_ref[...]), w_ref[...], precision=HI, preferred_element_type=F32) + b_ref[...]

    return pl.pallas_call(
        body, name=name, grid=(nl, ncol // tn),
        in_specs=[pl.BlockSpec((rows, d), lambda l, j: (0, 0)), pl.BlockSpec((None, d, tn), lambda l, j: (l, 0, j)),
                  pl.BlockSpec((None, 1, tn), lambda l, j: (l, 0, j))],
        out_specs=pl.BlockSpec((None, rows, tn), lambda l, j: (l, 0, j)),
        out_shape=jax.ShapeDtypeStruct((nl, rows, ncol), F32), compiler_params=_cp("parallel", "parallel"),
    )(c_rows, w_ada, b_shard)


def ada_bwd(c_rows, w_ada, dm_shard, dm_full, n_ex, *, name):
    nl, d, ncol = w_ada.shape
    rows = c_rows.shape[0]
    tn = _tile(ncol, (768, 512, 256, 128))
    nj = ncol // tn

    def body(c_ref, w_ref, dm_ref, dmf_ref, gw_ref, gb_ref, dc_ref, dact_ref):
        l, j = pl.program_id(0), pl.program_id(1)
        act, act_vjp = jax.vjp(jax.nn.silu, c_ref[...])
        gw_ref[...] = lax.dot_general(act, dm_ref[...], _TN, precision=HI, preferred_element_type=F32)
        gb_ref[...] = jnp.sum(dmf_ref[...], axis=0, keepdims=True)
        part = lax.dot_general(dm_ref[...], w_ref[...], _NT, precision=HI, preferred_element_type=F32)

        @pl.when((l == 0) & (j == 0))
        def _():
            dact_ref[...] = part

        @pl.when((l > 0) | (j > 0))
        def _():
            dact_ref[...] += part

        @pl.when((l == nl - 1) & (j == nj - 1))
        def _():
            dc, = act_vjp(dact_ref[...])
            dc_ref[...] = jnp.sum(dc[n_ex:, :], axis=0, keepdims=True)

    return pl.pallas_call(
        body, name=name, grid=(nl, nj),
        in_specs=[pl.BlockSpec((rows, d), lambda l, j: (0, 0)), pl.BlockSpec((None, d, tn), lambda l, j: (l, 0, j)),
                  pl.BlockSpec((None, rows, tn), lambda l, j: (l, 0, j)),
                  pl.BlockSpec((None, rows, dm_full.shape[-1]), lambda l, j: (l, 0, 0))],
        out_specs=[pl.BlockSpec((None, d, tn), lambda l, j: (l, 0, j)),
                   pl.BlockSpec((None, 1, dm_full.shape[-1]), lambda l, j: (l, 0, 0)),
                   pl.BlockSpec((1, d), lambda l, j: (0, 0))],
        out_shape=[jax.ShapeDtypeStruct((nl, d, ncol), F32), jax.ShapeDtypeStruct((nl, 1, dm_full.shape[-1]), F32),
                   jax.ShapeDtypeStruct((1, d), F32)],
        scratch_shapes=[pltpu.VMEM((rows, d), F32)], compiler_params=_cp("arbitrary", "arbitrary"),
    )(c_rows, w_ada, dm_shard, dm_full)


def adamw(w, g, m, v, *, name):
    shape = w.shape
    cols = shape[-1]
    rows = int(np.prod(shape[:-1])) if len(shape) > 1 else 1
    tr = _row_tile(rows, 256)

    def body(w_ref, g_ref, m_ref, v_ref, d_ref, nm_ref, nv_ref):
        gg = g_ref[...]
        nm = ADAM_B1 * m_ref[...] + (1.0 - ADAM_B1) * gg
        nv = ADAM_B2 * v_ref[...] + (1.0 - ADAM_B2) * jnp.square(gg)
        m_hat = nm / (1.0 - ADAM_B1 ** ADAM_STEP)
        v_hat = nv / (1.0 - ADAM_B2 ** ADAM_STEP)
        d_ref[...] = -ADAM_LR * (m_hat / (jnp.sqrt(v_hat) + ADAM_EPS) + ADAM_WD * w_ref[...])
        nm_ref[...] = nm
        nv_ref[...] = nv

    spec = pl.BlockSpec((tr, cols), lambda i: (i, 0))
    out = pl.pallas_call(body, name=name, grid=(rows // tr,), in_specs=[spec] * 4, out_specs=[spec] * 3,
                         out_shape=[jax.ShapeDtypeStruct((rows, cols), F32)] * 3, compiler_params=_cp("parallel"),
                         )(*[a.reshape(rows, cols) for a in (w, g, m, v)])
    return tuple(o.reshape(shape) for o in out)


def _heads(a, nh):
    nb, t, w = a.shape
    return a.reshape(nb, t, nh, w // nh).transpose(0, 2, 1, 3)


def _unheads(a):
    nb, nh, t, d = a.shape
    return a.transpose(0, 2, 1, 3).reshape(nb, t, nh * d)


def local_step(h0, target, mods, lw, small, *, lc):
    nb, t, d = h0.shape
    nt, nct = t // TM, lc // TM
    s_len = t - lc
    nl = len(lw)
    f2 = lw[0]['w_gu1'].shape[1]
    consts = _post_consts()
    cos_b, sin_b = _rope_tables(s_len, lc, HEAD_DIM, GQA_HEADS)
    cos_m, sin_m = _rope_tables(s_len, lc, MLA_ROPE, MLA_HEADS)
    na_scale = HEAD_DIM ** -0.5
    mla_scale = (MLA_NOPE + MLA_ROPE) ** -0.5
    rc = functools.partial(rowcall, nb=nb, nt=nt, nct=nct)
    flat = lambda a: a.reshape(nb * t, a.shape[-1])
    unflat = lambda a: a.reshape(nb, t, a.shape[-1])
    vec = lambda a: a.reshape(1, -1)

    def ffn_fwd(h, g, mod3, w, tag):
        shift, scale, gate = mod3
        n, = rc(tag + "_norm", lambda _, *a: (f_normmod(*a),), [(h, 'tok'), (vec(g), 'full'), (shift, 'mod'), (scale, 'mod')],
                [('tok', d, BF16)])
        gu = unflat(mm(flat(n), w['w_gu'], name=tag + "_up"))
        act, = rc(tag + "_act", lambda _, a: (f_act(a),), [(gu, 'tok')], [('tok', f2 // 2, BF16)])
        y = unflat(mm(flat(act), w['w_d'], name=tag + "_down"))
        h2, = rc(tag + "_res", lambda _, hh, yy, gt: (hh + 0.5 * gt * yy,), [(h, 'tok'), (y, 'tok'), (gate, 'mod')], [('tok', d, F32)])
        return h2, (h, n, gu, act, y)

    def ffn_bwd(dh2, saved, g, mod3, w, tag):
        shift, scale, gate = mod3
        h, n, gu, act, y = saved
        dy, dgate = rc(tag + "_res_bwd", lambda _, dd, yy, gt: (0.5 * gt * dd, jnp.sum(0.5 * yy * dd, axis=0, keepdims=True)),
                       [(dh2, 'tok'), (y, 'tok'), (gate, 'mod')], [('tok', d, BF16), ('mod', d)])
        dw_d = mm(flat(act), flat(dy), ta=True, name=tag + "_down_dw")
        dact = unflat(mm(flat(dy), w['w_d'], tb=True, name=tag + "_down_dx"))

        def act_bwd(_, a, da):
            return jax.vjp(f_act, a)[1](da)

        dgu, = rc(tag + "_act_bwd", act_bwd, [(gu, 'tok'), (dact, 'tok')], [('tok', f2, BF16)])
        dw_gu = mm(flat(n), flat(dgu), ta=True, name=tag + "_up_dw")
        dn = unflat(mm(flat(dgu), w['w_gu'], tb=True, name=tag + "_up_dx"))

        def norm_bwd(_, hh, gg, sh, sc, dnn, dres):
            dh, dg, dsh, dsc = jax.vjp(f_normmod, hh, gg, sh, sc)[1](dnn)
            return dh + dres, dg, dsh, dsc

        dh, dg, dshift, dscale = rc(tag + "_norm_bwd", norm_bwd,
                                    [(h, 'tok'), (vec(g), 'full'), (shift, 'mod'), (scale, 'mod'), (dn, 'tok'), (dh2, 'tok')],
                                    [('tok', d, F32), ('full', (1, d)), ('mod', d), ('mod', d)])
        return dh, dg.reshape(d), (dshift, dscale, dgate), dw_gu, dw_d

    def post_ins(p, sm, w):
        return [(p, ('tokc', MAIN_PAD, 0)), (cos_b, 'pos'), (sin_b, 'pos'), (cos_m, 'pos'), (sin_m, 'pos'),
                (vec(sm['gqa_q_norm']), 'full'), (vec(sm['gqa_k_norm']), 'full'), (vec(sm['mla_q_norm']), 'full'),
                (vec(sm['mla_kv_norm']), 'full'), (w['w_uq'], 'full'), (w['w_ukv'], 'full')] + [(c, 'full') for c in consts]

    def split_heads(parts):
        a_q, a_k, a_v, b_q, b_k, b_v, mq_n, mq_r, mk_n, mk_r, m_v = parts
        cat = lambda nope, rope: _heads(jnp.concatenate(
            [nope.reshape(nb, t, MLA_HEADS, MLA_NOPE), rope.reshape(nb, t, MLA_HEADS, MLA_ROPE)], axis=-1).reshape(nb, t, -1), MLA_HEADS)
        return ((_heads(a_q, NA_HEADS), _heads(a_k, NA_HEADS), _heads(a_v, NA_HEADS)),
                (_heads(b_q, GQA_HEADS), _heads(b_k, GQA_KV_HEADS), _heads(b_v, GQA_KV_HEADS)),
                (cat(mq_n, mq_r), cat(mk_n, mk_r), _heads(m_v, MLA_HEADS)))

    def mix_fwd(h, sm, mod3, w, tag):
        shift, scale, gate = mod3
        n, = rc(tag + "_norm", lambda _, *a: (f_normmod(*a),), [(h, 'tok'), (vec(sm['mix_norm']), 'full'), (shift, 'mod'), (scale, 'mod')],
                [('tok', d, BF16)])
        p = unflat(mm(flat(n), w['w_in'], name=tag + "_in"))
        parts = rc(tag + "_post", lambda _, *a: f_post(*a), post_ins(p, sm, w), [('tok', wd, BF16) for wd in POST_WIDTHS])
        (aq, ak, av), (bq, bk, bv), (mq, mk, mv) = split_heads(parts)
        bias = na_expand_bias(sm['na_rel_bias'])
        o_a, lse_a = na_fwd(aq, ak, av, bias, lc=lc, name=tag + "_na")
        o_b, lse_b = att_fwd(bq, bk, bv, scale=na_scale, lc=lc, name=tag + "_gqa")
        o_m, lse_m = att_fwd(mq, mk, mv, scale=mla_scale, lc=lc, name=tag + "_mla")
        fo = [_unheads(o) for o in (o_a, o_b, o_m)]
        ys = [unflat(mm(flat(o), w[k], name=tag + "_br" + k[-1])) for o, k in zip(fo, ('w_a', 'w_b', 'w_c'))]
        gcols = [(p, ('tokc', d, MAIN_PAD // d + j)) for j in range(3)]
        y, = rc(tag + "_merge", lambda _, *a: (f_merge(*a),), gcols + [(v, 'tok') for v in ys], [('tok', d, BF16)])
        z = unflat(mm(flat(y), w['w_o'], name=tag + "_out"))
        h2, = rc(tag + "_res", lambda _, hh, zz, gt: (hh + gt * zz,), [(h, 'tok'), (z, 'tok'), (gate, 'mod')], [('tok', d, F32)])
        saved = (h, n, p, (aq, ak, av, lse_a, bias), (bq, bk, bv, lse_b), (mq, mk, mv, lse_m), fo, ys, y, z)
        return h2, saved

    def mix_bwd(dh2, saved, sm, mod3, w, tag):
        shift, scale, gate = mod3
        h, n, p, (aq, ak, av, lse_a, bias), (bq, bk, bv, lse_b), (mq, mk, mv, lse_m), fo, ys, y, z = saved
        dz, dgate = rc(tag + "_res_bwd", lambda _, dd, zz, gt: (gt * dd, jnp.sum(zz * dd, axis=0, keepdims=True)),
                       [(dh2, 'tok'), (z, 'tok'), (gate, 'mod')], [('tok', d, BF16), ('mod', d)])
        dw_o = mm(flat(y), flat(dz), ta=True, name=tag + "_out_dw")
        dy = unflat(mm(flat(dz), w['w_o'], tb=True, name=tag + "_out_dx"))
        gcols = [(p, ('tokc', d, MAIN_PAD // d + j)) for j in range(3)]

        def merge_bwd(_, ga, gb, gm, ya, yb, ym, dyy):
            dga, dgb, dgm, dya, dyb, dym = jax.vjp(f_merge, ga, gb, gm, ya, yb, ym)[1](dyy)
            return dya, dyb, dym, jnp.concatenate([dga, dgb, dgm], axis=-1)

        dya, dyb, dym, dgl = rc(tag + "_merge_bwd", merge_bwd, gcols + [(v, 'tok') for v in ys] + [(dy, 'tok')],
                                [('tok', d, BF16)] * 3 + [('tok', 3 * d, BF16)])
        dws, dos = {}, []
        for o, dyk, k in zip(fo, (dya, dyb, dym), ('w_a', 'w_b', 'w_c')):
            dws[k] = mm(flat(o), flat(dyk), ta=True, name=tag + "_br" + k[-1] + "_dw")
            dos.append(unflat(mm(flat(dyk), w[k], tb=True, out_dtype=BF16, name=tag + "_br" + k[-1] + "_dx")))
        do_a, do_b, do_m = _heads(dos[0], NA_HEADS), _heads(dos[1], GQA_HEADS), _heads(dos[2], MLA_HEADS)
        daq, dak, dav, dbias = na_bwd(aq, ak, av, bias, lse_a, do_a, lc=lc, name=tag + "_na_bwd")
        dbq, dbk, dbv = att_bwd(bq, bk, bv, lse_b, do_b, scale=na_scale, lc=lc, name=tag + "_gqa_bwd")
        dmq, dmk, dmv = att_bwd(mq, mk, mv, lse_m, do_m, scale=mla_scale, lc=lc, name=tag + "_mla_bwd")
        d_rel = na_reduce_bias(dbias, tag + "_relb")

        def msplit(a):
            a = a.transpose(0, 2, 1, 3)
            return a[..., :MLA_NOPE].reshape(nb, t, -1), a[..., MLA_NOPE:].reshape(nb, t, -1)

        dmq_n, dmq_r = msplit(dmq)
        dmk_n, dmk_r = msplit(dmk)
        cots = [_unheads(daq), _unheads(dak), _unheads(dav), _unheads(dbq), _unheads(dbk), _unheads(dbv), dmq_n, dmq_r, dmk_n,
                dmk_r, _unheads(dmv)]
        ins = post_ins(p, sm, w)
        n_in = len(ins)

        def post_bwd(_, *a):
            prim, cot, dgl_v = a[:11], a[n_in:n_in + 11], a[-1]
            outs = jax.vjp(lambda pp, qn, kn, mqn, mkvn, wuq, wukv: f_post(pp, *prim[1:5], qn, kn, mqn, mkvn, wuq, wukv, *a[11:n_in]),
                           prim[0], *prim[5:11])[1](tuple(cot))
            return (jnp.concatenate([outs[0].astype(BF16), dgl_v], axis=-1),) + tuple(outs[1:])

        res = rc(tag + "_post_bwd", post_bwd, ins + [(cv, 'tok') for cv in cots] + [(dgl, 'tok')],
                 [('tok', MAIN_PAD + 3 * d, BF16), ('full', (1, HEAD_DIM)), ('full', (1, HEAD_DIM)), ('full', (1, MLA_Q_RANK)),
                  ('full', (1, MLA_KV_RANK)), ('full', w['w_uq'].shape), ('full', w['w_ukv'].shape)])
        dp, dqn, dkn, dmqn, dmkvn, dw_uq, dw_ukv = res
        dw_in = mm(flat(n), flat(dp), ta=True, name=tag + "_in_dw")
        dn = unflat(mm(flat(dp), w['w_in'], tb=True, name=tag + "_in_dx"))

        def norm_bwd(_, hh, gg, sh, sc, dnn, dres):
            dh, dg, dsh, dsc = jax.vjp(f_normmod, hh, gg, sh, sc)[1](dnn)
            return dh + dres, dg, dsh, dsc

        dh, dg, dshift, dscale = rc(tag + "_norm_bwd", norm_bwd,
                                    [(h, 'tok'), (vec(sm['mix_norm']), 'full'), (shift, 'mod'), (scale, 'mod'), (dn, 'tok'), (dh2, 'tok')],
                                    [('tok', d, F32), ('full', (1, d)), ('mod', d), ('mod', d)])
        dsm = {'mix_norm': dg.reshape(d), 'na_rel_bias': d_rel, 'gqa_q_norm': dqn.reshape(-1), 'gqa_k_norm': dkn.reshape(-1),
               'mla_q_norm': dmqn.reshape(-1), 'mla_kv_norm': dmkvn.reshape(-1)}
        dwl = {'w_in': dw_in, 'w_uq': dw_uq, 'w_ukv': dw_ukv, 'w_o': dw_o, **dws}
        return dh, dsm, (dshift, dscale, dgate), dwl

    h = h0
    saved = []
    for l in range(nl):
        sm = {k: small[k][l] for k in SMALL_LAYER}
        h, s1 = ffn_fwd(h, sm['ffn1_norm'], mods[l][0:3], {'w_gu': lw[l]['w_gu1'], 'w_d': lw[l]['w_d1']}, f"l{l}_ffn1")
        h, s2 = mix_fwd(h, sm, mods[l][3:6], lw[l], f"l{l}_mix")
        h, s3 = ffn_fwd(h, sm['ffn2_norm'], mods[l][6:9], {'w_gu': lw[l]['w_gu2'], 'w_d': lw[l]['w_d2']}, f"l{l}_ffn2")
        saved.append((sm, s1, s2, s3))

    def final(is_ctx, hh, gg, tgt):
        def loss_fn(hv, gv):
            return 0.5 * jnp.sum(jnp.mean(jnp.square(_rms(hv, gv) - tgt), axis=-1))

        keep = jnp.where(is_ctx, 0.0, 1.0)
        loss, (dh, dg) = jax.value_and_grad(loss_fn, argnums=(0, 1))(hh, gg)
        return dh * keep, jnp.full((1, LANE), loss * keep, F32), dg * keep

    dh, loss, dg_final = rc("final_loss", final, [(h, 'tok'), (vec(small['final_norm']), 'full'), (target, 'lat')],
                            [('tok', d, F32), ('full', (1, LANE)), ('full', (1, d))])

    dsmall = {k: [None] * nl for k in SMALL_LAYER}
    dmods, dlw = [None] * nl, [None] * nl
    for l in reversed(range(nl)):
        sm, s1, s2, s3 = saved[l]
        dh, dg3, dm3, dw_gu2, dw_d2 = ffn_bwd(dh, s3, sm['ffn2_norm'], mods[l][6:9], {'w_gu': lw[l]['w_gu2'], 'w_d': lw[l]['w_d2']}, f"l{l}_ffn2")
        dh, dsm, dm2, dwl = mix_bwd(dh, s2, sm, mods[l][3:6], lw[l], f"l{l}_mix")
        dh, dg1, dm1, dw_gu1, dw_d1 = ffn_bwd(dh, s1, sm['ffn1_norm'], mods[l][0:3], {'w_gu': lw[l]['w_gu1'], 'w_d': lw[l]['w_d1']}, f"l{l}_ffn1")
        dmods[l] = list(dm1) + list(dm2) + list(dm3)
        dlw[l] = {'w_gu1': dw_gu1, 'w_d1': dw_d1, 'w_gu2': dw_gu2, 'w_d2': dw_d2, **dwl}
        dsm.update(ffn1_norm=dg1, ffn2_norm=dg3)
        for k in SMALL_LAYER:
            dsmall[k][l] = dsm[k]
    dsmall = {k: jnp.stack(v) for k, v in dsmall.items()}
    dsmall['final_norm'] = dg_final.reshape(d)
    return loss, dh, dmods, dlw, dsmall


def _pack(parts, pad_rows):
    flat, where, off = [], [], 0
    for a in parts:
        n = _ceil_to(a.size, PACK_W)
        flat.append(jnp.pad(a.reshape(-1), (0, n - a.size)))
        where.append((off, n // PACK_W))
        off += n // PACK_W
    total = _ceil_to(off, pad_rows)
    if total > off:
        flat.append(jnp.zeros(((total - off) * PACK_W,), flat[0].dtype))
    return jnp.concatenate(flat).reshape(total, PACK_W), where


def _unpack(buf, where, shape):
    off, rows = where
    return buf[off:off + rows].reshape(-1)[:int(np.prod(shape))].reshape(shape)


def layer_weights(full, l):
    perm_uq, perm_ukv = _mla_perms()
    wi = full['w_in'][l]
    d = wi.shape[0]
    return {
        'w_gu1': jnp.concatenate([full['ffn1_w_gate'][l], full['ffn1_w_up'][l]], axis=1), 'w_d1': full['ffn1_w_down'][l],
        'w_gu2': jnp.concatenate([full['ffn2_w_gate'][l], full['ffn2_w_up'][l]], axis=1), 'w_d2': full['ffn2_w_down'][l],
        'w_in': jnp.concatenate([wi[:, :MAIN_W], jnp.zeros((d, MAIN_PAD - MAIN_W), wi.dtype), wi[:, MAIN_W:]], axis=1),
        'w_uq': full['mla_w_uq'][l][:, perm_uq].astype(F32), 'w_ukv': full['mla_w_ukv'][l][:, perm_ukv].astype(F32),
        'w_a': full['w_branch_a'][l], 'w_b': full['w_branch_b'][l], 'w_c': full['w_branch_c'][l], 'w_o': full['w_out'][l]}


def layer_grads_by_name(dlw):
    perm_uq, perm_ukv = _mla_perms()
    inv_uq, inv_ukv = np.argsort(perm_uq), np.argsort(perm_ukv)
    per_name = {k: [] for k, _ in BIG}
    for g in dlw:
        f = g['w_gu1'].shape[1] // 2
        per_name['ffn1_w_gate'].append(g['w_gu1'][:, :f])
        per_name['ffn1_w_up'].append(g['w_gu1'][:, f:])
        per_name['ffn1_w_down'].append(g['w_d1'])
        per_name['ffn2_w_gate'].append(g['w_gu2'][:, :f])
        per_name['ffn2_w_up'].append(g['w_gu2'][:, f:])
        per_name['ffn2_w_down'].append(g['w_d2'])
        per_name['w_in'].append(jnp.concatenate([g['w_in'][:, :MAIN_W], g['w_in'][:, MAIN_PAD:]], axis=1))
        per_name['mla_w_uq'].append(g['w_uq'][:, inv_uq])
        per_name['mla_w_ukv'].append(g['w_ukv'][:, inv_ukv])
        per_name['w_branch_a'].append(g['w_a'])
        per_name['w_branch_b'].append(g['w_b'])
        per_name['w_branch_c'].append(g['w_c'])
        per_name['w_out'].append(g['w_o'])
    return {k: jnp.stack(v) for k, v in per_name.items()}


def kernel(x, c, ctx, c_ctx, w_ada, b_ada, ffn1_norm, ffn1_w_gate, ffn1_w_up, ffn1_w_down, mix_norm, w_in, na_rel_bias, gqa_q_norm, gqa_k_norm, mla_q_norm, mla_kv_norm, mla_w_uq, mla_w_ukv, w_branch_a, w_branch_b, w_branch_c, w_out, ffn2_norm, ffn2_w_gate, ffn2_w_up, ffn2_w_down, final_norm, loss_target, m_c_ctx, m_w_ada, m_b_ada, m_ffn1_norm, m_ffn1_w_gate, m_ffn1_w_up, m_ffn1_w_down, m_mix_norm, m_w_in, m_na_rel_bias, m_gqa_q_norm, m_gqa_k_norm, m_mla_q_norm, m_mla_kv_norm, m_mla_w_uq, m_mla_w_ukv, m_w_branch_a, m_w_branch_b, m_w_branch_c, m_w_out, m_ffn2_norm, m_ffn2_w_gate, m_ffn2_w_up, m_ffn2_w_down, m_final_norm, v_c_ctx, v_w_ada, v_b_ada, v_ffn1_norm, v_ffn1_w_gate, v_ffn1_w_up, v_ffn1_w_down, v_mix_norm, v_w_in, v_na_rel_bias, v_gqa_q_norm, v_gqa_k_norm, v_mla_q_norm, v_mla_kv_norm, v_mla_w_uq, v_mla_w_ukv, v_w_branch_a, v_w_branch_b, v_w_branch_c, v_w_out, v_ffn2_norm, v_ffn2_w_gate, v_ffn2_w_up, v_ffn2_w_down, v_final_norm):
    args = locals()
    wts = {k: args[k] for k in WEIGHTS}
    mom = {k: args['m_' + k] for k in WEIGHTS}
    var = {k: args['v_' + k] for k in WEIGHTS}
    nb, s_len, d = x.shape
    lc = ctx.shape[1]
    nl = w_ada.shape[0]
    nsh, ndev = 4, 8
    mx, my, mc = _place()
    sidx = 2 * mx + my
    didx = 4 * mx + 2 * my + mc
    assert d % LANE == 0 and MAIN_PAD % d == 0 and lc % TQ == 0 and s_len % TQ == 0 and s_len // GRID_W >= NA_ROWS

    wpack, wwhere = _pack([wts[k].astype(BF16) for k, _ in BIG], 16)
    wall = all_gather(wpack, name="gather_weights", with_c=False)
    full = {}
    for (k, ax), wh in zip(BIG, wwhere):
        shp = wts[k].shape
        parts = jnp.stack([_unpack(wall[s], wh, shp) for s in range(nsh)])
        if ax == 1:
            full[k] = parts.transpose(1, 2, 0, 3).reshape(nl, shp[1], nsh * shp[2])
        else:
            full[k] = parts.transpose(1, 0, 2, 3).reshape(nl, nsh * shp[1], shp[2])
    lw = [layer_weights(full, l) for l in range(nl)]

    n_ex = ndev * nb
    ncol = w_ada.shape[-1]
    c_all = all_gather(c, name="gather_cond", with_c=True).reshape(n_ex, d)
    c_rows = jnp.concatenate([c_all, jnp.broadcast_to(c_ctx[None], (n_ex, d))], axis=0)
    b_shard = lax.dynamic_slice_in_dim(b_ada, sidx * ncol, ncol, axis=1)[:, None, :]
    mod_sh = ada_fwd(c_rows, w_ada, b_shard, name="ada_fwd")
    mod_all = all_gather(mod_sh, name="gather_mod", with_c=False)
    mod_all = mod_all.transpose(1, 2, 0, 3).reshape(nl, 2 * n_ex, nsh * ncol)
    mod_x = lax.dynamic_slice_in_dim(mod_all, didx * nb, nb, axis=1)
    mod_c = jnp.broadcast_to(mod_all[:, n_ex:n_ex + 1], mod_x.shape)
    mods = [[jnp.stack([mod_c[l, :, j * d:(j + 1) * d], mod_x[l, :, j * d:(j + 1) * d]], axis=1)[:, :, None, :]
             for j in range(N_MOD)] for l in range(nl)]

    small = {k: wts[k] for k in SMALL_LAYER + ['final_norm']}
    h0 = jnp.concatenate([ctx, x], axis=1)
    loss_part, dh0, dmods, dlw, dsmall = local_step(h0, loss_target, mods, lw, small, lc=lc)
    grad_x = dh0[:, lc:]

    dmod_mine = jnp.stack([jnp.concatenate([m[:, :, 0, :] for m in dmods[l]], axis=-1) for l in range(nl)])
    small_names = SMALL_LAYER + ['final_norm']
    spack, swhere = _pack([loss_part] + [dsmall[k] for k in small_names] + [dmod_mine], 8)
    sall = all_gather(spack, name="gather_small", with_c=True)
    ssum = sum_slots(sall, name="sum_small")
    loss = _unpack(ssum, swhere[0], (1, LANE))[0, 0]
    grads = {k: _unpack(ssum, wh, wts[k].shape) for k, wh in zip(small_names, swhere[1:])}
    off, rows = swhere[-1]
    dm_all = sall[:, off:off + rows].reshape(ndev, -1)[:, :dmod_mine.size].reshape((ndev,) + dmod_mine.shape)
    dm_all = dm_all.transpose(1, 3, 0, 2, 4).reshape(nl, 2, n_ex, N_MOD * d)
    dm_rows = jnp.concatenate([dm_all[:, 1], dm_all[:, 0]], axis=1)
    dm_shard = lax.dynamic_slice_in_dim(dm_rows, sidx * ncol, ncol, axis=2)
    grads['w_ada'], gb, dc_part = ada_bwd(c_rows, w_ada, dm_shard, dm_rows, n_ex, name="ada_bwd")
    grads['b_ada'] = gb.reshape(b_ada.shape)
    dc_all = all_gather(jnp.pad(dc_part, ((0, 7), (0, 0))), name="gather_dcond", with_c=False)
    grads['c_ctx'] = sum_slots(dc_all, name="sum_dcond")[0]

    per_name = layer_grads_by_name(dlw)
    shard_major = []
    for k, ax in BIG:
        g = per_name[k]
        shp = wts[k].shape
        if ax == 1:
            shard_major.append(g.reshape(nl, shp[1], nsh, shp[2]).transpose(2, 0, 1, 3))
        else:
            shard_major.append(g.reshape(nl, nsh, shp[1], shp[2]).transpose(1, 0, 2, 3))
    packs = [_pack([sm_[s] for sm_ in shard_major], 16) for s in range(nsh)]
    gpack = jnp.stack([p_[0] for p_ in packs])
    gwhere = packs[0][1]
    half = gpack.shape[1] // 2
    gpack = gpack.reshape(nsh, 2, half, PACK_W)
    from_pair = pair_exchange_halves(gpack, name="reduce_pair")
    chip_sum = add_kept_half(gpack, from_pair, jnp.reshape(mc, (1,)).astype(jnp.int32), name="reduce_pair_add")
    from_xy = all_to_all_xy(chip_sum, name="reduce_xy")
    reduced = sum_slots(from_xy, name="reduce_xy_add")
    gfull = pair_all_gather(reduced, name="reduce_share").reshape(2 * half, PACK_W)
    for (k, _), wh in zip(BIG, gwhere):
        grads[k] = _unpack(gfull, wh, wts[k].shape)

    outs = {k: adamw(wts[k], grads[k], mom[k], var[k], name="adamw_" + k) for k in WEIGHTS}
    return (loss, grad_x, *[grads[k] for k in WEIGHTS], *[outs[k][0] for k in WEIGHTS], *[outs[k][1] for k in WEIGHTS],
            *[outs[k][2] for k in WEIGHTS])
```

```python
import functools

import jax
import jax.numpy as jnp
import numpy as np
from jax import lax
from jax.experimental import pallas as pl
from jax.experimental.pallas import tpu as pltpu

F32 = jnp.float32
BF16 = jnp.bfloat16
HI = lax.Precision.HIGHEST
MESH = pl.DeviceIdType.MESH
ANY = pl.BlockSpec(memory_space=pl.ANY)

V7X_VMEM_BYTES = 64 * 1024 * 1024
VMEM_LIMIT = V7X_VMEM_BYTES - 8 * 1024 * 1024
LANE = 128
PACK_W = 1024

GRID_W = 64
HEAD_DIM = 64
NA_HEADS, NA_ROWS, NA_COLS = 4, 8, 16
GQA_HEADS, GQA_KV_HEADS = 8, 2
MLA_HEADS, MLA_Q_RANK, MLA_KV_RANK, MLA_NOPE, MLA_ROPE, MLA_V = 4, 256, 128, 64, 32, 64
N_MOD = 9
ROPE_THETA = 10000.0
EPS = 1e-6
NEG_BIG = -1e30
NA_W = NA_HEADS * HEAD_DIM
GQ_W = GQA_HEADS * HEAD_DIM
GK_W = GQA_KV_HEADS * HEAD_DIM
MAIN_W = 3 * NA_W + GQ_W + 2 * GK_W + MLA_Q_RANK + MLA_KV_RANK + MLA_ROPE
MAIN_PAD = 2048
TQ = 256
TM = 256

ADAM_LR, ADAM_B1, ADAM_B2, ADAM_EPS, ADAM_WD, ADAM_STEP = 0.001, 0.9, 0.999, 1e-08, 0.01, 10

ARG_NAMES = ['x', 'c', 'ctx', 'c_ctx', 'w_ada', 'b_ada', 'ffn1_norm', 'ffn1_w_gate', 'ffn1_w_up', 'ffn1_w_down', 'mix_norm', 'w_in',
             'na_rel_bias', 'gqa_q_norm', 'gqa_k_norm', 'mla_q_norm', 'mla_kv_norm', 'mla_w_uq', 'mla_w_ukv', 'w_branch_a',
             'w_branch_b', 'w_branch_c', 'w_out', 'ffn2_norm', 'ffn2_w_gate', 'ffn2_w_up', 'ffn2_w_down', 'final_norm']
WEIGHTS = ARG_NAMES[3:]
BIG = [('ffn1_w_gate', 1), ('ffn1_w_up', 1), ('ffn1_w_down', 0), ('w_in', 1), ('mla_w_uq', 1), ('mla_w_ukv', 1),
       ('w_branch_a', 1), ('w_branch_b', 1), ('w_branch_c', 1), ('w_out', 0), ('ffn2_w_gate', 1), ('ffn2_w_up', 1),
       ('ffn2_w_down', 0)]
SMALL_LAYER = ['ffn1_norm', 'mix_norm', 'na_rel_bias', 'gqa_q_norm', 'gqa_k_norm', 'mla_q_norm', 'mla_kv_norm', 'ffn2_norm']


def _cp(*sem):
    return pltpu.CompilerParams(dimension_semantics=sem, vmem_limit_bytes=VMEM_LIMIT)


def _tile(dim, cands):
    for t in cands:
        if dim % t == 0:
            return t
    return dim


def _row_tile(rows, cap=512, mult=16):
    best = None
    for t in range(mult, min(rows, cap) + 1, mult):
        if rows % t == 0:
            best = t
    return best or rows


def _ceil_to(n, m):
    return -(-n // m) * m


def mm(a, b, *, name, ta=False, tb=False, out_dtype=F32, precise=False):
    m, k = (a.shape[1], a.shape[0]) if ta else a.shape
    n = b.shape[0] if tb else b.shape[1]
    tm = _tile(m, (512, 256, 128))
    tn = _tile(n, (1024, 1408, 512, 256, 128))
    tk = _tile(k, (1024, 1408, 512, 256, 128))
    nk = k // tk
    dims = (((0 if ta else 1,), (1 if tb else 0,)), ((), ()))

    def body(a_ref, b_ref, o_ref, *acc):
        if precise:
            part = lax.dot_general(a_ref[...].astype(F32), b_ref[...].astype(F32), dims, precision=HI, preferred_element_type=F32)
        else:
            part = lax.dot_general(a_ref[...].astype(BF16), b_ref[...].astype(BF16), dims, preferred_element_type=F32)
        if nk == 1:
            o_ref[...] = part.astype(o_ref.dtype)
        else:
            acc_ref, = acc
            kk = pl.program_id(2)

            @pl.when(kk == 0)
            def _():
                acc_ref[...] = part

            @pl.when(kk > 0)
            def _():
                acc_ref[...] += part

            @pl.when(kk == nk - 1)
            def _():
                o_ref[...] = acc_ref[...].astype(o_ref.dtype)

    a_spec = pl.BlockSpec((tk, tm), lambda i, j, kk: (kk, i)) if ta else pl.BlockSpec((tm, tk), lambda i, j, kk: (i, kk))
    b_spec = pl.BlockSpec((tn, tk), lambda i, j, kk: (j, kk)) if tb else pl.BlockSpec((tk, tn), lambda i, j, kk: (kk, j))
    return pl.pallas_call(
        body, name=name, grid=(m // tm, n // tn, nk), in_specs=[a_spec, b_spec],
        out_specs=pl.BlockSpec((tm, tn), lambda i, j, kk: (i, j)),
        out_shape=jax.ShapeDtypeStruct((m, n), out_dtype),
        scratch_shapes=[pltpu.VMEM((tm, tn), F32)] if nk > 1 else [],
        compiler_params=_cp("parallel", "parallel", "arbitrary"),
    )(a, b)


def rowcall(name, fn, ins, outs, *, nb, nt, nct):
    in_specs, arrays = [], []
    for arr, kind in ins:
        arrays.append(arr)
        if kind == 'tok':
            in_specs.append(pl.BlockSpec((None, TM, arr.shape[-1]), lambda b, t: (b, t, 0)))
        elif kind == 'lat':
            in_specs.append(pl.BlockSpec((None, TM, arr.shape[-1]), lambda b, t: (b, jnp.maximum(t - nct, 0), 0)))
        elif kind == 'pos':
            in_specs.append(pl.BlockSpec((TM, arr.shape[-1]), lambda b, t: (t, 0)))
        elif kind == 'mod':
            in_specs.append(pl.BlockSpec((None, None, 1, arr.shape[-1]), lambda b, t: (b, jnp.where(t >= nct, 1, 0), 0, 0)))
        elif kind == 'full':
            in_specs.append(pl.BlockSpec(arr.shape, lambda b, t, nd=arr.ndim: (0,) * nd))
        else:
            _, w, j = kind
            in_specs.append(pl.BlockSpec((None, TM, w), lambda b, t, j=j: (b, t, j)))
    out_specs, out_shape = [], []
    for o in outs:
        if o[0] == 'tok':
            out_specs.append(pl.BlockSpec((None, TM, o[1]), lambda b, t: (b, t, 0)))
            out_shape.append(jax.ShapeDtypeStruct((nb, nt * TM, o[1]), o[2]))
        elif o[0] == 'mod':
            out_specs.append(pl.BlockSpec((None, None, 1, o[1]), lambda b, t: (b, jnp.where(t >= nct, 1, 0), 0, 0)))
            out_shape.append(jax.ShapeDtypeStruct((nb, 2, 1, o[1]), F32))
        else:
            out_specs.append(pl.BlockSpec(o[1], lambda b, t, nd=len(o[1]): (0,) * nd))
            out_shape.append(jax.ShapeDtypeStruct(o[1], F32))
    n_in = len(ins)

    def body(*refs):
        b, t = pl.program_id(0), pl.program_id(1)
        res = fn(t < nct, *[r[...] for r in refs[:n_in]])
        for ref, o, val in zip(refs[n_in:], outs, res, strict=True):
            if o[0] == 'tok':
                ref[...] = val.astype(ref.dtype)
                continue
            first = ((t == 0) | (t == nct)) if o[0] == 'mod' else ((b == 0) & (t == 0))

            @pl.when(first)
            def _(ref=ref, val=val):
                ref[...] = val

            @pl.when(jnp.logical_not(first))
            def _(ref=ref, val=val):
                ref[...] += val

    return pl.pallas_call(body, name=name, grid=(nb, nt), in_specs=in_specs, out_specs=out_specs, out_shape=out_shape,
                          compiler_params=_cp("arbitrary", "arbitrary"))(*arrays)


def _rms(x, g):
    return x * lax.rsqrt(jnp.mean(x * x, axis=-1, keepdims=True) + EPS) * g


def f_normmod(h, g, shift, scale):
    return _rms(h, g) * (1.0 + scale) + shift


def f_act(gu):
    f = gu.shape[-1] // 2
    return jax.nn.silu(gu[:, :f]) * gu[:, f:]


def f_merge(ga, gb, gm, ya, yb, ym):
    return jax.nn.sigmoid(ga) * ya + jax.nn.sigmoid(gb) * yb + jax.nn.sigmoid(gm) * ym


def f_post(p, cb, sb, cm, sm, qn, kn, mqn, mkvn, wuq, wukv, s_b, r_b, t_b, r_m, rep):
    def hnorm(x, g, w):
        ms = jnp.dot(x * x, s_b[:w, :w], precision=HI, preferred_element_type=F32)
        gw = jnp.dot(g, t_b[:, :w], precision=HI, preferred_element_type=F32)
        return x * lax.rsqrt(ms + EPS) * gw

    def rope(x, cos, sin, rot):
        return x * cos + jnp.dot(x, rot, precision=HI, preferred_element_type=F32) * sin

    o = 3 * NA_W
    a_q, a_k, a_v = p[:, 0:NA_W], p[:, NA_W:2 * NA_W], p[:, 2 * NA_W:o]
    b_q = rope(hnorm(p[:, o:o + GQ_W], qn, GQ_W), cb, sb, r_b)
    o += GQ_W
    b_k = rope(hnorm(p[:, o:o + GK_W], kn, GK_W), cb[:, :GK_W], sb[:, :GK_W], r_b[:GK_W, :GK_W])
    b_v = p[:, o + GK_W:o + 2 * GK_W]
    o += 2 * GK_W
    q_lat = jnp.dot(_rms(p[:, o:o + MLA_Q_RANK], mqn).astype(BF16), wuq.astype(BF16), preferred_element_type=F32)
    o += MLA_Q_RANK
    kv_lat = jnp.dot(_rms(p[:, o:o + MLA_KV_RANK], mkvn).astype(BF16), wukv.astype(BF16), preferred_element_type=F32)
    o += MLA_KV_RANK
    nw = MLA_HEADS * MLA_NOPE
    mq_nope, mq_rope = q_lat[:, :nw], rope(q_lat[:, nw:], cm, sm, r_m)
    mk_nope, m_v = kv_lat[:, :nw], kv_lat[:, nw:]
    mk_rope = jnp.dot(rope(p[:, o:o + LANE], cm, sm, r_m), rep, precision=HI, preferred_element_type=F32)
    return (a_q, a_k, a_v, b_q, b_k, b_v, mq_nope, mq_rope, mk_nope, mk_rope, m_v)


POST_WIDTHS = (NA_W, NA_W, NA_W, GQ_W, GK_W, GK_W, MLA_HEADS * MLA_NOPE, MLA_HEADS * MLA_ROPE, MLA_HEADS * MLA_NOPE,
               MLA_HEADS * MLA_ROPE, MLA_HEADS * MLA_V)


_NT = (((1,), (1,)), ((), ()))
_TN = (((0,), (0,)), ((), ()))


def _dot(a, b, dims=None):
    if dims is None:
        return jnp.dot(a, b, preferred_element_type=F32)
    return lax.dot_general(a, b, dims, preferred_element_type=F32)


def att_fwd(q, k, v, *, scale, lc, name):
    nb, hq, t, dk = q.shape
    hkv, dv = k.shape[1], v.shape[-1]
    grp = hq // hkv
    nctb = lc // TQ

    def body(q_ref, k_ref, v_ref, o_ref, lse_ref):
        i = pl.program_id(2)

        def run(kk, vv):
            s = _dot(q_ref[...], kk, _NT) * scale
            m = jnp.max(s, axis=-1, keepdims=True)
            p = jnp.exp(s - m)
            l = jnp.sum(p, axis=-1, keepdims=True)
            o_ref[...] = (_dot(p.astype(BF16), vv) / l).astype(o_ref.dtype)
            lse_ref[...] = m + jnp.log(l)

        @pl.when(i < nctb)
        def _():
            run(k_ref[0:lc, :], v_ref[0:lc, :])

        @pl.when(i >= nctb)
        def _():
            run(k_ref[...], v_ref[...])

    return pl.pallas_call(
        body, name=name, grid=(nb, hq, t // TQ),
        in_specs=[pl.BlockSpec((None, None, TQ, dk), lambda b, h, i: (b, h, i, 0)),
                  pl.BlockSpec((None, None, t, dk), lambda b, h, i: (b, h // grp, 0, 0)),
                  pl.BlockSpec((None, None, t, dv), lambda b, h, i: (b, h // grp, 0, 0))],
        out_specs=[pl.BlockSpec((None, None, TQ, dv), lambda b, h, i: (b, h, i, 0)),
                   pl.BlockSpec((None, None, TQ, 1), lambda b, h, i: (b, h, i, 0))],
        out_shape=[jax.ShapeDtypeStruct((nb, hq, t, dv), BF16), jax.ShapeDtypeStruct((nb, hq, t, 1), F32)],
        compiler_params=_cp("parallel", "parallel", "arbitrary"),
    )(q, k, v)


def att_bwd(q, k, v, lse, do, *, scale, lc, name):
    nb, hq, t, dk = q.shape
    hkv, dv = k.shape[1], v.shape[-1]
    grp = hq // hkv
    nctb = lc // TQ

    def body(q_ref, k_ref, v_ref, lse_ref, do_ref, dq_ref, dk_ref, dv_ref):
        g, i = pl.program_id(2), pl.program_id(3)

        @pl.when((g == 0) & (i == 0))
        def _():
            dk_ref[...] = jnp.zeros_like(dk_ref)
            dv_ref[...] = jnp.zeros_like(dv_ref)

        def run(rows):
            qq, dd = q_ref[...], do_ref[...]
            kk, vv = k_ref[rows, :], v_ref[rows, :]
            p = jnp.exp(_dot(qq, kk, _NT) * scale - lse_ref[...])
            dp = _dot(dd, vv, _NT)
            delta = jnp.sum(p * dp, axis=-1, keepdims=True)
            ds = (p * (dp - delta) * scale).astype(BF16)
            dq_ref[...] = _dot(ds, kk)
            dk_ref[rows, :] += _dot(ds, qq, _TN)
            dv_ref[rows, :] += _dot(p.astype(BF16), dd, _TN)

        @pl.when(i < nctb)
        def _():
            run(pl.ds(0, lc))

        @pl.when(i >= nctb)
        def _():
            run(pl.ds(0, t))

    qmap = lambda b, hk, g, i: (b, hk * grp + g, i, 0)
    kmap = lambda b, hk, g, i: (b, hk, 0, 0)
    return pl.pallas_call(
        body, name=name, grid=(nb, hkv, grp, t // TQ),
        in_specs=[pl.BlockSpec((None, None, TQ, dk), qmap), pl.BlockSpec((None, None, t, dk), kmap),
                  pl.BlockSpec((None, None, t, dv), kmap), pl.BlockSpec((None, None, TQ, 1), qmap),
                  pl.BlockSpec((None, None, TQ, dv), qmap)],
        out_specs=[pl.BlockSpec((None, None, TQ, dk), qmap), pl.BlockSpec((None, None, t, dk), kmap),
                   pl.BlockSpec((None, None, t, dv), kmap)],
        out_shape=[jax.ShapeDtypeStruct((nb, hq, t, dk), F32), jax.ShapeDtypeStruct((nb, hkv, t, dk), F32),
                   jax.ShapeDtypeStruct((nb, hkv, t, dv), F32)],
        compiler_params=_cp("arbitrary", "arbitrary", "arbitrary", "arbitrary"),
    )(q, k, v, lse, do)


def _na_window(st, nc, rows):
    r = jnp.maximum(st - nc, 0)
    r0 = jnp.clip(r - NA_ROWS // 2, 0, rows - NA_ROWS)
    return r, r0, r - r0


def na_fwd(q, k, v, bias, *, lc, name):
    nb, nh, t, d = q.shape
    nc, rows = lc // GRID_W, (t - lc) // GRID_W
    nwin = NA_ROWS * GRID_W
    scale = d ** -0.5

    def body(q_ref, k_ref, v_ref, bias_ref, o_ref, lse_ref):
        st = pl.program_id(2)
        qq = q_ref[...]
        kc, vc = k_ref[0:lc, :], v_ref[0:lc, :]
        s_ctx = _dot(qq, kc, _NT) * scale

        @pl.when(st < nc)
        def _():
            m = jnp.max(s_ctx, axis=-1, keepdims=True)
            p = jnp.exp(s_ctx - m)
            l = jnp.sum(p, axis=-1, keepdims=True)
            o_ref[...] = (_dot(p.astype(BF16), vc) / l).astype(o_ref.dtype)
            lse_ref[...] = m + jnp.log(l)

        @pl.when(st >= nc)
        def _():
            _, r0, _ = _na_window(st, nc, rows)
            win = pl.ds(pl.multiple_of(lc + r0 * GRID_W, GRID_W), nwin)
            kw, vw = k_ref[win, :], v_ref[win, :]
            s_loc = _dot(qq, kw, _NT) * scale + bias_ref[...]
            m = jnp.maximum(jnp.max(s_loc, axis=-1, keepdims=True), jnp.max(s_ctx, axis=-1, keepdims=True))
            p_loc, p_ctx = jnp.exp(s_loc - m), jnp.exp(s_ctx - m)
            l = jnp.sum(p_loc, axis=-1, keepdims=True) + jnp.sum(p_ctx, axis=-1, keepdims=True)
            o_ref[...] = ((_dot(p_loc.astype(BF16), vw) + _dot(p_ctx.astype(BF16), vc)) / l).astype(o_ref.dtype)
            lse_ref[...] = m + jnp.log(l)

    qmap = lambda h, b, st: (b, h, st, 0)
    kmap = lambda h, b, st: (b, h, 0, 0)
    return pl.pallas_call(
        body, name=name, grid=(nh, nb, nc + rows),
        in_specs=[pl.BlockSpec((None, None, GRID_W, d), qmap), pl.BlockSpec((None, None, t, d), kmap),
                  pl.BlockSpec((None, None, t, d), kmap),
                  pl.BlockSpec((None, None, GRID_W, nwin), lambda h, b, st: (h, _na_window(st, nc, rows)[2], 0, 0))],
        out_specs=[pl.BlockSpec((None, None, GRID_W, d), qmap), pl.BlockSpec((None, None, GRID_W, 1), qmap)],
        out_shape=[jax.ShapeDtypeStruct((nb, nh, t, d), BF16), jax.ShapeDtypeStruct((nb, nh, t, 1), F32)],
        compiler_params=_cp("parallel", "parallel", "arbitrary"),
    )(q, k, v, bias)


def na_bwd(q, k, v, bias, lse, do, *, lc, name):
    nb, nh, t, d = q.shape
    nc, rows = lc // GRID_W, (t - lc) // GRID_W
    nwin = NA_ROWS * GRID_W
    scale = d ** -0.5

    def body(q_ref, k_ref, v_ref, bias_ref, lse_ref, do_ref, dq_ref, dk_ref, dv_ref, db_ref):
        b, st = pl.program_id(1), pl.program_id(2)

        @pl.when(st == 0)
        def _():
            dk_ref[...] = jnp.zeros_like(dk_ref)
            dv_ref[...] = jnp.zeros_like(dv_ref)

        @pl.when((st == 0) & (b == 0))
        def _():
            db_ref[...] = jnp.zeros_like(db_ref)

        qq, dd, lse_v = q_ref[...], do_ref[...], lse_ref[...]
        ctx = pl.ds(0, lc)
        kc, vc = k_ref[ctx, :], v_ref[ctx, :]
        p_ctx = jnp.exp(_dot(qq, kc, _NT) * scale - lse_v)
        dp_ctx = _dot(dd, vc, _NT)
        dsum_ctx = jnp.sum(p_ctx * dp_ctx, axis=-1, keepdims=True)

        @pl.when(st < nc)
        def _():
            ds = (p_ctx * (dp_ctx - dsum_ctx) * scale).astype(BF16)
            dq_ref[...] = _dot(ds, kc)
            dk_ref[ctx, :] += _dot(ds, qq, _TN)
            dv_ref[ctx, :] += _dot(p_ctx.astype(BF16), dd, _TN)

        @pl.when(st >= nc)
        def _():
            _, r0, case = _na_window(st, nc, rows)
            win = pl.ds(pl.multiple_of(lc + r0 * GRID_W, GRID_W), nwin)
            kw, vw = k_ref[win, :], v_ref[win, :]
            p_loc = jnp.exp(_dot(qq, kw, _NT) * scale + bias_ref[...] - lse_v)
            dp_loc = _dot(dd, vw, _NT)
            delta = dsum_ctx + jnp.sum(p_loc * dp_loc, axis=-1, keepdims=True)
            ds_loc = p_loc * (dp_loc - delta)
            db_ref[case] += ds_loc
            ds_loc = (ds_loc * scale).astype(BF16)
            ds_ctx = (p_ctx * (dp_ctx - delta) * scale).astype(BF16)
            dq_ref[...] = _dot(ds_loc, kw) + _dot(ds_ctx, kc)
            dk_ref[win, :] += _dot(ds_loc, qq, _TN)
            dk_ref[ctx, :] += _dot(ds_ctx, qq, _TN)
            dv_ref[win, :] += _dot(p_loc.astype(BF16), dd, _TN)
            dv_ref[ctx, :] += _dot(p_ctx.astype(BF16), dd, _TN)

    qmap = lambda h, b, st: (b, h, st, 0)
    kmap = lambda h, b, st: (b, h, 0, 0)
    return pl.pallas_call(
        body, name=name, grid=(nh, nb, nc + rows),
        in_specs=[pl.BlockSpec((None, None, GRID_W, d), qmap), pl.BlockSpec((None, None, t, d), kmap),
                  pl.BlockSpec((None, None, t, d), kmap),
                  pl.BlockSpec((None, None, GRID_W, nwin), lambda h, b, st: (h, _na_window(st, nc, rows)[2], 0, 0)),
                  pl.BlockSpec((None, None, GRID_W, 1), qmap), pl.BlockSpec((None, None, GRID_W, d), qmap)],
        out_specs=[pl.BlockSpec((None, None, GRID_W, d), qmap), pl.BlockSpec((None, None, t, d), kmap),
                   pl.BlockSpec((None, None, t, d), kmap),
                   pl.BlockSpec((None, NA_ROWS, GRID_W, nwin), lambda h, b, st: (h, 0, 0, 0))],
        out_shape=[jax.ShapeDtypeStruct((nb, nh, t, d), F32), jax.ShapeDtypeStruct((nb, nh, t, d), F32),
                   jax.ShapeDtypeStruct((nb, nh, t, d), F32), jax.ShapeDtypeStruct((nh, NA_ROWS, GRID_W, nwin), F32)],
        compiler_params=_cp("arbitrary", "arbitrary", "arbitrary"),
    )(q, k, v, bias, lse, do)


def _na_tables():
    cols = np.arange(GRID_W)
    c0 = np.clip(cols - NA_COLS // 2, 0, GRID_W - NA_COLS)
    col_in = (cols[None, :] >= c0[:, None]) & (cols[None, :] < c0[:, None] + NA_COLS)
    dc = np.clip(cols[None, :] - cols[:, None] + NA_COLS - 1, 0, 2 * NA_COLS - 2)
    dr = np.arange(NA_ROWS)[None, :] + (NA_ROWS - 1) - np.arange(NA_ROWS)[:, None]
    return col_in, dc, dr


def _na_onehots():
    col_in, dc, dr = _na_tables()
    e1 = np.zeros((GRID_W, GRID_W, LANE), np.float32)
    qi, ki = np.nonzero(col_in)
    e1[qi, ki, dc[qi, ki]] = 1.0
    e2 = np.zeros((2 * NA_ROWS, NA_ROWS, NA_ROWS), np.float32)
    ci, ji = np.meshgrid(np.arange(NA_ROWS), np.arange(NA_ROWS), indexing='ij')
    e2[dr[ci, ji], ci, ji] = 1.0
    return jnp.asarray(e1.reshape(GRID_W * GRID_W, LANE)), jnp.asarray(e2.reshape(2 * NA_ROWS, NA_ROWS * NA_ROWS)), col_in


def na_expand_bias(rel_bias, name):
    e1, e2, col_in = _na_onehots()
    nh = rel_bias.shape[0]
    nrow = NA_ROWS * NA_ROWS
    rel = jnp.pad(rel_bias, ((0, 0), (0, 1), (0, LANE - rel_bias.shape[2])))
    rel = rel.transpose(1, 0, 2).reshape(2 * NA_ROWS, nh * LANE)
    y = mm(e2, rel, ta=True, name=name + "_rows", precise=True)
    y = y.reshape(nrow, nh, LANE).transpose(1, 0, 2).reshape(nh * nrow, LANE)
    g = mm(y, e1, tb=True, name=name + "_cols", precise=True)
    g = g.reshape(nh, NA_ROWS, NA_ROWS, GRID_W, GRID_W).transpose(0, 1, 3, 2, 4)
    g = jnp.where(col_in[None, None, :, None, :], g, NEG_BIG)
    return g.reshape(nh, NA_ROWS, GRID_W, NA_ROWS * GRID_W)


def na_reduce_bias(dexp, name):
    e1, e2, _ = _na_onehots()
    nh = dexp.shape[0]
    x = dexp.reshape(nh, NA_ROWS, GRID_W, NA_ROWS, GRID_W).transpose(0, 1, 3, 2, 4).reshape(nh * NA_ROWS * NA_ROWS, GRID_W * GRID_W)
    y = mm(x, e1, name=name + "_cols", precise=True)
    y = y.reshape(nh, NA_ROWS * NA_ROWS, LANE).transpose(1, 0, 2).reshape(NA_ROWS * NA_ROWS, nh * LANE)
    z = mm(e2, y, name=name + "_rows", precise=True)
    return z.reshape(2 * NA_ROWS, nh, LANE).transpose(1, 0, 2)[:, :2 * NA_ROWS - 1, :2 * NA_COLS - 1]


def _rot_matrix(width, d_rot):
    f = d_rot // 4
    r = np.zeros((width, width), np.float32)
    for base in range(0, width, d_rot // 2):
        for j in range(f):
            r[base + f + j, base + j] = -1.0
            r[base + j, base + f + j] = 1.0
    return r


def _rope_tables(s_len, lc, d_rot, reps):
    half = d_rot // 2
    freqs = ROPE_THETA ** (-jnp.arange(0, half, 2, dtype=F32) / half)
    tpos = jnp.arange(s_len)
    row = (tpos // GRID_W).astype(F32)[:, None] * freqs
    col = (tpos % GRID_W).astype(F32)[:, None] * freqs
    ang = jnp.concatenate([row, row, col, col], axis=-1)
    cos = jnp.concatenate([jnp.ones((lc, d_rot), F32), jnp.cos(ang)], axis=0)
    sin = jnp.concatenate([jnp.zeros((lc, d_rot), F32), jnp.sin(ang)], axis=0)
    return jnp.tile(cos, (1, reps)), jnp.tile(sin, (1, reps))


def _post_consts():
    s_b = np.kron(np.eye(GQA_HEADS, dtype=np.float32), np.full((HEAD_DIM, HEAD_DIM), 1.0 / HEAD_DIM, np.float32))
    t_b = np.tile(np.eye(HEAD_DIM, dtype=np.float32), (1, GQA_HEADS))
    r_b = _rot_matrix(GQ_W, HEAD_DIM)
    r_m = _rot_matrix(LANE, MLA_ROPE)
    rep = np.zeros((LANE, LANE), np.float32)
    for h in range(MLA_HEADS):
        rep[np.arange(MLA_ROPE), h * MLA_ROPE + np.arange(MLA_ROPE)] = 1.0
    return tuple(jnp.asarray(a) for a in (s_b, r_b, t_b, r_m, rep))


def _heads_to_parts(w, first):
    r = w.shape[0]
    w3 = w.reshape(r, MLA_HEADS, -1)
    return jnp.concatenate([w3[:, :, :first].reshape(r, -1), w3[:, :, first:].reshape(r, -1)], axis=1)


def _parts_to_heads(w, first):
    r = w.shape[0]
    nf = MLA_HEADS * first
    return jnp.concatenate([w[:, :nf].reshape(r, MLA_HEADS, first), w[:, nf:].reshape(r, MLA_HEADS, -1)], axis=2).reshape(r, -1)


def _place():
    return lax.axis_index("x"), lax.axis_index("y"), lax.axis_index("c")


def all_gather(v, *, name, with_c):
    flips = [(dx, dy, dc) for dx in (0, 1) for dy in (0, 1) for dc in ((0, 1) if with_c else (0,))][1:]
    n = len(flips) + 1

    def body(v_ref, out_ref, send_sems, recv_sems, local_sem):
        mx, my, mc = _place()

        def slot(px, py, pc):
            return 4 * px + 2 * py + pc if with_c else 2 * px + py

        mine = pltpu.make_async_copy(v_ref, out_ref.at[slot(mx, my, mc)], local_sem)
        mine.start()
        sends = []
        for j, (dx, dy, dc) in enumerate(flips):
            peer = (mx ^ dx, my ^ dy, mc ^ dc)
            cp = pltpu.make_async_remote_copy(src_ref=v_ref, dst_ref=out_ref.at[slot(mx, my, mc)], send_sem=send_sems.at[j],
                                              recv_sem=recv_sems.at[j], device_id=peer, device_id_type=MESH)
            cp.start()
            sends.append(cp)
        for j, (dx, dy, dc) in enumerate(flips):
            peer = (mx ^ dx, my ^ dy, mc ^ dc)
            pltpu.make_async_remote_copy(src_ref=v_ref, dst_ref=out_ref.at[slot(*peer)], send_sem=send_sems.at[j],
                                         recv_sem=recv_sems.at[j], device_id=peer, device_id_type=MESH).wait_recv()
        for cp in sends:
            cp.wait_send()
        mine.wait()

    return pl.pallas_call(
        body, name=name, in_specs=[ANY], out_specs=ANY, out_shape=jax.ShapeDtypeStruct((n,) + v.shape, v.dtype),
        scratch_shapes=[pltpu.SemaphoreType.DMA((n - 1,)), pltpu.SemaphoreType.DMA((n - 1,)), pltpu.SemaphoreType.DMA(())],
    )(v)


def gather_shards(v, *, name):
    _, h, w = v.shape
    flips = [(1, 0), (0, 1), (1, 1)]

    def body(v_ref, out_ref, send_sems, recv_sems, local_sem):
        mx, my, mc = _place()
        me = 2 * mx + my
        sib = (mx, my, 1 - mc)

        def copy(k, src, dst, to):
            return pltpu.make_async_remote_copy(src_ref=src, dst_ref=dst, send_sem=send_sems.at[k], recv_sem=recv_sems.at[k],
                                                device_id=to, device_id_type=MESH)

        mine = pltpu.make_async_copy(v_ref, out_ref.at[me], local_sem)
        mine.start()
        first = [copy(j, v_ref.at[mc], out_ref.at[me, mc], (mx ^ dx, my ^ dy, mc)) for j, (dx, dy) in enumerate(flips)]
        for cp in first:
            cp.start()
        passed = []
        for j, (dx, dy) in enumerate(flips):
            theirs = out_ref.at[2 * (mx ^ dx) + (my ^ dy), mc]
            copy(j, v_ref.at[mc], theirs, (mx ^ dx, my ^ dy, mc)).wait_recv()
            fw = copy(3 + j, theirs, theirs, sib)
            fw.start()
            passed.append(fw)
        for j, (dx, dy) in enumerate(flips):
            other = out_ref.at[2 * (mx ^ dx) + (my ^ dy), 1 - mc]
            copy(3 + j, other, other, sib).wait_recv()
        for cp in first + passed:
            cp.wait_send()
        mine.wait()

    return pl.pallas_call(
        body, name=name, in_specs=[ANY], out_specs=ANY, out_shape=jax.ShapeDtypeStruct((4, 2, h, w), v.dtype),
        scratch_shapes=[pltpu.SemaphoreType.DMA((6,)), pltpu.SemaphoreType.DMA((6,)), pltpu.SemaphoreType.DMA(())],
    )(v)


def pair_exchange_halves(g, *, name):
    n, _, h, w = g.shape

    def body(g_ref, out_ref, send_sems, recv_sems):
        mx, my, mc = _place()
        sib = (mx, my, 1 - mc)
        cps = [pltpu.make_async_remote_copy(src_ref=g_ref.at[s, 1 - mc], dst_ref=out_ref.at[s], send_sem=send_sems.at[s],
                                            recv_sem=recv_sems.at[s], device_id=sib, device_id_type=MESH) for s in range(n)]
        for cp in cps:
            cp.start()
        for cp in cps:
            cp.wait_recv()
        for cp in cps:
            cp.wait_send()

    return pl.pallas_call(
        body, name=name, in_specs=[ANY], out_specs=ANY, out_shape=jax.ShapeDtypeStruct((n, h, w), g.dtype),
        scratch_shapes=[pltpu.SemaphoreType.DMA((n,)), pltpu.SemaphoreType.DMA((n,))],
    )(g)


def all_to_all_xy(v, *, name):
    def body(v_ref, out_ref, send_sems, recv_sems, local_sem):
        mx, my, mc = _place()
        me = 2 * mx + my
        mine = pltpu.make_async_copy(v_ref.at[me], out_ref.at[me], local_sem)
        mine.start()
        flips = [(1, 0), (0, 1), (1, 1)]
        sends = []
        for j, (dx, dy) in enumerate(flips):
            px, py = mx ^ dx, my ^ dy
            cp = pltpu.make_async_remote_copy(src_ref=v_ref.at[2 * px + py], dst_ref=out_ref.at[me], send_sem=send_sems.at[j],
                                              recv_sem=recv_sems.at[j], device_id=(px, py, mc), device_id_type=MESH)
            cp.start()
            sends.append(cp)
        for j, (dx, dy) in enumerate(flips):
            px, py = mx ^ dx, my ^ dy
            pltpu.make_async_remote_copy(src_ref=v_ref.at[me], dst_ref=out_ref.at[2 * px + py], send_sem=send_sems.at[j],
                                         recv_sem=recv_sems.at[j], device_id=(px, py, mc), device_id_type=MESH).wait_recv()
        for cp in sends:
            cp.wait_send()
        mine.wait()

    return pl.pallas_call(
        body, name=name, in_specs=[ANY], out_specs=ANY, out_shape=jax.ShapeDtypeStruct(v.shape, v.dtype),
        scratch_shapes=[pltpu.SemaphoreType.DMA((3,)), pltpu.SemaphoreType.DMA((3,)), pltpu.SemaphoreType.DMA(())],
    )(v)


def pair_all_gather(v, *, name):
    def body(v_ref, out_ref, send_sem, recv_sem, local_sem):
        mx, my, mc = _place()
        mine = pltpu.make_async_copy(v_ref, out_ref.at[mc], local_sem)
        mine.start()
        cp = pltpu.make_async_remote_copy(src_ref=v_ref, dst_ref=out_ref.at[mc], send_sem=send_sem, recv_sem=recv_sem,
                                          device_id=(mx, my, 1 - mc), device_id_type=MESH)
        cp.start()
        pltpu.make_async_remote_copy(src_ref=v_ref, dst_ref=out_ref.at[1 - mc], send_sem=send_sem, recv_sem=recv_sem,
                                     device_id=(mx, my, 1 - mc), device_id_type=MESH).wait_recv()
        cp.wait_send()
        mine.wait()

    return pl.pallas_call(
        body, name=name, in_specs=[ANY], out_specs=ANY, out_shape=jax.ShapeDtypeStruct((2,) + v.shape, v.dtype),
        scratch_shapes=[pltpu.SemaphoreType.DMA(()), pltpu.SemaphoreType.DMA(()), pltpu.SemaphoreType.DMA(())],
    )(v)


def add_kept_half(g, r, c_idx, *, name, out_dtype):
    n, _, h, w = g.shape
    th = _row_tile(h)

    def body(c_ref, g_ref, r_ref, o_ref):
        o_ref[...] = (g_ref[...] + r_ref[...]).astype(o_ref.dtype)

    return pl.pallas_call(
        body, name=name,
        grid_spec=pltpu.PrefetchScalarGridSpec(
            num_scalar_prefetch=1, grid=(n, h // th),
            in_specs=[pl.BlockSpec((None, None, th, w), lambda s, i, c_ref: (s, c_ref[0], i, 0)),
                      pl.BlockSpec((None, th, w), lambda s, i, c_ref: (s, i, 0))],
            out_specs=pl.BlockSpec((None, th, w), lambda s, i, c_ref: (s, i, 0))),
        out_shape=jax.ShapeDtypeStruct((n, h, w), out_dtype), compiler_params=_cp("parallel", "parallel"),
    )(c_idx, g, r)


def sum_slots(v, *, name):
    n, rows, w = v.shape
    tr = _row_tile(rows, 256)

    def body(v_ref, o_ref):
        acc = v_ref[0].astype(F32)
        for s in range(1, n):
            acc = acc + v_ref[s].astype(F32)
        o_ref[...] = acc

    return pl.pallas_call(body, name=name, grid=(rows // tr,), in_specs=[pl.BlockSpec((n, tr, w), lambda i: (0, i, 0))],
                          out_specs=pl.BlockSpec((tr, w), lambda i: (i, 0)), out_shape=jax.ShapeDtypeStruct((rows, w), F32),
                          compiler_params=_cp("parallel"))(v)


def ada_fwd(c_rows, w_ada, b_shard, *, name):
    nl, d, ncol = w_ada.shape
    rows = c_rows.shape[0]
    tn = _tile(ncol, (768, 512, 256, 128))

    def body(c_ref, w_ref, b_ref, o_ref):
        o_ref[...] = jnp.dot(jax.nn.silu(c_ref[...]), w_ref[...], precision=HI, preferred_element_type=F32) + b_ref[...]

    return pl.pallas_call(
        body, name=name, grid=(nl, ncol // tn),
        in_specs=[pl.BlockSpec((rows, d), lambda l, j: (0, 0)), pl.BlockSpec((None, d, tn), lambda l, j: (l, 0, j)),
                  pl.BlockSpec((None, 1, tn), lambda l, j: (l, 0, j))],
        out_specs=pl.BlockSpec((None, rows, tn), lambda l, j: (l, 0, j)),
        out_shape=jax.ShapeDtypeStruct((nl, rows, ncol), F32), compiler_params=_cp("parallel", "parallel"),
    )(c_rows, w_ada, b_shard)


def ada_bwd(c_rows, w_ada, dm_shard, dm_full, n_ex, *, name):
    nl, d, ncol = w_ada.shape
    rows = c_rows.shape[0]
    tn = _tile(ncol, (768, 512, 256, 128))
    nj = ncol // tn

    def body(c_ref, w_ref, dm_ref, dmf_ref, gw_ref, gb_ref, dc_ref, dact_ref):
        l, j = pl.program_id(0), pl.program_id(1)
        act, act_vjp = jax.vjp(jax.nn.silu, c_ref[...])
        gw_ref[...] = lax.dot_general(act, dm_ref[...], _TN, precision=HI, preferred_element_type=F32)
        gb_ref[...] = jnp.sum(dmf_ref[...], axis=0, keepdims=True)
        part = lax.dot_general(dm_ref[...], w_ref[...], _NT, precision=HI, preferred_element_type=F32)

        @pl.when((l == 0) & (j == 0))
        def _():
            dact_ref[...] = part

        @pl.when((l > 0) | (j > 0))
        def _():
            dact_ref[...] += part

        @pl.when((l == nl - 1) & (j == nj - 1))
        def _():
            dc, = act_vjp(dact_ref[...])
            dc_ref[...] = jnp.sum(dc[n_ex:, :], axis=0, keepdims=True)

    return pl.pallas_call(
        body, name=name, grid=(nl, nj),
        in_specs=[pl.BlockSpec((rows, d), lambda l, j: (0, 0)), pl.BlockSpec((None, d, tn), lambda l, j: (l, 0, j)),
                  pl.BlockSpec((None, rows, tn), lambda l, j: (l, 0, j)),
                  pl.BlockSpec((None, rows, dm_full.shape[-1]), lambda l, j: (l, 0, 0))],
        out_specs=[pl.BlockSpec((None, d, tn), lambda l, j: (l, 0, j)),
                   pl.BlockSpec((None, 1, dm_full.shape[-1]), lambda l, j: (l, 0, 0)),
                   pl.BlockSpec((1, d), lambda l, j: (0, 0))],
        out_shape=[jax.ShapeDtypeStruct((nl, d, ncol), F32), jax.ShapeDtypeStruct((nl, 1, dm_full.shape[-1]), F32),
                   jax.ShapeDtypeStruct((1, d), F32)],
        scratch_shapes=[pltpu.VMEM((rows, d), F32)], compiler_params=_cp("arbitrary", "arbitrary"),
    )(c_rows, w_ada, dm_shard, dm_full)


def adamw(w, g, m, v, *, name):
    shape = w.shape
    cols = shape[-1]
    rows = int(np.prod(shape[:-1])) if len(shape) > 1 else 1
    tr = _row_tile(rows, 256)

    def body(w_ref, g_ref, m_ref, v_ref, d_ref, nm_ref, nv_ref):
        gg = g_ref[...]
        nm = ADAM_B1 * m_ref[...] + (1.0 - ADAM_B1) * gg
        nv = ADAM_B2 * v_ref[...] + (1.0 - ADAM_B2) * jnp.square(gg)
        m_hat = nm / (1.0 - ADAM_B1 ** ADAM_STEP)
        v_hat = nv / (1.0 - ADAM_B2 ** ADAM_STEP)
        d_ref[...] = -ADAM_LR * (m_hat / (jnp.sqrt(v_hat) + ADAM_EPS) + ADAM_WD * w_ref[...])
        nm_ref[...] = nm
        nv_ref[...] = nv

    spec = pl.BlockSpec((tr, cols), lambda i: (i, 0))
    out = pl.pallas_call(body, name=name, grid=(rows // tr,), in_specs=[spec] * 4, out_specs=[spec] * 3,
                         out_shape=[jax.ShapeDtypeStruct((rows, cols), F32)] * 3, compiler_params=_cp("parallel"),
                         )(*[a.reshape(rows, cols) for a in (w, g, m, v)])
    return tuple(o.reshape(shape) for o in out)


def _heads(a, nh):
    nb, t, w = a.shape
    return a.reshape(nb, t, nh, w // nh).transpose(0, 2, 1, 3)


def _unheads(a):
    nb, nh, t, d = a.shape
    return a.transpose(0, 2, 1, 3).reshape(nb, t, nh * d)


def local_step(h0, target, mods, lw, small, *, lc):
    nb, t, d = h0.shape
    nt, nct = t // TM, lc // TM
    s_len = t - lc
    nl = len(lw)
    f2 = lw[0]['w_gu1'].shape[1]
    consts = _post_consts()
    cos_b, sin_b = _rope_tables(s_len, lc, HEAD_DIM, GQA_HEADS)
    cos_m, sin_m = _rope_tables(s_len, lc, MLA_ROPE, MLA_HEADS)
    na_scale = HEAD_DIM ** -0.5
    mla_scale = (MLA_NOPE + MLA_ROPE) ** -0.5
    rc = functools.partial(rowcall, nb=nb, nt=nt, nct=nct)
    flat = lambda a: a.reshape(nb * t, a.shape[-1])
    unflat = lambda a: a.reshape(nb, t, a.shape[-1])
    vec = lambda a: a.reshape(1, -1)

    def ffn_fwd(h, g, mod3, w, tag):
        shift, scale, gate = mod3
        n, = rc(tag + "_norm", lambda _, *a: (f_normmod(*a),), [(h, 'tok'), (vec(g), 'full'), (shift, 'mod'), (scale, 'mod')],
                [('tok', d, BF16)])
        gu = unflat(mm(flat(n), w['w_gu'], name=tag + "_up"))
        act, = rc(tag + "_act", lambda _, a: (f_act(a),), [(gu, 'tok')], [('tok', f2 // 2, BF16)])
        y = unflat(mm(flat(act), w['w_d'], name=tag + "_down"))
        h2, = rc(tag + "_res", lambda _, hh, yy, gt: (hh + 0.5 * gt * yy,), [(h, 'tok'), (y, 'tok'), (gate, 'mod')], [('tok', d, F32)])
        return h2, (h, n, gu, act, y)

    def ffn_bwd(dh2, saved, g, mod3, w, tag):
        shift, scale, gate = mod3
        h, n, gu, act, y = saved
        dy, dgate = rc(tag + "_res_bwd", lambda _, dd, yy, gt: (0.5 * gt * dd, jnp.sum(0.5 * yy * dd, axis=0, keepdims=True)),
                       [(dh2, 'tok'), (y, 'tok'), (gate, 'mod')], [('tok', d, BF16), ('mod', d)])
        dw_d = mm(flat(act), flat(dy), ta=True, name=tag + "_down_dw")
        dact = unflat(mm(flat(dy), w['w_d'], tb=True, name=tag + "_down_dx"))

        def act_bwd(_, a, da):
            return jax.vjp(f_act, a)[1](da)

        dgu, = rc(tag + "_act_bwd", act_bwd, [(gu, 'tok'), (dact, 'tok')], [('tok', f2, BF16)])
        dw_gu = mm(flat(n), flat(dgu), ta=True, name=tag + "_up_dw")
        dn = unflat(mm(flat(dgu), w['w_gu'], tb=True, name=tag + "_up_dx"))

        def norm_bwd(_, hh, gg, sh, sc, dnn, dres):
            dh, dg, dsh, dsc = jax.vjp(f_normmod, hh, gg, sh, sc)[1](dnn)
            return dh + dres, dg, dsh, dsc

        dh, dg, dshift, dscale = rc(tag + "_norm_bwd", norm_bwd,
                                    [(h, 'tok'), (vec(g), 'full'), (shift, 'mod'), (scale, 'mod'), (dn, 'tok'), (dh2, 'tok')],
                                    [('tok', d, F32), ('full', (1, d)), ('mod', d), ('mod', d)])
        return dh, dg.reshape(d), (dshift, dscale, dgate), dw_gu, dw_d

    def post_ins(p, sm, w):
        return [(p, ('tokc', MAIN_PAD, 0)), (cos_b, 'pos'), (sin_b, 'pos'), (cos_m, 'pos'), (sin_m, 'pos'),
                (vec(sm['gqa_q_norm']), 'full'), (vec(sm['gqa_k_norm']), 'full'), (vec(sm['mla_q_norm']), 'full'),
                (vec(sm['mla_kv_norm']), 'full'), (w['w_uq'], 'full'), (w['w_ukv'], 'full')] + [(c, 'full') for c in consts]

    def split_heads(parts):
        a_q, a_k, a_v, b_q, b_k, b_v, mq_n, mq_r, mk_n, mk_r, m_v = parts
        cat = lambda nope, rope: _heads(jnp.concatenate(
            [nope.reshape(nb, t, MLA_HEADS, MLA_NOPE), rope.reshape(nb, t, MLA_HEADS, MLA_ROPE)], axis=-1).reshape(nb, t, -1), MLA_HEADS)
        return ((_heads(a_q, NA_HEADS), _heads(a_k, NA_HEADS), _heads(a_v, NA_HEADS)),
                (_heads(b_q, GQA_HEADS), _heads(b_k, GQA_KV_HEADS), _heads(b_v, GQA_KV_HEADS)),
                (cat(mq_n, mq_r), cat(mk_n, mk_r), _heads(m_v, MLA_HEADS)))

    def mix_fwd(h, sm, mod3, w, tag):
        shift, scale, gate = mod3
        n, = rc(tag + "_norm", lambda _, *a: (f_normmod(*a),), [(h, 'tok'), (vec(sm['mix_norm']), 'full'), (shift, 'mod'), (scale, 'mod')],
                [('tok', d, BF16)])
        p = unflat(mm(flat(n), w['w_in'], name=tag + "_in"))
        parts = rc(tag + "_post", lambda _, *a: f_post(*a), post_ins(p, sm, w), [('tok', wd, BF16) for wd in POST_WIDTHS])
        (aq, ak, av), (bq, bk, bv), (mq, mk, mv) = split_heads(parts)
        bias = na_expand_bias(sm['na_rel_bias'], tag + "_bias")
        o_a, lse_a = na_fwd(aq, ak, av, bias, lc=lc, name=tag + "_na")
        o_b, lse_b = att_fwd(bq, bk, bv, scale=na_scale, lc=lc, name=tag + "_gqa")
        o_m, lse_m = att_fwd(mq, mk, mv, scale=mla_scale, lc=lc, name=tag + "_mla")
        fo = [_unheads(o) for o in (o_a, o_b, o_m)]
        ys = [unflat(mm(flat(o), w[k], name=tag + "_br" + k[-1])) for o, k in zip(fo, ('w_a', 'w_b', 'w_c'))]
        gcols = [(p, ('tokc', d, MAIN_PAD // d + j)) for j in range(3)]
        y, = rc(tag + "_merge", lambda _, *a: (f_merge(*a),), gcols + [(v, 'tok') for v in ys], [('tok', d, BF16)])
        z = unflat(mm(flat(y), w['w_o'], name=tag + "_out"))
        h2, = rc(tag + "_res", lambda _, hh, zz, gt: (hh + gt * zz,), [(h, 'tok'), (z, 'tok'), (gate, 'mod')], [('tok', d, F32)])
        saved = (h, n, p, (aq, ak, av, lse_a, bias), (bq, bk, bv, lse_b), (mq, mk, mv, lse_m), fo, ys, y, z)
        return h2, saved

    def mix_bwd(dh2, saved, sm, mod3, w, tag):
        shift, scale, gate = mod3
        h, n, p, (aq, ak, av, lse_a, bias), (bq, bk, bv, lse_b), (mq, mk, mv, lse_m), fo, ys, y, z = saved
        dz, dgate = rc(tag + "_res_bwd", lambda _, dd, zz, gt: (gt * dd, jnp.sum(zz * dd, axis=0, keepdims=True)),
                       [(dh2, 'tok'), (z, 'tok'), (gate, 'mod')], [('tok', d, BF16), ('mod', d)])
        dw_o = mm(flat(y), flat(dz), ta=True, name=tag + "_out_dw")
        dy = unflat(mm(flat(dz), w['w_o'], tb=True, name=tag + "_out_dx"))
        gcols = [(p, ('tokc', d, MAIN_PAD // d + j)) for j in range(3)]

        def merge_bwd(_, ga, gb, gm, ya, yb, ym, dyy):
            dga, dgb, dgm, dya, dyb, dym = jax.vjp(f_merge, ga, gb, gm, ya, yb, ym)[1](dyy)
            return dya, dyb, dym, jnp.concatenate([dga, dgb, dgm], axis=-1)

        dya, dyb, dym, dgl = rc(tag + "_merge_bwd", merge_bwd, gcols + [(v, 'tok') for v in ys] + [(dy, 'tok')],
                                [('tok', d, BF16)] * 3 + [('tok', 3 * d, BF16)])
        dws, dos = {}, []
        for o, dyk, k in zip(fo, (dya, dyb, dym), ('w_a', 'w_b', 'w_c')):
            dws[k] = mm(flat(o), flat(dyk), ta=True, name=tag + "_br" + k[-1] + "_dw")
            dos.append(unflat(mm(flat(dyk), w[k], tb=True, out_dtype=BF16, name=tag + "_br" + k[-1] + "_dx")))
        do_a, do_b, do_m = _heads(dos[0], NA_HEADS), _heads(dos[1], GQA_HEADS), _heads(dos[2], MLA_HEADS)
        daq, dak, dav, dbias = na_bwd(aq, ak, av, bias, lse_a, do_a, lc=lc, name=tag + "_na_bwd")
        dbq, dbk, dbv = att_bwd(bq, bk, bv, lse_b, do_b, scale=na_scale, lc=lc, name=tag + "_gqa_bwd")
        dmq, dmk, dmv = att_bwd(mq, mk, mv, lse_m, do_m, scale=mla_scale, lc=lc, name=tag + "_mla_bwd")
        d_rel = na_reduce_bias(dbias, tag + "_relb")

        def msplit(a):
            a = a.transpose(0, 2, 1, 3)
            return a[..., :MLA_NOPE].reshape(nb, t, -1), a[..., MLA_NOPE:].reshape(nb, t, -1)

        dmq_n, dmq_r = msplit(dmq)
        dmk_n, dmk_r = msplit(dmk)
        cots = [_unheads(daq), _unheads(dak), _unheads(dav), _unheads(dbq), _unheads(dbk), _unheads(dbv), dmq_n, dmq_r, dmk_n,
                dmk_r, _unheads(dmv)]
        ins = post_ins(p, sm, w)
        n_in = len(ins)

        def post_bwd(_, *a):
            prim, cot, dgl_v = a[:11], a[n_in:n_in + 11], a[-1]
            outs = jax.vjp(lambda pp, qn, kn, mqn, mkvn, wuq, wukv: f_post(pp, *prim[1:5], qn, kn, mqn, mkvn, wuq, wukv, *a[11:n_in]),
                           prim[0], *prim[5:11])[1](tuple(cot))
            return (jnp.concatenate([outs[0].astype(BF16), dgl_v], axis=-1),) + tuple(outs[1:])

        res = rc(tag + "_post_bwd", post_bwd, ins + [(cv, 'tok') for cv in cots] + [(dgl, 'tok')],
                 [('tok', MAIN_PAD + 3 * d, BF16), ('full', (1, HEAD_DIM)), ('full', (1, HEAD_DIM)), ('full', (1, MLA_Q_RANK)),
                  ('full', (1, MLA_KV_RANK)), ('full', w['w_uq'].shape), ('full', w['w_ukv'].shape)])
        dp, dqn, dkn, dmqn, dmkvn, dw_uq, dw_ukv = res
        dw_in = mm(flat(n), flat(dp), ta=True, name=tag + "_in_dw")
        dn = unflat(mm(flat(dp), w['w_in'], tb=True, name=tag + "_in_dx"))

        def norm_bwd(_, hh, gg, sh, sc, dnn, dres):
            dh, dg, dsh, dsc = jax.vjp(f_normmod, hh, gg, sh, sc)[1](dnn)
            return dh + dres, dg, dsh, dsc

        dh, dg, dshift, dscale = rc(tag + "_norm_bwd", norm_bwd,
                                    [(h, 'tok'), (vec(sm['mix_norm']), 'full'), (shift, 'mod'), (scale, 'mod'), (dn, 'tok'), (dh2, 'tok')],
                                    [('tok', d, F32), ('full', (1, d)), ('mod', d), ('mod', d)])
        dsm = {'mix_norm': dg.reshape(d), 'na_rel_bias': d_rel, 'gqa_q_norm': dqn.reshape(-1), 'gqa_k_norm': dkn.reshape(-1),
               'mla_q_norm': dmqn.reshape(-1), 'mla_kv_norm': dmkvn.reshape(-1)}
        dwl = {'w_in': dw_in, 'w_uq': dw_uq, 'w_ukv': dw_ukv, 'w_o': dw_o, **dws}
        return dh, dsm, (dshift, dscale, dgate), dwl

    h = h0
    saved = []
    for l in range(nl):
        sm = {k: small[k][l] for k in SMALL_LAYER}
        h, s1 = ffn_fwd(h, sm['ffn1_norm'], mods[l][0:3], {'w_gu': lw[l]['w_gu1'], 'w_d': lw[l]['w_d1']}, f"l{l}_ffn1")
        h, s2 = mix_fwd(h, sm, mods[l][3:6], lw[l], f"l{l}_mix")
        h, s3 = ffn_fwd(h, sm['ffn2_norm'], mods[l][6:9], {'w_gu': lw[l]['w_gu2'], 'w_d': lw[l]['w_d2']}, f"l{l}_ffn2")
        saved.append((sm, s1, s2, s3))

    def final(is_ctx, hh, gg, tgt):
        def loss_fn(hv, gv):
            return 0.5 * jnp.sum(jnp.mean(jnp.square(_rms(hv, gv) - tgt), axis=-1))

        keep = jnp.where(is_ctx, 0.0, 1.0)
        loss, (dh, dg) = jax.value_and_grad(loss_fn, argnums=(0, 1))(hh, gg)
        return dh * keep, jnp.full((1, LANE), loss * keep, F32), dg * keep

    dh, loss, dg_final = rc("final_loss", final, [(h, 'tok'), (vec(small['final_norm']), 'full'), (target, 'lat')],
                            [('tok', d, F32), ('full', (1, LANE)), ('full', (1, d))])

    dsmall = {k: [None] * nl for k in SMALL_LAYER}
    dmods, dlw = [None] * nl, [None] * nl
    for l in reversed(range(nl)):
        sm, s1, s2, s3 = saved[l]
        dh, dg3, dm3, dw_gu2, dw_d2 = ffn_bwd(dh, s3, sm['ffn2_norm'], mods[l][6:9], {'w_gu': lw[l]['w_gu2'], 'w_d': lw[l]['w_d2']}, f"l{l}_ffn2")
        dh, dsm, dm2, dwl = mix_bwd(dh, s2, sm, mods[l][3:6], lw[l], f"l{l}_mix")
        dh, dg1, dm1, dw_gu1, dw_d1 = ffn_bwd(dh, s1, sm['ffn1_norm'], mods[l][0:3], {'w_gu': lw[l]['w_gu1'], 'w_d': lw[l]['w_d1']}, f"l{l}_ffn1")
        dmods[l] = list(dm1) + list(dm2) + list(dm3)
        dlw[l] = {'w_gu1': dw_gu1, 'w_d1': dw_d1, 'w_gu2': dw_gu2, 'w_d2': dw_d2, **dwl}
        dsm.update(ffn1_norm=dg1, ffn2_norm=dg3)
        for k in SMALL_LAYER:
            dsmall[k][l] = dsm[k]
    dsmall = {k: jnp.stack(v) for k, v in dsmall.items()}
    dsmall['final_norm'] = dg_final.reshape(d)
    return loss, dh, dmods, dlw, dsmall


def _pack(parts, pad_rows):
    flat, where, off = [], [], 0
    for a in parts:
        n = _ceil_to(a.size, PACK_W)
        flat.append(jnp.pad(a.reshape(-1), (0, n - a.size)))
        where.append((off, n // PACK_W))
        off += n // PACK_W
    total = _ceil_to(off, pad_rows)
    if total > off:
        flat.append(jnp.zeros(((total - off) * PACK_W,), flat[0].dtype))
    return jnp.concatenate(flat).reshape(total, PACK_W), where


def _unpack(buf, where, shape):
    off, rows = where
    return buf[off:off + rows].reshape(-1)[:int(np.prod(shape))].reshape(shape)


def layer_weights(full, l):
    wi = full['w_in'][l]
    d = wi.shape[0]
    return {
        'w_gu1': jnp.concatenate([full['ffn1_w_gate'][l], full['ffn1_w_up'][l]], axis=1), 'w_d1': full['ffn1_w_down'][l],
        'w_gu2': jnp.concatenate([full['ffn2_w_gate'][l], full['ffn2_w_up'][l]], axis=1), 'w_d2': full['ffn2_w_down'][l],
        'w_in': jnp.concatenate([wi[:, :MAIN_W], jnp.zeros((d, MAIN_PAD - MAIN_W), wi.dtype), wi[:, MAIN_W:]], axis=1),
        'w_uq': _heads_to_parts(full['mla_w_uq'][l], MLA_NOPE).astype(F32),
        'w_ukv': _heads_to_parts(full['mla_w_ukv'][l], MLA_NOPE).astype(F32),
        'w_a': full['w_branch_a'][l], 'w_b': full['w_branch_b'][l], 'w_c': full['w_branch_c'][l], 'w_o': full['w_out'][l]}


def layer_grads_by_name(dlw):
    per_name = {k: [] for k, _ in BIG}
    for g in dlw:
        f = g['w_gu1'].shape[1] // 2
        per_name['ffn1_w_gate'].append(g['w_gu1'][:, :f])
        per_name['ffn1_w_up'].append(g['w_gu1'][:, f:])
        per_name['ffn1_w_down'].append(g['w_d1'])
        per_name['ffn2_w_gate'].append(g['w_gu2'][:, :f])
        per_name['ffn2_w_up'].append(g['w_gu2'][:, f:])
        per_name['ffn2_w_down'].append(g['w_d2'])
        per_name['w_in'].append(jnp.concatenate([g['w_in'][:, :MAIN_W], g['w_in'][:, MAIN_PAD:]], axis=1))
        per_name['mla_w_uq'].append(_parts_to_heads(g['w_uq'], MLA_NOPE))
        per_name['mla_w_ukv'].append(_parts_to_heads(g['w_ukv'], MLA_NOPE))
        per_name['w_branch_a'].append(g['w_a'])
        per_name['w_branch_b'].append(g['w_b'])
        per_name['w_branch_c'].append(g['w_c'])
        per_name['w_out'].append(g['w_o'])
    return {k: jnp.stack(v) for k, v in per_name.items()}


def kernel(x, c, ctx, c_ctx, w_ada, b_ada, ffn1_norm, ffn1_w_gate, ffn1_w_up, ffn1_w_down, mix_norm, w_in, na_rel_bias, gqa_q_norm, gqa_k_norm, mla_q_norm, mla_kv_norm, mla_w_uq, mla_w_ukv, w_branch_a, w_branch_b, w_branch_c, w_out, ffn2_norm, ffn2_w_gate, ffn2_w_up, ffn2_w_down, final_norm, loss_target, m_c_ctx, m_w_ada, m_b_ada, m_ffn1_norm, m_ffn1_w_gate, m_ffn1_w_up, m_ffn1_w_down, m_mix_norm, m_w_in, m_na_rel_bias, m_gqa_q_norm, m_gqa_k_norm, m_mla_q_norm, m_mla_kv_norm, m_mla_w_uq, m_mla_w_ukv, m_w_branch_a, m_w_branch_b, m_w_branch_c, m_w_out, m_ffn2_norm, m_ffn2_w_gate, m_ffn2_w_up, m_ffn2_w_down, m_final_norm, v_c_ctx, v_w_ada, v_b_ada, v_ffn1_norm, v_ffn1_w_gate, v_ffn1_w_up, v_ffn1_w_down, v_mix_norm, v_w_in, v_na_rel_bias, v_gqa_q_norm, v_gqa_k_norm, v_mla_q_norm, v_mla_kv_norm, v_mla_w_uq, v_mla_w_ukv, v_w_branch_a, v_w_branch_b, v_w_branch_c, v_w_out, v_ffn2_norm, v_ffn2_w_gate, v_ffn2_w_up, v_ffn2_w_down, v_final_norm):
    args = locals()
    wts = {k: args[k] for k in WEIGHTS}
    mom = {k: args['m_' + k] for k in WEIGHTS}
    var = {k: args['v_' + k] for k in WEIGHTS}
    nb, s_len, d = x.shape
    lc = ctx.shape[1]
    nl = w_ada.shape[0]
    nsh, ndev = 4, 8
    mx, my, mc = _place()
    sidx = 2 * mx + my
    didx = 4 * mx + 2 * my + mc
    assert d % LANE == 0 and MAIN_PAD % d == 0 and lc % TQ == 0 and s_len % TQ == 0 and s_len // GRID_W >= NA_ROWS

    wpack, wwhere = _pack([wts[k].astype(BF16) for k, _ in BIG], 32)
    wall = gather_shards(wpack.reshape(2, -1, PACK_W), name="gather_weights").reshape(nsh, -1, PACK_W)
    full = {}
    for (k, ax), wh in zip(BIG, wwhere):
        shp = wts[k].shape
        parts = jnp.stack([_unpack(wall[s], wh, shp) for s in range(nsh)])
        if ax == 1:
            full[k] = parts.transpose(1, 2, 0, 3).reshape(nl, shp[1], nsh * shp[2])
        else:
            full[k] = parts.transpose(1, 0, 2, 3).reshape(nl, nsh * shp[1], shp[2])
    lw = [layer_weights(full, l) for l in range(nl)]

    n_ex = ndev * nb
    ncol = w_ada.shape[-1]
    c_all = all_gather(c, name="gather_cond", with_c=True).reshape(n_ex, d)
    c_rows = jnp.concatenate([c_all, jnp.broadcast_to(c_ctx[None], (n_ex, d))], axis=0)
    b_shard = lax.dynamic_slice_in_dim(b_ada, sidx * ncol, ncol, axis=1)[:, None, :]
    mod_sh = ada_fwd(c_rows, w_ada, b_shard, name="ada_fwd")
    mod_all = all_gather(mod_sh, name="gather_mod", with_c=False)
    mod_all = mod_all.transpose(1, 2, 0, 3).reshape(nl, 2 * n_ex, nsh * ncol)
    mod_x = lax.dynamic_slice_in_dim(mod_all, didx * nb, nb, axis=1)
    mod_c = jnp.broadcast_to(mod_all[:, n_ex:n_ex + 1], mod_x.shape)
    mods = [[jnp.stack([mod_c[l, :, j * d:(j + 1) * d], mod_x[l, :, j * d:(j + 1) * d]], axis=1)[:, :, None, :]
             for j in range(N_MOD)] for l in range(nl)]

    small = {k: wts[k] for k in SMALL_LAYER + ['final_norm']}
    h0 = jnp.concatenate([ctx, x], axis=1)
    loss_part, dh0, dmods, dlw, dsmall = local_step(h0, loss_target, mods, lw, small, lc=lc)
    grad_x = dh0[:, lc:]

    dmod_mine = jnp.stack([jnp.concatenate([m[:, :, 0, :] for m in dmods[l]], axis=-1) for l in range(nl)])
    small_names = SMALL_LAYER + ['final_norm']
    spack, swhere = _pack([loss_part] + [dsmall[k] for k in small_names] + [dmod_mine], 8)
    sall = all_gather(spack, name="gather_small", with_c=True)
    ssum = sum_slots(sall, name="sum_small")
    loss = _unpack(ssum, swhere[0], (1, LANE))[0, 0]
    grads = {k: _unpack(ssum, wh, wts[k].shape) for k, wh in zip(small_names, swhere[1:])}
    off, rows = swhere[-1]
    dm_all = sall[:, off:off + rows].reshape(ndev, -1)[:, :dmod_mine.size].reshape((ndev,) + dmod_mine.shape)
    dm_all = dm_all.transpose(1, 3, 0, 2, 4).reshape(nl, 2, n_ex, N_MOD * d)
    dm_rows = jnp.concatenate([dm_all[:, 1], dm_all[:, 0]], axis=1)
    dm_shard = lax.dynamic_slice_in_dim(dm_rows, sidx * ncol, ncol, axis=2)
    grads['w_ada'], gb, dc_part = ada_bwd(c_rows, w_ada, dm_shard, dm_rows, n_ex, name="ada_bwd")
    grads['b_ada'] = gb.reshape(b_ada.shape)
    dc_all = all_gather(jnp.pad(dc_part, ((0, 7), (0, 0))), name="gather_dcond", with_c=False)
    grads['c_ctx'] = sum_slots(dc_all, name="sum_dcond")[0]

    per_name = layer_grads_by_name(dlw)
    shard_major = []
    for k, ax in BIG:
        g = per_name[k]
        shp = wts[k].shape
        if ax == 1:
            shard_major.append(g.reshape(nl, shp[1], nsh, shp[2]).transpose(2, 0, 1, 3))
        else:
            shard_major.append(g.reshape(nl, nsh, shp[1], shp[2]).transpose(1, 0, 2, 3))
    packs = [_pack([sm_[s] for sm_ in shard_major], 32) for s in range(nsh)]
    gpack = jnp.stack([p_[0] for p_ in packs])
    gwhere = packs[0][1]
    half = gpack.shape[1] // 2
    gpack = gpack.reshape(nsh, 2, half, PACK_W)
    from_pair = pair_exchange_halves(gpack, name="reduce_pair")
    chip_sum = add_kept_half(gpack, from_pair, jnp.reshape(mc, (1,)).astype(jnp.int32), name="reduce_pair_add",
                             out_dtype=BF16)
    from_xy = all_to_all_xy(chip_sum, name="reduce_xy")
    reduced = sum_slots(from_xy, name="reduce_xy_add")
    gfull = pair_all_gather(reduced, name="reduce_share").reshape(2 * half, PACK_W)
    for (k, _), wh in zip(BIG, gwhere):
        grads[k] = _unpack(gfull, wh, wts[k].shape)

    outs = {k: adamw(wts[k], grads[k], mom[k], var[k], name="adamw_" + k) for k in WEIGHTS}
    return (loss, grad_x, *[grads[k] for k in WEIGHTS], *[outs[k][0] for k in WEIGHTS], *[outs[k][1] for k in WEIGHTS],
            *[outs[k][2] for k in WEIGHTS])
```

```python
import functools

import jax
import jax.numpy as jnp
import numpy as np
from jax import lax
from jax.experimental import pallas as pl
from jax.experimental.pallas import tpu as pltpu

F32 = jnp.float32
BF16 = jnp.bfloat16
HI = lax.Precision.HIGHEST
MESH = pl.DeviceIdType.MESH
ANY = pl.BlockSpec(memory_space=pl.ANY)

V7X_VMEM_BYTES = 64 * 1024 * 1024
VMEM_LIMIT = V7X_VMEM_BYTES - 8 * 1024 * 1024
LANE = 128
PACK_W = 1024

GRID_W = 64
HEAD_DIM = 64
NA_HEADS, NA_ROWS, NA_COLS = 4, 8, 16
GQA_HEADS, GQA_KV_HEADS = 8, 2
MLA_HEADS, MLA_Q_RANK, MLA_KV_RANK, MLA_NOPE, MLA_ROPE, MLA_V = 4, 256, 128, 64, 32, 64
N_MOD = 9
ROPE_THETA = 10000.0
EPS = 1e-6
NEG_BIG = -1e30
NA_W = NA_HEADS * HEAD_DIM
GQ_W = GQA_HEADS * HEAD_DIM
GK_W = GQA_KV_HEADS * HEAD_DIM
MAIN_W = 3 * NA_W + GQ_W + 2 * GK_W + MLA_Q_RANK + MLA_KV_RANK + MLA_ROPE
MAIN_PAD = 2048
TQ = 256
TM = 256

ADAM_LR, ADAM_B1, ADAM_B2, ADAM_EPS, ADAM_WD, ADAM_STEP = 0.001, 0.9, 0.999, 1e-08, 0.01, 10

ARG_NAMES = ['x', 'c', 'ctx', 'c_ctx', 'w_ada', 'b_ada', 'ffn1_norm', 'ffn1_w_gate', 'ffn1_w_up', 'ffn1_w_down', 'mix_norm', 'w_in',
             'na_rel_bias', 'gqa_q_norm', 'gqa_k_norm', 'mla_q_norm', 'mla_kv_norm', 'mla_w_uq', 'mla_w_ukv', 'w_branch_a',
             'w_branch_b', 'w_branch_c', 'w_out', 'ffn2_norm', 'ffn2_w_gate', 'ffn2_w_up', 'ffn2_w_down', 'final_norm']
WEIGHTS = ARG_NAMES[3:]
BIG = [('ffn1_w_gate', 1), ('ffn1_w_up', 1), ('ffn1_w_down', 0), ('w_in', 1), ('mla_w_uq', 1), ('mla_w_ukv', 1),
       ('w_branch_a', 1), ('w_branch_b', 1), ('w_branch_c', 1), ('w_out', 0), ('ffn2_w_gate', 1), ('ffn2_w_up', 1),
       ('ffn2_w_down', 0)]
SMALL_LAYER = ['ffn1_norm', 'mix_norm', 'na_rel_bias', 'gqa_q_norm', 'gqa_k_norm', 'mla_q_norm', 'mla_kv_norm', 'ffn2_norm']


def _cp(*sem):
    return pltpu.CompilerParams(dimension_semantics=sem, vmem_limit_bytes=VMEM_LIMIT)


def _tile(dim, cands):
    for t in cands:
        if dim % t == 0:
            return t
    return dim


def _row_tile(rows, cap=512, mult=16):
    best = None
    for t in range(mult, min(rows, cap) + 1, mult):
        if rows % t == 0:
            best = t
    return best or rows


def _ceil_to(n, m):
    return -(-n // m) * m


def mm(a, b, *, name, ta=False, tb=False, out_dtype=F32, precise=False):
    m, k = (a.shape[1], a.shape[0]) if ta else a.shape
    n = b.shape[0] if tb else b.shape[1]
    tm = _tile(m, (512, 256, 128))
    tn = _tile(n, (1024, 1408, 512, 256, 128))
    tk = _tile(k, (1024, 1408, 512, 256, 128))
    nk = k // tk
    dims = (((0 if ta else 1,), (1 if tb else 0,)), ((), ()))

    def body(a_ref, b_ref, o_ref, *acc):
        if precise:
            part = lax.dot_general(a_ref[...].astype(F32), b_ref[...].astype(F32), dims, precision=HI, preferred_element_type=F32)
        else:
            part = lax.dot_general(a_ref[...].astype(BF16), b_ref[...].astype(BF16), dims, preferred_element_type=F32)
        if nk == 1:
            o_ref[...] = part.astype(o_ref.dtype)
        else:
            acc_ref, = acc
            kk = pl.program_id(2)

            @pl.when(kk == 0)
            def _():
                acc_ref[...] = part

            @pl.when(kk > 0)
            def _():
                acc_ref[...] += part

            @pl.when(kk == nk - 1)
            def _():
                o_ref[...] = acc_ref[...].astype(o_ref.dtype)

    a_spec = pl.BlockSpec((tk, tm), lambda i, j, kk: (kk, i)) if ta else pl.BlockSpec((tm, tk), lambda i, j, kk: (i, kk))
    b_spec = pl.BlockSpec((tn, tk), lambda i, j, kk: (j, kk)) if tb else pl.BlockSpec((tk, tn), lambda i, j, kk: (kk, j))
    return pl.pallas_call(
        body, name=name, grid=(m // tm, n // tn, nk), in_specs=[a_spec, b_spec],
        out_specs=pl.BlockSpec((tm, tn), lambda i, j, kk: (i, j)),
        out_shape=jax.ShapeDtypeStruct((m, n), out_dtype),
        scratch_shapes=[pltpu.VMEM((tm, tn), F32)] if nk > 1 else [],
        compiler_params=_cp("parallel", "parallel", "arbitrary"),
    )(a, b)


def rowcall(name, fn, ins, outs, *, nb, nt, nct):
    in_specs, arrays = [], []
    for arr, kind in ins:
        arrays.append(arr)
        if kind == 'tok':
            in_specs.append(pl.BlockSpec((None, TM, arr.shape[-1]), lambda b, t: (b, t, 0)))
        elif kind == 'lat':
            in_specs.append(pl.BlockSpec((None, TM, arr.shape[-1]), lambda b, t: (b, jnp.maximum(t - nct, 0), 0)))
        elif kind == 'pos':
            in_specs.append(pl.BlockSpec((TM, arr.shape[-1]), lambda b, t: (t, 0)))
        elif kind == 'mod':
            in_specs.append(pl.BlockSpec((None, None, 1, arr.shape[-1]), lambda b, t: (b, jnp.where(t >= nct, 1, 0), 0, 0)))
        elif kind == 'full':
            in_specs.append(pl.BlockSpec(arr.shape, lambda b, t, nd=arr.ndim: (0,) * nd))
        else:
            _, w, j = kind
            in_specs.append(pl.BlockSpec((None, TM, w), lambda b, t, j=j: (b, t, j)))
    out_specs, out_shape = [], []
    for o in outs:
        if o[0] == 'tok':
            out_specs.append(pl.BlockSpec((None, TM, o[1]), lambda b, t: (b, t, 0)))
            out_shape.append(jax.ShapeDtypeStruct((nb, nt * TM, o[1]), o[2]))
        elif o[0] == 'mod':
            out_specs.append(pl.BlockSpec((None, None, 1, o[1]), lambda b, t: (b, jnp.where(t >= nct, 1, 0), 0, 0)))
            out_shape.append(jax.ShapeDtypeStruct((nb, 2, 1, o[1]), F32))
        else:
            out_specs.append(pl.BlockSpec(o[1], lambda b, t, nd=len(o[1]): (0,) * nd))
            out_shape.append(jax.ShapeDtypeStruct(o[1], F32))
    n_in = len(ins)

    def body(*refs):
        b, t = pl.program_id(0), pl.program_id(1)
        res = fn(t < nct, *[r[...] for r in refs[:n_in]])
        for ref, o, val in zip(refs[n_in:], outs, res, strict=True):
            if o[0] == 'tok':
                ref[...] = val.astype(ref.dtype)
                continue
            first = ((t == 0) | (t == nct)) if o[0] == 'mod' else ((b == 0) & (t == 0))

            @pl.when(first)
            def _(ref=ref, val=val):
                ref[...] = val

            @pl.when(jnp.logical_not(first))
            def _(ref=ref, val=val):
                ref[...] += val

    return pl.pallas_call(body, name=name, grid=(nb, nt), in_specs=in_specs, out_specs=out_specs, out_shape=out_shape,
                          compiler_params=_cp("arbitrary", "arbitrary"))(*arrays)


def _rms(x, g):
    return x * lax.rsqrt(jnp.mean(x * x, axis=-1, keepdims=True) + EPS) * g


def f_normmod(h, g, shift, scale):
    return _rms(h, g) * (1.0 + scale) + shift


def f_act(gu):
    f = gu.shape[-1] // 2
    return jax.nn.silu(gu[:, :f]) * gu[:, f:]


def f_merge(ga, gb, gm, ya, yb, ym):
    return jax.nn.sigmoid(ga) * ya + jax.nn.sigmoid(gb) * yb + jax.nn.sigmoid(gm) * ym


def f_post(p, cb, sb, cm, sm, qn, kn, mqn, mkvn, wuq, wukv, s_b, r_b, t_b, r_m, rep, dup):
    def hnorm(x, g, w):
        ms = jnp.dot(x * x, s_b[:w, :w], precision=HI, preferred_element_type=F32)
        gw = jnp.dot(g, t_b[:, :w], precision=HI, preferred_element_type=F32)
        return x * lax.rsqrt(ms + EPS) * gw

    def rope(x, cos, sin, rot):
        return x * cos + jnp.dot(x, rot, precision=HI, preferred_element_type=F32) * sin

    o = 3 * NA_W
    a_q, a_k, a_v = p[:, 0:NA_W], p[:, NA_W:2 * NA_W], p[:, 2 * NA_W:o]
    b_q = rope(hnorm(p[:, o:o + GQ_W], qn, GQ_W), cb, sb, r_b)
    o += GQ_W
    b_k = rope(hnorm(p[:, o:o + GK_W], kn, GK_W), cb[:, :GK_W], sb[:, :GK_W], r_b[:GK_W, :GK_W])
    b_v = p[:, o + GK_W:o + 2 * GK_W]
    o += 2 * GK_W
    q_lat = jnp.dot(_rms(p[:, o:o + MLA_Q_RANK], mqn).astype(BF16), wuq.astype(BF16), preferred_element_type=F32)
    o += MLA_Q_RANK
    kv_lat = jnp.dot(_rms(p[:, o:o + MLA_KV_RANK], mkvn).astype(BF16), wukv.astype(BF16), preferred_element_type=F32)
    o += MLA_KV_RANK
    nw = MLA_HEADS * MLA_NOPE
    mq_nope, mq_rope = q_lat[:, :nw], rope(q_lat[:, nw:], cm, sm, r_m)
    mk_nope, m_v = kv_lat[:, :nw], kv_lat[:, nw:]
    mk_rope = jnp.dot(rope(p[:, o:o + LANE], cm, sm, r_m), rep, precision=HI, preferred_element_type=F32)
    b_k2 = jnp.dot(b_k, dup, precision=HI, preferred_element_type=F32)
    b_v2 = jnp.dot(b_v, dup, precision=HI, preferred_element_type=F32)
    return (a_q, a_k, a_v, b_q, b_k2, b_v2, mq_nope, mq_rope, mk_nope, mk_rope, m_v)


POST_WIDTHS = (NA_W, NA_W, NA_W, GQ_W, 2 * GK_W, 2 * GK_W, MLA_HEADS * MLA_NOPE, MLA_HEADS * MLA_ROPE, MLA_HEADS * MLA_NOPE,
               MLA_HEADS * MLA_ROPE, MLA_HEADS * MLA_V)


_NT = (((1,), (1,)), ((), ()))
_TN = (((0,), (0,)), ((), ()))


def _dot(a, b, dims=None):
    if dims is None:
        return jnp.dot(a, b, preferred_element_type=F32)
    return lax.dot_general(a, b, dims, preferred_element_type=F32)


def _lanes(lo, width):
    lane = lax.broadcasted_iota(jnp.int32, (1, LANE), 1)
    return (lane >= lo) & (lane < lo + width)


def _only(x, mask):
    return jnp.where(mask, x, jnp.zeros_like(x))


def _pair_softmax(s):
    m = jnp.max(s, axis=-1, keepdims=True)
    p = jnp.exp(s - m)
    l = jnp.sum(p, axis=-1, keepdims=True)
    return p, l, m + jnp.log(l)


def gqa_fwd(q, k2, v2, *, scale, lc, name):
    nb, t, qw = q.shape
    npair = qw // LANE
    per_kv = npair // GQA_KV_HEADS
    nctb = lc // TQ

    def body(q_ref, k_ref, v_ref, o_ref, lse_ref):
        i = pl.program_id(2)

        def run(rows):
            kk, vv = k_ref[rows, :], v_ref[rows, :]
            outs = []
            for e in range(2):
                p, l, lse = _pair_softmax(_dot(_only(q_ref[...], _lanes(HEAD_DIM * e, HEAD_DIM)), kk, _NT) * scale)
                outs.append(_dot(p.astype(BF16), vv) / l)
                lse_ref[e] = lse
            o_ref[...] = jnp.where(_lanes(0, HEAD_DIM), outs[0], outs[1]).astype(o_ref.dtype)

        @pl.when(i < nctb)
        def _():
            run(pl.ds(0, lc))

        @pl.when(i >= nctb)
        def _():
            run(pl.ds(0, t))

    qmap = lambda b, p, i: (b, i, p)
    kmap = lambda b, p, i: (b, 0, p // per_kv)
    return pl.pallas_call(
        body, name=name, grid=(nb, npair, t // TQ),
        in_specs=[pl.BlockSpec((None, TQ, LANE), qmap), pl.BlockSpec((None, t, LANE), kmap), pl.BlockSpec((None, t, LANE), kmap)],
        out_specs=[pl.BlockSpec((None, TQ, LANE), qmap), pl.BlockSpec((None, 2, TQ, 1), lambda b, p, i: (b, p, i, 0))],
        out_shape=[jax.ShapeDtypeStruct((nb, t, qw), BF16), jax.ShapeDtypeStruct((nb, 2 * npair, t, 1), F32)],
        compiler_params=_cp("parallel", "parallel", "arbitrary"),
    )(q, k2, v2)


def gqa_bwd(q, k2, v2, lse, do, *, scale, lc, name):
    nb, t, qw = q.shape
    npair = qw // LANE
    per_kv = npair // GQA_KV_HEADS
    nctb = lc // TQ

    def body(q_ref, k_ref, v_ref, lse_ref, do_ref, dq_ref, dk_ref, dv_ref):
        g, i = pl.program_id(2), pl.program_id(3)

        @pl.when((g == 0) & (i == 0))
        def _():
            dk_ref[...] = jnp.zeros_like(dk_ref)
            dv_ref[...] = jnp.zeros_like(dv_ref)

        def run(rows):
            kk, vv = k_ref[rows, :], v_ref[rows, :]
            dqs = []
            for e in range(2):
                mine = _lanes(HEAD_DIM * e, HEAD_DIM)
                qq, dd = _only(q_ref[...], mine), _only(do_ref[...], mine)
                p = jnp.exp(_dot(qq, kk, _NT) * scale - lse_ref[e])
                dp = _dot(dd, vv, _NT)
                delta = jnp.sum(p * dp, axis=-1, keepdims=True)
                ds = (p * (dp - delta) * scale).astype(BF16)
                dqs.append(_dot(ds, kk))
                dk_ref[rows, :] += _dot(ds, qq, _TN)
                dv_ref[rows, :] += _dot(p.astype(BF16), dd, _TN)
            dq_ref[...] = jnp.where(_lanes(0, HEAD_DIM), dqs[0], dqs[1])

        @pl.when(i < nctb)
        def _():
            run(pl.ds(0, lc))

        @pl.when(i >= nctb)
        def _():
            run(pl.ds(0, t))

    qmap = lambda b, j, g, i: (b, i, j * per_kv + g)
    kmap = lambda b, j, g, i: (b, 0, j)
    return pl.pallas_call(
        body, name=name, grid=(nb, GQA_KV_HEADS, per_kv, t // TQ),
        in_specs=[pl.BlockSpec((None, TQ, LANE), qmap), pl.BlockSpec((None, t, LANE), kmap), pl.BlockSpec((None, t, LANE), kmap),
                  pl.BlockSpec((None, 2, TQ, 1), lambda b, j, g, i: (b, j * per_kv + g, i, 0)), pl.BlockSpec((None, TQ, LANE), qmap)],
        out_specs=[pl.BlockSpec((None, TQ, LANE), qmap), pl.BlockSpec((None, t, LANE), kmap), pl.BlockSpec((None, t, LANE), kmap)],
        out_shape=[jax.ShapeDtypeStruct((nb, t, qw), F32), jax.ShapeDtypeStruct(k2.shape, F32), jax.ShapeDtypeStruct(v2.shape, F32)],
        compiler_params=_cp("arbitrary", "arbitrary", "arbitrary", "arbitrary"),
    )(q, k2, v2, lse, do)


def mla_fwd(qn, qr, kn, kr, v, *, scale, lc, name):
    nb, t, w = qn.shape
    npair = w // LANE
    nctb = lc // TQ

    def body(qn_ref, qr_ref, kn_ref, kr_ref, v_ref, o_ref, lse_ref):
        pr, i = pl.program_id(1), pl.program_id(2)

        def run(rows):
            kk, kkr, vv = kn_ref[rows, :], kr_ref[rows, :], v_ref[rows, :]
            outs = []
            for e in range(2):
                s = (_dot(_only(qn_ref[...], _lanes(MLA_NOPE * e, MLA_NOPE)), kk, _NT)
                     + _dot(_only(qr_ref[...], _lanes(MLA_ROPE * (2 * pr + e), MLA_ROPE)), kkr, _NT)) * scale
                p, l, lse = _pair_softmax(s)
                outs.append(_dot(p.astype(BF16), vv) / l)
                lse_ref[e] = lse
            o_ref[...] = jnp.where(_lanes(0, MLA_V), outs[0], outs[1]).astype(o_ref.dtype)

        @pl.when(i < nctb)
        def _():
            run(pl.ds(0, lc))

        @pl.when(i >= nctb)
        def _():
            run(pl.ds(0, t))

    qmap = lambda b, p, i: (b, i, p)
    rmap = lambda b, p, i: (b, i, 0)
    kmap = lambda b, p, i: (b, 0, p)
    return pl.pallas_call(
        body, name=name, grid=(nb, npair, t // TQ),
        in_specs=[pl.BlockSpec((None, TQ, LANE), qmap), pl.BlockSpec((None, TQ, LANE), rmap), pl.BlockSpec((None, t, LANE), kmap),
                  pl.BlockSpec((None, t, LANE), lambda b, p, i: (b, 0, 0)), pl.BlockSpec((None, t, LANE), kmap)],
        out_specs=[pl.BlockSpec((None, TQ, LANE), qmap), pl.BlockSpec((None, 2, TQ, 1), lambda b, p, i: (b, p, i, 0))],
        out_shape=[jax.ShapeDtypeStruct((nb, t, w), BF16), jax.ShapeDtypeStruct((nb, 2 * npair, t, 1), F32)],
        compiler_params=_cp("parallel", "parallel", "arbitrary"),
    )(qn, qr, kn, kr, v)


def mla_bwd(qn, qr, kn, kr, v, lse, do, *, scale, lc, name):
    nb, t, w = qn.shape
    npair = w // LANE
    nctb = lc // TQ

    def body(qn_ref, qr_ref, kn_ref, kr_ref, v_ref, lse_ref, do_ref, dqn_ref, dqr_ref, dkn_ref, dkr_ref, dv_ref):
        pr, i = pl.program_id(1), pl.program_id(2)

        @pl.when(i == 0)
        def _():
            dkn_ref[...] = jnp.zeros_like(dkn_ref)
            dv_ref[...] = jnp.zeros_like(dv_ref)

        @pl.when((i == 0) & (pr == 0))
        def _():
            dkr_ref[...] = jnp.zeros_like(dkr_ref)

        def run(rows):
            kk, kkr, vv = kn_ref[rows, :], kr_ref[rows, :], v_ref[rows, :]
            dqns, dqrs = [], []
            for e in range(2):
                mine, mine_r = _lanes(MLA_NOPE * e, MLA_NOPE), _lanes(MLA_ROPE * (2 * pr + e), MLA_ROPE)
                qq, qqr, dd = _only(qn_ref[...], mine), _only(qr_ref[...], mine_r), _only(do_ref[...], mine)
                p = jnp.exp((_dot(qq, kk, _NT) + _dot(qqr, kkr, _NT)) * scale - lse_ref[e])
                dp = _dot(dd, vv, _NT)
                delta = jnp.sum(p * dp, axis=-1, keepdims=True)
                ds = (p * (dp - delta) * scale).astype(BF16)
                dqns.append(_dot(ds, kk))
                dqrs.append(_only(_dot(ds, kkr), mine_r))
                dkn_ref[rows, :] += _dot(ds, qq, _TN)
                dkr_ref[rows, :] += _dot(ds, qqr, _TN)
                dv_ref[rows, :] += _dot(p.astype(BF16), dd, _TN)
            dqn_ref[...] = jnp.where(_lanes(0, MLA_NOPE), dqns[0], dqns[1])
            dqr_ref[...] = dqrs[0] + dqrs[1]

        @pl.when(i < nctb)
        def _():
            run(pl.ds(0, lc))

        @pl.when(i >= nctb)
        def _():
            run(pl.ds(0, t))

    qmap = lambda b, p, i: (b, i, p)
    rmap = lambda b, p, i: (b, i, 0)
    kmap = lambda b, p, i: (b, 0, p)
    zmap = lambda b, p, i: (b, 0, 0)
    return pl.pallas_call(
        body, name=name, grid=(nb, npair, t // TQ),
        in_specs=[pl.BlockSpec((None, TQ, LANE), qmap), pl.BlockSpec((None, TQ, LANE), rmap), pl.BlockSpec((None, t, LANE), kmap),
                  pl.BlockSpec((None, t, LANE), zmap), pl.BlockSpec((None, t, LANE), kmap),
                  pl.BlockSpec((None, 2, TQ, 1), lambda b, p, i: (b, p, i, 0)), pl.BlockSpec((None, TQ, LANE), qmap)],
        out_specs=[pl.BlockSpec((None, TQ, LANE), qmap), pl.BlockSpec((None, TQ, LANE), qmap), pl.BlockSpec((None, t, LANE), kmap),
                   pl.BlockSpec((None, t, LANE), zmap), pl.BlockSpec((None, t, LANE), kmap)],
        out_shape=[jax.ShapeDtypeStruct((nb, t, w), F32), jax.ShapeDtypeStruct((nb, t, npair * LANE), F32),
                   jax.ShapeDtypeStruct((nb, t, w), F32), jax.ShapeDtypeStruct((nb, t, LANE), F32), jax.ShapeDtypeStruct((nb, t, w), F32)],
        compiler_params=_cp("arbitrary", "arbitrary", "arbitrary"),
    )(qn, qr, kn, kr, v, lse, do)


def _na_window(st, nc, rows):
    r = jnp.maximum(st - nc, 0)
    r0 = jnp.clip(r - NA_ROWS // 2, 0, rows - NA_ROWS)
    return r, r0, r - r0


def na_fwd(q, k, v, bias, *, lc, name):
    nb, t, w = q.shape
    npair = w // LANE
    nc, rows = lc // GRID_W, (t - lc) // GRID_W
    nwin = NA_ROWS * GRID_W
    scale = HEAD_DIM ** -0.5

    def body(q_ref, k_ref, v_ref, bias_ref, o_ref, lse_ref):
        st = pl.program_id(2)
        ctx = pl.ds(0, lc)
        kc, vc = k_ref[ctx, :], v_ref[ctx, :]
        outs = [None, None]

        @pl.when(st < nc)
        def _():
            for e in range(2):
                p, l, lse = _pair_softmax(_dot(_only(q_ref[...], _lanes(HEAD_DIM * e, HEAD_DIM)), kc, _NT) * scale)
                outs[e] = _dot(p.astype(BF16), vc) / l
                lse_ref[e] = lse
            o_ref[...] = jnp.where(_lanes(0, HEAD_DIM), outs[0], outs[1]).astype(o_ref.dtype)

        @pl.when(st >= nc)
        def _():
            _, r0, _ = _na_window(st, nc, rows)
            win = pl.ds(pl.multiple_of(lc + r0 * GRID_W, GRID_W), nwin)
            kw, vw = k_ref[win, :], v_ref[win, :]
            for e in range(2):
                qq = _only(q_ref[...], _lanes(HEAD_DIM * e, HEAD_DIM))
                s_loc = _dot(qq, kw, _NT) * scale + bias_ref[e]
                s_ctx = _dot(qq, kc, _NT) * scale
                m = jnp.maximum(jnp.max(s_loc, axis=-1, keepdims=True), jnp.max(s_ctx, axis=-1, keepdims=True))
                p_loc, p_ctx = jnp.exp(s_loc - m), jnp.exp(s_ctx - m)
                l = jnp.sum(p_loc, axis=-1, keepdims=True) + jnp.sum(p_ctx, axis=-1, keepdims=True)
                outs[e] = (_dot(p_loc.astype(BF16), vw) + _dot(p_ctx.astype(BF16), vc)) / l
                lse_ref[e] = m + jnp.log(l)
            o_ref[...] = jnp.where(_lanes(0, HEAD_DIM), outs[0], outs[1]).astype(o_ref.dtype)

    qmap = lambda p, b, st: (b, st, p)
    kmap = lambda p, b, st: (b, 0, p)
    return pl.pallas_call(
        body, name=name, grid=(npair, nb, nc + rows),
        in_specs=[pl.BlockSpec((None, GRID_W, LANE), qmap), pl.BlockSpec((None, t, LANE), kmap), pl.BlockSpec((None, t, LANE), kmap),
                  pl.BlockSpec((2, None, GRID_W, nwin), lambda p, b, st: (p, _na_window(st, nc, rows)[2], 0, 0))],
        out_specs=[pl.BlockSpec((None, GRID_W, LANE), qmap), pl.BlockSpec((None, 2, GRID_W, 1), lambda p, b, st: (b, p, st, 0))],
        out_shape=[jax.ShapeDtypeStruct((nb, t, w), BF16), jax.ShapeDtypeStruct((nb, 2 * npair, t, 1), F32)],
        compiler_params=_cp("parallel", "parallel", "arbitrary"),
    )(q, k, v, bias)


def na_bwd(q, k, v, bias, lse, do, *, lc, name):
    nb, t, w = q.shape
    npair = w // LANE
    nc, rows = lc // GRID_W, (t - lc) // GRID_W
    nwin = NA_ROWS * GRID_W
    scale = HEAD_DIM ** -0.5

    def body(q_ref, k_ref, v_ref, bias_ref, lse_ref, do_ref, dq_ref, dk_ref, dv_ref, db_ref):
        b, st = pl.program_id(1), pl.program_id(2)

        @pl.when(st == 0)
        def _():
            dk_ref[...] = jnp.zeros_like(dk_ref)
            dv_ref[...] = jnp.zeros_like(dv_ref)

        @pl.when((st == 0) & (b == 0))
        def _():
            db_ref[...] = jnp.zeros_like(db_ref)

        ctx = pl.ds(0, lc)
        kc, vc = k_ref[ctx, :], v_ref[ctx, :]
        dqs = [None, None]

        @pl.when(st < nc)
        def _():
            for e in range(2):
                mine = _lanes(HEAD_DIM * e, HEAD_DIM)
                qq, dd = _only(q_ref[...], mine), _only(do_ref[...], mine)
                p = jnp.exp(_dot(qq, kc, _NT) * scale - lse_ref[e])
                dp = _dot(dd, vc, _NT)
                delta = jnp.sum(p * dp, axis=-1, keepdims=True)
                ds = (p * (dp - delta) * scale).astype(BF16)
                dqs[e] = _dot(ds, kc)
                dk_ref[ctx, :] += _dot(ds, qq, _TN)
                dv_ref[ctx, :] += _dot(p.astype(BF16), dd, _TN)
            dq_ref[...] = jnp.where(_lanes(0, HEAD_DIM), dqs[0], dqs[1])

        @pl.when(st >= nc)
        def _():
            _, r0, case = _na_window(st, nc, rows)
            win = pl.ds(pl.multiple_of(lc + r0 * GRID_W, GRID_W), nwin)
            kw, vw = k_ref[win, :], v_ref[win, :]
            for e in range(2):
                mine = _lanes(HEAD_DIM * e, HEAD_DIM)
                qq, dd = _only(q_ref[...], mine), _only(do_ref[...], mine)
                p_loc = jnp.exp(_dot(qq, kw, _NT) * scale + bias_ref[e] - lse_ref[e])
                p_ctx = jnp.exp(_dot(qq, kc, _NT) * scale - lse_ref[e])
                dp_loc, dp_ctx = _dot(dd, vw, _NT), _dot(dd, vc, _NT)
                delta = jnp.sum(p_loc * dp_loc, axis=-1, keepdims=True) + jnp.sum(p_ctx * dp_ctx, axis=-1, keepdims=True)
                ds_loc = p_loc * (dp_loc - delta)
                db_ref[e, case] += ds_loc
                ds_loc = (ds_loc * scale).astype(BF16)
                ds_ctx = (p_ctx * (dp_ctx - delta) * scale).astype(BF16)
                dqs[e] = _dot(ds_loc, kw) + _dot(ds_ctx, kc)
                dk_ref[win, :] += _dot(ds_loc, qq, _TN)
                dk_ref[ctx, :] += _dot(ds_ctx, qq, _TN)
                dv_ref[win, :] += _dot(p_loc.astype(BF16), dd, _TN)
                dv_ref[ctx, :] += _dot(p_ctx.astype(BF16), dd, _TN)
            dq_ref[...] = jnp.where(_lanes(0, HEAD_DIM), dqs[0], dqs[1])

    qmap = lambda p, b, st: (b, st, p)
    kmap = lambda p, b, st: (b, 0, p)
    return pl.pallas_call(
        body, name=name, grid=(npair, nb, nc + rows),
        in_specs=[pl.BlockSpec((None, GRID_W, LANE), qmap), pl.BlockSpec((None, t, LANE), kmap), pl.BlockSpec((None, t, LANE), kmap),
                  pl.BlockSpec((2, None, GRID_W, nwin), lambda p, b, st: (p, _na_window(st, nc, rows)[2], 0, 0)),
                  pl.BlockSpec((None, 2, GRID_W, 1), lambda p, b, st: (b, p, st, 0)), pl.BlockSpec((None, GRID_W, LANE), qmap)],
        out_specs=[pl.BlockSpec((None, GRID_W, LANE), qmap), pl.BlockSpec((None, t, LANE), kmap), pl.BlockSpec((None, t, LANE), kmap),
                   pl.BlockSpec((2, NA_ROWS, GRID_W, nwin), lambda p, b, st: (p, 0, 0, 0))],
        out_shape=[jax.ShapeDtypeStruct((nb, t, w), F32), jax.ShapeDtypeStruct((nb, t, w), F32), jax.ShapeDtypeStruct((nb, t, w), F32),
                   jax.ShapeDtypeStruct((2 * npair, NA_ROWS, GRID_W, nwin), F32)],
        compiler_params=_cp("arbitrary", "arbitrary", "arbitrary"),
    )(q, k, v, bias, lse, do)


def _na_tables():
    cols = np.arange(GRID_W)
    c0 = np.clip(cols - NA_COLS // 2, 0, GRID_W - NA_COLS)
    col_in = (cols[None, :] >= c0[:, None]) & (cols[None, :] < c0[:, None] + NA_COLS)
    dc = np.clip(cols[None, :] - cols[:, None] + NA_COLS - 1, 0, 2 * NA_COLS - 2)
    dr = np.arange(NA_ROWS)[None, :] + (NA_ROWS - 1) - np.arange(NA_ROWS)[:, None]
    return col_in, dc, dr


def _na_onehots():
    col_in, dc, dr = _na_tables()
    e1 = np.zeros((GRID_W, GRID_W, LANE), np.float32)
    qi, ki = np.nonzero(col_in)
    e1[qi, ki, dc[qi, ki]] = 1.0
    e2 = np.zeros((2 * NA_ROWS, NA_ROWS, NA_ROWS), np.float32)
    ci, ji = np.meshgrid(np.arange(NA_ROWS), np.arange(NA_ROWS), indexing='ij')
    e2[dr[ci, ji], ci, ji] = 1.0
    return jnp.asarray(e1.reshape(GRID_W * GRID_W, LANE)), jnp.asarray(e2.reshape(2 * NA_ROWS, NA_ROWS * NA_ROWS)), col_in


def na_expand_bias(rel_bias, name):
    e1, e2, col_in = _na_onehots()
    nh = rel_bias.shape[0]
    nrow = NA_ROWS * NA_ROWS
    rel = jnp.pad(rel_bias, ((0, 0), (0, 1), (0, LANE - rel_bias.shape[2])))
    rel = rel.transpose(1, 0, 2).reshape(2 * NA_ROWS, nh * LANE)
    y = mm(e2, rel, ta=True, name=name + "_rows", precise=True)
    y = y.reshape(nrow, nh, LANE).transpose(1, 0, 2).reshape(nh * nrow, LANE)
    g = mm(y, e1, tb=True, name=name + "_cols", precise=True)
    g = g.reshape(nh, NA_ROWS, NA_ROWS, GRID_W, GRID_W).transpose(0, 1, 3, 2, 4)
    g = jnp.where(col_in[None, None, :, None, :], g, NEG_BIG)
    return g.reshape(nh, NA_ROWS, GRID_W, NA_ROWS * GRID_W)


def na_reduce_bias(dexp, name):
    e1, e2, _ = _na_onehots()
    nh = dexp.shape[0]
    x = dexp.reshape(nh, NA_ROWS, GRID_W, NA_ROWS, GRID_W).transpose(0, 1, 3, 2, 4).reshape(nh * NA_ROWS * NA_ROWS, GRID_W * GRID_W)
    y = mm(x, e1, name=name + "_cols", precise=True)
    y = y.reshape(nh, NA_ROWS * NA_ROWS, LANE).transpose(1, 0, 2).reshape(NA_ROWS * NA_ROWS, nh * LANE)
    z = mm(e2, y, name=name + "_rows", precise=True)
    return z.reshape(2 * NA_ROWS, nh, LANE).transpose(1, 0, 2)[:, :2 * NA_ROWS - 1, :2 * NA_COLS - 1]


def _rot_matrix(width, d_rot):
    f = d_rot // 4
    r = np.zeros((width, width), np.float32)
    for base in range(0, width, d_rot // 2):
        for j in range(f):
            r[base + f + j, base + j] = -1.0
            r[base + j, base + f + j] = 1.0
    return r


def _rope_tables(s_len, lc, d_rot, reps):
    half = d_rot // 2
    freqs = ROPE_THETA ** (-jnp.arange(0, half, 2, dtype=F32) / half)
    tpos = jnp.arange(s_len)
    row = (tpos // GRID_W).astype(F32)[:, None] * freqs
    col = (tpos % GRID_W).astype(F32)[:, None] * freqs
    ang = jnp.concatenate([row, row, col, col], axis=-1)
    cos = jnp.concatenate([jnp.ones((lc, d_rot), F32), jnp.cos(ang)], axis=0)
    sin = jnp.concatenate([jnp.zeros((lc, d_rot), F32), jnp.sin(ang)], axis=0)
    return jnp.tile(cos, (1, reps)), jnp.tile(sin, (1, reps))


def _post_consts():
    s_b = np.kron(np.eye(GQA_HEADS, dtype=np.float32), np.full((HEAD_DIM, HEAD_DIM), 1.0 / HEAD_DIM, np.float32))
    t_b = np.tile(np.eye(HEAD_DIM, dtype=np.float32), (1, GQA_HEADS))
    r_b = _rot_matrix(GQ_W, HEAD_DIM)
    r_m = _rot_matrix(LANE, MLA_ROPE)
    rep = np.zeros((LANE, LANE), np.float32)
    for h in range(MLA_HEADS):
        rep[np.arange(MLA_ROPE), h * MLA_ROPE + np.arange(MLA_ROPE)] = 1.0
    dup = np.zeros((GK_W, 2 * GK_W), np.float32)
    for j in range(GQA_KV_HEADS):
        for e in range(2):
            dup[HEAD_DIM * j + np.arange(HEAD_DIM), 2 * HEAD_DIM * j + HEAD_DIM * e + np.arange(HEAD_DIM)] = 1.0
    return tuple(jnp.asarray(a) for a in (s_b, r_b, t_b, r_m, rep, dup))


def _heads_to_parts(w, first):
    r = w.shape[0]
    w3 = w.reshape(r, MLA_HEADS, -1)
    return jnp.concatenate([w3[:, :, :first].reshape(r, -1), w3[:, :, first:].reshape(r, -1)], axis=1)


def _parts_to_heads(w, first):
    r = w.shape[0]
    nf = MLA_HEADS * first
    return jnp.concatenate([w[:, :nf].reshape(r, MLA_HEADS, first), w[:, nf:].reshape(r, MLA_HEADS, -1)], axis=2).reshape(r, -1)


def _place():
    return lax.axis_index("x"), lax.axis_index("y"), lax.axis_index("c")


def all_gather(v, *, name, with_c):
    flips = [(dx, dy, dc) for dx in (0, 1) for dy in (0, 1) for dc in ((0, 1) if with_c else (0,))][1:]
    n = len(flips) + 1

    def body(v_ref, out_ref, send_sems, recv_sems, local_sem):
        mx, my, mc = _place()

        def slot(px, py, pc):
            return 4 * px + 2 * py + pc if with_c else 2 * px + py

        mine = pltpu.make_async_copy(v_ref, out_ref.at[slot(mx, my, mc)], local_sem)
        mine.start()
        sends = []
        for j, (dx, dy, dc) in enumerate(flips):
            peer = (mx ^ dx, my ^ dy, mc ^ dc)
            cp = pltpu.make_async_remote_copy(src_ref=v_ref, dst_ref=out_ref.at[slot(mx, my, mc)], send_sem=send_sems.at[j],
                                              recv_sem=recv_sems.at[j], device_id=peer, device_id_type=MESH)
            cp.start()
            sends.append(cp)
        for j, (dx, dy, dc) in enumerate(flips):
            peer = (mx ^ dx, my ^ dy, mc ^ dc)
            pltpu.make_async_remote_copy(src_ref=v_ref, dst_ref=out_ref.at[slot(*peer)], send_sem=send_sems.at[j],
                                         recv_sem=recv_sems.at[j], device_id=peer, device_id_type=MESH).wait_recv()
        for cp in sends:
            cp.wait_send()
        mine.wait()

    return pl.pallas_call(
        body, name=name, in_specs=[ANY], out_specs=ANY, out_shape=jax.ShapeDtypeStruct((n,) + v.shape, v.dtype),
        scratch_shapes=[pltpu.SemaphoreType.DMA((n - 1,)), pltpu.SemaphoreType.DMA((n - 1,)), pltpu.SemaphoreType.DMA(())],
    )(v)


def gather_shards(v, *, name):
    _, h, w = v.shape
    flips = [(1, 0), (0, 1), (1, 1)]

    def body(v_ref, out_ref, send_sems, recv_sems):
        mx, my, mc = _place()
        me = 2 * mx + my
        sib = (mx, my, 1 - mc)

        def copy(k, src, dst, to):
            return pltpu.make_async_remote_copy(src_ref=src, dst_ref=dst, send_sem=send_sems.at[k], recv_sem=recv_sems.at[k],
                                                device_id=to, device_id_type=MESH)

        first = [copy(j, v_ref.at[mc], out_ref.at[me, mc], (mx ^ dx, my ^ dy, mc)) for j, (dx, dy) in enumerate(flips)]
        for cp in first:
            cp.start()
        passed = []
        for j, (dx, dy) in enumerate(flips):
            theirs = out_ref.at[2 * (mx ^ dx) + (my ^ dy), mc]
            copy(j, v_ref.at[mc], theirs, (mx ^ dx, my ^ dy, mc)).wait_recv()
            fw = copy(3 + j, theirs, theirs, sib)
            fw.start()
            passed.append(fw)
        for j, (dx, dy) in enumerate(flips):
            other = out_ref.at[2 * (mx ^ dx) + (my ^ dy), 1 - mc]
            copy(3 + j, other, other, sib).wait_recv()
        for cp in first + passed:
            cp.wait_send()

    out = pl.pallas_call(
        body, name=name, in_specs=[ANY], out_specs=ANY, out_shape=jax.ShapeDtypeStruct((4, 2, h, w), v.dtype),
        scratch_shapes=[pltpu.SemaphoreType.DMA((6,)), pltpu.SemaphoreType.DMA((6,))],
    )(v)
    mx, my, _ = _place()
    return lax.dynamic_update_slice(out, v[None], (2 * mx + my, 0, 0, 0))


def pair_exchange_halves(g, *, name):
    n, _, h, w = g.shape

    def body(g_ref, out_ref, send_sems, recv_sems):
        mx, my, mc = _place()
        sib = (mx, my, 1 - mc)
        cps = [pltpu.make_async_remote_copy(src_ref=g_ref.at[s, 1 - mc], dst_ref=out_ref.at[s], send_sem=send_sems.at[s],
                                            recv_sem=recv_sems.at[s], device_id=sib, device_id_type=MESH) for s in range(n)]
        for cp in cps:
            cp.start()
        for cp in cps:
            cp.wait_recv()
        for cp in cps:
            cp.wait_send()

    return pl.pallas_call(
        body, name=name, in_specs=[ANY], out_specs=ANY, out_shape=jax.ShapeDtypeStruct((n, h, w), g.dtype),
        scratch_shapes=[pltpu.SemaphoreType.DMA((n,)), pltpu.SemaphoreType.DMA((n,))],
    )(g)


def all_to_all_xy(v, *, name):
    def body(v_ref, out_ref, send_sems, recv_sems):
        mx, my, mc = _place()
        me = 2 * mx + my
        flips = [(1, 0), (0, 1), (1, 1)]
        sends = []
        for j, (dx, dy) in enumerate(flips):
            px, py = mx ^ dx, my ^ dy
            cp = pltpu.make_async_remote_copy(src_ref=v_ref.at[2 * px + py], dst_ref=out_ref.at[me], send_sem=send_sems.at[j],
                                              recv_sem=recv_sems.at[j], device_id=(px, py, mc), device_id_type=MESH)
            cp.start()
            sends.append(cp)
        for j, (dx, dy) in enumerate(flips):
            px, py = mx ^ dx, my ^ dy
            pltpu.make_async_remote_copy(src_ref=v_ref.at[me], dst_ref=out_ref.at[2 * px + py], send_sem=send_sems.at[j],
                                         recv_sem=recv_sems.at[j], device_id=(px, py, mc), device_id_type=MESH).wait_recv()
        for cp in sends:
            cp.wait_send()

    out = pl.pallas_call(
        body, name=name, in_specs=[ANY], out_specs=ANY, out_shape=jax.ShapeDtypeStruct(v.shape, v.dtype),
        scratch_shapes=[pltpu.SemaphoreType.DMA((3,)), pltpu.SemaphoreType.DMA((3,))],
    )(v)
    mx, my, _ = _place()
    me = 2 * mx + my
    return lax.dynamic_update_slice(out, lax.dynamic_slice_in_dim(v, me, 1, axis=0), (me, 0, 0))


def pair_all_gather(v, *, name):
    def body(v_ref, out_ref, send_sem, recv_sem):
        mx, my, mc = _place()
        cp = pltpu.make_async_remote_copy(src_ref=v_ref, dst_ref=out_ref.at[mc], send_sem=send_sem, recv_sem=recv_sem,
                                          device_id=(mx, my, 1 - mc), device_id_type=MESH)
        cp.start()
        pltpu.make_async_remote_copy(src_ref=v_ref, dst_ref=out_ref.at[1 - mc], send_sem=send_sem, recv_sem=recv_sem,
                                     device_id=(mx, my, 1 - mc), device_id_type=MESH).wait_recv()
        cp.wait_send()

    out = pl.pallas_call(
        body, name=name, in_specs=[ANY], out_specs=ANY, out_shape=jax.ShapeDtypeStruct((2,) + v.shape, v.dtype),
        scratch_shapes=[pltpu.SemaphoreType.DMA(()), pltpu.SemaphoreType.DMA(())],
    )(v)
    return lax.dynamic_update_slice(out, v[None], (_place()[2], 0, 0))


def add_kept_half(g, r, c_idx, *, name, out_dtype):
    n, _, h, w = g.shape
    th = _row_tile(h)

    def body(c_ref, g_ref, r_ref, o_ref):
        o_ref[...] = (g_ref[...] + r_ref[...]).astype(o_ref.dtype)

    return pl.pallas_call(
        body, name=name,
        grid_spec=pltpu.PrefetchScalarGridSpec(
            num_scalar_prefetch=1, grid=(n, h // th),
            in_specs=[pl.BlockSpec((None, None, th, w), lambda s, i, c_ref: (s, c_ref[0], i, 0)),
                      pl.BlockSpec((None, th, w), lambda s, i, c_ref: (s, i, 0))],
            out_specs=pl.BlockSpec((None, th, w), lambda s, i, c_ref: (s, i, 0))),
        out_shape=jax.ShapeDtypeStruct((n, h, w), out_dtype), compiler_params=_cp("parallel", "parallel"),
    )(c_idx, g, r)


def sum_slots(v, *, name):
    n, rows, w = v.shape
    tr = _row_tile(rows, 256)

    def body(v_ref, o_ref):
        acc = v_ref[0].astype(F32)
        for s in range(1, n):
            acc = acc + v_ref[s].astype(F32)
        o_ref[...] = acc

    return pl.pallas_call(body, name=name, grid=(rows // tr,), in_specs=[pl.BlockSpec((n, tr, w), lambda i: (0, i, 0))],
                          out_specs=pl.BlockSpec((tr, w), lambda i: (i, 0)), out_shape=jax.ShapeDtypeStruct((rows, w), F32),
                          compiler_params=_cp("parallel"))(v)


def ada_fwd(c_rows, w_ada, b_shard, *, name):
    nl, d, ncol = w_ada.shape
    rows = c_rows.shape[0]
    tn = _tile(ncol, (768, 512, 256, 128))

    def body(c_ref, w_ref, b_ref, o_ref):
        o_ref[...] = jnp.dot(jax.nn.silu(c_ref[...]), w_ref[...], precision=HI, preferred_element_type=F32) + b_ref[...]

    return pl.pallas_call(
        body, name=name, grid=(nl, ncol // tn),
        in_specs=[pl.BlockSpec((rows, d), lambda l, j: (0, 0)), pl.BlockSpec((None, d, tn), lambda l, j: (l, 0, j)),
                  pl.BlockSpec((None, 1, tn), lambda l, j: (l, 0, j))],
        out_specs=pl.BlockSpec((None, rows, tn), lambda l, j: (l, 0, j)),
        out_shape=jax.ShapeDtypeStruct((nl, rows, ncol), F32), compiler_params=_cp("parallel", "parallel"),
    )(c_rows, w_ada, b_shard)


def ada_bwd(c_rows, w_ada, dm_shard, dm_full, n_ex, *, name):
    nl, d, ncol = w_ada.shape
    rows = c_rows.shape[0]
    tn = _tile(ncol, (768, 512, 256, 128))
    nj = ncol // tn

    def body(c_ref, w_ref, dm_ref, dmf_ref, gw_ref, gb_ref, dc_ref, dact_ref):
        l, j = pl.program_id(0), pl.program_id(1)
        act, act_vjp = jax.vjp(jax.nn.silu, c_ref[...])
        gw_ref[...] = lax.dot_general(act, dm_ref[...], _TN, precision=HI, preferred_element_type=F32)
        gb_ref[...] = jnp.sum(dmf_ref[...], axis=0, keepdims=True)
        part = lax.dot_general(dm_ref[...], w_ref[...], _NT, precision=HI, preferred_element_type=F32)

        @pl.when((l == 0) & (j == 0))
        def _():
            dact_ref[...] = part

        @pl.when((l > 0) | (j > 0))
        def _():
            dact_ref[...] += part

        @pl.when((l == nl - 1) & (j == nj - 1))
        def _():
            dc, = act_vjp(dact_ref[...])
            dc_ref[...] = jnp.sum(dc[n_ex:, :], axis=0, keepdims=True)

    return pl.pallas_call(
        body, name=name, grid=(nl, nj),
        in_specs=[pl.BlockSpec((rows, d), lambda l, j: (0, 0)), pl.BlockSpec((None, d, tn), lambda l, j: (l, 0, j)),
                  pl.BlockSpec((None, rows, tn), lambda l, j: (l, 0, j)),
                  pl.BlockSpec((None, rows, dm_full.shape[-1]), lambda l, j: (l, 0, 0))],
        out_specs=[pl.BlockSpec((None, d, tn), lambda l, j: (l, 0, j)),
                   pl.BlockSpec((None, 1, dm_full.shape[-1]), lambda l, j: (l, 0, 0)),
                   pl.BlockSpec((1, d), lambda l, j: (0, 0))],
        out_shape=[jax.ShapeDtypeStruct((nl, d, ncol), F32), jax.ShapeDtypeStruct((nl, 1, dm_full.shape[-1]), F32),
                   jax.ShapeDtypeStruct((1, d), F32)],
        scratch_shapes=[pltpu.VMEM((rows, d), F32)], compiler_params=_cp("arbitrary", "arbitrary"),
    )(c_rows, w_ada, dm_shard, dm_full)


def adamw(w, g, m, v, *, name):
    shape = w.shape
    cols = shape[-1]
    rows = int(np.prod(shape[:-1])) if len(shape) > 1 else 1
    tr = _row_tile(rows, 256)

    def body(w_ref, g_ref, m_ref, v_ref, d_ref, nm_ref, nv_ref):
        gg = g_ref[...]
        nm = ADAM_B1 * m_ref[...] + (1.0 - ADAM_B1) * gg
        nv = ADAM_B2 * v_ref[...] + (1.0 - ADAM_B2) * jnp.square(gg)
        m_hat = nm / (1.0 - ADAM_B1 ** ADAM_STEP)
        v_hat = nv / (1.0 - ADAM_B2 ** ADAM_STEP)
        d_ref[...] = -ADAM_LR * (m_hat / (jnp.sqrt(v_hat) + ADAM_EPS) + ADAM_WD * w_ref[...])
        nm_ref[...] = nm
        nv_ref[...] = nv

    spec = pl.BlockSpec((tr, cols), lambda i: (i, 0))
    out = pl.pallas_call(body, name=name, grid=(rows // tr,), in_specs=[spec] * 4, out_specs=[spec] * 3,
                         out_shape=[jax.ShapeDtypeStruct((rows, cols), F32)] * 3, compiler_params=_cp("parallel"),
                         )(*[a.reshape(rows, cols) for a in (w, g, m, v)])
    return tuple(o.reshape(shape) for o in out)


def _heads(a, nh):
    nb, t, w = a.shape
    return a.reshape(nb, t, nh, w // nh).transpose(0, 2, 1, 3)


def _unheads(a):
    nb, nh, t, d = a.shape
    return a.transpose(0, 2, 1, 3).reshape(nb, t, nh * d)


def local_step(h0, target, mods, lw, small, *, lc):
    nb, t, d = h0.shape
    nt, nct = t // TM, lc // TM
    s_len = t - lc
    nl = len(lw)
    f2 = lw[0]['w_gu1'].shape[1]
    consts = _post_consts()
    cos_b, sin_b = _rope_tables(s_len, lc, HEAD_DIM, GQA_HEADS)
    cos_m, sin_m = _rope_tables(s_len, lc, MLA_ROPE, MLA_HEADS)
    na_scale = HEAD_DIM ** -0.5
    mla_scale = (MLA_NOPE + MLA_ROPE) ** -0.5
    rc = functools.partial(rowcall, nb=nb, nt=nt, nct=nct)
    flat = lambda a: a.reshape(nb * t, a.shape[-1])
    unflat = lambda a: a.reshape(nb, t, a.shape[-1])
    vec = lambda a: a.reshape(1, -1)

    def ffn_fwd(h, g, mod3, w, tag):
        shift, scale, gate = mod3
        n, = rc(tag + "_norm", lambda _, *a: (f_normmod(*a),), [(h, 'tok'), (vec(g), 'full'), (shift, 'mod'), (scale, 'mod')],
                [('tok', d, BF16)])
        gu = unflat(mm(flat(n), w['w_gu'], name=tag + "_up"))
        act, = rc(tag + "_act", lambda _, a: (f_act(a),), [(gu, 'tok')], [('tok', f2 // 2, BF16)])
        y = unflat(mm(flat(act), w['w_d'], name=tag + "_down"))
        h2, = rc(tag + "_res", lambda _, hh, yy, gt: (hh + 0.5 * gt * yy,), [(h, 'tok'), (y, 'tok'), (gate, 'mod')], [('tok', d, F32)])
        return h2, (h, n, gu, act, y)

    def ffn_bwd(dh2, saved, g, mod3, w, tag):
        shift, scale, gate = mod3
        h, n, gu, act, y = saved
        dy, dgate = rc(tag + "_res_bwd", lambda _, dd, yy, gt: (0.5 * gt * dd, jnp.sum(0.5 * yy * dd, axis=0, keepdims=True)),
                       [(dh2, 'tok'), (y, 'tok'), (gate, 'mod')], [('tok', d, BF16), ('mod', d)])
        dw_d = mm(flat(act), flat(dy), ta=True, name=tag + "_down_dw")
        dact = unflat(mm(flat(dy), w['w_d'], tb=True, name=tag + "_down_dx"))

        def act_bwd(_, a, da):
            return jax.vjp(f_act, a)[1](da)

        dgu, = rc(tag + "_act_bwd", act_bwd, [(gu, 'tok'), (dact, 'tok')], [('tok', f2, BF16)])
        dw_gu = mm(flat(n), flat(dgu), ta=True, name=tag + "_up_dw")
        dn = unflat(mm(flat(dgu), w['w_gu'], tb=True, name=tag + "_up_dx"))

        def norm_bwd(_, hh, gg, sh, sc, dnn, dres):
            dh, dg, dsh, dsc = jax.vjp(f_normmod, hh, gg, sh, sc)[1](dnn)
            return dh + dres, dg, dsh, dsc

        dh, dg, dshift, dscale = rc(tag + "_norm_bwd", norm_bwd,
                                    [(h, 'tok'), (vec(g), 'full'), (shift, 'mod'), (scale, 'mod'), (dn, 'tok'), (dh2, 'tok')],
                                    [('tok', d, F32), ('full', (1, d)), ('mod', d), ('mod', d)])
        return dh, dg.reshape(d), (dshift, dscale, dgate), dw_gu, dw_d

    def post_ins(p, sm, w):
        return [(p, ('tokc', MAIN_PAD, 0)), (cos_b, 'pos'), (sin_b, 'pos'), (cos_m, 'pos'), (sin_m, 'pos'),
                (vec(sm['gqa_q_norm']), 'full'), (vec(sm['gqa_k_norm']), 'full'), (vec(sm['mla_q_norm']), 'full'),
                (vec(sm['mla_kv_norm']), 'full'), (w['w_uq'], 'full'), (w['w_ukv'], 'full')] + [(c, 'full') for c in consts]

    def mix_fwd(h, sm, mod3, w, tag):
        shift, scale, gate = mod3
        n, = rc(tag + "_norm", lambda _, *a: (f_normmod(*a),), [(h, 'tok'), (vec(sm['mix_norm']), 'full'), (shift, 'mod'), (scale, 'mod')],
                [('tok', d, BF16)])
        p = unflat(mm(flat(n), w['w_in'], name=tag + "_in"))
        parts = rc(tag + "_post", lambda _, *a: f_post(*a), post_ins(p, sm, w), [('tok', wd, BF16) for wd in POST_WIDTHS])
        aq, ak, av, bq, bk, bv, mqn, mqr, mkn, mkr, mv = parts
        bias = na_expand_bias(sm['na_rel_bias'], tag + "_bias")
        o_a, lse_a = na_fwd(aq, ak, av, bias, lc=lc, name=tag + "_na")
        o_b, lse_b = gqa_fwd(bq, bk, bv, scale=na_scale, lc=lc, name=tag + "_gqa")
        o_m, lse_m = mla_fwd(mqn, mqr, mkn, mkr, mv, scale=mla_scale, lc=lc, name=tag + "_mla")
        fo = [o_a, o_b, o_m]
        ys = [unflat(mm(flat(o), w[k], name=tag + "_br" + k[-1])) for o, k in zip(fo, ('w_a', 'w_b', 'w_c'))]
        gcols = [(p, ('tokc', d, MAIN_PAD // d + j)) for j in range(3)]
        y, = rc(tag + "_merge", lambda _, *a: (f_merge(*a),), gcols + [(v, 'tok') for v in ys], [('tok', d, BF16)])
        z = unflat(mm(flat(y), w['w_o'], name=tag + "_out"))
        h2, = rc(tag + "_res", lambda _, hh, zz, gt: (hh + gt * zz,), [(h, 'tok'), (z, 'tok'), (gate, 'mod')], [('tok', d, F32)])
        saved = (h, n, p, (aq, ak, av, lse_a, bias), (bq, bk, bv, lse_b), (mqn, mqr, mkn, mkr, mv, lse_m), fo, ys, y, z)
        return h2, saved

    def mix_bwd(dh2, saved, sm, mod3, w, tag):
        shift, scale, gate = mod3
        h, n, p, (aq, ak, av, lse_a, bias), (bq, bk, bv, lse_b), (mqn, mqr, mkn, mkr, mv, lse_m), fo, ys, y, z = saved
        dz, dgate = rc(tag + "_res_bwd", lambda _, dd, zz, gt: (gt * dd, jnp.sum(zz * dd, axis=0, keepdims=True)),
                       [(dh2, 'tok'), (z, 'tok'), (gate, 'mod')], [('tok', d, BF16), ('mod', d)])
        dw_o = mm(flat(y), flat(dz), ta=True, name=tag + "_out_dw")
        dy = unflat(mm(flat(dz), w['w_o'], tb=True, name=tag + "_out_dx"))
        gcols = [(p, ('tokc', d, MAIN_PAD // d + j)) for j in range(3)]

        def merge_bwd(_, ga, gb, gm, ya, yb, ym, dyy):
            dga, dgb, dgm, dya, dyb, dym = jax.vjp(f_merge, ga, gb, gm, ya, yb, ym)[1](dyy)
            return dya, dyb, dym, jnp.concatenate([dga, dgb, dgm], axis=-1)

        dya, dyb, dym, dgl = rc(tag + "_merge_bwd", merge_bwd, gcols + [(v, 'tok') for v in ys] + [(dy, 'tok')],
                                [('tok', d, BF16)] * 3 + [('tok', 3 * d, BF16)])
        dws, dos = {}, []
        for o, dyk, k in zip(fo, (dya, dyb, dym), ('w_a', 'w_b', 'w_c')):
            dws[k] = mm(flat(o), flat(dyk), ta=True, name=tag + "_br" + k[-1] + "_dw")
            dos.append(unflat(mm(flat(dyk), w[k], tb=True, out_dtype=BF16, name=tag + "_br" + k[-1] + "_dx")))
        do_a, do_b, do_m = dos
        daq, dak, dav, dbias = na_bwd(aq, ak, av, bias, lse_a, do_a, lc=lc, name=tag + "_na_bwd")
        dbq, dbk, dbv = gqa_bwd(bq, bk, bv, lse_b, do_b, scale=na_scale, lc=lc, name=tag + "_gqa_bwd")
        dmqn, dmqr2, dmkn, dmkr, dmv = mla_bwd(mqn, mqr, mkn, mkr, mv, lse_m, do_m, scale=mla_scale, lc=lc, name=tag + "_mla_bwd")
        d_rel = na_reduce_bias(dbias, tag + "_relb")
        cots = [daq, dak, dav, dbq, dbk, dbv, dmqn, dmqr2, dmkn, dmkr, dmv]
        ins = post_ins(p, sm, w)
        n_in = len(ins)

        def post_bwd(_, *a):
            prim, cot, dgl_v = a[:11], list(a[n_in:n_in + 11]), a[-1]
            cot[7] = cot[7][:, :LANE] + cot[7][:, LANE:]
            outs = jax.vjp(lambda pp, qn, kn, mqn, mkvn, wuq, wukv: f_post(pp, *prim[1:5], qn, kn, mqn, mkvn, wuq, wukv, *a[11:n_in]),
                           prim[0], *prim[5:11])[1](tuple(cot))
            return (jnp.concatenate([outs[0].astype(BF16), dgl_v], axis=-1),) + tuple(outs[1:])

        res = rc(tag + "_post_bwd", post_bwd, ins + [(cv, 'tok') for cv in cots] + [(dgl, 'tok')],
                 [('tok', MAIN_PAD + 3 * d, BF16), ('full', (1, HEAD_DIM)), ('full', (1, HEAD_DIM)), ('full', (1, MLA_Q_RANK)),
                  ('full', (1, MLA_KV_RANK)), ('full', w['w_uq'].shape), ('full', w['w_ukv'].shape)])
        dp, dqn, dkn, dmqn, dmkvn, dw_uq, dw_ukv = res
        dw_in = mm(flat(n), flat(dp), ta=True, name=tag + "_in_dw")
        dn = unflat(mm(flat(dp), w['w_in'], tb=True, name=tag + "_in_dx"))

        def norm_bwd(_, hh, gg, sh, sc, dnn, dres):
            dh, dg, dsh, dsc = jax.vjp(f_normmod, hh, gg, sh, sc)[1](dnn)
            return dh + dres, dg, dsh, dsc

        dh, dg, dshift, dscale = rc(tag + "_norm_bwd", norm_bwd,
                                    [(h, 'tok'), (vec(sm['mix_norm']), 'full'), (shift, 'mod'), (scale, 'mod'), (dn, 'tok'), (dh2, 'tok')],
                                    [('tok', d, F32), ('full', (1, d)), ('mod', d), ('mod', d)])
        dsm = {'mix_norm': dg.reshape(d), 'na_rel_bias': d_rel, 'gqa_q_norm': dqn.reshape(-1), 'gqa_k_norm': dkn.reshape(-1),
               'mla_q_norm': dmqn.reshape(-1), 'mla_kv_norm': dmkvn.reshape(-1)}
        dwl = {'w_in': dw_in, 'w_uq': dw_uq, 'w_ukv': dw_ukv, 'w_o': dw_o, **dws}
        return dh, dsm, (dshift, dscale, dgate), dwl

    h = h0
    saved = []
    for l in range(nl):
        sm = {k: small[k][l] for k in SMALL_LAYER}
        h, s1 = ffn_fwd(h, sm['ffn1_norm'], mods[l][0:3], {'w_gu': lw[l]['w_gu1'], 'w_d': lw[l]['w_d1']}, f"l{l}_ffn1")
        h, s2 = mix_fwd(h, sm, mods[l][3:6], lw[l], f"l{l}_mix")
        h, s3 = ffn_fwd(h, sm['ffn2_norm'], mods[l][6:9], {'w_gu': lw[l]['w_gu2'], 'w_d': lw[l]['w_d2']}, f"l{l}_ffn2")
        saved.append((sm, s1, s2, s3))

    def final(is_ctx, hh, gg, tgt):
        def loss_fn(hv, gv):
            return 0.5 * jnp.sum(jnp.mean(jnp.square(_rms(hv, gv) - tgt), axis=-1))

        keep = jnp.where(is_ctx, 0.0, 1.0)
        loss, (dh, dg) = jax.value_and_grad(loss_fn, argnums=(0, 1))(hh, gg)
        return dh * keep, jnp.full((1, LANE), loss * keep, F32), dg * keep

    dh, loss, dg_final = rc("final_loss", final, [(h, 'tok'), (vec(small['final_norm']), 'full'), (target, 'lat')],
                            [('tok', d, F32), ('full', (1, LANE)), ('full', (1, d))])

    dsmall = {k: [None] * nl for k in SMALL_LAYER}
    dmods, dlw = [None] * nl, [None] * nl
    for l in reversed(range(nl)):
        sm, s1, s2, s3 = saved[l]
        dh, dg3, dm3, dw_gu2, dw_d2 = ffn_bwd(dh, s3, sm['ffn2_norm'], mods[l][6:9], {'w_gu': lw[l]['w_gu2'], 'w_d': lw[l]['w_d2']}, f"l{l}_ffn2")
        dh, dsm, dm2, dwl = mix_bwd(dh, s2, sm, mods[l][3:6], lw[l], f"l{l}_mix")
        dh, dg1, dm1, dw_gu1, dw_d1 = ffn_bwd(dh, s1, sm['ffn1_norm'], mods[l][0:3], {'w_gu': lw[l]['w_gu1'], 'w_d': lw[l]['w_d1']}, f"l{l}_ffn1")
        dmods[l] = list(dm1) + list(dm2) + list(dm3)
        dlw[l] = {'w_gu1': dw_gu1, 'w_d1': dw_d1, 'w_gu2': dw_gu2, 'w_d2': dw_d2, **dwl}
        dsm.update(ffn1_norm=dg1, ffn2_norm=dg3)
        for k in SMALL_LAYER:
            dsmall[k][l] = dsm[k]
    dsmall = {k: jnp.stack(v) for k, v in dsmall.items()}
    dsmall['final_norm'] = dg_final.reshape(d)
    return loss, dh, dmods, dlw, dsmall


def _pack(parts, pad_rows):
    flat, where, off = [], [], 0
    for a in parts:
        n = _ceil_to(a.size, PACK_W)
        flat.append(jnp.pad(a.reshape(-1), (0, n - a.size)))
        where.append((off, n // PACK_W))
        off += n // PACK_W
    total = _ceil_to(off, pad_rows)
    if total > off:
        flat.append(jnp.zeros(((total - off) * PACK_W,), flat[0].dtype))
    return jnp.concatenate(flat).reshape(total, PACK_W), where


def _unpack(buf, where, shape):
    off, rows = where
    return buf[off:off + rows].reshape(-1)[:int(np.prod(shape))].reshape(shape)


def layer_weights(full, l):
    wi = full['w_in'][l]
    d = wi.shape[0]
    return {
        'w_gu1': jnp.concatenate([full['ffn1_w_gate'][l], full['ffn1_w_up'][l]], axis=1), 'w_d1': full['ffn1_w_down'][l],
        'w_gu2': jnp.concatenate([full['ffn2_w_gate'][l], full['ffn2_w_up'][l]], axis=1), 'w_d2': full['ffn2_w_down'][l],
        'w_in': jnp.concatenate([wi[:, :MAIN_W], jnp.zeros((d, MAIN_PAD - MAIN_W), wi.dtype), wi[:, MAIN_W:]], axis=1),
        'w_uq': _heads_to_parts(full['mla_w_uq'][l], MLA_NOPE).astype(F32),
        'w_ukv': _heads_to_parts(full['mla_w_ukv'][l], MLA_NOPE).astype(F32),
        'w_a': full['w_branch_a'][l], 'w_b': full['w_branch_b'][l], 'w_c': full['w_branch_c'][l], 'w_o': full['w_out'][l]}


def layer_grads_by_name(dlw):
    per_name = {k: [] for k, _ in BIG}
    for g in dlw:
        f = g['w_gu1'].shape[1] // 2
        per_name['ffn1_w_gate'].append(g['w_gu1'][:, :f])
        per_name['ffn1_w_up'].append(g['w_gu1'][:, f:])
        per_name['ffn1_w_down'].append(g['w_d1'])
        per_name['ffn2_w_gate'].append(g['w_gu2'][:, :f])
        per_name['ffn2_w_up'].append(g['w_gu2'][:, f:])
        per_name['ffn2_w_down'].append(g['w_d2'])
        per_name['w_in'].append(jnp.concatenate([g['w_in'][:, :MAIN_W], g['w_in'][:, MAIN_PAD:]], axis=1))
        per_name['mla_w_uq'].append(_parts_to_heads(g['w_uq'], MLA_NOPE))
        per_name['mla_w_ukv'].append(_parts_to_heads(g['w_ukv'], MLA_NOPE))
        per_name['w_branch_a'].append(g['w_a'])
        per_name['w_branch_b'].append(g['w_b'])
        per_name['w_branch_c'].append(g['w_c'])
        per_name['w_out'].append(g['w_o'])
    return per_name


def kernel(x, c, ctx, c_ctx, w_ada, b_ada, ffn1_norm, ffn1_w_gate, ffn1_w_up, ffn1_w_down, mix_norm, w_in, na_rel_bias, gqa_q_norm, gqa_k_norm, mla_q_norm, mla_kv_norm, mla_w_uq, mla_w_ukv, w_branch_a, w_branch_b, w_branch_c, w_out, ffn2_norm, ffn2_w_gate, ffn2_w_up, ffn2_w_down, final_norm, loss_target, m_c_ctx, m_w_ada, m_b_ada, m_ffn1_norm, m_ffn1_w_gate, m_ffn1_w_up, m_ffn1_w_down, m_mix_norm, m_w_in, m_na_rel_bias, m_gqa_q_norm, m_gqa_k_norm, m_mla_q_norm, m_mla_kv_norm, m_mla_w_uq, m_mla_w_ukv, m_w_branch_a, m_w_branch_b, m_w_branch_c, m_w_out, m_ffn2_norm, m_ffn2_w_gate, m_ffn2_w_up, m_ffn2_w_down, m_final_norm, v_c_ctx, v_w_ada, v_b_ada, v_ffn1_norm, v_ffn1_w_gate, v_ffn1_w_up, v_ffn1_w_down, v_mix_norm, v_w_in, v_na_rel_bias, v_gqa_q_norm, v_gqa_k_norm, v_mla_q_norm, v_mla_kv_norm, v_mla_w_uq, v_mla_w_ukv, v_w_branch_a, v_w_branch_b, v_w_branch_c, v_w_out, v_ffn2_norm, v_ffn2_w_gate, v_ffn2_w_up, v_ffn2_w_down, v_final_norm):
    args = locals()
    wts = {k: args[k] for k in WEIGHTS}
    mom = {k: args['m_' + k] for k in WEIGHTS}
    var = {k: args['v_' + k] for k in WEIGHTS}
    nb, s_len, d = x.shape
    lc = ctx.shape[1]
    nl = w_ada.shape[0]
    nsh, ndev = 4, 8
    mx, my, mc = _place()
    sidx = 2 * mx + my
    didx = 4 * mx + 2 * my + mc
    assert d % LANE == 0 and MAIN_PAD % d == 0 and lc % TQ == 0 and s_len % TQ == 0 and s_len // GRID_W >= NA_ROWS

    wpack, wwhere = _pack([wts[k].astype(BF16) for k, _ in BIG], 32)
    wall = gather_shards(wpack.reshape(2, -1, PACK_W), name="gather_weights").reshape(nsh, -1, PACK_W)
    full = {}
    for (k, ax), wh in zip(BIG, wwhere):
        shp = wts[k].shape
        parts = jnp.stack([_unpack(wall[s], wh, shp) for s in range(nsh)])
        if ax == 1:
            full[k] = parts.transpose(1, 2, 0, 3).reshape(nl, shp[1], nsh * shp[2])
        else:
            full[k] = parts.transpose(1, 0, 2, 3).reshape(nl, nsh * shp[1], shp[2])
    lw = [layer_weights(full, l) for l in range(nl)]

    n_ex = ndev * nb
    ncol = w_ada.shape[-1]
    c_all = all_gather(c, name="gather_cond", with_c=True).reshape(n_ex, d)
    c_rows = jnp.concatenate([c_all, jnp.broadcast_to(c_ctx[None], (n_ex, d))], axis=0)
    b_shard = lax.dynamic_slice_in_dim(b_ada, sidx * ncol, ncol, axis=1)[:, None, :]
    mod_sh = ada_fwd(c_rows, w_ada, b_shard, name="ada_fwd")
    mod_all = all_gather(mod_sh, name="gather_mod", with_c=False)
    mod_all = mod_all.transpose(1, 2, 0, 3).reshape(nl, 2 * n_ex, nsh * ncol)
    mod_x = lax.dynamic_slice_in_dim(mod_all, didx * nb, nb, axis=1)
    mod_c = jnp.broadcast_to(mod_all[:, n_ex:n_ex + 1], mod_x.shape)
    mods = [[jnp.stack([mod_c[l, :, j * d:(j + 1) * d], mod_x[l, :, j * d:(j + 1) * d]], axis=1)[:, :, None, :]
             for j in range(N_MOD)] for l in range(nl)]

    small = {k: wts[k] for k in SMALL_LAYER + ['final_norm']}
    h0 = jnp.concatenate([ctx, x], axis=1)
    loss_part, dh0, dmods, dlw, dsmall = local_step(h0, loss_target, mods, lw, small, lc=lc)
    grad_x = dh0[:, lc:]

    dmod_mine = jnp.stack([jnp.concatenate([m[:, :, 0, :] for m in dmods[l]], axis=-1) for l in range(nl)])
    small_names = SMALL_LAYER + ['final_norm']
    spack, swhere = _pack([loss_part] + [dsmall[k] for k in small_names] + [dmod_mine], 8)
    sall = all_gather(spack, name="gather_small", with_c=True)
    ssum = sum_slots(sall, name="sum_small")
    loss = _unpack(ssum, swhere[0], (1, LANE))[0, 0]
    grads = {k: _unpack(ssum, wh, wts[k].shape) for k, wh in zip(small_names, swhere[1:])}
    off, rows = swhere[-1]
    dm_all = sall[:, off:off + rows].reshape(ndev, -1)[:, :dmod_mine.size].reshape((ndev,) + dmod_mine.shape)
    dm_all = dm_all.transpose(1, 3, 0, 2, 4).reshape(nl, 2, n_ex, N_MOD * d)
    dm_rows = jnp.concatenate([dm_all[:, 1], dm_all[:, 0]], axis=1)
    dm_shard = lax.dynamic_slice_in_dim(dm_rows, sidx * ncol, ncol, axis=2)
    grads['w_ada'], gb, dc_part = ada_bwd(c_rows, w_ada, dm_shard, dm_rows, n_ex, name="ada_bwd")
    grads['b_ada'] = gb.reshape(b_ada.shape)
    dc_all = all_gather(jnp.pad(dc_part, ((0, 7), (0, 0))), name="gather_dcond", with_c=False)
    grads['c_ctx'] = sum_slots(dc_all, name="sum_dcond")[0]

    per_name = layer_grads_by_name(dlw)
    pieces, gwhere, off = [], [], 0
    for k, ax in BIG:
        shp = wts[k].shape
        for g in per_name[k]:
            if ax == 1:
                pieces.append(g.reshape(shp[1], nsh, shp[2]).transpose(1, 0, 2).reshape(nsh, -1))
            else:
                pieces.append(g.reshape(nsh, -1))
        n = int(np.prod(shp))
        if n % PACK_W:
            pieces.append(jnp.zeros((nsh, _ceil_to(n, PACK_W) - n), F32))
        gwhere.append((off, _ceil_to(n, PACK_W) // PACK_W))
        off += _ceil_to(n, PACK_W) // PACK_W
    if off % 32:
        pieces.append(jnp.zeros((nsh, (_ceil_to(off, 32) - off) * PACK_W), F32))
    half = _ceil_to(off, 32) // 2
    gpack = jnp.concatenate(pieces, axis=1).reshape(nsh, 2, half, PACK_W)
    from_pair = pair_exchange_halves(gpack, name="reduce_pair")
    chip_sum = add_kept_half(gpack, from_pair, jnp.reshape(mc, (1,)).astype(jnp.int32), name="reduce_pair_add",
                             out_dtype=BF16)
    from_xy = all_to_all_xy(chip_sum, name="reduce_xy")
    reduced = sum_slots(from_xy, name="reduce_xy_add")
    gfull = pair_all_gather(reduced, name="reduce_share").reshape(2 * half, PACK_W)
    for (k, _), wh in zip(BIG, gwhere):
        grads[k] = _unpack(gfull, wh, wts[k].shape)

    outs = {k: adamw(wts[k], grads[k], mom[k], var[k], name="adamw_" + k) for k in WEIGHTS}
    return (loss, grad_x, *[grads[k] for k in WEIGHTS], *[outs[k][0] for k in WEIGHTS], *[outs[k][1] for k in WEIGHTS],
            *[outs[k][2] for k in WEIGHTS])
```

```python
import functools

import jax
import jax.numpy as jnp
import numpy as np
from jax import lax
from jax.experimental import pallas as pl
from jax.experimental.pallas import tpu as pltpu

F32 = jnp.float32
BF16 = jnp.bfloat16
HI = lax.Precision.HIGHEST
MESH = pl.DeviceIdType.MESH
ANY = pl.BlockSpec(memory_space=pl.ANY)

V7X_VMEM_BYTES = 64 * 1024 * 1024
VMEM_LIMIT = V7X_VMEM_BYTES - 8 * 1024 * 1024
LANE = 128
PACK_W = 1024

GRID_W = 64
HEAD_DIM = 64
NA_HEADS, NA_ROWS, NA_COLS = 4, 8, 16
GQA_HEADS, GQA_KV_HEADS = 8, 2
MLA_HEADS, MLA_Q_RANK, MLA_KV_RANK, MLA_NOPE, MLA_ROPE, MLA_V = 4, 256, 128, 64, 32, 64
N_MOD = 9
ROPE_THETA = 10000.0
EPS = 1e-6
NEG_BIG = -1e30
NA_W = NA_HEADS * HEAD_DIM
GQ_W = GQA_HEADS * HEAD_DIM
GK_W = GQA_KV_HEADS * HEAD_DIM
MAIN_W = 3 * NA_W + GQ_W + 2 * GK_W + MLA_Q_RANK + MLA_KV_RANK + MLA_ROPE
MAIN_PAD = 2048
TQ = 256
TM = 256

ADAM_LR, ADAM_B1, ADAM_B2, ADAM_EPS, ADAM_WD, ADAM_STEP = 0.001, 0.9, 0.999, 1e-08, 0.01, 10

ARG_NAMES = ['x', 'c', 'ctx', 'c_ctx', 'w_ada', 'b_ada', 'ffn1_norm', 'ffn1_w_gate', 'ffn1_w_up', 'ffn1_w_down', 'mix_norm', 'w_in',
             'na_rel_bias', 'gqa_q_norm', 'gqa_k_norm', 'mla_q_norm', 'mla_kv_norm', 'mla_w_uq', 'mla_w_ukv', 'w_branch_a',
             'w_branch_b', 'w_branch_c', 'w_out', 'ffn2_norm', 'ffn2_w_gate', 'ffn2_w_up', 'ffn2_w_down', 'final_norm']
WEIGHTS = ARG_NAMES[3:]
BIG = [('w_in', 1), ('mla_w_uq', 1), ('mla_w_ukv', 1), ('w_branch_a', 1), ('w_branch_b', 1), ('w_branch_c', 1), ('w_out', 0)]
FFN_NAMES = ['ffn1_w_gate', 'ffn1_w_up', 'ffn1_w_down', 'ffn2_w_gate', 'ffn2_w_up', 'ffn2_w_down']
FFN_TRANSPOSED = [True, True, False, True, True, False]
SMALL_LAYER = ['ffn1_norm', 'mix_norm', 'na_rel_bias', 'gqa_q_norm', 'gqa_k_norm', 'mla_q_norm', 'mla_kv_norm', 'ffn2_norm']


def _cp(*sem):
    return pltpu.CompilerParams(dimension_semantics=sem, vmem_limit_bytes=VMEM_LIMIT)


def _tile(dim, cands):
    for t in cands:
        if dim % t == 0:
            return t
    return dim


def _row_tile(rows, cap=512, mult=16):
    best = None
    for t in range(mult, min(rows, cap) + 1, mult):
        if rows % t == 0:
            best = t
    return best or rows


def _ceil_to(n, m):
    return -(-n // m) * m


def mm(a, b, *, name, ta=False, tb=False, out_dtype=F32, precise=False):
    m, k = (a.shape[1], a.shape[0]) if ta else a.shape
    n = b.shape[0] if tb else b.shape[1]
    tm = _tile(m, (512, 256, 128))
    tn = _tile(n, (1024, 1408, 512, 256, 128))
    tk = _tile(k, (1024, 1408, 512, 256, 128))
    nk = k // tk
    dims = (((0 if ta else 1,), (1 if tb else 0,)), ((), ()))

    def body(a_ref, b_ref, o_ref, *acc):
        if precise:
            part = lax.dot_general(a_ref[...].astype(F32), b_ref[...].astype(F32), dims, precision=HI, preferred_element_type=F32)
        else:
            part = lax.dot_general(a_ref[...].astype(BF16), b_ref[...].astype(BF16), dims, preferred_element_type=F32)
        if nk == 1:
            o_ref[...] = part.astype(o_ref.dtype)
        else:
            acc_ref, = acc
            kk = pl.program_id(2)

            @pl.when(kk == 0)
            def _():
                acc_ref[...] = part

            @pl.when(kk > 0)
            def _():
                acc_ref[...] += part

            @pl.when(kk == nk - 1)
            def _():
                o_ref[...] = acc_ref[...].astype(o_ref.dtype)

    a_spec = pl.BlockSpec((tk, tm), lambda i, j, kk: (kk, i)) if ta else pl.BlockSpec((tm, tk), lambda i, j, kk: (i, kk))
    b_spec = pl.BlockSpec((tn, tk), lambda i, j, kk: (j, kk)) if tb else pl.BlockSpec((tk, tn), lambda i, j, kk: (kk, j))
    return pl.pallas_call(
        body, name=name, grid=(m // tm, n // tn, nk), in_specs=[a_spec, b_spec],
        out_specs=pl.BlockSpec((tm, tn), lambda i, j, kk: (i, j)),
        out_shape=jax.ShapeDtypeStruct((m, n), out_dtype),
        scratch_shapes=[pltpu.VMEM((tm, tn), F32)] if nk > 1 else [],
        compiler_params=_cp("parallel", "parallel", "arbitrary"),
    )(a, b)


FFN_GATE, FFN_UP, FFN_DOWN = 0, 1, 2


def _ffn_wspec(wf, l, which):
    _, nsh, _, cs, d = wf.shape
    return pl.BlockSpec((None, nsh, None, cs, d), lambda *_: (l, 0, which, 0, 0), pipeline_mode=pl.Buffered(1))


def _ffn_group(cs):
    for g in (1, 2, 4):
        if (g * cs) % LANE == 0:
            return g
    raise ValueError(cs)


def ffn_up(n, wf, l, base, *, name):
    m, d = n.shape
    nsh, cs = wf.shape[1], wf.shape[3]
    f = nsh * cs
    grp = _ffn_group(cs)
    tm = _tile(m, (512, 256, 128))

    def body(n_ref, wg_ref, wu_ref, g_ref, u_ref, a_ref):
        nn = n_ref[...]
        for c in range(nsh // grp):
            cols = slice(grp * cs * c, grp * cs * (c + 1))
            g = _dot(nn, wg_ref[grp * c:grp * (c + 1)].reshape(grp * cs, d), _NT)
            u = _dot(nn, wu_ref[grp * c:grp * (c + 1)].reshape(grp * cs, d), _NT)
            g_ref[:, cols] = g.astype(BF16)
            u_ref[:, cols] = u.astype(BF16)
            a_ref[:, cols] = f_act_gu(g, u).astype(BF16)

    ospec = pl.BlockSpec((tm, f), lambda i: (i, 0))
    return pl.pallas_call(
        body, name=name, grid=(m // tm,),
        in_specs=[pl.BlockSpec((tm, d), lambda i: (i, 0)), _ffn_wspec(wf, l, base + FFN_GATE), _ffn_wspec(wf, l, base + FFN_UP)],
        out_specs=[ospec] * 3, out_shape=[jax.ShapeDtypeStruct((m, f), BF16)] * 3, compiler_params=_cp("parallel"),
    )(n, wf, wf)


def ffn_down(act, wf, l, base, *, name):
    m, f = act.shape
    nsh, cs, d = wf.shape[1], wf.shape[3], wf.shape[4]
    tm = _tile(m, (512, 256, 128))

    def body(a_ref, wd_ref, y_ref):
        y_ref[...] = _dot(a_ref[...], wd_ref[...].reshape(f, d))

    return pl.pallas_call(
        body, name=name, grid=(m // tm,),
        in_specs=[pl.BlockSpec((tm, f), lambda i: (i, 0)), _ffn_wspec(wf, l, base + FFN_DOWN)],
        out_specs=pl.BlockSpec((tm, d), lambda i: (i, 0)), out_shape=jax.ShapeDtypeStruct((m, d), F32), compiler_params=_cp("parallel"),
    )(act, wf)


def ffn_down_bwd(dy, g, u, wf, l, base, *, name):
    m, d = dy.shape
    nsh, cs = wf.shape[1], wf.shape[3]
    f = nsh * cs
    grp = _ffn_group(cs)
    tm = _tile(m, (512, 256, 128))

    def body(dy_ref, g_ref, u_ref, wd_ref, dg_ref, du_ref):
        dd = dy_ref[...]
        for c in range(nsh // grp):
            cols = slice(grp * cs * c, grp * cs * (c + 1))
            dact = _dot(dd, wd_ref[grp * c:grp * (c + 1)].reshape(grp * cs, d), _NT)
            dg, du = jax.vjp(f_act_gu, g_ref[:, cols].astype(F32), u_ref[:, cols].astype(F32))[1](dact)
            dg_ref[:, cols] = dg.astype(BF16)
            du_ref[:, cols] = du.astype(BF16)

    fspec = pl.BlockSpec((tm, f), lambda i: (i, 0))
    return pl.pallas_call(
        body, name=name, grid=(m // tm,),
        in_specs=[pl.BlockSpec((tm, d), lambda i: (i, 0)), fspec, fspec, _ffn_wspec(wf, l, base + FFN_DOWN)],
        out_specs=[fspec] * 2, out_shape=[jax.ShapeDtypeStruct((m, f), BF16)] * 2, compiler_params=_cp("parallel"),
    )(dy, g, u, wf)


def ffn_up_bwd(dg, du, wf, l, base, *, name):
    m, f = dg.shape
    nsh, cs, d = wf.shape[1], wf.shape[3], wf.shape[4]
    tm = _tile(m, (512, 256, 128))

    def body(dg_ref, du_ref, wg_ref, wu_ref, dn_ref):
        dn_ref[...] = _dot(dg_ref[...], wg_ref[...].reshape(f, d)) + _dot(du_ref[...], wu_ref[...].reshape(f, d))

    fspec = pl.BlockSpec((tm, f), lambda i: (i, 0))
    return pl.pallas_call(
        body, name=name, grid=(m // tm,),
        in_specs=[fspec, fspec, _ffn_wspec(wf, l, base + FFN_GATE), _ffn_wspec(wf, l, base + FFN_UP)],
        out_specs=pl.BlockSpec((tm, d), lambda i: (i, 0)), out_shape=jax.ShapeDtypeStruct((m, d), F32), compiler_params=_cp("parallel"),
    )(dg, du, wf, wf)


def ffn_dw(a, b, nsh, *, name):
    m, f = a.shape
    d = b.shape[1]
    cs = f // nsh
    grp = _ffn_group(cs)
    tm = _tile(m, (1024, 512, 256, 128))

    def body(a_ref, b_ref, o_ref):
        part = _dot(a_ref[...], b_ref[...], _TN).reshape(grp, cs, d)
        i = pl.program_id(1)

        @pl.when(i == 0)
        def _():
            o_ref[...] = part

        @pl.when(i > 0)
        def _():
            o_ref[...] += part

    return pl.pallas_call(
        body, name=name, grid=(nsh // grp, m // tm),
        in_specs=[pl.BlockSpec((tm, grp * cs), lambda j, i: (i, j)), pl.BlockSpec((tm, d), lambda j, i: (i, 0))],
        out_specs=pl.BlockSpec((grp, cs, d), lambda j, i: (j, 0, 0)), out_shape=jax.ShapeDtypeStruct((nsh, cs, d), F32),
        compiler_params=_cp("parallel", "arbitrary"),
    )(a, b)


def rowcall(name, fn, ins, outs, *, nb, nt, nct):
    in_specs, arrays = [], []
    for arr, kind in ins:
        arrays.append(arr)
        if kind == 'tok':
            in_specs.append(pl.BlockSpec((None, TM, arr.shape[-1]), lambda b, t: (b, t, 0)))
        elif kind == 'lat':
            in_specs.append(pl.BlockSpec((None, TM, arr.shape[-1]), lambda b, t: (b, jnp.maximum(t - nct, 0), 0)))
        elif kind == 'pos':
            in_specs.append(pl.BlockSpec((TM, arr.shape[-1]), lambda b, t: (t, 0)))
        elif kind == 'mod':
            in_specs.append(pl.BlockSpec((None, None, 1, arr.shape[-1]), lambda b, t: (b, jnp.where(t >= nct, 1, 0), 0, 0)))
        elif kind == 'full':
            in_specs.append(pl.BlockSpec(arr.shape, lambda b, t, nd=arr.ndim: (0,) * nd))
        else:
            _, w, j = kind
            in_specs.append(pl.BlockSpec((None, TM, w), lambda b, t, j=j: (b, t, j)))
    out_specs, out_shape = [], []
    for o in outs:
        if o[0] == 'tok':
            out_specs.append(pl.BlockSpec((None, TM, o[1]), lambda b, t: (b, t, 0)))
            out_shape.append(jax.ShapeDtypeStruct((nb, nt * TM, o[1]), o[2]))
        elif o[0] == 'mod':
            out_specs.append(pl.BlockSpec((None, None, 1, o[1]), lambda b, t: (b, jnp.where(t >= nct, 1, 0), 0, 0)))
            out_shape.append(jax.ShapeDtypeStruct((nb, 2, 1, o[1]), F32))
        else:
            out_specs.append(pl.BlockSpec(o[1], lambda b, t, nd=len(o[1]): (0,) * nd))
            out_shape.append(jax.ShapeDtypeStruct(o[1], F32))
    n_in = len(ins)

    def body(*refs):
        b, t = pl.program_id(0), pl.program_id(1)
        res = fn(t < nct, *[r[...] for r in refs[:n_in]])
        for ref, o, val in zip(refs[n_in:], outs, res, strict=True):
            if o[0] == 'tok':
                ref[...] = val.astype(ref.dtype)
                continue
            first = ((t == 0) | (t == nct)) if o[0] == 'mod' else ((b == 0) & (t == 0))

            @pl.when(first)
            def _(ref=ref, val=val):
                ref[...] = val

            @pl.when(jnp.logical_not(first))
            def _(ref=ref, val=val):
                ref[...] += val

    return pl.pallas_call(body, name=name, grid=(nb, nt), in_specs=in_specs, out_specs=out_specs, out_shape=out_shape,
                          compiler_params=_cp("arbitrary", "arbitrary"))(*arrays)


def _rms(x, g):
    return x * lax.rsqrt(jnp.mean(x * x, axis=-1, keepdims=True) + EPS) * g


def f_normmod(h, g, shift, scale):
    return _rms(h, g) * (1.0 + scale) + shift


def f_act_gu(g, u):
    return jax.nn.silu(g) * u


def f_merge(ga, gb, gm, ya, yb, ym):
    return jax.nn.sigmoid(ga) * ya + jax.nn.sigmoid(gb) * yb + jax.nn.sigmoid(gm) * ym


def f_post(p, cb, sb, cm, sm, qn, kn, mqn, mkvn, wuq, wukv, s_b, r_b, t_b, r_m, rep, dup):
    def hnorm(x, g, w):
        ms = jnp.dot(x * x, s_b[:w, :w], precision=HI, preferred_element_type=F32)
        gw = jnp.dot(g, t_b[:, :w], precision=HI, preferred_element_type=F32)
        return x * lax.rsqrt(ms + EPS) * gw

    def rope(x, cos, sin, rot):
        return x * cos + jnp.dot(x, rot, precision=HI, preferred_element_type=F32) * sin

    o = 3 * NA_W
    a_q, a_k, a_v = p[:, 0:NA_W], p[:, NA_W:2 * NA_W], p[:, 2 * NA_W:o]
    b_q = rope(hnorm(p[:, o:o + GQ_W], qn, GQ_W), cb, sb, r_b)
    o += GQ_W
    b_k = rope(hnorm(p[:, o:o + GK_W], kn, GK_W), cb[:, :GK_W], sb[:, :GK_W], r_b[:GK_W, :GK_W])
    b_v = p[:, o + GK_W:o + 2 * GK_W]
    o += 2 * GK_W
    q_lat = jnp.dot(_rms(p[:, o:o + MLA_Q_RANK], mqn).astype(BF16), wuq.astype(BF16), preferred_element_type=F32)
    o += MLA_Q_RANK
    kv_lat = jnp.dot(_rms(p[:, o:o + MLA_KV_RANK], mkvn).astype(BF16), wukv.astype(BF16), preferred_element_type=F32)
    o += MLA_KV_RANK
    nw = MLA_HEADS * MLA_NOPE
    mq_nope, mq_rope = q_lat[:, :nw], rope(q_lat[:, nw:], cm, sm, r_m)
    mk_nope, m_v = kv_lat[:, :nw], kv_lat[:, nw:]
    mk_rope = jnp.dot(rope(p[:, o:o + LANE], cm, sm, r_m), rep, precision=HI, preferred_element_type=F32)
    b_k2 = jnp.dot(b_k, dup, precision=HI, preferred_element_type=F32)
    b_v2 = jnp.dot(b_v, dup, precision=HI, preferred_element_type=F32)
    return (a_q, a_k, a_v, b_q, b_k2, b_v2, mq_nope, mq_rope, mk_nope, mk_rope, m_v)


POST_WIDTHS = (NA_W, NA_W, NA_W, GQ_W, 2 * GK_W, 2 * GK_W, MLA_HEADS * MLA_NOPE, MLA_HEADS * MLA_ROPE, MLA_HEADS * MLA_NOPE,
               MLA_HEADS * MLA_ROPE, MLA_HEADS * MLA_V)


_NT = (((1,), (1,)), ((), ()))
_TN = (((0,), (0,)), ((), ()))


def _dot(a, b, dims=None):
    if dims is None:
        return jnp.dot(a, b, preferred_element_type=F32)
    return lax.dot_general(a, b, dims, preferred_element_type=F32)


def _lanes(lo, width):
    lane = lax.broadcasted_iota(jnp.int32, (1, LANE), 1)
    return (lane >= lo) & (lane < lo + width)


def _only(x, mask):
    return jnp.where(mask, x, jnp.zeros_like(x))


def _pair_softmax(s):
    m = jnp.max(s, axis=-1, keepdims=True)
    p = jnp.exp(s - m)
    l = jnp.sum(p, axis=-1, keepdims=True)
    return p, l, m + jnp.log(l)


def gqa_fwd(q, k2, v2, *, scale, lc, name):
    nb, t, qw = q.shape
    npair = qw // LANE
    per_kv = npair // GQA_KV_HEADS
    nctb = lc // TQ

    def body(q_ref, k_ref, v_ref, o_ref, lse_ref):
        i = pl.program_id(2)

        def run(rows):
            kk, vv = k_ref[rows, :], v_ref[rows, :]
            outs = []
            for e in range(2):
                p, l, lse = _pair_softmax(_dot(_only(q_ref[...], _lanes(HEAD_DIM * e, HEAD_DIM)), kk, _NT) * scale)
                outs.append(_dot(p.astype(BF16), vv) / l)
                lse_ref[e] = lse
            o_ref[...] = jnp.where(_lanes(0, HEAD_DIM), outs[0], outs[1]).astype(o_ref.dtype)

        @pl.when(i < nctb)
        def _():
            run(pl.ds(0, lc))

        @pl.when(i >= nctb)
        def _():
            run(pl.ds(0, t))

    qmap = lambda b, p, i: (b, i, p)
    kmap = lambda b, p, i: (b, 0, p // per_kv)
    return pl.pallas_call(
        body, name=name, grid=(nb, npair, t // TQ),
        in_specs=[pl.BlockSpec((None, TQ, LANE), qmap), pl.BlockSpec((None, t, LANE), kmap), pl.BlockSpec((None, t, LANE), kmap)],
        out_specs=[pl.BlockSpec((None, TQ, LANE), qmap), pl.BlockSpec((None, 2, TQ, 1), lambda b, p, i: (b, p, i, 0))],
        out_shape=[jax.ShapeDtypeStruct((nb, t, qw), BF16), jax.ShapeDtypeStruct((nb, 2 * npair, t, 1), F32)],
        compiler_params=_cp("parallel", "parallel", "arbitrary"),
    )(q, k2, v2)


def gqa_bwd(q, k2, v2, lse, do, *, scale, lc, name):
    nb, t, qw = q.shape
    npair = qw // LANE
    per_kv = npair // GQA_KV_HEADS
    nctb = lc // TQ

    def body(q_ref, k_ref, v_ref, lse_ref, do_ref, dq_ref, dk_ref, dv_ref):
        g, i = pl.program_id(2), pl.program_id(3)

        @pl.when((g == 0) & (i == 0))
        def _():
            dk_ref[...] = jnp.zeros_like(dk_ref)
            dv_ref[...] = jnp.zeros_like(dv_ref)

        def run(rows):
            kk, vv = k_ref[rows, :], v_ref[rows, :]
            dqs = []
            for e in range(2):
                mine = _lanes(HEAD_DIM * e, HEAD_DIM)
                qq, dd = _only(q_ref[...], mine), _only(do_ref[...], mine)
                p = jnp.exp(_dot(qq, kk, _NT) * scale - lse_ref[e])
                dp = _dot(dd, vv, _NT)
                delta = jnp.sum(p * dp, axis=-1, keepdims=True)
                ds = (p * (dp - delta) * scale).astype(BF16)
                dqs.append(_dot(ds, kk))
                dk_ref[rows, :] += _dot(ds, qq, _TN)
                dv_ref[rows, :] += _dot(p.astype(BF16), dd, _TN)
            dq_ref[...] = jnp.where(_lanes(0, HEAD_DIM), dqs[0], dqs[1])

        @pl.when(i < nctb)
        def _():
            run(pl.ds(0, lc))

        @pl.when(i >= nctb)
        def _():
            run(pl.ds(0, t))

    qmap = lambda b, j, g, i: (b, i, j * per_kv + g)
    kmap = lambda b, j, g, i: (b, 0, j)
    return pl.pallas_call(
        body, name=name, grid=(nb, GQA_KV_HEADS, per_kv, t // TQ),
        in_specs=[pl.BlockSpec((None, TQ, LANE), qmap), pl.BlockSpec((None, t, LANE), kmap), pl.BlockSpec((None, t, LANE), kmap),
                  pl.BlockSpec((None, 2, TQ, 1), lambda b, j, g, i: (b, j * per_kv + g, i, 0)), pl.BlockSpec((None, TQ, LANE), qmap)],
        out_specs=[pl.BlockSpec((None, TQ, LANE), qmap), pl.BlockSpec((None, t, LANE), kmap), pl.BlockSpec((None, t, LANE), kmap)],
        out_shape=[jax.ShapeDtypeStruct((nb, t, qw), F32), jax.ShapeDtypeStruct(k2.shape, F32), jax.ShapeDtypeStruct(v2.shape, F32)],
        compiler_params=_cp("arbitrary", "arbitrary", "arbitrary", "arbitrary"),
    )(q, k2, v2, lse, do)


def mla_fwd(qn, qr, kn, kr, v, *, scale, lc, name):
    nb, t, w = qn.shape
    npair = w // LANE
    nctb = lc // TQ

    def body(qn_ref, qr_ref, kn_ref, kr_ref, v_ref, o_ref, lse_ref):
        pr, i = pl.program_id(1), pl.program_id(2)

        def run(rows):
            kk, kkr, vv = kn_ref[rows, :], kr_ref[rows, :], v_ref[rows, :]
            outs = []
            for e in range(2):
                s = (_dot(_only(qn_ref[...], _lanes(MLA_NOPE * e, MLA_NOPE)), kk, _NT)
                     + _dot(_only(qr_ref[...], _lanes(MLA_ROPE * (2 * pr + e), MLA_ROPE)), kkr, _NT)) * scale
                p, l, lse = _pair_softmax(s)
                outs.append(_dot(p.astype(BF16), vv) / l)
                lse_ref[e] = lse
            o_ref[...] = jnp.where(_lanes(0, MLA_V), outs[0], outs[1]).astype(o_ref.dtype)

        @pl.when(i < nctb)
        def _():
            run(pl.ds(0, lc))

        @pl.when(i >= nctb)
        def _():
            run(pl.ds(0, t))

    qmap = lambda b, p, i: (b, i, p)
    rmap = lambda b, p, i: (b, i, 0)
    kmap = lambda b, p, i: (b, 0, p)
    return pl.pallas_call(
        body, name=name, grid=(nb, npair, t // TQ),
        in_specs=[pl.BlockSpec((None, TQ, LANE), qmap), pl.BlockSpec((None, TQ, LANE), rmap), pl.BlockSpec((None, t, LANE), kmap),
                  pl.BlockSpec((None, t, LANE), lambda b, p, i: (b, 0, 0)), pl.BlockSpec((None, t, LANE), kmap)],
        out_specs=[pl.BlockSpec((None, TQ, LANE), qmap), pl.BlockSpec((None, 2, TQ, 1), lambda b, p, i: (b, p, i, 0))],
        out_shape=[jax.ShapeDtypeStruct((nb, t, w), BF16), jax.ShapeDtypeStruct((nb, 2 * npair, t, 1), F32)],
        compiler_params=_cp("parallel", "parallel", "arbitrary"),
    )(qn, qr, kn, kr, v)


def mla_bwd(qn, qr, kn, kr, v, lse, do, *, scale, lc, name):
    nb, t, w = qn.shape
    npair = w // LANE
    nctb = lc // TQ

    def body(qn_ref, qr_ref, kn_ref, kr_ref, v_ref, lse_ref, do_ref, dqn_ref, dqr_ref, dkn_ref, dkr_ref, dv_ref):
        pr, i = pl.program_id(1), pl.program_id(2)

        @pl.when(i == 0)
        def _():
            dkn_ref[...] = jnp.zeros_like(dkn_ref)
            dv_ref[...] = jnp.zeros_like(dv_ref)

        @pl.when((i == 0) & (pr == 0))
        def _():
            dkr_ref[...] = jnp.zeros_like(dkr_ref)

        def run(rows):
            kk, kkr, vv = kn_ref[rows, :], kr_ref[rows, :], v_ref[rows, :]
            dqns, dqrs = [], []
            for e in range(2):
                mine, mine_r = _lanes(MLA_NOPE * e, MLA_NOPE), _lanes(MLA_ROPE * (2 * pr + e), MLA_ROPE)
                qq, qqr, dd = _only(qn_ref[...], mine), _only(qr_ref[...], mine_r), _only(do_ref[...], mine)
                p = jnp.exp((_dot(qq, kk, _NT) + _dot(qqr, kkr, _NT)) * scale - lse_ref[e])
                dp = _dot(dd, vv, _NT)
                delta = jnp.sum(p * dp, axis=-1, keepdims=True)
                ds = (p * (dp - delta) * scale).astype(BF16)
                dqns.append(_dot(ds, kk))
                dqrs.append(_only(_dot(ds, kkr), mine_r))
                dkn_ref[rows, :] += _dot(ds, qq, _TN)
                dkr_ref[rows, :] += _dot(ds, qqr, _TN)
                dv_ref[rows, :] += _dot(p.astype(BF16), dd, _TN)
            dqn_ref[...] = jnp.where(_lanes(0, MLA_NOPE), dqns[0], dqns[1])
            dqr_ref[...] = dqrs[0] + dqrs[1]

        @pl.when(i < nctb)
        def _():
            run(pl.ds(0, lc))

        @pl.when(i >= nctb)
        def _():
            run(pl.ds(0, t))

    qmap = lambda b, p, i: (b, i, p)
    rmap = lambda b, p, i: (b, i, 0)
    kmap = lambda b, p, i: (b, 0, p)
    zmap = lambda b, p, i: (b, 0, 0)
    return pl.pallas_call(
        body, name=name, grid=(nb, npair, t // TQ),
        in_specs=[pl.BlockSpec((None, TQ, LANE), qmap), pl.BlockSpec((None, TQ, LANE), rmap), pl.BlockSpec((None, t, LANE), kmap),
                  pl.BlockSpec((None, t, LANE), zmap), pl.BlockSpec((None, t, LANE), kmap),
                  pl.BlockSpec((None, 2, TQ, 1), lambda b, p, i: (b, p, i, 0)), pl.BlockSpec((None, TQ, LANE), qmap)],
        out_specs=[pl.BlockSpec((None, TQ, LANE), qmap), pl.BlockSpec((None, TQ, LANE), qmap), pl.BlockSpec((None, t, LANE), kmap),
                   pl.BlockSpec((None, t, LANE), zmap), pl.BlockSpec((None, t, LANE), kmap)],
        out_shape=[jax.ShapeDtypeStruct((nb, t, w), F32), jax.ShapeDtypeStruct((nb, t, npair * LANE), F32),
                   jax.ShapeDtypeStruct((nb, t, w), F32), jax.ShapeDtypeStruct((nb, t, LANE), F32), jax.ShapeDtypeStruct((nb, t, w), F32)],
        compiler_params=_cp("arbitrary", "arbitrary", "arbitrary"),
    )(qn, qr, kn, kr, v, lse, do)


def _na_window(st, nc, rows):
    r = jnp.maximum(st - nc, 0)
    r0 = jnp.clip(r - NA_ROWS // 2, 0, rows - NA_ROWS)
    return r, r0, r - r0


def na_fwd(q, k, v, bias, *, lc, name):
    nb, t, w = q.shape
    npair = w // LANE
    nc, rows = lc // GRID_W, (t - lc) // GRID_W
    nwin = NA_ROWS * GRID_W
    scale = HEAD_DIM ** -0.5

    def body(q_ref, k_ref, v_ref, bias_ref, o_ref, lse_ref):
        st = pl.program_id(2)
        ctx = pl.ds(0, lc)
        kc, vc = k_ref[ctx, :], v_ref[ctx, :]
        outs = [None, None]

        @pl.when(st < nc)
        def _():
            for e in range(2):
                p, l, lse = _pair_softmax(_dot(_only(q_ref[...], _lanes(HEAD_DIM * e, HEAD_DIM)), kc, _NT) * scale)
                outs[e] = _dot(p.astype(BF16), vc) / l
                lse_ref[e] = lse
            o_ref[...] = jnp.where(_lanes(0, HEAD_DIM), outs[0], outs[1]).astype(o_ref.dtype)

        @pl.when(st >= nc)
        def _():
            _, r0, _ = _na_window(st, nc, rows)
            win = pl.ds(pl.multiple_of(lc + r0 * GRID_W, GRID_W), nwin)
            kw, vw = k_ref[win, :], v_ref[win, :]
            for e in range(2):
                qq = _only(q_ref[...], _lanes(HEAD_DIM * e, HEAD_DIM))
                s_loc = _dot(qq, kw, _NT) * scale + bias_ref[e]
                s_ctx = _dot(qq, kc, _NT) * scale
                m = jnp.maximum(jnp.max(s_loc, axis=-1, keepdims=True), jnp.max(s_ctx, axis=-1, keepdims=True))
                p_loc, p_ctx = jnp.exp(s_loc - m), jnp.exp(s_ctx - m)
                l = jnp.sum(p_loc, axis=-1, keepdims=True) + jnp.sum(p_ctx, axis=-1, keepdims=True)
                outs[e] = (_dot(p_loc.astype(BF16), vw) + _dot(p_ctx.astype(BF16), vc)) / l
                lse_ref[e] = m + jnp.log(l)
            o_ref[...] = jnp.where(_lanes(0, HEAD_DIM), outs[0], outs[1]).astype(o_ref.dtype)

    qmap = lambda p, b, st: (b, st, p)
    kmap = lambda p, b, st: (b, 0, p)
    return pl.pallas_call(
        body, name=name, grid=(npair, nb, nc + rows),
        in_specs=[pl.BlockSpec((None, GRID_W, LANE), qmap), pl.BlockSpec((None, t, LANE), kmap), pl.BlockSpec((None, t, LANE), kmap),
                  pl.BlockSpec((2, None, GRID_W, nwin), lambda p, b, st: (p, _na_window(st, nc, rows)[2], 0, 0))],
        out_specs=[pl.BlockSpec((None, GRID_W, LANE), qmap), pl.BlockSpec((None, 2, GRID_W, 1), lambda p, b, st: (b, p, st, 0))],
        out_shape=[jax.ShapeDtypeStruct((nb, t, w), BF16), jax.ShapeDtypeStruct((nb, 2 * npair, t, 1), F32)],
        compiler_params=_cp("parallel", "parallel", "arbitrary"),
    )(q, k, v, bias)


def na_bwd(q, k, v, bias, lse, do, *, lc, name):
    nb, t, w = q.shape
    npair = w // LANE
    nc, rows = lc // GRID_W, (t - lc) // GRID_W
    nwin = NA_ROWS * GRID_W
    scale = HEAD_DIM ** -0.5

    def body(q_ref, k_ref, v_ref, bias_ref, lse_ref, do_ref, dq_ref, dk_ref, dv_ref, db_ref):
        b, st = pl.program_id(1), pl.program_id(2)

        @pl.when(st == 0)
        def _():
            dk_ref[...] = jnp.zeros_like(dk_ref)
            dv_ref[...] = jnp.zeros_like(dv_ref)

        @pl.when((st == 0) & (b == 0))
        def _():
            db_ref[...] = jnp.zeros_like(db_ref)

        ctx = pl.ds(0, lc)
        kc, vc = k_ref[ctx, :], v_ref[ctx, :]
        dqs = [None, None]

        @pl.when(st < nc)
        def _():
            for e in range(2):
                mine = _lanes(HEAD_DIM * e, HEAD_DIM)
                qq, dd = _only(q_ref[...], mine), _only(do_ref[...], mine)
                p = jnp.exp(_dot(qq, kc, _NT) * scale - lse_ref[e])
                dp = _dot(dd, vc, _NT)
                delta = jnp.sum(p * dp, axis=-1, keepdims=True)
                ds = (p * (dp - delta) * scale).astype(BF16)
                dqs[e] = _dot(ds, kc)
                dk_ref[ctx, :] += _dot(ds, qq, _TN)
                dv_ref[ctx, :] += _dot(p.astype(BF16), dd, _TN)
            dq_ref[...] = jnp.where(_lanes(0, HEAD_DIM), dqs[0], dqs[1])

        @pl.when(st >= nc)
        def _():
            _, r0, case = _na_window(st, nc, rows)
            win = pl.ds(pl.multiple_of(lc + r0 * GRID_W, GRID_W), nwin)
            kw, vw = k_ref[win, :], v_ref[win, :]
            for e in range(2):
                mine = _lanes(HEAD_DIM * e, HEAD_DIM)
                qq, dd = _only(q_ref[...], mine), _only(do_ref[...], mine)
                p_loc = jnp.exp(_dot(qq, kw, _NT) * scale + bias_ref[e] - lse_ref[e])
                p_ctx = jnp.exp(_dot(qq, kc, _NT) * scale - lse_ref[e])
                dp_loc, dp_ctx = _dot(dd, vw, _NT), _dot(dd, vc, _NT)
                delta = jnp.sum(p_loc * dp_loc, axis=-1, keepdims=True) + jnp.sum(p_ctx * dp_ctx, axis=-1, keepdims=True)
                ds_loc = p_loc * (dp_loc - delta)
                db_ref[e, case] += ds_loc
                ds_loc = (ds_loc * scale).astype(BF16)
                ds_ctx = (p_ctx * (dp_ctx - delta) * scale).astype(BF16)
                dqs[e] = _dot(ds_loc, kw) + _dot(ds_ctx, kc)
                dk_ref[win, :] += _dot(ds_loc, qq, _TN)
                dk_ref[ctx, :] += _dot(ds_ctx, qq, _TN)
                dv_ref[win, :] += _dot(p_loc.astype(BF16), dd, _TN)
                dv_ref[ctx, :] += _dot(p_ctx.astype(BF16), dd, _TN)
            dq_ref[...] = jnp.where(_lanes(0, HEAD_DIM), dqs[0], dqs[1])

    qmap = lambda p, b, st: (b, st, p)
    kmap = lambda p, b, st: (b, 0, p)
    return pl.pallas_call(
        body, name=name, grid=(npair, nb, nc + rows),
        in_specs=[pl.BlockSpec((None, GRID_W, LANE), qmap), pl.BlockSpec((None, t, LANE), kmap), pl.BlockSpec((None, t, LANE), kmap),
                  pl.BlockSpec((2, None, GRID_W, nwin), lambda p, b, st: (p, _na_window(st, nc, rows)[2], 0, 0)),
                  pl.BlockSpec((None, 2, GRID_W, 1), lambda p, b, st: (b, p, st, 0)), pl.BlockSpec((None, GRID_W, LANE), qmap)],
        out_specs=[pl.BlockSpec((None, GRID_W, LANE), qmap), pl.BlockSpec((None, t, LANE), kmap), pl.BlockSpec((None, t, LANE), kmap),
                   pl.BlockSpec((2, NA_ROWS, GRID_W, nwin), lambda p, b, st: (p, 0, 0, 0))],
        out_shape=[jax.ShapeDtypeStruct((nb, t, w), F32), jax.ShapeDtypeStruct((nb, t, w), F32), jax.ShapeDtypeStruct((nb, t, w), F32),
                   jax.ShapeDtypeStruct((2 * npair, NA_ROWS, GRID_W, nwin), F32)],
        compiler_params=_cp("arbitrary", "arbitrary", "arbitrary"),
    )(q, k, v, bias, lse, do)


def _na_tables():
    cols = np.arange(GRID_W)
    c0 = np.clip(cols - NA_COLS // 2, 0, GRID_W - NA_COLS)
    col_in = (cols[None, :] >= c0[:, None]) & (cols[None, :] < c0[:, None] + NA_COLS)
    dc = np.clip(cols[None, :] - cols[:, None] + NA_COLS - 1, 0, 2 * NA_COLS - 2)
    dr = np.arange(NA_ROWS)[None, :] + (NA_ROWS - 1) - np.arange(NA_ROWS)[:, None]
    return col_in, dc, dr


def _na_onehots():
    col_in, dc, dr = _na_tables()
    e1 = np.zeros((GRID_W, GRID_W, LANE), np.float32)
    qi, ki = np.nonzero(col_in)
    e1[qi, ki, dc[qi, ki]] = 1.0
    e2 = np.zeros((2 * NA_ROWS, NA_ROWS, NA_ROWS), np.float32)
    ci, ji = np.meshgrid(np.arange(NA_ROWS), np.arange(NA_ROWS), indexing='ij')
    e2[dr[ci, ji], ci, ji] = 1.0
    return jnp.asarray(e1.reshape(GRID_W * GRID_W, LANE)), jnp.asarray(e2.reshape(2 * NA_ROWS, NA_ROWS * NA_ROWS)), col_in


def na_expand_bias(rel_bias, name):
    e1, e2, col_in = _na_onehots()
    nh = rel_bias.shape[0]
    nrow = NA_ROWS * NA_ROWS
    rel = jnp.pad(rel_bias, ((0, 0), (0, 1), (0, LANE - rel_bias.shape[2])))
    rel = rel.transpose(1, 0, 2).reshape(2 * NA_ROWS, nh * LANE)
    y = mm(e2, rel, ta=True, name=name + "_rows", precise=True)
    y = y.reshape(nrow, nh, LANE).transpose(1, 0, 2).reshape(nh * nrow, LANE)
    g = mm(y, e1, tb=True, name=name + "_cols", precise=True)
    g = g.reshape(nh, NA_ROWS, NA_ROWS, GRID_W, GRID_W).transpose(0, 1, 3, 2, 4)
    g = jnp.where(col_in[None, None, :, None, :], g, NEG_BIG)
    return g.reshape(nh, NA_ROWS, GRID_W, NA_ROWS * GRID_W)


def na_reduce_bias(dexp, name):
    e1, e2, _ = _na_onehots()
    nh = dexp.shape[0]
    x = dexp.reshape(nh, NA_ROWS, GRID_W, NA_ROWS, GRID_W).transpose(0, 1, 3, 2, 4).reshape(nh * NA_ROWS * NA_ROWS, GRID_W * GRID_W)
    y = mm(x, e1, name=name + "_cols", precise=True)
    y = y.reshape(nh, NA_ROWS * NA_ROWS, LANE).transpose(1, 0, 2).reshape(NA_ROWS * NA_ROWS, nh * LANE)
    z = mm(e2, y, name=name + "_rows", precise=True)
    return z.reshape(2 * NA_ROWS, nh, LANE).transpose(1, 0, 2)[:, :2 * NA_ROWS - 1, :2 * NA_COLS - 1]


def _rot_matrix(width, d_rot):
    f = d_rot // 4
    r = np.zeros((width, width), np.float32)
    for base in range(0, width, d_rot // 2):
        for j in range(f):
            r[base + f + j, base + j] = -1.0
            r[base + j, base + f + j] = 1.0
    return r


def _rope_tables(s_len, lc, d_rot, reps):
    half = d_rot // 2
    freqs = ROPE_THETA ** (-jnp.arange(0, half, 2, dtype=F32) / half)
    tpos = jnp.arange(s_len)
    row = (tpos // GRID_W).astype(F32)[:, None] * freqs
    col = (tpos % GRID_W).astype(F32)[:, None] * freqs
    ang = jnp.concatenate([row, row, col, col], axis=-1)
    cos = jnp.concatenate([jnp.ones((lc, d_rot), F32), jnp.cos(ang)], axis=0)
    sin = jnp.concatenate([jnp.zeros((lc, d_rot), F32), jnp.sin(ang)], axis=0)
    return jnp.tile(cos, (1, reps)), jnp.tile(sin, (1, reps))


def _post_consts():
    s_b = np.kron(np.eye(GQA_HEADS, dtype=np.float32), np.full((HEAD_DIM, HEAD_DIM), 1.0 / HEAD_DIM, np.float32))
    t_b = np.tile(np.eye(HEAD_DIM, dtype=np.float32), (1, GQA_HEADS))
    r_b = _rot_matrix(GQ_W, HEAD_DIM)
    r_m = _rot_matrix(LANE, MLA_ROPE)
    rep = np.zeros((LANE, LANE), np.float32)
    for h in range(MLA_HEADS):
        rep[np.arange(MLA_ROPE), h * MLA_ROPE + np.arange(MLA_ROPE)] = 1.0
    dup = np.zeros((GK_W, 2 * GK_W), np.float32)
    for j in range(GQA_KV_HEADS):
        for e in range(2):
            dup[HEAD_DIM * j + np.arange(HEAD_DIM), 2 * HEAD_DIM * j + HEAD_DIM * e + np.arange(HEAD_DIM)] = 1.0
    return tuple(jnp.asarray(a) for a in (s_b, r_b, t_b, r_m, rep, dup))


def _heads_to_parts(w, first):
    r = w.shape[0]
    w3 = w.reshape(r, MLA_HEADS, -1)
    return jnp.concatenate([w3[:, :, :first].reshape(r, -1), w3[:, :, first:].reshape(r, -1)], axis=1)


def _parts_to_heads(w, first):
    r = w.shape[0]
    nf = MLA_HEADS * first
    return jnp.concatenate([w[:, :nf].reshape(r, MLA_HEADS, first), w[:, nf:].reshape(r, MLA_HEADS, -1)], axis=2).reshape(r, -1)


def _place():
    return lax.axis_index("x"), lax.axis_index("y"), lax.axis_index("c")


def all_gather(v, *, name, with_c):
    flips = [(dx, dy, dc) for dx in (0, 1) for dy in (0, 1) for dc in ((0, 1) if with_c else (0,))][1:]
    n = len(flips) + 1

    def body(v_ref, out_ref, send_sems, recv_sems, local_sem):
        mx, my, mc = _place()

        def slot(px, py, pc):
            return 4 * px + 2 * py + pc if with_c else 2 * px + py

        mine = pltpu.make_async_copy(v_ref, out_ref.at[slot(mx, my, mc)], local_sem)
        mine.start()
        sends = []
        for j, (dx, dy, dc) in enumerate(flips):
            peer = (mx ^ dx, my ^ dy, mc ^ dc)
            cp = pltpu.make_async_remote_copy(src_ref=v_ref, dst_ref=out_ref.at[slot(mx, my, mc)], send_sem=send_sems.at[j],
                                              recv_sem=recv_sems.at[j], device_id=peer, device_id_type=MESH)
            cp.start()
            sends.append(cp)
        for j, (dx, dy, dc) in enumerate(flips):
            peer = (mx ^ dx, my ^ dy, mc ^ dc)
            pltpu.make_async_remote_copy(src_ref=v_ref, dst_ref=out_ref.at[slot(*peer)], send_sem=send_sems.at[j],
                                         recv_sem=recv_sems.at[j], device_id=peer, device_id_type=MESH).wait_recv()
        for cp in sends:
            cp.wait_send()
        mine.wait()

    return pl.pallas_call(
        body, name=name, in_specs=[ANY], out_specs=ANY, out_shape=jax.ShapeDtypeStruct((n,) + v.shape, v.dtype),
        scratch_shapes=[pltpu.SemaphoreType.DMA((n - 1,)), pltpu.SemaphoreType.DMA((n - 1,)), pltpu.SemaphoreType.DMA(())],
    )(v)


def gather_shards(v, *, name):
    _, h, w = v.shape
    flips = [(1, 0), (0, 1), (1, 1)]

    def body(v_ref, out_ref, send_sems, recv_sems):
        mx, my, mc = _place()
        me = 2 * mx + my
        sib = (mx, my, 1 - mc)

        def copy(k, src, dst, to):
            return pltpu.make_async_remote_copy(src_ref=src, dst_ref=dst, send_sem=send_sems.at[k], recv_sem=recv_sems.at[k],
                                                device_id=to, device_id_type=MESH)

        first = [copy(j, v_ref.at[mc], out_ref.at[me, mc], (mx ^ dx, my ^ dy, mc)) for j, (dx, dy) in enumerate(flips)]
        for cp in first:
            cp.start()
        passed = []
        for j, (dx, dy) in enumerate(flips):
            theirs = out_ref.at[2 * (mx ^ dx) + (my ^ dy), mc]
            copy(j, v_ref.at[mc], theirs, (mx ^ dx, my ^ dy, mc)).wait_recv()
            fw = copy(3 + j, theirs, theirs, sib)
            fw.start()
            passed.append(fw)
        for j, (dx, dy) in enumerate(flips):
            other = out_ref.at[2 * (mx ^ dx) + (my ^ dy), 1 - mc]
            copy(3 + j, other, other, sib).wait_recv()
        for cp in first + passed:
            cp.wait_send()

    out = pl.pallas_call(
        body, name=name, in_specs=[ANY], out_specs=ANY, out_shape=jax.ShapeDtypeStruct((4, 2, h, w), v.dtype),
        scratch_shapes=[pltpu.SemaphoreType.DMA((6,)), pltpu.SemaphoreType.DMA((6,))],
    )(v)
    mx, my, _ = _place()
    return lax.dynamic_update_slice(out, v[None], (2 * mx + my, 0, 0, 0))


def pair_exchange_halves(g, *, name):
    n, _, h, w = g.shape

    def body(g_ref, out_ref, send_sems, recv_sems):
        mx, my, mc = _place()
        sib = (mx, my, 1 - mc)
        cps = [pltpu.make_async_remote_copy(src_ref=g_ref.at[s, 1 - mc], dst_ref=out_ref.at[s], send_sem=send_sems.at[s],
                                            recv_sem=recv_sems.at[s], device_id=sib, device_id_type=MESH) for s in range(n)]
        for cp in cps:
            cp.start()
        for cp in cps:
            cp.wait_recv()
        for cp in cps:
            cp.wait_send()

    return pl.pallas_call(
        body, name=name, in_specs=[ANY], out_specs=ANY, out_shape=jax.ShapeDtypeStruct((n, h, w), g.dtype),
        scratch_shapes=[pltpu.SemaphoreType.DMA((n,)), pltpu.SemaphoreType.DMA((n,))],
    )(g)


def all_to_all_xy(v, *, name):
    def body(v_ref, out_ref, send_sems, recv_sems):
        mx, my, mc = _place()
        me = 2 * mx + my
        flips = [(1, 0), (0, 1), (1, 1)]
        sends = []
        for j, (dx, dy) in enumerate(flips):
            px, py = mx ^ dx, my ^ dy
            cp = pltpu.make_async_remote_copy(src_ref=v_ref.at[2 * px + py], dst_ref=out_ref.at[me], send_sem=send_sems.at[j],
                                              recv_sem=recv_sems.at[j], device_id=(px, py, mc), device_id_type=MESH)
            cp.start()
            sends.append(cp)
        for j, (dx, dy) in enumerate(flips):
            px, py = mx ^ dx, my ^ dy
            pltpu.make_async_remote_copy(src_ref=v_ref.at[me], dst_ref=out_ref.at[2 * px + py], send_sem=send_sems.at[j],
                                         recv_sem=recv_sems.at[j], device_id=(px, py, mc), device_id_type=MESH).wait_recv()
        for cp in sends:
            cp.wait_send()

    out = pl.pallas_call(
        body, name=name, in_specs=[ANY], out_specs=ANY, out_shape=jax.ShapeDtypeStruct(v.shape, v.dtype),
        scratch_shapes=[pltpu.SemaphoreType.DMA((3,)), pltpu.SemaphoreType.DMA((3,))],
    )(v)
    mx, my, _ = _place()
    me = 2 * mx + my
    return lax.dynamic_update_slice(out, lax.dynamic_slice_in_dim(v, me, 1, axis=0), (me, 0, 0))


def pair_all_gather(v, *, name):
    def body(v_ref, out_ref, send_sem, recv_sem):
        mx, my, mc = _place()
        cp = pltpu.make_async_remote_copy(src_ref=v_ref, dst_ref=out_ref.at[mc], send_sem=send_sem, recv_sem=recv_sem,
                                          device_id=(mx, my, 1 - mc), device_id_type=MESH)
        cp.start()
        pltpu.make_async_remote_copy(src_ref=v_ref, dst_ref=out_ref.at[1 - mc], send_sem=send_sem, recv_sem=recv_sem,
                                     device_id=(mx, my, 1 - mc), device_id_type=MESH).wait_recv()
        cp.wait_send()

    out = pl.pallas_call(
        body, name=name, in_specs=[ANY], out_specs=ANY, out_shape=jax.ShapeDtypeStruct((2,) + v.shape, v.dtype),
        scratch_shapes=[pltpu.SemaphoreType.DMA(()), pltpu.SemaphoreType.DMA(())],
    )(v)
    return lax.dynamic_update_slice(out, v[None], (_place()[2], 0, 0))


def gather_ffn(wl, *, name):
    nl, nblk, cs, d = wl.shape
    assert nl == 2
    flips = [(1, 0), (0, 1), (1, 1)]

    def body(v_ref, out_ref, send_sems, recv_sems):
        mx, my, mc = _place()
        me = 2 * mx + my
        sib = (mx, my, 1 - mc)

        def copy(k, src, dst, to):
            return pltpu.make_async_remote_copy(src_ref=src, dst_ref=dst, send_sem=send_sems.at[k], recv_sem=recv_sems.at[k],
                                                device_id=to, device_id_type=MESH)

        first = [copy(j, v_ref.at[mc], out_ref.at[mc, me], (mx ^ dx, my ^ dy, mc)) for j, (dx, dy) in enumerate(flips)]
        for cp in first:
            cp.start()
        passed = []
        for j, (dx, dy) in enumerate(flips):
            theirs = out_ref.at[mc, 2 * (mx ^ dx) + (my ^ dy)]
            copy(j, v_ref.at[mc], theirs, (mx ^ dx, my ^ dy, mc)).wait_recv()
            fw = copy(3 + j, theirs, theirs, sib)
            fw.start()
            passed.append(fw)
        for j, (dx, dy) in enumerate(flips):
            other = out_ref.at[1 - mc, 2 * (mx ^ dx) + (my ^ dy)]
            copy(3 + j, other, other, sib).wait_recv()
        for cp in first + passed:
            cp.wait_send()

    out = pl.pallas_call(
        body, name=name, in_specs=[ANY], out_specs=ANY, out_shape=jax.ShapeDtypeStruct((nl, 4, nblk, cs, d), wl.dtype),
        scratch_shapes=[pltpu.SemaphoreType.DMA((6,)), pltpu.SemaphoreType.DMA((6,))],
    )(wl)
    mx, my, _ = _place()
    return lax.dynamic_update_slice(out, wl[:, None], (0, 2 * mx + my, 0, 0, 0))


def reduce_ffn(g0, g1, *, name):
    nt = len(g0)
    nsh, cs, d = g0[0].shape
    flips = [(1, 0), (0, 1), (1, 1)]
    mx, my, mc = _place()
    me = 2 * mx + my
    c_idx = jnp.reshape(mc, (1,)).astype(jnp.int32)

    def pair_body(*refs):
        ins0, ins1, outs = refs[:nt], refs[nt:2 * nt], refs[2 * nt:3 * nt]
        send_sems, recv_sems = refs[3 * nt:]
        kx, ky, kc = _place()
        sib = (kx, ky, 1 - kc)
        for c in range(2):
            @pl.when(kc == c)
            def _(c=c):
                mine_out = (ins1, ins0)[c]
                cps = [pltpu.make_async_remote_copy(src_ref=mine_out[t], dst_ref=outs[t], send_sem=send_sems.at[t],
                                                    recv_sem=recv_sems.at[t], device_id=sib, device_id_type=MESH) for t in range(nt)]
                for cp in cps:
                    cp.start()
                for cp in cps:
                    cp.wait_recv()
                for cp in cps:
                    cp.wait_send()

    from_pair = pl.pallas_call(
        pair_body, name=name + "_pair", in_specs=[ANY] * (2 * nt), out_specs=[ANY] * nt,
        out_shape=[jax.ShapeDtypeStruct((nsh, cs, d), F32)] * nt,
        scratch_shapes=[pltpu.SemaphoreType.DMA((nt,)), pltpu.SemaphoreType.DMA((nt,))],
    )(*g0, *g1)

    tr = _row_tile(cs, 64)

    def add_body(c_ref, *refs):
        for t in range(nt):
            mine = jnp.where(c_ref[0] == 0, refs[t][...], refs[nt + t][...])
            refs[3 * nt + t][...] = (mine + refs[2 * nt + t][...]).astype(BF16)

    spec = pl.BlockSpec((None, tr, d), lambda s, i, c_ref: (s, i, 0))
    chip_sum = pl.pallas_call(
        add_body, name=name + "_pair_add",
        grid_spec=pltpu.PrefetchScalarGridSpec(num_scalar_prefetch=1, grid=(nsh, cs // tr), in_specs=[spec] * (3 * nt),
                                               out_specs=[spec] * nt),
        out_shape=[jax.ShapeDtypeStruct((nsh, cs, d), BF16)] * nt, compiler_params=_cp("parallel", "parallel"),
    )(c_idx, *g0, *g1, *from_pair)

    def xy_body(*refs):
        ins, outs = refs[:nt], refs[nt:2 * nt]
        send_sems, recv_sems = refs[2 * nt:]
        kx, ky, kc = _place()
        k_me = 2 * kx + ky
        sends = []
        for j, (dx, dy) in enumerate(flips):
            px, py = kx ^ dx, ky ^ dy
            for t in range(nt):
                cp = pltpu.make_async_remote_copy(src_ref=ins[t].at[2 * px + py], dst_ref=outs[t].at[k_me],
                                                  send_sem=send_sems.at[j * nt + t], recv_sem=recv_sems.at[j * nt + t],
                                                  device_id=(px, py, kc), device_id_type=MESH)
                cp.start()
                sends.append(cp)
        for j, (dx, dy) in enumerate(flips):
            px, py = kx ^ dx, ky ^ dy
            for t in range(nt):
                pltpu.make_async_remote_copy(src_ref=ins[t].at[k_me], dst_ref=outs[t].at[2 * px + py],
                                             send_sem=send_sems.at[j * nt + t], recv_sem=recv_sems.at[j * nt + t],
                                             device_id=(px, py, kc), device_id_type=MESH).wait_recv()
        for cp in sends:
            cp.wait_send()

    from_xy = pl.pallas_call(
        xy_body, name=name + "_xy", in_specs=[ANY] * nt, out_specs=[ANY] * nt,
        out_shape=[jax.ShapeDtypeStruct((nsh, cs, d), BF16)] * nt,
        scratch_shapes=[pltpu.SemaphoreType.DMA((3 * nt,)), pltpu.SemaphoreType.DMA((3 * nt,))],
    )(*chip_sum)
    from_xy = [lax.dynamic_update_slice(o, lax.dynamic_slice_in_dim(v, me, 1, axis=0), (me, 0, 0)) for o, v in zip(from_xy, chip_sum)]

    def sum_body(*refs):
        for t in range(nt):
            acc = refs[t][0].astype(F32)
            for s in range(1, nsh):
                acc = acc + refs[t][s].astype(F32)
            refs[nt + t][...] = acc

    reduced = pl.pallas_call(
        sum_body, name=name + "_xy_add", grid=(cs // tr,), in_specs=[pl.BlockSpec((nsh, tr, d), lambda i: (0, i, 0))] * nt,
        out_specs=[pl.BlockSpec((tr, d), lambda i: (i, 0))] * nt, out_shape=[jax.ShapeDtypeStruct((cs, d), F32)] * nt,
        compiler_params=_cp("parallel"),
    )(*from_xy)

    def share_body(*refs):
        ins, outs = refs[:nt], refs[nt:2 * nt]
        send_sems, recv_sems = refs[2 * nt:]
        kx, ky, kc = _place()
        sib = (kx, ky, 1 - kc)
        cps = [pltpu.make_async_remote_copy(src_ref=ins[t], dst_ref=outs[t].at[kc], send_sem=send_sems.at[t],
                                            recv_sem=recv_sems.at[t], device_id=sib, device_id_type=MESH) for t in range(nt)]
        for cp in cps:
            cp.start()
        for t in range(nt):
            pltpu.make_async_remote_copy(src_ref=ins[t], dst_ref=outs[t].at[1 - kc], send_sem=send_sems.at[t],
                                         recv_sem=recv_sems.at[t], device_id=sib, device_id_type=MESH).wait_recv()
        for cp in cps:
            cp.wait_send()

    both = pl.pallas_call(
        share_body, name=name + "_share", in_specs=[ANY] * nt, out_specs=[ANY] * nt,
        out_shape=[jax.ShapeDtypeStruct((2, cs, d), F32)] * nt,
        scratch_shapes=[pltpu.SemaphoreType.DMA((nt,)), pltpu.SemaphoreType.DMA((nt,))],
    )(*reduced)
    return [lax.dynamic_update_slice(o, v[None], (mc, 0, 0)) for o, v in zip(both, reduced)]


def add_kept_half(g, r, c_idx, *, name, out_dtype):
    n, _, h, w = g.shape
    th = _row_tile(h)

    def body(c_ref, g_ref, r_ref, o_ref):
        o_ref[...] = (g_ref[...] + r_ref[...]).astype(o_ref.dtype)

    return pl.pallas_call(
        body, name=name,
        grid_spec=pltpu.PrefetchScalarGridSpec(
            num_scalar_prefetch=1, grid=(n, h // th),
            in_specs=[pl.BlockSpec((None, None, th, w), lambda s, i, c_ref: (s, c_ref[0], i, 0)),
                      pl.BlockSpec((None, th, w), lambda s, i, c_ref: (s, i, 0))],
            out_specs=pl.BlockSpec((None, th, w), lambda s, i, c_ref: (s, i, 0))),
        out_shape=jax.ShapeDtypeStruct((n, h, w), out_dtype), compiler_params=_cp("parallel", "parallel"),
    )(c_idx, g, r)


def sum_slots(v, *, name):
    n, rows, w = v.shape
    tr = _row_tile(rows, 256)

    def body(v_ref, o_ref):
        acc = v_ref[0].astype(F32)
        for s in range(1, n):
            acc = acc + v_ref[s].astype(F32)
        o_ref[...] = acc

    return pl.pallas_call(body, name=name, grid=(rows // tr,), in_specs=[pl.BlockSpec((n, tr, w), lambda i: (0, i, 0))],
                          out_specs=pl.BlockSpec((tr, w), lambda i: (i, 0)), out_shape=jax.ShapeDtypeStruct((rows, w), F32),
                          compiler_params=_cp("parallel"))(v)


def ada_fwd(c_rows, w_ada, b_shard, *, name):
    nl, d, ncol = w_ada.shape
    rows = c_rows.shape[0]
    tn = _tile(ncol, (768, 512, 256, 128))

    def body(c_ref, w_ref, b_ref, o_ref):
        o_ref[...] = jnp.dot(jax.nn.silu(c_ref[...]), w_ref[...], precision=HI, preferred_element_type=F32) + b_ref[...]

    return pl.pallas_call(
        body, name=name, grid=(nl, ncol // tn),
        in_specs=[pl.BlockSpec((rows, d), lambda l, j: (0, 0)), pl.BlockSpec((None, d, tn), lambda l, j: (l, 0, j)),
                  pl.BlockSpec((None, 1, tn), lambda l, j: (l, 0, j))],
        out_specs=pl.BlockSpec((None, rows, tn), lambda l, j: (l, 0, j)),
        out_shape=jax.ShapeDtypeStruct((nl, rows, ncol), F32), compiler_params=_cp("parallel", "parallel"),
    )(c_rows, w_ada, b_shard)


def ada_bwd(c_rows, w_ada, dm_shard, dm_full, n_ex, *, name):
    nl, d, ncol = w_ada.shape
    rows = c_rows.shape[0]
    tn = _tile(ncol, (768, 512, 256, 128))
    nj = ncol // tn

    def body(c_ref, w_ref, dm_ref, dmf_ref, gw_ref, gb_ref, dc_ref, dact_ref):
        l, j = pl.program_id(0), pl.program_id(1)
        act, act_vjp = jax.vjp(jax.nn.silu, c_ref[...])
        gw_ref[...] = lax.dot_general(act, dm_ref[...], _TN, precision=HI, preferred_element_type=F32)
        gb_ref[...] = jnp.sum(dmf_ref[...], axis=0, keepdims=True)
        part = lax.dot_general(dm_ref[...], w_ref[...], _NT, precision=HI, preferred_element_type=F32)

        @pl.when((l == 0) & (j == 0))
        def _():
            dact_ref[...] = part

        @pl.when((l > 0) | (j > 0))
        def _():
            dact_ref[...] += part

        @pl.when((l == nl - 1) & (j == nj - 1))
        def _():
            dc, = act_vjp(dact_ref[...])
            dc_ref[...] = jnp.sum(dc[n_ex:, :], axis=0, keepdims=True)

    return pl.pallas_call(
        body, name=name, grid=(nl, nj),
        in_specs=[pl.BlockSpec((rows, d), lambda l, j: (0, 0)), pl.BlockSpec((None, d, tn), lambda l, j: (l, 0, j)),
                  pl.BlockSpec((None, rows, tn), lambda l, j: (l, 0, j)),
                  pl.BlockSpec((None, rows, dm_full.shape[-1]), lambda l, j: (l, 0, 0))],
        out_specs=[pl.BlockSpec((None, d, tn), lambda l, j: (l, 0, j)),
                   pl.BlockSpec((None, 1, dm_full.shape[-1]), lambda l, j: (l, 0, 0)),
                   pl.BlockSpec((1, d), lambda l, j: (0, 0))],
        out_shape=[jax.ShapeDtypeStruct((nl, d, ncol), F32), jax.ShapeDtypeStruct((nl, 1, dm_full.shape[-1]), F32),
                   jax.ShapeDtypeStruct((1, d), F32)],
        scratch_shapes=[pltpu.VMEM((rows, d), F32)], compiler_params=_cp("arbitrary", "arbitrary"),
    )(c_rows, w_ada, dm_shard, dm_full)


def adamw(w, g, m, v, *, name):
    shape = w.shape
    cols = shape[-1]
    rows = int(np.prod(shape[:-1])) if len(shape) > 1 else 1
    tr = _row_tile(rows, 256)

    def body(w_ref, g_ref, m_ref, v_ref, d_ref, nm_ref, nv_ref):
        gg = g_ref[...]
        nm = ADAM_B1 * m_ref[...] + (1.0 - ADAM_B1) * gg
        nv = ADAM_B2 * v_ref[...] + (1.0 - ADAM_B2) * jnp.square(gg)
        m_hat = nm / (1.0 - ADAM_B1 ** ADAM_STEP)
        v_hat = nv / (1.0 - ADAM_B2 ** ADAM_STEP)
        d_ref[...] = -ADAM_LR * (m_hat / (jnp.sqrt(v_hat) + ADAM_EPS) + ADAM_WD * w_ref[...])
        nm_ref[...] = nm
        nv_ref[...] = nv

    spec = pl.BlockSpec((tr, cols), lambda i: (i, 0))
    out = pl.pallas_call(body, name=name, grid=(rows // tr,), in_specs=[spec] * 4, out_specs=[spec] * 3,
                         out_shape=[jax.ShapeDtypeStruct((rows, cols), F32)] * 3, compiler_params=_cp("parallel"),
                         )(*[a.reshape(rows, cols) for a in (w, g, m, v)])
    return tuple(o.reshape(shape) for o in out)


def local_step(h0, target, mods, lw, wf, small, *, lc):
    nb, t, d = h0.shape
    nt, nct = t // TM, lc // TM
    s_len = t - lc
    nl = len(lw)
    nsh = wf.shape[1]
    consts = _post_consts()
    cos_b, sin_b = _rope_tables(s_len, lc, HEAD_DIM, GQA_HEADS)
    cos_m, sin_m = _rope_tables(s_len, lc, MLA_ROPE, MLA_HEADS)
    na_scale = HEAD_DIM ** -0.5
    mla_scale = (MLA_NOPE + MLA_ROPE) ** -0.5
    rc = functools.partial(rowcall, nb=nb, nt=nt, nct=nct)
    flat = lambda a: a.reshape(nb * t, a.shape[-1])
    unflat = lambda a: a.reshape(nb, t, a.shape[-1])
    vec = lambda a: a.reshape(1, -1)

    def ffn_fwd(h, g, mod3, l, base, tag):
        shift, scale, gate = mod3
        n, = rc(tag + "_norm", lambda _, *a: (f_normmod(*a),), [(h, 'tok'), (vec(g), 'full'), (shift, 'mod'), (scale, 'mod')],
                [('tok', d, BF16)])
        gg, uu, act = ffn_up(flat(n), wf, l, base, name=tag + "_up")
        y = unflat(ffn_down(act, wf, l, base, name=tag + "_down"))
        h2, = rc(tag + "_res", lambda _, hh, yy, gt: (hh + 0.5 * gt * yy,), [(h, 'tok'), (y, 'tok'), (gate, 'mod')], [('tok', d, F32)])
        return h2, (h, n, gg, uu, act, y)

    def ffn_bwd(dh2, saved, g, mod3, l, base, tag):
        shift, scale, gate = mod3
        h, n, gg, uu, act, y = saved
        dy, dgate = rc(tag + "_res_bwd", lambda _, dd, yy, gt: (0.5 * gt * dd, jnp.sum(0.5 * yy * dd, axis=0, keepdims=True)),
                       [(dh2, 'tok'), (y, 'tok'), (gate, 'mod')], [('tok', d, BF16), ('mod', d)])
        dw_d = ffn_dw(act, flat(dy), nsh, name=tag + "_down_dw")
        dgg, duu = ffn_down_bwd(flat(dy), gg, uu, wf, l, base, name=tag + "_down_dx")
        dw_g = ffn_dw(dgg, flat(n), nsh, name=tag + "_gate_dw")
        dw_u = ffn_dw(duu, flat(n), nsh, name=tag + "_up_dw")
        dn = unflat(ffn_up_bwd(dgg, duu, wf, l, base, name=tag + "_up_dx"))

        def norm_bwd(_, hh, gn, sh, sc, dnn, dres):
            dh, dg, dsh, dsc = jax.vjp(f_normmod, hh, gn, sh, sc)[1](dnn)
            return dh + dres, dg, dsh, dsc

        dh, dg, dshift, dscale = rc(tag + "_norm_bwd", norm_bwd,
                                    [(h, 'tok'), (vec(g), 'full'), (shift, 'mod'), (scale, 'mod'), (dn, 'tok'), (dh2, 'tok')],
                                    [('tok', d, F32), ('full', (1, d)), ('mod', d), ('mod', d)])
        return dh, dg.reshape(d), (dshift, dscale, dgate), [dw_g, dw_u, dw_d]

    def post_ins(p, sm, w):
        return [(p, ('tokc', MAIN_PAD, 0)), (cos_b, 'pos'), (sin_b, 'pos'), (cos_m, 'pos'), (sin_m, 'pos'),
                (vec(sm['gqa_q_norm']), 'full'), (vec(sm['gqa_k_norm']), 'full'), (vec(sm['mla_q_norm']), 'full'),
                (vec(sm['mla_kv_norm']), 'full'), (w['w_uq'], 'full'), (w['w_ukv'], 'full')] + [(c, 'full') for c in consts]

    def mix_fwd(h, sm, mod3, w, tag):
        shift, scale, gate = mod3
        n, = rc(tag + "_norm", lambda _, *a: (f_normmod(*a),), [(h, 'tok'), (vec(sm['mix_norm']), 'full'), (shift, 'mod'), (scale, 'mod')],
                [('tok', d, BF16)])
        p = unflat(mm(flat(n), w['w_in'], name=tag + "_in"))
        parts = rc(tag + "_post", lambda _, *a: f_post(*a), post_ins(p, sm, w), [('tok', wd, BF16) for wd in POST_WIDTHS])
        aq, ak, av, bq, bk, bv, mqn, mqr, mkn, mkr, mv = parts
        bias = na_expand_bias(sm['na_rel_bias'], tag + "_bias")
        o_a, lse_a = na_fwd(aq, ak, av, bias, lc=lc, name=tag + "_na")
        o_b, lse_b = gqa_fwd(bq, bk, bv, scale=na_scale, lc=lc, name=tag + "_gqa")
        o_m, lse_m = mla_fwd(mqn, mqr, mkn, mkr, mv, scale=mla_scale, lc=lc, name=tag + "_mla")
        fo = [o_a, o_b, o_m]
        ys = [unflat(mm(flat(o), w[k], name=tag + "_br" + k[-1])) for o, k in zip(fo, ('w_a', 'w_b', 'w_c'))]
        gcols = [(p, ('tokc', d, MAIN_PAD // d + j)) for j in range(3)]
        y, = rc(tag + "_merge", lambda _, *a: (f_merge(*a),), gcols + [(v, 'tok') for v in ys], [('tok', d, BF16)])
        z = unflat(mm(flat(y), w['w_o'], name=tag + "_out"))
        h2, = rc(tag + "_res", lambda _, hh, zz, gt: (hh + gt * zz,), [(h, 'tok'), (z, 'tok'), (gate, 'mod')], [('tok', d, F32)])
        saved = (h, n, p, (aq, ak, av, lse_a, bias), (bq, bk, bv, lse_b), (mqn, mqr, mkn, mkr, mv, lse_m), fo, ys, y, z)
        return h2, saved

    def mix_bwd(dh2, saved, sm, mod3, w, tag):
        shift, scale, gate = mod3
        h, n, p, (aq, ak, av, lse_a, bias), (bq, bk, bv, lse_b), (mqn, mqr, mkn, mkr, mv, lse_m), fo, ys, y, z = saved
        dz, dgate = rc(tag + "_res_bwd", lambda _, dd, zz, gt: (gt * dd, jnp.sum(zz * dd, axis=0, keepdims=True)),
                       [(dh2, 'tok'), (z, 'tok'), (gate, 'mod')], [('tok', d, BF16), ('mod', d)])
        dw_o = mm(flat(y), flat(dz), ta=True, name=tag + "_out_dw")
        dy = unflat(mm(flat(dz), w['w_o'], tb=True, name=tag + "_out_dx"))
        gcols = [(p, ('tokc', d, MAIN_PAD // d + j)) for j in range(3)]

        def merge_bwd(_, ga, gb, gm, ya, yb, ym, dyy):
            dga, dgb, dgm, dya, dyb, dym = jax.vjp(f_merge, ga, gb, gm, ya, yb, ym)[1](dyy)
            return dya, dyb, dym, jnp.concatenate([dga, dgb, dgm], axis=-1)

        dya, dyb, dym, dgl = rc(tag + "_merge_bwd", merge_bwd, gcols + [(v, 'tok') for v in ys] + [(dy, 'tok')],
                                [('tok', d, BF16)] * 3 + [('tok', 3 * d, BF16)])
        dws, dos = {}, []
        for o, dyk, k in zip(fo, (dya, dyb, dym), ('w_a', 'w_b', 'w_c')):
            dws[k] = mm(flat(o), flat(dyk), ta=True, name=tag + "_br" + k[-1] + "_dw")
            dos.append(unflat(mm(flat(dyk), w[k], tb=True, out_dtype=BF16, name=tag + "_br" + k[-1] + "_dx")))
        do_a, do_b, do_m = dos
        daq, dak, dav, dbias = na_bwd(aq, ak, av, bias, lse_a, do_a, lc=lc, name=tag + "_na_bwd")
        dbq, dbk, dbv = gqa_bwd(bq, bk, bv, lse_b, do_b, scale=na_scale, lc=lc, name=tag + "_gqa_bwd")
        dmqn, dmqr2, dmkn, dmkr, dmv = mla_bwd(mqn, mqr, mkn, mkr, mv, lse_m, do_m, scale=mla_scale, lc=lc, name=tag + "_mla_bwd")
        d_rel = na_reduce_bias(dbias, tag + "_relb")
        cots = [daq, dak, dav, dbq, dbk, dbv, dmqn, dmqr2, dmkn, dmkr, dmv]
        ins = post_ins(p, sm, w)
        n_in = len(ins)

        def post_bwd(_, *a):
            prim, cot, dgl_v = a[:11], list(a[n_in:n_in + 11]), a[-1]
            cot[7] = cot[7][:, :LANE] + cot[7][:, LANE:]
            outs = jax.vjp(lambda pp, qn, kn, mqn, mkvn, wuq, wukv: f_post(pp, *prim[1:5], qn, kn, mqn, mkvn, wuq, wukv, *a[11:n_in]),
                           prim[0], *prim[5:11])[1](tuple(cot))
            return (jnp.concatenate([outs[0].astype(BF16), dgl_v], axis=-1),) + tuple(outs[1:])

        res = rc(tag + "_post_bwd", post_bwd, ins + [(cv, 'tok') for cv in cots] + [(dgl, 'tok')],
                 [('tok', MAIN_PAD + 3 * d, BF16), ('full', (1, HEAD_DIM)), ('full', (1, HEAD_DIM)), ('full', (1, MLA_Q_RANK)),
                  ('full', (1, MLA_KV_RANK)), ('full', w['w_uq'].shape), ('full', w['w_ukv'].shape)])
        dp, dqn, dkn, dmqn, dmkvn, dw_uq, dw_ukv = res
        dw_in = mm(flat(n), flat(dp), ta=True, name=tag + "_in_dw")
        dn = unflat(mm(flat(dp), w['w_in'], tb=True, name=tag + "_in_dx"))

        def norm_bwd(_, hh, gg, sh, sc, dnn, dres):
            dh, dg, dsh, dsc = jax.vjp(f_normmod, hh, gg, sh, sc)[1](dnn)
            return dh + dres, dg, dsh, dsc

        dh, dg, dshift, dscale = rc(tag + "_norm_bwd", norm_bwd,
                                    [(h, 'tok'), (vec(sm['mix_norm']), 'full'), (shift, 'mod'), (scale, 'mod'), (dn, 'tok'), (dh2, 'tok')],
                                    [('tok', d, F32), ('full', (1, d)), ('mod', d), ('mod', d)])
        dsm = {'mix_norm': dg.reshape(d), 'na_rel_bias': d_rel, 'gqa_q_norm': dqn.reshape(-1), 'gqa_k_norm': dkn.reshape(-1),
               'mla_q_norm': dmqn.reshape(-1), 'mla_kv_norm': dmkvn.reshape(-1)}
        dwl = {'w_in': dw_in, 'w_uq': dw_uq, 'w_ukv': dw_ukv, 'w_o': dw_o, **dws}
        return dh, dsm, (dshift, dscale, dgate), dwl

    h = h0
    saved = []
    for l in range(nl):
        sm = {k: small[k][l] for k in SMALL_LAYER}
        h, s1 = ffn_fwd(h, sm['ffn1_norm'], mods[l][0:3], l, 0, f"l{l}_ffn1")
        h, s2 = mix_fwd(h, sm, mods[l][3:6], lw[l], f"l{l}_mix")
        h, s3 = ffn_fwd(h, sm['ffn2_norm'], mods[l][6:9], l, 3, f"l{l}_ffn2")
        saved.append((sm, s1, s2, s3))

    def final(is_ctx, hh, gg, tgt):
        def loss_fn(hv, gv):
            return 0.5 * jnp.sum(jnp.mean(jnp.square(_rms(hv, gv) - tgt), axis=-1))

        keep = jnp.where(is_ctx, 0.0, 1.0)
        loss, (dh, dg) = jax.value_and_grad(loss_fn, argnums=(0, 1))(hh, gg)
        return dh * keep, jnp.full((1, LANE), loss * keep, F32), dg * keep

    dh, loss, dg_final = rc("final_loss", final, [(h, 'tok'), (vec(small['final_norm']), 'full'), (target, 'lat')],
                            [('tok', d, F32), ('full', (1, LANE)), ('full', (1, d))])

    dsmall = {k: [None] * nl for k in SMALL_LAYER}
    dmods, dlw, dwf = [None] * nl, [None] * nl, [None] * nl
    for l in reversed(range(nl)):
        sm, s1, s2, s3 = saved[l]
        dh, dg3, dm3, dwf2 = ffn_bwd(dh, s3, sm['ffn2_norm'], mods[l][6:9], l, 3, f"l{l}_ffn2")
        dh, dsm, dm2, dlw[l] = mix_bwd(dh, s2, sm, mods[l][3:6], lw[l], f"l{l}_mix")
        dh, dg1, dm1, dwf1 = ffn_bwd(dh, s1, sm['ffn1_norm'], mods[l][0:3], l, 0, f"l{l}_ffn1")
        dmods[l] = list(dm1) + list(dm2) + list(dm3)
        dwf[l] = dwf1 + dwf2
        dsm.update(ffn1_norm=dg1, ffn2_norm=dg3)
        for k in SMALL_LAYER:
            dsmall[k][l] = dsm[k]
    dsmall = {k: jnp.stack(v) for k, v in dsmall.items()}
    dsmall['final_norm'] = dg_final.reshape(d)
    return loss, dh, dmods, dlw, dwf, dsmall


def _pack(parts, pad_rows):
    flat, where, off = [], [], 0
    for a in parts:
        n = _ceil_to(a.size, PACK_W)
        flat.append(jnp.pad(a.reshape(-1), (0, n - a.size)))
        where.append((off, n // PACK_W))
        off += n // PACK_W
    total = _ceil_to(off, pad_rows)
    if total > off:
        flat.append(jnp.zeros(((total - off) * PACK_W,), flat[0].dtype))
    return jnp.concatenate(flat).reshape(total, PACK_W), where


def _unpack(buf, where, shape):
    off, rows = where
    return buf[off:off + rows].reshape(-1)[:int(np.prod(shape))].reshape(shape)


def layer_weights(full, l):
    wi = full['w_in'][l]
    d = wi.shape[0]
    return {
        'w_in': jnp.concatenate([wi[:, :MAIN_W], jnp.zeros((d, MAIN_PAD - MAIN_W), wi.dtype), wi[:, MAIN_W:]], axis=1),
        'w_uq': _heads_to_parts(full['mla_w_uq'][l], MLA_NOPE).astype(F32),
        'w_ukv': _heads_to_parts(full['mla_w_ukv'][l], MLA_NOPE).astype(F32),
        'w_a': full['w_branch_a'][l], 'w_b': full['w_branch_b'][l], 'w_c': full['w_branch_c'][l], 'w_o': full['w_out'][l]}


def layer_grads_by_name(dlw):
    per_name = {k: [] for k, _ in BIG}
    for g in dlw:
        per_name['w_in'].append(jnp.concatenate([g['w_in'][:, :MAIN_W], g['w_in'][:, MAIN_PAD:]], axis=1))
        per_name['mla_w_uq'].append(_parts_to_heads(g['w_uq'], MLA_NOPE))
        per_name['mla_w_ukv'].append(_parts_to_heads(g['w_ukv'], MLA_NOPE))
        per_name['w_branch_a'].append(g['w_a'])
        per_name['w_branch_b'].append(g['w_b'])
        per_name['w_branch_c'].append(g['w_c'])
        per_name['w_out'].append(g['w_o'])
    return per_name


def kernel(x, c, ctx, c_ctx, w_ada, b_ada, ffn1_norm, ffn1_w_gate, ffn1_w_up, ffn1_w_down, mix_norm, w_in, na_rel_bias, gqa_q_norm, gqa_k_norm, mla_q_norm, mla_kv_norm, mla_w_uq, mla_w_ukv, w_branch_a, w_branch_b, w_branch_c, w_out, ffn2_norm, ffn2_w_gate, ffn2_w_up, ffn2_w_down, final_norm, loss_target, m_c_ctx, m_w_ada, m_b_ada, m_ffn1_norm, m_ffn1_w_gate, m_ffn1_w_up, m_ffn1_w_down, m_mix_norm, m_w_in, m_na_rel_bias, m_gqa_q_norm, m_gqa_k_norm, m_mla_q_norm, m_mla_kv_norm, m_mla_w_uq, m_mla_w_ukv, m_w_branch_a, m_w_branch_b, m_w_branch_c, m_w_out, m_ffn2_norm, m_ffn2_w_gate, m_ffn2_w_up, m_ffn2_w_down, m_final_norm, v_c_ctx, v_w_ada, v_b_ada, v_ffn1_norm, v_ffn1_w_gate, v_ffn1_w_up, v_ffn1_w_down, v_mix_norm, v_w_in, v_na_rel_bias, v_gqa_q_norm, v_gqa_k_norm, v_mla_q_norm, v_mla_kv_norm, v_mla_w_uq, v_mla_w_ukv, v_w_branch_a, v_w_branch_b, v_w_branch_c, v_w_out, v_ffn2_norm, v_ffn2_w_gate, v_ffn2_w_up, v_ffn2_w_down, v_final_norm):
    args = locals()
    wts = {k: args[k] for k in WEIGHTS}
    mom = {k: args['m_' + k] for k in WEIGHTS}
    var = {k: args['v_' + k] for k in WEIGHTS}
    nb, s_len, d = x.shape
    lc = ctx.shape[1]
    nl = w_ada.shape[0]
    nsh, ndev = 4, 8
    mx, my, mc = _place()
    sidx = 2 * mx + my
    didx = 4 * mx + 2 * my + mc
    assert d % LANE == 0 and MAIN_PAD % d == 0 and lc % TQ == 0 and s_len % TQ == 0 and s_len // GRID_W >= NA_ROWS

    wpack, wwhere = _pack([wts[k].astype(BF16) for k, _ in BIG], 32)
    wall = gather_shards(wpack.reshape(2, -1, PACK_W), name="gather_weights").reshape(nsh, -1, PACK_W)
    full = {}
    for (k, ax), wh in zip(BIG, wwhere):
        shp = wts[k].shape
        parts = jnp.stack([_unpack(wall[s], wh, shp) for s in range(nsh)])
        if ax == 1:
            full[k] = parts.transpose(1, 2, 0, 3).reshape(nl, shp[1], nsh * shp[2])
        else:
            full[k] = parts.transpose(1, 0, 2, 3).reshape(nl, nsh * shp[1], shp[2])
    lw = [layer_weights(full, l) for l in range(nl)]
    wl = jnp.stack([(wts[k].transpose(0, 2, 1) if tr else wts[k]).astype(BF16) for k, tr in zip(FFN_NAMES, FFN_TRANSPOSED)], axis=1)
    wf = gather_ffn(wl, name="gather_ffn")

    n_ex = ndev * nb
    ncol = w_ada.shape[-1]
    c_all = all_gather(c, name="gather_cond", with_c=True).reshape(n_ex, d)
    c_rows = jnp.concatenate([c_all, jnp.broadcast_to(c_ctx[None], (n_ex, d))], axis=0)
    b_shard = lax.dynamic_slice_in_dim(b_ada, sidx * ncol, ncol, axis=1)[:, None, :]
    mod_sh = ada_fwd(c_rows, w_ada, b_shard, name="ada_fwd")
    mod_all = all_gather(mod_sh, name="gather_mod", with_c=False)
    mod_all = mod_all.transpose(1, 2, 0, 3).reshape(nl, 2 * n_ex, nsh * ncol)
    mod_x = lax.dynamic_slice_in_dim(mod_all, didx * nb, nb, axis=1)
    mod_c = jnp.broadcast_to(mod_all[:, n_ex:n_ex + 1], mod_x.shape)
    mods = [[jnp.stack([mod_c[l, :, j * d:(j + 1) * d], mod_x[l, :, j * d:(j + 1) * d]], axis=1)[:, :, None, :]
             for j in range(N_MOD)] for l in range(nl)]

    small = {k: wts[k] for k in SMALL_LAYER + ['final_norm']}
    h0 = jnp.concatenate([ctx, x], axis=1)
    loss_part, dh0, dmods, dlw, dwf, dsmall = local_step(h0, loss_target, mods, lw, wf, small, lc=lc)
    grad_x = dh0[:, lc:]

    dmod_mine = jnp.stack([jnp.concatenate([m[:, :, 0, :] for m in dmods[l]], axis=-1) for l in range(nl)])
    small_names = SMALL_LAYER + ['final_norm']
    spack, swhere = _pack([loss_part] + [dsmall[k] for k in small_names] + [dmod_mine], 8)
    sall = all_gather(spack, name="gather_small", with_c=True)
    ssum = sum_slots(sall, name="sum_small")
    loss = _unpack(ssum, swhere[0], (1, LANE))[0, 0]
    grads = {k: _unpack(ssum, wh, wts[k].shape) for k, wh in zip(small_names, swhere[1:])}
    off, rows = swhere[-1]
    dm_all = sall[:, off:off + rows].reshape(ndev, -1)[:, :dmod_mine.size].reshape((ndev,) + dmod_mine.shape)
    dm_all = dm_all.transpose(1, 3, 0, 2, 4).reshape(nl, 2, n_ex, N_MOD * d)
    dm_rows = jnp.concatenate([dm_all[:, 1], dm_all[:, 0]], axis=1)
    dm_shard = lax.dynamic_slice_in_dim(dm_rows, sidx * ncol, ncol, axis=2)
    grads['w_ada'], gb, dc_part = ada_bwd(c_rows, w_ada, dm_shard, dm_rows, n_ex, name="ada_bwd")
    grads['b_ada'] = gb.reshape(b_ada.shape)
    dc_all = all_gather(jnp.pad(dc_part, ((0, 7), (0, 0))), name="gather_dcond", with_c=False)
    grads['c_ctx'] = sum_slots(dc_all, name="sum_dcond")[0]

    per_name = layer_grads_by_name(dlw)
    pieces, gwhere, off = [], [], 0
    for k, ax in BIG:
        shp = wts[k].shape
        for g in per_name[k]:
            if ax == 1:
                pieces.append(g.reshape(shp[1], nsh, shp[2]).transpose(1, 0, 2).reshape(nsh, -1))
            else:
                pieces.append(g.reshape(nsh, -1))
        n = int(np.prod(shp))
        if n % PACK_W:
            pieces.append(jnp.zeros((nsh, _ceil_to(n, PACK_W) - n), F32))
        gwhere.append((off, _ceil_to(n, PACK_W) // PACK_W))
        off += _ceil_to(n, PACK_W) // PACK_W
    if off % 32:
        pieces.append(jnp.zeros((nsh, (_ceil_to(off, 32) - off) * PACK_W), F32))
    half = _ceil_to(off, 32) // 2
    gpack = jnp.concatenate(pieces, axis=1).reshape(nsh, 2, half, PACK_W)
    from_pair = pair_exchange_halves(gpack, name="reduce_pair")
    chip_sum = add_kept_half(gpack, from_pair, jnp.reshape(mc, (1,)).astype(jnp.int32), name="reduce_pair_add",
                             out_dtype=BF16)
    from_xy = all_to_all_xy(chip_sum, name="reduce_xy")
    reduced = sum_slots(from_xy, name="reduce_xy_add")
    gfull = pair_all_gather(reduced, name="reduce_share").reshape(2 * half, PACK_W)
    for (k, _), wh in zip(BIG, gwhere):
        grads[k] = _unpack(gfull, wh, wts[k].shape)
    for k, tr, g in zip(FFN_NAMES, FFN_TRANSPOSED, reduce_ffn(dwf[0], dwf[1], name="reduce_ffn")):
        grads[k] = g.transpose(0, 2, 1) if tr else g

    outs = {k: adamw(wts[k], grads[k], mom[k], var[k], name="adamw_" + k) for k in WEIGHTS}
    return (loss, grad_x, *[grads[k] for k in WEIGHTS], *[outs[k][0] for k in WEIGHTS], *[outs[k][1] for k in WEIGHTS],
            *[outs[k][2] for k in WEIGHTS])
```

```python
import functools

import jax
import jax.numpy as jnp
import numpy as np
from jax import lax
from jax.experimental import pallas as pl
from jax.experimental.pallas import tpu as pltpu

F32 = jnp.float32
BF16 = jnp.bfloat16
HI = lax.Precision.HIGHEST
MESH = pl.DeviceIdType.MESH
ANY = pl.BlockSpec(memory_space=pl.ANY)

V7X_VMEM_BYTES = 64 * 1024 * 1024
VMEM_LIMIT = V7X_VMEM_BYTES - 8 * 1024 * 1024
LANE = 128
PACK_W = 1024

GRID_W = 64
HEAD_DIM = 64
NA_HEADS, NA_ROWS, NA_COLS = 4, 8, 16
GQA_HEADS, GQA_KV_HEADS = 8, 2
MLA_HEADS, MLA_Q_RANK, MLA_KV_RANK, MLA_NOPE, MLA_ROPE, MLA_V = 4, 256, 128, 64, 32, 64
N_MOD = 9
ROPE_THETA = 10000.0
EPS = 1e-6
NEG_BIG = -1e30
NA_W = NA_HEADS * HEAD_DIM
GQ_W = GQA_HEADS * HEAD_DIM
GK_W = GQA_KV_HEADS * HEAD_DIM
MAIN_W = 3 * NA_W + GQ_W + 2 * GK_W + MLA_Q_RANK + MLA_KV_RANK + MLA_ROPE
MAIN_PAD = 2048
LOG2E, LN2 = float(np.log2(np.e)), float(np.log(2.0))
Q_SCALE = HEAD_DIM ** -0.5 * LOG2E
MLA_Q_SCALE = (MLA_NOPE + MLA_ROPE) ** -0.5 * LOG2E
TQ = 256
TM = 256

ADAM_LR, ADAM_B1, ADAM_B2, ADAM_EPS, ADAM_WD, ADAM_STEP = 0.001, 0.9, 0.999, 1e-08, 0.01, 10

ARG_NAMES = ['x', 'c', 'ctx', 'c_ctx', 'w_ada', 'b_ada', 'ffn1_norm', 'ffn1_w_gate', 'ffn1_w_up', 'ffn1_w_down', 'mix_norm', 'w_in',
             'na_rel_bias', 'gqa_q_norm', 'gqa_k_norm', 'mla_q_norm', 'mla_kv_norm', 'mla_w_uq', 'mla_w_ukv', 'w_branch_a',
             'w_branch_b', 'w_branch_c', 'w_out', 'ffn2_norm', 'ffn2_w_gate', 'ffn2_w_up', 'ffn2_w_down', 'final_norm']
WEIGHTS = ARG_NAMES[3:]
BIG = [('w_in', 1), ('mla_w_uq', 1), ('mla_w_ukv', 1), ('w_branch_a', 1), ('w_branch_b', 1), ('w_branch_c', 1), ('w_out', 0)]
GRAD_TRANSPOSED = ('w_in',)
FFN_NAMES = ['ffn1_w_gate', 'ffn1_w_up', 'ffn1_w_down', 'ffn2_w_gate', 'ffn2_w_up', 'ffn2_w_down']
FFN_TRANSPOSED = [True, True, False, True, True, False]
SMALL_LAYER = ['ffn1_norm', 'mix_norm', 'na_rel_bias', 'gqa_q_norm', 'gqa_k_norm', 'mla_q_norm', 'mla_kv_norm', 'ffn2_norm']


def _cp(*sem):
    return pltpu.CompilerParams(dimension_semantics=sem, vmem_limit_bytes=VMEM_LIMIT)


def _tile(dim, cands):
    for t in cands:
        if dim % t == 0:
            return t
    return dim


def _row_tile(rows, cap=512, mult=16):
    best = None
    for t in range(mult, min(rows, cap) + 1, mult):
        if rows % t == 0:
            best = t
    return best or rows


def _ceil_to(n, m):
    return -(-n // m) * m


def mm(a, b, *, name, ta=False, tb=False, out_dtype=F32, precise=False):
    m, k = (a.shape[1], a.shape[0]) if ta else a.shape
    n = b.shape[0] if tb else b.shape[1]
    tm = _tile(m, (512, 256, 128))
    tn = _tile(n, (1024, 1408, 512, 256, 128))
    tk = _tile(k, (1024, 1408, 512, 256, 128))
    nk = k // tk
    dims = (((0 if ta else 1,), (1 if tb else 0,)), ((), ()))

    def body(a_ref, b_ref, o_ref, *acc):
        if precise:
            part = lax.dot_general(a_ref[...].astype(F32), b_ref[...].astype(F32), dims, precision=HI, preferred_element_type=F32)
        else:
            part = lax.dot_general(a_ref[...].astype(BF16), b_ref[...].astype(BF16), dims, preferred_element_type=F32)
        if nk == 1:
            o_ref[...] = part.astype(o_ref.dtype)
        else:
            acc_ref, = acc
            kk = pl.program_id(2)

            @pl.when(kk == 0)
            def _():
                acc_ref[...] = part

            @pl.when(kk > 0)
            def _():
                acc_ref[...] += part

            @pl.when(kk == nk - 1)
            def _():
                o_ref[...] = acc_ref[...].astype(o_ref.dtype)

    a_spec = pl.BlockSpec((tk, tm), lambda i, j, kk: (kk, i)) if ta else pl.BlockSpec((tm, tk), lambda i, j, kk: (i, kk))
    b_spec = pl.BlockSpec((tn, tk), lambda i, j, kk: (j, kk)) if tb else pl.BlockSpec((tk, tn), lambda i, j, kk: (kk, j))
    return pl.pallas_call(
        body, name=name, grid=(m // tm, n // tn, nk), in_specs=[a_spec, b_spec],
        out_specs=pl.BlockSpec((tm, tn), lambda i, j, kk: (i, j)),
        out_shape=jax.ShapeDtypeStruct((m, n), out_dtype),
        scratch_shapes=[pltpu.VMEM((tm, tn), F32)] if nk > 1 else [],
        compiler_params=_cp("parallel", "parallel", "arbitrary"),
    )(a, b)


FFN_GATE, FFN_UP, FFN_DOWN = 0, 1, 2


def _ffn_wspec(wf, l, which):
    _, nsh, _, cs, d = wf.shape
    return pl.BlockSpec((None, nsh, None, cs, d), lambda *_: (l, 0, which, 0, 0), pipeline_mode=pl.Buffered(1))


def _ffn_group(cs):
    for g in (1, 2, 4):
        if (g * cs) % LANE == 0:
            return g
    raise ValueError(cs)


def ffn_up(n, wf, l, base, *, name):
    m, d = n.shape
    nsh, cs = wf.shape[1], wf.shape[3]
    f = nsh * cs
    grp = _ffn_group(cs)
    tm = _tile(m, (512, 256, 128))

    def body(n_ref, wg_ref, wu_ref, g_ref, u_ref, a_ref):
        nn = n_ref[...]
        for c in range(nsh // grp):
            cols = slice(grp * cs * c, grp * cs * (c + 1))
            g = _dot(nn, wg_ref[grp * c:grp * (c + 1)].reshape(grp * cs, d), _NT)
            u = _dot(nn, wu_ref[grp * c:grp * (c + 1)].reshape(grp * cs, d), _NT)
            g_ref[:, cols] = g.astype(BF16)
            u_ref[:, cols] = u.astype(BF16)
            a_ref[:, cols] = f_act_gu(g, u).astype(BF16)

    ospec = pl.BlockSpec((tm, f), lambda i: (i, 0))
    return pl.pallas_call(
        body, name=name, grid=(m // tm,),
        in_specs=[pl.BlockSpec((tm, d), lambda i: (i, 0)), _ffn_wspec(wf, l, base + FFN_GATE), _ffn_wspec(wf, l, base + FFN_UP)],
        out_specs=[ospec] * 3, out_shape=[jax.ShapeDtypeStruct((m, f), BF16)] * 3, compiler_params=_cp("parallel"),
    )(n, wf, wf)


def ffn_down(act, wf, l, base, *, name):
    m, f = act.shape
    nsh, cs, d = wf.shape[1], wf.shape[3], wf.shape[4]
    tm = _tile(m, (512, 256, 128))

    def body(a_ref, wd_ref, y_ref):
        y_ref[...] = _dot(a_ref[...], wd_ref[...].reshape(f, d))

    return pl.pallas_call(
        body, name=name, grid=(m // tm,),
        in_specs=[pl.BlockSpec((tm, f), lambda i: (i, 0)), _ffn_wspec(wf, l, base + FFN_DOWN)],
        out_specs=pl.BlockSpec((tm, d), lambda i: (i, 0)), out_shape=jax.ShapeDtypeStruct((m, d), F32), compiler_params=_cp("parallel"),
    )(act, wf)


def ffn_down_bwd(dy, g, u, wf, l, base, *, name):
    m, d = dy.shape
    nsh, cs = wf.shape[1], wf.shape[3]
    f = nsh * cs
    grp = _ffn_group(cs)
    tm = _tile(m, (512, 256, 128))

    def body(dy_ref, g_ref, u_ref, wd_ref, dg_ref, du_ref):
        dd = dy_ref[...]
        for c in range(nsh // grp):
            cols = slice(grp * cs * c, grp * cs * (c + 1))
            dact = _dot(dd, wd_ref[grp * c:grp * (c + 1)].reshape(grp * cs, d), _NT)
            dg, du = jax.vjp(f_act_gu, g_ref[:, cols].astype(F32), u_ref[:, cols].astype(F32))[1](dact)
            dg_ref[:, cols] = dg.astype(BF16)
            du_ref[:, cols] = du.astype(BF16)

    fspec = pl.BlockSpec((tm, f), lambda i: (i, 0))
    return pl.pallas_call(
        body, name=name, grid=(m // tm,),
        in_specs=[pl.BlockSpec((tm, d), lambda i: (i, 0)), fspec, fspec, _ffn_wspec(wf, l, base + FFN_DOWN)],
        out_specs=[fspec] * 2, out_shape=[jax.ShapeDtypeStruct((m, f), BF16)] * 2, compiler_params=_cp("parallel"),
    )(dy, g, u, wf)


def ffn_up_bwd(dg, du, wf, l, base, *, name):
    m, f = dg.shape
    nsh, cs, d = wf.shape[1], wf.shape[3], wf.shape[4]
    tm = _tile(m, (512, 256, 128))

    def body(dg_ref, du_ref, wg_ref, wu_ref, dn_ref):
        dn_ref[...] = _dot(dg_ref[...], wg_ref[...].reshape(f, d)) + _dot(du_ref[...], wu_ref[...].reshape(f, d))

    fspec = pl.BlockSpec((tm, f), lambda i: (i, 0))
    return pl.pallas_call(
        body, name=name, grid=(m // tm,),
        in_specs=[fspec, fspec, _ffn_wspec(wf, l, base + FFN_GATE), _ffn_wspec(wf, l, base + FFN_UP)],
        out_specs=pl.BlockSpec((tm, d), lambda i: (i, 0)), out_shape=jax.ShapeDtypeStruct((m, d), F32), compiler_params=_cp("parallel"),
    )(dg, du, wf, wf)


def ffn_dw(a, b, nsh, *, name):
    m, f = a.shape
    d = b.shape[1]
    cs = f // nsh
    grp = _ffn_group(cs)
    tm = _tile(m, (1024, 512, 256, 128))

    def body(a_ref, b_ref, o_ref):
        part = _dot(a_ref[...], b_ref[...], _TN).reshape(grp, cs, d)
        i = pl.program_id(1)

        @pl.when(i == 0)
        def _():
            o_ref[...] = part

        @pl.when(i > 0)
        def _():
            o_ref[...] += part

    return pl.pallas_call(
        body, name=name, grid=(nsh // grp, m // tm),
        in_specs=[pl.BlockSpec((tm, grp * cs), lambda j, i: (i, j)), pl.BlockSpec((tm, d), lambda j, i: (i, 0))],
        out_specs=pl.BlockSpec((grp, cs, d), lambda j, i: (j, 0, 0)), out_shape=jax.ShapeDtypeStruct((nsh, cs, d), F32),
        compiler_params=_cp("parallel", "arbitrary"),
    )(a, b)


def rowcall(name, fn, ins, outs, *, nb, nt, nct):
    in_specs, arrays = [], []
    for arr, kind in ins:
        arrays.append(arr)
        if kind == 'tok':
            in_specs.append(pl.BlockSpec((None, TM, arr.shape[-1]), lambda b, t: (b, t, 0)))
        elif kind == 'lat':
            in_specs.append(pl.BlockSpec((None, TM, arr.shape[-1]), lambda b, t: (b, jnp.maximum(t - nct, 0), 0)))
        elif kind == 'pos':
            in_specs.append(pl.BlockSpec((TM, arr.shape[-1]), lambda b, t: (t, 0)))
        elif kind == 'mod':
            in_specs.append(pl.BlockSpec((None, None, 1, arr.shape[-1]), lambda b, t: (b, jnp.where(t >= nct, 1, 0), 0, 0)))
        elif kind == 'full':
            in_specs.append(pl.BlockSpec(arr.shape, lambda b, t, nd=arr.ndim: (0,) * nd))
        else:
            _, w, j = kind
            in_specs.append(pl.BlockSpec((None, TM, w), lambda b, t, j=j: (b, t, j)))
    out_specs, out_shape = [], []
    for o in outs:
        if o[0] == 'tok':
            out_specs.append(pl.BlockSpec((None, TM, o[1]), lambda b, t: (b, t, 0)))
            out_shape.append(jax.ShapeDtypeStruct((nb, nt * TM, o[1]), o[2]))
        elif o[0] == 'mod':
            out_specs.append(pl.BlockSpec((None, None, 1, o[1]), lambda b, t: (b, jnp.where(t >= nct, 1, 0), 0, 0)))
            out_shape.append(jax.ShapeDtypeStruct((nb, 2, 1, o[1]), F32))
        else:
            out_specs.append(pl.BlockSpec(o[1], lambda b, t, nd=len(o[1]): (0,) * nd))
            out_shape.append(jax.ShapeDtypeStruct(o[1], F32))
    n_in = len(ins)

    def body(*refs):
        b, t = pl.program_id(0), pl.program_id(1)
        res = fn(t < nct, *[r[...] for r in refs[:n_in]])
        for ref, o, val in zip(refs[n_in:], outs, res, strict=True):
            if o[0] == 'tok':
                ref[...] = val.astype(ref.dtype)
                continue
            first = ((t == 0) | (t == nct)) if o[0] == 'mod' else ((b == 0) & (t == 0))

            @pl.when(first)
            def _(ref=ref, val=val):
                ref[...] = val

            @pl.when(jnp.logical_not(first))
            def _(ref=ref, val=val):
                ref[...] += val

    return pl.pallas_call(body, name=name, grid=(nb, nt), in_specs=in_specs, out_specs=out_specs, out_shape=out_shape,
                          compiler_params=_cp("arbitrary", "arbitrary"))(*arrays)


def _rms(x, g):
    return x * lax.rsqrt(jnp.mean(x * x, axis=-1, keepdims=True) + EPS) * g


def f_normmod(h, g, shift, scale):
    return _rms(h, g) * (1.0 + scale) + shift


def f_act_gu(g, u):
    return jax.nn.silu(g) * u


def _dot_split(x, m, dims):
    hi = x.astype(BF16)
    lo = (x - hi.astype(F32)).astype(BF16)
    mb = m.astype(BF16)
    return (lax.dot_general(hi, mb, dims, preferred_element_type=F32) + lax.dot_general(lo, mb, dims, preferred_element_type=F32))


def dot_select(x, m):
    return _dot_select(x, m)


@jax.custom_vjp
def _dot_select(x, m):
    return _dot_split(x, m, (((1,), (0,)), ((), ())))


_dot_select.defvjp(lambda x, m: (_dot_split(x, m, (((1,), (0,)), ((), ()))), m),
                   lambda m, ct: (_dot_split(ct, m, (((1,), (1,)), ((), ()))), jnp.zeros_like(m)))


def f_merge(ga, gb, gm, ya, yb, ym):
    return jax.nn.sigmoid(ga) * ya + jax.nn.sigmoid(gb) * yb + jax.nn.sigmoid(gm) * ym


def f_post(p, cb, sb, cm, sm, qn, kn, mqn, mkvn, wuq, wukv, s_b, r_b, t_b, r_m, rep, dup):
    def hnorm(x, g, w):
        ms = dot_select(x * x, s_b[:w, :w])
        gw = dot_select(g, t_b[:, :w])
        return x * lax.rsqrt(ms + EPS) * gw

    def rope(x, cos, sin, rot):
        return x * cos + dot_select(x, rot) * sin

    o = 3 * NA_W
    a_q, a_k, a_v = p[:, 0:NA_W], p[:, NA_W:2 * NA_W], p[:, 2 * NA_W:o]
    b_q = rope(hnorm(p[:, o:o + GQ_W], qn, GQ_W), cb, sb, r_b)
    o += GQ_W
    b_k = rope(hnorm(p[:, o:o + GK_W], kn, GK_W), cb[:, :GK_W], sb[:, :GK_W], r_b[:GK_W, :GK_W])
    b_v = p[:, o + GK_W:o + 2 * GK_W]
    o += 2 * GK_W
    q_lat = jnp.dot(_rms(p[:, o:o + MLA_Q_RANK], mqn).astype(BF16), wuq.astype(BF16), preferred_element_type=F32)
    o += MLA_Q_RANK
    kv_lat = jnp.dot(_rms(p[:, o:o + MLA_KV_RANK], mkvn).astype(BF16), wukv.astype(BF16), preferred_element_type=F32)
    o += MLA_KV_RANK
    nw = MLA_HEADS * MLA_NOPE
    mq_nope, mq_rope = q_lat[:, :nw], rope(q_lat[:, nw:], cm, sm, r_m)
    mk_nope, m_v = kv_lat[:, :nw], kv_lat[:, nw:]
    mk_rope = dot_select(rope(p[:, o:o + LANE], cm, sm, r_m), rep)
    b_k2 = dot_select(b_k, dup)
    b_v2 = dot_select(b_v, dup)
    return (a_q * Q_SCALE, a_k, a_v, b_q * Q_SCALE, b_k2, b_v2, mq_nope * MLA_Q_SCALE, mq_rope * MLA_Q_SCALE, mk_nope, mk_rope, m_v)


POST_QK = (0, 1, 3, 4, 6, 7, 8, 9)


POST_WIDTHS = (NA_W, NA_W, NA_W, GQ_W, 2 * GK_W, 2 * GK_W, MLA_HEADS * MLA_NOPE, MLA_HEADS * MLA_ROPE, MLA_HEADS * MLA_NOPE,
               MLA_HEADS * MLA_ROPE, MLA_HEADS * MLA_V)


_NT = (((1,), (1,)), ((), ()))
_TN = (((0,), (0,)), ((), ()))


def _dot(a, b, dims=None):
    if dims is None:
        return jnp.dot(a, b, preferred_element_type=F32)
    return lax.dot_general(a, b, dims, preferred_element_type=F32)


def _lanes(lo, width):
    lane = lax.broadcasted_iota(jnp.int32, (1, LANE), 1)
    return (lane >= lo) & (lane < lo + width)


def _only(x, mask):
    return jnp.where(mask, x, jnp.zeros_like(x))


def _pair_softmax(s):
    m = jnp.max(s, axis=-1, keepdims=True)
    p = jnp.exp2(s - m)
    l = jnp.sum(p, axis=-1, keepdims=True)
    return p, l, m + jnp.log2(l)


def gqa_fwd(q, k2, v2, *, lc, name):
    nb, t, qw = q.shape
    npair = qw // LANE
    per_kv = npair // GQA_KV_HEADS
    nctb = lc // TQ

    def body(q_ref, k_ref, v_ref, o_ref, lse_ref):
        i = pl.program_id(2)

        def run(rows):
            kk, vv = k_ref[rows, :], v_ref[rows, :]
            outs = []
            for e in range(2):
                p, l, lse = _pair_softmax(_dot(_only(q_ref[...], _lanes(HEAD_DIM * e, HEAD_DIM)), kk, _NT))
                outs.append(_dot(p.astype(BF16), vv) / l)
                lse_ref[e] = lse
            o_ref[...] = jnp.where(_lanes(0, HEAD_DIM), outs[0], outs[1]).astype(o_ref.dtype)

        @pl.when(i < nctb)
        def _():
            run(pl.ds(0, lc))

        @pl.when(i >= nctb)
        def _():
            run(pl.ds(0, t))

    qmap = lambda b, p, i: (b, i, p)
    kmap = lambda b, p, i: (b, 0, p // per_kv)
    return pl.pallas_call(
        body, name=name, grid=(nb, npair, t // TQ),
        in_specs=[pl.BlockSpec((None, TQ, LANE), qmap), pl.BlockSpec((None, t, LANE), kmap), pl.BlockSpec((None, t, LANE), kmap)],
        out_specs=[pl.BlockSpec((None, TQ, LANE), qmap), pl.BlockSpec((None, 2, TQ, 1), lambda b, p, i: (b, p, i, 0))],
        out_shape=[jax.ShapeDtypeStruct((nb, t, qw), BF16), jax.ShapeDtypeStruct((nb, 2 * npair, t, 1), F32)],
        compiler_params=_cp("parallel", "parallel", "arbitrary"),
    )(q, k2, v2)


def gqa_bwd(q, k2, v2, lse, do, *, lc, name):
    nb, t, qw = q.shape
    npair = qw // LANE
    per_kv = npair // GQA_KV_HEADS
    nctb = lc // TQ

    def body(q_ref, k_ref, v_ref, lse_ref, do_ref, dq_ref, dk_ref, dv_ref):
        g, i = pl.program_id(2), pl.program_id(3)

        @pl.when((g == 0) & (i == 0))
        def _():
            dk_ref[...] = jnp.zeros_like(dk_ref)
            dv_ref[...] = jnp.zeros_like(dv_ref)

        def run(rows):
            kk, vv = k_ref[rows, :], v_ref[rows, :]
            dqs = []
            for e in range(2):
                mine = _lanes(HEAD_DIM * e, HEAD_DIM)
                qq, dd = _only(q_ref[...], mine), _only(do_ref[...], mine)
                p = jnp.exp2(_dot(qq, kk, _NT) - lse_ref[e])
                dp = _dot(dd, vv, _NT)
                delta = jnp.sum(p * dp, axis=-1, keepdims=True)
                ds = (p * (dp - delta)).astype(BF16)
                dqs.append(_dot(ds, kk))
                dk_ref[rows, :] += _dot(ds, qq, _TN)
                dv_ref[rows, :] += _dot(p.astype(BF16), dd, _TN)
            dq_ref[...] = jnp.where(_lanes(0, HEAD_DIM), dqs[0], dqs[1])

        @pl.when(i < nctb)
        def _():
            run(pl.ds(0, lc))

        @pl.when(i >= nctb)
        def _():
            run(pl.ds(0, t))

    qmap = lambda b, j, g, i: (b, i, j * per_kv + g)
    kmap = lambda b, j, g, i: (b, 0, j)
    return pl.pallas_call(
        body, name=name, grid=(nb, GQA_KV_HEADS, per_kv, t // TQ),
        in_specs=[pl.BlockSpec((None, TQ, LANE), qmap), pl.BlockSpec((None, t, LANE), kmap), pl.BlockSpec((None, t, LANE), kmap),
                  pl.BlockSpec((None, 2, TQ, 1), lambda b, j, g, i: (b, j * per_kv + g, i, 0)), pl.BlockSpec((None, TQ, LANE), qmap)],
        out_specs=[pl.BlockSpec((None, TQ, LANE), qmap), pl.BlockSpec((None, t, LANE), kmap), pl.BlockSpec((None, t, LANE), kmap)],
        out_shape=[jax.ShapeDtypeStruct((nb, t, qw), F32), jax.ShapeDtypeStruct(k2.shape, F32), jax.ShapeDtypeStruct(v2.shape, F32)],
        compiler_params=_cp("arbitrary", "arbitrary", "arbitrary", "arbitrary"),
    )(q, k2, v2, lse, do)


def mla_fwd(qn, qr, kn, kr, v, *, lc, name):
    nb, t, w = qn.shape
    npair = w // LANE
    nctb = lc // TQ

    def body(qn_ref, qr_ref, kn_ref, kr_ref, v_ref, o_ref, lse_ref):
        pr, i = pl.program_id(1), pl.program_id(2)

        def run(rows):
            kk, kkr, vv = kn_ref[rows, :], kr_ref[rows, :], v_ref[rows, :]
            outs = []
            for e in range(2):
                s = (_dot(_only(qn_ref[...], _lanes(MLA_NOPE * e, MLA_NOPE)), kk, _NT)
                     + _dot(_only(qr_ref[...], _lanes(MLA_ROPE * (2 * pr + e), MLA_ROPE)), kkr, _NT))
                p, l, lse = _pair_softmax(s)
                outs.append(_dot(p.astype(BF16), vv) / l)
                lse_ref[e] = lse
            o_ref[...] = jnp.where(_lanes(0, MLA_V), outs[0], outs[1]).astype(o_ref.dtype)

        @pl.when(i < nctb)
        def _():
            run(pl.ds(0, lc))

        @pl.when(i >= nctb)
        def _():
            run(pl.ds(0, t))

    qmap = lambda b, p, i: (b, i, p)
    rmap = lambda b, p, i: (b, i, 0)
    kmap = lambda b, p, i: (b, 0, p)
    return pl.pallas_call(
        body, name=name, grid=(nb, npair, t // TQ),
        in_specs=[pl.BlockSpec((None, TQ, LANE), qmap), pl.BlockSpec((None, TQ, LANE), rmap), pl.BlockSpec((None, t, LANE), kmap),
                  pl.BlockSpec((None, t, LANE), lambda b, p, i: (b, 0, 0)), pl.BlockSpec((None, t, LANE), kmap)],
        out_specs=[pl.BlockSpec((None, TQ, LANE), qmap), pl.BlockSpec((None, 2, TQ, 1), lambda b, p, i: (b, p, i, 0))],
        out_shape=[jax.ShapeDtypeStruct((nb, t, w), BF16), jax.ShapeDtypeStruct((nb, 2 * npair, t, 1), F32)],
        compiler_params=_cp("parallel", "parallel", "arbitrary"),
    )(qn, qr, kn, kr, v)


def mla_bwd(qn, qr, kn, kr, v, lse, do, *, lc, name):
    nb, t, w = qn.shape
    npair = w // LANE
    nctb = lc // TQ

    def body(qn_ref, qr_ref, kn_ref, kr_ref, v_ref, lse_ref, do_ref, dqn_ref, dqr_ref, dkn_ref, dkr_ref, dv_ref):
        pr, i = pl.program_id(1), pl.program_id(2)

        @pl.when(i == 0)
        def _():
            dkn_ref[...] = jnp.zeros_like(dkn_ref)
            dv_ref[...] = jnp.zeros_like(dv_ref)

        @pl.when((i == 0) & (pr == 0))
        def _():
            dkr_ref[...] = jnp.zeros_like(dkr_ref)

        def run(rows):
            kk, kkr, vv = kn_ref[rows, :], kr_ref[rows, :], v_ref[rows, :]
            dqns, dqrs = [], []
            for e in range(2):
                mine, mine_r = _lanes(MLA_NOPE * e, MLA_NOPE), _lanes(MLA_ROPE * (2 * pr + e), MLA_ROPE)
                qq, qqr, dd = _only(qn_ref[...], mine), _only(qr_ref[...], mine_r), _only(do_ref[...], mine)
                p = jnp.exp2(_dot(qq, kk, _NT) + _dot(qqr, kkr, _NT) - lse_ref[e])
                dp = _dot(dd, vv, _NT)
                delta = jnp.sum(p * dp, axis=-1, keepdims=True)
                ds = (p * (dp - delta)).astype(BF16)
                dqns.append(_dot(ds, kk))
                dqrs.append(_only(_dot(ds, kkr), mine_r))
                dkn_ref[rows, :] += _dot(ds, qq, _TN)
                dkr_ref[rows, :] += _dot(ds, qqr, _TN)
                dv_ref[rows, :] += _dot(p.astype(BF16), dd, _TN)
            dqn_ref[...] = jnp.where(_lanes(0, MLA_NOPE), dqns[0], dqns[1])
            dqr_ref[...] = dqrs[0] + dqrs[1]

        @pl.when(i < nctb)
        def _():
            run(pl.ds(0, lc))

        @pl.when(i >= nctb)
        def _():
            run(pl.ds(0, t))

    qmap = lambda b, p, i: (b, i, p)
    rmap = lambda b, p, i: (b, i, 0)
    kmap = lambda b, p, i: (b, 0, p)
    zmap = lambda b, p, i: (b, 0, 0)
    return pl.pallas_call(
        body, name=name, grid=(nb, npair, t // TQ),
        in_specs=[pl.BlockSpec((None, TQ, LANE), qmap), pl.BlockSpec((None, TQ, LANE), rmap), pl.BlockSpec((None, t, LANE), kmap),
                  pl.BlockSpec((None, t, LANE), zmap), pl.BlockSpec((None, t, LANE), kmap),
                  pl.BlockSpec((None, 2, TQ, 1), lambda b, p, i: (b, p, i, 0)), pl.BlockSpec((None, TQ, LANE), qmap)],
        out_specs=[pl.BlockSpec((None, TQ, LANE), qmap), pl.BlockSpec((None, TQ, LANE), qmap), pl.BlockSpec((None, t, LANE), kmap),
                   pl.BlockSpec((None, t, LANE), zmap), pl.BlockSpec((None, t, LANE), kmap)],
        out_shape=[jax.ShapeDtypeStruct((nb, t, w), F32), jax.ShapeDtypeStruct((nb, t, npair * LANE), F32),
                   jax.ShapeDtypeStruct((nb, t, w), F32), jax.ShapeDtypeStruct((nb, t, LANE), F32), jax.ShapeDtypeStruct((nb, t, w), F32)],
        compiler_params=_cp("arbitrary", "arbitrary", "arbitrary"),
    )(qn, qr, kn, kr, v, lse, do)


def _na_window(st, nc, rows):
    r = jnp.maximum(st - nc, 0)
    r0 = jnp.clip(r - NA_ROWS // 2, 0, rows - NA_ROWS)
    return r, r0, r - r0


def na_fwd(q, k, v, bias, *, lc, name):
    nb, t, w = q.shape
    npair = w // LANE
    nc, rows = lc // GRID_W, (t - lc) // GRID_W
    nwin = NA_ROWS * GRID_W

    def body(q_ref, k_ref, v_ref, bias_ref, o_ref, lse_ref):
        st = pl.program_id(2)
        ctx = pl.ds(0, lc)
        kc, vc = k_ref[ctx, :], v_ref[ctx, :]
        outs = [None, None]

        @pl.when(st < nc)
        def _():
            for e in range(2):
                p, l, lse = _pair_softmax(_dot(_only(q_ref[...], _lanes(HEAD_DIM * e, HEAD_DIM)), kc, _NT))
                outs[e] = _dot(p.astype(BF16), vc) / l
                lse_ref[e] = lse
            o_ref[...] = jnp.where(_lanes(0, HEAD_DIM), outs[0], outs[1]).astype(o_ref.dtype)

        @pl.when(st >= nc)
        def _():
            _, r0, _ = _na_window(st, nc, rows)
            win = pl.ds(pl.multiple_of(lc + r0 * GRID_W, GRID_W), nwin)
            kw, vw = k_ref[win, :], v_ref[win, :]
            for e in range(2):
                qq = _only(q_ref[...], _lanes(HEAD_DIM * e, HEAD_DIM))
                s_loc = _dot(qq, kw, _NT) + bias_ref[e] * LOG2E
                s_ctx = _dot(qq, kc, _NT)
                m = jnp.maximum(jnp.max(s_loc, axis=-1, keepdims=True), jnp.max(s_ctx, axis=-1, keepdims=True))
                p_loc, p_ctx = jnp.exp2(s_loc - m), jnp.exp2(s_ctx - m)
                l = jnp.sum(p_loc, axis=-1, keepdims=True) + jnp.sum(p_ctx, axis=-1, keepdims=True)
                outs[e] = (_dot(p_loc.astype(BF16), vw) + _dot(p_ctx.astype(BF16), vc)) / l
                lse_ref[e] = m + jnp.log2(l)
            o_ref[...] = jnp.where(_lanes(0, HEAD_DIM), outs[0], outs[1]).astype(o_ref.dtype)

    qmap = lambda p, b, st: (b, st, p)
    kmap = lambda p, b, st: (b, 0, p)
    return pl.pallas_call(
        body, name=name, grid=(npair, nb, nc + rows),
        in_specs=[pl.BlockSpec((None, GRID_W, LANE), qmap), pl.BlockSpec((None, t, LANE), kmap), pl.BlockSpec((None, t, LANE), kmap),
                  pl.BlockSpec((2, None, GRID_W, nwin), lambda p, b, st: (p, _na_window(st, nc, rows)[2], 0, 0))],
        out_specs=[pl.BlockSpec((None, GRID_W, LANE), qmap), pl.BlockSpec((None, 2, GRID_W, 1), lambda p, b, st: (b, p, st, 0))],
        out_shape=[jax.ShapeDtypeStruct((nb, t, w), BF16), jax.ShapeDtypeStruct((nb, 2 * npair, t, 1), F32)],
        compiler_params=_cp("parallel", "parallel", "arbitrary"),
    )(q, k, v, bias)


def na_bwd(q, k, v, bias, lse, do, *, lc, name):
    nb, t, w = q.shape
    npair = w // LANE
    nc, rows = lc // GRID_W, (t - lc) // GRID_W
    nwin = NA_ROWS * GRID_W

    def body(q_ref, k_ref, v_ref, bias_ref, lse_ref, do_ref, dq_ref, dk_ref, dv_ref, db_ref):
        b, st = pl.program_id(1), pl.program_id(2)

        @pl.when(st == 0)
        def _():
            dk_ref[...] = jnp.zeros_like(dk_ref)
            dv_ref[...] = jnp.zeros_like(dv_ref)

        @pl.when((st == 0) & (b == 0))
        def _():
            db_ref[...] = jnp.zeros_like(db_ref)

        ctx = pl.ds(0, lc)
        kc, vc = k_ref[ctx, :], v_ref[ctx, :]
        dqs = [None, None]

        @pl.when(st < nc)
        def _():
            for e in range(2):
                mine = _lanes(HEAD_DIM * e, HEAD_DIM)
                qq, dd = _only(q_ref[...], mine), _only(do_ref[...], mine)
                p = jnp.exp2(_dot(qq, kc, _NT) - lse_ref[e])
                dp = _dot(dd, vc, _NT)
                delta = jnp.sum(p * dp, axis=-1, keepdims=True)
                ds = (p * (dp - delta)).astype(BF16)
                dqs[e] = _dot(ds, kc)
                dk_ref[ctx, :] += _dot(ds, qq, _TN)
                dv_ref[ctx, :] += _dot(p.astype(BF16), dd, _TN)
            dq_ref[...] = jnp.where(_lanes(0, HEAD_DIM), dqs[0], dqs[1])

        @pl.when(st >= nc)
        def _():
            _, r0, case = _na_window(st, nc, rows)
            win = pl.ds(pl.multiple_of(lc + r0 * GRID_W, GRID_W), nwin)
            kw, vw = k_ref[win, :], v_ref[win, :]
            for e in range(2):
                mine = _lanes(HEAD_DIM * e, HEAD_DIM)
                qq, dd = _only(q_ref[...], mine), _only(do_ref[...], mine)
                p_loc = jnp.exp2(_dot(qq, kw, _NT) + bias_ref[e] * LOG2E - lse_ref[e])
                p_ctx = jnp.exp2(_dot(qq, kc, _NT) - lse_ref[e])
                dp_loc, dp_ctx = _dot(dd, vw, _NT), _dot(dd, vc, _NT)
                delta = jnp.sum(p_loc * dp_loc, axis=-1, keepdims=True) + jnp.sum(p_ctx * dp_ctx, axis=-1, keepdims=True)
                ds_loc = p_loc * (dp_loc - delta)
                db_ref[e, case] += ds_loc
                ds_loc = ds_loc.astype(BF16)
                ds_ctx = (p_ctx * (dp_ctx - delta)).astype(BF16)
                dqs[e] = _dot(ds_loc, kw) + _dot(ds_ctx, kc)
                dk_ref[win, :] += _dot(ds_loc, qq, _TN)
                dk_ref[ctx, :] += _dot(ds_ctx, qq, _TN)
                dv_ref[win, :] += _dot(p_loc.astype(BF16), dd, _TN)
                dv_ref[ctx, :] += _dot(p_ctx.astype(BF16), dd, _TN)
            dq_ref[...] = jnp.where(_lanes(0, HEAD_DIM), dqs[0], dqs[1])

    qmap = lambda p, b, st: (b, st, p)
    kmap = lambda p, b, st: (b, 0, p)
    return pl.pallas_call(
        body, name=name, grid=(npair, nb, nc + rows),
        in_specs=[pl.BlockSpec((None, GRID_W, LANE), qmap), pl.BlockSpec((None, t, LANE), kmap), pl.BlockSpec((None, t, LANE), kmap),
                  pl.BlockSpec((2, None, GRID_W, nwin), lambda p, b, st: (p, _na_window(st, nc, rows)[2], 0, 0)),
                  pl.BlockSpec((None, 2, GRID_W, 1), lambda p, b, st: (b, p, st, 0)), pl.BlockSpec((None, GRID_W, LANE), qmap)],
        out_specs=[pl.BlockSpec((None, GRID_W, LANE), qmap), pl.BlockSpec((None, t, LANE), kmap), pl.BlockSpec((None, t, LANE), kmap),
                   pl.BlockSpec((2, NA_ROWS, GRID_W, nwin), lambda p, b, st: (p, 0, 0, 0))],
        out_shape=[jax.ShapeDtypeStruct((nb, t, w), F32), jax.ShapeDtypeStruct((nb, t, w), F32), jax.ShapeDtypeStruct((nb, t, w), F32),
                   jax.ShapeDtypeStruct((2 * npair, NA_ROWS, GRID_W, nwin), F32)],
        compiler_params=_cp("arbitrary", "arbitrary", "arbitrary"),
    )(q, k, v, bias, lse, do)


def _na_tables():
    cols = np.arange(GRID_W)
    c0 = np.clip(cols - NA_COLS // 2, 0, GRID_W - NA_COLS)
    col_in = (cols[None, :] >= c0[:, None]) & (cols[None, :] < c0[:, None] + NA_COLS)
    dc = np.clip(cols[None, :] - cols[:, None] + NA_COLS - 1, 0, 2 * NA_COLS - 2)
    dr = np.arange(NA_ROWS)[None, :] + (NA_ROWS - 1) - np.arange(NA_ROWS)[:, None]
    return col_in, dc, dr


def _na_onehots():
    col_in, dc, dr = _na_tables()
    e1 = np.zeros((GRID_W, GRID_W, LANE), np.float32)
    qi, ki = np.nonzero(col_in)
    e1[qi, ki, dc[qi, ki]] = 1.0
    e2 = np.zeros((2 * NA_ROWS, NA_ROWS, NA_ROWS), np.float32)
    ci, ji = np.meshgrid(np.arange(NA_ROWS), np.arange(NA_ROWS), indexing='ij')
    e2[dr[ci, ji], ci, ji] = 1.0
    return jnp.asarray(e1.reshape(GRID_W * GRID_W, LANE)), jnp.asarray(e2.reshape(2 * NA_ROWS, NA_ROWS * NA_ROWS)), col_in


def na_expand_bias(rel_bias, name):
    e1, e2, col_in = _na_onehots()
    nh = rel_bias.shape[0]
    nrow = NA_ROWS * NA_ROWS
    rel = jnp.pad(rel_bias, ((0, 0), (0, 1), (0, LANE - rel_bias.shape[2])))
    rel = rel.transpose(1, 0, 2).reshape(2 * NA_ROWS, nh * LANE)
    y = mm(e2, rel, ta=True, name=name + "_rows", precise=True)
    y = y.reshape(nrow, nh, LANE).transpose(1, 0, 2).reshape(nh * nrow, LANE)
    g = mm(y, e1, tb=True, name=name + "_cols", precise=True)
    g = g.reshape(nh, NA_ROWS, NA_ROWS, GRID_W, GRID_W).transpose(0, 1, 3, 2, 4)
    g = jnp.where(col_in[None, None, :, None, :], g, NEG_BIG)
    return g.reshape(nh, NA_ROWS, GRID_W, NA_ROWS * GRID_W)


def na_reduce_bias(dexp, name):
    e1, e2, _ = _na_onehots()
    nh = dexp.shape[0]
    x = dexp.reshape(nh, NA_ROWS, GRID_W, NA_ROWS, GRID_W).transpose(0, 1, 3, 2, 4).reshape(nh * NA_ROWS * NA_ROWS, GRID_W * GRID_W)
    y = mm(x, e1, name=name + "_cols", precise=True)
    y = y.reshape(nh, NA_ROWS * NA_ROWS, LANE).transpose(1, 0, 2).reshape(NA_ROWS * NA_ROWS, nh * LANE)
    z = mm(e2, y, name=name + "_rows", precise=True)
    return z.reshape(2 * NA_ROWS, nh, LANE).transpose(1, 0, 2)[:, :2 * NA_ROWS - 1, :2 * NA_COLS - 1]


def _rot_matrix(width, d_rot):
    f = d_rot // 4
    r = np.zeros((width, width), np.float32)
    for base in range(0, width, d_rot // 2):
        for j in range(f):
            r[base + f + j, base + j] = -1.0
            r[base + j, base + f + j] = 1.0
    return r


def _rope_tables(s_len, lc, d_rot, reps):
    half = d_rot // 2
    freqs = ROPE_THETA ** (-jnp.arange(0, half, 2, dtype=F32) / half)
    tpos = jnp.arange(s_len)
    row = (tpos // GRID_W).astype(F32)[:, None] * freqs
    col = (tpos % GRID_W).astype(F32)[:, None] * freqs
    ang = jnp.concatenate([row, row, col, col], axis=-1)
    cos = jnp.concatenate([jnp.ones((lc, d_rot), F32), jnp.cos(ang)], axis=0)
    sin = jnp.concatenate([jnp.zeros((lc, d_rot), F32), jnp.sin(ang)], axis=0)
    return jnp.tile(cos, (1, reps)), jnp.tile(sin, (1, reps))


def _post_consts():
    s_b = np.kron(np.eye(GQA_HEADS, dtype=np.float32), np.full((HEAD_DIM, HEAD_DIM), 1.0 / HEAD_DIM, np.float32))
    t_b = np.tile(np.eye(HEAD_DIM, dtype=np.float32), (1, GQA_HEADS))
    r_b = _rot_matrix(GQ_W, HEAD_DIM)
    r_m = _rot_matrix(LANE, MLA_ROPE)
    rep = np.zeros((LANE, LANE), np.float32)
    for h in range(MLA_HEADS):
        rep[np.arange(MLA_ROPE), h * MLA_ROPE + np.arange(MLA_ROPE)] = 1.0
    dup = np.zeros((GK_W, 2 * GK_W), np.float32)
    for j in range(GQA_KV_HEADS):
        for e in range(2):
            dup[HEAD_DIM * j + np.arange(HEAD_DIM), 2 * HEAD_DIM * j + HEAD_DIM * e + np.arange(HEAD_DIM)] = 1.0
    return tuple(jnp.asarray(a) for a in (s_b, r_b, t_b, r_m, rep, dup))


def _heads_to_parts(w, first):
    r = w.shape[0]
    w3 = w.reshape(r, MLA_HEADS, -1)
    return jnp.concatenate([w3[:, :, :first].reshape(r, -1), w3[:, :, first:].reshape(r, -1)], axis=1)


def _parts_to_heads(w, first):
    r = w.shape[0]
    nf = MLA_HEADS * first
    return jnp.concatenate([w[:, :nf].reshape(r, MLA_HEADS, first), w[:, nf:].reshape(r, MLA_HEADS, -1)], axis=2).reshape(r, -1)


def _place():
    return lax.axis_index("x"), lax.axis_index("y"), lax.axis_index("c")


def all_gather(v, *, name, with_c):
    flips = [(dx, dy, dc) for dx in (0, 1) for dy in (0, 1) for dc in ((0, 1) if with_c else (0,))][1:]
    n = len(flips) + 1

    def body(v_ref, out_ref, send_sems, recv_sems, local_sem):
        mx, my, mc = _place()

        def slot(px, py, pc):
            return 4 * px + 2 * py + pc if with_c else 2 * px + py

        mine = pltpu.make_async_copy(v_ref, out_ref.at[slot(mx, my, mc)], local_sem)
        mine.start()
        sends = []
        for j, (dx, dy, dc) in enumerate(flips):
            peer = (mx ^ dx, my ^ dy, mc ^ dc)
            cp = pltpu.make_async_remote_copy(src_ref=v_ref, dst_ref=out_ref.at[slot(mx, my, mc)], send_sem=send_sems.at[j],
                                              recv_sem=recv_sems.at[j], device_id=peer, device_id_type=MESH)
            cp.start()
            sends.append(cp)
        for j, (dx, dy, dc) in enumerate(flips):
            peer = (mx ^ dx, my ^ dy, mc ^ dc)
            pltpu.make_async_remote_copy(src_ref=v_ref, dst_ref=out_ref.at[slot(*peer)], send_sem=send_sems.at[j],
                                         recv_sem=recv_sems.at[j], device_id=peer, device_id_type=MESH).wait_recv()
        for cp in sends:
            cp.wait_send()
        mine.wait()

    return pl.pallas_call(
        body, name=name, in_specs=[ANY], out_specs=ANY, out_shape=jax.ShapeDtypeStruct((n,) + v.shape, v.dtype),
        scratch_shapes=[pltpu.SemaphoreType.DMA((n - 1,)), pltpu.SemaphoreType.DMA((n - 1,)), pltpu.SemaphoreType.DMA(())],
    )(v)


def gather_shards(v, *, name):
    _, h, w = v.shape
    flips = [(1, 0), (0, 1), (1, 1)]

    def body(v_ref, out_ref, send_sems, recv_sems):
        mx, my, mc = _place()
        me = 2 * mx + my
        sib = (mx, my, 1 - mc)

        def copy(k, src, dst, to):
            return pltpu.make_async_remote_copy(src_ref=src, dst_ref=dst, send_sem=send_sems.at[k], recv_sem=recv_sems.at[k],
                                                device_id=to, device_id_type=MESH)

        first = [copy(j, v_ref.at[mc], out_ref.at[me, mc], (mx ^ dx, my ^ dy, mc)) for j, (dx, dy) in enumerate(flips)]
        for cp in first:
            cp.start()
        passed = []
        for j, (dx, dy) in enumerate(flips):
            theirs = out_ref.at[2 * (mx ^ dx) + (my ^ dy), mc]
            copy(j, v_ref.at[mc], theirs, (mx ^ dx, my ^ dy, mc)).wait_recv()
            fw = copy(3 + j, theirs, theirs, sib)
            fw.start()
            passed.append(fw)
        for j, (dx, dy) in enumerate(flips):
            other = out_ref.at[2 * (mx ^ dx) + (my ^ dy), 1 - mc]
            copy(3 + j, other, other, sib).wait_recv()
        for cp in first + passed:
            cp.wait_send()

    out = pl.pallas_call(
        body, name=name, in_specs=[ANY], out_specs=ANY, out_shape=jax.ShapeDtypeStruct((4, 2, h, w), v.dtype),
        scratch_shapes=[pltpu.SemaphoreType.DMA((6,)), pltpu.SemaphoreType.DMA((6,))],
    )(v)
    mx, my, _ = _place()
    return lax.dynamic_update_slice(out, v[None], (2 * mx + my, 0, 0, 0))


def pair_exchange_halves(g, *, name):
    n, _, h, w = g.shape

    def body(g_ref, out_ref, send_sems, recv_sems):
        mx, my, mc = _place()
        sib = (mx, my, 1 - mc)
        cps = [pltpu.make_async_remote_copy(src_ref=g_ref.at[s, 1 - mc], dst_ref=out_ref.at[s], send_sem=send_sems.at[s],
                                            recv_sem=recv_sems.at[s], device_id=sib, device_id_type=MESH) for s in range(n)]
        for cp in cps:
            cp.start()
        for cp in cps:
            cp.wait_recv()
        for cp in cps:
            cp.wait_send()

    return pl.pallas_call(
        body, name=name, in_specs=[ANY], out_specs=ANY, out_shape=jax.ShapeDtypeStruct((n, h, w), g.dtype),
        scratch_shapes=[pltpu.SemaphoreType.DMA((n,)), pltpu.SemaphoreType.DMA((n,))],
    )(g)


def all_to_all_xy(v, *, name):
    def body(v_ref, out_ref, send_sems, recv_sems):
        mx, my, mc = _place()
        me = 2 * mx + my
        flips = [(1, 0), (0, 1), (1, 1)]
        sends = []
        for j, (dx, dy) in enumerate(flips):
            px, py = mx ^ dx, my ^ dy
            cp = pltpu.make_async_remote_copy(src_ref=v_ref.at[2 * px + py], dst_ref=out_ref.at[me], send_sem=send_sems.at[j],
                                              recv_sem=recv_sems.at[j], device_id=(px, py, mc), device_id_type=MESH)
            cp.start()
            sends.append(cp)
        for j, (dx, dy) in enumerate(flips):
            px, py = mx ^ dx, my ^ dy
            pltpu.make_async_remote_copy(src_ref=v_ref.at[me], dst_ref=out_ref.at[2 * px + py], send_sem=send_sems.at[j],
                                         recv_sem=recv_sems.at[j], device_id=(px, py, mc), device_id_type=MESH).wait_recv()
        for cp in sends:
            cp.wait_send()

    out = pl.pallas_call(
        body, name=name, in_specs=[ANY], out_specs=ANY, out_shape=jax.ShapeDtypeStruct(v.shape, v.dtype),
        scratch_shapes=[pltpu.SemaphoreType.DMA((3,)), pltpu.SemaphoreType.DMA((3,))],
    )(v)
    mx, my, _ = _place()
    me = 2 * mx + my
    return lax.dynamic_update_slice(out, lax.dynamic_slice_in_dim(v, me, 1, axis=0), (me, 0, 0))


def pair_all_gather(v, *, name):
    def body(v_ref, out_ref, send_sem, recv_sem):
        mx, my, mc = _place()
        cp = pltpu.make_async_remote_copy(src_ref=v_ref, dst_ref=out_ref.at[mc], send_sem=send_sem, recv_sem=recv_sem,
                                          device_id=(mx, my, 1 - mc), device_id_type=MESH)
        cp.start()
        pltpu.make_async_remote_copy(src_ref=v_ref, dst_ref=out_ref.at[1 - mc], send_sem=send_sem, recv_sem=recv_sem,
                                     device_id=(mx, my, 1 - mc), device_id_type=MESH).wait_recv()
        cp.wait_send()

    out = pl.pallas_call(
        body, name=name, in_specs=[ANY], out_specs=ANY, out_shape=jax.ShapeDtypeStruct((2,) + v.shape, v.dtype),
        scratch_shapes=[pltpu.SemaphoreType.DMA(()), pltpu.SemaphoreType.DMA(())],
    )(v)
    return lax.dynamic_update_slice(out, v[None], (_place()[2], 0, 0))


def gather_ffn(wl, *, name):
    nl, nblk, cs, d = wl.shape
    assert nl == 2
    flips = [(1, 0), (0, 1), (1, 1)]

    def body(v_ref, out_ref, send_sems, recv_sems):
        mx, my, mc = _place()
        me = 2 * mx + my
        sib = (mx, my, 1 - mc)

        def copy(k, src, dst, to):
            return pltpu.make_async_remote_copy(src_ref=src, dst_ref=dst, send_sem=send_sems.at[k], recv_sem=recv_sems.at[k],
                                                device_id=to, device_id_type=MESH)

        first = [copy(j, v_ref.at[mc], out_ref.at[mc, me], (mx ^ dx, my ^ dy, mc)) for j, (dx, dy) in enumerate(flips)]
        for cp in first:
            cp.start()
        passed = []
        for j, (dx, dy) in enumerate(flips):
            theirs = out_ref.at[mc, 2 * (mx ^ dx) + (my ^ dy)]
            copy(j, v_ref.at[mc], theirs, (mx ^ dx, my ^ dy, mc)).wait_recv()
            fw = copy(3 + j, theirs, theirs, sib)
            fw.start()
            passed.append(fw)
        for j, (dx, dy) in enumerate(flips):
            other = out_ref.at[1 - mc, 2 * (mx ^ dx) + (my ^ dy)]
            copy(3 + j, other, other, sib).wait_recv()
        for cp in first + passed:
            cp.wait_send()

    out = pl.pallas_call(
        body, name=name, in_specs=[ANY], out_specs=ANY, out_shape=jax.ShapeDtypeStruct((nl, 4, nblk, cs, d), wl.dtype),
        scratch_shapes=[pltpu.SemaphoreType.DMA((6,)), pltpu.SemaphoreType.DMA((6,))],
    )(wl)
    mx, my, _ = _place()
    return lax.dynamic_update_slice(out, wl[:, None], (0, 2 * mx + my, 0, 0, 0))


def reduce_ffn(g0, g1, *, name):
    nt = len(g0)
    nsh, cs, d = g0[0].shape
    flips = [(1, 0), (0, 1), (1, 1)]
    mx, my, mc = _place()
    me = 2 * mx + my
    c_idx = jnp.reshape(mc, (1,)).astype(jnp.int32)

    def pair_body(*refs):
        ins0, ins1, outs = refs[:nt], refs[nt:2 * nt], refs[2 * nt:3 * nt]
        send_sems, recv_sems = refs[3 * nt:]
        kx, ky, kc = _place()
        sib = (kx, ky, 1 - kc)
        for c in range(2):
            @pl.when(kc == c)
            def _(c=c):
                mine_out = (ins1, ins0)[c]
                cps = [pltpu.make_async_remote_copy(src_ref=mine_out[t], dst_ref=outs[t], send_sem=send_sems.at[t],
                                                    recv_sem=recv_sems.at[t], device_id=sib, device_id_type=MESH) for t in range(nt)]
                for cp in cps:
                    cp.start()
                for cp in cps:
                    cp.wait_recv()
                for cp in cps:
                    cp.wait_send()

    from_pair = pl.pallas_call(
        pair_body, name=name + "_pair", in_specs=[ANY] * (2 * nt), out_specs=[ANY] * nt,
        out_shape=[jax.ShapeDtypeStruct((nsh, cs, d), F32)] * nt,
        scratch_shapes=[pltpu.SemaphoreType.DMA((nt,)), pltpu.SemaphoreType.DMA((nt,))],
    )(*g0, *g1)

    tr = _row_tile(cs, 64)

    def add_body(c_ref, *refs):
        for t in range(nt):
            mine = jnp.where(c_ref[0] == 0, refs[t][...], refs[nt + t][...])
            refs[3 * nt + t][...] = (mine + refs[2 * nt + t][...]).astype(BF16)

    spec = pl.BlockSpec((None, tr, d), lambda s, i, c_ref: (s, i, 0))
    chip_sum = pl.pallas_call(
        add_body, name=name + "_pair_add",
        grid_spec=pltpu.PrefetchScalarGridSpec(num_scalar_prefetch=1, grid=(nsh, cs // tr), in_specs=[spec] * (3 * nt),
                                               out_specs=[spec] * nt),
        out_shape=[jax.ShapeDtypeStruct((nsh, cs, d), BF16)] * nt, compiler_params=_cp("parallel", "parallel"),
    )(c_idx, *g0, *g1, *from_pair)

    def xy_body(*refs):
        ins, outs = refs[:nt], refs[nt:2 * nt]
        send_sems, recv_sems = refs[2 * nt:]
        kx, ky, kc = _place()
        k_me = 2 * kx + ky
        sends = []
        for j, (dx, dy) in enumerate(flips):
            px, py = kx ^ dx, ky ^ dy
            for t in range(nt):
                cp = pltpu.make_async_remote_copy(src_ref=ins[t].at[2 * px + py], dst_ref=outs[t].at[k_me],
                                                  send_sem=send_sems.at[j * nt + t], recv_sem=recv_sems.at[j * nt + t],
                                                  device_id=(px, py, kc), device_id_type=MESH)
                cp.start()
                sends.append(cp)
        for j, (dx, dy) in enumerate(flips):
            px, py = kx ^ dx, ky ^ dy
            for t in range(nt):
                pltpu.make_async_remote_copy(src_ref=ins[t].at[k_me], dst_ref=outs[t].at[2 * px + py],
                                             send_sem=send_sems.at[j * nt + t], recv_sem=recv_sems.at[j * nt + t],
                                             device_id=(px, py, kc), device_id_type=MESH).wait_recv()
        for cp in sends:
            cp.wait_send()

    from_xy = pl.pallas_call(
        xy_body, name=name + "_xy", in_specs=[ANY] * nt, out_specs=[ANY] * nt,
        out_shape=[jax.ShapeDtypeStruct((nsh, cs, d), BF16)] * nt,
        scratch_shapes=[pltpu.SemaphoreType.DMA((3 * nt,)), pltpu.SemaphoreType.DMA((3 * nt,))],
    )(*chip_sum)
    from_xy = [lax.dynamic_update_slice(o, lax.dynamic_slice_in_dim(v, me, 1, axis=0), (me, 0, 0)) for o, v in zip(from_xy, chip_sum)]

    def sum_body(*refs):
        for t in range(nt):
            acc = refs[t][0].astype(F32)
            for s in range(1, nsh):
                acc = acc + refs[t][s].astype(F32)
            refs[nt + t][...] = acc

    reduced = pl.pallas_call(
        sum_body, name=name + "_xy_add", grid=(cs // tr,), in_specs=[pl.BlockSpec((nsh, tr, d), lambda i: (0, i, 0))] * nt,
        out_specs=[pl.BlockSpec((tr, d), lambda i: (i, 0))] * nt, out_shape=[jax.ShapeDtypeStruct((cs, d), F32)] * nt,
        compiler_params=_cp("parallel"),
    )(*from_xy)

    def share_body(*refs):
        ins, outs = refs[:nt], refs[nt:2 * nt]
        send_sems, recv_sems = refs[2 * nt:]
        kx, ky, kc = _place()
        sib = (kx, ky, 1 - kc)
        cps = [pltpu.make_async_remote_copy(src_ref=ins[t], dst_ref=outs[t].at[kc], send_sem=send_sems.at[t],
                                            recv_sem=recv_sems.at[t], device_id=sib, device_id_type=MESH) for t in range(nt)]
        for cp in cps:
            cp.start()
        for t in range(nt):
            pltpu.make_async_remote_copy(src_ref=ins[t], dst_ref=outs[t].at[1 - kc], send_sem=send_sems.at[t],
                                         recv_sem=recv_sems.at[t], device_id=sib, device_id_type=MESH).wait_recv()
        for cp in cps:
            cp.wait_send()

    both = pl.pallas_call(
        share_body, name=name + "_share", in_specs=[ANY] * nt, out_specs=[ANY] * nt,
        out_shape=[jax.ShapeDtypeStruct((2, cs, d), F32)] * nt,
        scratch_shapes=[pltpu.SemaphoreType.DMA((nt,)), pltpu.SemaphoreType.DMA((nt,))],
    )(*reduced)
    return [lax.dynamic_update_slice(o, v[None], (mc, 0, 0)) for o, v in zip(both, reduced)]


def add_kept_half(g, r, c_idx, *, name, out_dtype):
    n, _, h, w = g.shape
    th = _row_tile(h)

    def body(c_ref, g_ref, r_ref, o_ref):
        o_ref[...] = (g_ref[...] + r_ref[...]).astype(o_ref.dtype)

    return pl.pallas_call(
        body, name=name,
        grid_spec=pltpu.PrefetchScalarGridSpec(
            num_scalar_prefetch=1, grid=(n, h // th),
            in_specs=[pl.BlockSpec((None, None, th, w), lambda s, i, c_ref: (s, c_ref[0], i, 0)),
                      pl.BlockSpec((None, th, w), lambda s, i, c_ref: (s, i, 0))],
            out_specs=pl.BlockSpec((None, th, w), lambda s, i, c_ref: (s, i, 0))),
        out_shape=jax.ShapeDtypeStruct((n, h, w), out_dtype), compiler_params=_cp("parallel", "parallel"),
    )(c_idx, g, r)


def sum_slots(v, *, name):
    n, rows, w = v.shape
    tr = _row_tile(rows, 256)

    def body(v_ref, o_ref):
        acc = v_ref[0].astype(F32)
        for s in range(1, n):
            acc = acc + v_ref[s].astype(F32)
        o_ref[...] = acc

    return pl.pallas_call(body, name=name, grid=(rows // tr,), in_specs=[pl.BlockSpec((n, tr, w), lambda i: (0, i, 0))],
                          out_specs=pl.BlockSpec((tr, w), lambda i: (i, 0)), out_shape=jax.ShapeDtypeStruct((rows, w), F32),
                          compiler_params=_cp("parallel"))(v)


def ada_fwd(c_rows, w_ada, b_shard, *, name):
    nl, d, ncol = w_ada.shape
    rows = c_rows.shape[0]
    tn = _tile(ncol, (768, 512, 256, 128))

    def body(c_ref, w_ref, b_ref, o_ref):
        o_ref[...] = jnp.dot(jax.nn.silu(c_ref[...]), w_ref[...], precision=HI, preferred_element_type=F32) + b_ref[...]

    return pl.pallas_call(
        body, name=name, grid=(nl, ncol // tn),
        in_specs=[pl.BlockSpec((rows, d), lambda l, j: (0, 0)), pl.BlockSpec((None, d, tn), lambda l, j: (l, 0, j)),
                  pl.BlockSpec((None, 1, tn), lambda l, j: (l, 0, j))],
        out_specs=pl.BlockSpec((None, rows, tn), lambda l, j: (l, 0, j)),
        out_shape=jax.ShapeDtypeStruct((nl, rows, ncol), F32), compiler_params=_cp("parallel", "parallel"),
    )(c_rows, w_ada, b_shard)


def ada_bwd(c_rows, w_ada, dm_shard, dm_full, n_ex, *, name):
    nl, d, ncol = w_ada.shape
    rows = c_rows.shape[0]
    tn = _tile(ncol, (768, 512, 256, 128))
    nj = ncol // tn

    def body(c_ref, w_ref, dm_ref, dmf_ref, gw_ref, gb_ref, dc_ref, dact_ref):
        l, j = pl.program_id(0), pl.program_id(1)
        act, act_vjp = jax.vjp(jax.nn.silu, c_ref[...])
        gw_ref[...] = lax.dot_general(act, dm_ref[...], _TN, precision=HI, preferred_element_type=F32)
        gb_ref[...] = jnp.sum(dmf_ref[...], axis=0, keepdims=True)
        part = lax.dot_general(dm_ref[...], w_ref[...], _NT, precision=HI, preferred_element_type=F32)

        @pl.when((l == 0) & (j == 0))
        def _():
            dact_ref[...] = part

        @pl.when((l > 0) | (j > 0))
        def _():
            dact_ref[...] += part

        @pl.when((l == nl - 1) & (j == nj - 1))
        def _():
            dc, = act_vjp(dact_ref[...])
            dc_ref[...] = jnp.sum(dc[n_ex:, :], axis=0, keepdims=True)

    return pl.pallas_call(
        body, name=name, grid=(nl, nj),
        in_specs=[pl.BlockSpec((rows, d), lambda l, j: (0, 0)), pl.BlockSpec((None, d, tn), lambda l, j: (l, 0, j)),
                  pl.BlockSpec((None, rows, tn), lambda l, j: (l, 0, j)),
                  pl.BlockSpec((None, rows, dm_full.shape[-1]), lambda l, j: (l, 0, 0))],
        out_specs=[pl.BlockSpec((None, d, tn), lambda l, j: (l, 0, j)),
                   pl.BlockSpec((None, 1, dm_full.shape[-1]), lambda l, j: (l, 0, 0)),
                   pl.BlockSpec((1, d), lambda l, j: (0, 0))],
        out_shape=[jax.ShapeDtypeStruct((nl, d, ncol), F32), jax.ShapeDtypeStruct((nl, 1, dm_full.shape[-1]), F32),
                   jax.ShapeDtypeStruct((1, d), F32)],
        scratch_shapes=[pltpu.VMEM((rows, d), F32)], compiler_params=_cp("arbitrary", "arbitrary"),
    )(c_rows, w_ada, dm_shard, dm_full)


def adamw(w, g, m, v, *, name):
    shape = w.shape
    cols = shape[-1]
    rows = int(np.prod(shape[:-1])) if len(shape) > 1 else 1
    tr = _row_tile(rows, 256)

    def body(w_ref, g_ref, m_ref, v_ref, d_ref, nm_ref, nv_ref):
        gg = g_ref[...]
        nm = ADAM_B1 * m_ref[...] + (1.0 - ADAM_B1) * gg
        nv = ADAM_B2 * v_ref[...] + (1.0 - ADAM_B2) * jnp.square(gg)
        m_hat = nm / (1.0 - ADAM_B1 ** ADAM_STEP)
        v_hat = nv / (1.0 - ADAM_B2 ** ADAM_STEP)
        d_ref[...] = -ADAM_LR * (m_hat / (jnp.sqrt(v_hat) + ADAM_EPS) + ADAM_WD * w_ref[...])
        nm_ref[...] = nm
        nv_ref[...] = nv

    spec = pl.BlockSpec((tr, cols), lambda i: (i, 0))
    out = pl.pallas_call(body, name=name, grid=(rows // tr,), in_specs=[spec] * 4, out_specs=[spec] * 3,
                         out_shape=[jax.ShapeDtypeStruct((rows, cols), F32)] * 3, compiler_params=_cp("parallel"),
                         )(*[a.reshape(rows, cols) for a in (w, g, m, v)])
    return tuple(o.reshape(shape) for o in out)


def local_step(h0, target, mods, lw, wf, small, *, lc):
    nb, t, d = h0.shape
    nt, nct = t // TM, lc // TM
    s_len = t - lc
    nl = len(lw)
    nsh = wf.shape[1]
    consts = _post_consts()
    cos_b, sin_b = _rope_tables(s_len, lc, HEAD_DIM, GQA_HEADS)
    cos_m, sin_m = _rope_tables(s_len, lc, MLA_ROPE, MLA_HEADS)
    rc = functools.partial(rowcall, nb=nb, nt=nt, nct=nct)
    flat = lambda a: a.reshape(nb * t, a.shape[-1])
    unflat = lambda a: a.reshape(nb, t, a.shape[-1])
    vec = lambda a: a.reshape(1, -1)

    def ffn_fwd(h, g, mod3, l, base, tag):
        shift, scale, gate = mod3
        n, = rc(tag + "_norm", lambda _, *a: (f_normmod(*a),), [(h, 'tok'), (vec(g), 'full'), (shift, 'mod'), (scale, 'mod')],
                [('tok', d, BF16)])
        gg, uu, act = ffn_up(flat(n), wf, l, base, name=tag + "_up")
        y = unflat(ffn_down(act, wf, l, base, name=tag + "_down"))
        h2, = rc(tag + "_res", lambda _, hh, yy, gt: (hh + 0.5 * gt * yy,), [(h, 'tok'), (y, 'tok'), (gate, 'mod')], [('tok', d, F32)])
        return h2, (h, n, gg, uu, act, y)

    def ffn_bwd(dh2, saved, g, mod3, l, base, tag):
        shift, scale, gate = mod3
        h, n, gg, uu, act, y = saved
        dy, dgate = rc(tag + "_res_bwd", lambda _, dd, yy, gt: (0.5 * gt * dd, jnp.sum(0.5 * yy * dd, axis=0, keepdims=True)),
                       [(dh2, 'tok'), (y, 'tok'), (gate, 'mod')], [('tok', d, BF16), ('mod', d)])
        dw_d = ffn_dw(act, flat(dy), nsh, name=tag + "_down_dw")
        dgg, duu = ffn_down_bwd(flat(dy), gg, uu, wf, l, base, name=tag + "_down_dx")
        dw_g = ffn_dw(dgg, flat(n), nsh, name=tag + "_gate_dw")
        dw_u = ffn_dw(duu, flat(n), nsh, name=tag + "_up_dw")
        dn = unflat(ffn_up_bwd(dgg, duu, wf, l, base, name=tag + "_up_dx"))

        def norm_bwd(_, hh, gn, sh, sc, dnn, dres):
            dh, dg, dsh, dsc = jax.vjp(f_normmod, hh, gn, sh, sc)[1](dnn)
            return dh + dres, dg, dsh, dsc

        dh, dg, dshift, dscale = rc(tag + "_norm_bwd", norm_bwd,
                                    [(h, 'tok'), (vec(g), 'full'), (shift, 'mod'), (scale, 'mod'), (dn, 'tok'), (dh2, 'tok')],
                                    [('tok', d, F32), ('full', (1, d)), ('mod', d), ('mod', d)])
        return dh, dg.reshape(d), (dshift, dscale, dgate), [dw_g, dw_u, dw_d]

    def post_ins(p, sm, w):
        return [(p, ('tokc', MAIN_PAD, 0)), (cos_b, 'pos'), (sin_b, 'pos'), (cos_m, 'pos'), (sin_m, 'pos'),
                (vec(sm['gqa_q_norm']), 'full'), (vec(sm['gqa_k_norm']), 'full'), (vec(sm['mla_q_norm']), 'full'),
                (vec(sm['mla_kv_norm']), 'full'), (w['w_uq'], 'full'), (w['w_ukv'], 'full')] + [(c, 'full') for c in consts]

    def mix_fwd(h, sm, mod3, w, tag):
        shift, scale, gate = mod3
        n, = rc(tag + "_norm", lambda _, *a: (f_normmod(*a),), [(h, 'tok'), (vec(sm['mix_norm']), 'full'), (shift, 'mod'), (scale, 'mod')],
                [('tok', d, BF16)])
        p = unflat(mm(flat(n), w['w_in'], name=tag + "_in"))
        parts = rc(tag + "_post", lambda _, *a: f_post(*a), post_ins(p, sm, w), [('tok', wd, BF16) for wd in POST_WIDTHS])
        aq, ak, av, bq, bk, bv, mqn, mqr, mkn, mkr, mv = parts
        bias = na_expand_bias(sm['na_rel_bias'], tag + "_bias")
        o_a, lse_a = na_fwd(aq, ak, av, bias, lc=lc, name=tag + "_na")
        o_b, lse_b = gqa_fwd(bq, bk, bv, lc=lc, name=tag + "_gqa")
        o_m, lse_m = mla_fwd(mqn, mqr, mkn, mkr, mv, lc=lc, name=tag + "_mla")
        fo = [o_a, o_b, o_m]
        ys = [unflat(mm(flat(o), w[k], name=tag + "_br" + k[-1])) for o, k in zip(fo, ('w_a', 'w_b', 'w_c'))]
        gcols = [(p, ('tokc', d, MAIN_PAD // d + j)) for j in range(3)]
        y, = rc(tag + "_merge", lambda _, *a: (f_merge(*a),), gcols + [(v, 'tok') for v in ys], [('tok', d, BF16)])
        z = unflat(mm(flat(y), w['w_o'], name=tag + "_out"))
        h2, = rc(tag + "_res", lambda _, hh, zz, gt: (hh + gt * zz,), [(h, 'tok'), (z, 'tok'), (gate, 'mod')], [('tok', d, F32)])
        saved = (h, n, p, (aq, ak, av, lse_a, bias), (bq, bk, bv, lse_b), (mqn, mqr, mkn, mkr, mv, lse_m), fo, ys, y, z)
        return h2, saved

    def mix_bwd(dh2, saved, sm, mod3, w, tag):
        shift, scale, gate = mod3
        h, n, p, (aq, ak, av, lse_a, bias), (bq, bk, bv, lse_b), (mqn, mqr, mkn, mkr, mv, lse_m), fo, ys, y, z = saved
        dz, dgate = rc(tag + "_res_bwd", lambda _, dd, zz, gt: (gt * dd, jnp.sum(zz * dd, axis=0, keepdims=True)),
                       [(dh2, 'tok'), (z, 'tok'), (gate, 'mod')], [('tok', d, BF16), ('mod', d)])
        dw_o = mm(flat(y), flat(dz), ta=True, name=tag + "_out_dw")
        dy = unflat(mm(flat(dz), w['w_o'], tb=True, name=tag + "_out_dx"))
        gcols = [(p, ('tokc', d, MAIN_PAD // d + j)) for j in range(3)]

        def merge_bwd(_, ga, gb, gm, ya, yb, ym, dyy):
            dga, dgb, dgm, dya, dyb, dym = jax.vjp(f_merge, ga, gb, gm, ya, yb, ym)[1](dyy)
            return dya, dyb, dym, jnp.concatenate([dga, dgb, dgm], axis=-1)

        dya, dyb, dym, dgl = rc(tag + "_merge_bwd", merge_bwd, gcols + [(v, 'tok') for v in ys] + [(dy, 'tok')],
                                [('tok', d, BF16)] * 3 + [('tok', 3 * d, BF16)])
        dws, dos = {}, []
        for o, dyk, k in zip(fo, (dya, dyb, dym), ('w_a', 'w_b', 'w_c')):
            dws[k] = mm(flat(o), flat(dyk), ta=True, name=tag + "_br" + k[-1] + "_dw")
            dos.append(unflat(mm(flat(dyk), w[k], tb=True, out_dtype=BF16, name=tag + "_br" + k[-1] + "_dx")))
        do_a, do_b, do_m = dos
        daq, dak, dav, dbias = na_bwd(aq, ak, av, bias, lse_a, do_a, lc=lc, name=tag + "_na_bwd")
        dbq, dbk, dbv = gqa_bwd(bq, bk, bv, lse_b, do_b, lc=lc, name=tag + "_gqa_bwd")
        dmqn, dmqr2, dmkn, dmkr, dmv = mla_bwd(mqn, mqr, mkn, mkr, mv, lse_m, do_m, lc=lc, name=tag + "_mla_bwd")
        d_rel = na_reduce_bias(dbias, tag + "_relb")
        cots = [daq, dak, dav, dbq, dbk, dbv, dmqn, dmqr2, dmkn, dmkr, dmv]
        ins = post_ins(p, sm, w)
        n_in = len(ins)

        def post_bwd(_, *a):
            prim, cot, dgl_v = a[:11], list(a[n_in:n_in + 11]), a[-1]
            cot[7] = cot[7][:, :LANE] + cot[7][:, LANE:]
            for j in POST_QK:
                cot[j] = cot[j] * LN2
            outs = jax.vjp(lambda pp, qn, kn, mqn, mkvn, wuq, wukv: f_post(pp, *prim[1:5], qn, kn, mqn, mkvn, wuq, wukv, *a[11:n_in]),
                           prim[0], *prim[5:11])[1](tuple(cot))
            return (jnp.concatenate([outs[0].astype(BF16), dgl_v], axis=-1),) + tuple(outs[1:])

        res = rc(tag + "_post_bwd", post_bwd, ins + [(cv, 'tok') for cv in cots] + [(dgl, 'tok')],
                 [('tok', MAIN_PAD + 3 * d, BF16), ('full', (1, HEAD_DIM)), ('full', (1, HEAD_DIM)), ('full', (1, MLA_Q_RANK)),
                  ('full', (1, MLA_KV_RANK)), ('full', w['w_uq'].shape), ('full', w['w_ukv'].shape)])
        dp, dqn, dkn, dmqn, dmkvn, dw_uq, dw_ukv = res
        dw_in = mm(flat(dp), flat(n), ta=True, name=tag + "_in_dw")
        dn = unflat(mm(flat(dp), w['w_in'], tb=True, name=tag + "_in_dx"))

        def norm_bwd(_, hh, gg, sh, sc, dnn, dres):
            dh, dg, dsh, dsc = jax.vjp(f_normmod, hh, gg, sh, sc)[1](dnn)
            return dh + dres, dg, dsh, dsc

        dh, dg, dshift, dscale = rc(tag + "_norm_bwd", norm_bwd,
                                    [(h, 'tok'), (vec(sm['mix_norm']), 'full'), (shift, 'mod'), (scale, 'mod'), (dn, 'tok'), (dh2, 'tok')],
                                    [('tok', d, F32), ('full', (1, d)), ('mod', d), ('mod', d)])
        dsm = {'mix_norm': dg.reshape(d), 'na_rel_bias': d_rel, 'gqa_q_norm': dqn.reshape(-1), 'gqa_k_norm': dkn.reshape(-1),
               'mla_q_norm': dmqn.reshape(-1), 'mla_kv_norm': dmkvn.reshape(-1)}
        dwl = {'w_in': dw_in, 'w_uq': dw_uq, 'w_ukv': dw_ukv, 'w_o': dw_o, **dws}
        return dh, dsm, (dshift, dscale, dgate), dwl

    h = h0
    saved = []
    for l in range(nl):
        sm = {k: small[k][l] for k in SMALL_LAYER}
        h, s1 = ffn_fwd(h, sm['ffn1_norm'], mods[l][0:3], l, 0, f"l{l}_ffn1")
        h, s2 = mix_fwd(h, sm, mods[l][3:6], lw[l], f"l{l}_mix")
        h, s3 = ffn_fwd(h, sm['ffn2_norm'], mods[l][6:9], l, 3, f"l{l}_ffn2")
        saved.append((sm, s1, s2, s3))

    def final(is_ctx, hh, gg, tgt):
        def loss_fn(hv, gv):
            return 0.5 * jnp.sum(jnp.mean(jnp.square(_rms(hv, gv) - tgt), axis=-1))

        keep = jnp.where(is_ctx, 0.0, 1.0)
        loss, (dh, dg) = jax.value_and_grad(loss_fn, argnums=(0, 1))(hh, gg)
        return dh * keep, jnp.full((1, LANE), loss * keep, F32), dg * keep

    dh, loss, dg_final = rc("final_loss", final, [(h, 'tok'), (vec(small['final_norm']), 'full'), (target, 'lat')],
                            [('tok', d, F32), ('full', (1, LANE)), ('full', (1, d))])

    dsmall = {k: [None] * nl for k in SMALL_LAYER}
    dmods, dlw, dwf = [None] * nl, [None] * nl, [None] * nl
    for l in reversed(range(nl)):
        sm, s1, s2, s3 = saved[l]
        dh, dg3, dm3, dwf2 = ffn_bwd(dh, s3, sm['ffn2_norm'], mods[l][6:9], l, 3, f"l{l}_ffn2")
        dh, dsm, dm2, dlw[l] = mix_bwd(dh, s2, sm, mods[l][3:6], lw[l], f"l{l}_mix")
        dh, dg1, dm1, dwf1 = ffn_bwd(dh, s1, sm['ffn1_norm'], mods[l][0:3], l, 0, f"l{l}_ffn1")
        dmods[l] = list(dm1) + list(dm2) + list(dm3)
        dwf[l] = dwf1 + dwf2
        dsm.update(ffn1_norm=dg1, ffn2_norm=dg3)
        for k in SMALL_LAYER:
            dsmall[k][l] = dsm[k]
    dsmall = {k: jnp.stack(v) for k, v in dsmall.items()}
    dsmall['final_norm'] = dg_final.reshape(d)
    return loss, dh, dmods, dlw, dwf, dsmall


def _pack(parts, pad_rows):
    flat, where, off = [], [], 0
    for a in parts:
        n = _ceil_to(a.size, PACK_W)
        flat.append(jnp.pad(a.reshape(-1), (0, n - a.size)))
        where.append((off, n // PACK_W))
        off += n // PACK_W
    total = _ceil_to(off, pad_rows)
    if total > off:
        flat.append(jnp.zeros(((total - off) * PACK_W,), flat[0].dtype))
    return jnp.concatenate(flat).reshape(total, PACK_W), where


def _unpack(buf, where, shape):
    off, rows = where
    return buf[off:off + rows].reshape(-1)[:int(np.prod(shape))].reshape(shape)


def layer_weights(full, l):
    wi = full['w_in'][l]
    d = wi.shape[0]
    return {
        'w_in': jnp.concatenate([wi[:, :MAIN_W], jnp.zeros((d, MAIN_PAD - MAIN_W), wi.dtype), wi[:, MAIN_W:]], axis=1),
        'w_uq': _heads_to_parts(full['mla_w_uq'][l], MLA_NOPE).astype(F32),
        'w_ukv': _heads_to_parts(full['mla_w_ukv'][l], MLA_NOPE).astype(F32),
        'w_a': full['w_branch_a'][l], 'w_b': full['w_branch_b'][l], 'w_c': full['w_branch_c'][l], 'w_o': full['w_out'][l]}


def layer_grads_by_name(dlw):
    per_name = {k: [] for k, _ in BIG}
    for g in dlw:
        per_name['w_in'].append(jnp.concatenate([g['w_in'][:MAIN_W], g['w_in'][MAIN_PAD:]], axis=0))
        per_name['mla_w_uq'].append(_parts_to_heads(g['w_uq'], MLA_NOPE))
        per_name['mla_w_ukv'].append(_parts_to_heads(g['w_ukv'], MLA_NOPE))
        per_name['w_branch_a'].append(g['w_a'])
        per_name['w_branch_b'].append(g['w_b'])
        per_name['w_branch_c'].append(g['w_c'])
        per_name['w_out'].append(g['w_o'])
    return per_name


def kernel(x, c, ctx, c_ctx, w_ada, b_ada, ffn1_norm, ffn1_w_gate, ffn1_w_up, ffn1_w_down, mix_norm, w_in, na_rel_bias, gqa_q_norm, gqa_k_norm, mla_q_norm, mla_kv_norm, mla_w_uq, mla_w_ukv, w_branch_a, w_branch_b, w_branch_c, w_out, ffn2_norm, ffn2_w_gate, ffn2_w_up, ffn2_w_down, final_norm, loss_target, m_c_ctx, m_w_ada, m_b_ada, m_ffn1_norm, m_ffn1_w_gate, m_ffn1_w_up, m_ffn1_w_down, m_mix_norm, m_w_in, m_na_rel_bias, m_gqa_q_norm, m_gqa_k_norm, m_mla_q_norm, m_mla_kv_norm, m_mla_w_uq, m_mla_w_ukv, m_w_branch_a, m_w_branch_b, m_w_branch_c, m_w_out, m_ffn2_norm, m_ffn2_w_gate, m_ffn2_w_up, m_ffn2_w_down, m_final_norm, v_c_ctx, v_w_ada, v_b_ada, v_ffn1_norm, v_ffn1_w_gate, v_ffn1_w_up, v_ffn1_w_down, v_mix_norm, v_w_in, v_na_rel_bias, v_gqa_q_norm, v_gqa_k_norm, v_mla_q_norm, v_mla_kv_norm, v_mla_w_uq, v_mla_w_ukv, v_w_branch_a, v_w_branch_b, v_w_branch_c, v_w_out, v_ffn2_norm, v_ffn2_w_gate, v_ffn2_w_up, v_ffn2_w_down, v_final_norm):
    args = locals()
    wts = {k: args[k] for k in WEIGHTS}
    mom = {k: args['m_' + k] for k in WEIGHTS}
    var = {k: args['v_' + k] for k in WEIGHTS}
    nb, s_len, d = x.shape
    lc = ctx.shape[1]
    nl = w_ada.shape[0]
    nsh, ndev = 4, 8
    mx, my, mc = _place()
    sidx = 2 * mx + my
    didx = 4 * mx + 2 * my + mc
    assert d % LANE == 0 and MAIN_PAD % d == 0 and lc % TQ == 0 and s_len % TQ == 0 and s_len // GRID_W >= NA_ROWS

    wpack, wwhere = _pack([wts[k].astype(BF16) for k, _ in BIG], 32)
    wall = gather_shards(wpack.reshape(2, -1, PACK_W), name="gather_weights").reshape(nsh, -1, PACK_W)
    full = {}
    for (k, ax), wh in zip(BIG, wwhere):
        shp = wts[k].shape
        parts = jnp.stack([_unpack(wall[s], wh, shp) for s in range(nsh)])
        if ax == 1:
            full[k] = parts.transpose(1, 2, 0, 3).reshape(nl, shp[1], nsh * shp[2])
        else:
            full[k] = parts.transpose(1, 0, 2, 3).reshape(nl, nsh * shp[1], shp[2])
    lw = [layer_weights(full, l) for l in range(nl)]
    wl = jnp.stack([(wts[k].transpose(0, 2, 1) if tr else wts[k]).astype(BF16) for k, tr in zip(FFN_NAMES, FFN_TRANSPOSED)], axis=1)
    wf = gather_ffn(wl, name="gather_ffn")

    n_ex = ndev * nb
    ncol = w_ada.shape[-1]
    c_all = all_gather(c, name="gather_cond", with_c=True).reshape(n_ex, d)
    c_rows = jnp.concatenate([c_all, jnp.broadcast_to(c_ctx[None], (n_ex, d))], axis=0)
    b_shard = lax.dynamic_slice_in_dim(b_ada, sidx * ncol, ncol, axis=1)[:, None, :]
    mod_sh = ada_fwd(c_rows, w_ada, b_shard, name="ada_fwd")
    mod_all = all_gather(mod_sh, name="gather_mod", with_c=False)
    mod_all = mod_all.transpose(1, 2, 0, 3).reshape(nl, 2 * n_ex, nsh * ncol)
    mod_x = lax.dynamic_slice_in_dim(mod_all, didx * nb, nb, axis=1)
    mod_c = jnp.broadcast_to(mod_all[:, n_ex:n_ex + 1], mod_x.shape)
    mods = [[jnp.stack([mod_c[l, :, j * d:(j + 1) * d], mod_x[l, :, j * d:(j + 1) * d]], axis=1)[:, :, None, :]
             for j in range(N_MOD)] for l in range(nl)]

    small = {k: wts[k] for k in SMALL_LAYER + ['final_norm']}
    h0 = jnp.concatenate([ctx, x], axis=1)
    loss_part, dh0, dmods, dlw, dwf, dsmall = local_step(h0, loss_target, mods, lw, wf, small, lc=lc)
    grad_x = dh0[:, lc:]

    dmod_mine = jnp.stack([jnp.concatenate([m[:, :, 0, :] for m in dmods[l]], axis=-1) for l in range(nl)])
    small_names = SMALL_LAYER + ['final_norm']
    spack, swhere = _pack([loss_part] + [dsmall[k] for k in small_names] + [dmod_mine], 8)
    sall = all_gather(spack, name="gather_small", with_c=True)
    ssum = sum_slots(sall, name="sum_small")
    loss = _unpack(ssum, swhere[0], (1, LANE))[0, 0]
    grads = {k: _unpack(ssum, wh, wts[k].shape) for k, wh in zip(small_names, swhere[1:])}
    off, rows = swhere[-1]
    dm_all = sall[:, off:off + rows].reshape(ndev, -1)[:, :dmod_mine.size].reshape((ndev,) + dmod_mine.shape)
    dm_all = dm_all.transpose(1, 3, 0, 2, 4).reshape(nl, 2, n_ex, N_MOD * d)
    dm_rows = jnp.concatenate([dm_all[:, 1], dm_all[:, 0]], axis=1)
    dm_shard = lax.dynamic_slice_in_dim(dm_rows, sidx * ncol, ncol, axis=2)
    grads['w_ada'], gb, dc_part = ada_bwd(c_rows, w_ada, dm_shard, dm_rows, n_ex, name="ada_bwd")
    grads['b_ada'] = gb.reshape(b_ada.shape)
    dc_all = all_gather(jnp.pad(dc_part, ((0, 7), (0, 0))), name="gather_dcond", with_c=False)
    grads['c_ctx'] = sum_slots(dc_all, name="sum_dcond")[0]

    per_name = layer_grads_by_name(dlw)
    pieces, gwhere, off = [], [], 0
    for k, ax in BIG:
        shp = wts[k].shape
        for g in per_name[k]:
            if ax == 1 and k not in GRAD_TRANSPOSED:
                pieces.append(g.reshape(shp[1], nsh, shp[2]).transpose(1, 0, 2).reshape(nsh, -1))
            else:
                pieces.append(g.reshape(nsh, -1))
        n = int(np.prod(shp))
        if n % PACK_W:
            pieces.append(jnp.zeros((nsh, _ceil_to(n, PACK_W) - n), F32))
        gwhere.append((off, _ceil_to(n, PACK_W) // PACK_W))
        off += _ceil_to(n, PACK_W) // PACK_W
    if off % 128:
        pieces.append(jnp.zeros((nsh, (_ceil_to(off, 128) - off) * PACK_W), F32))
    half = _ceil_to(off, 128) // 2
    gpack = jnp.concatenate(pieces, axis=1).reshape(nsh, 2, half, PACK_W)
    from_pair = pair_exchange_halves(gpack, name="reduce_pair")
    chip_sum = add_kept_half(gpack, from_pair, jnp.reshape(mc, (1,)).astype(jnp.int32), name="reduce_pair_add",
                             out_dtype=BF16)
    from_xy = all_to_all_xy(chip_sum, name="reduce_xy")
    reduced = sum_slots(from_xy, name="reduce_xy_add")
    gfull = pair_all_gather(reduced, name="reduce_share").reshape(2 * half, PACK_W)
    for (k, _), wh in zip(BIG, gwhere):
        shp = wts[k].shape
        grads[k] = (_unpack(gfull, wh, (shp[0], shp[2], shp[1])).transpose(0, 2, 1) if k in GRAD_TRANSPOSED
                    else _unpack(gfull, wh, shp))
    for k, tr, g in zip(FFN_NAMES, FFN_TRANSPOSED, reduce_ffn(dwf[0], dwf[1], name="reduce_ffn")):
        grads[k] = g.transpose(0, 2, 1) if tr else g

    outs = {k: adamw(wts[k], grads[k], mom[k], var[k], name="adamw_" + k) for k in WEIGHTS}
    return (loss, grad_x, *[grads[k] for k in WEIGHTS], *[outs[k][0] for k in WEIGHTS], *[outs[k][1] for k in WEIGHTS],
            *[outs[k][2] for k in WEIGHTS])
```

```python
import functools

import jax
import jax.numpy as jnp
import numpy as np
from jax import lax
from jax.experimental import pallas as pl
from jax.experimental.pallas import tpu as pltpu

F32 = jnp.float32
BF16 = jnp.bfloat16
HI = lax.Precision.HIGHEST
MESH = pl.DeviceIdType.MESH
ANY = pl.BlockSpec(memory_space=pl.ANY)

V7X_VMEM_BYTES = 64 * 1024 * 1024
VMEM_LIMIT = V7X_VMEM_BYTES - 8 * 1024 * 1024
LANE = 128
PACK_W = 1024

GRID_W = 64
HEAD_DIM = 64
NA_HEADS, NA_ROWS, NA_COLS = 4, 8, 16
GQA_HEADS, GQA_KV_HEADS = 8, 2
MLA_HEADS, MLA_Q_RANK, MLA_KV_RANK, MLA_NOPE, MLA_ROPE, MLA_V = 4, 256, 128, 64, 32, 64
N_MOD = 9
ROPE_THETA = 10000.0
EPS = 1e-6
NEG_BIG = -1e30
NA_W = NA_HEADS * HEAD_DIM
GQ_W = GQA_HEADS * HEAD_DIM
GK_W = GQA_KV_HEADS * HEAD_DIM
MAIN_W = 3 * NA_W + GQ_W + 2 * GK_W + MLA_Q_RANK + MLA_KV_RANK + MLA_ROPE
MAIN_PAD = 2048
LOG2E, LN2 = float(np.log2(np.e)), float(np.log(2.0))
Q_SCALE = HEAD_DIM ** -0.5 * LOG2E
MLA_Q_SCALE = (MLA_NOPE + MLA_ROPE) ** -0.5 * LOG2E
TQ = 256
TM = 256

ADAM_LR, ADAM_B1, ADAM_B2, ADAM_EPS, ADAM_WD, ADAM_STEP = 0.001, 0.9, 0.999, 1e-08, 0.01, 10

ARG_NAMES = ['x', 'c', 'ctx', 'c_ctx', 'w_ada', 'b_ada', 'ffn1_norm', 'ffn1_w_gate', 'ffn1_w_up', 'ffn1_w_down', 'mix_norm', 'w_in',
             'na_rel_bias', 'gqa_q_norm', 'gqa_k_norm', 'mla_q_norm', 'mla_kv_norm', 'mla_w_uq', 'mla_w_ukv', 'w_branch_a',
             'w_branch_b', 'w_branch_c', 'w_out', 'ffn2_norm', 'ffn2_w_gate', 'ffn2_w_up', 'ffn2_w_down', 'final_norm']
WEIGHTS = ARG_NAMES[3:]
BIG = [('w_in', 1), ('mla_w_uq', 1), ('mla_w_ukv', 1), ('w_branch_a', 1), ('w_branch_b', 1), ('w_branch_c', 1), ('w_out', 0)]
GRAD_TRANSPOSED = ('w_in',)
FFN_NAMES = ['ffn1_w_gate', 'ffn1_w_up', 'ffn1_w_down', 'ffn2_w_gate', 'ffn2_w_up', 'ffn2_w_down']
FFN_TRANSPOSED = [True, True, False, True, True, False]
SMALL_LAYER = ['ffn1_norm', 'mix_norm', 'na_rel_bias', 'gqa_q_norm', 'gqa_k_norm', 'mla_q_norm', 'mla_kv_norm', 'ffn2_norm']


def _cp(*sem):
    return pltpu.CompilerParams(dimension_semantics=sem, vmem_limit_bytes=VMEM_LIMIT)


def _tile(dim, cands):
    for t in cands:
        if dim % t == 0:
            return t
    return dim


def _row_tile(rows, cap=512, mult=16):
    best = None
    for t in range(mult, min(rows, cap) + 1, mult):
        if rows % t == 0:
            best = t
    return best or rows


def _ceil_to(n, m):
    return -(-n // m) * m


def mm(a, b, *, name, ta=False, tb=False, out_dtype=F32, precise=False):
    m, k = (a.shape[1], a.shape[0]) if ta else a.shape
    n = b.shape[0] if tb else b.shape[1]
    tm = _tile(m, (512, 256, 128))
    tn = _tile(n, (1024, 1408, 512, 256, 128))
    tk = _tile(k, (1024, 1408, 512, 256, 128))
    nk = k // tk
    dims = (((0 if ta else 1,), (1 if tb else 0,)), ((), ()))

    def body(a_ref, b_ref, o_ref, *acc):
        if precise:
            part = lax.dot_general(a_ref[...].astype(F32), b_ref[...].astype(F32), dims, precision=HI, preferred_element_type=F32)
        else:
            part = lax.dot_general(a_ref[...].astype(BF16), b_ref[...].astype(BF16), dims, preferred_element_type=F32)
        if nk == 1:
            o_ref[...] = part.astype(o_ref.dtype)
        else:
            acc_ref, = acc
            kk = pl.program_id(2)

            @pl.when(kk == 0)
            def _():
                acc_ref[...] = part

            @pl.when(kk > 0)
            def _():
                acc_ref[...] += part

            @pl.when(kk == nk - 1)
            def _():
                o_ref[...] = acc_ref[...].astype(o_ref.dtype)

    a_spec = pl.BlockSpec((tk, tm), lambda i, j, kk: (kk, i)) if ta else pl.BlockSpec((tm, tk), lambda i, j, kk: (i, kk))
    b_spec = pl.BlockSpec((tn, tk), lambda i, j, kk: (j, kk)) if tb else pl.BlockSpec((tk, tn), lambda i, j, kk: (kk, j))
    return pl.pallas_call(
        body, name=name, grid=(m // tm, n // tn, nk), in_specs=[a_spec, b_spec],
        out_specs=pl.BlockSpec((tm, tn), lambda i, j, kk: (i, j)),
        out_shape=jax.ShapeDtypeStruct((m, n), out_dtype),
        scratch_shapes=[pltpu.VMEM((tm, tn), F32)] if nk > 1 else [],
        compiler_params=_cp("parallel", "parallel", "arbitrary"),
    )(a, b)


def mm_resident(a, w, *, name, tb=False, out_dtype=F32):
    m, k = a.shape
    n = w.shape[0] if tb else w.shape[1]
    tm = _tile(m, (512, 256, 128))
    cn = n if tb else _tile(n, (1024, 512, 256, 128))

    def body(a_ref, w_ref, o_ref):
        aa = a_ref[...].astype(BF16)
        if tb:
            o_ref[...] = _dot(aa, w_ref[...], _NT).astype(o_ref.dtype)
        else:
            for c in range(n // cn):
                cols = slice(cn * c, cn * (c + 1))
                o_ref[:, cols] = _dot(aa, w_ref[:, cols]).astype(o_ref.dtype)

    return pl.pallas_call(
        body, name=name, grid=(m // tm,),
        in_specs=[pl.BlockSpec((tm, k), lambda i: (i, 0)), pl.BlockSpec(w.shape, lambda i: (0, 0), pipeline_mode=pl.Buffered(1))],
        out_specs=pl.BlockSpec((tm, n), lambda i: (i, 0)), out_shape=jax.ShapeDtypeStruct((m, n), out_dtype),
        compiler_params=_cp("parallel"),
    )(a, w)


FFN_GATE, FFN_UP, FFN_DOWN = 0, 1, 2


def _ffn_wspec(wf, l, which):
    _, nsh, _, cs, d = wf.shape
    return pl.BlockSpec((None, nsh, None, cs, d), lambda *_: (l, 0, which, 0, 0), pipeline_mode=pl.Buffered(1))


def _ffn_group(cs):
    for g in (1, 2, 4):
        if (g * cs) % LANE == 0:
            return g
    raise ValueError(cs)


def ffn_up(n, wf, l, base, *, name):
    m, d = n.shape
    nsh, cs = wf.shape[1], wf.shape[3]
    f = nsh * cs
    grp = _ffn_group(cs)
    tm = _tile(m, (512, 256, 128))

    def body(n_ref, wg_ref, wu_ref, g_ref, u_ref, a_ref):
        nn = n_ref[...]
        for c in range(nsh // grp):
            cols = slice(grp * cs * c, grp * cs * (c + 1))
            g = _dot(nn, wg_ref[grp * c:grp * (c + 1)].reshape(grp * cs, d), _NT)
            u = _dot(nn, wu_ref[grp * c:grp * (c + 1)].reshape(grp * cs, d), _NT)
            g_ref[:, cols] = g.astype(BF16)
            u_ref[:, cols] = u.astype(BF16)
            a_ref[:, cols] = f_act_gu(g, u).astype(BF16)

    ospec = pl.BlockSpec((tm, f), lambda i: (i, 0))
    return pl.pallas_call(
        body, name=name, grid=(m // tm,),
        in_specs=[pl.BlockSpec((tm, d), lambda i: (i, 0)), _ffn_wspec(wf, l, base + FFN_GATE), _ffn_wspec(wf, l, base + FFN_UP)],
        out_specs=[ospec] * 3, out_shape=[jax.ShapeDtypeStruct((m, f), BF16)] * 3, compiler_params=_cp("parallel"),
    )(n, wf, wf)


def ffn_down(act, wf, l, base, *, name):
    m, f = act.shape
    nsh, cs, d = wf.shape[1], wf.shape[3], wf.shape[4]
    tm = _tile(m, (512, 256, 128))

    def body(a_ref, wd_ref, y_ref):
        y_ref[...] = _dot(a_ref[...], wd_ref[...].reshape(f, d))

    return pl.pallas_call(
        body, name=name, grid=(m // tm,),
        in_specs=[pl.BlockSpec((tm, f), lambda i: (i, 0)), _ffn_wspec(wf, l, base + FFN_DOWN)],
        out_specs=pl.BlockSpec((tm, d), lambda i: (i, 0)), out_shape=jax.ShapeDtypeStruct((m, d), F32), compiler_params=_cp("parallel"),
    )(act, wf)


def ffn_down_bwd(dy, g, u, wf, l, base, *, name):
    m, d = dy.shape
    nsh, cs = wf.shape[1], wf.shape[3]
    f = nsh * cs
    grp = _ffn_group(cs)
    tm = _tile(m, (512, 256, 128))

    def body(dy_ref, g_ref, u_ref, wd_ref, dg_ref, du_ref):
        dd = dy_ref[...]
        for c in range(nsh // grp):
            cols = slice(grp * cs * c, grp * cs * (c + 1))
            dact = _dot(dd, wd_ref[grp * c:grp * (c + 1)].reshape(grp * cs, d), _NT)
            dg, du = jax.vjp(f_act_gu, g_ref[:, cols].astype(F32), u_ref[:, cols].astype(F32))[1](dact)
            dg_ref[:, cols] = dg.astype(BF16)
            du_ref[:, cols] = du.astype(BF16)

    fspec = pl.BlockSpec((tm, f), lambda i: (i, 0))
    return pl.pallas_call(
        body, name=name, grid=(m // tm,),
        in_specs=[pl.BlockSpec((tm, d), lambda i: (i, 0)), fspec, fspec, _ffn_wspec(wf, l, base + FFN_DOWN)],
        out_specs=[fspec] * 2, out_shape=[jax.ShapeDtypeStruct((m, f), BF16)] * 2, compiler_params=_cp("parallel"),
    )(dy, g, u, wf)


def ffn_up_bwd(dg, du, wf, l, base, *, name):
    m, f = dg.shape
    nsh, cs, d = wf.shape[1], wf.shape[3], wf.shape[4]
    tm = _tile(m, (512, 256, 128))

    def body(dg_ref, du_ref, wg_ref, wu_ref, dn_ref):
        dn_ref[...] = _dot(dg_ref[...], wg_ref[...].reshape(f, d)) + _dot(du_ref[...], wu_ref[...].reshape(f, d))

    fspec = pl.BlockSpec((tm, f), lambda i: (i, 0))
    return pl.pallas_call(
        body, name=name, grid=(m // tm,),
        in_specs=[fspec, fspec, _ffn_wspec(wf, l, base + FFN_GATE), _ffn_wspec(wf, l, base + FFN_UP)],
        out_specs=pl.BlockSpec((tm, d), lambda i: (i, 0)), out_shape=jax.ShapeDtypeStruct((m, d), F32), compiler_params=_cp("parallel"),
    )(dg, du, wf, wf)


def ffn_dw(a, b, nsh, *, name):
    m, f = a.shape
    d = b.shape[1]
    cs = f // nsh
    grp = _ffn_group(cs)
    tm = _tile(m, (1024, 512, 256, 128))

    def body(a_ref, b_ref, o_ref):
        part = _dot(a_ref[...], b_ref[...], _TN).reshape(grp, cs, d)
        i = pl.program_id(1)

        @pl.when(i == 0)
        def _():
            o_ref[...] = part

        @pl.when(i > 0)
        def _():
            o_ref[...] += part

    return pl.pallas_call(
        body, name=name, grid=(nsh // grp, m // tm),
        in_specs=[pl.BlockSpec((tm, grp * cs), lambda j, i: (i, j)), pl.BlockSpec((tm, d), lambda j, i: (i, 0))],
        out_specs=pl.BlockSpec((grp, cs, d), lambda j, i: (j, 0, 0)), out_shape=jax.ShapeDtypeStruct((nsh, cs, d), F32),
        compiler_params=_cp("parallel", "arbitrary"),
    )(a, b)


def rowcall(name, fn, ins, outs, *, nb, nt, nct):
    in_specs, arrays = [], []
    for arr, kind in ins:
        arrays.append(arr)
        if kind == 'tok':
            in_specs.append(pl.BlockSpec((None, TM, arr.shape[-1]), lambda b, t: (b, t, 0)))
        elif kind == 'lat':
            in_specs.append(pl.BlockSpec((None, TM, arr.shape[-1]), lambda b, t: (b, jnp.maximum(t - nct, 0), 0)))
        elif kind == 'pos':
            in_specs.append(pl.BlockSpec((TM, arr.shape[-1]), lambda b, t: (t, 0)))
        elif kind == 'mod':
            in_specs.append(pl.BlockSpec((None, None, 1, arr.shape[-1]), lambda b, t: (b, jnp.where(t >= nct, 1, 0), 0, 0)))
        elif kind == 'full':
            in_specs.append(pl.BlockSpec(arr.shape, lambda b, t, nd=arr.ndim: (0,) * nd))
        else:
            _, w, j = kind
            in_specs.append(pl.BlockSpec((None, TM, w), lambda b, t, j=j: (b, t, j)))
    out_specs, out_shape = [], []
    for o in outs:
        if o[0] == 'tok':
            out_specs.append(pl.BlockSpec((None, TM, o[1]), lambda b, t: (b, t, 0)))
            out_shape.append(jax.ShapeDtypeStruct((nb, nt * TM, o[1]), o[2]))
        elif o[0] == 'mod':
            out_specs.append(pl.BlockSpec((None, None, 1, o[1]), lambda b, t: (b, jnp.where(t >= nct, 1, 0), 0, 0)))
            out_shape.append(jax.ShapeDtypeStruct((nb, 2, 1, o[1]), F32))
        else:
            out_specs.append(pl.BlockSpec(o[1], lambda b, t, nd=len(o[1]): (0,) * nd))
            out_shape.append(jax.ShapeDtypeStruct(o[1], F32))
    n_in = len(ins)

    def body(*refs):
        b, t = pl.program_id(0), pl.program_id(1)
        res = fn(t < nct, *[r[...] for r in refs[:n_in]])
        for ref, o, val in zip(refs[n_in:], outs, res, strict=True):
            if o[0] == 'tok':
                ref[...] = val.astype(ref.dtype)
                continue
            first = ((t == 0) | (t == nct)) if o[0] == 'mod' else ((b == 0) & (t == 0))

            @pl.when(first)
            def _(ref=ref, val=val):
                ref[...] = val

            @pl.when(jnp.logical_not(first))
            def _(ref=ref, val=val):
                ref[...] += val

    return pl.pallas_call(body, name=name, grid=(nb, nt), in_specs=in_specs, out_specs=out_specs, out_shape=out_shape,
                          compiler_params=_cp("arbitrary", "arbitrary"))(*arrays)


def _rms(x, g):
    return x * lax.rsqrt(jnp.mean(x * x, axis=-1, keepdims=True) + EPS) * g


def f_normmod(h, g, shift, scale):
    return _rms(h, g) * (1.0 + scale) + shift


def f_act_gu(g, u):
    return jax.nn.silu(g) * u


def _dot_split(x, m, dims):
    hi = x.astype(BF16)
    lo = (x - hi.astype(F32)).astype(BF16)
    mb = m.astype(BF16)
    return (lax.dot_general(hi, mb, dims, preferred_element_type=F32) + lax.dot_general(lo, mb, dims, preferred_element_type=F32))


def dot_select(x, m):
    return _dot_select(x, m)


@jax.custom_vjp
def _dot_select(x, m):
    return _dot_split(x, m, (((1,), (0,)), ((), ())))


_dot_select.defvjp(lambda x, m: (_dot_split(x, m, (((1,), (0,)), ((), ()))), m),
                   lambda m, ct: (_dot_split(ct, m, (((1,), (1,)), ((), ()))), jnp.zeros_like(m)))


def f_merge(ga, gb, gm, ya, yb, ym):
    return jax.nn.sigmoid(ga) * ya + jax.nn.sigmoid(gb) * yb + jax.nn.sigmoid(gm) * ym


def f_post(p, cb, sb, cm, sm, qn, kn, mqn, mkvn, wuq, wukv, s_b, r_b, t_b, r_m, rep, dup):
    def hnorm(x, g, w):
        ms = dot_select(x * x, s_b[:w, :w])
        gw = dot_select(g, t_b[:, :w])
        return x * lax.rsqrt(ms + EPS) * gw

    def rope(x, cos, sin, rot):
        return x * cos + dot_select(x, rot) * sin

    o = 3 * NA_W
    a_q, a_k, a_v = p[:, 0:NA_W], p[:, NA_W:2 * NA_W], p[:, 2 * NA_W:o]
    b_q = rope(hnorm(p[:, o:o + GQ_W], qn, GQ_W), cb, sb, r_b)
    o += GQ_W
    b_k = rope(hnorm(p[:, o:o + GK_W], kn, GK_W), cb[:, :GK_W], sb[:, :GK_W], r_b[:GK_W, :GK_W])
    b_v = p[:, o + GK_W:o + 2 * GK_W]
    o += 2 * GK_W
    q_lat = jnp.dot(_rms(p[:, o:o + MLA_Q_RANK], mqn).astype(BF16), wuq.astype(BF16), preferred_element_type=F32)
    o += MLA_Q_RANK
    kv_lat = jnp.dot(_rms(p[:, o:o + MLA_KV_RANK], mkvn).astype(BF16), wukv.astype(BF16), preferred_element_type=F32)
    o += MLA_KV_RANK
    nw = MLA_HEADS * MLA_NOPE
    mq_nope, mq_rope = q_lat[:, :nw], rope(q_lat[:, nw:], cm, sm, r_m)
    mk_nope, m_v = kv_lat[:, :nw], kv_lat[:, nw:]
    mk_rope = dot_select(rope(p[:, o:o + LANE], cm, sm, r_m), rep)
    b_k2 = dot_select(b_k, dup)
    b_v2 = dot_select(b_v, dup)
    return (a_q * Q_SCALE, a_k, a_v, b_q * Q_SCALE, b_k2, b_v2, mq_nope * MLA_Q_SCALE, mq_rope * MLA_Q_SCALE, mk_nope, mk_rope, m_v)


POST_QK = (0, 1, 3, 4, 6, 7, 8, 9)


POST_WIDTHS = (NA_W, NA_W, NA_W, GQ_W, 2 * GK_W, 2 * GK_W, MLA_HEADS * MLA_NOPE, MLA_HEADS * MLA_ROPE, MLA_HEADS * MLA_NOPE,
               MLA_HEADS * MLA_ROPE, MLA_HEADS * MLA_V)


_NT = (((1,), (1,)), ((), ()))
_TN = (((0,), (0,)), ((), ()))


def _dot(a, b, dims=None):
    if dims is None:
        return jnp.dot(a, b, preferred_element_type=F32)
    return lax.dot_general(a, b, dims, preferred_element_type=F32)


def _lanes(lo, width):
    lane = lax.broadcasted_iota(jnp.int32, (1, LANE), 1)
    return (lane >= lo) & (lane < lo + width)


def _only(x, mask):
    return jnp.where(mask, x, jnp.zeros_like(x))


def _stack_pair(x, width, lo):
    return jnp.concatenate([_only(x, _lanes(lo, width)), _only(x, _lanes(lo + width, width))], axis=0)


def _pair_softmax(s):
    m = jnp.max(s, axis=-1, keepdims=True)
    p = jnp.exp2(s - m)
    l = jnp.sum(p, axis=-1, keepdims=True)
    return p, l, m + jnp.log2(l)


def gqa_fwd(q, k2, v2, *, lc, name):
    nb, t, qw = q.shape
    npair = qw // LANE
    per_kv = npair // GQA_KV_HEADS
    nctb = lc // TQ

    def body(q_ref, k_ref, v_ref, o_ref, lse_ref):
        i = pl.program_id(2)

        def run(rows):
            kk, vv = k_ref[rows, :], v_ref[rows, :]
            p, l, lse = _pair_softmax(_dot(_stack_pair(q_ref[...], HEAD_DIM, 0), kk, _NT))
            o = _dot(p.astype(BF16), vv) / l
            lse_ref[0], lse_ref[1] = lse[:TQ], lse[TQ:]
            o_ref[...] = jnp.where(_lanes(0, HEAD_DIM), o[:TQ], o[TQ:]).astype(o_ref.dtype)

        @pl.when(i < nctb)
        def _():
            run(pl.ds(0, lc))

        @pl.when(i >= nctb)
        def _():
            run(pl.ds(0, t))

    qmap = lambda b, p, i: (b, i, p)
    kmap = lambda b, p, i: (b, 0, p // per_kv)
    return pl.pallas_call(
        body, name=name, grid=(nb, npair, t // TQ),
        in_specs=[pl.BlockSpec((None, TQ, LANE), qmap), pl.BlockSpec((None, t, LANE), kmap), pl.BlockSpec((None, t, LANE), kmap)],
        out_specs=[pl.BlockSpec((None, TQ, LANE), qmap), pl.BlockSpec((None, 2, TQ, 1), lambda b, p, i: (b, p, i, 0))],
        out_shape=[jax.ShapeDtypeStruct((nb, t, qw), BF16), jax.ShapeDtypeStruct((nb, 2 * npair, t, 1), F32)],
        compiler_params=_cp("parallel", "parallel", "arbitrary"),
    )(q, k2, v2)


def gqa_bwd(q, k2, v2, lse, do, *, lc, name):
    nb, t, qw = q.shape
    npair = qw // LANE
    per_kv = npair // GQA_KV_HEADS
    nctb = lc // TQ

    def body(q_ref, k_ref, v_ref, lse_ref, do_ref, dq_ref, dk_ref, dv_ref):
        g, i = pl.program_id(2), pl.program_id(3)

        @pl.when((g == 0) & (i == 0))
        def _():
            dk_ref[...] = jnp.zeros_like(dk_ref)
            dv_ref[...] = jnp.zeros_like(dv_ref)

        def run(rows):
            kk, vv = k_ref[rows, :], v_ref[rows, :]
            qq, dd = _stack_pair(q_ref[...], HEAD_DIM, 0), _stack_pair(do_ref[...], HEAD_DIM, 0)
            p = jnp.exp2(_dot(qq, kk, _NT) - jnp.concatenate([lse_ref[0], lse_ref[1]], axis=0))
            dp = _dot(dd, vv, _NT)
            delta = jnp.sum(p * dp, axis=-1, keepdims=True)
            ds = (p * (dp - delta)).astype(BF16)
            dq = _dot(ds, kk)
            dq_ref[...] = jnp.where(_lanes(0, HEAD_DIM), dq[:TQ], dq[TQ:])
            dk_ref[rows, :] += _dot(ds, qq, _TN)
            dv_ref[rows, :] += _dot(p.astype(BF16), dd, _TN)

        @pl.when(i < nctb)
        def _():
            run(pl.ds(0, lc))

        @pl.when(i >= nctb)
        def _():
            run(pl.ds(0, t))

    qmap = lambda b, j, g, i: (b, i, j * per_kv + g)
    kmap = lambda b, j, g, i: (b, 0, j)
    return pl.pallas_call(
        body, name=name, grid=(nb, GQA_KV_HEADS, per_kv, t // TQ),
        in_specs=[pl.BlockSpec((None, TQ, LANE), qmap), pl.BlockSpec((None, t, LANE), kmap), pl.BlockSpec((None, t, LANE), kmap),
                  pl.BlockSpec((None, 2, TQ, 1), lambda b, j, g, i: (b, j * per_kv + g, i, 0)), pl.BlockSpec((None, TQ, LANE), qmap)],
        out_specs=[pl.BlockSpec((None, TQ, LANE), qmap), pl.BlockSpec((None, t, LANE), kmap), pl.BlockSpec((None, t, LANE), kmap)],
        out_shape=[jax.ShapeDtypeStruct((nb, t, qw), F32), jax.ShapeDtypeStruct(k2.shape, F32), jax.ShapeDtypeStruct(v2.shape, F32)],
        compiler_params=_cp("arbitrary", "arbitrary", "arbitrary", "arbitrary"),
    )(q, k2, v2, lse, do)


def mla_fwd(qn, qr, kn, kr, v, *, lc, name):
    nb, t, w = qn.shape
    npair = w // LANE
    nctb = lc // TQ

    def body(qn_ref, qr_ref, kn_ref, kr_ref, v_ref, o_ref, lse_ref):
        pr, i = pl.program_id(1), pl.program_id(2)

        def run(rows):
            kk, kkr, vv = kn_ref[rows, :], kr_ref[rows, :], v_ref[rows, :]
            s = (_dot(_stack_pair(qn_ref[...], MLA_NOPE, 0), kk, _NT)
                 + _dot(_stack_pair(qr_ref[...], MLA_ROPE, 2 * MLA_ROPE * pr), kkr, _NT))
            p, l, lse = _pair_softmax(s)
            o = _dot(p.astype(BF16), vv) / l
            lse_ref[0], lse_ref[1] = lse[:TQ], lse[TQ:]
            o_ref[...] = jnp.where(_lanes(0, MLA_V), o[:TQ], o[TQ:]).astype(o_ref.dtype)

        @pl.when(i < nctb)
        def _():
            run(pl.ds(0, lc))

        @pl.when(i >= nctb)
        def _():
            run(pl.ds(0, t))

    qmap = lambda b, p, i: (b, i, p)
    rmap = lambda b, p, i: (b, i, 0)
    kmap = lambda b, p, i: (b, 0, p)
    return pl.pallas_call(
        body, name=name, grid=(nb, npair, t // TQ),
        in_specs=[pl.BlockSpec((None, TQ, LANE), qmap), pl.BlockSpec((None, TQ, LANE), rmap), pl.BlockSpec((None, t, LANE), kmap),
                  pl.BlockSpec((None, t, LANE), lambda b, p, i: (b, 0, 0)), pl.BlockSpec((None, t, LANE), kmap)],
        out_specs=[pl.BlockSpec((None, TQ, LANE), qmap), pl.BlockSpec((None, 2, TQ, 1), lambda b, p, i: (b, p, i, 0))],
        out_shape=[jax.ShapeDtypeStruct((nb, t, w), BF16), jax.ShapeDtypeStruct((nb, 2 * npair, t, 1), F32)],
        compiler_params=_cp("parallel", "parallel", "arbitrary"),
    )(qn, qr, kn, kr, v)


def mla_bwd(qn, qr, kn, kr, v, lse, do, *, lc, name):
    nb, t, w = qn.shape
    npair = w // LANE
    nctb = lc // TQ

    def body(qn_ref, qr_ref, kn_ref, kr_ref, v_ref, lse_ref, do_ref, dqn_ref, dqr_ref, dkn_ref, dkr_ref, dv_ref):
        pr, i = pl.program_id(1), pl.program_id(2)

        @pl.when(i == 0)
        def _():
            dkn_ref[...] = jnp.zeros_like(dkn_ref)
            dv_ref[...] = jnp.zeros_like(dv_ref)

        @pl.when((i == 0) & (pr == 0))
        def _():
            dkr_ref[...] = jnp.zeros_like(dkr_ref)

        def run(rows):
            kk, kkr, vv = kn_ref[rows, :], kr_ref[rows, :], v_ref[rows, :]
            r_lo = 2 * MLA_ROPE * pr
            qq, qqr = _stack_pair(qn_ref[...], MLA_NOPE, 0), _stack_pair(qr_ref[...], MLA_ROPE, r_lo)
            dd = _stack_pair(do_ref[...], MLA_V, 0)
            p = jnp.exp2(_dot(qq, kk, _NT) + _dot(qqr, kkr, _NT) - jnp.concatenate([lse_ref[0], lse_ref[1]], axis=0))
            dp = _dot(dd, vv, _NT)
            delta = jnp.sum(p * dp, axis=-1, keepdims=True)
            ds = (p * (dp - delta)).astype(BF16)
            dqn, dqr = _dot(ds, kk), _dot(ds, kkr)
            dqn_ref[...] = jnp.where(_lanes(0, MLA_NOPE), dqn[:TQ], dqn[TQ:])
            dqr_ref[...] = _only(dqr[:TQ], _lanes(r_lo, MLA_ROPE)) + _only(dqr[TQ:], _lanes(r_lo + MLA_ROPE, MLA_ROPE))
            dkn_ref[rows, :] += _dot(ds, qq, _TN)
            dkr_ref[rows, :] += _dot(ds, qqr, _TN)
            dv_ref[rows, :] += _dot(p.astype(BF16), dd, _TN)

        @pl.when(i < nctb)
        def _():
            run(pl.ds(0, lc))

        @pl.when(i >= nctb)
        def _():
            run(pl.ds(0, t))

    qmap = lambda b, p, i: (b, i, p)
    rmap = lambda b, p, i: (b, i, 0)
    kmap = lambda b, p, i: (b, 0, p)
    zmap = lambda b, p, i: (b, 0, 0)
    return pl.pallas_call(
        body, name=name, grid=(nb, npair, t // TQ),
        in_specs=[pl.BlockSpec((None, TQ, LANE), qmap), pl.BlockSpec((None, TQ, LANE), rmap), pl.BlockSpec((None, t, LANE), kmap),
                  pl.BlockSpec((None, t, LANE), zmap), pl.BlockSpec((None, t, LANE), kmap),
                  pl.BlockSpec((None, 2, TQ, 1), lambda b, p, i: (b, p, i, 0)), pl.BlockSpec((None, TQ, LANE), qmap)],
        out_specs=[pl.BlockSpec((None, TQ, LANE), qmap), pl.BlockSpec((None, TQ, LANE), qmap), pl.BlockSpec((None, t, LANE), kmap),
                   pl.BlockSpec((None, t, LANE), zmap), pl.BlockSpec((None, t, LANE), kmap)],
        out_shape=[jax.ShapeDtypeStruct((nb, t, w), F32), jax.ShapeDtypeStruct((nb, t, npair * LANE), F32),
                   jax.ShapeDtypeStruct((nb, t, w), F32), jax.ShapeDtypeStruct((nb, t, LANE), F32), jax.ShapeDtypeStruct((nb, t, w), F32)],
        compiler_params=_cp("arbitrary", "arbitrary", "arbitrary"),
    )(qn, qr, kn, kr, v, lse, do)


def _na_window(st, nc, rows):
    r = jnp.maximum(st - nc, 0)
    r0 = jnp.clip(r - NA_ROWS // 2, 0, rows - NA_ROWS)
    return r, r0, r - r0


def na_fwd(q, k, v, bias, *, lc, name):
    nb, t, w = q.shape
    npair = w // LANE
    nc, rows = lc // GRID_W, (t - lc) // GRID_W
    nwin = NA_ROWS * GRID_W

    def body(q_ref, k_ref, v_ref, bias_ref, o_ref, lse_ref):
        st = pl.program_id(2)
        ctx = pl.ds(0, lc)
        kc, vc = k_ref[ctx, :], v_ref[ctx, :]
        outs = [None, None]

        @pl.when(st < nc)
        def _():
            for e in range(2):
                p, l, lse = _pair_softmax(_dot(_only(q_ref[...], _lanes(HEAD_DIM * e, HEAD_DIM)), kc, _NT))
                outs[e] = _dot(p.astype(BF16), vc) / l
                lse_ref[e] = lse
            o_ref[...] = jnp.where(_lanes(0, HEAD_DIM), outs[0], outs[1]).astype(o_ref.dtype)

        @pl.when(st >= nc)
        def _():
            _, r0, _ = _na_window(st, nc, rows)
            win = pl.ds(pl.multiple_of(lc + r0 * GRID_W, GRID_W), nwin)
            kw, vw = k_ref[win, :], v_ref[win, :]
            qq = _stack_pair(q_ref[...], HEAD_DIM, 0)
            s_loc = _dot(qq, kw, _NT) + jnp.concatenate([bias_ref[0], bias_ref[1]], axis=0) * LOG2E
            s_ctx = _dot(qq, kc, _NT)
            m = jnp.maximum(jnp.max(s_loc, axis=-1, keepdims=True), jnp.max(s_ctx, axis=-1, keepdims=True))
            p_loc, p_ctx = jnp.exp2(s_loc - m), jnp.exp2(s_ctx - m)
            l = jnp.sum(p_loc, axis=-1, keepdims=True) + jnp.sum(p_ctx, axis=-1, keepdims=True)
            o = (_dot(p_loc.astype(BF16), vw) + _dot(p_ctx.astype(BF16), vc)) / l
            lse = m + jnp.log2(l)
            lse_ref[0], lse_ref[1] = lse[:GRID_W], lse[GRID_W:]
            o_ref[...] = jnp.where(_lanes(0, HEAD_DIM), o[:GRID_W], o[GRID_W:]).astype(o_ref.dtype)

    qmap = lambda p, b, st: (b, st, p)
    kmap = lambda p, b, st: (b, 0, p)
    return pl.pallas_call(
        body, name=name, grid=(npair, nb, nc + rows),
        in_specs=[pl.BlockSpec((None, GRID_W, LANE), qmap), pl.BlockSpec((None, t, LANE), kmap), pl.BlockSpec((None, t, LANE), kmap),
                  pl.BlockSpec((2, None, GRID_W, nwin), lambda p, b, st: (p, _na_window(st, nc, rows)[2], 0, 0))],
        out_specs=[pl.BlockSpec((None, GRID_W, LANE), qmap), pl.BlockSpec((None, 2, GRID_W, 1), lambda p, b, st: (b, p, st, 0))],
        out_shape=[jax.ShapeDtypeStruct((nb, t, w), BF16), jax.ShapeDtypeStruct((nb, 2 * npair, t, 1), F32)],
        compiler_params=_cp("parallel", "parallel", "arbitrary"),
    )(q, k, v, bias)


def na_bwd(q, k, v, bias, lse, do, *, lc, name):
    nb, t, w = q.shape
    npair = w // LANE
    nc, rows = lc // GRID_W, (t - lc) // GRID_W
    nwin = NA_ROWS * GRID_W

    def body(q_ref, k_ref, v_ref, bias_ref, lse_ref, do_ref, dq_ref, dk_ref, dv_ref, db_ref):
        b, st = pl.program_id(1), pl.program_id(2)

        @pl.when(st == 0)
        def _():
            dk_ref[...] = jnp.zeros_like(dk_ref)
            dv_ref[...] = jnp.zeros_like(dv_ref)

        @pl.when((st == 0) & (b == 0))
        def _():
            db_ref[...] = jnp.zeros_like(db_ref)

        ctx = pl.ds(0, lc)
        kc, vc = k_ref[ctx, :], v_ref[ctx, :]
        dqs = [None, None]

        @pl.when(st < nc)
        def _():
            for e in range(2):
                mine = _lanes(HEAD_DIM * e, HEAD_DIM)
                qq, dd = _only(q_ref[...], mine), _only(do_ref[...], mine)
                p = jnp.exp2(_dot(qq, kc, _NT) - lse_ref[e])
                dp = _dot(dd, vc, _NT)
                delta = jnp.sum(p * dp, axis=-1, keepdims=True)
                ds = (p * (dp - delta)).astype(BF16)
                dqs[e] = _dot(ds, kc)
                dk_ref[ctx, :] += _dot(ds, qq, _TN)
                dv_ref[ctx, :] += _dot(p.astype(BF16), dd, _TN)
            dq_ref[...] = jnp.where(_lanes(0, HEAD_DIM), dqs[0], dqs[1])

        @pl.when(st >= nc)
        def _():
            _, r0, case = _na_window(st, nc, rows)
            win = pl.ds(pl.multiple_of(lc + r0 * GRID_W, GRID_W), nwin)
            kw, vw = k_ref[win, :], v_ref[win, :]
            qq, dd = _stack_pair(q_ref[...], HEAD_DIM, 0), _stack_pair(do_ref[...], HEAD_DIM, 0)
            lse = jnp.concatenate([lse_ref[0], lse_ref[1]], axis=0)
            p_loc = jnp.exp2(_dot(qq, kw, _NT) + jnp.concatenate([bias_ref[0], bias_ref[1]], axis=0) * LOG2E - lse)
            p_ctx = jnp.exp2(_dot(qq, kc, _NT) - lse)
            dp_loc, dp_ctx = _dot(dd, vw, _NT), _dot(dd, vc, _NT)
            delta = jnp.sum(p_loc * dp_loc, axis=-1, keepdims=True) + jnp.sum(p_ctx * dp_ctx, axis=-1, keepdims=True)
            ds_loc = p_loc * (dp_loc - delta)
            db_ref[0, case] += ds_loc[:GRID_W]
            db_ref[1, case] += ds_loc[GRID_W:]
            ds_loc = ds_loc.astype(BF16)
            ds_ctx = (p_ctx * (dp_ctx - delta)).astype(BF16)
            dq = _dot(ds_loc, kw) + _dot(ds_ctx, kc)
            dq_ref[...] = jnp.where(_lanes(0, HEAD_DIM), dq[:GRID_W], dq[GRID_W:])
            dk_ref[win, :] += _dot(ds_loc, qq, _TN)
            dk_ref[ctx, :] += _dot(ds_ctx, qq, _TN)
            dv_ref[win, :] += _dot(p_loc.astype(BF16), dd, _TN)
            dv_ref[ctx, :] += _dot(p_ctx.astype(BF16), dd, _TN)

    qmap = lambda p, b, st: (b, st, p)
    kmap = lambda p, b, st: (b, 0, p)
    return pl.pallas_call(
        body, name=name, grid=(npair, nb, nc + rows),
        in_specs=[pl.BlockSpec((None, GRID_W, LANE), qmap), pl.BlockSpec((None, t, LANE), kmap), pl.BlockSpec((None, t, LANE), kmap),
                  pl.BlockSpec((2, None, GRID_W, nwin), lambda p, b, st: (p, _na_window(st, nc, rows)[2], 0, 0)),
                  pl.BlockSpec((None, 2, GRID_W, 1), lambda p, b, st: (b, p, st, 0)), pl.BlockSpec((None, GRID_W, LANE), qmap)],
        out_specs=[pl.BlockSpec((None, GRID_W, LANE), qmap), pl.BlockSpec((None, t, LANE), kmap), pl.BlockSpec((None, t, LANE), kmap),
                   pl.BlockSpec((2, NA_ROWS, GRID_W, nwin), lambda p, b, st: (p, 0, 0, 0))],
        out_shape=[jax.ShapeDtypeStruct((nb, t, w), F32), jax.ShapeDtypeStruct((nb, t, w), F32), jax.ShapeDtypeStruct((nb, t, w), F32),
                   jax.ShapeDtypeStruct((2 * npair, NA_ROWS, GRID_W, nwin), F32)],
        compiler_params=_cp("arbitrary", "arbitrary", "arbitrary"),
    )(q, k, v, bias, lse, do)


def _na_tables():
    cols = np.arange(GRID_W)
    c0 = np.clip(cols - NA_COLS // 2, 0, GRID_W - NA_COLS)
    col_in = (cols[None, :] >= c0[:, None]) & (cols[None, :] < c0[:, None] + NA_COLS)
    dc = np.clip(cols[None, :] - cols[:, None] + NA_COLS - 1, 0, 2 * NA_COLS - 2)
    dr = np.arange(NA_ROWS)[None, :] + (NA_ROWS - 1) - np.arange(NA_ROWS)[:, None]
    return col_in, dc, dr


def _na_onehots():
    col_in, dc, dr = _na_tables()
    e1 = np.zeros((GRID_W, GRID_W, LANE), np.float32)
    qi, ki = np.nonzero(col_in)
    e1[qi, ki, dc[qi, ki]] = 1.0
    e2 = np.zeros((2 * NA_ROWS, NA_ROWS, NA_ROWS), np.float32)
    ci, ji = np.meshgrid(np.arange(NA_ROWS), np.arange(NA_ROWS), indexing='ij')
    e2[dr[ci, ji], ci, ji] = 1.0
    return jnp.asarray(e1.reshape(GRID_W * GRID_W, LANE)), jnp.asarray(e2.reshape(2 * NA_ROWS, NA_ROWS * NA_ROWS)), col_in


def na_expand_bias(rel_bias, name):
    e1, e2, col_in = _na_onehots()
    nh = rel_bias.shape[0]
    nrow = NA_ROWS * NA_ROWS
    rel = jnp.pad(rel_bias, ((0, 0), (0, 1), (0, LANE - rel_bias.shape[2])))
    rel = rel.transpose(1, 0, 2).reshape(2 * NA_ROWS, nh * LANE)
    y = mm(e2, rel, ta=True, name=name + "_rows", precise=True)
    y = y.reshape(nrow, nh, LANE).transpose(1, 0, 2).reshape(nh * nrow, LANE)
    g = mm(y, e1, tb=True, name=name + "_cols", precise=True)
    g = g.reshape(nh, NA_ROWS, NA_ROWS, GRID_W, GRID_W).transpose(0, 1, 3, 2, 4)
    g = jnp.where(col_in[None, None, :, None, :], g, NEG_BIG)
    return g.reshape(nh, NA_ROWS, GRID_W, NA_ROWS * GRID_W)


def na_reduce_bias(dexp, name):
    e1, e2, _ = _na_onehots()
    nh = dexp.shape[0]
    x = dexp.reshape(nh, NA_ROWS, GRID_W, NA_ROWS, GRID_W).transpose(0, 1, 3, 2, 4).reshape(nh * NA_ROWS * NA_ROWS, GRID_W * GRID_W)
    y = mm(x, e1, name=name + "_cols", precise=True)
    y = y.reshape(nh, NA_ROWS * NA_ROWS, LANE).transpose(1, 0, 2).reshape(NA_ROWS * NA_ROWS, nh * LANE)
    z = mm(e2, y, name=name + "_rows", precise=True)
    return z.reshape(2 * NA_ROWS, nh, LANE).transpose(1, 0, 2)[:, :2 * NA_ROWS - 1, :2 * NA_COLS - 1]


def _rot_matrix(width, d_rot):
    f = d_rot // 4
    r = np.zeros((width, width), np.float32)
    for base in range(0, width, d_rot // 2):
        for j in range(f):
            r[base + f + j, base + j] = -1.0
            r[base + j, base + f + j] = 1.0
    return r


def _rope_tables(s_len, lc, d_rot, reps):
    half = d_rot // 2
    freqs = ROPE_THETA ** (-jnp.arange(0, half, 2, dtype=F32) / half)
    tpos = jnp.arange(s_len)
    row = (tpos // GRID_W).astype(F32)[:, None] * freqs
    col = (tpos % GRID_W).astype(F32)[:, None] * freqs
    ang = jnp.concatenate([row, row, col, col], axis=-1)
    cos = jnp.concatenate([jnp.ones((lc, d_rot), F32), jnp.cos(ang)], axis=0)
    sin = jnp.concatenate([jnp.zeros((lc, d_rot), F32), jnp.sin(ang)], axis=0)
    return jnp.tile(cos, (1, reps)), jnp.tile(sin, (1, reps))


def _post_consts():
    s_b = np.kron(np.eye(GQA_HEADS, dtype=np.float32), np.full((HEAD_DIM, HEAD_DIM), 1.0 / HEAD_DIM, np.float32))
    t_b = np.tile(np.eye(HEAD_DIM, dtype=np.float32), (1, GQA_HEADS))
    r_b = _rot_matrix(GQ_W, HEAD_DIM)
    r_m = _rot_matrix(LANE, MLA_ROPE)
    rep = np.zeros((LANE, LANE), np.float32)
    for h in range(MLA_HEADS):
        rep[np.arange(MLA_ROPE), h * MLA_ROPE + np.arange(MLA_ROPE)] = 1.0
    dup = np.zeros((GK_W, 2 * GK_W), np.float32)
    for j in range(GQA_KV_HEADS):
        for e in range(2):
            dup[HEAD_DIM * j + np.arange(HEAD_DIM), 2 * HEAD_DIM * j + HEAD_DIM * e + np.arange(HEAD_DIM)] = 1.0
    return tuple(jnp.asarray(a) for a in (s_b, r_b, t_b, r_m, rep, dup))


def _heads_to_parts(w, first):
    r = w.shape[0]
    w3 = w.reshape(r, MLA_HEADS, -1)
    return jnp.concatenate([w3[:, :, :first].reshape(r, -1), w3[:, :, first:].reshape(r, -1)], axis=1)


def _parts_to_heads(w, first):
    r = w.shape[0]
    nf = MLA_HEADS * first
    return jnp.concatenate([w[:, :nf].reshape(r, MLA_HEADS, first), w[:, nf:].reshape(r, MLA_HEADS, -1)], axis=2).reshape(r, -1)


def _place():
    return lax.axis_index("x"), lax.axis_index("y"), lax.axis_index("c")


def all_gather(v, *, name, with_c):
    flips = [(dx, dy, dc) for dx in (0, 1) for dy in (0, 1) for dc in ((0, 1) if with_c else (0,))][1:]
    n = len(flips) + 1

    def body(v_ref, out_ref, send_sems, recv_sems, local_sem):
        mx, my, mc = _place()

        def slot(px, py, pc):
            return 4 * px + 2 * py + pc if with_c else 2 * px + py

        mine = pltpu.make_async_copy(v_ref, out_ref.at[slot(mx, my, mc)], local_sem)
        mine.start()
        sends = []
        for j, (dx, dy, dc) in enumerate(flips):
            peer = (mx ^ dx, my ^ dy, mc ^ dc)
            cp = pltpu.make_async_remote_copy(src_ref=v_ref, dst_ref=out_ref.at[slot(mx, my, mc)], send_sem=send_sems.at[j],
                                              recv_sem=recv_sems.at[j], device_id=peer, device_id_type=MESH)
            cp.start()
            sends.append(cp)
        for j, (dx, dy, dc) in enumerate(flips):
            peer = (mx ^ dx, my ^ dy, mc ^ dc)
            pltpu.make_async_remote_copy(src_ref=v_ref, dst_ref=out_ref.at[slot(*peer)], send_sem=send_sems.at[j],
                                         recv_sem=recv_sems.at[j], device_id=peer, device_id_type=MESH).wait_recv()
        for cp in sends:
            cp.wait_send()
        mine.wait()

    return pl.pallas_call(
        body, name=name, in_specs=[ANY], out_specs=ANY, out_shape=jax.ShapeDtypeStruct((n,) + v.shape, v.dtype),
        scratch_shapes=[pltpu.SemaphoreType.DMA((n - 1,)), pltpu.SemaphoreType.DMA((n - 1,)), pltpu.SemaphoreType.DMA(())],
    )(v)


def gather_shards(v, *, name):
    _, h, w = v.shape
    flips = [(1, 0), (0, 1), (1, 1)]

    def body(v_ref, out_ref, send_sems, recv_sems):
        mx, my, mc = _place()
        me = 2 * mx + my
        sib = (mx, my, 1 - mc)

        def copy(k, src, dst, to):
            return pltpu.make_async_remote_copy(src_ref=src, dst_ref=dst, send_sem=send_sems.at[k], recv_sem=recv_sems.at[k],
                                                device_id=to, device_id_type=MESH)

        first = [copy(j, v_ref.at[mc], out_ref.at[me, mc], (mx ^ dx, my ^ dy, mc)) for j, (dx, dy) in enumerate(flips)]
        for cp in first:
            cp.start()
        passed = []
        for j, (dx, dy) in enumerate(flips):
            theirs = out_ref.at[2 * (mx ^ dx) + (my ^ dy), mc]
            copy(j, v_ref.at[mc], theirs, (mx ^ dx, my ^ dy, mc)).wait_recv()
            fw = copy(3 + j, theirs, theirs, sib)
            fw.start()
            passed.append(fw)
        for j, (dx, dy) in enumerate(flips):
            other = out_ref.at[2 * (mx ^ dx) + (my ^ dy), 1 - mc]
            copy(3 + j, other, other, sib).wait_recv()
        for cp in first + passed:
            cp.wait_send()

    out = pl.pallas_call(
        body, name=name, in_specs=[ANY], out_specs=ANY, out_shape=jax.ShapeDtypeStruct((4, 2, h, w), v.dtype),
        scratch_shapes=[pltpu.SemaphoreType.DMA((6,)), pltpu.SemaphoreType.DMA((6,))],
    )(v)
    mx, my, _ = _place()
    return lax.dynamic_update_slice(out, v[None], (2 * mx + my, 0, 0, 0))


def pair_exchange_halves(g, *, name):
    n, _, h, w = g.shape

    def body(g_ref, out_ref, send_sems, recv_sems):
        mx, my, mc = _place()
        sib = (mx, my, 1 - mc)
        cps = [pltpu.make_async_remote_copy(src_ref=g_ref.at[s, 1 - mc], dst_ref=out_ref.at[s], send_sem=send_sems.at[s],
                                            recv_sem=recv_sems.at[s], device_id=sib, device_id_type=MESH) for s in range(n)]
        for cp in cps:
            cp.start()
        for cp in cps:
            cp.wait_recv()
        for cp in cps:
            cp.wait_send()

    return pl.pallas_call(
        body, name=name, in_specs=[ANY], out_specs=ANY, out_shape=jax.ShapeDtypeStruct((n, h, w), g.dtype),
        scratch_shapes=[pltpu.SemaphoreType.DMA((n,)), pltpu.SemaphoreType.DMA((n,))],
    )(g)


def all_to_all_xy(v, *, name):
    def body(v_ref, out_ref, send_sems, recv_sems):
        mx, my, mc = _place()
        me = 2 * mx + my
        flips = [(1, 0), (0, 1), (1, 1)]
        sends = []
        for j, (dx, dy) in enumerate(flips):
            px, py = mx ^ dx, my ^ dy
            cp = pltpu.make_async_remote_copy(src_ref=v_ref.at[2 * px + py], dst_ref=out_ref.at[me], send_sem=send_sems.at[j],
                                              recv_sem=recv_sems.at[j], device_id=(px, py, mc), device_id_type=MESH)
            cp.start()
            sends.append(cp)
        for j, (dx, dy) in enumerate(flips):
            px, py = mx ^ dx, my ^ dy
            pltpu.make_async_remote_copy(src_ref=v_ref.at[me], dst_ref=out_ref.at[2 * px + py], send_sem=send_sems.at[j],
                                         recv_sem=recv_sems.at[j], device_id=(px, py, mc), device_id_type=MESH).wait_recv()
        for cp in sends:
            cp.wait_send()

    out = pl.pallas_call(
        body, name=name, in_specs=[ANY], out_specs=ANY, out_shape=jax.ShapeDtypeStruct(v.shape, v.dtype),
        scratch_shapes=[pltpu.SemaphoreType.DMA((3,)), pltpu.SemaphoreType.DMA((3,))],
    )(v)
    mx, my, _ = _place()
    me = 2 * mx + my
    return lax.dynamic_update_slice(out, lax.dynamic_slice_in_dim(v, me, 1, axis=0), (me, 0, 0))


def pair_all_gather(v, *, name):
    def body(v_ref, out_ref, send_sem, recv_sem):
        mx, my, mc = _place()
        cp = pltpu.make_async_remote_copy(src_ref=v_ref, dst_ref=out_ref.at[mc], send_sem=send_sem, recv_sem=recv_sem,
                                          device_id=(mx, my, 1 - mc), device_id_type=MESH)
        cp.start()
        pltpu.make_async_remote_copy(src_ref=v_ref, dst_ref=out_ref.at[1 - mc], send_sem=send_sem, recv_sem=recv_sem,
                                     device_id=(mx, my, 1 - mc), device_id_type=MESH).wait_recv()
        cp.wait_send()

    out = pl.pallas_call(
        body, name=name, in_specs=[ANY], out_specs=ANY, out_shape=jax.ShapeDtypeStruct((2,) + v.shape, v.dtype),
        scratch_shapes=[pltpu.SemaphoreType.DMA(()), pltpu.SemaphoreType.DMA(())],
    )(v)
    return lax.dynamic_update_slice(out, v[None], (_place()[2], 0, 0))


def gather_ffn(wl, *, name):
    nl, nblk, cs, d = wl.shape
    assert nl == 2
    flips = [(1, 0), (0, 1), (1, 1)]

    def body(v_ref, out_ref, send_sems, recv_sems):
        mx, my, mc = _place()
        me = 2 * mx + my
        sib = (mx, my, 1 - mc)

        def copy(k, src, dst, to):
            return pltpu.make_async_remote_copy(src_ref=src, dst_ref=dst, send_sem=send_sems.at[k], recv_sem=recv_sems.at[k],
                                                device_id=to, device_id_type=MESH)

        first = [copy(j, v_ref.at[mc], out_ref.at[mc, me], (mx ^ dx, my ^ dy, mc)) for j, (dx, dy) in enumerate(flips)]
        for cp in first:
            cp.start()
        passed = []
        for j, (dx, dy) in enumerate(flips):
            theirs = out_ref.at[mc, 2 * (mx ^ dx) + (my ^ dy)]
            copy(j, v_ref.at[mc], theirs, (mx ^ dx, my ^ dy, mc)).wait_recv()
            fw = copy(3 + j, theirs, theirs, sib)
            fw.start()
            passed.append(fw)
        for j, (dx, dy) in enumerate(flips):
            other = out_ref.at[1 - mc, 2 * (mx ^ dx) + (my ^ dy)]
            copy(3 + j, other, other, sib).wait_recv()
        for cp in first + passed:
            cp.wait_send()

    out = pl.pallas_call(
        body, name=name, in_specs=[ANY], out_specs=ANY, out_shape=jax.ShapeDtypeStruct((nl, 4, nblk, cs, d), wl.dtype),
        scratch_shapes=[pltpu.SemaphoreType.DMA((6,)), pltpu.SemaphoreType.DMA((6,))],
    )(wl)
    mx, my, _ = _place()
    return lax.dynamic_update_slice(out, wl[:, None], (0, 2 * mx + my, 0, 0, 0))


def reduce_ffn(g0, g1, *, name):
    nt = len(g0)
    nsh, cs, d = g0[0].shape
    flips = [(1, 0), (0, 1), (1, 1)]
    mx, my, mc = _place()
    me = 2 * mx + my
    c_idx = jnp.reshape(mc, (1,)).astype(jnp.int32)

    def pair_body(*refs):
        ins0, ins1, outs = refs[:nt], refs[nt:2 * nt], refs[2 * nt:3 * nt]
        send_sems, recv_sems = refs[3 * nt:]
        kx, ky, kc = _place()
        sib = (kx, ky, 1 - kc)
        for c in range(2):
            @pl.when(kc == c)
            def _(c=c):
                mine_out = (ins1, ins0)[c]
                cps = [pltpu.make_async_remote_copy(src_ref=mine_out[t], dst_ref=outs[t], send_sem=send_sems.at[t],
                                                    recv_sem=recv_sems.at[t], device_id=sib, device_id_type=MESH) for t in range(nt)]
                for cp in cps:
                    cp.start()
                for cp in cps:
                    cp.wait_recv()
                for cp in cps:
                    cp.wait_send()

    from_pair = pl.pallas_call(
        pair_body, name=name + "_pair", in_specs=[ANY] * (2 * nt), out_specs=[ANY] * nt,
        out_shape=[jax.ShapeDtypeStruct((nsh, cs, d), F32)] * nt,
        scratch_shapes=[pltpu.SemaphoreType.DMA((nt,)), pltpu.SemaphoreType.DMA((nt,))],
    )(*g0, *g1)

    tr = _row_tile(cs, 64)

    def add_body(c_ref, *refs):
        for t in range(nt):
            mine = jnp.where(c_ref[0] == 0, refs[t][...], refs[nt + t][...])
            refs[3 * nt + t][...] = (mine + refs[2 * nt + t][...]).astype(BF16)

    spec = pl.BlockSpec((None, tr, d), lambda s, i, c_ref: (s, i, 0))
    chip_sum = pl.pallas_call(
        add_body, name=name + "_pair_add",
        grid_spec=pltpu.PrefetchScalarGridSpec(num_scalar_prefetch=1, grid=(nsh, cs // tr), in_specs=[spec] * (3 * nt),
                                               out_specs=[spec] * nt),
        out_shape=[jax.ShapeDtypeStruct((nsh, cs, d), BF16)] * nt, compiler_params=_cp("parallel", "parallel"),
    )(c_idx, *g0, *g1, *from_pair)

    def xy_body(*refs):
        ins, outs = refs[:nt], refs[nt:2 * nt]
        send_sems, recv_sems = refs[2 * nt:]
        kx, ky, kc = _place()
        k_me = 2 * kx + ky
        sends = []
        for j, (dx, dy) in enumerate(flips):
            px, py = kx ^ dx, ky ^ dy
            for t in range(nt):
                cp = pltpu.make_async_remote_copy(src_ref=ins[t].at[2 * px + py], dst_ref=outs[t].at[k_me],
                                                  send_sem=send_sems.at[j * nt + t], recv_sem=recv_sems.at[j * nt + t],
                                                  device_id=(px, py, kc), device_id_type=MESH)
                cp.start()
                sends.append(cp)
        for j, (dx, dy) in enumerate(flips):
            px, py = kx ^ dx, ky ^ dy
            for t in range(nt):
                pltpu.make_async_remote_copy(src_ref=ins[t].at[k_me], dst_ref=outs[t].at[2 * px + py],
                                             send_sem=send_sems.at[j * nt + t], recv_sem=recv_sems.at[j * nt + t],
                                             device_id=(px, py, kc), device_id_type=MESH).wait_recv()
        for cp in sends:
            cp.wait_send()

    from_xy = pl.pallas_call(
        xy_body, name=name + "_xy", in_specs=[ANY] * nt, out_specs=[ANY] * nt,
        out_shape=[jax.ShapeDtypeStruct((nsh, cs, d), BF16)] * nt,
        scratch_shapes=[pltpu.SemaphoreType.DMA((3 * nt,)), pltpu.SemaphoreType.DMA((3 * nt,))],
    )(*chip_sum)
    from_xy = [lax.dynamic_update_slice(o, lax.dynamic_slice_in_dim(v, me, 1, axis=0), (me, 0, 0)) for o, v in zip(from_xy, chip_sum)]

    def sum_body(*refs):
        for t in range(nt):
            acc = refs[t][0].astype(F32)
            for s in range(1, nsh):
                acc = acc + refs[t][s].astype(F32)
            refs[nt + t][...] = acc

    reduced = pl.pallas_call(
        sum_body, name=name + "_xy_add", grid=(cs // tr,), in_specs=[pl.BlockSpec((nsh, tr, d), lambda i: (0, i, 0))] * nt,
        out_specs=[pl.BlockSpec((tr, d), lambda i: (i, 0))] * nt, out_shape=[jax.ShapeDtypeStruct((cs, d), F32)] * nt,
        compiler_params=_cp("parallel"),
    )(*from_xy)

    def share_body(*refs):
        ins, outs = refs[:nt], refs[nt:2 * nt]
        send_sems, recv_sems = refs[2 * nt:]
        kx, ky, kc = _place()
        sib = (kx, ky, 1 - kc)
        cps = [pltpu.make_async_remote_copy(src_ref=ins[t], dst_ref=outs[t].at[kc], send_sem=send_sems.at[t],
                                            recv_sem=recv_sems.at[t], device_id=sib, device_id_type=MESH) for t in range(nt)]
        for cp in cps:
            cp.start()
        for t in range(nt):
            pltpu.make_async_remote_copy(src_ref=ins[t], dst_ref=outs[t].at[1 - kc], send_sem=send_sems.at[t],
                                         recv_sem=recv_sems.at[t], device_id=sib, device_id_type=MESH).wait_recv()
        for cp in cps:
            cp.wait_send()

    both = pl.pallas_call(
        share_body, name=name + "_share", in_specs=[ANY] * nt, out_specs=[ANY] * nt,
        out_shape=[jax.ShapeDtypeStruct((2, cs, d), F32)] * nt,
        scratch_shapes=[pltpu.SemaphoreType.DMA((nt,)), pltpu.SemaphoreType.DMA((nt,))],
    )(*reduced)
    return [lax.dynamic_update_slice(o, v[None], (mc, 0, 0)) for o, v in zip(both, reduced)]


def add_kept_half(g, r, c_idx, *, name, out_dtype):
    n, _, h, w = g.shape
    th = _row_tile(h)

    def body(c_ref, g_ref, r_ref, o_ref):
        o_ref[...] = (g_ref[...] + r_ref[...]).astype(o_ref.dtype)

    return pl.pallas_call(
        body, name=name,
        grid_spec=pltpu.PrefetchScalarGridSpec(
            num_scalar_prefetch=1, grid=(n, h // th),
            in_specs=[pl.BlockSpec((None, None, th, w), lambda s, i, c_ref: (s, c_ref[0], i, 0)),
                      pl.BlockSpec((None, th, w), lambda s, i, c_ref: (s, i, 0))],
            out_specs=pl.BlockSpec((None, th, w), lambda s, i, c_ref: (s, i, 0))),
        out_shape=jax.ShapeDtypeStruct((n, h, w), out_dtype), compiler_params=_cp("parallel", "parallel"),
    )(c_idx, g, r)


def sum_slots(v, *, name):
    n, rows, w = v.shape
    tr = _row_tile(rows, 256)

    def body(v_ref, o_ref):
        acc = v_ref[0].astype(F32)
        for s in range(1, n):
            acc = acc + v_ref[s].astype(F32)
        o_ref[...] = acc

    return pl.pallas_call(body, name=name, grid=(rows // tr,), in_specs=[pl.BlockSpec((n, tr, w), lambda i: (0, i, 0))],
                          out_specs=pl.BlockSpec((tr, w), lambda i: (i, 0)), out_shape=jax.ShapeDtypeStruct((rows, w), F32),
                          compiler_params=_cp("parallel"))(v)


def ada_fwd(c_rows, w_ada, b_shard, *, name):
    nl, d, ncol = w_ada.shape
    rows = c_rows.shape[0]
    tn = _tile(ncol, (768, 512, 256, 128))

    def body(c_ref, w_ref, b_ref, o_ref):
        o_ref[...] = jnp.dot(jax.nn.silu(c_ref[...]), w_ref[...], precision=HI, preferred_element_type=F32) + b_ref[...]

    return pl.pallas_call(
        body, name=name, grid=(nl, ncol // tn),
        in_specs=[pl.BlockSpec((rows, d), lambda l, j: (0, 0)), pl.BlockSpec((None, d, tn), lambda l, j: (l, 0, j)),
                  pl.BlockSpec((None, 1, tn), lambda l, j: (l, 0, j))],
        out_specs=pl.BlockSpec((None, rows, tn), lambda l, j: (l, 0, j)),
        out_shape=jax.ShapeDtypeStruct((nl, rows, ncol), F32), compiler_params=_cp("parallel", "parallel"),
    )(c_rows, w_ada, b_shard)


def ada_bwd(c_rows, w_ada, dm_shard, dm_full, n_ex, *, name):
    nl, d, ncol = w_ada.shape
    rows = c_rows.shape[0]
    tn = _tile(ncol, (768, 512, 256, 128))
    nj = ncol // tn

    def body(c_ref, w_ref, dm_ref, dmf_ref, gw_ref, gb_ref, dc_ref, dact_ref):
        l, j = pl.program_id(0), pl.program_id(1)
        act, act_vjp = jax.vjp(jax.nn.silu, c_ref[...])
        gw_ref[...] = lax.dot_general(act, dm_ref[...], _TN, precision=HI, preferred_element_type=F32)
        gb_ref[...] = jnp.sum(dmf_ref[...], axis=0, keepdims=True)
        part = lax.dot_general(dm_ref[...], w_ref[...], _NT, precision=HI, preferred_element_type=F32)

        @pl.when((l == 0) & (j == 0))
        def _():
            dact_ref[...] = part

        @pl.when((l > 0) | (j > 0))
        def _():
            dact_ref[...] += part

        @pl.when((l == nl - 1) & (j == nj - 1))
        def _():
            dc, = act_vjp(dact_ref[...])
            dc_ref[...] = jnp.sum(dc[n_ex:, :], axis=0, keepdims=True)

    return pl.pallas_call(
        body, name=name, grid=(nl, nj),
        in_specs=[pl.BlockSpec((rows, d), lambda l, j: (0, 0)), pl.BlockSpec((None, d, tn), lambda l, j: (l, 0, j)),
                  pl.BlockSpec((None, rows, tn), lambda l, j: (l, 0, j)),
                  pl.BlockSpec((None, rows, dm_full.shape[-1]), lambda l, j: (l, 0, 0))],
        out_specs=[pl.BlockSpec((None, d, tn), lambda l, j: (l, 0, j)),
                   pl.BlockSpec((None, 1, dm_full.shape[-1]), lambda l, j: (l, 0, 0)),
                   pl.BlockSpec((1, d), lambda l, j: (0, 0))],
        out_shape=[jax.ShapeDtypeStruct((nl, d, ncol), F32), jax.ShapeDtypeStruct((nl, 1, dm_full.shape[-1]), F32),
                   jax.ShapeDtypeStruct((1, d), F32)],
        scratch_shapes=[pltpu.VMEM((rows, d), F32)], compiler_params=_cp("arbitrary", "arbitrary"),
    )(c_rows, w_ada, dm_shard, dm_full)


def adamw(w, g, m, v, *, name):
    shape = w.shape
    cols = shape[-1]
    rows = int(np.prod(shape[:-1])) if len(shape) > 1 else 1
    tr = _row_tile(rows, 256)

    def body(w_ref, g_ref, m_ref, v_ref, d_ref, nm_ref, nv_ref):
        gg = g_ref[...]
        nm = ADAM_B1 * m_ref[...] + (1.0 - ADAM_B1) * gg
        nv = ADAM_B2 * v_ref[...] + (1.0 - ADAM_B2) * jnp.square(gg)
        m_hat = nm / (1.0 - ADAM_B1 ** ADAM_STEP)
        v_hat = nv / (1.0 - ADAM_B2 ** ADAM_STEP)
        d_ref[...] = -ADAM_LR * (m_hat / (jnp.sqrt(v_hat) + ADAM_EPS) + ADAM_WD * w_ref[...])
        nm_ref[...] = nm
        nv_ref[...] = nv

    spec = pl.BlockSpec((tr, cols), lambda i: (i, 0))
    out = pl.pallas_call(body, name=name, grid=(rows // tr,), in_specs=[spec] * 4, out_specs=[spec] * 3,
                         out_shape=[jax.ShapeDtypeStruct((rows, cols), F32)] * 3, compiler_params=_cp("parallel"),
                         )(*[a.reshape(rows, cols) for a in (w, g, m, v)])
    return tuple(o.reshape(shape) for o in out)


def local_step(h0, target, mods, lw, wf, small, *, lc):
    nb, t, d = h0.shape
    nt, nct = t // TM, lc // TM
    s_len = t - lc
    nl = len(lw)
    nsh = wf.shape[1]
    consts = _post_consts()
    cos_b, sin_b = _rope_tables(s_len, lc, HEAD_DIM, GQA_HEADS)
    cos_m, sin_m = _rope_tables(s_len, lc, MLA_ROPE, MLA_HEADS)
    rc = functools.partial(rowcall, nb=nb, nt=nt, nct=nct)
    flat = lambda a: a.reshape(nb * t, a.shape[-1])
    unflat = lambda a: a.reshape(nb, t, a.shape[-1])
    vec = lambda a: a.reshape(1, -1)

    def ffn_fwd(h, g, mod3, l, base, tag):
        shift, scale, gate = mod3
        n, = rc(tag + "_norm", lambda _, *a: (f_normmod(*a),), [(h, 'tok'), (vec(g), 'full'), (shift, 'mod'), (scale, 'mod')],
                [('tok', d, BF16)])
        gg, uu, act = ffn_up(flat(n), wf, l, base, name=tag + "_up")
        y = unflat(ffn_down(act, wf, l, base, name=tag + "_down"))
        h2, = rc(tag + "_res", lambda _, hh, yy, gt: (hh + 0.5 * gt * yy,), [(h, 'tok'), (y, 'tok'), (gate, 'mod')], [('tok', d, F32)])
        return h2, (h, n, gg, uu, act, y)

    def ffn_bwd(dh2, saved, g, mod3, l, base, tag):
        shift, scale, gate = mod3
        h, n, gg, uu, act, y = saved
        dy, dgate = rc(tag + "_res_bwd", lambda _, dd, yy, gt: (0.5 * gt * dd, jnp.sum(0.5 * yy * dd, axis=0, keepdims=True)),
                       [(dh2, 'tok'), (y, 'tok'), (gate, 'mod')], [('tok', d, BF16), ('mod', d)])
        dw_d = ffn_dw(act, flat(dy), nsh, name=tag + "_down_dw")
        dgg, duu = ffn_down_bwd(flat(dy), gg, uu, wf, l, base, name=tag + "_down_dx")
        dw_g = ffn_dw(dgg, flat(n), nsh, name=tag + "_gate_dw")
        dw_u = ffn_dw(duu, flat(n), nsh, name=tag + "_up_dw")
        dn = unflat(ffn_up_bwd(dgg, duu, wf, l, base, name=tag + "_up_dx"))

        def norm_bwd(_, hh, gn, sh, sc, dnn, dres):
            dh, dg, dsh, dsc = jax.vjp(f_normmod, hh, gn, sh, sc)[1](dnn)
            return dh + dres, dg, dsh, dsc

        dh, dg, dshift, dscale = rc(tag + "_norm_bwd", norm_bwd,
                                    [(h, 'tok'), (vec(g), 'full'), (shift, 'mod'), (scale, 'mod'), (dn, 'tok'), (dh2, 'tok')],
                                    [('tok', d, F32), ('full', (1, d)), ('mod', d), ('mod', d)])
        return dh, dg.reshape(d), (dshift, dscale, dgate), [dw_g, dw_u, dw_d]

    def post_ins(p, sm, w):
        return [(p, ('tokc', MAIN_PAD, 0)), (cos_b, 'pos'), (sin_b, 'pos'), (cos_m, 'pos'), (sin_m, 'pos'),
                (vec(sm['gqa_q_norm']), 'full'), (vec(sm['gqa_k_norm']), 'full'), (vec(sm['mla_q_norm']), 'full'),
                (vec(sm['mla_kv_norm']), 'full'), (w['w_uq'], 'full'), (w['w_ukv'], 'full')] + [(c, 'full') for c in consts]

    def mix_fwd(h, sm, mod3, w, tag):
        shift, scale, gate = mod3
        n, = rc(tag + "_norm", lambda _, *a: (f_normmod(*a),), [(h, 'tok'), (vec(sm['mix_norm']), 'full'), (shift, 'mod'), (scale, 'mod')],
                [('tok', d, BF16)])
        p = unflat(mm_resident(flat(n), w['w_in'], name=tag + "_in"))
        parts = rc(tag + "_post", lambda _, *a: f_post(*a), post_ins(p, sm, w), [('tok', wd, BF16) for wd in POST_WIDTHS])
        aq, ak, av, bq, bk, bv, mqn, mqr, mkn, mkr, mv = parts
        bias = na_expand_bias(sm['na_rel_bias'], tag + "_bias")
        o_a, lse_a = na_fwd(aq, ak, av, bias, lc=lc, name=tag + "_na")
        o_b, lse_b = gqa_fwd(bq, bk, bv, lc=lc, name=tag + "_gqa")
        o_m, lse_m = mla_fwd(mqn, mqr, mkn, mkr, mv, lc=lc, name=tag + "_mla")
        fo = [o_a, o_b, o_m]
        ys = [unflat(mm(flat(o), w[k], name=tag + "_br" + k[-1])) for o, k in zip(fo, ('w_a', 'w_b', 'w_c'))]
        gcols = [(p, ('tokc', d, MAIN_PAD // d + j)) for j in range(3)]
        y, = rc(tag + "_merge", lambda _, *a: (f_merge(*a),), gcols + [(v, 'tok') for v in ys], [('tok', d, BF16)])
        z = unflat(mm(flat(y), w['w_o'], name=tag + "_out"))
        h2, = rc(tag + "_res", lambda _, hh, zz, gt: (hh + gt * zz,), [(h, 'tok'), (z, 'tok'), (gate, 'mod')], [('tok', d, F32)])
        saved = (h, n, p, (aq, ak, av, lse_a, bias), (bq, bk, bv, lse_b), (mqn, mqr, mkn, mkr, mv, lse_m), fo, ys, y, z)
        return h2, saved

    def mix_bwd(dh2, saved, sm, mod3, w, tag):
        shift, scale, gate = mod3
        h, n, p, (aq, ak, av, lse_a, bias), (bq, bk, bv, lse_b), (mqn, mqr, mkn, mkr, mv, lse_m), fo, ys, y, z = saved
        dz, dgate = rc(tag + "_res_bwd", lambda _, dd, zz, gt: (gt * dd, jnp.sum(zz * dd, axis=0, keepdims=True)),
                       [(dh2, 'tok'), (z, 'tok'), (gate, 'mod')], [('tok', d, BF16), ('mod', d)])
        dw_o = mm(flat(y), flat(dz), ta=True, name=tag + "_out_dw")
        dy = unflat(mm(flat(dz), w['w_o'], tb=True, name=tag + "_out_dx"))
        gcols = [(p, ('tokc', d, MAIN_PAD // d + j)) for j in range(3)]

        def merge_bwd(_, ga, gb, gm, ya, yb, ym, dyy):
            dga, dgb, dgm, dya, dyb, dym = jax.vjp(f_merge, ga, gb, gm, ya, yb, ym)[1](dyy)
            return dya, dyb, dym, jnp.concatenate([dga, dgb, dgm], axis=-1)

        dya, dyb, dym, dgl = rc(tag + "_merge_bwd", merge_bwd, gcols + [(v, 'tok') for v in ys] + [(dy, 'tok')],
                                [('tok', d, BF16)] * 3 + [('tok', 3 * d, BF16)])
        dws, dos = {}, []
        for o, dyk, k in zip(fo, (dya, dyb, dym), ('w_a', 'w_b', 'w_c')):
            dws[k] = mm(flat(o), flat(dyk), ta=True, name=tag + "_br" + k[-1] + "_dw")
            dos.append(unflat(mm(flat(dyk), w[k], tb=True, out_dtype=BF16, name=tag + "_br" + k[-1] + "_dx")))
        do_a, do_b, do_m = dos
        daq, dak, dav, dbias = na_bwd(aq, ak, av, bias, lse_a, do_a, lc=lc, name=tag + "_na_bwd")
        dbq, dbk, dbv = gqa_bwd(bq, bk, bv, lse_b, do_b, lc=lc, name=tag + "_gqa_bwd")
        dmqn, dmqr2, dmkn, dmkr, dmv = mla_bwd(mqn, mqr, mkn, mkr, mv, lse_m, do_m, lc=lc, name=tag + "_mla_bwd")
        d_rel = na_reduce_bias(dbias, tag + "_relb")
        cots = [daq, dak, dav, dbq, dbk, dbv, dmqn, dmqr2, dmkn, dmkr, dmv]
        ins = post_ins(p, sm, w)
        n_in = len(ins)

        def post_bwd(_, *a):
            prim, cot, dgl_v = a[:11], list(a[n_in:n_in + 11]), a[-1]
            cot[7] = cot[7][:, :LANE] + cot[7][:, LANE:]
            for j in POST_QK:
                cot[j] = cot[j] * LN2
            outs = jax.vjp(lambda pp, qn, kn, mqn, mkvn, wuq, wukv: f_post(pp, *prim[1:5], qn, kn, mqn, mkvn, wuq, wukv, *a[11:n_in]),
                           prim[0], *prim[5:11])[1](tuple(cot))
            return (jnp.concatenate([outs[0].astype(BF16), dgl_v], axis=-1),) + tuple(outs[1:])

        res = rc(tag + "_post_bwd", post_bwd, ins + [(cv, 'tok') for cv in cots] + [(dgl, 'tok')],
                 [('tok', MAIN_PAD + 3 * d, BF16), ('full', (1, HEAD_DIM)), ('full', (1, HEAD_DIM)), ('full', (1, MLA_Q_RANK)),
                  ('full', (1, MLA_KV_RANK)), ('full', w['w_uq'].shape), ('full', w['w_ukv'].shape)])
        dp, dqn, dkn, dmqn, dmkvn, dw_uq, dw_ukv = res
        dw_in = mm(flat(dp), flat(n), ta=True, name=tag + "_in_dw")
        dn = unflat(mm_resident(flat(dp), w['w_in'], tb=True, name=tag + "_in_dx"))

        def norm_bwd(_, hh, gg, sh, sc, dnn, dres):
            dh, dg, dsh, dsc = jax.vjp(f_normmod, hh, gg, sh, sc)[1](dnn)
            return dh + dres, dg, dsh, dsc

        dh, dg, dshift, dscale = rc(tag + "_norm_bwd", norm_bwd,
                                    [(h, 'tok'), (vec(sm['mix_norm']), 'full'), (shift, 'mod'), (scale, 'mod'), (dn, 'tok'), (dh2, 'tok')],
                                    [('tok', d, F32), ('full', (1, d)), ('mod', d), ('mod', d)])
        dsm = {'mix_norm': dg.reshape(d), 'na_rel_bias': d_rel, 'gqa_q_norm': dqn.reshape(-1), 'gqa_k_norm': dkn.reshape(-1),
               'mla_q_norm': dmqn.reshape(-1), 'mla_kv_norm': dmkvn.reshape(-1)}
        dwl = {'w_in': dw_in, 'w_uq': dw_uq, 'w_ukv': dw_ukv, 'w_o': dw_o, **dws}
        return dh, dsm, (dshift, dscale, dgate), dwl

    h = h0
    saved = []
    for l in range(nl):
        sm = {k: small[k][l] for k in SMALL_LAYER}
        h, s1 = ffn_fwd(h, sm['ffn1_norm'], mods[l][0:3], l, 0, f"l{l}_ffn1")
        h, s2 = mix_fwd(h, sm, mods[l][3:6], lw[l], f"l{l}_mix")
        h, s3 = ffn_fwd(h, sm['ffn2_norm'], mods[l][6:9], l, 3, f"l{l}_ffn2")
        saved.append((sm, s1, s2, s3))

    def final(is_ctx, hh, gg, tgt):
        def loss_fn(hv, gv):
            return 0.5 * jnp.sum(jnp.mean(jnp.square(_rms(hv, gv) - tgt), axis=-1))

        keep = jnp.where(is_ctx, 0.0, 1.0)
        loss, (dh, dg) = jax.value_and_grad(loss_fn, argnums=(0, 1))(hh, gg)
        return dh * keep, jnp.full((1, LANE), loss * keep, F32), dg * keep

    dh, loss, dg_final = rc("final_loss", final, [(h, 'tok'), (vec(small['final_norm']), 'full'), (target, 'lat')],
                            [('tok', d, F32), ('full', (1, LANE)), ('full', (1, d))])

    dsmall = {k: [None] * nl for k in SMALL_LAYER}
    dmods, dlw, dwf = [None] * nl, [None] * nl, [None] * nl
    for l in reversed(range(nl)):
        sm, s1, s2, s3 = saved[l]
        dh, dg3, dm3, dwf2 = ffn_bwd(dh, s3, sm['ffn2_norm'], mods[l][6:9], l, 3, f"l{l}_ffn2")
        dh, dsm, dm2, dlw[l] = mix_bwd(dh, s2, sm, mods[l][3:6], lw[l], f"l{l}_mix")
        dh, dg1, dm1, dwf1 = ffn_bwd(dh, s1, sm['ffn1_norm'], mods[l][0:3], l, 0, f"l{l}_ffn1")
        dmods[l] = list(dm1) + list(dm2) + list(dm3)
        dwf[l] = dwf1 + dwf2
        dsm.update(ffn1_norm=dg1, ffn2_norm=dg3)
        for k in SMALL_LAYER:
            dsmall[k][l] = dsm[k]
    dsmall = {k: jnp.stack(v) for k, v in dsmall.items()}
    dsmall['final_norm'] = dg_final.reshape(d)
    return loss, dh, dmods, dlw, dwf, dsmall


def _pack(parts, pad_rows):
    flat, where, off = [], [], 0
    for a in parts:
        n = _ceil_to(a.size, PACK_W)
        flat.append(jnp.pad(a.reshape(-1), (0, n - a.size)))
        where.append((off, n // PACK_W))
        off += n // PACK_W
    total = _ceil_to(off, pad_rows)
    if total > off:
        flat.append(jnp.zeros(((total - off) * PACK_W,), flat[0].dtype))
    return jnp.concatenate(flat).reshape(total, PACK_W), where


def _unpack(buf, where, shape):
    off, rows = where
    return buf[off:off + rows].reshape(-1)[:int(np.prod(shape))].reshape(shape)


def layer_weights(full, l):
    wi = full['w_in'][l]
    d = wi.shape[0]
    return {
        'w_in': jnp.concatenate([wi[:, :MAIN_W], jnp.zeros((d, MAIN_PAD - MAIN_W), wi.dtype), wi[:, MAIN_W:]], axis=1),
        'w_uq': _heads_to_parts(full['mla_w_uq'][l], MLA_NOPE).astype(F32),
        'w_ukv': _heads_to_parts(full['mla_w_ukv'][l], MLA_NOPE).astype(F32),
        'w_a': full['w_branch_a'][l], 'w_b': full['w_branch_b'][l], 'w_c': full['w_branch_c'][l], 'w_o': full['w_out'][l]}


def layer_grads_by_name(dlw):
    per_name = {k: [] for k, _ in BIG}
    for g in dlw:
        per_name['w_in'].append(jnp.concatenate([g['w_in'][:MAIN_W], g['w_in'][MAIN_PAD:]], axis=0))
        per_name['mla_w_uq'].append(_parts_to_heads(g['w_uq'], MLA_NOPE))
        per_name['mla_w_ukv'].append(_parts_to_heads(g['w_ukv'], MLA_NOPE))
        per_name['w_branch_a'].append(g['w_a'])
        per_name['w_branch_b'].append(g['w_b'])
        per_name['w_branch_c'].append(g['w_c'])
        per_name['w_out'].append(g['w_o'])
    return per_name


def kernel(x, c, ctx, c_ctx, w_ada, b_ada, ffn1_norm, ffn1_w_gate, ffn1_w_up, ffn1_w_down, mix_norm, w_in, na_rel_bias, gqa_q_norm, gqa_k_norm, mla_q_norm, mla_kv_norm, mla_w_uq, mla_w_ukv, w_branch_a, w_branch_b, w_branch_c, w_out, ffn2_norm, ffn2_w_gate, ffn2_w_up, ffn2_w_down, final_norm, loss_target, m_c_ctx, m_w_ada, m_b_ada, m_ffn1_norm, m_ffn1_w_gate, m_ffn1_w_up, m_ffn1_w_down, m_mix_norm, m_w_in, m_na_rel_bias, m_gqa_q_norm, m_gqa_k_norm, m_mla_q_norm, m_mla_kv_norm, m_mla_w_uq, m_mla_w_ukv, m_w_branch_a, m_w_branch_b, m_w_branch_c, m_w_out, m_ffn2_norm, m_ffn2_w_gate, m_ffn2_w_up, m_ffn2_w_down, m_final_norm, v_c_ctx, v_w_ada, v_b_ada, v_ffn1_norm, v_ffn1_w_gate, v_ffn1_w_up, v_ffn1_w_down, v_mix_norm, v_w_in, v_na_rel_bias, v_gqa_q_norm, v_gqa_k_norm, v_mla_q_norm, v_mla_kv_norm, v_mla_w_uq, v_mla_w_ukv, v_w_branch_a, v_w_branch_b, v_w_branch_c, v_w_out, v_ffn2_norm, v_ffn2_w_gate, v_ffn2_w_up, v_ffn2_w_down, v_final_norm):
    args = locals()
    wts = {k: args[k] for k in WEIGHTS}
    mom = {k: args['m_' + k] for k in WEIGHTS}
    var = {k: args['v_' + k] for k in WEIGHTS}
    nb, s_len, d = x.shape
    lc = ctx.shape[1]
    nl = w_ada.shape[0]
    nsh, ndev = 4, 8
    mx, my, mc = _place()
    sidx = 2 * mx + my
    didx = 4 * mx + 2 * my + mc
    assert d % LANE == 0 and MAIN_PAD % d == 0 and lc % TQ == 0 and s_len % TQ == 0 and s_len // GRID_W >= NA_ROWS

    wpack, wwhere = _pack([wts[k].astype(BF16) for k, _ in BIG], 32)
    wall = gather_shards(wpack.reshape(2, -1, PACK_W), name="gather_weights").reshape(nsh, -1, PACK_W)
    full = {}
    for (k, ax), wh in zip(BIG, wwhere):
        shp = wts[k].shape
        parts = jnp.stack([_unpack(wall[s], wh, shp) for s in range(nsh)])
        if ax == 1:
            full[k] = parts.transpose(1, 2, 0, 3).reshape(nl, shp[1], nsh * shp[2])
        else:
            full[k] = parts.transpose(1, 0, 2, 3).reshape(nl, nsh * shp[1], shp[2])
    lw = [layer_weights(full, l) for l in range(nl)]
    wl = jnp.stack([(wts[k].transpose(0, 2, 1) if tr else wts[k]).astype(BF16) for k, tr in zip(FFN_NAMES, FFN_TRANSPOSED)], axis=1)
    wf = gather_ffn(wl, name="gather_ffn")

    n_ex = ndev * nb
    ncol = w_ada.shape[-1]
    c_all = all_gather(c, name="gather_cond", with_c=True).reshape(n_ex, d)
    c_rows = jnp.concatenate([c_all, jnp.broadcast_to(c_ctx[None], (n_ex, d))], axis=0)
    b_shard = lax.dynamic_slice_in_dim(b_ada, sidx * ncol, ncol, axis=1)[:, None, :]
    mod_sh = ada_fwd(c_rows, w_ada, b_shard, name="ada_fwd")
    mod_all = all_gather(mod_sh, name="gather_mod", with_c=False)
    mod_all = mod_all.transpose(1, 2, 0, 3).reshape(nl, 2 * n_ex, nsh * ncol)
    mod_x = lax.dynamic_slice_in_dim(mod_all, didx * nb, nb, axis=1)
    mod_c = jnp.broadcast_to(mod_all[:, n_ex:n_ex + 1], mod_x.shape)
    mods = [[jnp.stack([mod_c[l, :, j * d:(j + 1) * d], mod_x[l, :, j * d:(j + 1) * d]], axis=1)[:, :, None, :]
             for j in range(N_MOD)] for l in range(nl)]

    small = {k: wts[k] for k in SMALL_LAYER + ['final_norm']}
    h0 = jnp.concatenate([ctx, x], axis=1)
    loss_part, dh0, dmods, dlw, dwf, dsmall = local_step(h0, loss_target, mods, lw, wf, small, lc=lc)
    grad_x = dh0[:, lc:]

    dmod_mine = jnp.stack([jnp.concatenate([m[:, :, 0, :] for m in dmods[l]], axis=-1) for l in range(nl)])
    small_names = SMALL_LAYER + ['final_norm']
    spack, swhere = _pack([loss_part] + [dsmall[k] for k in small_names] + [dmod_mine], 8)
    sall = all_gather(spack, name="gather_small", with_c=True)
    ssum = sum_slots(sall, name="sum_small")
    loss = _unpack(ssum, swhere[0], (1, LANE))[0, 0]
    grads = {k: _unpack(ssum, wh, wts[k].shape) for k, wh in zip(small_names, swhere[1:])}
    off, rows = swhere[-1]
    dm_all = sall[:, off:off + rows].reshape(ndev, -1)[:, :dmod_mine.size].reshape((ndev,) + dmod_mine.shape)
    dm_all = dm_all.transpose(1, 3, 0, 2, 4).reshape(nl, 2, n_ex, N_MOD * d)
    dm_rows = jnp.concatenate([dm_all[:, 1], dm_all[:, 0]], axis=1)
    dm_shard = lax.dynamic_slice_in_dim(dm_rows, sidx * ncol, ncol, axis=2)
    grads['w_ada'], gb, dc_part = ada_bwd(c_rows, w_ada, dm_shard, dm_rows, n_ex, name="ada_bwd")
    grads['b_ada'] = gb.reshape(b_ada.shape)
    dc_all = all_gather(jnp.pad(dc_part, ((0, 7), (0, 0))), name="gather_dcond", with_c=False)
    grads['c_ctx'] = sum_slots(dc_all, name="sum_dcond")[0]

    per_name = layer_grads_by_name(dlw)
    pieces, gwhere, off = [], [], 0
    for k, ax in BIG:
        shp = wts[k].shape
        for g in per_name[k]:
            if ax == 1 and k not in GRAD_TRANSPOSED:
                pieces.append(g.reshape(shp[1], nsh, shp[2]).transpose(1, 0, 2).reshape(nsh, -1))
            else:
                pieces.append(g.reshape(nsh, -1))
        n = int(np.prod(shp))
        if n % PACK_W:
            pieces.append(jnp.zeros((nsh, _ceil_to(n, PACK_W) - n), F32))
        gwhere.append((off, _ceil_to(n, PACK_W) // PACK_W))
        off += _ceil_to(n, PACK_W) // PACK_W
    if off % 128:
        pieces.append(jnp.zeros((nsh, (_ceil_to(off, 128) - off) * PACK_W), F32))
    half = _ceil_to(off, 128) // 2
    gpack = jnp.concatenate(pieces, axis=1).reshape(nsh, 2, half, PACK_W)
    from_pair = pair_exchange_halves(gpack, name="reduce_pair")
    chip_sum = add_kept_half(gpack, from_pair, jnp.reshape(mc, (1,)).astype(jnp.int32), name="reduce_pair_add",
                             out_dtype=BF16)
    from_xy = all_to_all_xy(chip_sum, name="reduce_xy")
    reduced = sum_slots(from_xy, name="reduce_xy_add")
    gfull = pair_all_gather(reduced, name="reduce_share").reshape(2 * half, PACK_W)
    for (k, _), wh in zip(BIG, gwhere):
        shp = wts[k].shape
        grads[k] = (_unpack(gfull, wh, (shp[0], shp[2], shp[1])).transpose(0, 2, 1) if k in GRAD_TRANSPOSED
                    else _unpack(gfull, wh, shp))
    for k, tr, g in zip(FFN_NAMES, FFN_TRANSPOSED, reduce_ffn(dwf[0], dwf[1], name="reduce_ffn")):
        grads[k] = g.transpose(0, 2, 1) if tr else g

    outs = {k: adamw(wts[k], grads[k], mom[k], var[k], name="adamw_" + k) for k in WEIGHTS}
    return (loss, grad_x, *[grads[k] for k in WEIGHTS], *[outs[k][0] for k in WEIGHTS], *[outs[k][1] for k in WEIGHTS],
            *[outs[k][2] for k in WEIGHTS])
```

```python
import functools

import jax
import jax.numpy as jnp
import numpy as np
from jax import lax
from jax.experimental import pallas as pl
from jax.experimental.pallas import tpu as pltpu

F32 = jnp.float32
BF16 = jnp.bfloat16
HI = lax.Precision.HIGHEST
MESH = pl.DeviceIdType.MESH
ANY = pl.BlockSpec(memory_space=pl.ANY)

V7X_VMEM_BYTES = 64 * 1024 * 1024
VMEM_LIMIT = V7X_VMEM_BYTES - 8 * 1024 * 1024
LANE = 128
PACK_W = 1024

GRID_W = 64
HEAD_DIM = 64
NA_HEADS, NA_ROWS, NA_COLS = 4, 8, 16
GQA_HEADS, GQA_KV_HEADS = 8, 2
MLA_HEADS, MLA_Q_RANK, MLA_KV_RANK, MLA_NOPE, MLA_ROPE, MLA_V = 4, 256, 128, 64, 32, 64
N_MOD = 9
ROPE_THETA = 10000.0
EPS = 1e-6
NEG_BIG = -1e30
NA_W = NA_HEADS * HEAD_DIM
GQ_W = GQA_HEADS * HEAD_DIM
GK_W = GQA_KV_HEADS * HEAD_DIM
MAIN_W = 3 * NA_W + GQ_W + 2 * GK_W + MLA_Q_RANK + MLA_KV_RANK + MLA_ROPE
MAIN_PAD = 2048
LOG2E, LN2 = float(np.log2(np.e)), float(np.log(2.0))
Q_SCALE = HEAD_DIM ** -0.5 * LOG2E
MLA_Q_SCALE = (MLA_NOPE + MLA_ROPE) ** -0.5 * LOG2E
TQ = 256
TM = 256

ADAM_LR, ADAM_B1, ADAM_B2, ADAM_EPS, ADAM_WD, ADAM_STEP = 0.001, 0.9, 0.999, 1e-08, 0.01, 10

ARG_NAMES = ['x', 'c', 'ctx', 'c_ctx', 'w_ada', 'b_ada', 'ffn1_norm', 'ffn1_w_gate', 'ffn1_w_up', 'ffn1_w_down', 'mix_norm', 'w_in',
             'na_rel_bias', 'gqa_q_norm', 'gqa_k_norm', 'mla_q_norm', 'mla_kv_norm', 'mla_w_uq', 'mla_w_ukv', 'w_branch_a',
             'w_branch_b', 'w_branch_c', 'w_out', 'ffn2_norm', 'ffn2_w_gate', 'ffn2_w_up', 'ffn2_w_down', 'final_norm']
WEIGHTS = ARG_NAMES[3:]
BIG = [('w_in', 1), ('mla_w_uq', 1), ('mla_w_ukv', 1), ('w_branch_a', 1), ('w_branch_b', 1), ('w_branch_c', 1), ('w_out', 0)]
GRAD_TRANSPOSED = ('w_in',)
FFN_NAMES = ['ffn1_w_gate', 'ffn1_w_up', 'ffn1_w_down', 'ffn2_w_gate', 'ffn2_w_up', 'ffn2_w_down']
FFN_TRANSPOSED = [True, True, False, True, True, False]
SMALL_LAYER = ['ffn1_norm', 'mix_norm', 'na_rel_bias', 'gqa_q_norm', 'gqa_k_norm', 'mla_q_norm', 'mla_kv_norm', 'ffn2_norm']


def _cp(*sem):
    return pltpu.CompilerParams(dimension_semantics=sem, vmem_limit_bytes=VMEM_LIMIT)


def _tile(dim, cands):
    for t in cands:
        if dim % t == 0:
            return t
    return dim


def _row_tile(rows, cap=512, mult=16):
    best = None
    for t in range(mult, min(rows, cap) + 1, mult):
        if rows % t == 0:
            best = t
    return best or rows


def _ceil_to(n, m):
    return -(-n // m) * m


def mm(a, b, *, name, ta=False, tb=False, out_dtype=F32, precise=False):
    m, k = (a.shape[1], a.shape[0]) if ta else a.shape
    n = b.shape[0] if tb else b.shape[1]
    tm = _tile(m, (512, 256, 128))
    tn = _tile(n, (1024, 1408, 512, 256, 128))
    tk = _tile(k, (1024, 1408, 512, 256, 128))
    nk = k // tk
    dims = (((0 if ta else 1,), (1 if tb else 0,)), ((), ()))

    def body(a_ref, b_ref, o_ref, *acc):
        if precise:
            part = lax.dot_general(a_ref[...].astype(F32), b_ref[...].astype(F32), dims, precision=HI, preferred_element_type=F32)
        else:
            part = lax.dot_general(a_ref[...].astype(BF16), b_ref[...].astype(BF16), dims, preferred_element_type=F32)
        if nk == 1:
            o_ref[...] = part.astype(o_ref.dtype)
        else:
            acc_ref, = acc
            kk = pl.program_id(2)

            @pl.when(kk == 0)
            def _():
                acc_ref[...] = part

            @pl.when(kk > 0)
            def _():
                acc_ref[...] += part

            @pl.when(kk == nk - 1)
            def _():
                o_ref[...] = acc_ref[...].astype(o_ref.dtype)

    a_spec = pl.BlockSpec((tk, tm), lambda i, j, kk: (kk, i)) if ta else pl.BlockSpec((tm, tk), lambda i, j, kk: (i, kk))
    b_spec = pl.BlockSpec((tn, tk), lambda i, j, kk: (j, kk)) if tb else pl.BlockSpec((tk, tn), lambda i, j, kk: (kk, j))
    return pl.pallas_call(
        body, name=name, grid=(m // tm, n // tn, nk), in_specs=[a_spec, b_spec],
        out_specs=pl.BlockSpec((tm, tn), lambda i, j, kk: (i, j)),
        out_shape=jax.ShapeDtypeStruct((m, n), out_dtype),
        scratch_shapes=[pltpu.VMEM((tm, tn), F32)] if nk > 1 else [],
        compiler_params=_cp("parallel", "parallel", "arbitrary"),
    )(a, b)


def mm_resident(a, w, *, name, tb=False, out_dtype=F32):
    m, k = a.shape
    n = w.shape[0] if tb else w.shape[1]
    tm = _tile(m, (512, 256, 128))
    cn = n if tb else _tile(n, (1024, 512, 256, 128))

    def body(a_ref, w_ref, o_ref):
        aa = a_ref[...].astype(BF16)
        if tb:
            o_ref[...] = _dot(aa, w_ref[...], _NT).astype(o_ref.dtype)
        else:
            for c in range(n // cn):
                cols = slice(cn * c, cn * (c + 1))
                o_ref[:, cols] = _dot(aa, w_ref[:, cols]).astype(o_ref.dtype)

    return pl.pallas_call(
        body, name=name, grid=(m // tm,),
        in_specs=[pl.BlockSpec((tm, k), lambda i: (i, 0)), pl.BlockSpec(w.shape, lambda i: (0, 0), pipeline_mode=pl.Buffered(1))],
        out_specs=pl.BlockSpec((tm, n), lambda i: (i, 0)), out_shape=jax.ShapeDtypeStruct((m, n), out_dtype),
        compiler_params=_cp("parallel"),
    )(a, w)


FFN_GATE, FFN_UP, FFN_DOWN = 0, 1, 2


def _ffn_wspec(wf, l, which):
    _, nsh, _, cs, d = wf.shape
    return pl.BlockSpec((None, nsh, None, cs, d), lambda *_: (l, 0, which, 0, 0), pipeline_mode=pl.Buffered(1))


def _ffn_group(cs):
    for g in (1, 2, 4):
        if (g * cs) % LANE == 0:
            return g
    raise ValueError(cs)


def ffn_up(n, wf, l, base, *, name):
    m, d = n.shape
    nsh, cs = wf.shape[1], wf.shape[3]
    f = nsh * cs
    grp = _ffn_group(cs)
    tm = _tile(m, (512, 256, 128))

    def body(n_ref, wg_ref, wu_ref, g_ref, u_ref, a_ref):
        nn = n_ref[...]
        for c in range(nsh // grp):
            cols = slice(grp * cs * c, grp * cs * (c + 1))
            g = _dot(nn, wg_ref[grp * c:grp * (c + 1)].reshape(grp * cs, d), _NT)
            u = _dot(nn, wu_ref[grp * c:grp * (c + 1)].reshape(grp * cs, d), _NT)
            g_ref[:, cols] = g.astype(BF16)
            u_ref[:, cols] = u.astype(BF16)
            a_ref[:, cols] = f_act_gu(g, u).astype(BF16)

    ospec = pl.BlockSpec((tm, f), lambda i: (i, 0))
    return pl.pallas_call(
        body, name=name, grid=(m // tm,),
        in_specs=[pl.BlockSpec((tm, d), lambda i: (i, 0)), _ffn_wspec(wf, l, base + FFN_GATE), _ffn_wspec(wf, l, base + FFN_UP)],
        out_specs=[ospec] * 3, out_shape=[jax.ShapeDtypeStruct((m, f), BF16)] * 3, compiler_params=_cp("parallel"),
    )(n, wf, wf)


def ffn_down(act, wf, l, base, *, name):
    m, f = act.shape
    nsh, cs, d = wf.shape[1], wf.shape[3], wf.shape[4]
    tm = _tile(m, (512, 256, 128))

    def body(a_ref, wd_ref, y_ref):
        y_ref[...] = _dot(a_ref[...], wd_ref[...].reshape(f, d))

    return pl.pallas_call(
        body, name=name, grid=(m // tm,),
        in_specs=[pl.BlockSpec((tm, f), lambda i: (i, 0)), _ffn_wspec(wf, l, base + FFN_DOWN)],
        out_specs=pl.BlockSpec((tm, d), lambda i: (i, 0)), out_shape=jax.ShapeDtypeStruct((m, d), F32), compiler_params=_cp("parallel"),
    )(act, wf)


def ffn_down_bwd(dy, g, u, wf, l, base, *, name):
    m, d = dy.shape
    nsh, cs = wf.shape[1], wf.shape[3]
    f = nsh * cs
    grp = _ffn_group(cs)
    tm = _tile(m, (512, 256, 128))

    def body(dy_ref, g_ref, u_ref, wd_ref, dg_ref, du_ref):
        dd = dy_ref[...]
        for c in range(nsh // grp):
            cols = slice(grp * cs * c, grp * cs * (c + 1))
            dact = _dot(dd, wd_ref[grp * c:grp * (c + 1)].reshape(grp * cs, d), _NT)
            dg, du = jax.vjp(f_act_gu, g_ref[:, cols].astype(F32), u_ref[:, cols].astype(F32))[1](dact)
            dg_ref[:, cols] = dg.astype(BF16)
            du_ref[:, cols] = du.astype(BF16)

    fspec = pl.BlockSpec((tm, f), lambda i: (i, 0))
    return pl.pallas_call(
        body, name=name, grid=(m // tm,),
        in_specs=[pl.BlockSpec((tm, d), lambda i: (i, 0)), fspec, fspec, _ffn_wspec(wf, l, base + FFN_DOWN)],
        out_specs=[fspec] * 2, out_shape=[jax.ShapeDtypeStruct((m, f), BF16)] * 2, compiler_params=_cp("parallel"),
    )(dy, g, u, wf)


def ffn_up_bwd(dg, du, wf, l, base, *, name):
    m, f = dg.shape
    nsh, cs, d = wf.shape[1], wf.shape[3], wf.shape[4]
    tm = _tile(m, (512, 256, 128))

    def body(dg_ref, du_ref, wg_ref, wu_ref, dn_ref):
        dn_ref[...] = _dot(dg_ref[...], wg_ref[...].reshape(f, d)) + _dot(du_ref[...], wu_ref[...].reshape(f, d))

    fspec = pl.BlockSpec((tm, f), lambda i: (i, 0))
    return pl.pallas_call(
        body, name=name, grid=(m // tm,),
        in_specs=[fspec, fspec, _ffn_wspec(wf, l, base + FFN_GATE), _ffn_wspec(wf, l, base + FFN_UP)],
        out_specs=pl.BlockSpec((tm, d), lambda i: (i, 0)), out_shape=jax.ShapeDtypeStruct((m, d), F32), compiler_params=_cp("parallel"),
    )(dg, du, wf, wf)


def ffn_dw(a, b, nsh, *, name):
    m, f = a.shape
    d = b.shape[1]
    cs = f // nsh
    grp = _ffn_group(cs)
    tm = _tile(m, (1024, 512, 256, 128))

    def body(a_ref, b_ref, o_ref):
        part = _dot(a_ref[...], b_ref[...], _TN).reshape(grp, cs, d)
        i = pl.program_id(1)

        @pl.when(i == 0)
        def _():
            o_ref[...] = part

        @pl.when(i > 0)
        def _():
            o_ref[...] += part

    return pl.pallas_call(
        body, name=name, grid=(nsh // grp, m // tm),
        in_specs=[pl.BlockSpec((tm, grp * cs), lambda j, i: (i, j)), pl.BlockSpec((tm, d), lambda j, i: (i, 0))],
        out_specs=pl.BlockSpec((grp, cs, d), lambda j, i: (j, 0, 0)), out_shape=jax.ShapeDtypeStruct((nsh, cs, d), F32),
        compiler_params=_cp("parallel", "arbitrary"),
    )(a, b)


def rowcall(name, fn, ins, outs, *, nb, nt, nct):
    in_specs, arrays = [], []
    for arr, kind in ins:
        arrays.append(arr)
        if kind == 'tok':
            in_specs.append(pl.BlockSpec((None, TM, arr.shape[-1]), lambda b, t: (b, t, 0)))
        elif kind == 'lat':
            in_specs.append(pl.BlockSpec((None, TM, arr.shape[-1]), lambda b, t: (b, jnp.maximum(t - nct, 0), 0)))
        elif kind == 'pos':
            in_specs.append(pl.BlockSpec((TM, arr.shape[-1]), lambda b, t: (t, 0)))
        elif kind == 'mod':
            in_specs.append(pl.BlockSpec((None, None, 1, arr.shape[-1]), lambda b, t: (b, jnp.where(t >= nct, 1, 0), 0, 0)))
        elif kind == 'full':
            in_specs.append(pl.BlockSpec(arr.shape, lambda b, t, nd=arr.ndim: (0,) * nd))
        else:
            _, w, j = kind
            in_specs.append(pl.BlockSpec((None, TM, w), lambda b, t, j=j: (b, t, j)))
    out_specs, out_shape = [], []
    for o in outs:
        if o[0] == 'tok':
            out_specs.append(pl.BlockSpec((None, TM, o[1]), lambda b, t: (b, t, 0)))
            out_shape.append(jax.ShapeDtypeStruct((nb, nt * TM, o[1]), o[2]))
        elif o[0] == 'mod':
            out_specs.append(pl.BlockSpec((None, None, 1, o[1]), lambda b, t: (b, jnp.where(t >= nct, 1, 0), 0, 0)))
            out_shape.append(jax.ShapeDtypeStruct((nb, 2, 1, o[1]), F32))
        else:
            out_specs.append(pl.BlockSpec(o[1], lambda b, t, nd=len(o[1]): (0,) * nd))
            out_shape.append(jax.ShapeDtypeStruct(o[1], F32))
    n_in = len(ins)

    def body(*refs):
        b, t = pl.program_id(0), pl.program_id(1)
        res = fn(t < nct, *[r[...] for r in refs[:n_in]])
        for ref, o, val in zip(refs[n_in:], outs, res, strict=True):
            if o[0] == 'tok':
                ref[...] = val.astype(ref.dtype)
                continue
            first = ((t == 0) | (t == nct)) if o[0] == 'mod' else ((b == 0) & (t == 0))

            @pl.when(first)
            def _(ref=ref, val=val):
                ref[...] = val

            @pl.when(jnp.logical_not(first))
            def _(ref=ref, val=val):
                ref[...] += val

    return pl.pallas_call(body, name=name, grid=(nb, nt), in_specs=in_specs, out_specs=out_specs, out_shape=out_shape,
                          compiler_params=_cp("arbitrary", "arbitrary"))(*arrays)


def _rms(x, g):
    return x * lax.rsqrt(jnp.mean(x * x, axis=-1, keepdims=True) + EPS) * g


def f_normmod(h, g, shift, scale):
    return _rms(h, g) * (1.0 + scale) + shift


def f_act_gu(g, u):
    return jax.nn.silu(g) * u


def _dot_split(x, m, dims):
    hi = x.astype(BF16)
    lo = (x - hi.astype(F32)).astype(BF16)
    mb = m.astype(BF16)
    return (lax.dot_general(hi, mb, dims, preferred_element_type=F32) + lax.dot_general(lo, mb, dims, preferred_element_type=F32))


def dot_select(x, m):
    return _dot_select(x, m)


@jax.custom_vjp
def _dot_select(x, m):
    return _dot_split(x, m, (((1,), (0,)), ((), ())))


_dot_select.defvjp(lambda x, m: (_dot_split(x, m, (((1,), (0,)), ((), ()))), m),
                   lambda m, ct: (_dot_split(ct, m, (((1,), (1,)), ((), ()))), jnp.zeros_like(m)))


def f_merge(ga, gb, gm, ya, yb, ym):
    return jax.nn.sigmoid(ga) * ya + jax.nn.sigmoid(gb) * yb + jax.nn.sigmoid(gm) * ym


def f_post(p, cb, sb, cm, sm, qn, kn, mqn, mkvn, wuq, wukv, s_b, r_b, t_b, r_m, rep, dup):
    def hnorm(x, g, w):
        ms = dot_select(x * x, s_b[:w, :w])
        gw = dot_select(g, t_b[:, :w])
        return x * lax.rsqrt(ms + EPS) * gw

    def rope(x, cos, sin, rot):
        return x * cos + dot_select(x, rot) * sin

    o = 3 * NA_W
    a_q, a_k, a_v = p[:, 0:NA_W], p[:, NA_W:2 * NA_W], p[:, 2 * NA_W:o]
    b_q = rope(hnorm(p[:, o:o + GQ_W], qn, GQ_W), cb, sb, r_b)
    o += GQ_W
    b_k = rope(hnorm(p[:, o:o + GK_W], kn, GK_W), cb[:, :GK_W], sb[:, :GK_W], r_b[:GK_W, :GK_W])
    b_v = p[:, o + GK_W:o + 2 * GK_W]
    o += 2 * GK_W
    q_lat = jnp.dot(_rms(p[:, o:o + MLA_Q_RANK], mqn).astype(BF16), wuq.astype(BF16), preferred_element_type=F32)
    o += MLA_Q_RANK
    kv_lat = jnp.dot(_rms(p[:, o:o + MLA_KV_RANK], mkvn).astype(BF16), wukv.astype(BF16), preferred_element_type=F32)
    o += MLA_KV_RANK
    nw = MLA_HEADS * MLA_NOPE
    mq_nope, mq_rope = q_lat[:, :nw], rope(q_lat[:, nw:], cm, sm, r_m)
    mk_nope, m_v = kv_lat[:, :nw], kv_lat[:, nw:]
    mk_rope = dot_select(rope(p[:, o:o + LANE], cm, sm, r_m), rep)
    b_k2 = dot_select(b_k, dup)
    b_v2 = dot_select(b_v, dup)
    return (a_q * Q_SCALE, a_k, a_v, b_q * Q_SCALE, b_k2, b_v2, mq_nope * MLA_Q_SCALE, mq_rope * MLA_Q_SCALE, mk_nope, mk_rope, m_v)


POST_QK = (0, 1, 3, 4, 6, 7, 8, 9)


POST_WIDTHS = (NA_W, NA_W, NA_W, GQ_W, 2 * GK_W, 2 * GK_W, MLA_HEADS * MLA_NOPE, MLA_HEADS * MLA_ROPE, MLA_HEADS * MLA_NOPE,
               MLA_HEADS * MLA_ROPE, MLA_HEADS * MLA_V)


_NT = (((1,), (1,)), ((), ()))
_TN = (((0,), (0,)), ((), ()))


def _dot(a, b, dims=None):
    if dims is None:
        return jnp.dot(a, b, preferred_element_type=F32)
    return lax.dot_general(a, b, dims, preferred_element_type=F32)


def _lanes(lo, width):
    lane = lax.broadcasted_iota(jnp.int32, (1, LANE), 1)
    return (lane >= lo) & (lane < lo + width)


def _only(x, mask):
    return jnp.where(mask, x, jnp.zeros_like(x))


def _stack_pair(x, width, lo):
    return jnp.concatenate([_only(x, _lanes(lo, width)), _only(x, _lanes(lo + width, width))], axis=0)


def _pair_softmax(s):
    m = jnp.max(s, axis=-1, keepdims=True)
    p = jnp.exp2(s - m)
    l = jnp.sum(p, axis=-1, keepdims=True)
    return p, l, m + jnp.log2(l)


def gqa_fwd(q, k2, v2, *, lc, name):
    nb, t, qw = q.shape
    npair = qw // LANE
    per_kv = npair // GQA_KV_HEADS
    nctb = lc // TQ

    def body(q_ref, k_ref, v_ref, o_ref, lse_ref):
        i = pl.program_id(2)

        def run(rows):
            kk, vv = k_ref[rows, :], v_ref[rows, :]
            outs = []
            for e in range(2):
                p, l, lse = _pair_softmax(_dot(_only(q_ref[...], _lanes(HEAD_DIM * e, HEAD_DIM)), kk, _NT))
                outs.append(_dot(p.astype(BF16), vv) / l)
                lse_ref[e] = lse
            o_ref[...] = jnp.where(_lanes(0, HEAD_DIM), outs[0], outs[1]).astype(o_ref.dtype)

        @pl.when(i < nctb)
        def _():
            run(pl.ds(0, lc))

        @pl.when(i >= nctb)
        def _():
            run(pl.ds(0, t))

    qmap = lambda b, p, i: (b, i, p)
    kmap = lambda b, p, i: (b, 0, p // per_kv)
    return pl.pallas_call(
        body, name=name, grid=(nb, npair, t // TQ),
        in_specs=[pl.BlockSpec((None, TQ, LANE), qmap), pl.BlockSpec((None, t, LANE), kmap), pl.BlockSpec((None, t, LANE), kmap)],
        out_specs=[pl.BlockSpec((None, TQ, LANE), qmap), pl.BlockSpec((None, 2, TQ, 1), lambda b, p, i: (b, p, i, 0))],
        out_shape=[jax.ShapeDtypeStruct((nb, t, qw), BF16), jax.ShapeDtypeStruct((nb, 2 * npair, t, 1), F32)],
        compiler_params=_cp("parallel", "parallel", "arbitrary"),
    )(q, k2, v2)


def gqa_bwd(q, k2, v2, lse, do, *, lc, name):
    nb, t, qw = q.shape
    npair = qw // LANE
    per_kv = npair // GQA_KV_HEADS
    nctb = lc // TQ

    def body(q_ref, k_ref, v_ref, lse_ref, do_ref, dq_ref, dk_ref, dv_ref):
        g, i = pl.program_id(2), pl.program_id(3)

        @pl.when((g == 0) & (i == 0))
        def _():
            dk_ref[...] = jnp.zeros_like(dk_ref)
            dv_ref[...] = jnp.zeros_like(dv_ref)

        def run(rows):
            kk, vv = k_ref[rows, :], v_ref[rows, :]
            qq, dd = _stack_pair(q_ref[...], HEAD_DIM, 0), _stack_pair(do_ref[...], HEAD_DIM, 0)
            p = jnp.exp2(_dot(qq, kk, _NT) - jnp.concatenate([lse_ref[0], lse_ref[1]], axis=0))
            dp = _dot(dd, vv, _NT)
            delta = jnp.sum(p * dp, axis=-1, keepdims=True)
            ds = (p * (dp - delta)).astype(BF16)
            dq = _dot(ds, kk)
            dq_ref[...] = jnp.where(_lanes(0, HEAD_DIM), dq[:TQ], dq[TQ:])
            dk_ref[rows, :] += _dot(ds, qq, _TN)
            dv_ref[rows, :] += _dot(p.astype(BF16), dd, _TN)

        @pl.when(i < nctb)
        def _():
            run(pl.ds(0, lc))

        @pl.when(i >= nctb)
        def _():
            run(pl.ds(0, t))

    qmap = lambda b, j, g, i: (b, i, j * per_kv + g)
    kmap = lambda b, j, g, i: (b, 0, j)
    return pl.pallas_call(
        body, name=name, grid=(nb, GQA_KV_HEADS, per_kv, t // TQ),
        in_specs=[pl.BlockSpec((None, TQ, LANE), qmap), pl.BlockSpec((None, t, LANE), kmap), pl.BlockSpec((None, t, LANE), kmap),
                  pl.BlockSpec((None, 2, TQ, 1), lambda b, j, g, i: (b, j * per_kv + g, i, 0)), pl.BlockSpec((None, TQ, LANE), qmap)],
        out_specs=[pl.BlockSpec((None, TQ, LANE), qmap), pl.BlockSpec((None, t, LANE), kmap), pl.BlockSpec((None, t, LANE), kmap)],
        out_shape=[jax.ShapeDtypeStruct((nb, t, qw), F32), jax.ShapeDtypeStruct(k2.shape, F32), jax.ShapeDtypeStruct(v2.shape, F32)],
        compiler_params=_cp("arbitrary", "arbitrary", "arbitrary", "arbitrary"),
    )(q, k2, v2, lse, do)


def mla_fwd(qn, qr, kn, kr, v, *, lc, name):
    nb, t, w = qn.shape
    npair = w // LANE
    nctb = lc // TQ

    def body(qn_ref, qr_ref, kn_ref, kr_ref, v_ref, o_ref, lse_ref):
        pr, i = pl.program_id(1), pl.program_id(2)

        def run(rows):
            kk, kkr, vv = kn_ref[rows, :], kr_ref[rows, :], v_ref[rows, :]
            outs = []
            for e in range(2):
                s = (_dot(_only(qn_ref[...], _lanes(MLA_NOPE * e, MLA_NOPE)), kk, _NT)
                     + _dot(_only(qr_ref[...], _lanes(MLA_ROPE * (2 * pr + e), MLA_ROPE)), kkr, _NT))
                p, l, lse = _pair_softmax(s)
                outs.append(_dot(p.astype(BF16), vv) / l)
                lse_ref[e] = lse
            o_ref[...] = jnp.where(_lanes(0, MLA_V), outs[0], outs[1]).astype(o_ref.dtype)

        @pl.when(i < nctb)
        def _():
            run(pl.ds(0, lc))

        @pl.when(i >= nctb)
        def _():
            run(pl.ds(0, t))

    qmap = lambda b, p, i: (b, i, p)
    rmap = lambda b, p, i: (b, i, 0)
    kmap = lambda b, p, i: (b, 0, p)
    return pl.pallas_call(
        body, name=name, grid=(nb, npair, t // TQ),
        in_specs=[pl.BlockSpec((None, TQ, LANE), qmap), pl.BlockSpec((None, TQ, LANE), rmap), pl.BlockSpec((None, t, LANE), kmap),
                  pl.BlockSpec((None, t, LANE), lambda b, p, i: (b, 0, 0)), pl.BlockSpec((None, t, LANE), kmap)],
        out_specs=[pl.BlockSpec((None, TQ, LANE), qmap), pl.BlockSpec((None, 2, TQ, 1), lambda b, p, i: (b, p, i, 0))],
        out_shape=[jax.ShapeDtypeStruct((nb, t, w), BF16), jax.ShapeDtypeStruct((nb, 2 * npair, t, 1), F32)],
        compiler_params=_cp("parallel", "parallel", "arbitrary"),
    )(qn, qr, kn, kr, v)


def mla_bwd(qn, qr, kn, kr, v, lse, do, *, lc, name):
    nb, t, w = qn.shape
    npair = w // LANE
    nctb = lc // TQ

    def body(qn_ref, qr_ref, kn_ref, kr_ref, v_ref, lse_ref, do_ref, dqn_ref, dqr_ref, dkn_ref, dkr_ref, dv_ref):
        pr, i = pl.program_id(1), pl.program_id(2)

        @pl.when(i == 0)
        def _():
            dkn_ref[...] = jnp.zeros_like(dkn_ref)
            dv_ref[...] = jnp.zeros_like(dv_ref)

        @pl.when((i == 0) & (pr == 0))
        def _():
            dkr_ref[...] = jnp.zeros_like(dkr_ref)

        def run(rows):
            kk, kkr, vv = kn_ref[rows, :], kr_ref[rows, :], v_ref[rows, :]
            r_lo = 2 * MLA_ROPE * pr
            qq, qqr = _stack_pair(qn_ref[...], MLA_NOPE, 0), _stack_pair(qr_ref[...], MLA_ROPE, r_lo)
            dd = _stack_pair(do_ref[...], MLA_V, 0)
            p = jnp.exp2(_dot(qq, kk, _NT) + _dot(qqr, kkr, _NT) - jnp.concatenate([lse_ref[0], lse_ref[1]], axis=0))
            dp = _dot(dd, vv, _NT)
            delta = jnp.sum(p * dp, axis=-1, keepdims=True)
            ds = (p * (dp - delta)).astype(BF16)
            dqn, dqr = _dot(ds, kk), _dot(ds, kkr)
            dqn_ref[...] = jnp.where(_lanes(0, MLA_NOPE), dqn[:TQ], dqn[TQ:])
            dqr_ref[...] = _only(dqr[:TQ], _lanes(r_lo, MLA_ROPE)) + _only(dqr[TQ:], _lanes(r_lo + MLA_ROPE, MLA_ROPE))
            dkn_ref[rows, :] += _dot(ds, qq, _TN)
            dkr_ref[rows, :] += _dot(ds, qqr, _TN)
            dv_ref[rows, :] += _dot(p.astype(BF16), dd, _TN)

        @pl.when(i < nctb)
        def _():
            run(pl.ds(0, lc))

        @pl.when(i >= nctb)
        def _():
            run(pl.ds(0, t))

    qmap = lambda b, p, i: (b, i, p)
    rmap = lambda b, p, i: (b, i, 0)
    kmap = lambda b, p, i: (b, 0, p)
    zmap = lambda b, p, i: (b, 0, 0)
    return pl.pallas_call(
        body, name=name, grid=(nb, npair, t // TQ),
        in_specs=[pl.BlockSpec((None, TQ, LANE), qmap), pl.BlockSpec((None, TQ, LANE), rmap), pl.BlockSpec((None, t, LANE), kmap),
                  pl.BlockSpec((None, t, LANE), zmap), pl.BlockSpec((None, t, LANE), kmap),
                  pl.BlockSpec((None, 2, TQ, 1), lambda b, p, i: (b, p, i, 0)), pl.BlockSpec((None, TQ, LANE), qmap)],
        out_specs=[pl.BlockSpec((None, TQ, LANE), qmap), pl.BlockSpec((None, TQ, LANE), qmap), pl.BlockSpec((None, t, LANE), kmap),
                   pl.BlockSpec((None, t, LANE), zmap), pl.BlockSpec((None, t, LANE), kmap)],
        out_shape=[jax.ShapeDtypeStruct((nb, t, w), F32), jax.ShapeDtypeStruct((nb, t, npair * LANE), F32),
                   jax.ShapeDtypeStruct((nb, t, w), F32), jax.ShapeDtypeStruct((nb, t, LANE), F32), jax.ShapeDtypeStruct((nb, t, w), F32)],
        compiler_params=_cp("arbitrary", "arbitrary", "arbitrary"),
    )(qn, qr, kn, kr, v, lse, do)


def _na_window(st, nc, rows):
    r = jnp.maximum(st - nc, 0)
    r0 = jnp.clip(r - NA_ROWS // 2, 0, rows - NA_ROWS)
    return r, r0, r - r0


def na_fwd(q, k, v, bias, *, lc, name):
    nb, t, w = q.shape
    npair = w // LANE
    nc, rows = lc // GRID_W, (t - lc) // GRID_W
    nwin = NA_ROWS * GRID_W

    def body(q_ref, k_ref, v_ref, bias_ref, o_ref, lse_ref):
        st = pl.program_id(2)
        ctx = pl.ds(0, lc)
        kc, vc = k_ref[ctx, :], v_ref[ctx, :]
        outs = [None, None]

        @pl.when(st < nc)
        def _():
            for e in range(2):
                p, l, lse = _pair_softmax(_dot(_only(q_ref[...], _lanes(HEAD_DIM * e, HEAD_DIM)), kc, _NT))
                outs[e] = _dot(p.astype(BF16), vc) / l
                lse_ref[e] = lse
            o_ref[...] = jnp.where(_lanes(0, HEAD_DIM), outs[0], outs[1]).astype(o_ref.dtype)

        @pl.when(st >= nc)
        def _():
            _, r0, _ = _na_window(st, nc, rows)
            win = pl.ds(pl.multiple_of(lc + r0 * GRID_W, GRID_W), nwin)
            kw, vw = k_ref[win, :], v_ref[win, :]
            qq = _stack_pair(q_ref[...], HEAD_DIM, 0)
            s_loc = _dot(qq, kw, _NT) + jnp.concatenate([bias_ref[0], bias_ref[1]], axis=0) * LOG2E
            s_ctx = _dot(qq, kc, _NT)
            m = jnp.maximum(jnp.max(s_loc, axis=-1, keepdims=True), jnp.max(s_ctx, axis=-1, keepdims=True))
            p_loc, p_ctx = jnp.exp2(s_loc - m), jnp.exp2(s_ctx - m)
            l = jnp.sum(p_loc, axis=-1, keepdims=True) + jnp.sum(p_ctx, axis=-1, keepdims=True)
            o = (_dot(p_loc.astype(BF16), vw) + _dot(p_ctx.astype(BF16), vc)) / l
            lse = m + jnp.log2(l)
            lse_ref[0], lse_ref[1] = lse[:GRID_W], lse[GRID_W:]
            o_ref[...] = jnp.where(_lanes(0, HEAD_DIM), o[:GRID_W], o[GRID_W:]).astype(o_ref.dtype)

    qmap = lambda p, b, st: (b, st, p)
    kmap = lambda p, b, st: (b, 0, p)
    return pl.pallas_call(
        body, name=name, grid=(npair, nb, nc + rows),
        in_specs=[pl.BlockSpec((None, GRID_W, LANE), qmap), pl.BlockSpec((None, t, LANE), kmap), pl.BlockSpec((None, t, LANE), kmap),
                  pl.BlockSpec((2, None, GRID_W, nwin), lambda p, b, st: (p, _na_window(st, nc, rows)[2], 0, 0))],
        out_specs=[pl.BlockSpec((None, GRID_W, LANE), qmap), pl.BlockSpec((None, 2, GRID_W, 1), lambda p, b, st: (b, p, st, 0))],
        out_shape=[jax.ShapeDtypeStruct((nb, t, w), BF16), jax.ShapeDtypeStruct((nb, 2 * npair, t, 1), F32)],
        compiler_params=_cp("parallel", "parallel", "arbitrary"),
    )(q, k, v, bias)


def na_bwd(q, k, v, bias, lse, do, *, lc, name):
    nb, t, w = q.shape
    npair = w // LANE
    nc, rows = lc // GRID_W, (t - lc) // GRID_W
    nwin = NA_ROWS * GRID_W

    def body(q_ref, k_ref, v_ref, bias_ref, lse_ref, do_ref, dq_ref, dk_ref, dv_ref, db_ref):
        b, st = pl.program_id(1), pl.program_id(2)

        @pl.when(st == 0)
        def _():
            dk_ref[...] = jnp.zeros_like(dk_ref)
            dv_ref[...] = jnp.zeros_like(dv_ref)

        @pl.when((st == 0) & (b == 0))
        def _():
            db_ref[...] = jnp.zeros_like(db_ref)

        ctx = pl.ds(0, lc)
        kc, vc = k_ref[ctx, :], v_ref[ctx, :]
        dqs = [None, None]

        @pl.when(st < nc)
        def _():
            for e in range(2):
                mine = _lanes(HEAD_DIM * e, HEAD_DIM)
                qq, dd = _only(q_ref[...], mine), _only(do_ref[...], mine)
                p = jnp.exp2(_dot(qq, kc, _NT) - lse_ref[e])
                dp = _dot(dd, vc, _NT)
                delta = jnp.sum(p * dp, axis=-1, keepdims=True)
                ds = (p * (dp - delta)).astype(BF16)
                dqs[e] = _dot(ds, kc)
                dk_ref[ctx, :] += _dot(ds, qq, _TN)
                dv_ref[ctx, :] += _dot(p.astype(BF16), dd, _TN)
            dq_ref[...] = jnp.where(_lanes(0, HEAD_DIM), dqs[0], dqs[1])

        @pl.when(st >= nc)
        def _():
            _, r0, case = _na_window(st, nc, rows)
            win = pl.ds(pl.multiple_of(lc + r0 * GRID_W, GRID_W), nwin)
            kw, vw = k_ref[win, :], v_ref[win, :]
            qq, dd = _stack_pair(q_ref[...], HEAD_DIM, 0), _stack_pair(do_ref[...], HEAD_DIM, 0)
            lse = jnp.concatenate([lse_ref[0], lse_ref[1]], axis=0)
            p_loc = jnp.exp2(_dot(qq, kw, _NT) + jnp.concatenate([bias_ref[0], bias_ref[1]], axis=0) * LOG2E - lse)
            p_ctx = jnp.exp2(_dot(qq, kc, _NT) - lse)
            dp_loc, dp_ctx = _dot(dd, vw, _NT), _dot(dd, vc, _NT)
            delta = jnp.sum(p_loc * dp_loc, axis=-1, keepdims=True) + jnp.sum(p_ctx * dp_ctx, axis=-1, keepdims=True)
            ds_loc = p_loc * (dp_loc - delta)
            db_ref[0, case] += ds_loc[:GRID_W]
            db_ref[1, case] += ds_loc[GRID_W:]
            ds_loc = ds_loc.astype(BF16)
            ds_ctx = (p_ctx * (dp_ctx - delta)).astype(BF16)
            dq = _dot(ds_loc, kw) + _dot(ds_ctx, kc)
            dq_ref[...] = jnp.where(_lanes(0, HEAD_DIM), dq[:GRID_W], dq[GRID_W:])
            dk_ref[win, :] += _dot(ds_loc, qq, _TN)
            dk_ref[ctx, :] += _dot(ds_ctx, qq, _TN)
            dv_ref[win, :] += _dot(p_loc.astype(BF16), dd, _TN)
            dv_ref[ctx, :] += _dot(p_ctx.astype(BF16), dd, _TN)

    qmap = lambda p, b, st: (b, st, p)
    kmap = lambda p, b, st: (b, 0, p)
    return pl.pallas_call(
        body, name=name, grid=(npair, nb, nc + rows),
        in_specs=[pl.BlockSpec((None, GRID_W, LANE), qmap), pl.BlockSpec((None, t, LANE), kmap), pl.BlockSpec((None, t, LANE), kmap),
                  pl.BlockSpec((2, None, GRID_W, nwin), lambda p, b, st: (p, _na_window(st, nc, rows)[2], 0, 0)),
                  pl.BlockSpec((None, 2, GRID_W, 1), lambda p, b, st: (b, p, st, 0)), pl.BlockSpec((None, GRID_W, LANE), qmap)],
        out_specs=[pl.BlockSpec((None, GRID_W, LANE), qmap), pl.BlockSpec((None, t, LANE), kmap), pl.BlockSpec((None, t, LANE), kmap),
                   pl.BlockSpec((2, NA_ROWS, GRID_W, nwin), lambda p, b, st: (p, 0, 0, 0))],
        out_shape=[jax.ShapeDtypeStruct((nb, t, w), F32), jax.ShapeDtypeStruct((nb, t, w), F32), jax.ShapeDtypeStruct((nb, t, w), F32),
                   jax.ShapeDtypeStruct((2 * npair, NA_ROWS, GRID_W, nwin), F32)],
        compiler_params=_cp("arbitrary", "arbitrary", "arbitrary"),
    )(q, k, v, bias, lse, do)


def _na_tables():
    cols = np.arange(GRID_W)
    c0 = np.clip(cols - NA_COLS // 2, 0, GRID_W - NA_COLS)
    col_in = (cols[None, :] >= c0[:, None]) & (cols[None, :] < c0[:, None] + NA_COLS)
    dc = np.clip(cols[None, :] - cols[:, None] + NA_COLS - 1, 0, 2 * NA_COLS - 2)
    dr = np.arange(NA_ROWS)[None, :] + (NA_ROWS - 1) - np.arange(NA_ROWS)[:, None]
    return col_in, dc, dr


def _na_onehots():
    col_in, dc, dr = _na_tables()
    e1 = np.zeros((GRID_W, GRID_W, LANE), np.float32)
    qi, ki = np.nonzero(col_in)
    e1[qi, ki, dc[qi, ki]] = 1.0
    e2 = np.zeros((2 * NA_ROWS, NA_ROWS, NA_ROWS), np.float32)
    ci, ji = np.meshgrid(np.arange(NA_ROWS), np.arange(NA_ROWS), indexing='ij')
    e2[dr[ci, ji], ci, ji] = 1.0
    return jnp.asarray(e1.reshape(GRID_W * GRID_W, LANE)), jnp.asarray(e2.reshape(2 * NA_ROWS, NA_ROWS * NA_ROWS)), col_in


def na_expand_bias(rel_bias, name):
    e1, e2, col_in = _na_onehots()
    nh = rel_bias.shape[0]
    nrow = NA_ROWS * NA_ROWS
    rel = jnp.pad(rel_bias, ((0, 0), (0, 1), (0, LANE - rel_bias.shape[2])))
    rel = rel.transpose(1, 0, 2).reshape(2 * NA_ROWS, nh * LANE)
    y = mm(e2, rel, ta=True, name=name + "_rows", precise=True)
    y = y.reshape(nrow, nh, LANE).transpose(1, 0, 2).reshape(nh * nrow, LANE)
    g = mm(y, e1, tb=True, name=name + "_cols", precise=True)
    g = g.reshape(nh, NA_ROWS, NA_ROWS, GRID_W, GRID_W).transpose(0, 1, 3, 2, 4)
    g = jnp.where(col_in[None, None, :, None, :], g, NEG_BIG)
    return g.reshape(nh, NA_ROWS, GRID_W, NA_ROWS * GRID_W)


def na_reduce_bias(dexp, name):
    e1, e2, _ = _na_onehots()
    nh = dexp.shape[0]
    x = dexp.reshape(nh, NA_ROWS, GRID_W, NA_ROWS, GRID_W).transpose(0, 1, 3, 2, 4).reshape(nh * NA_ROWS * NA_ROWS, GRID_W * GRID_W)
    y = mm(x, e1, name=name + "_cols", precise=True)
    y = y.reshape(nh, NA_ROWS * NA_ROWS, LANE).transpose(1, 0, 2).reshape(NA_ROWS * NA_ROWS, nh * LANE)
    z = mm(e2, y, name=name + "_rows", precise=True)
    return z.reshape(2 * NA_ROWS, nh, LANE).transpose(1, 0, 2)[:, :2 * NA_ROWS - 1, :2 * NA_COLS - 1]


def _rot_matrix(width, d_rot):
    f = d_rot // 4
    r = np.zeros((width, width), np.float32)
    for base in range(0, width, d_rot // 2):
        for j in range(f):
            r[base + f + j, base + j] = -1.0
            r[base + j, base + f + j] = 1.0
    return r


def _rope_tables(s_len, lc, d_rot, reps):
    half = d_rot // 2
    freqs = ROPE_THETA ** (-jnp.arange(0, half, 2, dtype=F32) / half)
    tpos = jnp.arange(s_len)
    row = (tpos // GRID_W).astype(F32)[:, None] * freqs
    col = (tpos % GRID_W).astype(F32)[:, None] * freqs
    ang = jnp.concatenate([row, row, col, col], axis=-1)
    cos = jnp.concatenate([jnp.ones((lc, d_rot), F32), jnp.cos(ang)], axis=0)
    sin = jnp.concatenate([jnp.zeros((lc, d_rot), F32), jnp.sin(ang)], axis=0)
    return jnp.tile(cos, (1, reps)), jnp.tile(sin, (1, reps))


def _post_consts():
    s_b = np.kron(np.eye(GQA_HEADS, dtype=np.float32), np.full((HEAD_DIM, HEAD_DIM), 1.0 / HEAD_DIM, np.float32))
    t_b = np.tile(np.eye(HEAD_DIM, dtype=np.float32), (1, GQA_HEADS))
    r_b = _rot_matrix(GQ_W, HEAD_DIM)
    r_m = _rot_matrix(LANE, MLA_ROPE)
    rep = np.zeros((LANE, LANE), np.float32)
    for h in range(MLA_HEADS):
        rep[np.arange(MLA_ROPE), h * MLA_ROPE + np.arange(MLA_ROPE)] = 1.0
    dup = np.zeros((GK_W, 2 * GK_W), np.float32)
    for j in range(GQA_KV_HEADS):
        for e in range(2):
            dup[HEAD_DIM * j + np.arange(HEAD_DIM), 2 * HEAD_DIM * j + HEAD_DIM * e + np.arange(HEAD_DIM)] = 1.0
    return tuple(jnp.asarray(a) for a in (s_b, r_b, t_b, r_m, rep, dup))


def _heads_to_parts(w, first):
    r = w.shape[0]
    w3 = w.reshape(r, MLA_HEADS, -1)
    return jnp.concatenate([w3[:, :, :first].reshape(r, -1), w3[:, :, first:].reshape(r, -1)], axis=1)


def _parts_to_heads(w, first):
    r = w.shape[0]
    nf = MLA_HEADS * first
    return jnp.concatenate([w[:, :nf].reshape(r, MLA_HEADS, first), w[:, nf:].reshape(r, MLA_HEADS, -1)], axis=2).reshape(r, -1)


def _place():
    return lax.axis_index("x"), lax.axis_index("y"), lax.axis_index("c")


def all_gather(v, *, name, with_c):
    flips = [(dx, dy, dc) for dx in (0, 1) for dy in (0, 1) for dc in ((0, 1) if with_c else (0,))][1:]
    n = len(flips) + 1

    def body(v_ref, out_ref, send_sems, recv_sems, local_sem):
        mx, my, mc = _place()

        def slot(px, py, pc):
            return 4 * px + 2 * py + pc if with_c else 2 * px + py

        mine = pltpu.make_async_copy(v_ref, out_ref.at[slot(mx, my, mc)], local_sem)
        mine.start()
        sends = []
        for j, (dx, dy, dc) in enumerate(flips):
            peer = (mx ^ dx, my ^ dy, mc ^ dc)
            cp = pltpu.make_async_remote_copy(src_ref=v_ref, dst_ref=out_ref.at[slot(mx, my, mc)], send_sem=send_sems.at[j],
                                              recv_sem=recv_sems.at[j], device_id=peer, device_id_type=MESH)
            cp.start()
            sends.append(cp)
        for j, (dx, dy, dc) in enumerate(flips):
            peer = (mx ^ dx, my ^ dy, mc ^ dc)
            pltpu.make_async_remote_copy(src_ref=v_ref, dst_ref=out_ref.at[slot(*peer)], send_sem=send_sems.at[j],
                                         recv_sem=recv_sems.at[j], device_id=peer, device_id_type=MESH).wait_recv()
        for cp in sends:
            cp.wait_send()
        mine.wait()

    return pl.pallas_call(
        body, name=name, in_specs=[ANY], out_specs=ANY, out_shape=jax.ShapeDtypeStruct((n,) + v.shape, v.dtype),
        scratch_shapes=[pltpu.SemaphoreType.DMA((n - 1,)), pltpu.SemaphoreType.DMA((n - 1,)), pltpu.SemaphoreType.DMA(())],
    )(v)


def gather_shards(v, *, name):
    _, h, w = v.shape
    flips = [(1, 0), (0, 1), (1, 1)]

    def body(v_ref, out_ref, send_sems, recv_sems):
        mx, my, mc = _place()
        me = 2 * mx + my
        sib = (mx, my, 1 - mc)

        def copy(k, src, dst, to):
            return pltpu.make_async_remote_copy(src_ref=src, dst_ref=dst, send_sem=send_sems.at[k], recv_sem=recv_sems.at[k],
                                                device_id=to, device_id_type=MESH)

        first = [copy(j, v_ref.at[mc], out_ref.at[me, mc], (mx ^ dx, my ^ dy, mc)) for j, (dx, dy) in enumerate(flips)]
        for cp in first:
            cp.start()
        passed = []
        for j, (dx, dy) in enumerate(flips):
            theirs = out_ref.at[2 * (mx ^ dx) + (my ^ dy), mc]
            copy(j, v_ref.at[mc], theirs, (mx ^ dx, my ^ dy, mc)).wait_recv()
            fw = copy(3 + j, theirs, theirs, sib)
            fw.start()
            passed.append(fw)
        for j, (dx, dy) in enumerate(flips):
            other = out_ref.at[2 * (mx ^ dx) + (my ^ dy), 1 - mc]
            copy(3 + j, other, other, sib).wait_recv()
        for cp in first + passed:
            cp.wait_send()

    out = pl.pallas_call(
        body, name=name, in_specs=[ANY], out_specs=ANY, out_shape=jax.ShapeDtypeStruct((4, 2, h, w), v.dtype),
        scratch_shapes=[pltpu.SemaphoreType.DMA((6,)), pltpu.SemaphoreType.DMA((6,))],
    )(v)
    mx, my, _ = _place()
    return lax.dynamic_update_slice(out, v[None], (2 * mx + my, 0, 0, 0))


def pair_exchange_halves(g, *, name):
    n, _, h, w = g.shape

    def body(g_ref, out_ref, send_sems, recv_sems):
        mx, my, mc = _place()
        sib = (mx, my, 1 - mc)
        cps = [pltpu.make_async_remote_copy(src_ref=g_ref.at[s, 1 - mc], dst_ref=out_ref.at[s], send_sem=send_sems.at[s],
                                            recv_sem=recv_sems.at[s], device_id=sib, device_id_type=MESH) for s in range(n)]
        for cp in cps:
            cp.start()
        for cp in cps:
            cp.wait_recv()
        for cp in cps:
            cp.wait_send()

    return pl.pallas_call(
        body, name=name, in_specs=[ANY], out_specs=ANY, out_shape=jax.ShapeDtypeStruct((n, h, w), g.dtype),
        scratch_shapes=[pltpu.SemaphoreType.DMA((n,)), pltpu.SemaphoreType.DMA((n,))],
    )(g)


def all_to_all_xy(v, *, name):
    def body(v_ref, out_ref, send_sems, recv_sems):
        mx, my, mc = _place()
        me = 2 * mx + my
        flips = [(1, 0), (0, 1), (1, 1)]
        sends = []
        for j, (dx, dy) in enumerate(flips):
            px, py = mx ^ dx, my ^ dy
            cp = pltpu.make_async_remote_copy(src_ref=v_ref.at[2 * px + py], dst_ref=out_ref.at[me], send_sem=send_sems.at[j],
                                              recv_sem=recv_sems.at[j], device_id=(px, py, mc), device_id_type=MESH)
            cp.start()
            sends.append(cp)
        for j, (dx, dy) in enumerate(flips):
            px, py = mx ^ dx, my ^ dy
            pltpu.make_async_remote_copy(src_ref=v_ref.at[me], dst_ref=out_ref.at[2 * px + py], send_sem=send_sems.at[j],
                                         recv_sem=recv_sems.at[j], device_id=(px, py, mc), device_id_type=MESH).wait_recv()
        for cp in sends:
            cp.wait_send()

    out = pl.pallas_call(
        body, name=name, in_specs=[ANY], out_specs=ANY, out_shape=jax.ShapeDtypeStruct(v.shape, v.dtype),
        scratch_shapes=[pltpu.SemaphoreType.DMA((3,)), pltpu.SemaphoreType.DMA((3,))],
    )(v)
    mx, my, _ = _place()
    me = 2 * mx + my
    return lax.dynamic_update_slice(out, lax.dynamic_slice_in_dim(v, me, 1, axis=0), (me, 0, 0))


def pair_all_gather(v, *, name):
    def body(v_ref, out_ref, send_sem, recv_sem):
        mx, my, mc = _place()
        cp = pltpu.make_async_remote_copy(src_ref=v_ref, dst_ref=out_ref.at[mc], send_sem=send_sem, recv_sem=recv_sem,
                                          device_id=(mx, my, 1 - mc), device_id_type=MESH)
        cp.start()
        pltpu.make_async_remote_copy(src_ref=v_ref, dst_ref=out_ref.at[1 - mc], send_sem=send_sem, recv_sem=recv_sem,
                                     device_id=(mx, my, 1 - mc), device_id_type=MESH).wait_recv()
        cp.wait_send()

    out = pl.pallas_call(
        body, name=name, in_specs=[ANY], out_specs=ANY, out_shape=jax.ShapeDtypeStruct((2,) + v.shape, v.dtype),
        scratch_shapes=[pltpu.SemaphoreType.DMA(()), pltpu.SemaphoreType.DMA(())],
    )(v)
    return lax.dynamic_update_slice(out, v[None], (_place()[2], 0, 0))


def gather_ffn(wl, *, name):
    nl, nblk, cs, d = wl.shape
    assert nl == 2
    flips = [(1, 0), (0, 1), (1, 1)]

    def body(v_ref, out_ref, send_sems, recv_sems):
        mx, my, mc = _place()
        me = 2 * mx + my
        sib = (mx, my, 1 - mc)

        def copy(k, src, dst, to):
            return pltpu.make_async_remote_copy(src_ref=src, dst_ref=dst, send_sem=send_sems.at[k], recv_sem=recv_sems.at[k],
                                                device_id=to, device_id_type=MESH)

        first = [copy(j, v_ref.at[mc], out_ref.at[mc, me], (mx ^ dx, my ^ dy, mc)) for j, (dx, dy) in enumerate(flips)]
        for cp in first:
            cp.start()
        passed = []
        for j, (dx, dy) in enumerate(flips):
            theirs = out_ref.at[mc, 2 * (mx ^ dx) + (my ^ dy)]
            copy(j, v_ref.at[mc], theirs, (mx ^ dx, my ^ dy, mc)).wait_recv()
            fw = copy(3 + j, theirs, theirs, sib)
            fw.start()
            passed.append(fw)
        for j, (dx, dy) in enumerate(flips):
            other = out_ref.at[1 - mc, 2 * (mx ^ dx) + (my ^ dy)]
            copy(3 + j, other, other, sib).wait_recv()
        for cp in first + passed:
            cp.wait_send()

    out = pl.pallas_call(
        body, name=name, in_specs=[ANY], out_specs=ANY, out_shape=jax.ShapeDtypeStruct((nl, 4, nblk, cs, d), wl.dtype),
        scratch_shapes=[pltpu.SemaphoreType.DMA((6,)), pltpu.SemaphoreType.DMA((6,))],
    )(wl)
    mx, my, _ = _place()
    return lax.dynamic_update_slice(out, wl[:, None], (0, 2 * mx + my, 0, 0, 0))


def reduce_ffn(g0, g1, *, name):
    nt = len(g0)
    nsh, cs, d = g0[0].shape
    flips = [(1, 0), (0, 1), (1, 1)]
    mx, my, mc = _place()
    me = 2 * mx + my
    c_idx = jnp.reshape(mc, (1,)).astype(jnp.int32)

    def pair_body(*refs):
        ins0, ins1, outs = refs[:nt], refs[nt:2 * nt], refs[2 * nt:3 * nt]
        send_sems, recv_sems = refs[3 * nt:]
        kx, ky, kc = _place()
        sib = (kx, ky, 1 - kc)
        for c in range(2):
            @pl.when(kc == c)
            def _(c=c):
                mine_out = (ins1, ins0)[c]
                cps = [pltpu.make_async_remote_copy(src_ref=mine_out[t], dst_ref=outs[t], send_sem=send_sems.at[t],
                                                    recv_sem=recv_sems.at[t], device_id=sib, device_id_type=MESH) for t in range(nt)]
                for cp in cps:
                    cp.start()
                for cp in cps:
                    cp.wait_recv()
                for cp in cps:
                    cp.wait_send()

    from_pair = pl.pallas_call(
        pair_body, name=name + "_pair", in_specs=[ANY] * (2 * nt), out_specs=[ANY] * nt,
        out_shape=[jax.ShapeDtypeStruct((nsh, cs, d), F32)] * nt,
        scratch_shapes=[pltpu.SemaphoreType.DMA((nt,)), pltpu.SemaphoreType.DMA((nt,))],
    )(*g0, *g1)

    tr = _row_tile(cs, 64)

    def add_body(c_ref, *refs):
        for t in range(nt):
            mine = jnp.where(c_ref[0] == 0, refs[t][...], refs[nt + t][...])
            refs[3 * nt + t][...] = (mine + refs[2 * nt + t][...]).astype(BF16)

    spec = pl.BlockSpec((None, tr, d), lambda s, i, c_ref: (s, i, 0))
    chip_sum = pl.pallas_call(
        add_body, name=name + "_pair_add",
        grid_spec=pltpu.PrefetchScalarGridSpec(num_scalar_prefetch=1, grid=(nsh, cs // tr), in_specs=[spec] * (3 * nt),
                                               out_specs=[spec] * nt),
        out_shape=[jax.ShapeDtypeStruct((nsh, cs, d), BF16)] * nt, compiler_params=_cp("parallel", "parallel"),
    )(c_idx, *g0, *g1, *from_pair)

    def xy_body(*refs):
        ins, outs = refs[:nt], refs[nt:2 * nt]
        send_sems, recv_sems = refs[2 * nt:]
        kx, ky, kc = _place()
        k_me = 2 * kx + ky
        sends = []
        for j, (dx, dy) in enumerate(flips):
            px, py = kx ^ dx, ky ^ dy
            for t in range(nt):
                cp = pltpu.make_async_remote_copy(src_ref=ins[t].at[2 * px + py], dst_ref=outs[t].at[k_me],
                                                  send_sem=send_sems.at[j * nt + t], recv_sem=recv_sems.at[j * nt + t],
                                                  device_id=(px, py, kc), device_id_type=MESH)
                cp.start()
                sends.append(cp)
        for j, (dx, dy) in enumerate(flips):
            px, py = kx ^ dx, ky ^ dy
            for t in range(nt):
                pltpu.make_async_remote_copy(src_ref=ins[t].at[k_me], dst_ref=outs[t].at[2 * px + py],
                                             send_sem=send_sems.at[j * nt + t], recv_sem=recv_sems.at[j * nt + t],
                                             device_id=(px, py, kc), device_id_type=MESH).wait_recv()
        for cp in sends:
            cp.wait_send()

    from_xy = pl.pallas_call(
        xy_body, name=name + "_xy", in_specs=[ANY] * nt, out_specs=[ANY] * nt,
        out_shape=[jax.ShapeDtypeStruct((nsh, cs, d), BF16)] * nt,
        scratch_shapes=[pltpu.SemaphoreType.DMA((3 * nt,)), pltpu.SemaphoreType.DMA((3 * nt,))],
    )(*chip_sum)
    from_xy = [lax.dynamic_update_slice(o, lax.dynamic_slice_in_dim(v, me, 1, axis=0), (me, 0, 0)) for o, v in zip(from_xy, chip_sum)]

    def sum_body(*refs):
        for t in range(nt):
            acc = refs[t][0].astype(F32)
            for s in range(1, nsh):
                acc = acc + refs[t][s].astype(F32)
            refs[nt + t][...] = acc

    reduced = pl.pallas_call(
        sum_body, name=name + "_xy_add", grid=(cs // tr,), in_specs=[pl.BlockSpec((nsh, tr, d), lambda i: (0, i, 0))] * nt,
        out_specs=[pl.BlockSpec((tr, d), lambda i: (i, 0))] * nt, out_shape=[jax.ShapeDtypeStruct((cs, d), F32)] * nt,
        compiler_params=_cp("parallel"),
    )(*from_xy)

    def share_body(*refs):
        ins, outs = refs[:nt], refs[nt:2 * nt]
        send_sems, recv_sems = refs[2 * nt:]
        kx, ky, kc = _place()
        sib = (kx, ky, 1 - kc)
        cps = [pltpu.make_async_remote_copy(src_ref=ins[t], dst_ref=outs[t].at[kc], send_sem=send_sems.at[t],
                                            recv_sem=recv_sems.at[t], device_id=sib, device_id_type=MESH) for t in range(nt)]
        for cp in cps:
            cp.start()
        for t in range(nt):
            pltpu.make_async_remote_copy(src_ref=ins[t], dst_ref=outs[t].at[1 - kc], send_sem=send_sems.at[t],
                                         recv_sem=recv_sems.at[t], device_id=sib, device_id_type=MESH).wait_recv()
        for cp in cps:
            cp.wait_send()

    both = pl.pallas_call(
        share_body, name=name + "_share", in_specs=[ANY] * nt, out_specs=[ANY] * nt,
        out_shape=[jax.ShapeDtypeStruct((2, cs, d), F32)] * nt,
        scratch_shapes=[pltpu.SemaphoreType.DMA((nt,)), pltpu.SemaphoreType.DMA((nt,))],
    )(*reduced)
    return [lax.dynamic_update_slice(o, v[None], (mc, 0, 0)) for o, v in zip(both, reduced)]


def add_kept_half(g, r, c_idx, *, name, out_dtype):
    n, _, h, w = g.shape
    th = _row_tile(h)

    def body(c_ref, g_ref, r_ref, o_ref):
        o_ref[...] = (g_ref[...] + r_ref[...]).astype(o_ref.dtype)

    return pl.pallas_call(
        body, name=name,
        grid_spec=pltpu.PrefetchScalarGridSpec(
            num_scalar_prefetch=1, grid=(n, h // th),
            in_specs=[pl.BlockSpec((None, None, th, w), lambda s, i, c_ref: (s, c_ref[0], i, 0)),
                      pl.BlockSpec((None, th, w), lambda s, i, c_ref: (s, i, 0))],
            out_specs=pl.BlockSpec((None, th, w), lambda s, i, c_ref: (s, i, 0))),
        out_shape=jax.ShapeDtypeStruct((n, h, w), out_dtype), compiler_params=_cp("parallel", "parallel"),
    )(c_idx, g, r)


def sum_slots(v, *, name):
    n, rows, w = v.shape
    tr = _row_tile(rows, 256)

    def body(v_ref, o_ref):
        acc = v_ref[0].astype(F32)
        for s in range(1, n):
            acc = acc + v_ref[s].astype(F32)
        o_ref[...] = acc

    return pl.pallas_call(body, name=name, grid=(rows // tr,), in_specs=[pl.BlockSpec((n, tr, w), lambda i: (0, i, 0))],
                          out_specs=pl.BlockSpec((tr, w), lambda i: (i, 0)), out_shape=jax.ShapeDtypeStruct((rows, w), F32),
                          compiler_params=_cp("parallel"))(v)


def ada_fwd(c_rows, w_ada, b_shard, *, name):
    nl, d, ncol = w_ada.shape
    rows = c_rows.shape[0]
    tn = _tile(ncol, (768, 512, 256, 128))

    def body(c_ref, w_ref, b_ref, o_ref):
        o_ref[...] = jnp.dot(jax.nn.silu(c_ref[...]), w_ref[...], precision=HI, preferred_element_type=F32) + b_ref[...]

    return pl.pallas_call(
        body, name=name, grid=(nl, ncol // tn),
        in_specs=[pl.BlockSpec((rows, d), lambda l, j: (0, 0)), pl.BlockSpec((None, d, tn), lambda l, j: (l, 0, j)),
                  pl.BlockSpec((None, 1, tn), lambda l, j: (l, 0, j))],
        out_specs=pl.BlockSpec((None, rows, tn), lambda l, j: (l, 0, j)),
        out_shape=jax.ShapeDtypeStruct((nl, rows, ncol), F32), compiler_params=_cp("parallel", "parallel"),
    )(c_rows, w_ada, b_shard)


def ada_bwd(c_rows, w_ada, dm_shard, dm_full, n_ex, *, name):
    nl, d, ncol = w_ada.shape
    rows = c_rows.shape[0]
    tn = _tile(ncol, (768, 512, 256, 128))
    nj = ncol // tn

    def body(c_ref, w_ref, dm_ref, dmf_ref, gw_ref, gb_ref, dc_ref, dact_ref):
        l, j = pl.program_id(0), pl.program_id(1)
        act, act_vjp = jax.vjp(jax.nn.silu, c_ref[...])
        gw_ref[...] = lax.dot_general(act, dm_ref[...], _TN, precision=HI, preferred_element_type=F32)
        gb_ref[...] = jnp.sum(dmf_ref[...], axis=0, keepdims=True)
        part = lax.dot_general(dm_ref[...], w_ref[...], _NT, precision=HI, preferred_element_type=F32)

        @pl.when((l == 0) & (j == 0))
        def _():
            dact_ref[...] = part

        @pl.when((l > 0) | (j > 0))
        def _():
            dact_ref[...] += part

        @pl.when((l == nl - 1) & (j == nj - 1))
        def _():
            dc, = act_vjp(dact_ref[...])
            dc_ref[...] = jnp.sum(dc[n_ex:, :], axis=0, keepdims=True)

    return pl.pallas_call(
        body, name=name, grid=(nl, nj),
        in_specs=[pl.BlockSpec((rows, d), lambda l, j: (0, 0)), pl.BlockSpec((None, d, tn), lambda l, j: (l, 0, j)),
                  pl.BlockSpec((None, rows, tn), lambda l, j: (l, 0, j)),
                  pl.BlockSpec((None, rows, dm_full.shape[-1]), lambda l, j: (l, 0, 0))],
        out_specs=[pl.BlockSpec((None, d, tn), lambda l, j: (l, 0, j)),
                   pl.BlockSpec((None, 1, dm_full.shape[-1]), lambda l, j: (l, 0, 0)),
                   pl.BlockSpec((1, d), lambda l, j: (0, 0))],
        out_shape=[jax.ShapeDtypeStruct((nl, d, ncol), F32), jax.ShapeDtypeStruct((nl, 1, dm_full.shape[-1]), F32),
                   jax.ShapeDtypeStruct((1, d), F32)],
        scratch_shapes=[pltpu.VMEM((rows, d), F32)], compiler_params=_cp("arbitrary", "arbitrary"),
    )(c_rows, w_ada, dm_shard, dm_full)


def adamw(w, g, m, v, *, name):
    shape = w.shape
    cols = shape[-1]
    rows = int(np.prod(shape[:-1])) if len(shape) > 1 else 1
    tr = _row_tile(rows, 256)

    def body(w_ref, g_ref, m_ref, v_ref, d_ref, nm_ref, nv_ref):
        gg = g_ref[...]
        nm = ADAM_B1 * m_ref[...] + (1.0 - ADAM_B1) * gg
        nv = ADAM_B2 * v_ref[...] + (1.0 - ADAM_B2) * jnp.square(gg)
        m_hat = nm / (1.0 - ADAM_B1 ** ADAM_STEP)
        v_hat = nv / (1.0 - ADAM_B2 ** ADAM_STEP)
        d_ref[...] = -ADAM_LR * (m_hat / (jnp.sqrt(v_hat) + ADAM_EPS) + ADAM_WD * w_ref[...])
        nm_ref[...] = nm
        nv_ref[...] = nv

    spec = pl.BlockSpec((tr, cols), lambda i: (i, 0))
    out = pl.pallas_call(body, name=name, grid=(rows // tr,), in_specs=[spec] * 4, out_specs=[spec] * 3,
                         out_shape=[jax.ShapeDtypeStruct((rows, cols), F32)] * 3, compiler_params=_cp("parallel"),
                         )(*[a.reshape(rows, cols) for a in (w, g, m, v)])
    return tuple(o.reshape(shape) for o in out)


def local_step(h0, target, mods, lw, wf, small, *, lc):
    nb, t, d = h0.shape
    nt, nct = t // TM, lc // TM
    s_len = t - lc
    nl = len(lw)
    nsh = wf.shape[1]
    consts = _post_consts()
    cos_b, sin_b = _rope_tables(s_len, lc, HEAD_DIM, GQA_HEADS)
    cos_m, sin_m = _rope_tables(s_len, lc, MLA_ROPE, MLA_HEADS)
    rc = functools.partial(rowcall, nb=nb, nt=nt, nct=nct)
    flat = lambda a: a.reshape(nb * t, a.shape[-1])
    unflat = lambda a: a.reshape(nb, t, a.shape[-1])
    vec = lambda a: a.reshape(1, -1)

    def norm_first(h, g, shift, scale, tag):
        n, = rc(tag + "_norm", lambda _, *a: (f_normmod(*a),), [(h, 'tok'), (vec(g), 'full'), (shift, 'mod'), (scale, 'mod')],
                [('tok', d, BF16)])
        return n

    def res_norm(h, y, gate, coef, g, shift, scale, tag):
        def fn(_, hh, yy, gt, gn, sh, sc):
            h2 = hh + coef * gt * yy
            return h2, f_normmod(h2, gn, sh, sc)

        return rc(tag + "_res_norm", fn, [(h, 'tok'), (y, 'tok'), (gate, 'mod'), (vec(g), 'full'), (shift, 'mod'), (scale, 'mod')],
                  [('tok', d, F32), ('tok', d, BF16)])

    def res_last(h, y, gate, coef, tag):
        h2, = rc(tag + "_res", lambda _, hh, yy, gt: (hh + coef * gt * yy,), [(h, 'tok'), (y, 'tok'), (gate, 'mod')], [('tok', d, F32)])
        return h2

    def res_bwd_last(dh2, y, gate, coef, tag):
        return rc(tag + "_res_bwd", lambda _, dd, yy, gt: (coef * gt * dd, jnp.sum(coef * yy * dd, axis=0, keepdims=True)),
                  [(dh2, 'tok'), (y, 'tok'), (gate, 'mod')], [('tok', d, BF16), ('mod', d)])

    def norm_bwd_first(h, g, shift, scale, dn, dres, tag):
        def fn(_, hh, gn, sh, sc, dnn, dr):
            dh, dg, dsh, dsc = jax.vjp(f_normmod, hh, gn, sh, sc)[1](dnn)
            return dh + dr, dg, dsh, dsc

        return rc(tag + "_norm_bwd", fn, [(h, 'tok'), (vec(g), 'full'), (shift, 'mod'), (scale, 'mod'), (dn, 'tok'), (dres, 'tok')],
                  [('tok', d, F32), ('full', (1, d)), ('mod', d), ('mod', d)])

    def norm_bwd_res_bwd(h, g, shift, scale, dn, dres, y_prev, gate_prev, coef_prev, tag):
        def fn(_, hh, gn, sh, sc, dnn, dr, yy, gt):
            dh, dg, dsh, dsc = jax.vjp(f_normmod, hh, gn, sh, sc)[1](dnn)
            dh = dh + dr
            return dh, dg, dsh, dsc, coef_prev * gt * dh, jnp.sum(coef_prev * yy * dh, axis=0, keepdims=True)

        return rc(tag + "_norm_bwd", fn,
                  [(h, 'tok'), (vec(g), 'full'), (shift, 'mod'), (scale, 'mod'), (dn, 'tok'), (dres, 'tok'), (y_prev, 'tok'), (gate_prev, 'mod')],
                  [('tok', d, F32), ('full', (1, d)), ('mod', d), ('mod', d), ('tok', d, BF16), ('mod', d)])

    def ffn_fwd(n, l, base, tag):
        gg, uu, act = ffn_up(flat(n), wf, l, base, name=tag + "_up")
        return unflat(ffn_down(act, wf, l, base, name=tag + "_down")), (n, gg, uu, act)

    def ffn_bwd(dy, saved, l, base, tag):
        n, gg, uu, act = saved
        dw_d = ffn_dw(act, flat(dy), nsh, name=tag + "_down_dw")
        dgg, duu = ffn_down_bwd(flat(dy), gg, uu, wf, l, base, name=tag + "_down_dx")
        dw_g = ffn_dw(dgg, flat(n), nsh, name=tag + "_gate_dw")
        dw_u = ffn_dw(duu, flat(n), nsh, name=tag + "_up_dw")
        return unflat(ffn_up_bwd(dgg, duu, wf, l, base, name=tag + "_up_dx")), [dw_g, dw_u, dw_d]

    def post_ins(p, sm, w):
        return [(p, ('tokc', MAIN_PAD, 0)), (cos_b, 'pos'), (sin_b, 'pos'), (cos_m, 'pos'), (sin_m, 'pos'),
                (vec(sm['gqa_q_norm']), 'full'), (vec(sm['gqa_k_norm']), 'full'), (vec(sm['mla_q_norm']), 'full'),
                (vec(sm['mla_kv_norm']), 'full'), (w['w_uq'], 'full'), (w['w_ukv'], 'full')] + [(c, 'full') for c in consts]

    def mix_fwd(n, sm, w, tag):
        p = unflat(mm_resident(flat(n), w['w_in'], name=tag + "_in"))
        parts = rc(tag + "_post", lambda _, *a: f_post(*a), post_ins(p, sm, w), [('tok', wd, BF16) for wd in POST_WIDTHS])
        aq, ak, av, bq, bk, bv, mqn, mqr, mkn, mkr, mv = parts
        bias = na_expand_bias(sm['na_rel_bias'], tag + "_bias")
        o_a, lse_a = na_fwd(aq, ak, av, bias, lc=lc, name=tag + "_na")
        o_b, lse_b = gqa_fwd(bq, bk, bv, lc=lc, name=tag + "_gqa")
        o_m, lse_m = mla_fwd(mqn, mqr, mkn, mkr, mv, lc=lc, name=tag + "_mla")
        fo = [o_a, o_b, o_m]
        ys = [unflat(mm_resident(flat(o), w[k], name=tag + "_br" + k[-1])) for o, k in zip(fo, ('w_a', 'w_b', 'w_c'))]
        gcols = [(p, ('tokc', d, MAIN_PAD // d + j)) for j in range(3)]
        y, = rc(tag + "_merge", lambda _, *a: (f_merge(*a),), gcols + [(v, 'tok') for v in ys], [('tok', d, BF16)])
        z = unflat(mm_resident(flat(y), w['w_o'], name=tag + "_out"))
        saved = (n, p, (aq, ak, av, lse_a, bias), (bq, bk, bv, lse_b), (mqn, mqr, mkn, mkr, mv, lse_m), fo, ys, y)
        return z, saved

    def mix_bwd(dz, saved, sm, w, tag):
        n, p, (aq, ak, av, lse_a, bias), (bq, bk, bv, lse_b), (mqn, mqr, mkn, mkr, mv, lse_m), fo, ys, y = saved
        dw_o = mm(flat(y), flat(dz), ta=True, name=tag + "_out_dw")
        dy = unflat(mm_resident(flat(dz), w['w_o'], tb=True, name=tag + "_out_dx"))
        gcols = [(p, ('tokc', d, MAIN_PAD // d + j)) for j in range(3)]

        def merge_bwd(_, ga, gb, gm, ya, yb, ym, dyy):
            dga, dgb, dgm, dya, dyb, dym = jax.vjp(f_merge, ga, gb, gm, ya, yb, ym)[1](dyy)
            return dya, dyb, dym, jnp.concatenate([dga, dgb, dgm], axis=-1)

        dya, dyb, dym, dgl = rc(tag + "_merge_bwd", merge_bwd, gcols + [(v, 'tok') for v in ys] + [(dy, 'tok')],
                                [('tok', d, BF16)] * 3 + [('tok', 3 * d, BF16)])
        dws, dos = {}, []
        for o, dyk, k in zip(fo, (dya, dyb, dym), ('w_a', 'w_b', 'w_c')):
            dws[k] = mm(flat(o), flat(dyk), ta=True, name=tag + "_br" + k[-1] + "_dw")
            dos.append(unflat(mm_resident(flat(dyk), w[k], tb=True, out_dtype=BF16, name=tag + "_br" + k[-1] + "_dx")))
        do_a, do_b, do_m = dos
        daq, dak, dav, dbias = na_bwd(aq, ak, av, bias, lse_a, do_a, lc=lc, name=tag + "_na_bwd")
        dbq, dbk, dbv = gqa_bwd(bq, bk, bv, lse_b, do_b, lc=lc, name=tag + "_gqa_bwd")
        dmqn, dmqr2, dmkn, dmkr, dmv = mla_bwd(mqn, mqr, mkn, mkr, mv, lse_m, do_m, lc=lc, name=tag + "_mla_bwd")
        d_rel = na_reduce_bias(dbias, tag + "_relb")
        cots = [daq, dak, dav, dbq, dbk, dbv, dmqn, dmqr2, dmkn, dmkr, dmv]
        ins = post_ins(p, sm, w)
        n_in = len(ins)

        def post_bwd(_, *a):
            prim, cot, dgl_v = a[:11], list(a[n_in:n_in + 11]), a[-1]
            cot[7] = cot[7][:, :LANE] + cot[7][:, LANE:]
            for j in POST_QK:
                cot[j] = cot[j] * LN2
            outs = jax.vjp(lambda pp, qn, kn, mqn, mkvn, wuq, wukv: f_post(pp, *prim[1:5], qn, kn, mqn, mkvn, wuq, wukv, *a[11:n_in]),
                           prim[0], *prim[5:11])[1](tuple(cot))
            return (jnp.concatenate([outs[0].astype(BF16), dgl_v], axis=-1),) + tuple(outs[1:])

        res = rc(tag + "_post_bwd", post_bwd, ins + [(cv, 'tok') for cv in cots] + [(dgl, 'tok')],
                 [('tok', MAIN_PAD + 3 * d, BF16), ('full', (1, HEAD_DIM)), ('full', (1, HEAD_DIM)), ('full', (1, MLA_Q_RANK)),
                  ('full', (1, MLA_KV_RANK)), ('full', w['w_uq'].shape), ('full', w['w_ukv'].shape)])
        dp, dqn, dkn, dmqn, dmkvn, dw_uq, dw_ukv = res
        dw_in = ffn_dw(flat(dp), flat(n), 4, name=tag + "_in_dw").reshape(-1, d)
        dn = unflat(mm_resident(flat(dp), w['w_in'], tb=True, name=tag + "_in_dx"))
        dsm = {'na_rel_bias': d_rel, 'gqa_q_norm': dqn.reshape(-1), 'gqa_k_norm': dkn.reshape(-1),
               'mla_q_norm': dmqn.reshape(-1), 'mla_kv_norm': dmkvn.reshape(-1)}
        dwl = {'w_in': dw_in, 'w_uq': dw_uq, 'w_ukv': dw_ukv, 'w_o': dw_o, **dws}
        return dn, dsm, dwl

    subs = [(l, kind, gain, coef) for l in range(nl)
            for kind, gain, coef in (('ffn1', 'ffn1_norm', 0.5), ('mix', 'mix_norm', 1.0), ('ffn2', 'ffn2_norm', 0.5))]
    ns = len(subs)
    sms = [{k: small[k][l] for k in SMALL_LAYER} for l in range(nl)]

    def params(k):
        l, _, gain, _ = subs[k]
        j = 3 * (k % 3)
        return small[gain][l], mods[l][j], mods[l][j + 1], mods[l][j + 2]

    def tag_of(k):
        return f"l{subs[k][0]}_{subs[k][1]}"

    h = h0
    g0, sh0, sc0, _ = params(0)
    n = norm_first(h, g0, sh0, sc0, tag_of(0))
    h_in, core_out, saved = [None] * ns, [None] * ns, [None] * ns
    for k, (l, kind, _, coef) in enumerate(subs):
        h_in[k] = h
        if kind == 'mix':
            core_out[k], saved[k] = mix_fwd(n, sms[l], lw[l], tag_of(k))
        else:
            core_out[k], saved[k] = ffn_fwd(n, l, 0 if kind == 'ffn1' else 3, tag_of(k))
        gate = params(k)[3]
        if k + 1 < ns:
            gn, shn, scn, _ = params(k + 1)
            h, n = res_norm(h, core_out[k], gate, coef, gn, shn, scn, tag_of(k))
        else:
            h = res_last(h, core_out[k], gate, coef, tag_of(k))

    def final(is_ctx, hh, gg, tgt):
        def loss_fn(hv, gv):
            return 0.5 * jnp.sum(jnp.mean(jnp.square(_rms(hv, gv) - tgt), axis=-1))

        keep = jnp.where(is_ctx, 0.0, 1.0)
        loss, (dh, dg) = jax.value_and_grad(loss_fn, argnums=(0, 1))(hh, gg)
        return dh * keep, jnp.full((1, LANE), loss * keep, F32), dg * keep

    dh, loss, dg_final = rc("final_loss", final, [(h, 'tok'), (vec(small['final_norm']), 'full'), (target, 'lat')],
                            [('tok', d, F32), ('full', (1, LANE)), ('full', (1, d))])

    dsmall = {k: [None] * nl for k in SMALL_LAYER}
    dmods, dlw, dwf = [[None] * N_MOD for _ in range(nl)], [None] * nl, [[None] * 6 for _ in range(nl)]
    l_last, _, _, coef_last = subs[-1]
    dcore, dmods[l_last][8] = res_bwd_last(dh, core_out[-1], params(ns - 1)[3], coef_last, tag_of(ns - 1))
    for k in reversed(range(ns)):
        l, kind, gain, _ = subs[k]
        j = 3 * (k % 3)
        if kind == 'mix':
            dn, dsm, dlw[l] = mix_bwd(dcore, saved[k], sms[l], lw[l], tag_of(k))
            for name, val in dsm.items():
                dsmall[name][l] = val
        else:
            base = 0 if kind == 'ffn1' else 3
            dn, dwf[l][base:base + 3] = ffn_bwd(dcore, saved[k], l, base, tag_of(k))
        g, shift, scale, _ = params(k)
        if k > 0:
            lp, _, _, coef_prev = subs[k - 1]
            dh, dg, dmods[l][j], dmods[l][j + 1], dcore, dmods[lp][3 * ((k - 1) % 3) + 2] = norm_bwd_res_bwd(
                h_in[k], g, shift, scale, dn, dh, core_out[k - 1], params(k - 1)[3], coef_prev, tag_of(k))
        else:
            dh, dg, dmods[l][j], dmods[l][j + 1] = norm_bwd_first(h_in[k], g, shift, scale, dn, dh, tag_of(k))
        dsmall[gain][l] = dg.reshape(d)
    dsmall = {k: jnp.stack(v) for k, v in dsmall.items()}
    dsmall['final_norm'] = dg_final.reshape(d)
    return loss, dh, dmods, dlw, dwf, dsmall


def _pack(parts, pad_rows):
    flat, where, off = [], [], 0
    for a in parts:
        n = _ceil_to(a.size, PACK_W)
        flat.append(jnp.pad(a.reshape(-1), (0, n - a.size)))
        where.append((off, n // PACK_W))
        off += n // PACK_W
    total = _ceil_to(off, pad_rows)
    if total > off:
        flat.append(jnp.zeros(((total - off) * PACK_W,), flat[0].dtype))
    return jnp.concatenate(flat).reshape(total, PACK_W), where


def _unpack(buf, where, shape):
    off, rows = where
    return buf[off:off + rows].reshape(-1)[:int(np.prod(shape))].reshape(shape)


def layer_weights(full, l):
    wi = full['w_in'][l]
    d = wi.shape[0]
    return {
        'w_in': jnp.concatenate([wi[:, :MAIN_W], jnp.zeros((d, MAIN_PAD - MAIN_W), wi.dtype), wi[:, MAIN_W:]], axis=1),
        'w_uq': _heads_to_parts(full['mla_w_uq'][l], MLA_NOPE).astype(F32),
        'w_ukv': _heads_to_parts(full['mla_w_ukv'][l], MLA_NOPE).astype(F32),
        'w_a': full['w_branch_a'][l], 'w_b': full['w_branch_b'][l], 'w_c': full['w_branch_c'][l], 'w_o': full['w_out'][l]}


def layer_grads_by_name(dlw):
    per_name = {k: [] for k, _ in BIG}
    for g in dlw:
        per_name['w_in'].append(jnp.concatenate([g['w_in'][:MAIN_W], g['w_in'][MAIN_PAD:]], axis=0))
        per_name['mla_w_uq'].append(_parts_to_heads(g['w_uq'], MLA_NOPE))
        per_name['mla_w_ukv'].append(_parts_to_heads(g['w_ukv'], MLA_NOPE))
        per_name['w_branch_a'].append(g['w_a'])
        per_name['w_branch_b'].append(g['w_b'])
        per_name['w_branch_c'].append(g['w_c'])
        per_name['w_out'].append(g['w_o'])
    return per_name


def kernel(x, c, ctx, c_ctx, w_ada, b_ada, ffn1_norm, ffn1_w_gate, ffn1_w_up, ffn1_w_down, mix_norm, w_in, na_rel_bias, gqa_q_norm, gqa_k_norm, mla_q_norm, mla_kv_norm, mla_w_uq, mla_w_ukv, w_branch_a, w_branch_b, w_branch_c, w_out, ffn2_norm, ffn2_w_gate, ffn2_w_up, ffn2_w_down, final_norm, loss_target, m_c_ctx, m_w_ada, m_b_ada, m_ffn1_norm, m_ffn1_w_gate, m_ffn1_w_up, m_ffn1_w_down, m_mix_norm, m_w_in, m_na_rel_bias, m_gqa_q_norm, m_gqa_k_norm, m_mla_q_norm, m_mla_kv_norm, m_mla_w_uq, m_mla_w_ukv, m_w_branch_a, m_w_branch_b, m_w_branch_c, m_w_out, m_ffn2_norm, m_ffn2_w_gate, m_ffn2_w_up, m_ffn2_w_down, m_final_norm, v_c_ctx, v_w_ada, v_b_ada, v_ffn1_norm, v_ffn1_w_gate, v_ffn1_w_up, v_ffn1_w_down, v_mix_norm, v_w_in, v_na_rel_bias, v_gqa_q_norm, v_gqa_k_norm, v_mla_q_norm, v_mla_kv_norm, v_mla_w_uq, v_mla_w_ukv, v_w_branch_a, v_w_branch_b, v_w_branch_c, v_w_out, v_ffn2_norm, v_ffn2_w_gate, v_ffn2_w_up, v_ffn2_w_down, v_final_norm):
    args = locals()
    wts = {k: args[k] for k in WEIGHTS}
    mom = {k: args['m_' + k] for k in WEIGHTS}
    var = {k: args['v_' + k] for k in WEIGHTS}
    nb, s_len, d = x.shape
    lc = ctx.shape[1]
    nl = w_ada.shape[0]
    nsh, ndev = 4, 8
    mx, my, mc = _place()
    sidx = 2 * mx + my
    didx = 4 * mx + 2 * my + mc
    assert d % LANE == 0 and MAIN_PAD % d == 0 and lc % TQ == 0 and s_len % TQ == 0 and s_len // GRID_W >= NA_ROWS

    wpack, wwhere = _pack([wts[k].astype(BF16) for k, _ in BIG], 32)
    wall = gather_shards(wpack.reshape(2, -1, PACK_W), name="gather_weights").reshape(nsh, -1, PACK_W)
    full = {}
    for (k, ax), wh in zip(BIG, wwhere):
        shp = wts[k].shape
        parts = jnp.stack([_unpack(wall[s], wh, shp) for s in range(nsh)])
        if ax == 1:
            full[k] = parts.transpose(1, 2, 0, 3).reshape(nl, shp[1], nsh * shp[2])
        else:
            full[k] = parts.transpose(1, 0, 2, 3).reshape(nl, nsh * shp[1], shp[2])
    lw = [layer_weights(full, l) for l in range(nl)]
    wl = jnp.stack([(wts[k].transpose(0, 2, 1) if tr else wts[k]).astype(BF16) for k, tr in zip(FFN_NAMES, FFN_TRANSPOSED)], axis=1)
    wf = gather_ffn(wl, name="gather_ffn")

    n_ex = ndev * nb
    ncol = w_ada.shape[-1]
    c_all = all_gather(c, name="gather_cond", with_c=True).reshape(n_ex, d)
    c_rows = jnp.concatenate([c_all, jnp.broadcast_to(c_ctx[None], (n_ex, d))], axis=0)
    b_shard = lax.dynamic_slice_in_dim(b_ada, sidx * ncol, ncol, axis=1)[:, None, :]
    mod_sh = ada_fwd(c_rows, w_ada, b_shard, name="ada_fwd")
    mod_all = all_gather(mod_sh, name="gather_mod", with_c=False)
    mod_all = mod_all.transpose(1, 2, 0, 3).reshape(nl, 2 * n_ex, nsh * ncol)
    mod_x = lax.dynamic_slice_in_dim(mod_all, didx * nb, nb, axis=1)
    mod_c = jnp.broadcast_to(mod_all[:, n_ex:n_ex + 1], mod_x.shape)
    mods = [[jnp.stack([mod_c[l, :, j * d:(j + 1) * d], mod_x[l, :, j * d:(j + 1) * d]], axis=1)[:, :, None, :]
             for j in range(N_MOD)] for l in range(nl)]

    small = {k: wts[k] for k in SMALL_LAYER + ['final_norm']}
    h0 = jnp.concatenate([ctx, x], axis=1)
    loss_part, dh0, dmods, dlw, dwf, dsmall = local_step(h0, loss_target, mods, lw, wf, small, lc=lc)
    grad_x = dh0[:, lc:]

    dmod_mine = jnp.stack([jnp.concatenate([m[:, :, 0, :] for m in dmods[l]], axis=-1) for l in range(nl)])
    small_names = SMALL_LAYER + ['final_norm']
    spack, swhere = _pack([loss_part] + [dsmall[k] for k in small_names] + [dmod_mine], 8)
    sall = all_gather(spack, name="gather_small", with_c=True)
    ssum = sum_slots(sall, name="sum_small")
    loss = _unpack(ssum, swhere[0], (1, LANE))[0, 0]
    grads = {k: _unpack(ssum, wh, wts[k].shape) for k, wh in zip(small_names, swhere[1:])}
    off, rows = swhere[-1]
    dm_all = sall[:, off:off + rows].reshape(ndev, -1)[:, :dmod_mine.size].reshape((ndev,) + dmod_mine.shape)
    dm_all = dm_all.transpose(1, 3, 0, 2, 4).reshape(nl, 2, n_ex, N_MOD * d)
    dm_rows = jnp.concatenate([dm_all[:, 1], dm_all[:, 0]], axis=1)
    dm_shard = lax.dynamic_slice_in_dim(dm_rows, sidx * ncol, ncol, axis=2)
    grads['w_ada'], gb, dc_part = ada_bwd(c_rows, w_ada, dm_shard, dm_rows, n_ex, name="ada_bwd")
    grads['b_ada'] = gb.reshape(b_ada.shape)
    dc_all = all_gather(jnp.pad(dc_part, ((0, 7), (0, 0))), name="gather_dcond", with_c=False)
    grads['c_ctx'] = sum_slots(dc_all, name="sum_dcond")[0]

    per_name = layer_grads_by_name(dlw)
    pieces, gwhere, off = [], [], 0
    for k, ax in BIG:
        shp = wts[k].shape
        for g in per_name[k]:
            if ax == 1 and k not in GRAD_TRANSPOSED:
                pieces.append(g.reshape(shp[1], nsh, shp[2]).transpose(1, 0, 2).reshape(nsh, -1))
            else:
                pieces.append(g.reshape(nsh, -1))
        n = int(np.prod(shp))
        if n % PACK_W:
            pieces.append(jnp.zeros((nsh, _ceil_to(n, PACK_W) - n), F32))
        gwhere.append((off, _ceil_to(n, PACK_W) // PACK_W))
        off += _ceil_to(n, PACK_W) // PACK_W
    if off % 128:
        pieces.append(jnp.zeros((nsh, (_ceil_to(off, 128) - off) * PACK_W), F32))
    half = _ceil_to(off, 128) // 2
    gpack = jnp.concatenate(pieces, axis=1).reshape(nsh, 2, half, PACK_W)
    from_pair = pair_exchange_halves(gpack, name="reduce_pair")
    chip_sum = add_kept_half(gpack, from_pair, jnp.reshape(mc, (1,)).astype(jnp.int32), name="reduce_pair_add",
                             out_dtype=BF16)
    from_xy = all_to_all_xy(chip_sum, name="reduce_xy")
    reduced = sum_slots(from_xy, name="reduce_xy_add")
    gfull = pair_all_gather(reduced, name="reduce_share").reshape(2 * half, PACK_W)
    for (k, _), wh in zip(BIG, gwhere):
        shp = wts[k].shape
        grads[k] = (_unpack(gfull, wh, (shp[0], shp[2], shp[1])).transpose(0, 2, 1) if k in GRAD_TRANSPOSED
                    else _unpack(gfull, wh, shp))
    for k, tr, g in zip(FFN_NAMES, FFN_TRANSPOSED, reduce_ffn(dwf[0], dwf[1], name="reduce_ffn")):
        grads[k] = g.transpose(0, 2, 1) if tr else g

    outs = {k: adamw(wts[k], grads[k], mom[k], var[k], name="adamw_" + k) for k in WEIGHTS}
    return (loss, grad_x, *[grads[k] for k in WEIGHTS], *[outs[k][0] for k in WEIGHTS], *[outs[k][1] for k in WEIGHTS],
            *[outs[k][2] for k in WEIGHTS])
```

```python
import functools

import jax
import jax.numpy as jnp
import numpy as np
from jax import lax
from jax.experimental import pallas as pl
from jax.experimental.pallas import tpu as pltpu

F32 = jnp.float32
BF16 = jnp.bfloat16
HI = lax.Precision.HIGHEST
MESH = pl.DeviceIdType.MESH
ANY = pl.BlockSpec(memory_space=pl.ANY)

V7X_VMEM_BYTES = 64 * 1024 * 1024
VMEM_LIMIT = V7X_VMEM_BYTES - 8 * 1024 * 1024
LANE = 128
PACK_W = 1024

GRID_W = 64
HEAD_DIM = 64
NA_HEADS, NA_ROWS, NA_COLS = 4, 8, 16
GQA_HEADS, GQA_KV_HEADS = 8, 2
MLA_HEADS, MLA_Q_RANK, MLA_KV_RANK, MLA_NOPE, MLA_ROPE, MLA_V = 4, 256, 128, 64, 32, 64
N_MOD = 9
ROPE_THETA = 10000.0
EPS = 1e-6
NEG_BIG = -1e30
NA_W = NA_HEADS * HEAD_DIM
GQ_W = GQA_HEADS * HEAD_DIM
GK_W = GQA_KV_HEADS * HEAD_DIM
MAIN_W = 3 * NA_W + GQ_W + 2 * GK_W + MLA_Q_RANK + MLA_KV_RANK + MLA_ROPE
MAIN_PAD = 2048
LOG2E, LN2 = float(np.log2(np.e)), float(np.log(2.0))
Q_SCALE = HEAD_DIM ** -0.5 * LOG2E
MLA_Q_SCALE = (MLA_NOPE + MLA_ROPE) ** -0.5 * LOG2E
TQ = 256
TM = 256

ADAM_LR, ADAM_B1, ADAM_B2, ADAM_EPS, ADAM_WD, ADAM_STEP = 0.001, 0.9, 0.999, 1e-08, 0.01, 10

ARG_NAMES = ['x', 'c', 'ctx', 'c_ctx', 'w_ada', 'b_ada', 'ffn1_norm', 'ffn1_w_gate', 'ffn1_w_up', 'ffn1_w_down', 'mix_norm', 'w_in',
             'na_rel_bias', 'gqa_q_norm', 'gqa_k_norm', 'mla_q_norm', 'mla_kv_norm', 'mla_w_uq', 'mla_w_ukv', 'w_branch_a',
             'w_branch_b', 'w_branch_c', 'w_out', 'ffn2_norm', 'ffn2_w_gate', 'ffn2_w_up', 'ffn2_w_down', 'final_norm']
WEIGHTS = ARG_NAMES[3:]
BIG = [('w_in', 1), ('mla_w_uq', 1), ('mla_w_ukv', 1), ('w_branch_a', 1), ('w_branch_b', 1), ('w_branch_c', 1), ('w_out', 0)]
GRAD_TRANSPOSED = ('w_in',)
FFN_NAMES = ['ffn1_w_gate', 'ffn1_w_up', 'ffn1_w_down', 'ffn2_w_gate', 'ffn2_w_up', 'ffn2_w_down']
FFN_TRANSPOSED = [True, True, False, True, True, False]
SMALL_LAYER = ['ffn1_norm', 'mix_norm', 'na_rel_bias', 'gqa_q_norm', 'gqa_k_norm', 'mla_q_norm', 'mla_kv_norm', 'ffn2_norm']


def _cp(*sem):
    return pltpu.CompilerParams(dimension_semantics=sem, vmem_limit_bytes=VMEM_LIMIT)


def _tile(dim, cands):
    for t in cands:
        if dim % t == 0:
            return t
    return dim


def _row_tile(rows, cap=512, mult=16):
    best = None
    for t in range(mult, min(rows, cap) + 1, mult):
        if rows % t == 0:
            best = t
    return best or rows


def _ceil_to(n, m):
    return -(-n // m) * m


def mm(a, b, *, name, ta=False, tb=False, out_dtype=F32, precise=False):
    m, k = (a.shape[1], a.shape[0]) if ta else a.shape
    n = b.shape[0] if tb else b.shape[1]
    tm = _tile(m, (512, 256, 128))
    tn = _tile(n, (1024, 1408, 512, 256, 128))
    tk = _tile(k, (1024, 1408, 512, 256, 128))
    nk = k // tk
    dims = (((0 if ta else 1,), (1 if tb else 0,)), ((), ()))

    def body(a_ref, b_ref, o_ref, *acc):
        if precise:
            part = lax.dot_general(a_ref[...].astype(F32), b_ref[...].astype(F32), dims, precision=HI, preferred_element_type=F32)
        else:
            part = lax.dot_general(a_ref[...].astype(BF16), b_ref[...].astype(BF16), dims, preferred_element_type=F32)
        if nk == 1:
            o_ref[...] = part.astype(o_ref.dtype)
        else:
            acc_ref, = acc
            kk = pl.program_id(2)

            @pl.when(kk == 0)
            def _():
                acc_ref[...] = part

            @pl.when(kk > 0)
            def _():
                acc_ref[...] += part

            @pl.when(kk == nk - 1)
            def _():
                o_ref[...] = acc_ref[...].astype(o_ref.dtype)

    a_spec = pl.BlockSpec((tk, tm), lambda i, j, kk: (kk, i)) if ta else pl.BlockSpec((tm, tk), lambda i, j, kk: (i, kk))
    b_spec = pl.BlockSpec((tn, tk), lambda i, j, kk: (j, kk)) if tb else pl.BlockSpec((tk, tn), lambda i, j, kk: (kk, j))
    return pl.pallas_call(
        body, name=name, grid=(m // tm, n // tn, nk), in_specs=[a_spec, b_spec],
        out_specs=pl.BlockSpec((tm, tn), lambda i, j, kk: (i, j)),
        out_shape=jax.ShapeDtypeStruct((m, n), out_dtype),
        scratch_shapes=[pltpu.VMEM((tm, tn), F32)] if nk > 1 else [],
        compiler_params=_cp("parallel", "parallel", "arbitrary"),
    )(a, b)


def mm_resident(a, w, *, name, tb=False, out_dtype=F32):
    m, k = a.shape
    n = w.shape[0] if tb else w.shape[1]
    tm = _tile(m, (512, 256, 128))
    cn = n if tb else _tile(n, (1024, 512, 256, 128))

    def body(a_ref, w_ref, o_ref):
        aa = a_ref[...].astype(BF16)
        if tb:
            o_ref[...] = _dot(aa, w_ref[...], _NT).astype(o_ref.dtype)
        else:
            for c in range(n // cn):
                cols = slice(cn * c, cn * (c + 1))
                o_ref[:, cols] = _dot(aa, w_ref[:, cols]).astype(o_ref.dtype)

    return pl.pallas_call(
        body, name=name, grid=(m // tm,),
        in_specs=[pl.BlockSpec((tm, k), lambda i: (i, 0)), pl.BlockSpec(w.shape, lambda i: (0, 0), pipeline_mode=pl.Buffered(1))],
        out_specs=pl.BlockSpec((tm, n), lambda i: (i, 0)), out_shape=jax.ShapeDtypeStruct((m, n), out_dtype),
        compiler_params=_cp("parallel"),
    )(a, w)


FFN_GATE, FFN_UP, FFN_DOWN = 0, 1, 2


def _ffn_wspec(wf, l, which):
    _, nsh, _, cs, d = wf.shape
    return pl.BlockSpec((None, nsh, None, cs, d), lambda *_: (l, 0, which, 0, 0), pipeline_mode=pl.Buffered(1))


def _ffn_group(cs):
    for g in (1, 2, 4):
        if (g * cs) % LANE == 0:
            return g
    raise ValueError(cs)


def ffn_up(n, wf, l, base, *, name):
    m, d = n.shape
    nsh, cs = wf.shape[1], wf.shape[3]
    f = nsh * cs
    grp = _ffn_group(cs)
    tm = _tile(m, (512, 256, 128))

    def body(n_ref, wg_ref, wu_ref, g_ref, u_ref, a_ref):
        nn = n_ref[...]
        for c in range(nsh // grp):
            cols = slice(grp * cs * c, grp * cs * (c + 1))
            g = _dot(nn, wg_ref[grp * c:grp * (c + 1)].reshape(grp * cs, d), _NT)
            u = _dot(nn, wu_ref[grp * c:grp * (c + 1)].reshape(grp * cs, d), _NT)
            g_ref[:, cols] = g.astype(BF16)
            u_ref[:, cols] = u.astype(BF16)
            a_ref[:, cols] = f_act_gu(g, u).astype(BF16)

    ospec = pl.BlockSpec((tm, f), lambda i: (i, 0))
    return pl.pallas_call(
        body, name=name, grid=(m // tm,),
        in_specs=[pl.BlockSpec((tm, d), lambda i: (i, 0)), _ffn_wspec(wf, l, base + FFN_GATE), _ffn_wspec(wf, l, base + FFN_UP)],
        out_specs=[ospec] * 3, out_shape=[jax.ShapeDtypeStruct((m, f), BF16)] * 3, compiler_params=_cp("parallel"),
    )(n, wf, wf)


def ffn_down(act, wf, l, base, *, name):
    m, f = act.shape
    nsh, cs, d = wf.shape[1], wf.shape[3], wf.shape[4]
    tm = _tile(m, (512, 256, 128))

    def body(a_ref, wd_ref, y_ref):
        y_ref[...] = _dot(a_ref[...], wd_ref[...].reshape(f, d))

    return pl.pallas_call(
        body, name=name, grid=(m // tm,),
        in_specs=[pl.BlockSpec((tm, f), lambda i: (i, 0)), _ffn_wspec(wf, l, base + FFN_DOWN)],
        out_specs=pl.BlockSpec((tm, d), lambda i: (i, 0)), out_shape=jax.ShapeDtypeStruct((m, d), F32), compiler_params=_cp("parallel"),
    )(act, wf)


def ffn_down_bwd(dy, g, u, wf, l, base, *, name):
    m, d = dy.shape
    nsh, cs = wf.shape[1], wf.shape[3]
    f = nsh * cs
    grp = _ffn_group(cs)
    tm = _tile(m, (512, 256, 128))

    def body(dy_ref, g_ref, u_ref, wd_ref, dg_ref, du_ref):
        dd = dy_ref[...]
        for c in range(nsh // grp):
            cols = slice(grp * cs * c, grp * cs * (c + 1))
            dact = _dot(dd, wd_ref[grp * c:grp * (c + 1)].reshape(grp * cs, d), _NT)
            dg, du = jax.vjp(f_act_gu, g_ref[:, cols].astype(F32), u_ref[:, cols].astype(F32))[1](dact)
            dg_ref[:, cols] = dg.astype(BF16)
            du_ref[:, cols] = du.astype(BF16)

    fspec = pl.BlockSpec((tm, f), lambda i: (i, 0))
    return pl.pallas_call(
        body, name=name, grid=(m // tm,),
        in_specs=[pl.BlockSpec((tm, d), lambda i: (i, 0)), fspec, fspec, _ffn_wspec(wf, l, base + FFN_DOWN)],
        out_specs=[fspec] * 2, out_shape=[jax.ShapeDtypeStruct((m, f), BF16)] * 2, compiler_params=_cp("parallel"),
    )(dy, g, u, wf)


def ffn_up_bwd(dg, du, wf, l, base, *, name):
    m, f = dg.shape
    nsh, cs, d = wf.shape[1], wf.shape[3], wf.shape[4]
    tm = _tile(m, (512, 256, 128))

    def body(dg_ref, du_ref, wg_ref, wu_ref, dn_ref):
        dn_ref[...] = _dot(dg_ref[...], wg_ref[...].reshape(f, d)) + _dot(du_ref[...], wu_ref[...].reshape(f, d))

    fspec = pl.BlockSpec((tm, f), lambda i: (i, 0))
    return pl.pallas_call(
        body, name=name, grid=(m // tm,),
        in_specs=[fspec, fspec, _ffn_wspec(wf, l, base + FFN_GATE), _ffn_wspec(wf, l, base + FFN_UP)],
        out_specs=pl.BlockSpec((tm, d), lambda i: (i, 0)), out_shape=jax.ShapeDtypeStruct((m, d), F32), compiler_params=_cp("parallel"),
    )(dg, du, wf, wf)


def ffn_dw(a, b, nsh, *, name):
    m, f = a.shape
    d = b.shape[1]
    cs = f // nsh
    grp = _ffn_group(cs)
    tm = _tile(m, (1024, 512, 256, 128))

    def body(a_ref, b_ref, o_ref):
        part = _dot(a_ref[...], b_ref[...], _TN).reshape(grp, cs, d)
        i = pl.program_id(1)

        @pl.when(i == 0)
        def _():
            o_ref[...] = part

        @pl.when(i > 0)
        def _():
            o_ref[...] += part

    return pl.pallas_call(
        body, name=name, grid=(nsh // grp, m // tm),
        in_specs=[pl.BlockSpec((tm, grp * cs), lambda j, i: (i, j)), pl.BlockSpec((tm, d), lambda j, i: (i, 0))],
        out_specs=pl.BlockSpec((grp, cs, d), lambda j, i: (j, 0, 0)), out_shape=jax.ShapeDtypeStruct((nsh, cs, d), F32),
        compiler_params=_cp("parallel", "arbitrary"),
    )(a, b)


def rowcall(name, fn, ins, outs, *, nb, nt, nct):
    in_specs, arrays = [], []
    for arr, kind in ins:
        arrays.append(arr)
        if kind == 'tok':
            in_specs.append(pl.BlockSpec((None, TM, arr.shape[-1]), lambda b, t: (b, t, 0)))
        elif kind == 'lat':
            in_specs.append(pl.BlockSpec((None, TM, arr.shape[-1]), lambda b, t: (b, jnp.maximum(t - nct, 0), 0)))
        elif kind == 'pos':
            in_specs.append(pl.BlockSpec((TM, arr.shape[-1]), lambda b, t: (t, 0)))
        elif kind == 'mod':
            in_specs.append(pl.BlockSpec((None, None, 1, arr.shape[-1]), lambda b, t: (b, jnp.where(t >= nct, 1, 0), 0, 0)))
        elif kind == 'full':
            in_specs.append(pl.BlockSpec(arr.shape, lambda b, t, nd=arr.ndim: (0,) * nd))
        else:
            _, w, j = kind
            in_specs.append(pl.BlockSpec((None, TM, w), lambda b, t, j=j: (b, t, j)))
    out_specs, out_shape = [], []
    for o in outs:
        if o[0] == 'tok':
            out_specs.append(pl.BlockSpec((None, TM, o[1]), lambda b, t: (b, t, 0)))
            out_shape.append(jax.ShapeDtypeStruct((nb, nt * TM, o[1]), o[2]))
        elif o[0] == 'mod':
            out_specs.append(pl.BlockSpec((None, None, 1, o[1]), lambda b, t: (b, jnp.where(t >= nct, 1, 0), 0, 0)))
            out_shape.append(jax.ShapeDtypeStruct((nb, 2, 1, o[1]), F32))
        else:
            out_specs.append(pl.BlockSpec(o[1], lambda b, t, nd=len(o[1]): (0,) * nd))
            out_shape.append(jax.ShapeDtypeStruct(o[1], F32))
    n_in = len(ins)

    def body(*refs):
        b, t = pl.program_id(0), pl.program_id(1)
        res = fn(t < nct, *[r[...] for r in refs[:n_in]])
        for ref, o, val in zip(refs[n_in:], outs, res, strict=True):
            if o[0] == 'tok':
                ref[...] = val.astype(ref.dtype)
                continue
            first = ((t == 0) | (t == nct)) if o[0] == 'mod' else ((b == 0) & (t == 0))

            @pl.when(first)
            def _(ref=ref, val=val):
                ref[...] = val

            @pl.when(jnp.logical_not(first))
            def _(ref=ref, val=val):
                ref[...] += val

    return pl.pallas_call(body, name=name, grid=(nb, nt), in_specs=in_specs, out_specs=out_specs, out_shape=out_shape,
                          compiler_params=_cp("arbitrary", "arbitrary"))(*arrays)


def _rms(x, g):
    return x * lax.rsqrt(jnp.mean(x * x, axis=-1, keepdims=True) + EPS) * g


def f_normmod(h, g, shift, scale):
    return _rms(h, g) * (1.0 + scale) + shift


def f_act_gu(g, u):
    return jax.nn.silu(g) * u


def _dot_split(x, m, dims):
    hi = x.astype(BF16)
    lo = (x - hi.astype(F32)).astype(BF16)
    mb = m.astype(BF16)
    return (lax.dot_general(hi, mb, dims, preferred_element_type=F32) + lax.dot_general(lo, mb, dims, preferred_element_type=F32))


def dot_select(x, m):
    return _dot_select(x, m)


@jax.custom_vjp
def _dot_select(x, m):
    return _dot_split(x, m, (((1,), (0,)), ((), ())))


_dot_select.defvjp(lambda x, m: (_dot_split(x, m, (((1,), (0,)), ((), ()))), m),
                   lambda m, ct: (_dot_split(ct, m, (((1,), (1,)), ((), ()))), jnp.zeros_like(m)))


def f_merge(ga, gb, gm, ya, yb, ym):
    return jax.nn.sigmoid(ga) * ya + jax.nn.sigmoid(gb) * yb + jax.nn.sigmoid(gm) * ym


def f_post(p, cb, sb, cm, sm, qn, kn, mqn, mkvn, wuq, wukv, s_b, r_b, t_b, r_m, rep, dup):
    def hnorm(x, g, w):
        ms = dot_select(x * x, s_b[:w, :w])
        gw = dot_select(g, t_b[:, :w])
        return x * lax.rsqrt(ms + EPS) * gw

    def rope(x, cos, sin, rot):
        return x * cos + dot_select(x, rot) * sin

    o = 3 * NA_W
    a_q, a_k, a_v = p[:, 0:NA_W], p[:, NA_W:2 * NA_W], p[:, 2 * NA_W:o]
    b_q = rope(hnorm(p[:, o:o + GQ_W], qn, GQ_W), cb, sb, r_b)
    o += GQ_W
    b_k = rope(hnorm(p[:, o:o + GK_W], kn, GK_W), cb[:, :GK_W], sb[:, :GK_W], r_b[:GK_W, :GK_W])
    b_v = p[:, o + GK_W:o + 2 * GK_W]
    o += 2 * GK_W
    q_lat = jnp.dot(_rms(p[:, o:o + MLA_Q_RANK], mqn).astype(BF16), wuq.astype(BF16), preferred_element_type=F32)
    o += MLA_Q_RANK
    kv_lat = jnp.dot(_rms(p[:, o:o + MLA_KV_RANK], mkvn).astype(BF16), wukv.astype(BF16), preferred_element_type=F32)
    o += MLA_KV_RANK
    nw = MLA_HEADS * MLA_NOPE
    mq_nope, mq_rope = q_lat[:, :nw], rope(q_lat[:, nw:], cm, sm, r_m)
    mk_nope, m_v = kv_lat[:, :nw], kv_lat[:, nw:]
    mk_rope = dot_select(rope(p[:, o:o + LANE], cm, sm, r_m), rep)
    b_k2 = dot_select(b_k, dup)
    b_v2 = dot_select(b_v, dup)
    return (a_q * Q_SCALE, a_k, a_v, b_q * Q_SCALE, b_k2, b_v2, mq_nope * MLA_Q_SCALE, mq_rope * MLA_Q_SCALE, mk_nope, mk_rope, m_v)


POST_QK = (0, 1, 3, 4, 6, 7, 8, 9)


POST_WIDTHS = (NA_W, NA_W, NA_W, GQ_W, 2 * GK_W, 2 * GK_W, MLA_HEADS * MLA_NOPE, MLA_HEADS * MLA_ROPE, MLA_HEADS * MLA_NOPE,
               MLA_HEADS * MLA_ROPE, MLA_HEADS * MLA_V)


_NT = (((1,), (1,)), ((), ()))
_TN = (((0,), (0,)), ((), ()))


def _dot(a, b, dims=None):
    if dims is None:
        return jnp.dot(a, b, preferred_element_type=F32)
    return lax.dot_general(a, b, dims, preferred_element_type=F32)


def _lanes(lo, width):
    lane = lax.broadcasted_iota(jnp.int32, (1, LANE), 1)
    return (lane >= lo) & (lane < lo + width)


def _only(x, mask):
    return jnp.where(mask, x, jnp.zeros_like(x))


def _stack_pair(x, width, lo):
    return jnp.concatenate([_only(x, _lanes(lo, width)), _only(x, _lanes(lo + width, width))], axis=0)


def _pair_softmax(s):
    m = jnp.max(s, axis=-1, keepdims=True)
    p = jnp.exp2(s - m)
    l = jnp.sum(p, axis=-1, keepdims=True)
    return p, l, m + jnp.log2(l)


def gqa_fwd(q, k2, v2, *, lc, ctx_q, name):
    nb, t, qw = q.shape
    npair = qw // LANE
    per_kv = npair // GQA_KV_HEADS
    nctb = lc // TQ

    def body(q_ref, k_ref, v_ref, o_ref, lse_ref):
        i = pl.program_id(2)

        def run(rows):
            kk, vv = k_ref[rows, :], v_ref[rows, :]
            outs = []
            for e in range(2):
                p, l, lse = _pair_softmax(_dot(_only(q_ref[...], _lanes(HEAD_DIM * e, HEAD_DIM)), kk, _NT))
                outs.append(_dot(p.astype(BF16), vv) / l)
                lse_ref[e] = lse
            o_ref[...] = jnp.where(_lanes(0, HEAD_DIM), outs[0], outs[1]).astype(o_ref.dtype)

        @pl.when(i < nctb)
        def _():
            if ctx_q:
                run(pl.ds(0, lc))
            else:
                o_ref[...] = jnp.zeros_like(o_ref)
                lse_ref[...] = jnp.zeros_like(lse_ref)

        @pl.when(i >= nctb)
        def _():
            run(pl.ds(0, t))

    qmap = lambda b, p, i: (b, i, p)
    kmap = lambda b, p, i: (b, 0, p // per_kv)
    return pl.pallas_call(
        body, name=name, grid=(nb, npair, t // TQ),
        in_specs=[pl.BlockSpec((None, TQ, LANE), qmap), pl.BlockSpec((None, t, LANE), kmap), pl.BlockSpec((None, t, LANE), kmap)],
        out_specs=[pl.BlockSpec((None, TQ, LANE), qmap), pl.BlockSpec((None, 2, TQ, 1), lambda b, p, i: (b, p, i, 0))],
        out_shape=[jax.ShapeDtypeStruct((nb, t, qw), BF16), jax.ShapeDtypeStruct((nb, 2 * npair, t, 1), F32)],
        compiler_params=_cp("parallel", "parallel", "arbitrary"),
    )(q, k2, v2)


def gqa_bwd(q, k2, v2, lse, do, *, lc, ctx_q, name):
    nb, t, qw = q.shape
    npair = qw // LANE
    per_kv = npair // GQA_KV_HEADS
    nctb = lc // TQ

    def body(q_ref, k_ref, v_ref, lse_ref, do_ref, dq_ref, dk_ref, dv_ref):
        g, i = pl.program_id(2), pl.program_id(3)

        @pl.when((g == 0) & (i == 0))
        def _():
            dk_ref[...] = jnp.zeros_like(dk_ref)
            dv_ref[...] = jnp.zeros_like(dv_ref)

        def run(rows):
            kk, vv = k_ref[rows, :], v_ref[rows, :]
            qq, dd = _stack_pair(q_ref[...], HEAD_DIM, 0), _stack_pair(do_ref[...], HEAD_DIM, 0)
            p = jnp.exp2(_dot(qq, kk, _NT) - jnp.concatenate([lse_ref[0], lse_ref[1]], axis=0))
            dp = _dot(dd, vv, _NT)
            delta = jnp.sum(p * dp, axis=-1, keepdims=True)
            ds = (p * (dp - delta)).astype(BF16)
            dq = _dot(ds, kk)
            dq_ref[...] = jnp.where(_lanes(0, HEAD_DIM), dq[:TQ], dq[TQ:])
            dk_ref[rows, :] += _dot(ds, qq, _TN)
            dv_ref[rows, :] += _dot(p.astype(BF16), dd, _TN)

        @pl.when(i < nctb)
        def _():
            if ctx_q:
                run(pl.ds(0, lc))
            else:
                dq_ref[...] = jnp.zeros_like(dq_ref)

        @pl.when(i >= nctb)
        def _():
            run(pl.ds(0, t))

    qmap = lambda b, j, g, i: (b, i, j * per_kv + g)
    kmap = lambda b, j, g, i: (b, 0, j)
    return pl.pallas_call(
        body, name=name, grid=(nb, GQA_KV_HEADS, per_kv, t // TQ),
        in_specs=[pl.BlockSpec((None, TQ, LANE), qmap), pl.BlockSpec((None, t, LANE), kmap), pl.BlockSpec((None, t, LANE), kmap),
                  pl.BlockSpec((None, 2, TQ, 1), lambda b, j, g, i: (b, j * per_kv + g, i, 0)), pl.BlockSpec((None, TQ, LANE), qmap)],
        out_specs=[pl.BlockSpec((None, TQ, LANE), qmap), pl.BlockSpec((None, t, LANE), kmap), pl.BlockSpec((None, t, LANE), kmap)],
        out_shape=[jax.ShapeDtypeStruct((nb, t, qw), F32), jax.ShapeDtypeStruct(k2.shape, F32), jax.ShapeDtypeStruct(v2.shape, F32)],
        compiler_params=_cp("arbitrary", "arbitrary", "arbitrary", "arbitrary"),
    )(q, k2, v2, lse, do)


def mla_fwd(qn, qr, kn, kr, v, *, lc, ctx_q, name):
    nb, t, w = qn.shape
    npair = w // LANE
    nctb = lc // TQ

    def body(qn_ref, qr_ref, kn_ref, kr_ref, v_ref, o_ref, lse_ref):
        pr, i = pl.program_id(1), pl.program_id(2)

        def run(rows):
            kk, kkr, vv = kn_ref[rows, :], kr_ref[rows, :], v_ref[rows, :]
            outs = []
            for e in range(2):
                s = (_dot(_only(qn_ref[...], _lanes(MLA_NOPE * e, MLA_NOPE)), kk, _NT)
                     + _dot(_only(qr_ref[...], _lanes(MLA_ROPE * (2 * pr + e), MLA_ROPE)), kkr, _NT))
                p, l, lse = _pair_softmax(s)
                outs.append(_dot(p.astype(BF16), vv) / l)
                lse_ref[e] = lse
            o_ref[...] = jnp.where(_lanes(0, MLA_V), outs[0], outs[1]).astype(o_ref.dtype)

        @pl.when(i < nctb)
        def _():
            if ctx_q:
                run(pl.ds(0, lc))
            else:
                o_ref[...] = jnp.zeros_like(o_ref)
                lse_ref[...] = jnp.zeros_like(lse_ref)

        @pl.when(i >= nctb)
        def _():
            run(pl.ds(0, t))

    qmap = lambda b, p, i: (b, i, p)
    rmap = lambda b, p, i: (b, i, 0)
    kmap = lambda b, p, i: (b, 0, p)
    return pl.pallas_call(
        body, name=name, grid=(nb, npair, t // TQ),
        in_specs=[pl.BlockSpec((None, TQ, LANE), qmap), pl.BlockSpec((None, TQ, LANE), rmap), pl.BlockSpec((None, t, LANE), kmap),
                  pl.BlockSpec((None, t, LANE), lambda b, p, i: (b, 0, 0)), pl.BlockSpec((None, t, LANE), kmap)],
        out_specs=[pl.BlockSpec((None, TQ, LANE), qmap), pl.BlockSpec((None, 2, TQ, 1), lambda b, p, i: (b, p, i, 0))],
        out_shape=[jax.ShapeDtypeStruct((nb, t, w), BF16), jax.ShapeDtypeStruct((nb, 2 * npair, t, 1), F32)],
        compiler_params=_cp("parallel", "parallel", "arbitrary"),
    )(qn, qr, kn, kr, v)


def mla_bwd(qn, qr, kn, kr, v, lse, do, *, lc, ctx_q, name):
    nb, t, w = qn.shape
    npair = w // LANE
    nctb = lc // TQ

    def body(qn_ref, qr_ref, kn_ref, kr_ref, v_ref, lse_ref, do_ref, dqn_ref, dqr_ref, dkn_ref, dkr_ref, dv_ref):
        pr, i = pl.program_id(1), pl.program_id(2)

        @pl.when(i == 0)
        def _():
            dkn_ref[...] = jnp.zeros_like(dkn_ref)
            dv_ref[...] = jnp.zeros_like(dv_ref)

        @pl.when((i == 0) & (pr == 0))
        def _():
            dkr_ref[...] = jnp.zeros_like(dkr_ref)

        def run(rows):
            kk, kkr, vv = kn_ref[rows, :], kr_ref[rows, :], v_ref[rows, :]
            r_lo = 2 * MLA_ROPE * pr
            qq, qqr = _stack_pair(qn_ref[...], MLA_NOPE, 0), _stack_pair(qr_ref[...], MLA_ROPE, r_lo)
            dd = _stack_pair(do_ref[...], MLA_V, 0)
            p = jnp.exp2(_dot(qq, kk, _NT) + _dot(qqr, kkr, _NT) - jnp.concatenate([lse_ref[0], lse_ref[1]], axis=0))
            dp = _dot(dd, vv, _NT)
            delta = jnp.sum(p * dp, axis=-1, keepdims=True)
            ds = (p * (dp - delta)).astype(BF16)
            dqn, dqr = _dot(ds, kk), _dot(ds, kkr)
            dqn_ref[...] = jnp.where(_lanes(0, MLA_NOPE), dqn[:TQ], dqn[TQ:])
            dqr_ref[...] = _only(dqr[:TQ], _lanes(r_lo, MLA_ROPE)) + _only(dqr[TQ:], _lanes(r_lo + MLA_ROPE, MLA_ROPE))
            dkn_ref[rows, :] += _dot(ds, qq, _TN)
            dkr_ref[rows, :] += _dot(ds, qqr, _TN)
            dv_ref[rows, :] += _dot(p.astype(BF16), dd, _TN)

        @pl.when(i < nctb)
        def _():
            if ctx_q:
                run(pl.ds(0, lc))
            else:
                dqn_ref[...] = jnp.zeros_like(dqn_ref)
                dqr_ref[...] = jnp.zeros_like(dqr_ref)

        @pl.when(i >= nctb)
        def _():
            run(pl.ds(0, t))

    qmap = lambda b, p, i: (b, i, p)
    rmap = lambda b, p, i: (b, i, 0)
    kmap = lambda b, p, i: (b, 0, p)
    zmap = lambda b, p, i: (b, 0, 0)
    return pl.pallas_call(
        body, name=name, grid=(nb, npair, t // TQ),
        in_specs=[pl.BlockSpec((None, TQ, LANE), qmap), pl.BlockSpec((None, TQ, LANE), rmap), pl.BlockSpec((None, t, LANE), kmap),
                  pl.BlockSpec((None, t, LANE), zmap), pl.BlockSpec((None, t, LANE), kmap),
                  pl.BlockSpec((None, 2, TQ, 1), lambda b, p, i: (b, p, i, 0)), pl.BlockSpec((None, TQ, LANE), qmap)],
        out_specs=[pl.BlockSpec((None, TQ, LANE), qmap), pl.BlockSpec((None, TQ, LANE), qmap), pl.BlockSpec((None, t, LANE), kmap),
                   pl.BlockSpec((None, t, LANE), zmap), pl.BlockSpec((None, t, LANE), kmap)],
        out_shape=[jax.ShapeDtypeStruct((nb, t, w), F32), jax.ShapeDtypeStruct((nb, t, npair * LANE), F32),
                   jax.ShapeDtypeStruct((nb, t, w), F32), jax.ShapeDtypeStruct((nb, t, LANE), F32), jax.ShapeDtypeStruct((nb, t, w), F32)],
        compiler_params=_cp("arbitrary", "arbitrary", "arbitrary"),
    )(qn, qr, kn, kr, v, lse, do)


def _na_window(st, nc, rows):
    r = jnp.maximum(st - nc, 0)
    r0 = jnp.clip(r - NA_ROWS // 2, 0, rows - NA_ROWS)
    return r, r0, r - r0


def na_fwd(q, k, v, bias, *, lc, ctx_q, name):
    nb, t, w = q.shape
    npair = w // LANE
    nc, rows = lc // GRID_W, (t - lc) // GRID_W
    nwin = NA_ROWS * GRID_W

    def body(q_ref, k_ref, v_ref, bias_ref, o_ref, lse_ref):
        st = pl.program_id(2)
        ctx = pl.ds(0, lc)
        kc, vc = k_ref[ctx, :], v_ref[ctx, :]
        outs = [None, None]

        @pl.when(st < nc)
        def _():
            if not ctx_q:
                o_ref[...] = jnp.zeros_like(o_ref)
                lse_ref[...] = jnp.zeros_like(lse_ref)
                return
            for e in range(2):
                p, l, lse = _pair_softmax(_dot(_only(q_ref[...], _lanes(HEAD_DIM * e, HEAD_DIM)), kc, _NT))
                outs[e] = _dot(p.astype(BF16), vc) / l
                lse_ref[e] = lse
            o_ref[...] = jnp.where(_lanes(0, HEAD_DIM), outs[0], outs[1]).astype(o_ref.dtype)

        @pl.when(st >= nc)
        def _():
            _, r0, _ = _na_window(st, nc, rows)
            win = pl.ds(pl.multiple_of(lc + r0 * GRID_W, GRID_W), nwin)
            kw, vw = k_ref[win, :], v_ref[win, :]
            qq = _stack_pair(q_ref[...], HEAD_DIM, 0)
            s_loc = _dot(qq, kw, _NT) + jnp.concatenate([bias_ref[0], bias_ref[1]], axis=0) * LOG2E
            s_ctx = _dot(qq, kc, _NT)
            m = jnp.maximum(jnp.max(s_loc, axis=-1, keepdims=True), jnp.max(s_ctx, axis=-1, keepdims=True))
            p_loc, p_ctx = jnp.exp2(s_loc - m), jnp.exp2(s_ctx - m)
            l = jnp.sum(p_loc, axis=-1, keepdims=True) + jnp.sum(p_ctx, axis=-1, keepdims=True)
            o = (_dot(p_loc.astype(BF16), vw) + _dot(p_ctx.astype(BF16), vc)) / l
            lse = m + jnp.log2(l)
            lse_ref[0], lse_ref[1] = lse[:GRID_W], lse[GRID_W:]
            o_ref[...] = jnp.where(_lanes(0, HEAD_DIM), o[:GRID_W], o[GRID_W:]).astype(o_ref.dtype)

    qmap = lambda p, b, st: (b, st, p)
    kmap = lambda p, b, st: (b, 0, p)
    return pl.pallas_call(
        body, name=name, grid=(npair, nb, nc + rows),
        in_specs=[pl.BlockSpec((None, GRID_W, LANE), qmap), pl.BlockSpec((None, t, LANE), kmap), pl.BlockSpec((None, t, LANE), kmap),
                  pl.BlockSpec((2, None, GRID_W, nwin), lambda p, b, st: (p, _na_window(st, nc, rows)[2], 0, 0))],
        out_specs=[pl.BlockSpec((None, GRID_W, LANE), qmap), pl.BlockSpec((None, 2, GRID_W, 1), lambda p, b, st: (b, p, st, 0))],
        out_shape=[jax.ShapeDtypeStruct((nb, t, w), BF16), jax.ShapeDtypeStruct((nb, 2 * npair, t, 1), F32)],
        compiler_params=_cp("parallel", "parallel", "arbitrary"),
    )(q, k, v, bias)


def na_bwd(q, k, v, bias, lse, do, *, lc, ctx_q, name):
    nb, t, w = q.shape
    npair = w // LANE
    nc, rows = lc // GRID_W, (t - lc) // GRID_W
    nwin = NA_ROWS * GRID_W

    def body(q_ref, k_ref, v_ref, bias_ref, lse_ref, do_ref, dq_ref, dk_ref, dv_ref, db_ref):
        b, st = pl.program_id(1), pl.program_id(2)

        @pl.when(st == 0)
        def _():
            dk_ref[...] = jnp.zeros_like(dk_ref)
            dv_ref[...] = jnp.zeros_like(dv_ref)

        @pl.when((st == 0) & (b == 0))
        def _():
            db_ref[...] = jnp.zeros_like(db_ref)

        ctx = pl.ds(0, lc)
        kc, vc = k_ref[ctx, :], v_ref[ctx, :]
        dqs = [None, None]

        @pl.when(st < nc)
        def _():
            if not ctx_q:
                dq_ref[...] = jnp.zeros_like(dq_ref)
                return
            for e in range(2):
                mine = _lanes(HEAD_DIM * e, HEAD_DIM)
                qq, dd = _only(q_ref[...], mine), _only(do_ref[...], mine)
                p = jnp.exp2(_dot(qq, kc, _NT) - lse_ref[e])
                dp = _dot(dd, vc, _NT)
                delta = jnp.sum(p * dp, axis=-1, keepdims=True)
                ds = (p * (dp - delta)).astype(BF16)
                dqs[e] = _dot(ds, kc)
                dk_ref[ctx, :] += _dot(ds, qq, _TN)
                dv_ref[ctx, :] += _dot(p.astype(BF16), dd, _TN)
            dq_ref[...] = jnp.where(_lanes(0, HEAD_DIM), dqs[0], dqs[1])

        @pl.when(st >= nc)
        def _():
            _, r0, case = _na_window(st, nc, rows)
            win = pl.ds(pl.multiple_of(lc + r0 * GRID_W, GRID_W), nwin)
            kw, vw = k_ref[win, :], v_ref[win, :]
            qq, dd = _stack_pair(q_ref[...], HEAD_DIM, 0), _stack_pair(do_ref[...], HEAD_DIM, 0)
            lse = jnp.concatenate([lse_ref[0], lse_ref[1]], axis=0)
            p_loc = jnp.exp2(_dot(qq, kw, _NT) + jnp.concatenate([bias_ref[0], bias_ref[1]], axis=0) * LOG2E - lse)
            p_ctx = jnp.exp2(_dot(qq, kc, _NT) - lse)
            dp_loc, dp_ctx = _dot(dd, vw, _NT), _dot(dd, vc, _NT)
            delta = jnp.sum(p_loc * dp_loc, axis=-1, keepdims=True) + jnp.sum(p_ctx * dp_ctx, axis=-1, keepdims=True)
            ds_loc = p_loc * (dp_loc - delta)
            db_ref[0, case] += ds_loc[:GRID_W]
            db_ref[1, case] += ds_loc[GRID_W:]
            ds_loc = ds_loc.astype(BF16)
            ds_ctx = (p_ctx * (dp_ctx - delta)).astype(BF16)
            dq = _dot(ds_loc, kw) + _dot(ds_ctx, kc)
            dq_ref[...] = jnp.where(_lanes(0, HEAD_DIM), dq[:GRID_W], dq[GRID_W:])
            dk_ref[win, :] += _dot(ds_loc, qq, _TN)
            dk_ref[ctx, :] += _dot(ds_ctx, qq, _TN)
            dv_ref[win, :] += _dot(p_loc.astype(BF16), dd, _TN)
            dv_ref[ctx, :] += _dot(p_ctx.astype(BF16), dd, _TN)

    qmap = lambda p, b, st: (b, st, p)
    kmap = lambda p, b, st: (b, 0, p)
    return pl.pallas_call(
        body, name=name, grid=(npair, nb, nc + rows),
        in_specs=[pl.BlockSpec((None, GRID_W, LANE), qmap), pl.BlockSpec((None, t, LANE), kmap), pl.BlockSpec((None, t, LANE), kmap),
                  pl.BlockSpec((2, None, GRID_W, nwin), lambda p, b, st: (p, _na_window(st, nc, rows)[2], 0, 0)),
                  pl.BlockSpec((None, 2, GRID_W, 1), lambda p, b, st: (b, p, st, 0)), pl.BlockSpec((None, GRID_W, LANE), qmap)],
        out_specs=[pl.BlockSpec((None, GRID_W, LANE), qmap), pl.BlockSpec((None, t, LANE), kmap), pl.BlockSpec((None, t, LANE), kmap),
                   pl.BlockSpec((2, NA_ROWS, GRID_W, nwin), lambda p, b, st: (p, 0, 0, 0))],
        out_shape=[jax.ShapeDtypeStruct((nb, t, w), F32), jax.ShapeDtypeStruct((nb, t, w), F32), jax.ShapeDtypeStruct((nb, t, w), F32),
                   jax.ShapeDtypeStruct((2 * npair, NA_ROWS, GRID_W, nwin), F32)],
        compiler_params=_cp("arbitrary", "arbitrary", "arbitrary"),
    )(q, k, v, bias, lse, do)


def _na_tables():
    cols = np.arange(GRID_W)
    c0 = np.clip(cols - NA_COLS // 2, 0, GRID_W - NA_COLS)
    col_in = (cols[None, :] >= c0[:, None]) & (cols[None, :] < c0[:, None] + NA_COLS)
    dc = np.clip(cols[None, :] - cols[:, None] + NA_COLS - 1, 0, 2 * NA_COLS - 2)
    dr = np.arange(NA_ROWS)[None, :] + (NA_ROWS - 1) - np.arange(NA_ROWS)[:, None]
    return col_in, dc, dr


def _na_onehots():
    col_in, dc, dr = _na_tables()
    e1 = np.zeros((GRID_W, GRID_W, LANE), np.float32)
    qi, ki = np.nonzero(col_in)
    e1[qi, ki, dc[qi, ki]] = 1.0
    e2 = np.zeros((2 * NA_ROWS, NA_ROWS, NA_ROWS), np.float32)
    ci, ji = np.meshgrid(np.arange(NA_ROWS), np.arange(NA_ROWS), indexing='ij')
    e2[dr[ci, ji], ci, ji] = 1.0
    return jnp.asarray(e1.reshape(GRID_W * GRID_W, LANE)), jnp.asarray(e2.reshape(2 * NA_ROWS, NA_ROWS * NA_ROWS)), col_in


def na_expand_bias(rel_bias, name):
    e1, e2, col_in = _na_onehots()
    nh = rel_bias.shape[0]
    nrow = NA_ROWS * NA_ROWS
    rel = jnp.pad(rel_bias, ((0, 0), (0, 1), (0, LANE - rel_bias.shape[2])))
    rel = rel.transpose(1, 0, 2).reshape(2 * NA_ROWS, nh * LANE)
    y = mm(e2, rel, ta=True, name=name + "_rows", precise=True)
    y = y.reshape(nrow, nh, LANE).transpose(1, 0, 2).reshape(nh * nrow, LANE)
    g = mm(y, e1, tb=True, name=name + "_cols", precise=True)
    g = g.reshape(nh, NA_ROWS, NA_ROWS, GRID_W, GRID_W).transpose(0, 1, 3, 2, 4)
    g = jnp.where(col_in[None, None, :, None, :], g, NEG_BIG)
    return g.reshape(nh, NA_ROWS, GRID_W, NA_ROWS * GRID_W)


def na_reduce_bias(dexp, name):
    e1, e2, _ = _na_onehots()
    nh = dexp.shape[0]
    x = dexp.reshape(nh, NA_ROWS, GRID_W, NA_ROWS, GRID_W).transpose(0, 1, 3, 2, 4).reshape(nh * NA_ROWS * NA_ROWS, GRID_W * GRID_W)
    y = mm(x, e1, name=name + "_cols", precise=True)
    y = y.reshape(nh, NA_ROWS * NA_ROWS, LANE).transpose(1, 0, 2).reshape(NA_ROWS * NA_ROWS, nh * LANE)
    z = mm(e2, y, name=name + "_rows", precise=True)
    return z.reshape(2 * NA_ROWS, nh, LANE).transpose(1, 0, 2)[:, :2 * NA_ROWS - 1, :2 * NA_COLS - 1]


def _rot_matrix(width, d_rot):
    f = d_rot // 4
    r = np.zeros((width, width), np.float32)
    for base in range(0, width, d_rot // 2):
        for j in range(f):
            r[base + f + j, base + j] = -1.0
            r[base + j, base + f + j] = 1.0
    return r


def _rope_tables(s_len, lc, d_rot, reps):
    half = d_rot // 2
    freqs = ROPE_THETA ** (-jnp.arange(0, half, 2, dtype=F32) / half)
    tpos = jnp.arange(s_len)
    row = (tpos // GRID_W).astype(F32)[:, None] * freqs
    col = (tpos % GRID_W).astype(F32)[:, None] * freqs
    ang = jnp.concatenate([row, row, col, col], axis=-1)
    cos = jnp.concatenate([jnp.ones((lc, d_rot), F32), jnp.cos(ang)], axis=0)
    sin = jnp.concatenate([jnp.zeros((lc, d_rot), F32), jnp.sin(ang)], axis=0)
    return jnp.tile(cos, (1, reps)), jnp.tile(sin, (1, reps))


def _post_consts():
    s_b = np.kron(np.eye(GQA_HEADS, dtype=np.float32), np.full((HEAD_DIM, HEAD_DIM), 1.0 / HEAD_DIM, np.float32))
    t_b = np.tile(np.eye(HEAD_DIM, dtype=np.float32), (1, GQA_HEADS))
    r_b = _rot_matrix(GQ_W, HEAD_DIM)
    r_m = _rot_matrix(LANE, MLA_ROPE)
    rep = np.zeros((LANE, LANE), np.float32)
    for h in range(MLA_HEADS):
        rep[np.arange(MLA_ROPE), h * MLA_ROPE + np.arange(MLA_ROPE)] = 1.0
    dup = np.zeros((GK_W, 2 * GK_W), np.float32)
    for j in range(GQA_KV_HEADS):
        for e in range(2):
            dup[HEAD_DIM * j + np.arange(HEAD_DIM), 2 * HEAD_DIM * j + HEAD_DIM * e + np.arange(HEAD_DIM)] = 1.0
    return tuple(jnp.asarray(a) for a in (s_b, r_b, t_b, r_m, rep, dup))


def _heads_to_parts(w, first):
    r = w.shape[0]
    w3 = w.reshape(r, MLA_HEADS, -1)
    return jnp.concatenate([w3[:, :, :first].reshape(r, -1), w3[:, :, first:].reshape(r, -1)], axis=1)


def _parts_to_heads(w, first):
    r = w.shape[0]
    nf = MLA_HEADS * first
    return jnp.concatenate([w[:, :nf].reshape(r, MLA_HEADS, first), w[:, nf:].reshape(r, MLA_HEADS, -1)], axis=2).reshape(r, -1)


def _place():
    return lax.axis_index("x"), lax.axis_index("y"), lax.axis_index("c")


def all_gather(v, *, name, with_c):
    flips = [(dx, dy, dc) for dx in (0, 1) for dy in (0, 1) for dc in ((0, 1) if with_c else (0,))][1:]
    n = len(flips) + 1

    def body(v_ref, out_ref, send_sems, recv_sems, local_sem):
        mx, my, mc = _place()

        def slot(px, py, pc):
            return 4 * px + 2 * py + pc if with_c else 2 * px + py

        mine = pltpu.make_async_copy(v_ref, out_ref.at[slot(mx, my, mc)], local_sem)
        mine.start()
        sends = []
        for j, (dx, dy, dc) in enumerate(flips):
            peer = (mx ^ dx, my ^ dy, mc ^ dc)
            cp = pltpu.make_async_remote_copy(src_ref=v_ref, dst_ref=out_ref.at[slot(mx, my, mc)], send_sem=send_sems.at[j],
                                              recv_sem=recv_sems.at[j], device_id=peer, device_id_type=MESH)
            cp.start()
            sends.append(cp)
        for j, (dx, dy, dc) in enumerate(flips):
            peer = (mx ^ dx, my ^ dy, mc ^ dc)
            pltpu.make_async_remote_copy(src_ref=v_ref, dst_ref=out_ref.at[slot(*peer)], send_sem=send_sems.at[j],
                                         recv_sem=recv_sems.at[j], device_id=peer, device_id_type=MESH).wait_recv()
        for cp in sends:
            cp.wait_send()
        mine.wait()

    return pl.pallas_call(
        body, name=name, in_specs=[ANY], out_specs=ANY, out_shape=jax.ShapeDtypeStruct((n,) + v.shape, v.dtype),
        scratch_shapes=[pltpu.SemaphoreType.DMA((n - 1,)), pltpu.SemaphoreType.DMA((n - 1,)), pltpu.SemaphoreType.DMA(())],
    )(v)


def gather_shards(v, *, name):
    _, h, w = v.shape
    flips = [(1, 0), (0, 1), (1, 1)]

    def body(v_ref, out_ref, send_sems, recv_sems):
        mx, my, mc = _place()
        me = 2 * mx + my
        sib = (mx, my, 1 - mc)

        def copy(k, src, dst, to):
            return pltpu.make_async_remote_copy(src_ref=src, dst_ref=dst, send_sem=send_sems.at[k], recv_sem=recv_sems.at[k],
                                                device_id=to, device_id_type=MESH)

        first = [copy(j, v_ref.at[mc], out_ref.at[me, mc], (mx ^ dx, my ^ dy, mc)) for j, (dx, dy) in enumerate(flips)]
        for cp in first:
            cp.start()
        passed = []
        for j, (dx, dy) in enumerate(flips):
            theirs = out_ref.at[2 * (mx ^ dx) + (my ^ dy), mc]
            copy(j, v_ref.at[mc], theirs, (mx ^ dx, my ^ dy, mc)).wait_recv()
            fw = copy(3 + j, theirs, theirs, sib)
            fw.start()
            passed.append(fw)
        for j, (dx, dy) in enumerate(flips):
            other = out_ref.at[2 * (mx ^ dx) + (my ^ dy), 1 - mc]
            copy(3 + j, other, other, sib).wait_recv()
        for cp in first + passed:
            cp.wait_send()

    out = pl.pallas_call(
        body, name=name, in_specs=[ANY], out_specs=ANY, out_shape=jax.ShapeDtypeStruct((4, 2, h, w), v.dtype),
        scratch_shapes=[pltpu.SemaphoreType.DMA((6,)), pltpu.SemaphoreType.DMA((6,))],
    )(v)
    mx, my, _ = _place()
    return lax.dynamic_update_slice(out, v[None], (2 * mx + my, 0, 0, 0))


def pair_exchange_halves(g, *, name):
    n, _, h, w = g.shape

    def body(g_ref, out_ref, send_sems, recv_sems):
        mx, my, mc = _place()
        sib = (mx, my, 1 - mc)
        cps = [pltpu.make_async_remote_copy(src_ref=g_ref.at[s, 1 - mc], dst_ref=out_ref.at[s], send_sem=send_sems.at[s],
                                            recv_sem=recv_sems.at[s], device_id=sib, device_id_type=MESH) for s in range(n)]
        for cp in cps:
            cp.start()
        for cp in cps:
            cp.wait_recv()
        for cp in cps:
            cp.wait_send()

    return pl.pallas_call(
        body, name=name, in_specs=[ANY], out_specs=ANY, out_shape=jax.ShapeDtypeStruct((n, h, w), g.dtype),
        scratch_shapes=[pltpu.SemaphoreType.DMA((n,)), pltpu.SemaphoreType.DMA((n,))],
    )(g)


def all_to_all_xy(v, *, name):
    def body(v_ref, out_ref, send_sems, recv_sems):
        mx, my, mc = _place()
        me = 2 * mx + my
        flips = [(1, 0), (0, 1), (1, 1)]
        sends = []
        for j, (dx, dy) in enumerate(flips):
            px, py = mx ^ dx, my ^ dy
            cp = pltpu.make_async_remote_copy(src_ref=v_ref.at[2 * px + py], dst_ref=out_ref.at[me], send_sem=send_sems.at[j],
                                              recv_sem=recv_sems.at[j], device_id=(px, py, mc), device_id_type=MESH)
            cp.start()
            sends.append(cp)
        for j, (dx, dy) in enumerate(flips):
            px, py = mx ^ dx, my ^ dy
            pltpu.make_async_remote_copy(src_ref=v_ref.at[me], dst_ref=out_ref.at[2 * px + py], send_sem=send_sems.at[j],
                                         recv_sem=recv_sems.at[j], device_id=(px, py, mc), device_id_type=MESH).wait_recv()
        for cp in sends:
            cp.wait_send()

    out = pl.pallas_call(
        body, name=name, in_specs=[ANY], out_specs=ANY, out_shape=jax.ShapeDtypeStruct(v.shape, v.dtype),
        scratch_shapes=[pltpu.SemaphoreType.DMA((3,)), pltpu.SemaphoreType.DMA((3,))],
    )(v)
    mx, my, _ = _place()
    me = 2 * mx + my
    return lax.dynamic_update_slice(out, lax.dynamic_slice_in_dim(v, me, 1, axis=0), (me, 0, 0))


def pair_all_gather(v, *, name):
    def body(v_ref, out_ref, send_sem, recv_sem):
        mx, my, mc = _place()
        cp = pltpu.make_async_remote_copy(src_ref=v_ref, dst_ref=out_ref.at[mc], send_sem=send_sem, recv_sem=recv_sem,
                                          device_id=(mx, my, 1 - mc), device_id_type=MESH)
        cp.start()
        pltpu.make_async_remote_copy(src_ref=v_ref, dst_ref=out_ref.at[1 - mc], send_sem=send_sem, recv_sem=recv_sem,
                                     device_id=(mx, my, 1 - mc), device_id_type=MESH).wait_recv()
        cp.wait_send()

    out = pl.pallas_call(
        body, name=name, in_specs=[ANY], out_specs=ANY, out_shape=jax.ShapeDtypeStruct((2,) + v.shape, v.dtype),
        scratch_shapes=[pltpu.SemaphoreType.DMA(()), pltpu.SemaphoreType.DMA(())],
    )(v)
    return lax.dynamic_update_slice(out, v[None], (_place()[2], 0, 0))


def gather_ffn(wl, *, name):
    nl, nblk, cs, d = wl.shape
    assert nl == 2
    flips = [(1, 0), (0, 1), (1, 1)]

    def body(v_ref, out_ref, send_sems, recv_sems):
        mx, my, mc = _place()
        me = 2 * mx + my
        sib = (mx, my, 1 - mc)

        def copy(k, src, dst, to):
            return pltpu.make_async_remote_copy(src_ref=src, dst_ref=dst, send_sem=send_sems.at[k], recv_sem=recv_sems.at[k],
                                                device_id=to, device_id_type=MESH)

        first = [copy(j, v_ref.at[mc], out_ref.at[mc, me], (mx ^ dx, my ^ dy, mc)) for j, (dx, dy) in enumerate(flips)]
        for cp in first:
            cp.start()
        passed = []
        for j, (dx, dy) in enumerate(flips):
            theirs = out_ref.at[mc, 2 * (mx ^ dx) + (my ^ dy)]
            copy(j, v_ref.at[mc], theirs, (mx ^ dx, my ^ dy, mc)).wait_recv()
            fw = copy(3 + j, theirs, theirs, sib)
            fw.start()
            passed.append(fw)
        for j, (dx, dy) in enumerate(flips):
            other = out_ref.at[1 - mc, 2 * (mx ^ dx) + (my ^ dy)]
            copy(3 + j, other, other, sib).wait_recv()
        for cp in first + passed:
            cp.wait_send()

    out = pl.pallas_call(
        body, name=name, in_specs=[ANY], out_specs=ANY, out_shape=jax.ShapeDtypeStruct((nl, 4, nblk, cs, d), wl.dtype),
        scratch_shapes=[pltpu.SemaphoreType.DMA((6,)), pltpu.SemaphoreType.DMA((6,))],
    )(wl)
    mx, my, _ = _place()
    return lax.dynamic_update_slice(out, wl[:, None], (0, 2 * mx + my, 0, 0, 0))


def reduce_ffn(g0, g1, *, name):
    nt = len(g0)
    nsh, cs, d = g0[0].shape
    flips = [(1, 0), (0, 1), (1, 1)]
    mx, my, mc = _place()
    me = 2 * mx + my
    c_idx = jnp.reshape(mc, (1,)).astype(jnp.int32)

    def pair_body(*refs):
        ins0, ins1, outs = refs[:nt], refs[nt:2 * nt], refs[2 * nt:3 * nt]
        send_sems, recv_sems = refs[3 * nt:]
        kx, ky, kc = _place()
        sib = (kx, ky, 1 - kc)
        for c in range(2):
            @pl.when(kc == c)
            def _(c=c):
                mine_out = (ins1, ins0)[c]
                cps = [pltpu.make_async_remote_copy(src_ref=mine_out[t], dst_ref=outs[t], send_sem=send_sems.at[t],
                                                    recv_sem=recv_sems.at[t], device_id=sib, device_id_type=MESH) for t in range(nt)]
                for cp in cps:
                    cp.start()
                for cp in cps:
                    cp.wait_recv()
                for cp in cps:
                    cp.wait_send()

    from_pair = pl.pallas_call(
        pair_body, name=name + "_pair", in_specs=[ANY] * (2 * nt), out_specs=[ANY] * nt,
        out_shape=[jax.ShapeDtypeStruct((nsh, cs, d), F32)] * nt,
        scratch_shapes=[pltpu.SemaphoreType.DMA((nt,)), pltpu.SemaphoreType.DMA((nt,))],
    )(*g0, *g1)

    tr = _row_tile(cs, 64)

    def add_body(c_ref, *refs):
        for t in range(nt):
            mine = jnp.where(c_ref[0] == 0, refs[t][...], refs[nt + t][...])
            refs[3 * nt + t][...] = (mine + refs[2 * nt + t][...]).astype(BF16)

    spec = pl.BlockSpec((None, tr, d), lambda s, i, c_ref: (s, i, 0))
    chip_sum = pl.pallas_call(
        add_body, name=name + "_pair_add",
        grid_spec=pltpu.PrefetchScalarGridSpec(num_scalar_prefetch=1, grid=(nsh, cs // tr), in_specs=[spec] * (3 * nt),
                                               out_specs=[spec] * nt),
        out_shape=[jax.ShapeDtypeStruct((nsh, cs, d), BF16)] * nt, compiler_params=_cp("parallel", "parallel"),
    )(c_idx, *g0, *g1, *from_pair)

    def xy_body(*refs):
        ins, outs = refs[:nt], refs[nt:2 * nt]
        send_sems, recv_sems = refs[2 * nt:]
        kx, ky, kc = _place()
        k_me = 2 * kx + ky
        sends = []
        for j, (dx, dy) in enumerate(flips):
            px, py = kx ^ dx, ky ^ dy
            for t in range(nt):
                cp = pltpu.make_async_remote_copy(src_ref=ins[t].at[2 * px + py], dst_ref=outs[t].at[k_me],
                                                  send_sem=send_sems.at[j * nt + t], recv_sem=recv_sems.at[j * nt + t],
                                                  device_id=(px, py, kc), device_id_type=MESH)
                cp.start()
                sends.append(cp)
        for j, (dx, dy) in enumerate(flips):
            px, py = kx ^ dx, ky ^ dy
            for t in range(nt):
                pltpu.make_async_remote_copy(src_ref=ins[t].at[k_me], dst_ref=outs[t].at[2 * px + py],
                                             send_sem=send_sems.at[j * nt + t], recv_sem=recv_sems.at[j * nt + t],
                                             device_id=(px, py, kc), device_id_type=MESH).wait_recv()
        for cp in sends:
            cp.wait_send()

    from_xy = pl.pallas_call(
        xy_body, name=name + "_xy", in_specs=[ANY] * nt, out_specs=[ANY] * nt,
        out_shape=[jax.ShapeDtypeStruct((nsh, cs, d), BF16)] * nt,
        scratch_shapes=[pltpu.SemaphoreType.DMA((3 * nt,)), pltpu.SemaphoreType.DMA((3 * nt,))],
    )(*chip_sum)
    from_xy = [lax.dynamic_update_slice(o, lax.dynamic_slice_in_dim(v, me, 1, axis=0), (me, 0, 0)) for o, v in zip(from_xy, chip_sum)]

    def sum_body(*refs):
        for t in range(nt):
            acc = refs[t][0].astype(F32)
            for s in range(1, nsh):
                acc = acc + refs[t][s].astype(F32)
            refs[nt + t][...] = acc

    reduced = pl.pallas_call(
        sum_body, name=name + "_xy_add", grid=(cs // tr,), in_specs=[pl.BlockSpec((nsh, tr, d), lambda i: (0, i, 0))] * nt,
        out_specs=[pl.BlockSpec((tr, d), lambda i: (i, 0))] * nt, out_shape=[jax.ShapeDtypeStruct((cs, d), F32)] * nt,
        compiler_params=_cp("parallel"),
    )(*from_xy)

    def share_body(*refs):
        ins, outs = refs[:nt], refs[nt:2 * nt]
        send_sems, recv_sems = refs[2 * nt:]
        kx, ky, kc = _place()
        sib = (kx, ky, 1 - kc)
        cps = [pltpu.make_async_remote_copy(src_ref=ins[t], dst_ref=outs[t].at[kc], send_sem=send_sems.at[t],
                                            recv_sem=recv_sems.at[t], device_id=sib, device_id_type=MESH) for t in range(nt)]
        for cp in cps:
            cp.start()
        for t in range(nt):
            pltpu.make_async_remote_copy(src_ref=ins[t], dst_ref=outs[t].at[1 - kc], send_sem=send_sems.at[t],
                                         recv_sem=recv_sems.at[t], device_id=sib, device_id_type=MESH).wait_recv()
        for cp in cps:
            cp.wait_send()

    both = pl.pallas_call(
        share_body, name=name + "_share", in_specs=[ANY] * nt, out_specs=[ANY] * nt,
        out_shape=[jax.ShapeDtypeStruct((2, cs, d), F32)] * nt,
        scratch_shapes=[pltpu.SemaphoreType.DMA((nt,)), pltpu.SemaphoreType.DMA((nt,))],
    )(*reduced)
    return [lax.dynamic_update_slice(o, v[None], (mc, 0, 0)) for o, v in zip(both, reduced)]


def add_kept_half(g, r, c_idx, *, name, out_dtype):
    n, _, h, w = g.shape
    th = _row_tile(h)

    def body(c_ref, g_ref, r_ref, o_ref):
        o_ref[...] = (g_ref[...] + r_ref[...]).astype(o_ref.dtype)

    return pl.pallas_call(
        body, name=name,
        grid_spec=pltpu.PrefetchScalarGridSpec(
            num_scalar_prefetch=1, grid=(n, h // th),
            in_specs=[pl.BlockSpec((None, None, th, w), lambda s, i, c_ref: (s, c_ref[0], i, 0)),
                      pl.BlockSpec((None, th, w), lambda s, i, c_ref: (s, i, 0))],
            out_specs=pl.BlockSpec((None, th, w), lambda s, i, c_ref: (s, i, 0))),
        out_shape=jax.ShapeDtypeStruct((n, h, w), out_dtype), compiler_params=_cp("parallel", "parallel"),
    )(c_idx, g, r)


def sum_slots(v, *, name):
    n, rows, w = v.shape
    tr = _row_tile(rows, 256)

    def body(v_ref, o_ref):
        acc = v_ref[0].astype(F32)
        for s in range(1, n):
            acc = acc + v_ref[s].astype(F32)
        o_ref[...] = acc

    return pl.pallas_call(body, name=name, grid=(rows // tr,), in_specs=[pl.BlockSpec((n, tr, w), lambda i: (0, i, 0))],
                          out_specs=pl.BlockSpec((tr, w), lambda i: (i, 0)), out_shape=jax.ShapeDtypeStruct((rows, w), F32),
                          compiler_params=_cp("parallel"))(v)


def ada_fwd(c_rows, w_ada, b_shard, *, name):
    nl, d, ncol = w_ada.shape
    rows = c_rows.shape[0]
    tn = _tile(ncol, (768, 512, 256, 128))

    def body(c_ref, w_ref, b_ref, o_ref):
        o_ref[...] = jnp.dot(jax.nn.silu(c_ref[...]), w_ref[...], precision=HI, preferred_element_type=F32) + b_ref[...]

    return pl.pallas_call(
        body, name=name, grid=(nl, ncol // tn),
        in_specs=[pl.BlockSpec((rows, d), lambda l, j: (0, 0)), pl.BlockSpec((None, d, tn), lambda l, j: (l, 0, j)),
                  pl.BlockSpec((None, 1, tn), lambda l, j: (l, 0, j))],
        out_specs=pl.BlockSpec((None, rows, tn), lambda l, j: (l, 0, j)),
        out_shape=jax.ShapeDtypeStruct((nl, rows, ncol), F32), compiler_params=_cp("parallel", "parallel"),
    )(c_rows, w_ada, b_shard)


def ada_bwd(c_rows, w_ada, dm_shard, dm_full, n_ex, *, name):
    nl, d, ncol = w_ada.shape
    rows = c_rows.shape[0]
    tn = _tile(ncol, (768, 512, 256, 128))
    nj = ncol // tn

    def body(c_ref, w_ref, dm_ref, dmf_ref, gw_ref, gb_ref, dc_ref, dact_ref):
        l, j = pl.program_id(0), pl.program_id(1)
        act, act_vjp = jax.vjp(jax.nn.silu, c_ref[...])
        gw_ref[...] = lax.dot_general(act, dm_ref[...], _TN, precision=HI, preferred_element_type=F32)
        gb_ref[...] = jnp.sum(dmf_ref[...], axis=0, keepdims=True)
        part = lax.dot_general(dm_ref[...], w_ref[...], _NT, precision=HI, preferred_element_type=F32)

        @pl.when((l == 0) & (j == 0))
        def _():
            dact_ref[...] = part

        @pl.when((l > 0) | (j > 0))
        def _():
            dact_ref[...] += part

        @pl.when((l == nl - 1) & (j == nj - 1))
        def _():
            dc, = act_vjp(dact_ref[...])
            dc_ref[...] = jnp.sum(dc[n_ex:, :], axis=0, keepdims=True)

    return pl.pallas_call(
        body, name=name, grid=(nl, nj),
        in_specs=[pl.BlockSpec((rows, d), lambda l, j: (0, 0)), pl.BlockSpec((None, d, tn), lambda l, j: (l, 0, j)),
                  pl.BlockSpec((None, rows, tn), lambda l, j: (l, 0, j)),
                  pl.BlockSpec((None, rows, dm_full.shape[-1]), lambda l, j: (l, 0, 0))],
        out_specs=[pl.BlockSpec((None, d, tn), lambda l, j: (l, 0, j)),
                   pl.BlockSpec((None, 1, dm_full.shape[-1]), lambda l, j: (l, 0, 0)),
                   pl.BlockSpec((1, d), lambda l, j: (0, 0))],
        out_shape=[jax.ShapeDtypeStruct((nl, d, ncol), F32), jax.ShapeDtypeStruct((nl, 1, dm_full.shape[-1]), F32),
                   jax.ShapeDtypeStruct((1, d), F32)],
        scratch_shapes=[pltpu.VMEM((rows, d), F32)], compiler_params=_cp("arbitrary", "arbitrary"),
    )(c_rows, w_ada, dm_shard, dm_full)


def adamw(w, g, m, v, *, name):
    shape = w.shape
    cols = shape[-1]
    rows = int(np.prod(shape[:-1])) if len(shape) > 1 else 1
    tr = _row_tile(rows, 256)

    def body(w_ref, g_ref, m_ref, v_ref, d_ref, nm_ref, nv_ref):
        gg = g_ref[...]
        nm = ADAM_B1 * m_ref[...] + (1.0 - ADAM_B1) * gg
        nv = ADAM_B2 * v_ref[...] + (1.0 - ADAM_B2) * jnp.square(gg)
        m_hat = nm / (1.0 - ADAM_B1 ** ADAM_STEP)
        v_hat = nv / (1.0 - ADAM_B2 ** ADAM_STEP)
        d_ref[...] = -ADAM_LR * (m_hat / (jnp.sqrt(v_hat) + ADAM_EPS) + ADAM_WD * w_ref[...])
        nm_ref[...] = nm
        nv_ref[...] = nv

    spec = pl.BlockSpec((tr, cols), lambda i: (i, 0))
    out = pl.pallas_call(body, name=name, grid=(rows // tr,), in_specs=[spec] * 4, out_specs=[spec] * 3,
                         out_shape=[jax.ShapeDtypeStruct((rows, cols), F32)] * 3, compiler_params=_cp("parallel"),
                         )(*[a.reshape(rows, cols) for a in (w, g, m, v)])
    return tuple(o.reshape(shape) for o in out)


def local_step(h0, target, mods, lw, wf, small, *, lc):
    nb, t, d = h0.shape
    nt, nct = t // TM, lc // TM
    s_len = t - lc
    nl = len(lw)
    nsh = wf.shape[1]
    consts = _post_consts()
    cos_b, sin_b = _rope_tables(s_len, lc, HEAD_DIM, GQA_HEADS)
    cos_m, sin_m = _rope_tables(s_len, lc, MLA_ROPE, MLA_HEADS)
    rc = functools.partial(rowcall, nb=nb, nt=nt, nct=nct)
    flat = lambda a: a.reshape(nb * t, a.shape[-1])
    unflat = lambda a: a.reshape(nb, t, a.shape[-1])
    vec = lambda a: a.reshape(1, -1)

    def norm_first(h, g, shift, scale, tag):
        n, = rc(tag + "_norm", lambda _, *a: (f_normmod(*a),), [(h, 'tok'), (vec(g), 'full'), (shift, 'mod'), (scale, 'mod')],
                [('tok', d, BF16)])
        return n

    def res_norm(h, y, gate, coef, g, shift, scale, tag):
        def fn(_, hh, yy, gt, gn, sh, sc):
            h2 = hh + coef * gt * yy
            return h2, f_normmod(h2, gn, sh, sc)

        return rc(tag + "_res_norm", fn, [(h, 'tok'), (y, 'tok'), (gate, 'mod'), (vec(g), 'full'), (shift, 'mod'), (scale, 'mod')],
                  [('tok', d, F32), ('tok', d, BF16)])

    def res_last(h, y, gate, coef, tag):
        h2, = rc(tag + "_res", lambda _, hh, yy, gt: (hh + coef * gt * yy,), [(h, 'tok'), (y, 'tok'), (gate, 'mod')], [('tok', d, F32)])
        return h2

    def res_bwd_last(dh2, y, gate, coef, tag):
        return rc(tag + "_res_bwd", lambda _, dd, yy, gt: (coef * gt * dd, jnp.sum(coef * yy * dd, axis=0, keepdims=True)),
                  [(dh2, 'tok'), (y, 'tok'), (gate, 'mod')], [('tok', d, BF16), ('mod', d)])

    def norm_bwd_first(h, g, shift, scale, dn, dres, tag):
        def fn(_, hh, gn, sh, sc, dnn, dr):
            dh, dg, dsh, dsc = jax.vjp(f_normmod, hh, gn, sh, sc)[1](dnn)
            return dh + dr, dg, dsh, dsc

        return rc(tag + "_norm_bwd", fn, [(h, 'tok'), (vec(g), 'full'), (shift, 'mod'), (scale, 'mod'), (dn, 'tok'), (dres, 'tok')],
                  [('tok', d, F32), ('full', (1, d)), ('mod', d), ('mod', d)])

    def norm_bwd_res_bwd(h, g, shift, scale, dn, dres, y_prev, gate_prev, coef_prev, tag):
        def fn(_, hh, gn, sh, sc, dnn, dr, yy, gt):
            dh, dg, dsh, dsc = jax.vjp(f_normmod, hh, gn, sh, sc)[1](dnn)
            dh = dh + dr
            return dh, dg, dsh, dsc, coef_prev * gt * dh, jnp.sum(coef_prev * yy * dh, axis=0, keepdims=True)

        return rc(tag + "_norm_bwd", fn,
                  [(h, 'tok'), (vec(g), 'full'), (shift, 'mod'), (scale, 'mod'), (dn, 'tok'), (dres, 'tok'), (y_prev, 'tok'), (gate_prev, 'mod')],
                  [('tok', d, F32), ('full', (1, d)), ('mod', d), ('mod', d), ('tok', d, BF16), ('mod', d)])

    def ffn_fwd(n, l, base, tag):
        gg, uu, act = ffn_up(flat(n), wf, l, base, name=tag + "_up")
        return unflat(ffn_down(act, wf, l, base, name=tag + "_down")), (n, gg, uu, act)

    def ffn_bwd(dy, saved, l, base, tag):
        n, gg, uu, act = saved
        dw_d = ffn_dw(act, flat(dy), nsh, name=tag + "_down_dw")
        dgg, duu = ffn_down_bwd(flat(dy), gg, uu, wf, l, base, name=tag + "_down_dx")
        dw_g = ffn_dw(dgg, flat(n), nsh, name=tag + "_gate_dw")
        dw_u = ffn_dw(duu, flat(n), nsh, name=tag + "_up_dw")
        return unflat(ffn_up_bwd(dgg, duu, wf, l, base, name=tag + "_up_dx")), [dw_g, dw_u, dw_d]

    def post_ins(p, sm, w):
        return [(p, ('tokc', MAIN_PAD, 0)), (cos_b, 'pos'), (sin_b, 'pos'), (cos_m, 'pos'), (sin_m, 'pos'),
                (vec(sm['gqa_q_norm']), 'full'), (vec(sm['gqa_k_norm']), 'full'), (vec(sm['mla_q_norm']), 'full'),
                (vec(sm['mla_kv_norm']), 'full'), (w['w_uq'], 'full'), (w['w_ukv'], 'full')] + [(c, 'full') for c in consts]

    def mix_fwd(n, sm, w, ctx_q, tag):
        p = unflat(mm_resident(flat(n), w['w_in'], name=tag + "_in"))
        parts = rc(tag + "_post", lambda _, *a: f_post(*a), post_ins(p, sm, w), [('tok', wd, BF16) for wd in POST_WIDTHS])
        aq, ak, av, bq, bk, bv, mqn, mqr, mkn, mkr, mv = parts
        bias = na_expand_bias(sm['na_rel_bias'], tag + "_bias")
        o_a, lse_a = na_fwd(aq, ak, av, bias, lc=lc, ctx_q=ctx_q, name=tag + "_na")
        o_b, lse_b = gqa_fwd(bq, bk, bv, lc=lc, ctx_q=ctx_q, name=tag + "_gqa")
        o_m, lse_m = mla_fwd(mqn, mqr, mkn, mkr, mv, lc=lc, ctx_q=ctx_q, name=tag + "_mla")
        fo = [o_a, o_b, o_m]
        ys = [unflat(mm_resident(flat(o), w[k], name=tag + "_br" + k[-1])) for o, k in zip(fo, ('w_a', 'w_b', 'w_c'))]
        gcols = [(p, ('tokc', d, MAIN_PAD // d + j)) for j in range(3)]
        y, = rc(tag + "_merge", lambda _, *a: (f_merge(*a),), gcols + [(v, 'tok') for v in ys], [('tok', d, BF16)])
        z = unflat(mm_resident(flat(y), w['w_o'], name=tag + "_out"))
        saved = (n, p, (aq, ak, av, lse_a, bias), (bq, bk, bv, lse_b), (mqn, mqr, mkn, mkr, mv, lse_m), fo, ys, y)
        return z, saved

    def mix_bwd(dz, saved, sm, w, ctx_q, tag):
        n, p, (aq, ak, av, lse_a, bias), (bq, bk, bv, lse_b), (mqn, mqr, mkn, mkr, mv, lse_m), fo, ys, y = saved
        dw_o = mm(flat(y), flat(dz), ta=True, name=tag + "_out_dw")
        dy = unflat(mm_resident(flat(dz), w['w_o'], tb=True, name=tag + "_out_dx"))
        gcols = [(p, ('tokc', d, MAIN_PAD // d + j)) for j in range(3)]

        def merge_bwd(_, ga, gb, gm, ya, yb, ym, dyy):
            dga, dgb, dgm, dya, dyb, dym = jax.vjp(f_merge, ga, gb, gm, ya, yb, ym)[1](dyy)
            return dya, dyb, dym, jnp.concatenate([dga, dgb, dgm], axis=-1)

        dya, dyb, dym, dgl = rc(tag + "_merge_bwd", merge_bwd, gcols + [(v, 'tok') for v in ys] + [(dy, 'tok')],
                                [('tok', d, BF16)] * 3 + [('tok', 3 * d, BF16)])
        dws, dos = {}, []
        for o, dyk, k in zip(fo, (dya, dyb, dym), ('w_a', 'w_b', 'w_c')):
            dws[k] = mm(flat(o), flat(dyk), ta=True, name=tag + "_br" + k[-1] + "_dw")
            dos.append(unflat(mm_resident(flat(dyk), w[k], tb=True, out_dtype=BF16, name=tag + "_br" + k[-1] + "_dx")))
        do_a, do_b, do_m = dos
        daq, dak, dav, dbias = na_bwd(aq, ak, av, bias, lse_a, do_a, lc=lc, ctx_q=ctx_q, name=tag + "_na_bwd")
        dbq, dbk, dbv = gqa_bwd(bq, bk, bv, lse_b, do_b, lc=lc, ctx_q=ctx_q, name=tag + "_gqa_bwd")
        dmqn, dmqr2, dmkn, dmkr, dmv = mla_bwd(mqn, mqr, mkn, mkr, mv, lse_m, do_m, lc=lc, ctx_q=ctx_q, name=tag + "_mla_bwd")
        d_rel = na_reduce_bias(dbias, tag + "_relb")
        cots = [daq, dak, dav, dbq, dbk, dbv, dmqn, dmqr2, dmkn, dmkr, dmv]
        ins = post_ins(p, sm, w)
        n_in = len(ins)

        def post_bwd(_, *a):
            prim, cot, dgl_v = a[:11], list(a[n_in:n_in + 11]), a[-1]
            cot[7] = cot[7][:, :LANE] + cot[7][:, LANE:]
            for j in POST_QK:
                cot[j] = cot[j] * LN2
            outs = jax.vjp(lambda pp, qn, kn, mqn, mkvn, wuq, wukv: f_post(pp, *prim[1:5], qn, kn, mqn, mkvn, wuq, wukv, *a[11:n_in]),
                           prim[0], *prim[5:11])[1](tuple(cot))
            return (jnp.concatenate([outs[0].astype(BF16), dgl_v], axis=-1),) + tuple(outs[1:])

        res = rc(tag + "_post_bwd", post_bwd, ins + [(cv, 'tok') for cv in cots] + [(dgl, 'tok')],
                 [('tok', MAIN_PAD + 3 * d, BF16), ('full', (1, HEAD_DIM)), ('full', (1, HEAD_DIM)), ('full', (1, MLA_Q_RANK)),
                  ('full', (1, MLA_KV_RANK)), ('full', w['w_uq'].shape), ('full', w['w_ukv'].shape)])
        dp, dqn, dkn, dmqn, dmkvn, dw_uq, dw_ukv = res
        dw_in = ffn_dw(flat(dp), flat(n), 4, name=tag + "_in_dw").reshape(-1, d)
        dn = unflat(mm_resident(flat(dp), w['w_in'], tb=True, name=tag + "_in_dx"))
        dsm = {'na_rel_bias': d_rel, 'gqa_q_norm': dqn.reshape(-1), 'gqa_k_norm': dkn.reshape(-1),
               'mla_q_norm': dmqn.reshape(-1), 'mla_kv_norm': dmkvn.reshape(-1)}
        dwl = {'w_in': dw_in, 'w_uq': dw_uq, 'w_ukv': dw_ukv, 'w_o': dw_o, **dws}
        return dn, dsm, dwl

    subs = [(l, kind, gain, coef) for l in range(nl)
            for kind, gain, coef in (('ffn1', 'ffn1_norm', 0.5), ('mix', 'mix_norm', 1.0), ('ffn2', 'ffn2_norm', 0.5))]
    ns = len(subs)
    sms = [{k: small[k][l] for k in SMALL_LAYER} for l in range(nl)]

    def params(k):
        l, _, gain, _ = subs[k]
        j = 3 * (k % 3)
        return small[gain][l], mods[l][j], mods[l][j + 1], mods[l][j + 2]

    def tag_of(k):
        return f"l{subs[k][0]}_{subs[k][1]}"

    h = h0
    g0, sh0, sc0, _ = params(0)
    n = norm_first(h, g0, sh0, sc0, tag_of(0))
    h_in, core_out, saved = [None] * ns, [None] * ns, [None] * ns
    for k, (l, kind, _, coef) in enumerate(subs):
        h_in[k] = h
        if kind == 'mix':
            core_out[k], saved[k] = mix_fwd(n, sms[l], lw[l], l + 1 < nl, tag_of(k))
        else:
            core_out[k], saved[k] = ffn_fwd(n, l, 0 if kind == 'ffn1' else 3, tag_of(k))
        gate = params(k)[3]
        if k + 1 < ns:
            gn, shn, scn, _ = params(k + 1)
            h, n = res_norm(h, core_out[k], gate, coef, gn, shn, scn, tag_of(k))
        else:
            h = res_last(h, core_out[k], gate, coef, tag_of(k))

    def final(is_ctx, hh, gg, tgt):
        def loss_fn(hv, gv):
            return 0.5 * jnp.sum(jnp.mean(jnp.square(_rms(hv, gv) - tgt), axis=-1))

        keep = jnp.where(is_ctx, 0.0, 1.0)
        loss, (dh, dg) = jax.value_and_grad(loss_fn, argnums=(0, 1))(hh, gg)
        return dh * keep, jnp.full((1, LANE), loss * keep, F32), dg * keep

    dh, loss, dg_final = rc("final_loss", final, [(h, 'tok'), (vec(small['final_norm']), 'full'), (target, 'lat')],
                            [('tok', d, F32), ('full', (1, LANE)), ('full', (1, d))])

    dsmall = {k: [None] * nl for k in SMALL_LAYER}
    dmods, dlw, dwf = [[None] * N_MOD for _ in range(nl)], [None] * nl, [[None] * 6 for _ in range(nl)]
    l_last, _, _, coef_last = subs[-1]
    dcore, dmods[l_last][8] = res_bwd_last(dh, core_out[-1], params(ns - 1)[3], coef_last, tag_of(ns - 1))
    for k in reversed(range(ns)):
        l, kind, gain, _ = subs[k]
        j = 3 * (k % 3)
        if kind == 'mix':
            dn, dsm, dlw[l] = mix_bwd(dcore, saved[k], sms[l], lw[l], l + 1 < nl, tag_of(k))
            for name, val in dsm.items():
                dsmall[name][l] = val
        else:
            base = 0 if kind == 'ffn1' else 3
            dn, dwf[l][base:base + 3] = ffn_bwd(dcore, saved[k], l, base, tag_of(k))
        g, shift, scale, _ = params(k)
        if k > 0:
            lp, _, _, coef_prev = subs[k - 1]
            dh, dg, dmods[l][j], dmods[l][j + 1], dcore, dmods[lp][3 * ((k - 1) % 3) + 2] = norm_bwd_res_bwd(
                h_in[k], g, shift, scale, dn, dh, core_out[k - 1], params(k - 1)[3], coef_prev, tag_of(k))
        else:
            dh, dg, dmods[l][j], dmods[l][j + 1] = norm_bwd_first(h_in[k], g, shift, scale, dn, dh, tag_of(k))
        dsmall[gain][l] = dg.reshape(d)
    dsmall = {k: jnp.stack(v) for k, v in dsmall.items()}
    dsmall['final_norm'] = dg_final.reshape(d)
    return loss, dh, dmods, dlw, dwf, dsmall


def _pack(parts, pad_rows):
    flat, where, off = [], [], 0
    for a in parts:
        n = _ceil_to(a.size, PACK_W)
        flat.append(jnp.pad(a.reshape(-1), (0, n - a.size)))
        where.append((off, n // PACK_W))
        off += n // PACK_W
    total = _ceil_to(off, pad_rows)
    if total > off:
        flat.append(jnp.zeros(((total - off) * PACK_W,), flat[0].dtype))
    return jnp.concatenate(flat).reshape(total, PACK_W), where


def _unpack(buf, where, shape):
    off, rows = where
    return buf[off:off + rows].reshape(-1)[:int(np.prod(shape))].reshape(shape)


def layer_weights(full, l):
    wi = full['w_in'][l]
    d = wi.shape[0]
    return {
        'w_in': jnp.concatenate([wi[:, :MAIN_W], jnp.zeros((d, MAIN_PAD - MAIN_W), wi.dtype), wi[:, MAIN_W:]], axis=1),
        'w_uq': _heads_to_parts(full['mla_w_uq'][l], MLA_NOPE).astype(F32),
        'w_ukv': _heads_to_parts(full['mla_w_ukv'][l], MLA_NOPE).astype(F32),
        'w_a': full['w_branch_a'][l], 'w_b': full['w_branch_b'][l], 'w_c': full['w_branch_c'][l], 'w_o': full['w_out'][l]}


def layer_grads_by_name(dlw):
    per_name = {k: [] for k, _ in BIG}
    for g in dlw:
        per_name['w_in'].append(jnp.concatenate([g['w_in'][:MAIN_W], g['w_in'][MAIN_PAD:]], axis=0))
        per_name['mla_w_uq'].append(_parts_to_heads(g['w_uq'], MLA_NOPE))
        per_name['mla_w_ukv'].append(_parts_to_heads(g['w_ukv'], MLA_NOPE))
        per_name['w_branch_a'].append(g['w_a'])
        per_name['w_branch_b'].append(g['w_b'])
        per_name['w_branch_c'].append(g['w_c'])
        per_name['w_out'].append(g['w_o'])
    return per_name


def kernel(x, c, ctx, c_ctx, w_ada, b_ada, ffn1_norm, ffn1_w_gate, ffn1_w_up, ffn1_w_down, mix_norm, w_in, na_rel_bias, gqa_q_norm, gqa_k_norm, mla_q_norm, mla_kv_norm, mla_w_uq, mla_w_ukv, w_branch_a, w_branch_b, w_branch_c, w_out, ffn2_norm, ffn2_w_gate, ffn2_w_up, ffn2_w_down, final_norm, loss_target, m_c_ctx, m_w_ada, m_b_ada, m_ffn1_norm, m_ffn1_w_gate, m_ffn1_w_up, m_ffn1_w_down, m_mix_norm, m_w_in, m_na_rel_bias, m_gqa_q_norm, m_gqa_k_norm, m_mla_q_norm, m_mla_kv_norm, m_mla_w_uq, m_mla_w_ukv, m_w_branch_a, m_w_branch_b, m_w_branch_c, m_w_out, m_ffn2_norm, m_ffn2_w_gate, m_ffn2_w_up, m_ffn2_w_down, m_final_norm, v_c_ctx, v_w_ada, v_b_ada, v_ffn1_norm, v_ffn1_w_gate, v_ffn1_w_up, v_ffn1_w_down, v_mix_norm, v_w_in, v_na_rel_bias, v_gqa_q_norm, v_gqa_k_norm, v_mla_q_norm, v_mla_kv_norm, v_mla_w_uq, v_mla_w_ukv, v_w_branch_a, v_w_branch_b, v_w_branch_c, v_w_out, v_ffn2_norm, v_ffn2_w_gate, v_ffn2_w_up, v_ffn2_w_down, v_final_norm):
    args = locals()
    wts = {k: args[k] for k in WEIGHTS}
    mom = {k: args['m_' + k] for k in WEIGHTS}
    var = {k: args['v_' + k] for k in WEIGHTS}
    nb, s_len, d = x.shape
    lc = ctx.shape[1]
    nl = w_ada.shape[0]
    nsh, ndev = 4, 8
    mx, my, mc = _place()
    sidx = 2 * mx + my
    didx = 4 * mx + 2 * my + mc
    assert d % LANE == 0 and MAIN_PAD % d == 0 and lc % TQ == 0 and s_len % TQ == 0 and s_len // GRID_W >= NA_ROWS

    wpack, wwhere = _pack([wts[k].astype(BF16) for k, _ in BIG], 32)
    wall = gather_shards(wpack.reshape(2, -1, PACK_W), name="gather_weights").reshape(nsh, -1, PACK_W)
    full = {}
    for (k, ax), wh in zip(BIG, wwhere):
        shp = wts[k].shape
        parts = jnp.stack([_unpack(wall[s], wh, shp) for s in range(nsh)])
        if ax == 1:
            full[k] = parts.transpose(1, 2, 0, 3).reshape(nl, shp[1], nsh * shp[2])
        else:
            full[k] = parts.transpose(1, 0, 2, 3).reshape(nl, nsh * shp[1], shp[2])
    lw = [layer_weights(full, l) for l in range(nl)]
    wl = jnp.stack([(wts[k].transpose(0, 2, 1) if tr else wts[k]).astype(BF16) for k, tr in zip(FFN_NAMES, FFN_TRANSPOSED)], axis=1)
    wf = gather_ffn(wl, name="gather_ffn")

    n_ex = ndev * nb
    ncol = w_ada.shape[-1]
    c_all = all_gather(c, name="gather_cond", with_c=True).reshape(n_ex, d)
    c_rows = jnp.concatenate([c_all, jnp.broadcast_to(c_ctx[None], (n_ex, d))], axis=0)
    b_shard = lax.dynamic_slice_in_dim(b_ada, sidx * ncol, ncol, axis=1)[:, None, :]
    mod_sh = ada_fwd(c_rows, w_ada, b_shard, name="ada_fwd")
    mod_all = all_gather(mod_sh, name="gather_mod", with_c=False)
    mod_all = mod_all.transpose(1, 2, 0, 3).reshape(nl, 2 * n_ex, nsh * ncol)
    mod_x = lax.dynamic_slice_in_dim(mod_all, didx * nb, nb, axis=1)
    mod_c = jnp.broadcast_to(mod_all[:, n_ex:n_ex + 1], mod_x.shape)
    mods = [[jnp.stack([mod_c[l, :, j * d:(j + 1) * d], mod_x[l, :, j * d:(j + 1) * d]], axis=1)[:, :, None, :]
             for j in range(N_MOD)] for l in range(nl)]

    small = {k: wts[k] for k in SMALL_LAYER + ['final_norm']}
    h0 = jnp.concatenate([ctx, x], axis=1)
    loss_part, dh0, dmods, dlw, dwf, dsmall = local_step(h0, loss_target, mods, lw, wf, small, lc=lc)
    grad_x = dh0[:, lc:]

    dmod_mine = jnp.stack([jnp.concatenate([m[:, :, 0, :] for m in dmods[l]], axis=-1) for l in range(nl)])
    small_names = SMALL_LAYER + ['final_norm']
    spack, swhere = _pack([loss_part] + [dsmall[k] for k in small_names] + [dmod_mine], 8)
    sall = all_gather(spack, name="gather_small", with_c=True)
    ssum = sum_slots(sall, name="sum_small")
    loss = _unpack(ssum, swhere[0], (1, LANE))[0, 0]
    grads = {k: _unpack(ssum, wh, wts[k].shape) for k, wh in zip(small_names, swhere[1:])}
    off, rows = swhere[-1]
    dm_all = sall[:, off:off + rows].reshape(ndev, -1)[:, :dmod_mine.size].reshape((ndev,) + dmod_mine.shape)
    dm_all = dm_all.transpose(1, 3, 0, 2, 4).reshape(nl, 2, n_ex, N_MOD * d)
    dm_rows = jnp.concatenate([dm_all[:, 1], dm_all[:, 0]], axis=1)
    dm_shard = lax.dynamic_slice_in_dim(dm_rows, sidx * ncol, ncol, axis=2)
    grads['w_ada'], gb, dc_part = ada_bwd(c_rows, w_ada, dm_shard, dm_rows, n_ex, name="ada_bwd")
    grads['b_ada'] = gb.reshape(b_ada.shape)
    dc_all = all_gather(jnp.pad(dc_part, ((0, 7), (0, 0))), name="gather_dcond", with_c=False)
    grads['c_ctx'] = sum_slots(dc_all, name="sum_dcond")[0]

    per_name = layer_grads_by_name(dlw)
    pieces, gwhere, off = [], [], 0
    for k, ax in BIG:
        shp = wts[k].shape
        for g in per_name[k]:
            if ax == 1 and k not in GRAD_TRANSPOSED:
                pieces.append(g.reshape(shp[1], nsh, shp[2]).transpose(1, 0, 2).reshape(nsh, -1))
            else:
                pieces.append(g.reshape(nsh, -1))
        n = int(np.prod(shp))
        if n % PACK_W:
            pieces.append(jnp.zeros((nsh, _ceil_to(n, PACK_W) - n), F32))
        gwhere.append((off, _ceil_to(n, PACK_W) // PACK_W))
        off += _ceil_to(n, PACK_W) // PACK_W
    if off % 128:
        pieces.append(jnp.zeros((nsh, (_ceil_to(off, 128) - off) * PACK_W), F32))
    half = _ceil_to(off, 128) // 2
    gpack = jnp.concatenate(pieces, axis=1).reshape(nsh, 2, half, PACK_W)
    from_pair = pair_exchange_halves(gpack, name="reduce_pair")
    chip_sum = add_kept_half(gpack, from_pair, jnp.reshape(mc, (1,)).astype(jnp.int32), name="reduce_pair_add",
                             out_dtype=BF16)
    from_xy = all_to_all_xy(chip_sum, name="reduce_xy")
    reduced = sum_slots(from_xy, name="reduce_xy_add")
    gfull = pair_all_gather(reduced, name="reduce_share").reshape(2 * half, PACK_W)
    for (k, _), wh in zip(BIG, gwhere):
        shp = wts[k].shape
        grads[k] = (_unpack(gfull, wh, (shp[0], shp[2], shp[1])).transpose(0, 2, 1) if k in GRAD_TRANSPOSED
                    else _unpack(gfull, wh, shp))
    for k, tr, g in zip(FFN_NAMES, FFN_TRANSPOSED, reduce_ffn(dwf[0], dwf[1], name="reduce_ffn")):
        grads[k] = g.transpose(0, 2, 1) if tr else g

    outs = {k: adamw(wts[k], grads[k], mom[k], var[k], name="adamw_" + k) for k in WEIGHTS}
    return (loss, grad_x, *[grads[k] for k in WEIGHTS], *[outs[k][0] for k in WEIGHTS], *[outs[k][1] for k in WEIGHTS],
            *[outs[k][2] for k in WEIGHTS])
```

```python
import functools

import jax
import jax.numpy as jnp
import numpy as np
from jax import lax
from jax.experimental import pallas as pl
from jax.experimental.pallas import tpu as pltpu

F32 = jnp.float32
BF16 = jnp.bfloat16
HI = lax.Precision.HIGHEST
MESH = pl.DeviceIdType.MESH
ANY = pl.BlockSpec(memory_space=pl.ANY)

V7X_VMEM_BYTES = 64 * 1024 * 1024
VMEM_LIMIT = V7X_VMEM_BYTES - 8 * 1024 * 1024
LANE = 128
PACK_W = 1024

GRID_W = 64
HEAD_DIM = 64
NA_HEADS, NA_ROWS, NA_COLS = 4, 8, 16
GQA_HEADS, GQA_KV_HEADS = 8, 2
MLA_HEADS, MLA_Q_RANK, MLA_KV_RANK, MLA_NOPE, MLA_ROPE, MLA_V = 4, 256, 128, 64, 32, 64
N_MOD = 9
ROPE_THETA = 10000.0
EPS = 1e-6
NEG_BIG = -1e30
NA_W = NA_HEADS * HEAD_DIM
GQ_W = GQA_HEADS * HEAD_DIM
GK_W = GQA_KV_HEADS * HEAD_DIM
MAIN_W = 3 * NA_W + GQ_W + 2 * GK_W + MLA_Q_RANK + MLA_KV_RANK + MLA_ROPE
MAIN_PAD = 2048
LOG2E, LN2 = float(np.log2(np.e)), float(np.log(2.0))
Q_SCALE = HEAD_DIM ** -0.5 * LOG2E
MLA_Q_SCALE = (MLA_NOPE + MLA_ROPE) ** -0.5 * LOG2E
TQ = 256
TM = 256

ADAM_LR, ADAM_B1, ADAM_B2, ADAM_EPS, ADAM_WD, ADAM_STEP = 0.001, 0.9, 0.999, 1e-08, 0.01, 10

ARG_NAMES = ['x', 'c', 'ctx', 'c_ctx', 'w_ada', 'b_ada', 'ffn1_norm', 'ffn1_w_gate', 'ffn1_w_up', 'ffn1_w_down', 'mix_norm', 'w_in',
             'na_rel_bias', 'gqa_q_norm', 'gqa_k_norm', 'mla_q_norm', 'mla_kv_norm', 'mla_w_uq', 'mla_w_ukv', 'w_branch_a',
             'w_branch_b', 'w_branch_c', 'w_out', 'ffn2_norm', 'ffn2_w_gate', 'ffn2_w_up', 'ffn2_w_down', 'final_norm']
WEIGHTS = ARG_NAMES[3:]
BIG = [('w_in', 1), ('mla_w_uq', 1), ('mla_w_ukv', 1), ('w_branch_a', 1), ('w_branch_b', 1), ('w_branch_c', 1), ('w_out', 0)]
GRAD_TRANSPOSED = ('w_in',)
FFN_NAMES = ['ffn1_w_gate', 'ffn1_w_up', 'ffn1_w_down', 'ffn2_w_gate', 'ffn2_w_up', 'ffn2_w_down']
FFN_TRANSPOSED = [True, True, False, True, True, False]
SMALL_LAYER = ['ffn1_norm', 'mix_norm', 'na_rel_bias', 'gqa_q_norm', 'gqa_k_norm', 'mla_q_norm', 'mla_kv_norm', 'ffn2_norm']


def _cp(*sem):
    return pltpu.CompilerParams(dimension_semantics=sem, vmem_limit_bytes=VMEM_LIMIT)


def _tile(dim, cands):
    for t in cands:
        if dim % t == 0:
            return t
    return dim


def _row_tile(rows, cap=512, mult=16):
    best = None
    for t in range(mult, min(rows, cap) + 1, mult):
        if rows % t == 0:
            best = t
    return best or rows


def _ceil_to(n, m):
    return -(-n // m) * m


def mm(a, b, *, name, ta=False, tb=False, out_dtype=F32, precise=False):
    m, k = (a.shape[1], a.shape[0]) if ta else a.shape
    n = b.shape[0] if tb else b.shape[1]
    tm = _tile(m, (512, 256, 128))
    tn = _tile(n, (1024, 1408, 512, 256, 128))
    tk = _tile(k, (1024, 1408, 512, 256, 128))
    nk = k // tk
    dims = (((0 if ta else 1,), (1 if tb else 0,)), ((), ()))

    def body(a_ref, b_ref, o_ref, *acc):
        if precise:
            part = lax.dot_general(a_ref[...].astype(F32), b_ref[...].astype(F32), dims, precision=HI, preferred_element_type=F32)
        else:
            part = lax.dot_general(a_ref[...].astype(BF16), b_ref[...].astype(BF16), dims, preferred_element_type=F32)
        if nk == 1:
            o_ref[...] = part.astype(o_ref.dtype)
        else:
            acc_ref, = acc
            kk = pl.program_id(2)

            @pl.when(kk == 0)
            def _():
                acc_ref[...] = part

            @pl.when(kk > 0)
            def _():
                acc_ref[...] += part

            @pl.when(kk == nk - 1)
            def _():
                o_ref[...] = acc_ref[...].astype(o_ref.dtype)

    a_spec = pl.BlockSpec((tk, tm), lambda i, j, kk: (kk, i)) if ta else pl.BlockSpec((tm, tk), lambda i, j, kk: (i, kk))
    b_spec = pl.BlockSpec((tn, tk), lambda i, j, kk: (j, kk)) if tb else pl.BlockSpec((tk, tn), lambda i, j, kk: (kk, j))
    return pl.pallas_call(
        body, name=name, grid=(m // tm, n // tn, nk), in_specs=[a_spec, b_spec],
        out_specs=pl.BlockSpec((tm, tn), lambda i, j, kk: (i, j)),
        out_shape=jax.ShapeDtypeStruct((m, n), out_dtype),
        scratch_shapes=[pltpu.VMEM((tm, tn), F32)] if nk > 1 else [],
        compiler_params=_cp("parallel", "parallel", "arbitrary"),
    )(a, b)


def mm_resident(a, w, *, name, tb=False, out_dtype=F32):
    m, k = a.shape
    n = w.shape[0] if tb else w.shape[1]
    tm = _tile(m, (512, 256, 128))
    cn = n if tb else _tile(n, (1024, 512, 256, 128))

    def body(a_ref, w_ref, o_ref):
        aa = a_ref[...].astype(BF16)
        if tb:
            o_ref[...] = _dot(aa, w_ref[...], _NT).astype(o_ref.dtype)
        else:
            for c in range(n // cn):
                cols = slice(cn * c, cn * (c + 1))
                o_ref[:, cols] = _dot(aa, w_ref[:, cols]).astype(o_ref.dtype)

    return pl.pallas_call(
        body, name=name, grid=(m // tm,),
        in_specs=[pl.BlockSpec((tm, k), lambda i: (i, 0)), pl.BlockSpec(w.shape, lambda i: (0, 0), pipeline_mode=pl.Buffered(1))],
        out_specs=pl.BlockSpec((tm, n), lambda i: (i, 0)), out_shape=jax.ShapeDtypeStruct((m, n), out_dtype),
        compiler_params=_cp("parallel"),
    )(a, w)


FFN_GATE, FFN_UP, FFN_DOWN = 0, 1, 2


def _ffn_wspec(wf, l, which):
    _, nsh, _, cs, d = wf.shape
    return pl.BlockSpec((None, nsh, None, cs, d), lambda *_: (l, 0, which, 0, 0), pipeline_mode=pl.Buffered(1))


def _ffn_group(cs):
    for g in (1, 2, 4):
        if (g * cs) % LANE == 0:
            return g
    raise ValueError(cs)


def ffn_up(n, wf, l, base, *, name):
    m, d = n.shape
    nsh, cs = wf.shape[1], wf.shape[3]
    f = nsh * cs
    grp = _ffn_group(cs)
    tm = _tile(m, (512, 256, 128))

    def body(n_ref, wg_ref, wu_ref, g_ref, u_ref, a_ref):
        nn = n_ref[...]
        for c in range(nsh // grp):
            cols = slice(grp * cs * c, grp * cs * (c + 1))
            g = _dot(nn, wg_ref[grp * c:grp * (c + 1)].reshape(grp * cs, d), _NT)
            u = _dot(nn, wu_ref[grp * c:grp * (c + 1)].reshape(grp * cs, d), _NT)
            g_ref[:, cols] = g.astype(BF16)
            u_ref[:, cols] = u.astype(BF16)
            a_ref[:, cols] = f_act_gu(g, u).astype(BF16)

    ospec = pl.BlockSpec((tm, f), lambda i: (i, 0))
    return pl.pallas_call(
        body, name=name, grid=(m // tm,),
        in_specs=[pl.BlockSpec((tm, d), lambda i: (i, 0)), _ffn_wspec(wf, l, base + FFN_GATE), _ffn_wspec(wf, l, base + FFN_UP)],
        out_specs=[ospec] * 3, out_shape=[jax.ShapeDtypeStruct((m, f), BF16)] * 3, compiler_params=_cp("parallel"),
    )(n, wf, wf)


def ffn_down(act, wf, l, base, *, name):
    m, f = act.shape
    nsh, cs, d = wf.shape[1], wf.shape[3], wf.shape[4]
    tm = _tile(m, (512, 256, 128))

    def body(a_ref, wd_ref, y_ref):
        y_ref[...] = _dot(a_ref[...], wd_ref[...].reshape(f, d))

    return pl.pallas_call(
        body, name=name, grid=(m // tm,),
        in_specs=[pl.BlockSpec((tm, f), lambda i: (i, 0)), _ffn_wspec(wf, l, base + FFN_DOWN)],
        out_specs=pl.BlockSpec((tm, d), lambda i: (i, 0)), out_shape=jax.ShapeDtypeStruct((m, d), F32), compiler_params=_cp("parallel"),
    )(act, wf)


def ffn_down_bwd(dy, g, u, wf, l, base, *, name):
    m, d = dy.shape
    nsh, cs = wf.shape[1], wf.shape[3]
    f = nsh * cs
    grp = _ffn_group(cs)
    tm = _tile(m, (512, 256, 128))

    def body(dy_ref, g_ref, u_ref, wd_ref, dg_ref, du_ref):
        dd = dy_ref[...]
        for c in range(nsh // grp):
            cols = slice(grp * cs * c, grp * cs * (c + 1))
            dact = _dot(dd, wd_ref[grp * c:grp * (c + 1)].reshape(grp * cs, d), _NT)
            dg, du = jax.vjp(f_act_gu, g_ref[:, cols].astype(F32), u_ref[:, cols].astype(F32))[1](dact)
            dg_ref[:, cols] = dg.astype(BF16)
            du_ref[:, cols] = du.astype(BF16)

    fspec = pl.BlockSpec((tm, f), lambda i: (i, 0))
    return pl.pallas_call(
        body, name=name, grid=(m // tm,),
        in_specs=[pl.BlockSpec((tm, d), lambda i: (i, 0)), fspec, fspec, _ffn_wspec(wf, l, base + FFN_DOWN)],
        out_specs=[fspec] * 2, out_shape=[jax.ShapeDtypeStruct((m, f), BF16)] * 2, compiler_params=_cp("parallel"),
    )(dy, g, u, wf)


def ffn_up_bwd(dg, du, wf, l, base, *, name):
    m, f = dg.shape
    nsh, cs, d = wf.shape[1], wf.shape[3], wf.shape[4]
    tm = _tile(m, (512, 256, 128))

    def body(dg_ref, du_ref, wg_ref, wu_ref, dn_ref):
        dn_ref[...] = _dot(dg_ref[...], wg_ref[...].reshape(f, d)) + _dot(du_ref[...], wu_ref[...].reshape(f, d))

    fspec = pl.BlockSpec((tm, f), lambda i: (i, 0))
    return pl.pallas_call(
        body, name=name, grid=(m // tm,),
        in_specs=[fspec, fspec, _ffn_wspec(wf, l, base + FFN_GATE), _ffn_wspec(wf, l, base + FFN_UP)],
        out_specs=pl.BlockSpec((tm, d), lambda i: (i, 0)), out_shape=jax.ShapeDtypeStruct((m, d), F32), compiler_params=_cp("parallel"),
    )(dg, du, wf, wf)


def ffn_dw(a, b, nsh, *, name):
    m, f = a.shape
    d = b.shape[1]
    cs = f // nsh
    grp = _ffn_group(cs)
    tm = _tile(m, (1024, 512, 256, 128))

    def body(a_ref, b_ref, o_ref):
        part = _dot(a_ref[...], b_ref[...], _TN).reshape(grp, cs, d)
        i = pl.program_id(1)

        @pl.when(i == 0)
        def _():
            o_ref[...] = part

        @pl.when(i > 0)
        def _():
            o_ref[...] += part

    return pl.pallas_call(
        body, name=name, grid=(nsh // grp, m // tm),
        in_specs=[pl.BlockSpec((tm, grp * cs), lambda j, i: (i, j)), pl.BlockSpec((tm, d), lambda j, i: (i, 0))],
        out_specs=pl.BlockSpec((grp, cs, d), lambda j, i: (j, 0, 0)), out_shape=jax.ShapeDtypeStruct((nsh, cs, d), F32),
        compiler_params=_cp("parallel", "arbitrary"),
    )(a, b)


def rowcall(name, fn, ins, outs, *, nb, nt, nct):
    in_specs, arrays = [], []
    for arr, kind in ins:
        arrays.append(arr)
        if kind == 'tok':
            in_specs.append(pl.BlockSpec((None, TM, arr.shape[-1]), lambda b, t: (b, t, 0)))
        elif kind == 'lat':
            in_specs.append(pl.BlockSpec((None, TM, arr.shape[-1]), lambda b, t: (b, jnp.maximum(t - nct, 0), 0)))
        elif kind == 'pos':
            in_specs.append(pl.BlockSpec((TM, arr.shape[-1]), lambda b, t: (t, 0)))
        elif kind == 'mod':
            in_specs.append(pl.BlockSpec((None, None, 1, arr.shape[-1]), lambda b, t: (b, jnp.where(t >= nct, 1, 0), 0, 0)))
        elif kind == 'full':
            in_specs.append(pl.BlockSpec(arr.shape, lambda b, t, nd=arr.ndim: (0,) * nd))
        else:
            _, w, j = kind
            in_specs.append(pl.BlockSpec((None, TM, w), lambda b, t, j=j: (b, t, j)))
    out_specs, out_shape = [], []
    for o in outs:
        if o[0] == 'tok':
            out_specs.append(pl.BlockSpec((None, TM, o[1]), lambda b, t: (b, t, 0)))
            out_shape.append(jax.ShapeDtypeStruct((nb, nt * TM, o[1]), o[2]))
        elif o[0] == 'mod':
            out_specs.append(pl.BlockSpec((None, None, 1, o[1]), lambda b, t: (b, jnp.where(t >= nct, 1, 0), 0, 0)))
            out_shape.append(jax.ShapeDtypeStruct((nb, 2, 1, o[1]), F32))
        else:
            out_specs.append(pl.BlockSpec(o[1], lambda b, t, nd=len(o[1]): (0,) * nd))
            out_shape.append(jax.ShapeDtypeStruct(o[1], F32))
    n_in = len(ins)

    def body(*refs):
        b, t = pl.program_id(0), pl.program_id(1)
        res = fn(t < nct, *[r[...] for r in refs[:n_in]])
        for ref, o, val in zip(refs[n_in:], outs, res, strict=True):
            if o[0] == 'tok':
                ref[...] = val.astype(ref.dtype)
                continue
            first = ((t == 0) | (t == nct)) if o[0] == 'mod' else ((b == 0) & (t == 0))

            @pl.when(first)
            def _(ref=ref, val=val):
                ref[...] = val

            @pl.when(jnp.logical_not(first))
            def _(ref=ref, val=val):
                ref[...] += val

    return pl.pallas_call(body, name=name, grid=(nb, nt), in_specs=in_specs, out_specs=out_specs, out_shape=out_shape,
                          compiler_params=_cp("arbitrary", "arbitrary"))(*arrays)


def _rms(x, g):
    return x * lax.rsqrt(jnp.mean(x * x, axis=-1, keepdims=True) + EPS) * g


def f_normmod(h, g, shift, scale):
    return _rms(h, g) * (1.0 + scale) + shift


def f_act_gu(g, u):
    return jax.nn.silu(g) * u


def _dot_split(x, m, dims):
    hi = x.astype(BF16)
    lo = (x - hi.astype(F32)).astype(BF16)
    mb = m.astype(BF16)
    return (lax.dot_general(hi, mb, dims, preferred_element_type=F32) + lax.dot_general(lo, mb, dims, preferred_element_type=F32))


def dot_select(x, m):
    return _dot_select(x, m)


@jax.custom_vjp
def _dot_select(x, m):
    return _dot_split(x, m, (((1,), (0,)), ((), ())))


_dot_select.defvjp(lambda x, m: (_dot_split(x, m, (((1,), (0,)), ((), ()))), m),
                   lambda m, ct: (_dot_split(ct, m, (((1,), (1,)), ((), ()))), jnp.zeros_like(m)))


def f_merge(ga, gb, gm, ya, yb, ym):
    return jax.nn.sigmoid(ga) * ya + jax.nn.sigmoid(gb) * yb + jax.nn.sigmoid(gm) * ym


def f_post(p, cb, sb, cm, sm, qn, kn, mqn, mkvn, wuq, wukv, s_b, r_b, t_b, r_m, rep, dup):
    def hnorm(x, g, w):
        ms = dot_select(x * x, s_b[:w, :w])
        gw = dot_select(g, t_b[:, :w])
        return x * lax.rsqrt(ms + EPS) * gw

    def rope(x, cos, sin, rot):
        return x * cos + dot_select(x, rot) * sin

    o = 3 * NA_W
    a_q, a_k, a_v = p[:, 0:NA_W], p[:, NA_W:2 * NA_W], p[:, 2 * NA_W:o]
    b_q = rope(hnorm(p[:, o:o + GQ_W], qn, GQ_W), cb, sb, r_b)
    o += GQ_W
    b_k = rope(hnorm(p[:, o:o + GK_W], kn, GK_W), cb[:, :GK_W], sb[:, :GK_W], r_b[:GK_W, :GK_W])
    b_v = p[:, o + GK_W:o + 2 * GK_W]
    o += 2 * GK_W
    q_lat = jnp.dot(_rms(p[:, o:o + MLA_Q_RANK], mqn).astype(BF16), wuq.astype(BF16), preferred_element_type=F32)
    o += MLA_Q_RANK
    kv_lat = jnp.dot(_rms(p[:, o:o + MLA_KV_RANK], mkvn).astype(BF16), wukv.astype(BF16), preferred_element_type=F32)
    o += MLA_KV_RANK
    nw = MLA_HEADS * MLA_NOPE
    mq_nope, mq_rope = q_lat[:, :nw], rope(q_lat[:, nw:], cm, sm, r_m)
    mk_nope, m_v = kv_lat[:, :nw], kv_lat[:, nw:]
    mk_rope = dot_select(rope(p[:, o:o + LANE], cm, sm, r_m), rep)
    b_k2 = dot_select(b_k, dup)
    b_v2 = dot_select(b_v, dup)
    return (a_q * Q_SCALE, a_k, a_v, b_q * Q_SCALE, b_k2, b_v2, mq_nope * MLA_Q_SCALE, mq_rope * MLA_Q_SCALE, mk_nope, mk_rope, m_v)


POST_QK = (0, 1, 3, 4, 6, 7, 8, 9)


POST_WIDTHS = (NA_W, NA_W, NA_W, GQ_W, 2 * GK_W, 2 * GK_W, MLA_HEADS * MLA_NOPE, MLA_HEADS * MLA_ROPE, MLA_HEADS * MLA_NOPE,
               MLA_HEADS * MLA_ROPE, MLA_HEADS * MLA_V)


_NT = (((1,), (1,)), ((), ()))
_TN = (((0,), (0,)), ((), ()))


def _dot(a, b, dims=None):
    if dims is None:
        return jnp.dot(a, b, preferred_element_type=F32)
    return lax.dot_general(a, b, dims, preferred_element_type=F32)


def _lanes(lo, width):
    lane = lax.broadcasted_iota(jnp.int32, (1, LANE), 1)
    return (lane >= lo) & (lane < lo + width)


def _only(x, mask):
    return jnp.where(mask, x, jnp.zeros_like(x))


def _stack_pair(x, width, lo):
    return jnp.concatenate([_only(x, _lanes(lo, width)), _only(x, _lanes(lo + width, width))], axis=0)


def _pair_softmax(s):
    m = jnp.max(s, axis=-1, keepdims=True)
    p = jnp.exp2(s - m)
    l = jnp.sum(p, axis=-1, keepdims=True)
    return p, l, m + jnp.log2(l)


def gqa_fwd(q, k2, v2, *, lc, ctx_q, name):
    nb, t, qw = q.shape
    npair = qw // LANE
    per_kv = npair // GQA_KV_HEADS
    nctb = lc // TQ

    def body(q_ref, k_ref, v_ref, o_ref, lse_ref):
        i = pl.program_id(2)

        def run(rows):
            kk, vv = k_ref[rows, :], v_ref[rows, :]
            outs = []
            for e in range(2):
                p, l, lse = _pair_softmax(_dot(_only(q_ref[...], _lanes(HEAD_DIM * e, HEAD_DIM)), kk, _NT))
                outs.append(_dot(p.astype(BF16), vv) / l)
                lse_ref[e] = lse
            o_ref[...] = jnp.where(_lanes(0, HEAD_DIM), outs[0], outs[1]).astype(o_ref.dtype)

        @pl.when(i < nctb)
        def _():
            if ctx_q:
                run(pl.ds(0, lc))
            else:
                o_ref[...] = jnp.zeros_like(o_ref)
                lse_ref[...] = jnp.zeros_like(lse_ref)

        @pl.when(i >= nctb)
        def _():
            run(pl.ds(0, t))

    qmap = lambda b, p, i: (b, i, p)
    kmap = lambda b, p, i: (b, 0, p // per_kv)
    return pl.pallas_call(
        body, name=name, grid=(nb, npair, t // TQ),
        in_specs=[pl.BlockSpec((None, TQ, LANE), qmap), pl.BlockSpec((None, t, LANE), kmap), pl.BlockSpec((None, t, LANE), kmap)],
        out_specs=[pl.BlockSpec((None, TQ, LANE), qmap), pl.BlockSpec((None, 2, TQ, 1), lambda b, p, i: (b, p, i, 0))],
        out_shape=[jax.ShapeDtypeStruct((nb, t, qw), BF16), jax.ShapeDtypeStruct((nb, 2 * npair, t, 1), F32)],
        compiler_params=_cp("parallel", "parallel", "arbitrary"),
    )(q, k2, v2)


def gqa_bwd(q, k2, v2, lse, do, *, lc, ctx_q, name):
    nb, t, qw = q.shape
    npair = qw // LANE
    per_kv = npair // GQA_KV_HEADS
    nctb = lc // TQ

    def body(q_ref, k_ref, v_ref, lse_ref, do_ref, dq_ref, dk_ref, dv_ref):
        g, i = pl.program_id(2), pl.program_id(3)

        @pl.when((g == 0) & (i == 0))
        def _():
            dk_ref[...] = jnp.zeros_like(dk_ref)
            dv_ref[...] = jnp.zeros_like(dv_ref)

        def run(rows):
            kk, vv = k_ref[rows, :], v_ref[rows, :]
            qq, dd = _stack_pair(q_ref[...], HEAD_DIM, 0), _stack_pair(do_ref[...], HEAD_DIM, 0)
            p = jnp.exp2(_dot(qq, kk, _NT) - jnp.concatenate([lse_ref[0], lse_ref[1]], axis=0))
            dp = _dot(dd, vv, _NT)
            delta = jnp.sum(p * dp, axis=-1, keepdims=True)
            ds = (p * (dp - delta)).astype(BF16)
            dq = _dot(ds, kk)
            dq_ref[...] = jnp.where(_lanes(0, HEAD_DIM), dq[:TQ], dq[TQ:])
            dk_ref[rows, :] += _dot(ds, qq, _TN)
            dv_ref[rows, :] += _dot(p.astype(BF16), dd, _TN)

        @pl.when(i < nctb)
        def _():
            if ctx_q:
                run(pl.ds(0, lc))
            else:
                dq_ref[...] = jnp.zeros_like(dq_ref)

        @pl.when(i >= nctb)
        def _():
            run(pl.ds(0, t))

    qmap = lambda b, j, g, i: (b, i, j * per_kv + g)
    kmap = lambda b, j, g, i: (b, 0, j)
    return pl.pallas_call(
        body, name=name, grid=(nb, GQA_KV_HEADS, per_kv, t // TQ),
        in_specs=[pl.BlockSpec((None, TQ, LANE), qmap), pl.BlockSpec((None, t, LANE), kmap), pl.BlockSpec((None, t, LANE), kmap),
                  pl.BlockSpec((None, 2, TQ, 1), lambda b, j, g, i: (b, j * per_kv + g, i, 0)), pl.BlockSpec((None, TQ, LANE), qmap)],
        out_specs=[pl.BlockSpec((None, TQ, LANE), qmap), pl.BlockSpec((None, t, LANE), kmap), pl.BlockSpec((None, t, LANE), kmap)],
        out_shape=[jax.ShapeDtypeStruct((nb, t, qw), F32), jax.ShapeDtypeStruct(k2.shape, F32), jax.ShapeDtypeStruct(v2.shape, F32)],
        compiler_params=_cp("arbitrary", "arbitrary", "arbitrary", "arbitrary"),
    )(q, k2, v2, lse, do)


def mla_fwd(qn, qr, kn, kr, v, *, lc, ctx_q, name):
    nb, t, w = qn.shape
    npair = w // LANE
    nctb = lc // TQ

    def body(qn_ref, qr_ref, kn_ref, kr_ref, v_ref, o_ref, lse_ref):
        pr, i = pl.program_id(1), pl.program_id(2)

        def run(rows):
            kk, kkr, vv = kn_ref[rows, :], kr_ref[rows, :], v_ref[rows, :]
            outs = []
            for e in range(2):
                s = (_dot(_only(qn_ref[...], _lanes(MLA_NOPE * e, MLA_NOPE)), kk, _NT)
                     + _dot(_only(qr_ref[...], _lanes(MLA_ROPE * (2 * pr + e), MLA_ROPE)), kkr, _NT))
                p, l, lse = _pair_softmax(s)
                outs.append(_dot(p.astype(BF16), vv) / l)
                lse_ref[e] = lse
            o_ref[...] = jnp.where(_lanes(0, MLA_V), outs[0], outs[1]).astype(o_ref.dtype)

        @pl.when(i < nctb)
        def _():
            if ctx_q:
                run(pl.ds(0, lc))
            else:
                o_ref[...] = jnp.zeros_like(o_ref)
                lse_ref[...] = jnp.zeros_like(lse_ref)

        @pl.when(i >= nctb)
        def _():
            run(pl.ds(0, t))

    qmap = lambda b, p, i: (b, i, p)
    rmap = lambda b, p, i: (b, i, 0)
    kmap = lambda b, p, i: (b, 0, p)
    return pl.pallas_call(
        body, name=name, grid=(nb, npair, t // TQ),
        in_specs=[pl.BlockSpec((None, TQ, LANE), qmap), pl.BlockSpec((None, TQ, LANE), rmap), pl.BlockSpec((None, t, LANE), kmap),
                  pl.BlockSpec((None, t, LANE), lambda b, p, i: (b, 0, 0)), pl.BlockSpec((None, t, LANE), kmap)],
        out_specs=[pl.BlockSpec((None, TQ, LANE), qmap), pl.BlockSpec((None, 2, TQ, 1), lambda b, p, i: (b, p, i, 0))],
        out_shape=[jax.ShapeDtypeStruct((nb, t, w), BF16), jax.ShapeDtypeStruct((nb, 2 * npair, t, 1), F32)],
        compiler_params=_cp("parallel", "parallel", "arbitrary"),
    )(qn, qr, kn, kr, v)


def mla_bwd(qn, qr, kn, kr, v, lse, do, *, lc, ctx_q, name):
    nb, t, w = qn.shape
    npair = w // LANE
    nctb = lc // TQ

    def body(qn_ref, qr_ref, kn_ref, kr_ref, v_ref, lse_ref, do_ref, dqn_ref, dqr_ref, dkn_ref, dkr_ref, dv_ref):
        pr, i = pl.program_id(1), pl.program_id(2)

        @pl.when(i == 0)
        def _():
            dkn_ref[...] = jnp.zeros_like(dkn_ref)
            dv_ref[...] = jnp.zeros_like(dv_ref)

        @pl.when((i == 0) & (pr == 0))
        def _():
            dkr_ref[...] = jnp.zeros_like(dkr_ref)

        def run(rows):
            kk, kkr, vv = kn_ref[rows, :], kr_ref[rows, :], v_ref[rows, :]
            r_lo = 2 * MLA_ROPE * pr
            qq, qqr = _stack_pair(qn_ref[...], MLA_NOPE, 0), _stack_pair(qr_ref[...], MLA_ROPE, r_lo)
            dd = _stack_pair(do_ref[...], MLA_V, 0)
            p = jnp.exp2(_dot(qq, kk, _NT) + _dot(qqr, kkr, _NT) - jnp.concatenate([lse_ref[0], lse_ref[1]], axis=0))
            dp = _dot(dd, vv, _NT)
            delta = jnp.sum(p * dp, axis=-1, keepdims=True)
            ds = (p * (dp - delta)).astype(BF16)
            dqn, dqr = _dot(ds, kk), _dot(ds, kkr)
            dqn_ref[...] = jnp.where(_lanes(0, MLA_NOPE), dqn[:TQ], dqn[TQ:])
            dqr_ref[...] = _only(dqr[:TQ], _lanes(r_lo, MLA_ROPE)) + _only(dqr[TQ:], _lanes(r_lo + MLA_ROPE, MLA_ROPE))
            dkn_ref[rows, :] += _dot(ds, qq, _TN)
            dkr_ref[rows, :] += _dot(ds, qqr, _TN)
            dv_ref[rows, :] += _dot(p.astype(BF16), dd, _TN)

        @pl.when(i < nctb)
        def _():
            if ctx_q:
                run(pl.ds(0, lc))
            else:
                dqn_ref[...] = jnp.zeros_like(dqn_ref)
                dqr_ref[...] = jnp.zeros_like(dqr_ref)

        @pl.when(i >= nctb)
        def _():
            run(pl.ds(0, t))

    qmap = lambda b, p, i: (b, i, p)
    rmap = lambda b, p, i: (b, i, 0)
    kmap = lambda b, p, i: (b, 0, p)
    zmap = lambda b, p, i: (b, 0, 0)
    return pl.pallas_call(
        body, name=name, grid=(nb, npair, t // TQ),
        in_specs=[pl.BlockSpec((None, TQ, LANE), qmap), pl.BlockSpec((None, TQ, LANE), rmap), pl.BlockSpec((None, t, LANE), kmap),
                  pl.BlockSpec((None, t, LANE), zmap), pl.BlockSpec((None, t, LANE), kmap),
                  pl.BlockSpec((None, 2, TQ, 1), lambda b, p, i: (b, p, i, 0)), pl.BlockSpec((None, TQ, LANE), qmap)],
        out_specs=[pl.BlockSpec((None, TQ, LANE), qmap), pl.BlockSpec((None, TQ, LANE), qmap), pl.BlockSpec((None, t, LANE), kmap),
                   pl.BlockSpec((None, t, LANE), zmap), pl.BlockSpec((None, t, LANE), kmap)],
        out_shape=[jax.ShapeDtypeStruct((nb, t, w), F32), jax.ShapeDtypeStruct((nb, t, npair * LANE), F32),
                   jax.ShapeDtypeStruct((nb, t, w), F32), jax.ShapeDtypeStruct((nb, t, LANE), F32), jax.ShapeDtypeStruct((nb, t, w), F32)],
        compiler_params=_cp("arbitrary", "arbitrary", "arbitrary"),
    )(qn, qr, kn, kr, v, lse, do)


def _na_window(st, nc, rows):
    r = jnp.maximum(st - nc, 0)
    r0 = jnp.clip(r - NA_ROWS // 2, 0, rows - NA_ROWS)
    return r, r0, r - r0


def na_fwd(q, k, v, bias, *, lc, ctx_q, name):
    nb, t, w = q.shape
    npair = w // LANE
    nc, rows = lc // GRID_W, (t - lc) // GRID_W
    nwin = NA_ROWS * GRID_W

    def body(q_ref, k_ref, v_ref, bias_ref, o_ref, lse_ref):
        st = pl.program_id(1)
        ctx = pl.ds(0, lc)

        @pl.when(st < nc)
        def _():
            if not ctx_q:
                o_ref[...] = jnp.zeros_like(o_ref)
                lse_ref[...] = jnp.zeros_like(lse_ref)
                return
            for pr in range(npair):
                lanes = slice(LANE * pr, LANE * (pr + 1))
                kc, vc = k_ref[ctx, lanes], v_ref[ctx, lanes]
                outs = []
                for e in range(2):
                    p, l, lse = _pair_softmax(_dot(_only(q_ref[:, lanes], _lanes(HEAD_DIM * e, HEAD_DIM)), kc, _NT))
                    outs.append(_dot(p.astype(BF16), vc) / l)
                    lse_ref[2 * pr + e] = lse
                o_ref[:, lanes] = jnp.where(_lanes(0, HEAD_DIM), outs[0], outs[1]).astype(o_ref.dtype)

        @pl.when(st >= nc)
        def _():
            _, r0, _ = _na_window(st, nc, rows)
            win = pl.ds(pl.multiple_of(lc + r0 * GRID_W, GRID_W), nwin)
            for pr in range(npair):
                lanes = slice(LANE * pr, LANE * (pr + 1))
                kc, vc, kw, vw = k_ref[ctx, lanes], v_ref[ctx, lanes], k_ref[win, lanes], v_ref[win, lanes]
                qq = _stack_pair(q_ref[:, lanes], HEAD_DIM, 0)
                s_loc = _dot(qq, kw, _NT) + jnp.concatenate([bias_ref[2 * pr], bias_ref[2 * pr + 1]], axis=0) * LOG2E
                s_ctx = _dot(qq, kc, _NT)
                m = jnp.maximum(jnp.max(s_loc, axis=-1, keepdims=True), jnp.max(s_ctx, axis=-1, keepdims=True))
                p_loc, p_ctx = jnp.exp2(s_loc - m), jnp.exp2(s_ctx - m)
                l = jnp.sum(p_loc, axis=-1, keepdims=True) + jnp.sum(p_ctx, axis=-1, keepdims=True)
                o = (_dot(p_loc.astype(BF16), vw) + _dot(p_ctx.astype(BF16), vc)) / l
                lse = m + jnp.log2(l)
                lse_ref[2 * pr], lse_ref[2 * pr + 1] = lse[:GRID_W], lse[GRID_W:]
                o_ref[:, lanes] = jnp.where(_lanes(0, HEAD_DIM), o[:GRID_W], o[GRID_W:]).astype(o_ref.dtype)

    qmap = lambda b, st: (b, st, 0)
    kmap = lambda b, st: (b, 0, 0)
    return pl.pallas_call(
        body, name=name, grid=(nb, nc + rows),
        in_specs=[pl.BlockSpec((None, GRID_W, w), qmap), pl.BlockSpec((None, t, w), kmap), pl.BlockSpec((None, t, w), kmap),
                  pl.BlockSpec((2 * npair, None, GRID_W, nwin), lambda b, st: (0, _na_window(st, nc, rows)[2], 0, 0))],
        out_specs=[pl.BlockSpec((None, GRID_W, w), qmap), pl.BlockSpec((None, 2 * npair, GRID_W, 1), lambda b, st: (b, 0, st, 0))],
        out_shape=[jax.ShapeDtypeStruct((nb, t, w), BF16), jax.ShapeDtypeStruct((nb, 2 * npair, t, 1), F32)],
        compiler_params=_cp("parallel", "arbitrary"),
    )(q, k, v, bias)


def na_bwd(q, k, v, bias, lse, do, *, lc, ctx_q, name):
    nb, t, w = q.shape
    npair = w // LANE
    nc, rows = lc // GRID_W, (t - lc) // GRID_W
    nwin = NA_ROWS * GRID_W

    def body(q_ref, k_ref, v_ref, bias_ref, lse_ref, do_ref, dq_ref, dk_ref, dv_ref, db_ref):
        b, st = pl.program_id(0), pl.program_id(1)

        @pl.when(st == 0)
        def _():
            dk_ref[...] = jnp.zeros_like(dk_ref)
            dv_ref[...] = jnp.zeros_like(dv_ref)

        @pl.when((st == 0) & (b == 0))
        def _():
            db_ref[...] = jnp.zeros_like(db_ref)

        ctx = pl.ds(0, lc)

        @pl.when(st < nc)
        def _():
            if not ctx_q:
                dq_ref[...] = jnp.zeros_like(dq_ref)
                return
            for pr in range(npair):
                lanes = slice(LANE * pr, LANE * (pr + 1))
                kc, vc = k_ref[ctx, lanes], v_ref[ctx, lanes]
                dqs = []
                for e in range(2):
                    mine = _lanes(HEAD_DIM * e, HEAD_DIM)
                    qq, dd = _only(q_ref[:, lanes], mine), _only(do_ref[:, lanes], mine)
                    p = jnp.exp2(_dot(qq, kc, _NT) - lse_ref[2 * pr + e])
                    dp = _dot(dd, vc, _NT)
                    delta = jnp.sum(p * dp, axis=-1, keepdims=True)
                    ds = (p * (dp - delta)).astype(BF16)
                    dqs.append(_dot(ds, kc))
                    dk_ref[ctx, lanes] += _dot(ds, qq, _TN)
                    dv_ref[ctx, lanes] += _dot(p.astype(BF16), dd, _TN)
                dq_ref[:, lanes] = jnp.where(_lanes(0, HEAD_DIM), dqs[0], dqs[1])

        @pl.when(st >= nc)
        def _():
            _, r0, case = _na_window(st, nc, rows)
            win = pl.ds(pl.multiple_of(lc + r0 * GRID_W, GRID_W), nwin)
            for pr in range(npair):
                lanes = slice(LANE * pr, LANE * (pr + 1))
                kc, vc, kw, vw = k_ref[ctx, lanes], v_ref[ctx, lanes], k_ref[win, lanes], v_ref[win, lanes]
                qq, dd = _stack_pair(q_ref[:, lanes], HEAD_DIM, 0), _stack_pair(do_ref[:, lanes], HEAD_DIM, 0)
                lse = jnp.concatenate([lse_ref[2 * pr], lse_ref[2 * pr + 1]], axis=0)
                bias2 = jnp.concatenate([bias_ref[2 * pr], bias_ref[2 * pr + 1]], axis=0)
                p_loc = jnp.exp2(_dot(qq, kw, _NT) + bias2 * LOG2E - lse)
                p_ctx = jnp.exp2(_dot(qq, kc, _NT) - lse)
                dp_loc, dp_ctx = _dot(dd, vw, _NT), _dot(dd, vc, _NT)
                delta = jnp.sum(p_loc * dp_loc, axis=-1, keepdims=True) + jnp.sum(p_ctx * dp_ctx, axis=-1, keepdims=True)
                ds_loc = p_loc * (dp_loc - delta)
                db_ref[2 * pr, case] += ds_loc[:GRID_W]
                db_ref[2 * pr + 1, case] += ds_loc[GRID_W:]
                ds_loc = ds_loc.astype(BF16)
                ds_ctx = (p_ctx * (dp_ctx - delta)).astype(BF16)
                dq = _dot(ds_loc, kw) + _dot(ds_ctx, kc)
                dq_ref[:, lanes] = jnp.where(_lanes(0, HEAD_DIM), dq[:GRID_W], dq[GRID_W:])
                dk_ref[win, lanes] += _dot(ds_loc, qq, _TN)
                dk_ref[ctx, lanes] += _dot(ds_ctx, qq, _TN)
                dv_ref[win, lanes] += _dot(p_loc.astype(BF16), dd, _TN)
                dv_ref[ctx, lanes] += _dot(p_ctx.astype(BF16), dd, _TN)

    qmap = lambda b, st: (b, st, 0)
    kmap = lambda b, st: (b, 0, 0)
    nh = 2 * npair
    return pl.pallas_call(
        body, name=name, grid=(nb, nc + rows),
        in_specs=[pl.BlockSpec((None, GRID_W, w), qmap), pl.BlockSpec((None, t, w), kmap), pl.BlockSpec((None, t, w), kmap),
                  pl.BlockSpec((nh, None, GRID_W, nwin), lambda b, st: (0, _na_window(st, nc, rows)[2], 0, 0)),
                  pl.BlockSpec((None, nh, GRID_W, 1), lambda b, st: (b, 0, st, 0)), pl.BlockSpec((None, GRID_W, w), qmap)],
        out_specs=[pl.BlockSpec((None, GRID_W, w), qmap), pl.BlockSpec((None, t, w), kmap), pl.BlockSpec((None, t, w), kmap),
                   pl.BlockSpec((nh, NA_ROWS, GRID_W, nwin), lambda b, st: (0, 0, 0, 0))],
        out_shape=[jax.ShapeDtypeStruct((nb, t, w), F32), jax.ShapeDtypeStruct((nb, t, w), F32), jax.ShapeDtypeStruct((nb, t, w), F32),
                   jax.ShapeDtypeStruct((nh, NA_ROWS, GRID_W, nwin), F32)],
        compiler_params=_cp("arbitrary", "arbitrary"),
    )(q, k, v, bias, lse, do)


def _na_tables():
    cols = np.arange(GRID_W)
    c0 = np.clip(cols - NA_COLS // 2, 0, GRID_W - NA_COLS)
    col_in = (cols[None, :] >= c0[:, None]) & (cols[None, :] < c0[:, None] + NA_COLS)
    dc = np.clip(cols[None, :] - cols[:, None] + NA_COLS - 1, 0, 2 * NA_COLS - 2)
    dr = np.arange(NA_ROWS)[None, :] + (NA_ROWS - 1) - np.arange(NA_ROWS)[:, None]
    return col_in, dc, dr


def _na_onehots():
    col_in, dc, dr = _na_tables()
    e1 = np.zeros((GRID_W, GRID_W, LANE), np.float32)
    qi, ki = np.nonzero(col_in)
    e1[qi, ki, dc[qi, ki]] = 1.0
    e2 = np.zeros((2 * NA_ROWS, NA_ROWS, NA_ROWS), np.float32)
    ci, ji = np.meshgrid(np.arange(NA_ROWS), np.arange(NA_ROWS), indexing='ij')
    e2[dr[ci, ji], ci, ji] = 1.0
    return jnp.asarray(e1.reshape(GRID_W * GRID_W, LANE)), jnp.asarray(e2.reshape(2 * NA_ROWS, NA_ROWS * NA_ROWS)), col_in


def na_expand_bias(rel_bias, name):
    e1, e2, col_in = _na_onehots()
    nh = rel_bias.shape[0]
    nrow = NA_ROWS * NA_ROWS
    rel = jnp.pad(rel_bias, ((0, 0), (0, 1), (0, LANE - rel_bias.shape[2])))
    rel = rel.transpose(1, 0, 2).reshape(2 * NA_ROWS, nh * LANE)
    y = mm(e2, rel, ta=True, name=name + "_rows", precise=True)
    y = y.reshape(nrow, nh, LANE).transpose(1, 0, 2).reshape(nh * nrow, LANE)
    g = mm(y, e1, tb=True, name=name + "_cols", precise=True)
    g = g.reshape(nh, NA_ROWS, NA_ROWS, GRID_W, GRID_W).transpose(0, 1, 3, 2, 4)
    g = jnp.where(col_in[None, None, :, None, :], g, NEG_BIG)
    return g.reshape(nh, NA_ROWS, GRID_W, NA_ROWS * GRID_W)


def na_reduce_bias(dexp, name):
    e1, e2, _ = _na_onehots()
    nh = dexp.shape[0]
    x = dexp.reshape(nh, NA_ROWS, GRID_W, NA_ROWS, GRID_W).transpose(0, 1, 3, 2, 4).reshape(nh * NA_ROWS * NA_ROWS, GRID_W * GRID_W)
    y = mm(x, e1, name=name + "_cols", precise=True)
    y = y.reshape(nh, NA_ROWS * NA_ROWS, LANE).transpose(1, 0, 2).reshape(NA_ROWS * NA_ROWS, nh * LANE)
    z = mm(e2, y, name=name + "_rows", precise=True)
    return z.reshape(2 * NA_ROWS, nh, LANE).transpose(1, 0, 2)[:, :2 * NA_ROWS - 1, :2 * NA_COLS - 1]


def _rot_matrix(width, d_rot):
    f = d_rot // 4
    r = np.zeros((width, width), np.float32)
    for base in range(0, width, d_rot // 2):
        for j in range(f):
            r[base + f + j, base + j] = -1.0
            r[base + j, base + f + j] = 1.0
    return r


def _rope_tables(s_len, lc, d_rot, reps):
    half = d_rot // 2
    freqs = ROPE_THETA ** (-jnp.arange(0, half, 2, dtype=F32) / half)
    tpos = jnp.arange(s_len)
    row = (tpos // GRID_W).astype(F32)[:, None] * freqs
    col = (tpos % GRID_W).astype(F32)[:, None] * freqs
    ang = jnp.concatenate([row, row, col, col], axis=-1)
    cos = jnp.concatenate([jnp.ones((lc, d_rot), F32), jnp.cos(ang)], axis=0)
    sin = jnp.concatenate([jnp.zeros((lc, d_rot), F32), jnp.sin(ang)], axis=0)
    return jnp.tile(cos, (1, reps)), jnp.tile(sin, (1, reps))


def _post_consts():
    s_b = np.kron(np.eye(GQA_HEADS, dtype=np.float32), np.full((HEAD_DIM, HEAD_DIM), 1.0 / HEAD_DIM, np.float32))
    t_b = np.tile(np.eye(HEAD_DIM, dtype=np.float32), (1, GQA_HEADS))
    r_b = _rot_matrix(GQ_W, HEAD_DIM)
    r_m = _rot_matrix(LANE, MLA_ROPE)
    rep = np.zeros((LANE, LANE), np.float32)
    for h in range(MLA_HEADS):
        rep[np.arange(MLA_ROPE), h * MLA_ROPE + np.arange(MLA_ROPE)] = 1.0
    dup = np.zeros((GK_W, 2 * GK_W), np.float32)
    for j in range(GQA_KV_HEADS):
        for e in range(2):
            dup[HEAD_DIM * j + np.arange(HEAD_DIM), 2 * HEAD_DIM * j + HEAD_DIM * e + np.arange(HEAD_DIM)] = 1.0
    return tuple(jnp.asarray(a) for a in (s_b, r_b, t_b, r_m, rep, dup))


def _heads_to_parts(w, first):
    r = w.shape[0]
    w3 = w.reshape(r, MLA_HEADS, -1)
    return jnp.concatenate([w3[:, :, :first].reshape(r, -1), w3[:, :, first:].reshape(r, -1)], axis=1)


def _parts_to_heads(w, first):
    r = w.shape[0]
    nf = MLA_HEADS * first
    return jnp.concatenate([w[:, :nf].reshape(r, MLA_HEADS, first), w[:, nf:].reshape(r, MLA_HEADS, -1)], axis=2).reshape(r, -1)


def _place():
    return lax.axis_index("x"), lax.axis_index("y"), lax.axis_index("c")


def all_gather(v, *, name, with_c):
    flips = [(dx, dy, dc) for dx in (0, 1) for dy in (0, 1) for dc in ((0, 1) if with_c else (0,))][1:]
    n = len(flips) + 1

    def body(v_ref, out_ref, send_sems, recv_sems, local_sem):
        mx, my, mc = _place()

        def slot(px, py, pc):
            return 4 * px + 2 * py + pc if with_c else 2 * px + py

        mine = pltpu.make_async_copy(v_ref, out_ref.at[slot(mx, my, mc)], local_sem)
        mine.start()
        sends = []
        for j, (dx, dy, dc) in enumerate(flips):
            peer = (mx ^ dx, my ^ dy, mc ^ dc)
            cp = pltpu.make_async_remote_copy(src_ref=v_ref, dst_ref=out_ref.at[slot(mx, my, mc)], send_sem=send_sems.at[j],
                                              recv_sem=recv_sems.at[j], device_id=peer, device_id_type=MESH)
            cp.start()
            sends.append(cp)
        for j, (dx, dy, dc) in enumerate(flips):
            peer = (mx ^ dx, my ^ dy, mc ^ dc)
            pltpu.make_async_remote_copy(src_ref=v_ref, dst_ref=out_ref.at[slot(*peer)], send_sem=send_sems.at[j],
                                         recv_sem=recv_sems.at[j], device_id=peer, device_id_type=MESH).wait_recv()
        for cp in sends:
            cp.wait_send()
        mine.wait()

    return pl.pallas_call(
        body, name=name, in_specs=[ANY], out_specs=ANY, out_shape=jax.ShapeDtypeStruct((n,) + v.shape, v.dtype),
        scratch_shapes=[pltpu.SemaphoreType.DMA((n - 1,)), pltpu.SemaphoreType.DMA((n - 1,)), pltpu.SemaphoreType.DMA(())],
    )(v)


def gather_shards(v, *, name):
    _, h, w = v.shape
    flips = [(1, 0), (0, 1), (1, 1)]

    def body(v_ref, out_ref, send_sems, recv_sems):
        mx, my, mc = _place()
        me = 2 * mx + my
        sib = (mx, my, 1 - mc)

        def copy(k, src, dst, to):
            return pltpu.make_async_remote_copy(src_ref=src, dst_ref=dst, send_sem=send_sems.at[k], recv_sem=recv_sems.at[k],
                                                device_id=to, device_id_type=MESH)

        first = [copy(j, v_ref.at[mc], out_ref.at[me, mc], (mx ^ dx, my ^ dy, mc)) for j, (dx, dy) in enumerate(flips)]
        for cp in first:
            cp.start()
        passed = []
        for j, (dx, dy) in enumerate(flips):
            theirs = out_ref.at[2 * (mx ^ dx) + (my ^ dy), mc]
            copy(j, v_ref.at[mc], theirs, (mx ^ dx, my ^ dy, mc)).wait_recv()
            fw = copy(3 + j, theirs, theirs, sib)
            fw.start()
            passed.append(fw)
        for j, (dx, dy) in enumerate(flips):
            other = out_ref.at[2 * (mx ^ dx) + (my ^ dy), 1 - mc]
            copy(3 + j, other, other, sib).wait_recv()
        for cp in first + passed:
            cp.wait_send()

    out = pl.pallas_call(
        body, name=name, in_specs=[ANY], out_specs=ANY, out_shape=jax.ShapeDtypeStruct((4, 2, h, w), v.dtype),
        scratch_shapes=[pltpu.SemaphoreType.DMA((6,)), pltpu.SemaphoreType.DMA((6,))],
    )(v)
    mx, my, _ = _place()
    return lax.dynamic_update_slice(out, v[None], (2 * mx + my, 0, 0, 0))


def pair_exchange_halves(g, *, name):
    n, _, h, w = g.shape

    def body(g_ref, out_ref, send_sems, recv_sems):
        mx, my, mc = _place()
        sib = (mx, my, 1 - mc)
        cps = [pltpu.make_async_remote_copy(src_ref=g_ref.at[s, 1 - mc], dst_ref=out_ref.at[s], send_sem=send_sems.at[s],
                                            recv_sem=recv_sems.at[s], device_id=sib, device_id_type=MESH) for s in range(n)]
        for cp in cps:
            cp.start()
        for cp in cps:
            cp.wait_recv()
        for cp in cps:
            cp.wait_send()

    return pl.pallas_call(
        body, name=name, in_specs=[ANY], out_specs=ANY, out_shape=jax.ShapeDtypeStruct((n, h, w), g.dtype),
        scratch_shapes=[pltpu.SemaphoreType.DMA((n,)), pltpu.SemaphoreType.DMA((n,))],
    )(g)


def all_to_all_xy(v, *, name):
    def body(v_ref, out_ref, send_sems, recv_sems):
        mx, my, mc = _place()
        me = 2 * mx + my
        flips = [(1, 0), (0, 1), (1, 1)]
        sends = []
        for j, (dx, dy) in enumerate(flips):
            px, py = mx ^ dx, my ^ dy
            cp = pltpu.make_async_remote_copy(src_ref=v_ref.at[2 * px + py], dst_ref=out_ref.at[me], send_sem=send_sems.at[j],
                                              recv_sem=recv_sems.at[j], device_id=(px, py, mc), device_id_type=MESH)
            cp.start()
            sends.append(cp)
        for j, (dx, dy) in enumerate(flips):
            px, py = mx ^ dx, my ^ dy
            pltpu.make_async_remote_copy(src_ref=v_ref.at[me], dst_ref=out_ref.at[2 * px + py], send_sem=send_sems.at[j],
                                         recv_sem=recv_sems.at[j], device_id=(px, py, mc), device_id_type=MESH).wait_recv()
        for cp in sends:
            cp.wait_send()

    out = pl.pallas_call(
        body, name=name, in_specs=[ANY], out_specs=ANY, out_shape=jax.ShapeDtypeStruct(v.shape, v.dtype),
        scratch_shapes=[pltpu.SemaphoreType.DMA((3,)), pltpu.SemaphoreType.DMA((3,))],
    )(v)
    mx, my, _ = _place()
    me = 2 * mx + my
    return lax.dynamic_update_slice(out, lax.dynamic_slice_in_dim(v, me, 1, axis=0), (me, 0, 0))


def pair_all_gather(v, *, name):
    def body(v_ref, out_ref, send_sem, recv_sem):
        mx, my, mc = _place()
        cp = pltpu.make_async_remote_copy(src_ref=v_ref, dst_ref=out_ref.at[mc], send_sem=send_sem, recv_sem=recv_sem,
                                          device_id=(mx, my, 1 - mc), device_id_type=MESH)
        cp.start()
        pltpu.make_async_remote_copy(src_ref=v_ref, dst_ref=out_ref.at[1 - mc], send_sem=send_sem, recv_sem=recv_sem,
                                     device_id=(mx, my, 1 - mc), device_id_type=MESH).wait_recv()
        cp.wait_send()

    out = pl.pallas_call(
        body, name=name, in_specs=[ANY], out_specs=ANY, out_shape=jax.ShapeDtypeStruct((2,) + v.shape, v.dtype),
        scratch_shapes=[pltpu.SemaphoreType.DMA(()), pltpu.SemaphoreType.DMA(())],
    )(v)
    return lax.dynamic_update_slice(out, v[None], (_place()[2], 0, 0))


def gather_ffn(wl, *, name):
    nl, nblk, cs, d = wl.shape
    assert nl == 2
    flips = [(1, 0), (0, 1), (1, 1)]

    def body(v_ref, out_ref, send_sems, recv_sems):
        mx, my, mc = _place()
        me = 2 * mx + my
        sib = (mx, my, 1 - mc)

        def copy(k, src, dst, to):
            return pltpu.make_async_remote_copy(src_ref=src, dst_ref=dst, send_sem=send_sems.at[k], recv_sem=recv_sems.at[k],
                                                device_id=to, device_id_type=MESH)

        first = [copy(j, v_ref.at[mc], out_ref.at[mc, me], (mx ^ dx, my ^ dy, mc)) for j, (dx, dy) in enumerate(flips)]
        for cp in first:
            cp.start()
        passed = []
        for j, (dx, dy) in enumerate(flips):
            theirs = out_ref.at[mc, 2 * (mx ^ dx) + (my ^ dy)]
            copy(j, v_ref.at[mc], theirs, (mx ^ dx, my ^ dy, mc)).wait_recv()
            fw = copy(3 + j, theirs, theirs, sib)
            fw.start()
            passed.append(fw)
        for j, (dx, dy) in enumerate(flips):
            other = out_ref.at[1 - mc, 2 * (mx ^ dx) + (my ^ dy)]
            copy(3 + j, other, other, sib).wait_recv()
        for cp in first + passed:
            cp.wait_send()

    out = pl.pallas_call(
        body, name=name, in_specs=[ANY], out_specs=ANY, out_shape=jax.ShapeDtypeStruct((nl, 4, nblk, cs, d), wl.dtype),
        scratch_shapes=[pltpu.SemaphoreType.DMA((6,)), pltpu.SemaphoreType.DMA((6,))],
    )(wl)
    mx, my, _ = _place()
    return lax.dynamic_update_slice(out, wl[:, None], (0, 2 * mx + my, 0, 0, 0))


def reduce_ffn(g0, g1, *, name):
    nt = len(g0)
    nsh, cs, d = g0[0].shape
    flips = [(1, 0), (0, 1), (1, 1)]
    mx, my, mc = _place()
    me = 2 * mx + my
    c_idx = jnp.reshape(mc, (1,)).astype(jnp.int32)

    def pair_body(*refs):
        ins0, ins1, outs = refs[:nt], refs[nt:2 * nt], refs[2 * nt:3 * nt]
        send_sems, recv_sems = refs[3 * nt:]
        kx, ky, kc = _place()
        sib = (kx, ky, 1 - kc)
        for c in range(2):
            @pl.when(kc == c)
            def _(c=c):
                mine_out = (ins1, ins0)[c]
                cps = [pltpu.make_async_remote_copy(src_ref=mine_out[t], dst_ref=outs[t], send_sem=send_sems.at[t],
                                                    recv_sem=recv_sems.at[t], device_id=sib, device_id_type=MESH) for t in range(nt)]
                for cp in cps:
                    cp.start()
                for cp in cps:
                    cp.wait_recv()
                for cp in cps:
                    cp.wait_send()

    from_pair = pl.pallas_call(
        pair_body, name=name + "_pair", in_specs=[ANY] * (2 * nt), out_specs=[ANY] * nt,
        out_shape=[jax.ShapeDtypeStruct((nsh, cs, d), F32)] * nt,
        scratch_shapes=[pltpu.SemaphoreType.DMA((nt,)), pltpu.SemaphoreType.DMA((nt,))],
    )(*g0, *g1)

    tr = _row_tile(cs, 64)

    def add_body(c_ref, *refs):
        for t in range(nt):
            mine = jnp.where(c_ref[0] == 0, refs[t][...], refs[nt + t][...])
            refs[3 * nt + t][...] = (mine + refs[2 * nt + t][...]).astype(BF16)

    spec = pl.BlockSpec((None, tr, d), lambda s, i, c_ref: (s, i, 0))
    chip_sum = pl.pallas_call(
        add_body, name=name + "_pair_add",
        grid_spec=pltpu.PrefetchScalarGridSpec(num_scalar_prefetch=1, grid=(nsh, cs // tr), in_specs=[spec] * (3 * nt),
                                               out_specs=[spec] * nt),
        out_shape=[jax.ShapeDtypeStruct((nsh, cs, d), BF16)] * nt, compiler_params=_cp("parallel", "parallel"),
    )(c_idx, *g0, *g1, *from_pair)

    def xy_body(*refs):
        ins, outs = refs[:nt], refs[nt:2 * nt]
        send_sems, recv_sems = refs[2 * nt:]
        kx, ky, kc = _place()
        k_me = 2 * kx + ky
        sends = []
        for j, (dx, dy) in enumerate(flips):
            px, py = kx ^ dx, ky ^ dy
            for t in range(nt):
                cp = pltpu.make_async_remote_copy(src_ref=ins[t].at[2 * px + py], dst_ref=outs[t].at[k_me],
                                                  send_sem=send_sems.at[j * nt + t], recv_sem=recv_sems.at[j * nt + t],
                                                  device_id=(px, py, kc), device_id_type=MESH)
                cp.start()
                sends.append(cp)
        for j, (dx, dy) in enumerate(flips):
            px, py = kx ^ dx, ky ^ dy
            for t in range(nt):
                pltpu.make_async_remote_copy(src_ref=ins[t].at[k_me], dst_ref=outs[t].at[2 * px + py],
                                             send_sem=send_sems.at[j * nt + t], recv_sem=recv_sems.at[j * nt + t],
                                             device_id=(px, py, kc), device_id_type=MESH).wait_recv()
        for cp in sends:
            cp.wait_send()

    from_xy = pl.pallas_call(
        xy_body, name=name + "_xy", in_specs=[ANY] * nt, out_specs=[ANY] * nt,
        out_shape=[jax.ShapeDtypeStruct((nsh, cs, d), BF16)] * nt,
        scratch_shapes=[pltpu.SemaphoreType.DMA((3 * nt,)), pltpu.SemaphoreType.DMA((3 * nt,))],
    )(*chip_sum)
    from_xy = [lax.dynamic_update_slice(o, lax.dynamic_slice_in_dim(v, me, 1, axis=0), (me, 0, 0)) for o, v in zip(from_xy, chip_sum)]

    def sum_body(*refs):
        for t in range(nt):
            acc = refs[t][0].astype(F32)
            for s in range(1, nsh):
                acc = acc + refs[t][s].astype(F32)
            refs[nt + t][...] = acc

    reduced = pl.pallas_call(
        sum_body, name=name + "_xy_add", grid=(cs // tr,), in_specs=[pl.BlockSpec((nsh, tr, d), lambda i: (0, i, 0))] * nt,
        out_specs=[pl.BlockSpec((tr, d), lambda i: (i, 0))] * nt, out_shape=[jax.ShapeDtypeStruct((cs, d), F32)] * nt,
        compiler_params=_cp("parallel"),
    )(*from_xy)

    def share_body(*refs):
        ins, outs = refs[:nt], refs[nt:2 * nt]
        send_sems, recv_sems = refs[2 * nt:]
        kx, ky, kc = _place()
        sib = (kx, ky, 1 - kc)
        cps = [pltpu.make_async_remote_copy(src_ref=ins[t], dst_ref=outs[t].at[kc], send_sem=send_sems.at[t],
                                            recv_sem=recv_sems.at[t], device_id=sib, device_id_type=MESH) for t in range(nt)]
        for cp in cps:
            cp.start()
        for t in range(nt):
            pltpu.make_async_remote_copy(src_ref=ins[t], dst_ref=outs[t].at[1 - kc], send_sem=send_sems.at[t],
                                         recv_sem=recv_sems.at[t], device_id=sib, device_id_type=MESH).wait_recv()
        for cp in cps:
            cp.wait_send()

    both = pl.pallas_call(
        share_body, name=name + "_share", in_specs=[ANY] * nt, out_specs=[ANY] * nt,
        out_shape=[jax.ShapeDtypeStruct((2, cs, d), F32)] * nt,
        scratch_shapes=[pltpu.SemaphoreType.DMA((nt,)), pltpu.SemaphoreType.DMA((nt,))],
    )(*reduced)
    return [lax.dynamic_update_slice(o, v[None], (mc, 0, 0)) for o, v in zip(both, reduced)]


def add_kept_half(g, r, c_idx, *, name, out_dtype):
    n, _, h, w = g.shape
    th = _row_tile(h)

    def body(c_ref, g_ref, r_ref, o_ref):
        o_ref[...] = (g_ref[...] + r_ref[...]).astype(o_ref.dtype)

    return pl.pallas_call(
        body, name=name,
        grid_spec=pltpu.PrefetchScalarGridSpec(
            num_scalar_prefetch=1, grid=(n, h // th),
            in_specs=[pl.BlockSpec((None, None, th, w), lambda s, i, c_ref: (s, c_ref[0], i, 0)),
                      pl.BlockSpec((None, th, w), lambda s, i, c_ref: (s, i, 0))],
            out_specs=pl.BlockSpec((None, th, w), lambda s, i, c_ref: (s, i, 0))),
        out_shape=jax.ShapeDtypeStruct((n, h, w), out_dtype), compiler_params=_cp("parallel", "parallel"),
    )(c_idx, g, r)


def sum_slots(v, *, name):
    n, rows, w = v.shape
    tr = _row_tile(rows, 256)

    def body(v_ref, o_ref):
        acc = v_ref[0].astype(F32)
        for s in range(1, n):
            acc = acc + v_ref[s].astype(F32)
        o_ref[...] = acc

    return pl.pallas_call(body, name=name, grid=(rows // tr,), in_specs=[pl.BlockSpec((n, tr, w), lambda i: (0, i, 0))],
                          out_specs=pl.BlockSpec((tr, w), lambda i: (i, 0)), out_shape=jax.ShapeDtypeStruct((rows, w), F32),
                          compiler_params=_cp("parallel"))(v)


def ada_fwd(c_rows, w_ada, b_shard, *, name):
    nl, d, ncol = w_ada.shape
    rows = c_rows.shape[0]
    tn = _tile(ncol, (768, 512, 256, 128))

    def body(c_ref, w_ref, b_ref, o_ref):
        o_ref[...] = jnp.dot(jax.nn.silu(c_ref[...]), w_ref[...], precision=HI, preferred_element_type=F32) + b_ref[...]

    return pl.pallas_call(
        body, name=name, grid=(nl, ncol // tn),
        in_specs=[pl.BlockSpec((rows, d), lambda l, j: (0, 0)), pl.BlockSpec((None, d, tn), lambda l, j: (l, 0, j)),
                  pl.BlockSpec((None, 1, tn), lambda l, j: (l, 0, j))],
        out_specs=pl.BlockSpec((None, rows, tn), lambda l, j: (l, 0, j)),
        out_shape=jax.ShapeDtypeStruct((nl, rows, ncol), F32), compiler_params=_cp("parallel", "parallel"),
    )(c_rows, w_ada, b_shard)


def ada_bwd(c_rows, w_ada, dm_shard, dm_full, n_ex, *, name):
    nl, d, ncol = w_ada.shape
    rows = c_rows.shape[0]
    tn = _tile(ncol, (768, 512, 256, 128))
    nj = ncol // tn

    def body(c_ref, w_ref, dm_ref, dmf_ref, gw_ref, gb_ref, dc_ref, dact_ref):
        l, j = pl.program_id(0), pl.program_id(1)
        act, act_vjp = jax.vjp(jax.nn.silu, c_ref[...])
        gw_ref[...] = lax.dot_general(act, dm_ref[...], _TN, precision=HI, preferred_element_type=F32)
        gb_ref[...] = jnp.sum(dmf_ref[...], axis=0, keepdims=True)
        part = lax.dot_general(dm_ref[...], w_ref[...], _NT, precision=HI, preferred_element_type=F32)

        @pl.when((l == 0) & (j == 0))
        def _():
            dact_ref[...] = part

        @pl.when((l > 0) | (j > 0))
        def _():
            dact_ref[...] += part

        @pl.when((l == nl - 1) & (j == nj - 1))
        def _():
            dc, = act_vjp(dact_ref[...])
            dc_ref[...] = jnp.sum(dc[n_ex:, :], axis=0, keepdims=True)

    return pl.pallas_call(
        body, name=name, grid=(nl, nj),
        in_specs=[pl.BlockSpec((rows, d), lambda l, j: (0, 0)), pl.BlockSpec((None, d, tn), lambda l, j: (l, 0, j)),
                  pl.BlockSpec((None, rows, tn), lambda l, j: (l, 0, j)),
                  pl.BlockSpec((None, rows, dm_full.shape[-1]), lambda l, j: (l, 0, 0))],
        out_specs=[pl.BlockSpec((None, d, tn), lambda l, j: (l, 0, j)),
                   pl.BlockSpec((None, 1, dm_full.shape[-1]), lambda l, j: (l, 0, 0)),
                   pl.BlockSpec((1, d), lambda l, j: (0, 0))],
        out_shape=[jax.ShapeDtypeStruct((nl, d, ncol), F32), jax.ShapeDtypeStruct((nl, 1, dm_full.shape[-1]), F32),
                   jax.ShapeDtypeStruct((1, d), F32)],
        scratch_shapes=[pltpu.VMEM((rows, d), F32)], compiler_params=_cp("arbitrary", "arbitrary"),
    )(c_rows, w_ada, dm_shard, dm_full)


def adamw(w, g, m, v, *, name):
    shape = w.shape
    cols = shape[-1]
    rows = int(np.prod(shape[:-1])) if len(shape) > 1 else 1
    tr = _row_tile(rows, 256)

    def body(w_ref, g_ref, m_ref, v_ref, d_ref, nm_ref, nv_ref):
        gg = g_ref[...]
        nm = ADAM_B1 * m_ref[...] + (1.0 - ADAM_B1) * gg
        nv = ADAM_B2 * v_ref[...] + (1.0 - ADAM_B2) * jnp.square(gg)
        m_hat = nm / (1.0 - ADAM_B1 ** ADAM_STEP)
        v_hat = nv / (1.0 - ADAM_B2 ** ADAM_STEP)
        d_ref[...] = -ADAM_LR * (m_hat / (jnp.sqrt(v_hat) + ADAM_EPS) + ADAM_WD * w_ref[...])
        nm_ref[...] = nm
        nv_ref[...] = nv

    spec = pl.BlockSpec((tr, cols), lambda i: (i, 0))
    out = pl.pallas_call(body, name=name, grid=(rows // tr,), in_specs=[spec] * 4, out_specs=[spec] * 3,
                         out_shape=[jax.ShapeDtypeStruct((rows, cols), F32)] * 3, compiler_params=_cp("parallel"),
                         )(*[a.reshape(rows, cols) for a in (w, g, m, v)])
    return tuple(o.reshape(shape) for o in out)


def local_step(h0, target, mods, lw, wf, small, *, lc):
    nb, t, d = h0.shape
    nt, nct = t // TM, lc // TM
    s_len = t - lc
    nl = len(lw)
    nsh = wf.shape[1]
    consts = _post_consts()
    cos_b, sin_b = _rope_tables(s_len, lc, HEAD_DIM, GQA_HEADS)
    cos_m, sin_m = _rope_tables(s_len, lc, MLA_ROPE, MLA_HEADS)
    rc = functools.partial(rowcall, nb=nb, nt=nt, nct=nct)
    flat = lambda a: a.reshape(nb * t, a.shape[-1])
    unflat = lambda a: a.reshape(nb, t, a.shape[-1])
    vec = lambda a: a.reshape(1, -1)

    def norm_first(h, g, shift, scale, tag):
        n, = rc(tag + "_norm", lambda _, *a: (f_normmod(*a),), [(h, 'tok'), (vec(g), 'full'), (shift, 'mod'), (scale, 'mod')],
                [('tok', d, BF16)])
        return n

    def res_norm(h, y, gate, coef, g, shift, scale, tag):
        def fn(_, hh, yy, gt, gn, sh, sc):
            h2 = hh + coef * gt * yy
            return h2, f_normmod(h2, gn, sh, sc)

        return rc(tag + "_res_norm", fn, [(h, 'tok'), (y, 'tok'), (gate, 'mod'), (vec(g), 'full'), (shift, 'mod'), (scale, 'mod')],
                  [('tok', d, F32), ('tok', d, BF16)])

    def res_last(h, y, gate, coef, tag):
        h2, = rc(tag + "_res", lambda _, hh, yy, gt: (hh + coef * gt * yy,), [(h, 'tok'), (y, 'tok'), (gate, 'mod')], [('tok', d, F32)])
        return h2

    def res_bwd_last(dh2, y, gate, coef, tag):
        return rc(tag + "_res_bwd", lambda _, dd, yy, gt: (coef * gt * dd, jnp.sum(coef * yy * dd, axis=0, keepdims=True)),
                  [(dh2, 'tok'), (y, 'tok'), (gate, 'mod')], [('tok', d, BF16), ('mod', d)])

    def norm_bwd_first(h, g, shift, scale, dn, dres, tag):
        def fn(_, hh, gn, sh, sc, dnn, dr):
            dh, dg, dsh, dsc = jax.vjp(f_normmod, hh, gn, sh, sc)[1](dnn)
            return dh + dr, dg, dsh, dsc

        return rc(tag + "_norm_bwd", fn, [(h, 'tok'), (vec(g), 'full'), (shift, 'mod'), (scale, 'mod'), (dn, 'tok'), (dres, 'tok')],
                  [('tok', d, F32), ('full', (1, d)), ('mod', d), ('mod', d)])

    def norm_bwd_res_bwd(h, g, shift, scale, dn, dres, y_prev, gate_prev, coef_prev, tag):
        def fn(_, hh, gn, sh, sc, dnn, dr, yy, gt):
            dh, dg, dsh, dsc = jax.vjp(f_normmod, hh, gn, sh, sc)[1](dnn)
            dh = dh + dr
            return dh, dg, dsh, dsc, coef_prev * gt * dh, jnp.sum(coef_prev * yy * dh, axis=0, keepdims=True)

        return rc(tag + "_norm_bwd", fn,
                  [(h, 'tok'), (vec(g), 'full'), (shift, 'mod'), (scale, 'mod'), (dn, 'tok'), (dres, 'tok'), (y_prev, 'tok'), (gate_prev, 'mod')],
                  [('tok', d, F32), ('full', (1, d)), ('mod', d), ('mod', d), ('tok', d, BF16), ('mod', d)])

    def ffn_fwd(n, l, base, tag):
        gg, uu, act = ffn_up(flat(n), wf, l, base, name=tag + "_up")
        return unflat(ffn_down(act, wf, l, base, name=tag + "_down")), (n, gg, uu, act)

    def ffn_bwd(dy, saved, l, base, tag):
        n, gg, uu, act = saved
        dw_d = ffn_dw(act, flat(dy), nsh, name=tag + "_down_dw")
        dgg, duu = ffn_down_bwd(flat(dy), gg, uu, wf, l, base, name=tag + "_down_dx")
        dw_g = ffn_dw(dgg, flat(n), nsh, name=tag + "_gate_dw")
        dw_u = ffn_dw(duu, flat(n), nsh, name=tag + "_up_dw")
        return unflat(ffn_up_bwd(dgg, duu, wf, l, base, name=tag + "_up_dx")), [dw_g, dw_u, dw_d]

    def post_ins(p, sm, w):
        return [(p, ('tokc', MAIN_PAD, 0)), (cos_b, 'pos'), (sin_b, 'pos'), (cos_m, 'pos'), (sin_m, 'pos'),
                (vec(sm['gqa_q_norm']), 'full'), (vec(sm['gqa_k_norm']), 'full'), (vec(sm['mla_q_norm']), 'full'),
                (vec(sm['mla_kv_norm']), 'full'), (w['w_uq'], 'full'), (w['w_ukv'], 'full')] + [(c, 'full') for c in consts]

    def mix_fwd(n, sm, w, ctx_q, tag):
        p = unflat(mm_resident(flat(n), w['w_in'], out_dtype=BF16, name=tag + "_in"))
        parts = rc(tag + "_post", lambda _, pp, *a: f_post(pp.astype(F32), *a), post_ins(p, sm, w),
                   [('tok', wd, BF16) for wd in POST_WIDTHS])
        aq, ak, av, bq, bk, bv, mqn, mqr, mkn, mkr, mv = parts
        bias = na_expand_bias(sm['na_rel_bias'], tag + "_bias")
        o_a, lse_a = na_fwd(aq, ak, av, bias, lc=lc, ctx_q=ctx_q, name=tag + "_na")
        o_b, lse_b = gqa_fwd(bq, bk, bv, lc=lc, ctx_q=ctx_q, name=tag + "_gqa")
        o_m, lse_m = mla_fwd(mqn, mqr, mkn, mkr, mv, lc=lc, ctx_q=ctx_q, name=tag + "_mla")
        fo = [o_a, o_b, o_m]
        ys = [unflat(mm_resident(flat(o), w[k], out_dtype=BF16, name=tag + "_br" + k[-1])) for o, k in zip(fo, ('w_a', 'w_b', 'w_c'))]
        gcols = [(p, ('tokc', d, MAIN_PAD // d + j)) for j in range(3)]
        y, = rc(tag + "_merge", lambda _, *a: (f_merge(*[v.astype(F32) for v in a]),), gcols + [(v, 'tok') for v in ys],
                [('tok', d, BF16)])
        z = unflat(mm_resident(flat(y), w['w_o'], name=tag + "_out"))
        saved = (n, p, (aq, ak, av, lse_a, bias), (bq, bk, bv, lse_b), (mqn, mqr, mkn, mkr, mv, lse_m), fo, ys, y)
        return z, saved

    def mix_bwd(dz, saved, sm, w, ctx_q, tag):
        n, p, (aq, ak, av, lse_a, bias), (bq, bk, bv, lse_b), (mqn, mqr, mkn, mkr, mv, lse_m), fo, ys, y = saved
        dw_o = mm(flat(y), flat(dz), ta=True, name=tag + "_out_dw")
        dy = unflat(mm_resident(flat(dz), w['w_o'], tb=True, name=tag + "_out_dx"))
        gcols = [(p, ('tokc', d, MAIN_PAD // d + j)) for j in range(3)]

        def merge_bwd(_, ga, gb, gm, ya, yb, ym, dyy):
            dga, dgb, dgm, dya, dyb, dym = jax.vjp(f_merge, *[v.astype(F32) for v in (ga, gb, gm, ya, yb, ym)])[1](dyy)
            return dya, dyb, dym, jnp.concatenate([dga, dgb, dgm], axis=-1)

        dya, dyb, dym, dgl = rc(tag + "_merge_bwd", merge_bwd, gcols + [(v, 'tok') for v in ys] + [(dy, 'tok')],
                                [('tok', d, BF16)] * 3 + [('tok', 3 * d, BF16)])
        dws, dos = {}, []
        for o, dyk, k in zip(fo, (dya, dyb, dym), ('w_a', 'w_b', 'w_c')):
            dws[k] = mm(flat(o), flat(dyk), ta=True, name=tag + "_br" + k[-1] + "_dw")
            dos.append(unflat(mm_resident(flat(dyk), w[k], tb=True, out_dtype=BF16, name=tag + "_br" + k[-1] + "_dx")))
        do_a, do_b, do_m = dos
        daq, dak, dav, dbias = na_bwd(aq, ak, av, bias, lse_a, do_a, lc=lc, ctx_q=ctx_q, name=tag + "_na_bwd")
        dbq, dbk, dbv = gqa_bwd(bq, bk, bv, lse_b, do_b, lc=lc, ctx_q=ctx_q, name=tag + "_gqa_bwd")
        dmqn, dmqr2, dmkn, dmkr, dmv = mla_bwd(mqn, mqr, mkn, mkr, mv, lse_m, do_m, lc=lc, ctx_q=ctx_q, name=tag + "_mla_bwd")
        d_rel = na_reduce_bias(dbias, tag + "_relb")
        cots = [daq, dak, dav, dbq, dbk, dbv, dmqn, dmqr2, dmkn, dmkr, dmv]
        ins = post_ins(p, sm, w)
        n_in = len(ins)

        def post_bwd(_, *a):
            prim, cot, dgl_v = a[:11], list(a[n_in:n_in + 11]), a[-1]
            cot[7] = cot[7][:, :LANE] + cot[7][:, LANE:]
            for j in POST_QK:
                cot[j] = cot[j] * LN2
            outs = jax.vjp(lambda pp, qn, kn, mqn, mkvn, wuq, wukv: f_post(pp, *prim[1:5], qn, kn, mqn, mkvn, wuq, wukv, *a[11:n_in]),
                           prim[0].astype(F32), *prim[5:11])[1](tuple(cot))
            return (jnp.concatenate([outs[0].astype(BF16), dgl_v], axis=-1),) + tuple(outs[1:])

        res = rc(tag + "_post_bwd", post_bwd, ins + [(cv, 'tok') for cv in cots] + [(dgl, 'tok')],
                 [('tok', MAIN_PAD + 3 * d, BF16), ('full', (1, HEAD_DIM)), ('full', (1, HEAD_DIM)), ('full', (1, MLA_Q_RANK)),
                  ('full', (1, MLA_KV_RANK)), ('full', w['w_uq'].shape), ('full', w['w_ukv'].shape)])
        dp, dqn, dkn, dmqn, dmkvn, dw_uq, dw_ukv = res
        dw_in = ffn_dw(flat(dp), flat(n), 4, name=tag + "_in_dw").reshape(-1, d)
        dn = unflat(mm_resident(flat(dp), w['w_in'], tb=True, name=tag + "_in_dx"))
        dsm = {'na_rel_bias': d_rel, 'gqa_q_norm': dqn.reshape(-1), 'gqa_k_norm': dkn.reshape(-1),
               'mla_q_norm': dmqn.reshape(-1), 'mla_kv_norm': dmkvn.reshape(-1)}
        dwl = {'w_in': dw_in, 'w_uq': dw_uq, 'w_ukv': dw_ukv, 'w_o': dw_o, **dws}
        return dn, dsm, dwl

    subs = [(l, kind, gain, coef) for l in range(nl)
            for kind, gain, coef in (('ffn1', 'ffn1_norm', 0.5), ('mix', 'mix_norm', 1.0), ('ffn2', 'ffn2_norm', 0.5))]
    ns = len(subs)
    sms = [{k: small[k][l] for k in SMALL_LAYER} for l in range(nl)]

    def params(k):
        l, _, gain, _ = subs[k]
        j = 3 * (k % 3)
        return small[gain][l], mods[l][j], mods[l][j + 1], mods[l][j + 2]

    def tag_of(k):
        return f"l{subs[k][0]}_{subs[k][1]}"

    h = h0
    g0, sh0, sc0, _ = params(0)
    n = norm_first(h, g0, sh0, sc0, tag_of(0))
    h_in, core_out, saved = [None] * ns, [None] * ns, [None] * ns
    for k, (l, kind, _, coef) in enumerate(subs):
        h_in[k] = h
        if kind == 'mix':
            core_out[k], saved[k] = mix_fwd(n, sms[l], lw[l], l + 1 < nl, tag_of(k))
        else:
            core_out[k], saved[k] = ffn_fwd(n, l, 0 if kind == 'ffn1' else 3, tag_of(k))
        gate = params(k)[3]
        if k + 1 < ns:
            gn, shn, scn, _ = params(k + 1)
            h, n = res_norm(h, core_out[k], gate, coef, gn, shn, scn, tag_of(k))
        else:
            h = res_last(h, core_out[k], gate, coef, tag_of(k))

    def final(is_ctx, hh, gg, tgt):
        def loss_fn(hv, gv):
            return 0.5 * jnp.sum(jnp.mean(jnp.square(_rms(hv, gv) - tgt), axis=-1))

        keep = jnp.where(is_ctx, 0.0, 1.0)
        loss, (dh, dg) = jax.value_and_grad(loss_fn, argnums=(0, 1))(hh, gg)
        return dh * keep, jnp.full((1, LANE), loss * keep, F32), dg * keep

    dh, loss, dg_final = rc("final_loss", final, [(h, 'tok'), (vec(small['final_norm']), 'full'), (target, 'lat')],
                            [('tok', d, F32), ('full', (1, LANE)), ('full', (1, d))])

    dsmall = {k: [None] * nl for k in SMALL_LAYER}
    dmods, dlw, dwf = [[None] * N_MOD for _ in range(nl)], [None] * nl, [[None] * 6 for _ in range(nl)]
    l_last, _, _, coef_last = subs[-1]
    dcore, dmods[l_last][8] = res_bwd_last(dh, core_out[-1], params(ns - 1)[3], coef_last, tag_of(ns - 1))
    for k in reversed(range(ns)):
        l, kind, gain, _ = subs[k]
        j = 3 * (k % 3)
        if kind == 'mix':
            dn, dsm, dlw[l] = mix_bwd(dcore, saved[k], sms[l], lw[l], l + 1 < nl, tag_of(k))
            for name, val in dsm.items():
                dsmall[name][l] = val
        else:
            base = 0 if kind == 'ffn1' else 3
            dn, dwf[l][base:base + 3] = ffn_bwd(dcore, saved[k], l, base, tag_of(k))
        g, shift, scale, _ = params(k)
        if k > 0:
            lp, _, _, coef_prev = subs[k - 1]
            dh, dg, dmods[l][j], dmods[l][j + 1], dcore, dmods[lp][3 * ((k - 1) % 3) + 2] = norm_bwd_res_bwd(
                h_in[k], g, shift, scale, dn, dh, core_out[k - 1], params(k - 1)[3], coef_prev, tag_of(k))
        else:
            dh, dg, dmods[l][j], dmods[l][j + 1] = norm_bwd_first(h_in[k], g, shift, scale, dn, dh, tag_of(k))
        dsmall[gain][l] = dg.reshape(d)
    dsmall = {k: jnp.stack(v) for k, v in dsmall.items()}
    dsmall['final_norm'] = dg_final.reshape(d)
    return loss, dh, dmods, dlw, dwf, dsmall


def _pack(parts, pad_rows):
    flat, where, off = [], [], 0
    for a in parts:
        n = _ceil_to(a.size, PACK_W)
        flat.append(jnp.pad(a.reshape(-1), (0, n - a.size)))
        where.append((off, n // PACK_W))
        off += n // PACK_W
    total = _ceil_to(off, pad_rows)
    if total > off:
        flat.append(jnp.zeros(((total - off) * PACK_W,), flat[0].dtype))
    return jnp.concatenate(flat).reshape(total, PACK_W), where


def _unpack(buf, where, shape):
    off, rows = where
    return buf[off:off + rows].reshape(-1)[:int(np.prod(shape))].reshape(shape)


def layer_weights(full, l):
    wi = full['w_in'][l]
    d = wi.shape[0]
    return {
        'w_in': jnp.concatenate([wi[:, :MAIN_W], jnp.zeros((d, MAIN_PAD - MAIN_W), wi.dtype), wi[:, MAIN_W:]], axis=1),
        'w_uq': _heads_to_parts(full['mla_w_uq'][l], MLA_NOPE).astype(F32),
        'w_ukv': _heads_to_parts(full['mla_w_ukv'][l], MLA_NOPE).astype(F32),
        'w_a': full['w_branch_a'][l], 'w_b': full['w_branch_b'][l], 'w_c': full['w_branch_c'][l], 'w_o': full['w_out'][l]}


def layer_grads_by_name(dlw):
    per_name = {k: [] for k, _ in BIG}
    for g in dlw:
        per_name['w_in'].append(jnp.concatenate([g['w_in'][:MAIN_W], g['w_in'][MAIN_PAD:]], axis=0))
        per_name['mla_w_uq'].append(_parts_to_heads(g['w_uq'], MLA_NOPE))
        per_name['mla_w_ukv'].append(_parts_to_heads(g['w_ukv'], MLA_NOPE))
        per_name['w_branch_a'].append(g['w_a'])
        per_name['w_branch_b'].append(g['w_b'])
        per_name['w_branch_c'].append(g['w_c'])
        per_name['w_out'].append(g['w_o'])
    return per_name


def kernel(x, c, ctx, c_ctx, w_ada, b_ada, ffn1_norm, ffn1_w_gate, ffn1_w_up, ffn1_w_down, mix_norm, w_in, na_rel_bias, gqa_q_norm, gqa_k_norm, mla_q_norm, mla_kv_norm, mla_w_uq, mla_w_ukv, w_branch_a, w_branch_b, w_branch_c, w_out, ffn2_norm, ffn2_w_gate, ffn2_w_up, ffn2_w_down, final_norm, loss_target, m_c_ctx, m_w_ada, m_b_ada, m_ffn1_norm, m_ffn1_w_gate, m_ffn1_w_up, m_ffn1_w_down, m_mix_norm, m_w_in, m_na_rel_bias, m_gqa_q_norm, m_gqa_k_norm, m_mla_q_norm, m_mla_kv_norm, m_mla_w_uq, m_mla_w_ukv, m_w_branch_a, m_w_branch_b, m_w_branch_c, m_w_out, m_ffn2_norm, m_ffn2_w_gate, m_ffn2_w_up, m_ffn2_w_down, m_final_norm, v_c_ctx, v_w_ada, v_b_ada, v_ffn1_norm, v_ffn1_w_gate, v_ffn1_w_up, v_ffn1_w_down, v_mix_norm, v_w_in, v_na_rel_bias, v_gqa_q_norm, v_gqa_k_norm, v_mla_q_norm, v_mla_kv_norm, v_mla_w_uq, v_mla_w_ukv, v_w_branch_a, v_w_branch_b, v_w_branch_c, v_w_out, v_ffn2_norm, v_ffn2_w_gate, v_ffn2_w_up, v_ffn2_w_down, v_final_norm):
    args = locals()
    wts = {k: args[k] for k in WEIGHTS}
    mom = {k: args['m_' + k] for k in WEIGHTS}
    var = {k: args['v_' + k] for k in WEIGHTS}
    nb, s_len, d = x.shape
    lc = ctx.shape[1]
    nl = w_ada.shape[0]
    nsh, ndev = 4, 8
    mx, my, mc = _place()
    sidx = 2 * mx + my
    didx = 4 * mx + 2 * my + mc
    assert d % LANE == 0 and MAIN_PAD % d == 0 and lc % TQ == 0 and s_len % TQ == 0 and s_len // GRID_W >= NA_ROWS

    wpack, wwhere = _pack([wts[k].astype(BF16) for k, _ in BIG], 32)
    wall = gather_shards(wpack.reshape(2, -1, PACK_W), name="gather_weights").reshape(nsh, -1, PACK_W)
    full = {}
    for (k, ax), wh in zip(BIG, wwhere):
        shp = wts[k].shape
        parts = jnp.stack([_unpack(wall[s], wh, shp) for s in range(nsh)])
        if ax == 1:
            full[k] = parts.transpose(1, 2, 0, 3).reshape(nl, shp[1], nsh * shp[2])
        else:
            full[k] = parts.transpose(1, 0, 2, 3).reshape(nl, nsh * shp[1], shp[2])
    lw = [layer_weights(full, l) for l in range(nl)]
    wl = jnp.stack([(wts[k].transpose(0, 2, 1) if tr else wts[k]).astype(BF16) for k, tr in zip(FFN_NAMES, FFN_TRANSPOSED)], axis=1)
    wf = gather_ffn(wl, name="gather_ffn")

    n_ex = ndev * nb
    ncol = w_ada.shape[-1]
    c_all = all_gather(c, name="gather_cond", with_c=True).reshape(n_ex, d)
    c_rows = jnp.concatenate([c_all, jnp.broadcast_to(c_ctx[None], (n_ex, d))], axis=0)
    b_shard = lax.dynamic_slice_in_dim(b_ada, sidx * ncol, ncol, axis=1)[:, None, :]
    mod_sh = ada_fwd(c_rows, w_ada, b_shard, name="ada_fwd")
    mod_all = all_gather(mod_sh, name="gather_mod", with_c=False)
    mod_all = mod_all.transpose(1, 2, 0, 3).reshape(nl, 2 * n_ex, nsh * ncol)
    mod_x = lax.dynamic_slice_in_dim(mod_all, didx * nb, nb, axis=1)
    mod_c = jnp.broadcast_to(mod_all[:, n_ex:n_ex + 1], mod_x.shape)
    mods = [[jnp.stack([mod_c[l, :, j * d:(j + 1) * d], mod_x[l, :, j * d:(j + 1) * d]], axis=1)[:, :, None, :]
             for j in range(N_MOD)] for l in range(nl)]

    small = {k: wts[k] for k in SMALL_LAYER + ['final_norm']}
    h0 = jnp.concatenate([ctx, x], axis=1)
    loss_part, dh0, dmods, dlw, dwf, dsmall = local_step(h0, loss_target, mods, lw, wf, small, lc=lc)
    grad_x = dh0[:, lc:]

    dmod_mine = jnp.stack([jnp.concatenate([m[:, :, 0, :] for m in dmods[l]], axis=-1) for l in range(nl)])
    small_names = SMALL_LAYER + ['final_norm']
    spack, swhere = _pack([loss_part] + [dsmall[k] for k in small_names] + [dmod_mine], 8)
    sall = all_gather(spack, name="gather_small", with_c=True)
    ssum = sum_slots(sall, name="sum_small")
    loss = _unpack(ssum, swhere[0], (1, LANE))[0, 0]
    grads = {k: _unpack(ssum, wh, wts[k].shape) for k, wh in zip(small_names, swhere[1:])}
    off, rows = swhere[-1]
    dm_all = sall[:, off:off + rows].reshape(ndev, -1)[:, :dmod_mine.size].reshape((ndev,) + dmod_mine.shape)
    dm_all = dm_all.transpose(1, 3, 0, 2, 4).reshape(nl, 2, n_ex, N_MOD * d)
    dm_rows = jnp.concatenate([dm_all[:, 1], dm_all[:, 0]], axis=1)
    dm_shard = lax.dynamic_slice_in_dim(dm_rows, sidx * ncol, ncol, axis=2)
    grads['w_ada'], gb, dc_part = ada_bwd(c_rows, w_ada, dm_shard, dm_rows, n_ex, name="ada_bwd")
    grads['b_ada'] = gb.reshape(b_ada.shape)
    dc_all = all_gather(jnp.pad(dc_part, ((0, 7), (0, 0))), name="gather_dcond", with_c=False)
    grads['c_ctx'] = sum_slots(dc_all, name="sum_dcond")[0]

    per_name = layer_grads_by_name(dlw)
    pieces, gwhere, off = [], [], 0
    for k, ax in BIG:
        shp = wts[k].shape
        for g in per_name[k]:
            if ax == 1 and k not in GRAD_TRANSPOSED:
                pieces.append(g.reshape(shp[1], nsh, shp[2]).transpose(1, 0, 2).reshape(nsh, -1))
            else:
                pieces.append(g.reshape(nsh, -1))
        n = int(np.prod(shp))
        if n % PACK_W:
            pieces.append(jnp.zeros((nsh, _ceil_to(n, PACK_W) - n), F32))
        gwhere.append((off, _ceil_to(n, PACK_W) // PACK_W))
        off += _ceil_to(n, PACK_W) // PACK_W
    if off % 128:
        pieces.append(jnp.zeros((nsh, (_ceil_to(off, 128) - off) * PACK_W), F32))
    half = _ceil_to(off, 128) // 2
    gpack = jnp.concatenate(pieces, axis=1).reshape(nsh, 2, half, PACK_W)
    from_pair = pair_exchange_halves(gpack, name="reduce_pair")
    chip_sum = add_kept_half(gpack, from_pair, jnp.reshape(mc, (1,)).astype(jnp.int32), name="reduce_pair_add",
                             out_dtype=BF16)
    from_xy = all_to_all_xy(chip_sum, name="reduce_xy")
    reduced = sum_slots(from_xy, name="reduce_xy_add")
    gfull = pair_all_gather(reduced, name="reduce_share").reshape(2 * half, PACK_W)
    for (k, _), wh in zip(BIG, gwhere):
        shp = wts[k].shape
        grads[k] = (_unpack(gfull, wh, (shp[0], shp[2], shp[1])).transpose(0, 2, 1) if k in GRAD_TRANSPOSED
                    else _unpack(gfull, wh, shp))
    for k, tr, g in zip(FFN_NAMES, FFN_TRANSPOSED, reduce_ffn(dwf[0], dwf[1], name="reduce_ffn")):
        grads[k] = g.transpose(0, 2, 1) if tr else g

    outs = {k: adamw(wts[k], grads[k], mom[k], var[k], name="adamw_" + k) for k in WEIGHTS}
    return (loss, grad_x, *[grads[k] for k in WEIGHTS], *[outs[k][0] for k in WEIGHTS], *[outs[k][1] for k in WEIGHTS],
            *[outs[k][2] for k in WEIGHTS])
```

```python
import functools

import jax
import jax.numpy as jnp
import numpy as np
from jax import lax
from jax.experimental import pallas as pl
from jax.experimental.pallas import tpu as pltpu

F32 = jnp.float32
BF16 = jnp.bfloat16
HI = lax.Precision.HIGHEST
MESH = pl.DeviceIdType.MESH
ANY = pl.BlockSpec(memory_space=pl.ANY)

V7X_VMEM_BYTES = 64 * 1024 * 1024
VMEM_LIMIT = V7X_VMEM_BYTES - 8 * 1024 * 1024
LANE = 128
PACK_W = 1024

GRID_W = 64
HEAD_DIM = 64
NA_HEADS, NA_ROWS, NA_COLS = 4, 8, 16
GQA_HEADS, GQA_KV_HEADS = 8, 2
MLA_HEADS, MLA_Q_RANK, MLA_KV_RANK, MLA_NOPE, MLA_ROPE, MLA_V = 4, 256, 128, 64, 32, 64
N_MOD = 9
ROPE_THETA = 10000.0
EPS = 1e-6
NEG_BIG = -1e30
NA_W = NA_HEADS * HEAD_DIM
GQ_W = GQA_HEADS * HEAD_DIM
GK_W = GQA_KV_HEADS * HEAD_DIM
MAIN_W = 3 * NA_W + GQ_W + 2 * GK_W + MLA_Q_RANK + MLA_KV_RANK + MLA_ROPE
MAIN_PAD = 2048
LOG2E, LN2 = float(np.log2(np.e)), float(np.log(2.0))
Q_SCALE = HEAD_DIM ** -0.5 * LOG2E
MLA_Q_SCALE = (MLA_NOPE + MLA_ROPE) ** -0.5 * LOG2E
TQ = 256
TM = 256

ADAM_LR, ADAM_B1, ADAM_B2, ADAM_EPS, ADAM_WD, ADAM_STEP = 0.001, 0.9, 0.999, 1e-08, 0.01, 10

ARG_NAMES = ['x', 'c', 'ctx', 'c_ctx', 'w_ada', 'b_ada', 'ffn1_norm', 'ffn1_w_gate', 'ffn1_w_up', 'ffn1_w_down', 'mix_norm', 'w_in',
             'na_rel_bias', 'gqa_q_norm', 'gqa_k_norm', 'mla_q_norm', 'mla_kv_norm', 'mla_w_uq', 'mla_w_ukv', 'w_branch_a',
             'w_branch_b', 'w_branch_c', 'w_out', 'ffn2_norm', 'ffn2_w_gate', 'ffn2_w_up', 'ffn2_w_down', 'final_norm']
WEIGHTS = ARG_NAMES[3:]
BIG = [('w_in', 1), ('mla_w_uq', 1), ('mla_w_ukv', 1), ('w_branch_a', 1), ('w_branch_b', 1), ('w_branch_c', 1), ('w_out', 0)]
GRAD_TRANSPOSED = ('w_in',)
FFN_NAMES = ['ffn1_w_gate', 'ffn1_w_up', 'ffn1_w_down', 'ffn2_w_gate', 'ffn2_w_up', 'ffn2_w_down']
FFN_TRANSPOSED = [True, True, False, True, True, False]
SMALL_LAYER = ['ffn1_norm', 'mix_norm', 'na_rel_bias', 'gqa_q_norm', 'gqa_k_norm', 'mla_q_norm', 'mla_kv_norm', 'ffn2_norm']


def _cp(*sem):
    return pltpu.CompilerParams(dimension_semantics=sem, vmem_limit_bytes=VMEM_LIMIT)


def _tile(dim, cands):
    for t in cands:
        if dim % t == 0:
            return t
    return dim


def _row_tile(rows, cap=512, mult=16):
    best = None
    for t in range(mult, min(rows, cap) + 1, mult):
        if rows % t == 0:
            best = t
    return best or rows


def _ceil_to(n, m):
    return -(-n // m) * m


def mm(a, b, *, name, ta=False, tb=False, out_dtype=F32, precise=False):
    m, k = (a.shape[1], a.shape[0]) if ta else a.shape
    n = b.shape[0] if tb else b.shape[1]
    tm = _tile(m, (512, 256, 128))
    tn = _tile(n, (1024, 1408, 512, 256, 128))
    tk = _tile(k, (1024, 1408, 512, 256, 128))
    nk = k // tk
    dims = (((0 if ta else 1,), (1 if tb else 0,)), ((), ()))

    def body(a_ref, b_ref, o_ref, *acc):
        if precise:
            part = lax.dot_general(a_ref[...].astype(F32), b_ref[...].astype(F32), dims, precision=HI, preferred_element_type=F32)
        else:
            part = lax.dot_general(a_ref[...].astype(BF16), b_ref[...].astype(BF16), dims, preferred_element_type=F32)
        if nk == 1:
            o_ref[...] = part.astype(o_ref.dtype)
        else:
            acc_ref, = acc
            kk = pl.program_id(2)

            @pl.when(kk == 0)
            def _():
                acc_ref[...] = part

            @pl.when(kk > 0)
            def _():
                acc_ref[...] += part

            @pl.when(kk == nk - 1)
            def _():
                o_ref[...] = acc_ref[...].astype(o_ref.dtype)

    a_spec = pl.BlockSpec((tk, tm), lambda i, j, kk: (kk, i)) if ta else pl.BlockSpec((tm, tk), lambda i, j, kk: (i, kk))
    b_spec = pl.BlockSpec((tn, tk), lambda i, j, kk: (j, kk)) if tb else pl.BlockSpec((tk, tn), lambda i, j, kk: (kk, j))
    return pl.pallas_call(
        body, name=name, grid=(m // tm, n // tn, nk), in_specs=[a_spec, b_spec],
        out_specs=pl.BlockSpec((tm, tn), lambda i, j, kk: (i, j)),
        out_shape=jax.ShapeDtypeStruct((m, n), out_dtype),
        scratch_shapes=[pltpu.VMEM((tm, tn), F32)] if nk > 1 else [],
        compiler_params=_cp("parallel", "parallel", "arbitrary"),
    )(a, b)


def mm_resident(a, w, *, name, tb=False, out_dtype=F32):
    m, k = a.shape
    n = w.shape[0] if tb else w.shape[1]
    tm = _tile(m, (512, 256, 128))
    cn = n if tb else _tile(n, (1024, 512, 256, 128))

    def body(a_ref, w_ref, o_ref):
        aa = a_ref[...].astype(BF16)
        if tb:
            o_ref[...] = _dot(aa, w_ref[...], _NT).astype(o_ref.dtype)
        else:
            for c in range(n // cn):
                cols = slice(cn * c, cn * (c + 1))
                o_ref[:, cols] = _dot(aa, w_ref[:, cols]).astype(o_ref.dtype)

    return pl.pallas_call(
        body, name=name, grid=(m // tm,),
        in_specs=[pl.BlockSpec((tm, k), lambda i: (i, 0)), pl.BlockSpec(w.shape, lambda i: (0, 0), pipeline_mode=pl.Buffered(1))],
        out_specs=pl.BlockSpec((tm, n), lambda i: (i, 0)), out_shape=jax.ShapeDtypeStruct((m, n), out_dtype),
        compiler_params=_cp("parallel"),
    )(a, w)


FFN_GATE, FFN_UP, FFN_DOWN = 0, 1, 2


def _ffn_wspec(wf, l, which):
    _, nsh, _, cs, d = wf.shape
    return pl.BlockSpec((None, nsh, None, cs, d), lambda *_: (l, 0, which, 0, 0), pipeline_mode=pl.Buffered(1))


def _ffn_group(cs):
    for g in (1, 2, 4):
        if (g * cs) % LANE == 0:
            return g
    raise ValueError(cs)


def ffn_up(n, wf, l, base, *, name):
    m, d = n.shape
    nsh, cs = wf.shape[1], wf.shape[3]
    f = nsh * cs
    grp = _ffn_group(cs)
    tm = _tile(m, (512, 256, 128))

    def body(n_ref, wg_ref, wu_ref, g_ref, u_ref, a_ref):
        nn = n_ref[...]
        for c in range(nsh // grp):
            cols = slice(grp * cs * c, grp * cs * (c + 1))
            g = _dot(nn, wg_ref[grp * c:grp * (c + 1)].reshape(grp * cs, d), _NT)
            u = _dot(nn, wu_ref[grp * c:grp * (c + 1)].reshape(grp * cs, d), _NT)
            g_ref[:, cols] = g.astype(BF16)
            u_ref[:, cols] = u.astype(BF16)
            a_ref[:, cols] = f_act_gu(g, u).astype(BF16)

    ospec = pl.BlockSpec((tm, f), lambda i: (i, 0))
    return pl.pallas_call(
        body, name=name, grid=(m // tm,),
        in_specs=[pl.BlockSpec((tm, d), lambda i: (i, 0)), _ffn_wspec(wf, l, base + FFN_GATE), _ffn_wspec(wf, l, base + FFN_UP)],
        out_specs=[ospec] * 3, out_shape=[jax.ShapeDtypeStruct((m, f), BF16)] * 3, compiler_params=_cp("parallel"),
    )(n, wf, wf)


def ffn_down(act, wf, l, base, *, name):
    m, f = act.shape
    nsh, cs, d = wf.shape[1], wf.shape[3], wf.shape[4]
    tm = _tile(m, (512, 256, 128))

    def body(a_ref, wd_ref, y_ref):
        y_ref[...] = _dot(a_ref[...], wd_ref[...].reshape(f, d))

    return pl.pallas_call(
        body, name=name, grid=(m // tm,),
        in_specs=[pl.BlockSpec((tm, f), lambda i: (i, 0)), _ffn_wspec(wf, l, base + FFN_DOWN)],
        out_specs=pl.BlockSpec((tm, d), lambda i: (i, 0)), out_shape=jax.ShapeDtypeStruct((m, d), F32), compiler_params=_cp("parallel"),
    )(act, wf)


def ffn_down_bwd(dy, g, u, wf, l, base, *, name):
    m, d = dy.shape
    nsh, cs = wf.shape[1], wf.shape[3]
    f = nsh * cs
    grp = _ffn_group(cs)
    tm = _tile(m, (512, 256, 128))

    def body(dy_ref, g_ref, u_ref, wd_ref, dg_ref, du_ref):
        dd = dy_ref[...]
        for c in range(nsh // grp):
            cols = slice(grp * cs * c, grp * cs * (c + 1))
            dact = _dot(dd, wd_ref[grp * c:grp * (c + 1)].reshape(grp * cs, d), _NT)
            dg, du = jax.vjp(f_act_gu, g_ref[:, cols].astype(F32), u_ref[:, cols].astype(F32))[1](dact)
            dg_ref[:, cols] = dg.astype(BF16)
            du_ref[:, cols] = du.astype(BF16)

    fspec = pl.BlockSpec((tm, f), lambda i: (i, 0))
    return pl.pallas_call(
        body, name=name, grid=(m // tm,),
        in_specs=[pl.BlockSpec((tm, d), lambda i: (i, 0)), fspec, fspec, _ffn_wspec(wf, l, base + FFN_DOWN)],
        out_specs=[fspec] * 2, out_shape=[jax.ShapeDtypeStruct((m, f), BF16)] * 2, compiler_params=_cp("parallel"),
    )(dy, g, u, wf)


def ffn_up_bwd(dg, du, wf, l, base, *, name):
    m, f = dg.shape
    nsh, cs, d = wf.shape[1], wf.shape[3], wf.shape[4]
    tm = _tile(m, (512, 256, 128))

    def body(dg_ref, du_ref, wg_ref, wu_ref, dn_ref):
        dn_ref[...] = _dot(dg_ref[...], wg_ref[...].reshape(f, d)) + _dot(du_ref[...], wu_ref[...].reshape(f, d))

    fspec = pl.BlockSpec((tm, f), lambda i: (i, 0))
    return pl.pallas_call(
        body, name=name, grid=(m // tm,),
        in_specs=[fspec, fspec, _ffn_wspec(wf, l, base + FFN_GATE), _ffn_wspec(wf, l, base + FFN_UP)],
        out_specs=pl.BlockSpec((tm, d), lambda i: (i, 0)), out_shape=jax.ShapeDtypeStruct((m, d), F32), compiler_params=_cp("parallel"),
    )(dg, du, wf, wf)


def ffn_dw(a, b, nsh, *, name):
    m, f = a.shape
    d = b.shape[1]
    cs = f // nsh
    grp = _ffn_group(cs)
    tm = _tile(m, (1024, 512, 256, 128))

    def body(a_ref, b_ref, o_ref):
        part = _dot(a_ref[...], b_ref[...], _TN).reshape(grp, cs, d)
        i = pl.program_id(1)

        @pl.when(i == 0)
        def _():
            o_ref[...] = part

        @pl.when(i > 0)
        def _():
            o_ref[...] += part

    return pl.pallas_call(
        body, name=name, grid=(nsh // grp, m // tm),
        in_specs=[pl.BlockSpec((tm, grp * cs), lambda j, i: (i, j)), pl.BlockSpec((tm, d), lambda j, i: (i, 0))],
        out_specs=pl.BlockSpec((grp, cs, d), lambda j, i: (j, 0, 0)), out_shape=jax.ShapeDtypeStruct((nsh, cs, d), F32),
        compiler_params=_cp("parallel", "arbitrary"),
    )(a, b)


def rowcall(name, fn, ins, outs, *, nb, nt, nct):
    in_specs, arrays = [], []
    for arr, kind in ins:
        arrays.append(arr)
        if kind == 'tok':
            in_specs.append(pl.BlockSpec((None, TM, arr.shape[-1]), lambda b, t: (b, t, 0)))
        elif kind == 'lat':
            in_specs.append(pl.BlockSpec((None, TM, arr.shape[-1]), lambda b, t: (b, jnp.maximum(t - nct, 0), 0)))
        elif kind == 'pos':
            in_specs.append(pl.BlockSpec((TM, arr.shape[-1]), lambda b, t: (t, 0)))
        elif kind == 'mod':
            in_specs.append(pl.BlockSpec((None, None, 1, arr.shape[-1]), lambda b, t: (b, jnp.where(t >= nct, 1, 0), 0, 0)))
        elif kind == 'full':
            in_specs.append(pl.BlockSpec(arr.shape, lambda b, t, nd=arr.ndim: (0,) * nd))
        else:
            _, w, j = kind
            in_specs.append(pl.BlockSpec((None, TM, w), lambda b, t, j=j: (b, t, j)))
    out_specs, out_shape = [], []
    for o in outs:
        if o[0] == 'tok':
            out_specs.append(pl.BlockSpec((None, TM, o[1]), lambda b, t: (b, t, 0)))
            out_shape.append(jax.ShapeDtypeStruct((nb, nt * TM, o[1]), o[2]))
        elif o[0] == 'mod':
            out_specs.append(pl.BlockSpec((None, None, 1, o[1]), lambda b, t: (b, jnp.where(t >= nct, 1, 0), 0, 0)))
            out_shape.append(jax.ShapeDtypeStruct((nb, 2, 1, o[1]), F32))
        else:
            out_specs.append(pl.BlockSpec(o[1], lambda b, t, nd=len(o[1]): (0,) * nd))
            out_shape.append(jax.ShapeDtypeStruct(o[1], F32))
    n_in = len(ins)

    def body(*refs):
        b, t = pl.program_id(0), pl.program_id(1)
        res = fn(t < nct, *[r[...] for r in refs[:n_in]])
        for ref, o, val in zip(refs[n_in:], outs, res, strict=True):
            if o[0] == 'tok':
                ref[...] = val.astype(ref.dtype)
                continue
            first = ((t == 0) | (t == nct)) if o[0] == 'mod' else ((b == 0) & (t == 0))

            @pl.when(first)
            def _(ref=ref, val=val):
                ref[...] = val

            @pl.when(jnp.logical_not(first))
            def _(ref=ref, val=val):
                ref[...] += val

    return pl.pallas_call(body, name=name, grid=(nb, nt), in_specs=in_specs, out_specs=out_specs, out_shape=out_shape,
                          compiler_params=_cp("arbitrary", "arbitrary"))(*arrays)


def _rms(x, g):
    return x * lax.rsqrt(jnp.mean(x * x, axis=-1, keepdims=True) + EPS) * g


def f_normmod(h, g, shift, scale):
    return _rms(h, g) * (1.0 + scale) + shift


def f_act_gu(g, u):
    return jax.nn.silu(g) * u


def _dot_split(x, m, dims):
    hi = x.astype(BF16)
    lo = (x - hi.astype(F32)).astype(BF16)
    mb = m.astype(BF16)
    return (lax.dot_general(hi, mb, dims, preferred_element_type=F32) + lax.dot_general(lo, mb, dims, preferred_element_type=F32))


def dot_select(x, m):
    return _dot_select(x, m)


@jax.custom_vjp
def _dot_select(x, m):
    return _dot_split(x, m, (((1,), (0,)), ((), ())))


_dot_select.defvjp(lambda x, m: (_dot_split(x, m, (((1,), (0,)), ((), ()))), m),
                   lambda m, ct: (_dot_split(ct, m, (((1,), (1,)), ((), ()))), jnp.zeros_like(m)))


def f_merge(ga, gb, gm, ya, yb, ym):
    return jax.nn.sigmoid(ga) * ya + jax.nn.sigmoid(gb) * yb + jax.nn.sigmoid(gm) * ym


def f_post(p, cb, sb, cm, sm, qn, kn, mqn, mkvn, wuq, wukv, s_b, r_b, t_b, r_m, rep, dup):
    def hnorm(x, g, w):
        ms = dot_select(x * x, s_b[:w, :w])
        gw = dot_select(g, t_b[:, :w])
        return x * lax.rsqrt(ms + EPS) * gw

    def rope(x, cos, sin, rot):
        return x * cos + dot_select(x, rot) * sin

    o = 3 * NA_W
    a_q, a_k, a_v = p[:, 0:NA_W], p[:, NA_W:2 * NA_W], p[:, 2 * NA_W:o]
    b_q = rope(hnorm(p[:, o:o + GQ_W], qn, GQ_W), cb, sb, r_b)
    o += GQ_W
    b_k = rope(hnorm(p[:, o:o + GK_W], kn, GK_W), cb[:, :GK_W], sb[:, :GK_W], r_b[:GK_W, :GK_W])
    b_v = p[:, o + GK_W:o + 2 * GK_W]
    o += 2 * GK_W
    q_lat = jnp.dot(_rms(p[:, o:o + MLA_Q_RANK], mqn).astype(BF16), wuq.astype(BF16), preferred_element_type=F32)
    o += MLA_Q_RANK
    kv_lat = jnp.dot(_rms(p[:, o:o + MLA_KV_RANK], mkvn).astype(BF16), wukv.astype(BF16), preferred_element_type=F32)
    o += MLA_KV_RANK
    nw = MLA_HEADS * MLA_NOPE
    mq_nope, mq_rope = q_lat[:, :nw], rope(q_lat[:, nw:], cm, sm, r_m)
    mk_nope, m_v = kv_lat[:, :nw], kv_lat[:, nw:]
    mk_rope = dot_select(rope(p[:, o:o + LANE], cm, sm, r_m), rep)
    b_k2 = dot_select(b_k, dup)
    b_v2 = dot_select(b_v, dup)
    mq_cat = jnp.concatenate([mq_nope[:, :LANE], mq_rope, mq_nope[:, LANE:], mq_rope], axis=1) * MLA_Q_SCALE
    mk_cat = jnp.concatenate([mk_nope[:, :LANE], mk_rope, mk_nope[:, LANE:], mk_rope], axis=1)
    return (a_q * Q_SCALE, a_k, a_v, b_q * Q_SCALE, b_k2, b_v2, mq_cat, mk_cat, m_v)


POST_QK = (0, 1, 3, 4, 6, 7)


POST_WIDTHS = (NA_W, NA_W, NA_W, GQ_W, 2 * GK_W, 2 * GK_W, 4 * LANE, 4 * LANE, MLA_HEADS * MLA_V)
N_POST = len(POST_WIDTHS)


_NT = (((1,), (1,)), ((), ()))
_TN = (((0,), (0,)), ((), ()))


def _dot(a, b, dims=None):
    if dims is None:
        return jnp.dot(a, b, preferred_element_type=F32)
    return lax.dot_general(a, b, dims, preferred_element_type=F32)


def _lanes(lo, width):
    lane = lax.broadcasted_iota(jnp.int32, (1, LANE), 1)
    return (lane >= lo) & (lane < lo + width)


def _only(x, mask):
    return jnp.where(mask, x, jnp.zeros_like(x))


def _stack_pair(x, width, lo):
    return jnp.concatenate([_only(x, _lanes(lo, width)), _only(x, _lanes(lo + width, width))], axis=0)


def _pair_softmax(s):
    m = jnp.max(s, axis=-1, keepdims=True)
    p = jnp.exp2(s - m)
    l = jnp.sum(p, axis=-1, keepdims=True)
    return p, l, m + jnp.log2(l)


def gqa_fwd(q, k2, v2, *, lc, ctx_q, name):
    nb, t, qw = q.shape
    npair = qw // LANE
    per_kv = npair // GQA_KV_HEADS
    nctb = lc // TQ

    def body(q_ref, k_ref, v_ref, o_ref, lse_ref):
        i = pl.program_id(2)

        def run(rows):
            kk, vv = k_ref[rows, :], v_ref[rows, :]
            outs = []
            for e in range(2):
                p, l, lse = _pair_softmax(_dot(_only(q_ref[...], _lanes(HEAD_DIM * e, HEAD_DIM)), kk, _NT))
                outs.append(_dot(p.astype(BF16), vv) / l)
                lse_ref[e] = lse
            o_ref[...] = jnp.where(_lanes(0, HEAD_DIM), outs[0], outs[1]).astype(o_ref.dtype)

        @pl.when(i < nctb)
        def _():
            if ctx_q:
                run(pl.ds(0, lc))
            else:
                o_ref[...] = jnp.zeros_like(o_ref)
                lse_ref[...] = jnp.zeros_like(lse_ref)

        @pl.when(i >= nctb)
        def _():
            run(pl.ds(0, t))

    qmap = lambda b, p, i: (b, i, p)
    kmap = lambda b, p, i: (b, 0, p // per_kv)
    return pl.pallas_call(
        body, name=name, grid=(nb, npair, t // TQ),
        in_specs=[pl.BlockSpec((None, TQ, LANE), qmap), pl.BlockSpec((None, t, LANE), kmap), pl.BlockSpec((None, t, LANE), kmap)],
        out_specs=[pl.BlockSpec((None, TQ, LANE), qmap), pl.BlockSpec((None, 2, TQ, 1), lambda b, p, i: (b, p, i, 0))],
        out_shape=[jax.ShapeDtypeStruct((nb, t, qw), BF16), jax.ShapeDtypeStruct((nb, 2 * npair, t, 1), F32)],
        compiler_params=_cp("parallel", "parallel", "arbitrary"),
    )(q, k2, v2)


def gqa_bwd(q, k2, v2, lse, do, *, lc, ctx_q, name):
    nb, t, qw = q.shape
    npair = qw // LANE
    per_kv = npair // GQA_KV_HEADS
    nctb = lc // TQ

    def body(q_ref, k_ref, v_ref, lse_ref, do_ref, dq_ref, dk_ref, dv_ref):
        g, i = pl.program_id(2), pl.program_id(3)

        @pl.when((g == 0) & (i == 0))
        def _():
            dk_ref[...] = jnp.zeros_like(dk_ref)
            dv_ref[...] = jnp.zeros_like(dv_ref)

        def run(rows):
            kk, vv = k_ref[rows, :], v_ref[rows, :]
            qq, dd = _stack_pair(q_ref[...], HEAD_DIM, 0), _stack_pair(do_ref[...], HEAD_DIM, 0)
            p = jnp.exp2(_dot(qq, kk, _NT) - jnp.concatenate([lse_ref[0], lse_ref[1]], axis=0))
            dp = _dot(dd, vv, _NT)
            delta = jnp.sum(p * dp, axis=-1, keepdims=True)
            ds = (p * (dp - delta)).astype(BF16)
            dq = _dot(ds, kk)
            dq_ref[...] = jnp.where(_lanes(0, HEAD_DIM), dq[:TQ], dq[TQ:])
            dk_ref[rows, :] += _dot(ds, qq, _TN)
            dv_ref[rows, :] += _dot(p.astype(BF16), dd, _TN)

        @pl.when(i < nctb)
        def _():
            if ctx_q:
                run(pl.ds(0, lc))
            else:
                dq_ref[...] = jnp.zeros_like(dq_ref)

        @pl.when(i >= nctb)
        def _():
            run(pl.ds(0, t))

    qmap = lambda b, j, g, i: (b, i, j * per_kv + g)
    kmap = lambda b, j, g, i: (b, 0, j)
    return pl.pallas_call(
        body, name=name, grid=(nb, GQA_KV_HEADS, per_kv, t // TQ),
        in_specs=[pl.BlockSpec((None, TQ, LANE), qmap), pl.BlockSpec((None, t, LANE), kmap), pl.BlockSpec((None, t, LANE), kmap),
                  pl.BlockSpec((None, 2, TQ, 1), lambda b, j, g, i: (b, j * per_kv + g, i, 0)), pl.BlockSpec((None, TQ, LANE), qmap)],
        out_specs=[pl.BlockSpec((None, TQ, LANE), qmap), pl.BlockSpec((None, t, LANE), kmap), pl.BlockSpec((None, t, LANE), kmap)],
        out_shape=[jax.ShapeDtypeStruct((nb, t, qw), F32), jax.ShapeDtypeStruct(k2.shape, F32), jax.ShapeDtypeStruct(v2.shape, F32)],
        compiler_params=_cp("arbitrary", "arbitrary", "arbitrary", "arbitrary"),
    )(q, k2, v2, lse, do)


def _mla_lanes(pr, e):
    lane = lax.broadcasted_iota(jnp.int32, (1, 2 * LANE), 1)
    lo = LANE + MLA_ROPE * (2 * pr + e)
    return ((lane >= MLA_NOPE * e) & (lane < MLA_NOPE * (e + 1))) | ((lane >= lo) & (lane < lo + MLA_ROPE))


def mla_fwd(q, k, v, *, lc, ctx_q, name):
    nb, t, w = v.shape
    npair = w // LANE
    nctb = lc // TQ

    def body(q_ref, k_ref, v_ref, o_ref, lse_ref):
        pr, i = pl.program_id(1), pl.program_id(2)

        def run(rows):
            kk, vv = k_ref[rows, :], v_ref[rows, :]
            outs = []
            for e in range(2):
                p, l, lse = _pair_softmax(_dot(_only(q_ref[...], _mla_lanes(pr, e)), kk, _NT))
                outs.append(_dot(p.astype(BF16), vv) / l)
                lse_ref[e] = lse
            o_ref[...] = jnp.where(_lanes(0, MLA_V), outs[0], outs[1]).astype(o_ref.dtype)

        @pl.when(i < nctb)
        def _():
            if ctx_q:
                run(pl.ds(0, lc))
            else:
                o_ref[...] = jnp.zeros_like(o_ref)
                lse_ref[...] = jnp.zeros_like(lse_ref)

        @pl.when(i >= nctb)
        def _():
            run(pl.ds(0, t))

    qmap = lambda b, p, i: (b, i, p)
    kmap = lambda b, p, i: (b, 0, p)
    return pl.pallas_call(
        body, name=name, grid=(nb, npair, t // TQ),
        in_specs=[pl.BlockSpec((None, TQ, 2 * LANE), qmap), pl.BlockSpec((None, t, 2 * LANE), kmap), pl.BlockSpec((None, t, LANE), kmap)],
        out_specs=[pl.BlockSpec((None, TQ, LANE), qmap), pl.BlockSpec((None, 2, TQ, 1), lambda b, p, i: (b, p, i, 0))],
        out_shape=[jax.ShapeDtypeStruct((nb, t, w), BF16), jax.ShapeDtypeStruct((nb, 2 * npair, t, 1), F32)],
        compiler_params=_cp("parallel", "parallel", "arbitrary"),
    )(q, k, v)


def mla_bwd(q, k, v, lse, do, *, lc, ctx_q, name):
    nb, t, w = v.shape
    npair = w // LANE
    nctb = lc // TQ

    def body(q_ref, k_ref, v_ref, lse_ref, do_ref, dq_ref, dk_ref, dv_ref):
        pr, i = pl.program_id(1), pl.program_id(2)

        @pl.when(i == 0)
        def _():
            dk_ref[...] = jnp.zeros_like(dk_ref)
            dv_ref[...] = jnp.zeros_like(dv_ref)

        def run(rows):
            kk, vv = k_ref[rows, :], v_ref[rows, :]
            m0, m1 = _mla_lanes(pr, 0), _mla_lanes(pr, 1)
            qq = jnp.concatenate([_only(q_ref[...], m0), _only(q_ref[...], m1)], axis=0)
            dd = _stack_pair(do_ref[...], MLA_V, 0)
            p = jnp.exp2(_dot(qq, kk, _NT) - jnp.concatenate([lse_ref[0], lse_ref[1]], axis=0))
            dp = _dot(dd, vv, _NT)
            delta = jnp.sum(p * dp, axis=-1, keepdims=True)
            ds = (p * (dp - delta)).astype(BF16)
            dq = _dot(ds, kk)
            dq_ref[...] = _only(dq[:TQ], m0) + _only(dq[TQ:], m1)
            dk_ref[rows, :] += _dot(ds, qq, _TN)
            dv_ref[rows, :] += _dot(p.astype(BF16), dd, _TN)

        @pl.when(i < nctb)
        def _():
            if ctx_q:
                run(pl.ds(0, lc))
            else:
                dq_ref[...] = jnp.zeros_like(dq_ref)

        @pl.when(i >= nctb)
        def _():
            run(pl.ds(0, t))

    qmap = lambda b, p, i: (b, i, p)
    kmap = lambda b, p, i: (b, 0, p)
    return pl.pallas_call(
        body, name=name, grid=(nb, npair, t // TQ),
        in_specs=[pl.BlockSpec((None, TQ, 2 * LANE), qmap), pl.BlockSpec((None, t, 2 * LANE), kmap), pl.BlockSpec((None, t, LANE), kmap),
                  pl.BlockSpec((None, 2, TQ, 1), lambda b, p, i: (b, p, i, 0)), pl.BlockSpec((None, TQ, LANE), qmap)],
        out_specs=[pl.BlockSpec((None, TQ, 2 * LANE), qmap), pl.BlockSpec((None, t, 2 * LANE), kmap), pl.BlockSpec((None, t, LANE), kmap)],
        out_shape=[jax.ShapeDtypeStruct(q.shape, F32), jax.ShapeDtypeStruct(k.shape, F32), jax.ShapeDtypeStruct(v.shape, F32)],
        compiler_params=_cp("arbitrary", "arbitrary", "arbitrary"),
    )(q, k, v, lse, do)


def _na_window(st, nc, rows):
    r = jnp.maximum(st - nc, 0)
    r0 = jnp.clip(r - NA_ROWS // 2, 0, rows - NA_ROWS)
    return r, r0, r - r0


def na_fwd(q, k, v, bias, *, lc, ctx_q, name):
    nb, t, w = q.shape
    npair = w // LANE
    nc, rows = lc // GRID_W, (t - lc) // GRID_W
    nwin = NA_ROWS * GRID_W

    def body(q_ref, k_ref, v_ref, bias_ref, o_ref, lse_ref):
        st = pl.program_id(1)
        ctx = pl.ds(0, lc)

        @pl.when(st < nc)
        def _():
            if not ctx_q:
                o_ref[...] = jnp.zeros_like(o_ref)
                lse_ref[...] = jnp.zeros_like(lse_ref)
                return
            for pr in range(npair):
                lanes = slice(LANE * pr, LANE * (pr + 1))
                kc, vc = k_ref[ctx, lanes], v_ref[ctx, lanes]
                outs = []
                for e in range(2):
                    p, l, lse = _pair_softmax(_dot(_only(q_ref[:, lanes], _lanes(HEAD_DIM * e, HEAD_DIM)), kc, _NT))
                    outs.append(_dot(p.astype(BF16), vc) / l)
                    lse_ref[2 * pr + e] = lse
                o_ref[:, lanes] = jnp.where(_lanes(0, HEAD_DIM), outs[0], outs[1]).astype(o_ref.dtype)

        @pl.when(st >= nc)
        def _():
            _, r0, _ = _na_window(st, nc, rows)
            win = pl.ds(pl.multiple_of(lc + r0 * GRID_W, GRID_W), nwin)
            for pr in range(npair):
                lanes = slice(LANE * pr, LANE * (pr + 1))
                kc, vc, kw, vw = k_ref[ctx, lanes], v_ref[ctx, lanes], k_ref[win, lanes], v_ref[win, lanes]
                qq = _stack_pair(q_ref[:, lanes], HEAD_DIM, 0)
                s_loc = _dot(qq, kw, _NT) + jnp.concatenate([bias_ref[2 * pr], bias_ref[2 * pr + 1]], axis=0) * LOG2E
                s_ctx = _dot(qq, kc, _NT)
                m = jnp.maximum(jnp.max(s_loc, axis=-1, keepdims=True), jnp.max(s_ctx, axis=-1, keepdims=True))
                p_loc, p_ctx = jnp.exp2(s_loc - m), jnp.exp2(s_ctx - m)
                l = jnp.sum(p_loc, axis=-1, keepdims=True) + jnp.sum(p_ctx, axis=-1, keepdims=True)
                o = (_dot(p_loc.astype(BF16), vw) + _dot(p_ctx.astype(BF16), vc)) / l
                lse = m + jnp.log2(l)
                lse_ref[2 * pr], lse_ref[2 * pr + 1] = lse[:GRID_W], lse[GRID_W:]
                o_ref[:, lanes] = jnp.where(_lanes(0, HEAD_DIM), o[:GRID_W], o[GRID_W:]).astype(o_ref.dtype)

    qmap = lambda b, st: (b, st, 0)
    kmap = lambda b, st: (b, 0, 0)
    return pl.pallas_call(
        body, name=name, grid=(nb, nc + rows),
        in_specs=[pl.BlockSpec((None, GRID_W, w), qmap), pl.BlockSpec((None, t, w), kmap), pl.BlockSpec((None, t, w), kmap),
                  pl.BlockSpec((2 * npair, None, GRID_W, nwin), lambda b, st: (0, _na_window(st, nc, rows)[2], 0, 0))],
        out_specs=[pl.BlockSpec((None, GRID_W, w), qmap), pl.BlockSpec((None, 2 * npair, GRID_W, 1), lambda b, st: (b, 0, st, 0))],
        out_shape=[jax.ShapeDtypeStruct((nb, t, w), BF16), jax.ShapeDtypeStruct((nb, 2 * npair, t, 1), F32)],
        compiler_params=_cp("parallel", "arbitrary"),
    )(q, k, v, bias)


def na_bwd(q, k, v, bias, lse, do, *, lc, ctx_q, name):
    nb, t, w = q.shape
    npair = w // LANE
    nc, rows = lc // GRID_W, (t - lc) // GRID_W
    nwin = NA_ROWS * GRID_W

    def body(q_ref, k_ref, v_ref, bias_ref, lse_ref, do_ref, dq_ref, dk_ref, dv_ref, db_ref):
        b, st = pl.program_id(0), pl.program_id(1)

        @pl.when(st == 0)
        def _():
            dk_ref[...] = jnp.zeros_like(dk_ref)
            dv_ref[...] = jnp.zeros_like(dv_ref)

        @pl.when((st == 0) & (b == 0))
        def _():
            db_ref[...] = jnp.zeros_like(db_ref)

        ctx = pl.ds(0, lc)

        @pl.when(st < nc)
        def _():
            if not ctx_q:
                dq_ref[...] = jnp.zeros_like(dq_ref)
                return
            for pr in range(npair):
                lanes = slice(LANE * pr, LANE * (pr + 1))
                kc, vc = k_ref[ctx, lanes], v_ref[ctx, lanes]
                dqs = []
                for e in range(2):
                    mine = _lanes(HEAD_DIM * e, HEAD_DIM)
                    qq, dd = _only(q_ref[:, lanes], mine), _only(do_ref[:, lanes], mine)
                    p = jnp.exp2(_dot(qq, kc, _NT) - lse_ref[2 * pr + e])
                    dp = _dot(dd, vc, _NT)
                    delta = jnp.sum(p * dp, axis=-1, keepdims=True)
                    ds = (p * (dp - delta)).astype(BF16)
                    dqs.append(_dot(ds, kc))
                    dk_ref[ctx, lanes] += _dot(ds, qq, _TN)
                    dv_ref[ctx, lanes] += _dot(p.astype(BF16), dd, _TN)
                dq_ref[:, lanes] = jnp.where(_lanes(0, HEAD_DIM), dqs[0], dqs[1])

        @pl.when(st >= nc)
        def _():
            _, r0, case = _na_window(st, nc, rows)
            win = pl.ds(pl.multiple_of(lc + r0 * GRID_W, GRID_W), nwin)
            for pr in range(npair):
                lanes = slice(LANE * pr, LANE * (pr + 1))
                kc, vc, kw, vw = k_ref[ctx, lanes], v_ref[ctx, lanes], k_ref[win, lanes], v_ref[win, lanes]
                qq, dd = _stack_pair(q_ref[:, lanes], HEAD_DIM, 0), _stack_pair(do_ref[:, lanes], HEAD_DIM, 0)
                lse = jnp.concatenate([lse_ref[2 * pr], lse_ref[2 * pr + 1]], axis=0)
                bias2 = jnp.concatenate([bias_ref[2 * pr], bias_ref[2 * pr + 1]], axis=0)
                p_loc = jnp.exp2(_dot(qq, kw, _NT) + bias2 * LOG2E - lse)
                p_ctx = jnp.exp2(_dot(qq, kc, _NT) - lse)
                dp_loc, dp_ctx = _dot(dd, vw, _NT), _dot(dd, vc, _NT)
                delta = jnp.sum(p_loc * dp_loc, axis=-1, keepdims=True) + jnp.sum(p_ctx * dp_ctx, axis=-1, keepdims=True)
                ds_loc = p_loc * (dp_loc - delta)
                db_ref[2 * pr, case] += ds_loc[:GRID_W]
                db_ref[2 * pr + 1, case] += ds_loc[GRID_W:]
                ds_loc = ds_loc.astype(BF16)
                ds_ctx = (p_ctx * (dp_ctx - delta)).astype(BF16)
                dq = _dot(ds_loc, kw) + _dot(ds_ctx, kc)
                dq_ref[:, lanes] = jnp.where(_lanes(0, HEAD_DIM), dq[:GRID_W], dq[GRID_W:])
                dk_ref[win, lanes] += _dot(ds_loc, qq, _TN)
                dk_ref[ctx, lanes] += _dot(ds_ctx, qq, _TN)
                dv_ref[win, lanes] += _dot(p_loc.astype(BF16), dd, _TN)
                dv_ref[ctx, lanes] += _dot(p_ctx.astype(BF16), dd, _TN)

    qmap = lambda b, st: (b, st, 0)
    kmap = lambda b, st: (b, 0, 0)
    nh = 2 * npair
    return pl.pallas_call(
        body, name=name, grid=(nb, nc + rows),
        in_specs=[pl.BlockSpec((None, GRID_W, w), qmap), pl.BlockSpec((None, t, w), kmap), pl.BlockSpec((None, t, w), kmap),
                  pl.BlockSpec((nh, None, GRID_W, nwin), lambda b, st: (0, _na_window(st, nc, rows)[2], 0, 0)),
                  pl.BlockSpec((None, nh, GRID_W, 1), lambda b, st: (b, 0, st, 0)), pl.BlockSpec((None, GRID_W, w), qmap)],
        out_specs=[pl.BlockSpec((None, GRID_W, w), qmap), pl.BlockSpec((None, t, w), kmap), pl.BlockSpec((None, t, w), kmap),
                   pl.BlockSpec((nh, NA_ROWS, GRID_W, nwin), lambda b, st: (0, 0, 0, 0))],
        out_shape=[jax.ShapeDtypeStruct((nb, t, w), F32), jax.ShapeDtypeStruct((nb, t, w), F32), jax.ShapeDtypeStruct((nb, t, w), F32),
                   jax.ShapeDtypeStruct((nh, NA_ROWS, GRID_W, nwin), F32)],
        compiler_params=_cp("arbitrary", "arbitrary"),
    )(q, k, v, bias, lse, do)


def _na_tables():
    cols = np.arange(GRID_W)
    c0 = np.clip(cols - NA_COLS // 2, 0, GRID_W - NA_COLS)
    col_in = (cols[None, :] >= c0[:, None]) & (cols[None, :] < c0[:, None] + NA_COLS)
    dc = np.clip(cols[None, :] - cols[:, None] + NA_COLS - 1, 0, 2 * NA_COLS - 2)
    dr = np.arange(NA_ROWS)[None, :] + (NA_ROWS - 1) - np.arange(NA_ROWS)[:, None]
    return col_in, dc, dr


def _na_onehots():
    col_in, dc, dr = _na_tables()
    e1 = np.zeros((GRID_W, GRID_W, LANE), np.float32)
    qi, ki = np.nonzero(col_in)
    e1[qi, ki, dc[qi, ki]] = 1.0
    e2 = np.zeros((2 * NA_ROWS, NA_ROWS, NA_ROWS), np.float32)
    ci, ji = np.meshgrid(np.arange(NA_ROWS), np.arange(NA_ROWS), indexing='ij')
    e2[dr[ci, ji], ci, ji] = 1.0
    return jnp.asarray(e1.reshape(GRID_W * GRID_W, LANE)), jnp.asarray(e2.reshape(2 * NA_ROWS, NA_ROWS * NA_ROWS)), col_in


def na_expand_bias(rel_bias, name):
    e1, e2, col_in = _na_onehots()
    nh = rel_bias.shape[0]
    nrow = NA_ROWS * NA_ROWS
    rel = jnp.pad(rel_bias, ((0, 0), (0, 1), (0, LANE - rel_bias.shape[2])))
    rel = rel.transpose(1, 0, 2).reshape(2 * NA_ROWS, nh * LANE)
    y = mm(e2, rel, ta=True, name=name + "_rows", precise=True)
    y = y.reshape(nrow, nh, LANE).transpose(1, 0, 2).reshape(nh * nrow, LANE)
    g = mm(y, e1, tb=True, name=name + "_cols", precise=True)
    g = g.reshape(nh, NA_ROWS, NA_ROWS, GRID_W, GRID_W).transpose(0, 1, 3, 2, 4)
    g = jnp.where(col_in[None, None, :, None, :], g, NEG_BIG)
    return g.reshape(nh, NA_ROWS, GRID_W, NA_ROWS * GRID_W)


def na_reduce_bias(dexp, name):
    e1, e2, _ = _na_onehots()
    nh = dexp.shape[0]
    x = dexp.reshape(nh, NA_ROWS, GRID_W, NA_ROWS, GRID_W).transpose(0, 1, 3, 2, 4).reshape(nh * NA_ROWS * NA_ROWS, GRID_W * GRID_W)
    y = mm(x, e1, name=name + "_cols", precise=True)
    y = y.reshape(nh, NA_ROWS * NA_ROWS, LANE).transpose(1, 0, 2).reshape(NA_ROWS * NA_ROWS, nh * LANE)
    z = mm(e2, y, name=name + "_rows", precise=True)
    return z.reshape(2 * NA_ROWS, nh, LANE).transpose(1, 0, 2)[:, :2 * NA_ROWS - 1, :2 * NA_COLS - 1]


def _rot_matrix(width, d_rot):
    f = d_rot // 4
    r = np.zeros((width, width), np.float32)
    for base in range(0, width, d_rot // 2):
        for j in range(f):
            r[base + f + j, base + j] = -1.0
            r[base + j, base + f + j] = 1.0
    return r


def _rope_tables(s_len, lc, d_rot, reps):
    half = d_rot // 2
    freqs = ROPE_THETA ** (-jnp.arange(0, half, 2, dtype=F32) / half)
    tpos = jnp.arange(s_len)
    row = (tpos // GRID_W).astype(F32)[:, None] * freqs
    col = (tpos % GRID_W).astype(F32)[:, None] * freqs
    ang = jnp.concatenate([row, row, col, col], axis=-1)
    cos = jnp.concatenate([jnp.ones((lc, d_rot), F32), jnp.cos(ang)], axis=0)
    sin = jnp.concatenate([jnp.zeros((lc, d_rot), F32), jnp.sin(ang)], axis=0)
    return jnp.tile(cos, (1, reps)), jnp.tile(sin, (1, reps))


def _post_consts():
    s_b = np.kron(np.eye(GQA_HEADS, dtype=np.float32), np.full((HEAD_DIM, HEAD_DIM), 1.0 / HEAD_DIM, np.float32))
    t_b = np.tile(np.eye(HEAD_DIM, dtype=np.float32), (1, GQA_HEADS))
    r_b = _rot_matrix(GQ_W, HEAD_DIM)
    r_m = _rot_matrix(LANE, MLA_ROPE)
    rep = np.zeros((LANE, LANE), np.float32)
    for h in range(MLA_HEADS):
        rep[np.arange(MLA_ROPE), h * MLA_ROPE + np.arange(MLA_ROPE)] = 1.0
    dup = np.zeros((GK_W, 2 * GK_W), np.float32)
    for j in range(GQA_KV_HEADS):
        for e in range(2):
            dup[HEAD_DIM * j + np.arange(HEAD_DIM), 2 * HEAD_DIM * j + HEAD_DIM * e + np.arange(HEAD_DIM)] = 1.0
    return tuple(jnp.asarray(a) for a in (s_b, r_b, t_b, r_m, rep, dup))


def _heads_to_parts(w, first):
    r = w.shape[0]
    w3 = w.reshape(r, MLA_HEADS, -1)
    return jnp.concatenate([w3[:, :, :first].reshape(r, -1), w3[:, :, first:].reshape(r, -1)], axis=1)


def _parts_to_heads(w, first):
    r = w.shape[0]
    nf = MLA_HEADS * first
    return jnp.concatenate([w[:, :nf].reshape(r, MLA_HEADS, first), w[:, nf:].reshape(r, MLA_HEADS, -1)], axis=2).reshape(r, -1)


def _place():
    return lax.axis_index("x"), lax.axis_index("y"), lax.axis_index("c")


def all_gather(v, *, name, with_c):
    flips = [(dx, dy, dc) for dx in (0, 1) for dy in (0, 1) for dc in ((0, 1) if with_c else (0,))][1:]
    n = len(flips) + 1

    def body(v_ref, out_ref, send_sems, recv_sems, local_sem):
        mx, my, mc = _place()

        def slot(px, py, pc):
            return 4 * px + 2 * py + pc if with_c else 2 * px + py

        mine = pltpu.make_async_copy(v_ref, out_ref.at[slot(mx, my, mc)], local_sem)
        mine.start()
        sends = []
        for j, (dx, dy, dc) in enumerate(flips):
            peer = (mx ^ dx, my ^ dy, mc ^ dc)
            cp = pltpu.make_async_remote_copy(src_ref=v_ref, dst_ref=out_ref.at[slot(mx, my, mc)], send_sem=send_sems.at[j],
                                              recv_sem=recv_sems.at[j], device_id=peer, device_id_type=MESH)
            cp.start()
            sends.append(cp)
        for j, (dx, dy, dc) in enumerate(flips):
            peer = (mx ^ dx, my ^ dy, mc ^ dc)
            pltpu.make_async_remote_copy(src_ref=v_ref, dst_ref=out_ref.at[slot(*peer)], send_sem=send_sems.at[j],
                                         recv_sem=recv_sems.at[j], device_id=peer, device_id_type=MESH).wait_recv()
        for cp in sends:
            cp.wait_send()
        mine.wait()

    return pl.pallas_call(
        body, name=name, in_specs=[ANY], out_specs=ANY, out_shape=jax.ShapeDtypeStruct((n,) + v.shape, v.dtype),
        scratch_shapes=[pltpu.SemaphoreType.DMA((n - 1,)), pltpu.SemaphoreType.DMA((n - 1,)), pltpu.SemaphoreType.DMA(())],
    )(v)


def gather_shards(v, *, name):
    _, h, w = v.shape
    flips = [(1, 0), (0, 1), (1, 1)]

    def body(v_ref, out_ref, send_sems, recv_sems):
        mx, my, mc = _place()
        me = 2 * mx + my
        sib = (mx, my, 1 - mc)

        def copy(k, src, dst, to):
            return pltpu.make_async_remote_copy(src_ref=src, dst_ref=dst, send_sem=send_sems.at[k], recv_sem=recv_sems.at[k],
                                                device_id=to, device_id_type=MESH)

        first = [copy(j, v_ref.at[mc], out_ref.at[me, mc], (mx ^ dx, my ^ dy, mc)) for j, (dx, dy) in enumerate(flips)]
        for cp in first:
            cp.start()
        passed = []
        for j, (dx, dy) in enumerate(flips):
            theirs = out_ref.at[2 * (mx ^ dx) + (my ^ dy), mc]
            copy(j, v_ref.at[mc], theirs, (mx ^ dx, my ^ dy, mc)).wait_recv()
            fw = copy(3 + j, theirs, theirs, sib)
            fw.start()
            passed.append(fw)
        for j, (dx, dy) in enumerate(flips):
            other = out_ref.at[2 * (mx ^ dx) + (my ^ dy), 1 - mc]
            copy(3 + j, other, other, sib).wait_recv()
        for cp in first + passed:
            cp.wait_send()

    out = pl.pallas_call(
        body, name=name, in_specs=[ANY], out_specs=ANY, out_shape=jax.ShapeDtypeStruct((4, 2, h, w), v.dtype),
        scratch_shapes=[pltpu.SemaphoreType.DMA((6,)), pltpu.SemaphoreType.DMA((6,))],
    )(v)
    mx, my, _ = _place()
    return lax.dynamic_update_slice(out, v[None], (2 * mx + my, 0, 0, 0))


def pair_exchange_halves(g, *, name):
    n, _, h, w = g.shape

    def body(g_ref, out_ref, send_sems, recv_sems):
        mx, my, mc = _place()
        sib = (mx, my, 1 - mc)
        cps = [pltpu.make_async_remote_copy(src_ref=g_ref.at[s, 1 - mc], dst_ref=out_ref.at[s], send_sem=send_sems.at[s],
                                            recv_sem=recv_sems.at[s], device_id=sib, device_id_type=MESH) for s in range(n)]
        for cp in cps:
            cp.start()
        for cp in cps:
            cp.wait_recv()
        for cp in cps:
            cp.wait_send()

    return pl.pallas_call(
        body, name=name, in_specs=[ANY], out_specs=ANY, out_shape=jax.ShapeDtypeStruct((n, h, w), g.dtype),
        scratch_shapes=[pltpu.SemaphoreType.DMA((n,)), pltpu.SemaphoreType.DMA((n,))],
    )(g)


def all_to_all_xy(v, *, name):
    def body(v_ref, out_ref, send_sems, recv_sems):
        mx, my, mc = _place()
        me = 2 * mx + my
        flips = [(1, 0), (0, 1), (1, 1)]
        sends = []
        for j, (dx, dy) in enumerate(flips):
            px, py = mx ^ dx, my ^ dy
            cp = pltpu.make_async_remote_copy(src_ref=v_ref.at[2 * px + py], dst_ref=out_ref.at[me], send_sem=send_sems.at[j],
                                              recv_sem=recv_sems.at[j], device_id=(px, py, mc), device_id_type=MESH)
            cp.start()
            sends.append(cp)
        for j, (dx, dy) in enumerate(flips):
            px, py = mx ^ dx, my ^ dy
            pltpu.make_async_remote_copy(src_ref=v_ref.at[me], dst_ref=out_ref.at[2 * px + py], send_sem=send_sems.at[j],
                                         recv_sem=recv_sems.at[j], device_id=(px, py, mc), device_id_type=MESH).wait_recv()
        for cp in sends:
            cp.wait_send()

    out = pl.pallas_call(
        body, name=name, in_specs=[ANY], out_specs=ANY, out_shape=jax.ShapeDtypeStruct(v.shape, v.dtype),
        scratch_shapes=[pltpu.SemaphoreType.DMA((3,)), pltpu.SemaphoreType.DMA((3,))],
    )(v)
    mx, my, _ = _place()
    me = 2 * mx + my
    return lax.dynamic_update_slice(out, lax.dynamic_slice_in_dim(v, me, 1, axis=0), (me, 0, 0))


def pair_all_gather(v, *, name):
    def body(v_ref, out_ref, send_sem, recv_sem):
        mx, my, mc = _place()
        cp = pltpu.make_async_remote_copy(src_ref=v_ref, dst_ref=out_ref.at[mc], send_sem=send_sem, recv_sem=recv_sem,
                                          device_id=(mx, my, 1 - mc), device_id_type=MESH)
        cp.start()
        pltpu.make_async_remote_copy(src_ref=v_ref, dst_ref=out_ref.at[1 - mc], send_sem=send_sem, recv_sem=recv_sem,
                                     device_id=(mx, my, 1 - mc), device_id_type=MESH).wait_recv()
        cp.wait_send()

    out = pl.pallas_call(
        body, name=name, in_specs=[ANY], out_specs=ANY, out_shape=jax.ShapeDtypeStruct((2,) + v.shape, v.dtype),
        scratch_shapes=[pltpu.SemaphoreType.DMA(()), pltpu.SemaphoreType.DMA(())],
    )(v)
    return lax.dynamic_update_slice(out, v[None], (_place()[2], 0, 0))


def gather_ffn(wl, *, name):
    nl, nblk, cs, d = wl.shape
    assert nl == 2
    flips = [(1, 0), (0, 1), (1, 1)]

    def body(v_ref, out_ref, send_sems, recv_sems):
        mx, my, mc = _place()
        me = 2 * mx + my
        sib = (mx, my, 1 - mc)

        def copy(k, src, dst, to):
            return pltpu.make_async_remote_copy(src_ref=src, dst_ref=dst, send_sem=send_sems.at[k], recv_sem=recv_sems.at[k],
                                                device_id=to, device_id_type=MESH)

        first = [copy(j, v_ref.at[mc], out_ref.at[mc, me], (mx ^ dx, my ^ dy, mc)) for j, (dx, dy) in enumerate(flips)]
        for cp in first:
            cp.start()
        passed = []
        for j, (dx, dy) in enumerate(flips):
            theirs = out_ref.at[mc, 2 * (mx ^ dx) + (my ^ dy)]
            copy(j, v_ref.at[mc], theirs, (mx ^ dx, my ^ dy, mc)).wait_recv()
            fw = copy(3 + j, theirs, theirs, sib)
            fw.start()
            passed.append(fw)
        for j, (dx, dy) in enumerate(flips):
            other = out_ref.at[1 - mc, 2 * (mx ^ dx) + (my ^ dy)]
            copy(3 + j, other, other, sib).wait_recv()
        for cp in first + passed:
            cp.wait_send()

    out = pl.pallas_call(
        body, name=name, in_specs=[ANY], out_specs=ANY, out_shape=jax.ShapeDtypeStruct((nl, 4, nblk, cs, d), wl.dtype),
        scratch_shapes=[pltpu.SemaphoreType.DMA((6,)), pltpu.SemaphoreType.DMA((6,))],
    )(wl)
    mx, my, _ = _place()
    return lax.dynamic_update_slice(out, wl[:, None], (0, 2 * mx + my, 0, 0, 0))


def reduce_ffn(g0, g1, *, name):
    nt = len(g0)
    nsh, cs, d = g0[0].shape
    flips = [(1, 0), (0, 1), (1, 1)]
    mx, my, mc = _place()
    me = 2 * mx + my
    c_idx = jnp.reshape(mc, (1,)).astype(jnp.int32)

    def pair_body(*refs):
        ins0, ins1, outs = refs[:nt], refs[nt:2 * nt], refs[2 * nt:3 * nt]
        send_sems, recv_sems = refs[3 * nt:]
        kx, ky, kc = _place()
        sib = (kx, ky, 1 - kc)
        for c in range(2):
            @pl.when(kc == c)
            def _(c=c):
                mine_out = (ins1, ins0)[c]
                cps = [pltpu.make_async_remote_copy(src_ref=mine_out[t], dst_ref=outs[t], send_sem=send_sems.at[t],
                                                    recv_sem=recv_sems.at[t], device_id=sib, device_id_type=MESH) for t in range(nt)]
                for cp in cps:
                    cp.start()
                for cp in cps:
                    cp.wait_recv()
                for cp in cps:
                    cp.wait_send()

    from_pair = pl.pallas_call(
        pair_body, name=name + "_pair", in_specs=[ANY] * (2 * nt), out_specs=[ANY] * nt,
        out_shape=[jax.ShapeDtypeStruct((nsh, cs, d), F32)] * nt,
        scratch_shapes=[pltpu.SemaphoreType.DMA((nt,)), pltpu.SemaphoreType.DMA((nt,))],
    )(*g0, *g1)

    tr = _row_tile(cs, 64)

    def add_body(c_ref, *refs):
        for t in range(nt):
            mine = jnp.where(c_ref[0] == 0, refs[t][...], refs[nt + t][...])
            refs[3 * nt + t][...] = (mine + refs[2 * nt + t][...]).astype(BF16)

    spec = pl.BlockSpec((None, tr, d), lambda s, i, c_ref: (s, i, 0))
    chip_sum = pl.pallas_call(
        add_body, name=name + "_pair_add",
        grid_spec=pltpu.PrefetchScalarGridSpec(num_scalar_prefetch=1, grid=(nsh, cs // tr), in_specs=[spec] * (3 * nt),
                                               out_specs=[spec] * nt),
        out_shape=[jax.ShapeDtypeStruct((nsh, cs, d), BF16)] * nt, compiler_params=_cp("parallel", "parallel"),
    )(c_idx, *g0, *g1, *from_pair)

    def xy_body(*refs):
        ins, outs = refs[:nt], refs[nt:2 * nt]
        send_sems, recv_sems = refs[2 * nt:]
        kx, ky, kc = _place()
        k_me = 2 * kx + ky
        sends = []
        for j, (dx, dy) in enumerate(flips):
            px, py = kx ^ dx, ky ^ dy
            for t in range(nt):
                cp = pltpu.make_async_remote_copy(src_ref=ins[t].at[2 * px + py], dst_ref=outs[t].at[k_me],
                                                  send_sem=send_sems.at[j * nt + t], recv_sem=recv_sems.at[j * nt + t],
                                                  device_id=(px, py, kc), device_id_type=MESH)
                cp.start()
                sends.append(cp)
        for j, (dx, dy) in enumerate(flips):
            px, py = kx ^ dx, ky ^ dy
            for t in range(nt):
                pltpu.make_async_remote_copy(src_ref=ins[t].at[k_me], dst_ref=outs[t].at[2 * px + py],
                                             send_sem=send_sems.at[j * nt + t], recv_sem=recv_sems.at[j * nt + t],
                                             device_id=(px, py, kc), device_id_type=MESH).wait_recv()
        for cp in sends:
            cp.wait_send()

    from_xy = pl.pallas_call(
        xy_body, name=name + "_xy", in_specs=[ANY] * nt, out_specs=[ANY] * nt,
        out_shape=[jax.ShapeDtypeStruct((nsh, cs, d), BF16)] * nt,
        scratch_shapes=[pltpu.SemaphoreType.DMA((3 * nt,)), pltpu.SemaphoreType.DMA((3 * nt,))],
    )(*chip_sum)
    from_xy = [lax.dynamic_update_slice(o, lax.dynamic_slice_in_dim(v, me, 1, axis=0), (me, 0, 0)) for o, v in zip(from_xy, chip_sum)]

    def sum_body(*refs):
        for t in range(nt):
            acc = refs[t][0].astype(F32)
            for s in range(1, nsh):
                acc = acc + refs[t][s].astype(F32)
            refs[nt + t][...] = acc

    reduced = pl.pallas_call(
        sum_body, name=name + "_xy_add", grid=(cs // tr,), in_specs=[pl.BlockSpec((nsh, tr, d), lambda i: (0, i, 0))] * nt,
        out_specs=[pl.BlockSpec((tr, d), lambda i: (i, 0))] * nt, out_shape=[jax.ShapeDtypeStruct((cs, d), F32)] * nt,
        compiler_params=_cp("parallel"),
    )(*from_xy)

    def share_body(*refs):
        ins, outs = refs[:nt], refs[nt:2 * nt]
        send_sems, recv_sems = refs[2 * nt:]
        kx, ky, kc = _place()
        sib = (kx, ky, 1 - kc)
        cps = [pltpu.make_async_remote_copy(src_ref=ins[t], dst_ref=outs[t].at[kc], send_sem=send_sems.at[t],
                                            recv_sem=recv_sems.at[t], device_id=sib, device_id_type=MESH) for t in range(nt)]
        for cp in cps:
            cp.start()
        for t in range(nt):
            pltpu.make_async_remote_copy(src_ref=ins[t], dst_ref=outs[t].at[1 - kc], send_sem=send_sems.at[t],
                                         recv_sem=recv_sems.at[t], device_id=sib, device_id_type=MESH).wait_recv()
        for cp in cps:
            cp.wait_send()

    both = pl.pallas_call(
        share_body, name=name + "_share", in_specs=[ANY] * nt, out_specs=[ANY] * nt,
        out_shape=[jax.ShapeDtypeStruct((2, cs, d), F32)] * nt,
        scratch_shapes=[pltpu.SemaphoreType.DMA((nt,)), pltpu.SemaphoreType.DMA((nt,))],
    )(*reduced)
    return [lax.dynamic_update_slice(o, v[None], (mc, 0, 0)) for o, v in zip(both, reduced)]


def add_kept_half(g, r, c_idx, *, name, out_dtype):
    n, _, h, w = g.shape
    th = _row_tile(h)

    def body(c_ref, g_ref, r_ref, o_ref):
        o_ref[...] = (g_ref[...] + r_ref[...]).astype(o_ref.dtype)

    return pl.pallas_call(
        body, name=name,
        grid_spec=pltpu.PrefetchScalarGridSpec(
            num_scalar_prefetch=1, grid=(n, h // th),
            in_specs=[pl.BlockSpec((None, None, th, w), lambda s, i, c_ref: (s, c_ref[0], i, 0)),
                      pl.BlockSpec((None, th, w), lambda s, i, c_ref: (s, i, 0))],
            out_specs=pl.BlockSpec((None, th, w), lambda s, i, c_ref: (s, i, 0))),
        out_shape=jax.ShapeDtypeStruct((n, h, w), out_dtype), compiler_params=_cp("parallel", "parallel"),
    )(c_idx, g, r)


def sum_slots(v, *, name):
    n, rows, w = v.shape
    tr = _row_tile(rows, 256)

    def body(v_ref, o_ref):
        acc = v_ref[0].astype(F32)
        for s in range(1, n):
            acc = acc + v_ref[s].astype(F32)
        o_ref[...] = acc

    return pl.pallas_call(body, name=name, grid=(rows // tr,), in_specs=[pl.BlockSpec((n, tr, w), lambda i: (0, i, 0))],
                          out_specs=pl.BlockSpec((tr, w), lambda i: (i, 0)), out_shape=jax.ShapeDtypeStruct((rows, w), F32),
                          compiler_params=_cp("parallel"))(v)


def ada_fwd(c_rows, w_ada, b_shard, *, name):
    nl, d, ncol = w_ada.shape
    rows = c_rows.shape[0]
    tn = _tile(ncol, (768, 512, 256, 128))

    def body(c_ref, w_ref, b_ref, o_ref):
        o_ref[...] = jnp.dot(jax.nn.silu(c_ref[...]), w_ref[...], precision=HI, preferred_element_type=F32) + b_ref[...]

    return pl.pallas_call(
        body, name=name, grid=(nl, ncol // tn),
        in_specs=[pl.BlockSpec((rows, d), lambda l, j: (0, 0)), pl.BlockSpec((None, d, tn), lambda l, j: (l, 0, j)),
                  pl.BlockSpec((None, 1, tn), lambda l, j: (l, 0, j))],
        out_specs=pl.BlockSpec((None, rows, tn), lambda l, j: (l, 0, j)),
        out_shape=jax.ShapeDtypeStruct((nl, rows, ncol), F32), compiler_params=_cp("parallel", "parallel"),
    )(c_rows, w_ada, b_shard)


def ada_bwd(c_rows, w_ada, dm_shard, dm_full, n_ex, *, name):
    nl, d, ncol = w_ada.shape
    rows = c_rows.shape[0]
    tn = _tile(ncol, (768, 512, 256, 128))
    nj = ncol // tn

    def body(c_ref, w_ref, dm_ref, dmf_ref, gw_ref, gb_ref, dc_ref, dact_ref):
        l, j = pl.program_id(0), pl.program_id(1)
        act, act_vjp = jax.vjp(jax.nn.silu, c_ref[...])
        gw_ref[...] = lax.dot_general(act, dm_ref[...], _TN, precision=HI, preferred_element_type=F32)
        gb_ref[...] = jnp.sum(dmf_ref[...], axis=0, keepdims=True)
        part = lax.dot_general(dm_ref[...], w_ref[...], _NT, precision=HI, preferred_element_type=F32)

        @pl.when((l == 0) & (j == 0))
        def _():
            dact_ref[...] = part

        @pl.when((l > 0) | (j > 0))
        def _():
            dact_ref[...] += part

        @pl.when((l == nl - 1) & (j == nj - 1))
        def _():
            dc, = act_vjp(dact_ref[...])
            dc_ref[...] = jnp.sum(dc[n_ex:, :], axis=0, keepdims=True)

    return pl.pallas_call(
        body, name=name, grid=(nl, nj),
        in_specs=[pl.BlockSpec((rows, d), lambda l, j: (0, 0)), pl.BlockSpec((None, d, tn), lambda l, j: (l, 0, j)),
                  pl.BlockSpec((None, rows, tn), lambda l, j: (l, 0, j)),
                  pl.BlockSpec((None, rows, dm_full.shape[-1]), lambda l, j: (l, 0, 0))],
        out_specs=[pl.BlockSpec((None, d, tn), lambda l, j: (l, 0, j)),
                   pl.BlockSpec((None, 1, dm_full.shape[-1]), lambda l, j: (l, 0, 0)),
                   pl.BlockSpec((1, d), lambda l, j: (0, 0))],
        out_shape=[jax.ShapeDtypeStruct((nl, d, ncol), F32), jax.ShapeDtypeStruct((nl, 1, dm_full.shape[-1]), F32),
                   jax.ShapeDtypeStruct((1, d), F32)],
        scratch_shapes=[pltpu.VMEM((rows, d), F32)], compiler_params=_cp("arbitrary", "arbitrary"),
    )(c_rows, w_ada, dm_shard, dm_full)


def adamw(w, g, m, v, *, name):
    shape = w.shape
    cols = shape[-1]
    rows = int(np.prod(shape[:-1])) if len(shape) > 1 else 1
    tr = _row_tile(rows, 256)

    def body(w_ref, g_ref, m_ref, v_ref, d_ref, nm_ref, nv_ref):
        gg = g_ref[...]
        nm = ADAM_B1 * m_ref[...] + (1.0 - ADAM_B1) * gg
        nv = ADAM_B2 * v_ref[...] + (1.0 - ADAM_B2) * jnp.square(gg)
        m_hat = nm / (1.0 - ADAM_B1 ** ADAM_STEP)
        v_hat = nv / (1.0 - ADAM_B2 ** ADAM_STEP)
        d_ref[...] = -ADAM_LR * (m_hat / (jnp.sqrt(v_hat) + ADAM_EPS) + ADAM_WD * w_ref[...])
        nm_ref[...] = nm
        nv_ref[...] = nv

    spec = pl.BlockSpec((tr, cols), lambda i: (i, 0))
    out = pl.pallas_call(body, name=name, grid=(rows // tr,), in_specs=[spec] * 4, out_specs=[spec] * 3,
                         out_shape=[jax.ShapeDtypeStruct((rows, cols), F32)] * 3, compiler_params=_cp("parallel"),
                         )(*[a.reshape(rows, cols) for a in (w, g, m, v)])
    return tuple(o.reshape(shape) for o in out)


def local_step(h0, target, mods, lw, wf, small, *, lc):
    nb, t, d = h0.shape
    nt, nct = t // TM, lc // TM
    s_len = t - lc
    nl = len(lw)
    nsh = wf.shape[1]
    consts = _post_consts()
    cos_b, sin_b = _rope_tables(s_len, lc, HEAD_DIM, GQA_HEADS)
    cos_m, sin_m = _rope_tables(s_len, lc, MLA_ROPE, MLA_HEADS)
    rc = functools.partial(rowcall, nb=nb, nt=nt, nct=nct)
    flat = lambda a: a.reshape(nb * t, a.shape[-1])
    unflat = lambda a: a.reshape(nb, t, a.shape[-1])
    vec = lambda a: a.reshape(1, -1)

    def norm_first(h, g, shift, scale, tag):
        n, = rc(tag + "_norm", lambda _, *a: (f_normmod(*a),), [(h, 'tok'), (vec(g), 'full'), (shift, 'mod'), (scale, 'mod')],
                [('tok', d, BF16)])
        return n

    def res_norm(h, y, gate, coef, g, shift, scale, tag):
        def fn(_, hh, yy, gt, gn, sh, sc):
            h2 = hh + coef * gt * yy
            return h2, f_normmod(h2, gn, sh, sc)

        return rc(tag + "_res_norm", fn, [(h, 'tok'), (y, 'tok'), (gate, 'mod'), (vec(g), 'full'), (shift, 'mod'), (scale, 'mod')],
                  [('tok', d, F32), ('tok', d, BF16)])

    def res_last(h, y, gate, coef, tag):
        h2, = rc(tag + "_res", lambda _, hh, yy, gt: (hh + coef * gt * yy,), [(h, 'tok'), (y, 'tok'), (gate, 'mod')], [('tok', d, F32)])
        return h2

    def res_bwd_last(dh2, y, gate, coef, tag):
        return rc(tag + "_res_bwd", lambda _, dd, yy, gt: (coef * gt * dd, jnp.sum(coef * yy * dd, axis=0, keepdims=True)),
                  [(dh2, 'tok'), (y, 'tok'), (gate, 'mod')], [('tok', d, BF16), ('mod', d)])

    def norm_bwd_first(h, g, shift, scale, dn, dres, tag):
        def fn(_, hh, gn, sh, sc, dnn, dr):
            dh, dg, dsh, dsc = jax.vjp(f_normmod, hh, gn, sh, sc)[1](dnn)
            return dh + dr, dg, dsh, dsc

        return rc(tag + "_norm_bwd", fn, [(h, 'tok'), (vec(g), 'full'), (shift, 'mod'), (scale, 'mod'), (dn, 'tok'), (dres, 'tok')],
                  [('tok', d, F32), ('full', (1, d)), ('mod', d), ('mod', d)])

    def norm_bwd_res_bwd(h, g, shift, scale, dn, dres, y_prev, gate_prev, coef_prev, tag):
        def fn(_, hh, gn, sh, sc, dnn, dr, yy, gt):
            dh, dg, dsh, dsc = jax.vjp(f_normmod, hh, gn, sh, sc)[1](dnn)
            dh = dh + dr
            return dh, dg, dsh, dsc, coef_prev * gt * dh, jnp.sum(coef_prev * yy * dh, axis=0, keepdims=True)

        return rc(tag + "_norm_bwd", fn,
                  [(h, 'tok'), (vec(g), 'full'), (shift, 'mod'), (scale, 'mod'), (dn, 'tok'), (dres, 'tok'), (y_prev, 'tok'), (gate_prev, 'mod')],
                  [('tok', d, F32), ('full', (1, d)), ('mod', d), ('mod', d), ('tok', d, BF16), ('mod', d)])

    def ffn_fwd(n, l, base, tag):
        gg, uu, act = ffn_up(flat(n), wf, l, base, name=tag + "_up")
        return unflat(ffn_down(act, wf, l, base, name=tag + "_down")), (n, gg, uu, act)

    def ffn_bwd(dy, saved, l, base, tag):
        n, gg, uu, act = saved
        dw_d = ffn_dw(act, flat(dy), nsh, name=tag + "_down_dw")
        dgg, duu = ffn_down_bwd(flat(dy), gg, uu, wf, l, base, name=tag + "_down_dx")
        dw_g = ffn_dw(dgg, flat(n), nsh, name=tag + "_gate_dw")
        dw_u = ffn_dw(duu, flat(n), nsh, name=tag + "_up_dw")
        return unflat(ffn_up_bwd(dgg, duu, wf, l, base, name=tag + "_up_dx")), [dw_g, dw_u, dw_d]

    def post_ins(p, sm, w):
        return [(p, ('tokc', MAIN_PAD, 0)), (cos_b, 'pos'), (sin_b, 'pos'), (cos_m, 'pos'), (sin_m, 'pos'),
                (vec(sm['gqa_q_norm']), 'full'), (vec(sm['gqa_k_norm']), 'full'), (vec(sm['mla_q_norm']), 'full'),
                (vec(sm['mla_kv_norm']), 'full'), (w['w_uq'], 'full'), (w['w_ukv'], 'full')] + [(c, 'full') for c in consts]

    def mix_fwd(n, sm, w, ctx_q, tag):
        p = unflat(mm_resident(flat(n), w['w_in'], out_dtype=BF16, name=tag + "_in"))
        parts = rc(tag + "_post", lambda _, pp, *a: f_post(pp.astype(F32), *a), post_ins(p, sm, w),
                   [('tok', wd, BF16) for wd in POST_WIDTHS])
        aq, ak, av, bq, bk, bv, mq, mk, mv = parts
        bias = na_expand_bias(sm['na_rel_bias'], tag + "_bias")
        o_a, lse_a = na_fwd(aq, ak, av, bias, lc=lc, ctx_q=ctx_q, name=tag + "_na")
        o_b, lse_b = gqa_fwd(bq, bk, bv, lc=lc, ctx_q=ctx_q, name=tag + "_gqa")
        o_m, lse_m = mla_fwd(mq, mk, mv, lc=lc, ctx_q=ctx_q, name=tag + "_mla")
        fo = [o_a, o_b, o_m]
        ys = [unflat(mm_resident(flat(o), w[k], out_dtype=BF16, name=tag + "_br" + k[-1])) for o, k in zip(fo, ('w_a', 'w_b', 'w_c'))]
        gcols = [(p, ('tokc', d, MAIN_PAD // d + j)) for j in range(3)]
        y, = rc(tag + "_merge", lambda _, *a: (f_merge(*[v.astype(F32) for v in a]),), gcols + [(v, 'tok') for v in ys],
                [('tok', d, BF16)])
        z = unflat(mm_resident(flat(y), w['w_o'], name=tag + "_out"))
        saved = (n, p, (aq, ak, av, lse_a, bias), (bq, bk, bv, lse_b), (mq, mk, mv, lse_m), fo, ys, y)
        return z, saved

    def mix_bwd(dz, saved, sm, w, ctx_q, tag):
        n, p, (aq, ak, av, lse_a, bias), (bq, bk, bv, lse_b), (mq, mk, mv, lse_m), fo, ys, y = saved
        dw_o = mm(flat(y), flat(dz), ta=True, name=tag + "_out_dw")
        dy = unflat(mm_resident(flat(dz), w['w_o'], tb=True, name=tag + "_out_dx"))
        gcols = [(p, ('tokc', d, MAIN_PAD // d + j)) for j in range(3)]

        def merge_bwd(_, ga, gb, gm, ya, yb, ym, dyy):
            dga, dgb, dgm, dya, dyb, dym = jax.vjp(f_merge, *[v.astype(F32) for v in (ga, gb, gm, ya, yb, ym)])[1](dyy)
            return dya, dyb, dym, jnp.concatenate([dga, dgb, dgm], axis=-1)

        dya, dyb, dym, dgl = rc(tag + "_merge_bwd", merge_bwd, gcols + [(v, 'tok') for v in ys] + [(dy, 'tok')],
                                [('tok', d, BF16)] * 3 + [('tok', 3 * d, BF16)])
        dws, dos = {}, []
        for o, dyk, k in zip(fo, (dya, dyb, dym), ('w_a', 'w_b', 'w_c')):
            dws[k] = mm(flat(o), flat(dyk), ta=True, name=tag + "_br" + k[-1] + "_dw")
            dos.append(unflat(mm_resident(flat(dyk), w[k], tb=True, out_dtype=BF16, name=tag + "_br" + k[-1] + "_dx")))
        do_a, do_b, do_m = dos
        daq, dak, dav, dbias = na_bwd(aq, ak, av, bias, lse_a, do_a, lc=lc, ctx_q=ctx_q, name=tag + "_na_bwd")
        dbq, dbk, dbv = gqa_bwd(bq, bk, bv, lse_b, do_b, lc=lc, ctx_q=ctx_q, name=tag + "_gqa_bwd")
        dmq, dmk, dmv = mla_bwd(mq, mk, mv, lse_m, do_m, lc=lc, ctx_q=ctx_q, name=tag + "_mla_bwd")
        d_rel = na_reduce_bias(dbias, tag + "_relb")
        cots = [daq, dak, dav, dbq, dbk, dbv, dmq, dmk, dmv]
        ins = post_ins(p, sm, w)
        n_in = len(ins)

        def post_bwd(_, *a):
            prim, cot, dgl_v = a[:11], list(a[n_in:n_in + N_POST]), a[-1]
            for j in POST_QK:
                cot[j] = cot[j] * LN2
            outs = jax.vjp(lambda pp, qn, kn, mqn, mkvn, wuq, wukv: f_post(pp, *prim[1:5], qn, kn, mqn, mkvn, wuq, wukv, *a[11:n_in]),
                           prim[0].astype(F32), *prim[5:11])[1](tuple(cot))
            return (jnp.concatenate([outs[0].astype(BF16), dgl_v], axis=-1),) + tuple(outs[1:])

        res = rc(tag + "_post_bwd", post_bwd, ins + [(cv, 'tok') for cv in cots] + [(dgl, 'tok')],
                 [('tok', MAIN_PAD + 3 * d, BF16), ('full', (1, HEAD_DIM)), ('full', (1, HEAD_DIM)), ('full', (1, MLA_Q_RANK)),
                  ('full', (1, MLA_KV_RANK)), ('full', w['w_uq'].shape), ('full', w['w_ukv'].shape)])
        dp, dqn, dkn, dmqn, dmkvn, dw_uq, dw_ukv = res
        dw_in = ffn_dw(flat(dp), flat(n), 4, name=tag + "_in_dw").reshape(-1, d)
        dn = unflat(mm_resident(flat(dp), w['w_in'], tb=True, name=tag + "_in_dx"))
        dsm = {'na_rel_bias': d_rel, 'gqa_q_norm': dqn.reshape(-1), 'gqa_k_norm': dkn.reshape(-1),
               'mla_q_norm': dmqn.reshape(-1), 'mla_kv_norm': dmkvn.reshape(-1)}
        dwl = {'w_in': dw_in, 'w_uq': dw_uq, 'w_ukv': dw_ukv, 'w_o': dw_o, **dws}
        return dn, dsm, dwl

    subs = [(l, kind, gain, coef) for l in range(nl)
            for kind, gain, coef in (('ffn1', 'ffn1_norm', 0.5), ('mix', 'mix_norm', 1.0), ('ffn2', 'ffn2_norm', 0.5))]
    ns = len(subs)
    sms = [{k: small[k][l] for k in SMALL_LAYER} for l in range(nl)]

    def params(k):
        l, _, gain, _ = subs[k]
        j = 3 * (k % 3)
        return small[gain][l], mods[l][j], mods[l][j + 1], mods[l][j + 2]

    def tag_of(k):
        return f"l{subs[k][0]}_{subs[k][1]}"

    h = h0
    g0, sh0, sc0, _ = params(0)
    n = norm_first(h, g0, sh0, sc0, tag_of(0))
    h_in, core_out, saved = [None] * ns, [None] * ns, [None] * ns
    for k, (l, kind, _, coef) in enumerate(subs):
        h_in[k] = h
        if kind == 'mix':
            core_out[k], saved[k] = mix_fwd(n, sms[l], lw[l], l + 1 < nl, tag_of(k))
        else:
            core_out[k], saved[k] = ffn_fwd(n, l, 0 if kind == 'ffn1' else 3, tag_of(k))
        gate = params(k)[3]
        if k + 1 < ns:
            gn, shn, scn, _ = params(k + 1)
            h, n = res_norm(h, core_out[k], gate, coef, gn, shn, scn, tag_of(k))
        else:
            h = res_last(h, core_out[k], gate, coef, tag_of(k))

    def final(is_ctx, hh, gg, tgt):
        def loss_fn(hv, gv):
            return 0.5 * jnp.sum(jnp.mean(jnp.square(_rms(hv, gv) - tgt), axis=-1))

        keep = jnp.where(is_ctx, 0.0, 1.0)
        loss, (dh, dg) = jax.value_and_grad(loss_fn, argnums=(0, 1))(hh, gg)
        return dh * keep, jnp.full((1, LANE), loss * keep, F32), dg * keep

    dh, loss, dg_final = rc("final_loss", final, [(h, 'tok'), (vec(small['final_norm']), 'full'), (target, 'lat')],
                            [('tok', d, F32), ('full', (1, LANE)), ('full', (1, d))])

    dsmall = {k: [None] * nl for k in SMALL_LAYER}
    dmods, dlw, dwf = [[None] * N_MOD for _ in range(nl)], [None] * nl, [[None] * 6 for _ in range(nl)]
    l_last, _, _, coef_last = subs[-1]
    dcore, dmods[l_last][8] = res_bwd_last(dh, core_out[-1], params(ns - 1)[3], coef_last, tag_of(ns - 1))
    for k in reversed(range(ns)):
        l, kind, gain, _ = subs[k]
        j = 3 * (k % 3)
        if kind == 'mix':
            dn, dsm, dlw[l] = mix_bwd(dcore, saved[k], sms[l], lw[l], l + 1 < nl, tag_of(k))
            for name, val in dsm.items():
                dsmall[name][l] = val
        else:
            base = 0 if kind == 'ffn1' else 3
            dn, dwf[l][base:base + 3] = ffn_bwd(dcore, saved[k], l, base, tag_of(k))
        g, shift, scale, _ = params(k)
        if k > 0:
            lp, _, _, coef_prev = subs[k - 1]
            dh, dg, dmods[l][j], dmods[l][j + 1], dcore, dmods[lp][3 * ((k - 1) % 3) + 2] = norm_bwd_res_bwd(
                h_in[k], g, shift, scale, dn, dh, core_out[k - 1], params(k - 1)[3], coef_prev, tag_of(k))
        else:
            dh, dg, dmods[l][j], dmods[l][j + 1] = norm_bwd_first(h_in[k], g, shift, scale, dn, dh, tag_of(k))
        dsmall[gain][l] = dg.reshape(d)
    dsmall = {k: jnp.stack(v) for k, v in dsmall.items()}
    dsmall['final_norm'] = dg_final.reshape(d)
    return loss, dh, dmods, dlw, dwf, dsmall


def _pack(parts, pad_rows):
    flat, where, off = [], [], 0
    for a in parts:
        n = _ceil_to(a.size, PACK_W)
        flat.append(jnp.pad(a.reshape(-1), (0, n - a.size)))
        where.append((off, n // PACK_W))
        off += n // PACK_W
    total = _ceil_to(off, pad_rows)
    if total > off:
        flat.append(jnp.zeros(((total - off) * PACK_W,), flat[0].dtype))
    return jnp.concatenate(flat).reshape(total, PACK_W), where


def _unpack(buf, where, shape):
    off, rows = where
    return buf[off:off + rows].reshape(-1)[:int(np.prod(shape))].reshape(shape)


def layer_weights(full, l):
    wi = full['w_in'][l]
    d = wi.shape[0]
    return {
        'w_in': jnp.concatenate([wi[:, :MAIN_W], jnp.zeros((d, MAIN_PAD - MAIN_W), wi.dtype), wi[:, MAIN_W:]], axis=1),
        'w_uq': _heads_to_parts(full['mla_w_uq'][l], MLA_NOPE).astype(F32),
        'w_ukv': _heads_to_parts(full['mla_w_ukv'][l], MLA_NOPE).astype(F32),
        'w_a': full['w_branch_a'][l], 'w_b': full['w_branch_b'][l], 'w_c': full['w_branch_c'][l], 'w_o': full['w_out'][l]}


def layer_grads_by_name(dlw):
    per_name = {k: [] for k, _ in BIG}
    for g in dlw:
        per_name['w_in'].append(jnp.concatenate([g['w_in'][:MAIN_W], g['w_in'][MAIN_PAD:]], axis=0))
        per_name['mla_w_uq'].append(_parts_to_heads(g['w_uq'], MLA_NOPE))
        per_name['mla_w_ukv'].append(_parts_to_heads(g['w_ukv'], MLA_NOPE))
        per_name['w_branch_a'].append(g['w_a'])
        per_name['w_branch_b'].append(g['w_b'])
        per_name['w_branch_c'].append(g['w_c'])
        per_name['w_out'].append(g['w_o'])
    return per_name


def kernel(x, c, ctx, c_ctx, w_ada, b_ada, ffn1_norm, ffn1_w_gate, ffn1_w_up, ffn1_w_down, mix_norm, w_in, na_rel_bias, gqa_q_norm, gqa_k_norm, mla_q_norm, mla_kv_norm, mla_w_uq, mla_w_ukv, w_branch_a, w_branch_b, w_branch_c, w_out, ffn2_norm, ffn2_w_gate, ffn2_w_up, ffn2_w_down, final_norm, loss_target, m_c_ctx, m_w_ada, m_b_ada, m_ffn1_norm, m_ffn1_w_gate, m_ffn1_w_up, m_ffn1_w_down, m_mix_norm, m_w_in, m_na_rel_bias, m_gqa_q_norm, m_gqa_k_norm, m_mla_q_norm, m_mla_kv_norm, m_mla_w_uq, m_mla_w_ukv, m_w_branch_a, m_w_branch_b, m_w_branch_c, m_w_out, m_ffn2_norm, m_ffn2_w_gate, m_ffn2_w_up, m_ffn2_w_down, m_final_norm, v_c_ctx, v_w_ada, v_b_ada, v_ffn1_norm, v_ffn1_w_gate, v_ffn1_w_up, v_ffn1_w_down, v_mix_norm, v_w_in, v_na_rel_bias, v_gqa_q_norm, v_gqa_k_norm, v_mla_q_norm, v_mla_kv_norm, v_mla_w_uq, v_mla_w_ukv, v_w_branch_a, v_w_branch_b, v_w_branch_c, v_w_out, v_ffn2_norm, v_ffn2_w_gate, v_ffn2_w_up, v_ffn2_w_down, v_final_norm):
    args = locals()
    wts = {k: args[k] for k in WEIGHTS}
    mom = {k: args['m_' + k] for k in WEIGHTS}
    var = {k: args['v_' + k] for k in WEIGHTS}
    nb, s_len, d = x.shape
    lc = ctx.shape[1]
    nl = w_ada.shape[0]
    nsh, ndev = 4, 8
    mx, my, mc = _place()
    sidx = 2 * mx + my
    didx = 4 * mx + 2 * my + mc
    assert d % LANE == 0 and MAIN_PAD % d == 0 and lc % TQ == 0 and s_len % TQ == 0 and s_len // GRID_W >= NA_ROWS

    wpack, wwhere = _pack([wts[k].astype(BF16) for k, _ in BIG], 32)
    wall = gather_shards(wpack.reshape(2, -1, PACK_W), name="gather_weights").reshape(nsh, -1, PACK_W)
    full = {}
    for (k, ax), wh in zip(BIG, wwhere):
        shp = wts[k].shape
        parts = jnp.stack([_unpack(wall[s], wh, shp) for s in range(nsh)])
        if ax == 1:
            full[k] = parts.transpose(1, 2, 0, 3).reshape(nl, shp[1], nsh * shp[2])
        else:
            full[k] = parts.transpose(1, 0, 2, 3).reshape(nl, nsh * shp[1], shp[2])
    lw = [layer_weights(full, l) for l in range(nl)]
    wl = jnp.stack([(wts[k].transpose(0, 2, 1) if tr else wts[k]).astype(BF16) for k, tr in zip(FFN_NAMES, FFN_TRANSPOSED)], axis=1)
    wf = gather_ffn(wl, name="gather_ffn")

    n_ex = ndev * nb
    ncol = w_ada.shape[-1]
    c_all = all_gather(c, name="gather_cond", with_c=True).reshape(n_ex, d)
    c_rows = jnp.concatenate([c_all, jnp.broadcast_to(c_ctx[None], (n_ex, d))], axis=0)
    b_shard = lax.dynamic_slice_in_dim(b_ada, sidx * ncol, ncol, axis=1)[:, None, :]
    mod_sh = ada_fwd(c_rows, w_ada, b_shard, name="ada_fwd")
    mod_all = all_gather(mod_sh, name="gather_mod", with_c=False)
    mod_all = mod_all.transpose(1, 2, 0, 3).reshape(nl, 2 * n_ex, nsh * ncol)
    mod_x = lax.dynamic_slice_in_dim(mod_all, didx * nb, nb, axis=1)
    mod_c = jnp.broadcast_to(mod_all[:, n_ex:n_ex + 1], mod_x.shape)
    mods = [[jnp.stack([mod_c[l, :, j * d:(j + 1) * d], mod_x[l, :, j * d:(j + 1) * d]], axis=1)[:, :, None, :]
             for j in range(N_MOD)] for l in range(nl)]

    small = {k: wts[k] for k in SMALL_LAYER + ['final_norm']}
    h0 = jnp.concatenate([ctx, x], axis=1)
    loss_part, dh0, dmods, dlw, dwf, dsmall = local_step(h0, loss_target, mods, lw, wf, small, lc=lc)
    grad_x = dh0[:, lc:]

    dmod_mine = jnp.stack([jnp.concatenate([m[:, :, 0, :] for m in dmods[l]], axis=-1) for l in range(nl)])
    small_names = SMALL_LAYER + ['final_norm']
    spack, swhere = _pack([loss_part] + [dsmall[k] for k in small_names] + [dmod_mine], 8)
    sall = all_gather(spack, name="gather_small", with_c=True)
    ssum = sum_slots(sall, name="sum_small")
    loss = _unpack(ssum, swhere[0], (1, LANE))[0, 0]
    grads = {k: _unpack(ssum, wh, wts[k].shape) for k, wh in zip(small_names, swhere[1:])}
    off, rows = swhere[-1]
    dm_all = sall[:, off:off + rows].reshape(ndev, -1)[:, :dmod_mine.size].reshape((ndev,) + dmod_mine.shape)
    dm_all = dm_all.transpose(1, 3, 0, 2, 4).reshape(nl, 2, n_ex, N_MOD * d)
    dm_rows = jnp.concatenate([dm_all[:, 1], dm_all[:, 0]], axis=1)
    dm_shard = lax.dynamic_slice_in_dim(dm_rows, sidx * ncol, ncol, axis=2)
    grads['w_ada'], gb, dc_part = ada_bwd(c_rows, w_ada, dm_shard, dm_rows, n_ex, name="ada_bwd")
    grads['b_ada'] = gb.reshape(b_ada.shape)
    dc_all = all_gather(jnp.pad(dc_part, ((0, 7), (0, 0))), name="gather_dcond", with_c=False)
    grads['c_ctx'] = sum_slots(dc_all, name="sum_dcond")[0]

    per_name = layer_grads_by_name(dlw)
    pieces, gwhere, off = [], [], 0
    for k, ax in BIG:
        shp = wts[k].shape
        for g in per_name[k]:
            if ax == 1 and k not in GRAD_TRANSPOSED:
                pieces.append(g.reshape(shp[1], nsh, shp[2]).transpose(1, 0, 2).reshape(nsh, -1))
            else:
                pieces.append(g.reshape(nsh, -1))
        n = int(np.prod(shp))
        if n % PACK_W:
            pieces.append(jnp.zeros((nsh, _ceil_to(n, PACK_W) - n), F32))
        gwhere.append((off, _ceil_to(n, PACK_W) // PACK_W))
        off += _ceil_to(n, PACK_W) // PACK_W
    if off % 128:
        pieces.append(jnp.zeros((nsh, (_ceil_to(off, 128) - off) * PACK_W), F32))
    half = _ceil_to(off, 128) // 2
    gpack = jnp.concatenate(pieces, axis=1).reshape(nsh, 2, half, PACK_W)
    from_pair = pair_exchange_halves(gpack, name="reduce_pair")
    chip_sum = add_kept_half(gpack, from_pair, jnp.reshape(mc, (1,)).astype(jnp.int32), name="reduce_pair_add",
                             out_dtype=BF16)
    from_xy = all_to_all_xy(chip_sum, name="reduce_xy")
    reduced = sum_slots(from_xy, name="reduce_xy_add")
    gfull = pair_all_gather(reduced, name="reduce_share").reshape(2 * half, PACK_W)
    for (k, _), wh in zip(BIG, gwhere):
        shp = wts[k].shape
        grads[k] = (_unpack(gfull, wh, (shp[0], shp[2], shp[1])).transpose(0, 2, 1) if k in GRAD_TRANSPOSED
                    else _unpack(gfull, wh, shp))
    for k, tr, g in zip(FFN_NAMES, FFN_TRANSPOSED, reduce_ffn(dwf[0], dwf[1], name="reduce_ffn")):
        grads[k] = g.transpose(0, 2, 1) if tr else g

    outs = {k: adamw(wts[k], grads[k], mom[k], var[k], name="adamw_" + k) for k in WEIGHTS}
    return (loss, grad_x, *[grads[k] for k in WEIGHTS], *[outs[k][0] for k in WEIGHTS], *[outs[k][1] for k in WEIGHTS],
            *[outs[k][2] for k in WEIGHTS])
```

```python
import functools

import jax
import jax.numpy as jnp
import numpy as np
from jax import lax
from jax.experimental import pallas as pl
from jax.experimental.pallas import tpu as pltpu

F32 = jnp.float32
BF16 = jnp.bfloat16
HI = lax.Precision.HIGHEST
MESH = pl.DeviceIdType.MESH
ANY = pl.BlockSpec(memory_space=pl.ANY)

V7X_VMEM_BYTES = 64 * 1024 * 1024
VMEM_LIMIT = V7X_VMEM_BYTES - 8 * 1024 * 1024
LANE = 128
PACK_W = 1024

GRID_W = 64
HEAD_DIM = 64
NA_HEADS, NA_ROWS, NA_COLS = 4, 8, 16
GQA_HEADS, GQA_KV_HEADS = 8, 2
MLA_HEADS, MLA_Q_RANK, MLA_KV_RANK, MLA_NOPE, MLA_ROPE, MLA_V = 4, 256, 128, 64, 32, 64
N_MOD = 9
ROPE_THETA = 10000.0
EPS = 1e-6
NEG_BIG = -1e30
NA_W = NA_HEADS * HEAD_DIM
GQ_W = GQA_HEADS * HEAD_DIM
GK_W = GQA_KV_HEADS * HEAD_DIM
MAIN_W = 3 * NA_W + GQ_W + 2 * GK_W + MLA_Q_RANK + MLA_KV_RANK + MLA_ROPE
MAIN_PAD = 2048
LOG2E, LN2 = float(np.log2(np.e)), float(np.log(2.0))
Q_SCALE = HEAD_DIM ** -0.5 * LOG2E
MLA_Q_SCALE = (MLA_NOPE + MLA_ROPE) ** -0.5 * LOG2E
TQ = 256
TM = 256

ADAM_LR, ADAM_B1, ADAM_B2, ADAM_EPS, ADAM_WD, ADAM_STEP = 0.001, 0.9, 0.999, 1e-08, 0.01, 10

ARG_NAMES = ['x', 'c', 'ctx', 'c_ctx', 'w_ada', 'b_ada', 'ffn1_norm', 'ffn1_w_gate', 'ffn1_w_up', 'ffn1_w_down', 'mix_norm', 'w_in',
             'na_rel_bias', 'gqa_q_norm', 'gqa_k_norm', 'mla_q_norm', 'mla_kv_norm', 'mla_w_uq', 'mla_w_ukv', 'w_branch_a',
             'w_branch_b', 'w_branch_c', 'w_out', 'ffn2_norm', 'ffn2_w_gate', 'ffn2_w_up', 'ffn2_w_down', 'final_norm']
WEIGHTS = ARG_NAMES[3:]
BIG = [('w_in', 1), ('mla_w_uq', 1), ('mla_w_ukv', 1), ('w_branch_a', 1), ('w_branch_b', 1), ('w_branch_c', 1), ('w_out', 0)]
GRAD_TRANSPOSED = ('w_in',)
FFN_NAMES = ['ffn1_w_gate', 'ffn1_w_up', 'ffn1_w_down', 'ffn2_w_gate', 'ffn2_w_up', 'ffn2_w_down']
FFN_TRANSPOSED = [True, True, False, True, True, False]
SMALL_LAYER = ['ffn1_norm', 'mix_norm', 'na_rel_bias', 'gqa_q_norm', 'gqa_k_norm', 'mla_q_norm', 'mla_kv_norm', 'ffn2_norm']


def _cp(*sem):
    return pltpu.CompilerParams(dimension_semantics=sem, vmem_limit_bytes=VMEM_LIMIT)


def _tile(dim, cands):
    for t in cands:
        if dim % t == 0:
            return t
    return dim


def _row_tile(rows, cap=512, mult=16):
    best = None
    for t in range(mult, min(rows, cap) + 1, mult):
        if rows % t == 0:
            best = t
    return best or rows


def _ceil_to(n, m):
    return -(-n // m) * m


def mm(a, b, *, name, ta=False, tb=False, out_dtype=F32, precise=False):
    m, k = (a.shape[1], a.shape[0]) if ta else a.shape
    n = b.shape[0] if tb else b.shape[1]
    tm = _tile(m, (512, 256, 128))
    tn = _tile(n, (1024, 1408, 512, 256, 128))
    tk = _tile(k, (1024, 1408, 512, 256, 128))
    nk = k // tk
    dims = (((0 if ta else 1,), (1 if tb else 0,)), ((), ()))

    def body(a_ref, b_ref, o_ref, *acc):
        if precise:
            part = lax.dot_general(a_ref[...].astype(F32), b_ref[...].astype(F32), dims, precision=HI, preferred_element_type=F32)
        else:
            part = lax.dot_general(a_ref[...].astype(BF16), b_ref[...].astype(BF16), dims, preferred_element_type=F32)
        if nk == 1:
            o_ref[...] = part.astype(o_ref.dtype)
        else:
            acc_ref, = acc
            kk = pl.program_id(2)

            @pl.when(kk == 0)
            def _():
                acc_ref[...] = part

            @pl.when(kk > 0)
            def _():
                acc_ref[...] += part

            @pl.when(kk == nk - 1)
            def _():
                o_ref[...] = acc_ref[...].astype(o_ref.dtype)

    a_spec = pl.BlockSpec((tk, tm), lambda i, j, kk: (kk, i)) if ta else pl.BlockSpec((tm, tk), lambda i, j, kk: (i, kk))
    b_spec = pl.BlockSpec((tn, tk), lambda i, j, kk: (j, kk)) if tb else pl.BlockSpec((tk, tn), lambda i, j, kk: (kk, j))
    return pl.pallas_call(
        body, name=name, grid=(m // tm, n // tn, nk), in_specs=[a_spec, b_spec],
        out_specs=pl.BlockSpec((tm, tn), lambda i, j, kk: (i, j)),
        out_shape=jax.ShapeDtypeStruct((m, n), out_dtype),
        scratch_shapes=[pltpu.VMEM((tm, tn), F32)] if nk > 1 else [],
        compiler_params=_cp("parallel", "parallel", "arbitrary"),
    )(a, b)


def mm_resident(a, w, *, name, tb=False, out_dtype=F32):
    m, k = a.shape
    n = w.shape[0] if tb else w.shape[1]
    tm = _tile(m, (512, 256, 128))
    cn = n if tb else _tile(n, (1024, 512, 256, 128))

    def body(a_ref, w_ref, o_ref):
        aa = a_ref[...].astype(BF16)
        if tb:
            o_ref[...] = _dot(aa, w_ref[...], _NT).astype(o_ref.dtype)
        else:
            for c in range(n // cn):
                cols = slice(cn * c, cn * (c + 1))
                o_ref[:, cols] = _dot(aa, w_ref[:, cols]).astype(o_ref.dtype)

    return pl.pallas_call(
        body, name=name, grid=(m // tm,),
        in_specs=[pl.BlockSpec((tm, k), lambda i: (i, 0)), pl.BlockSpec(w.shape, lambda i: (0, 0), pipeline_mode=pl.Buffered(1))],
        out_specs=pl.BlockSpec((tm, n), lambda i: (i, 0)), out_shape=jax.ShapeDtypeStruct((m, n), out_dtype),
        compiler_params=_cp("parallel"),
    )(a, w)


FFN_GATE, FFN_UP, FFN_DOWN = 0, 1, 2


def _ffn_wspec(wf, l, which):
    _, nsh, _, cs, d = wf.shape
    return pl.BlockSpec((None, nsh, None, cs, d), lambda *_: (l, 0, which, 0, 0), pipeline_mode=pl.Buffered(1))


def _ffn_group(cs):
    for g in (1, 2, 4):
        if (g * cs) % LANE == 0:
            return g
    raise ValueError(cs)


def ffn_up(n, wf, l, base, *, name):
    m, d = n.shape
    nsh, cs = wf.shape[1], wf.shape[3]
    f = nsh * cs
    grp = _ffn_group(cs)
    tm = _tile(m, (512, 256, 128))

    def body(n_ref, wg_ref, wu_ref, g_ref, u_ref, a_ref):
        nn = n_ref[...]
        for c in range(nsh // grp):
            cols = slice(grp * cs * c, grp * cs * (c + 1))
            g = _dot(nn, wg_ref[grp * c:grp * (c + 1)].reshape(grp * cs, d), _NT)
            u = _dot(nn, wu_ref[grp * c:grp * (c + 1)].reshape(grp * cs, d), _NT)
            g_ref[:, cols] = g.astype(BF16)
            u_ref[:, cols] = u.astype(BF16)
            a_ref[:, cols] = f_act_gu(g, u).astype(BF16)

    ospec = pl.BlockSpec((tm, f), lambda i: (i, 0))
    return pl.pallas_call(
        body, name=name, grid=(m // tm,),
        in_specs=[pl.BlockSpec((tm, d), lambda i: (i, 0)), _ffn_wspec(wf, l, base + FFN_GATE), _ffn_wspec(wf, l, base + FFN_UP)],
        out_specs=[ospec] * 3, out_shape=[jax.ShapeDtypeStruct((m, f), BF16)] * 3, compiler_params=_cp("parallel"),
    )(n, wf, wf)


def ffn_down(act, wf, l, base, *, name):
    m, f = act.shape
    nsh, cs, d = wf.shape[1], wf.shape[3], wf.shape[4]
    tm = _tile(m, (512, 256, 128))

    def body(a_ref, wd_ref, y_ref):
        y_ref[...] = _dot(a_ref[...], wd_ref[...].reshape(f, d))

    return pl.pallas_call(
        body, name=name, grid=(m // tm,),
        in_specs=[pl.BlockSpec((tm, f), lambda i: (i, 0)), _ffn_wspec(wf, l, base + FFN_DOWN)],
        out_specs=pl.BlockSpec((tm, d), lambda i: (i, 0)), out_shape=jax.ShapeDtypeStruct((m, d), F32), compiler_params=_cp("parallel"),
    )(act, wf)


def ffn_down_bwd(dy, g, u, wf, l, base, *, name):
    m, d = dy.shape
    nsh, cs = wf.shape[1], wf.shape[3]
    f = nsh * cs
    grp = _ffn_group(cs)
    tm = _tile(m, (512, 256, 128))

    def body(dy_ref, g_ref, u_ref, wd_ref, dg_ref, du_ref):
        dd = dy_ref[...]
        for c in range(nsh // grp):
            cols = slice(grp * cs * c, grp * cs * (c + 1))
            dact = _dot(dd, wd_ref[grp * c:grp * (c + 1)].reshape(grp * cs, d), _NT)
            dg, du = jax.vjp(f_act_gu, g_ref[:, cols].astype(F32), u_ref[:, cols].astype(F32))[1](dact)
            dg_ref[:, cols] = dg.astype(BF16)
            du_ref[:, cols] = du.astype(BF16)

    fspec = pl.BlockSpec((tm, f), lambda i: (i, 0))
    return pl.pallas_call(
        body, name=name, grid=(m // tm,),
        in_specs=[pl.BlockSpec((tm, d), lambda i: (i, 0)), fspec, fspec, _ffn_wspec(wf, l, base + FFN_DOWN)],
        out_specs=[fspec] * 2, out_shape=[jax.ShapeDtypeStruct((m, f), BF16)] * 2, compiler_params=_cp("parallel"),
    )(dy, g, u, wf)


def ffn_up_bwd(dg, du, wf, l, base, *, name):
    m, f = dg.shape
    nsh, cs, d = wf.shape[1], wf.shape[3], wf.shape[4]
    tm = _tile(m, (512, 256, 128))

    def body(dg_ref, du_ref, wg_ref, wu_ref, dn_ref):
        dn_ref[...] = _dot(dg_ref[...], wg_ref[...].reshape(f, d)) + _dot(du_ref[...], wu_ref[...].reshape(f, d))

    fspec = pl.BlockSpec((tm, f), lambda i: (i, 0))
    return pl.pallas_call(
        body, name=name, grid=(m // tm,),
        in_specs=[fspec, fspec, _ffn_wspec(wf, l, base + FFN_GATE), _ffn_wspec(wf, l, base + FFN_UP)],
        out_specs=pl.BlockSpec((tm, d), lambda i: (i, 0)), out_shape=jax.ShapeDtypeStruct((m, d), F32), compiler_params=_cp("parallel"),
    )(dg, du, wf, wf)


def ffn_dw(a, b, nsh, *, name):
    m, f = a.shape
    d = b.shape[1]
    cs = f // nsh
    grp = _ffn_group(cs)
    tm = _tile(m, (1024, 512, 256, 128))

    def body(a_ref, b_ref, o_ref):
        part = _dot(a_ref[...], b_ref[...], _TN).reshape(grp, cs, d)
        i = pl.program_id(1)

        @pl.when(i == 0)
        def _():
            o_ref[...] = part

        @pl.when(i > 0)
        def _():
            o_ref[...] += part

    return pl.pallas_call(
        body, name=name, grid=(nsh // grp, m // tm),
        in_specs=[pl.BlockSpec((tm, grp * cs), lambda j, i: (i, j)), pl.BlockSpec((tm, d), lambda j, i: (i, 0))],
        out_specs=pl.BlockSpec((grp, cs, d), lambda j, i: (j, 0, 0)), out_shape=jax.ShapeDtypeStruct((nsh, cs, d), F32),
        compiler_params=_cp("parallel", "arbitrary"),
    )(a, b)


def rowcall(name, fn, ins, outs, *, nb, nt, nct):
    in_specs, arrays = [], []
    for arr, kind in ins:
        arrays.append(arr)
        if kind == 'tok':
            in_specs.append(pl.BlockSpec((None, TM, arr.shape[-1]), lambda b, t: (b, t, 0)))
        elif kind == 'lat':
            in_specs.append(pl.BlockSpec((None, TM, arr.shape[-1]), lambda b, t: (b, jnp.maximum(t - nct, 0), 0)))
        elif kind == 'pos':
            in_specs.append(pl.BlockSpec((TM, arr.shape[-1]), lambda b, t: (t, 0)))
        elif kind == 'mod':
            in_specs.append(pl.BlockSpec((None, None, 1, arr.shape[-1]), lambda b, t: (b, jnp.where(t >= nct, 1, 0), 0, 0)))
        elif kind == 'full':
            in_specs.append(pl.BlockSpec(arr.shape, lambda b, t, nd=arr.ndim: (0,) * nd))
        else:
            _, w, j = kind
            in_specs.append(pl.BlockSpec((None, TM, w), lambda b, t, j=j: (b, t, j)))
    out_specs, out_shape = [], []
    for o in outs:
        if o[0] == 'tok':
            out_specs.append(pl.BlockSpec((None, TM, o[1]), lambda b, t: (b, t, 0)))
            out_shape.append(jax.ShapeDtypeStruct((nb, nt * TM, o[1]), o[2]))
        elif o[0] == 'mod':
            out_specs.append(pl.BlockSpec((None, None, 1, o[1]), lambda b, t: (b, jnp.where(t >= nct, 1, 0), 0, 0)))
            out_shape.append(jax.ShapeDtypeStruct((nb, 2, 1, o[1]), F32))
        else:
            out_specs.append(pl.BlockSpec(o[1], lambda b, t, nd=len(o[1]): (0,) * nd))
            out_shape.append(jax.ShapeDtypeStruct(o[1], F32))
    n_in = len(ins)

    def body(*refs):
        b, t = pl.program_id(0), pl.program_id(1)
        res = fn(t < nct, *[r[...] for r in refs[:n_in]])
        for ref, o, val in zip(refs[n_in:], outs, res, strict=True):
            if o[0] == 'tok':
                ref[...] = val.astype(ref.dtype)
                continue
            first = ((t == 0) | (t == nct)) if o[0] == 'mod' else ((b == 0) & (t == 0))

            @pl.when(first)
            def _(ref=ref, val=val):
                ref[...] = val

            @pl.when(jnp.logical_not(first))
            def _(ref=ref, val=val):
                ref[...] += val

    return pl.pallas_call(body, name=name, grid=(nb, nt), in_specs=in_specs, out_specs=out_specs, out_shape=out_shape,
                          compiler_params=_cp("arbitrary", "arbitrary"))(*arrays)


def _rms(x, g):
    return x * lax.rsqrt(jnp.mean(x * x, axis=-1, keepdims=True) + EPS) * g


def f_normmod(h, g, shift, scale):
    return _rms(h, g) * (1.0 + scale) + shift


def f_act_gu(g, u):
    return jax.nn.silu(g) * u


def _dot_split(x, m, dims):
    hi = x.astype(BF16)
    lo = (x - hi.astype(F32)).astype(BF16)
    mb = m.astype(BF16)
    return (lax.dot_general(hi, mb, dims, preferred_element_type=F32) + lax.dot_general(lo, mb, dims, preferred_element_type=F32))


def dot_select(x, m):
    return _dot_select(x, m)


@jax.custom_vjp
def _dot_select(x, m):
    return _dot_split(x, m, (((1,), (0,)), ((), ())))


_dot_select.defvjp(lambda x, m: (_dot_split(x, m, (((1,), (0,)), ((), ()))), m),
                   lambda m, ct: (_dot_split(ct, m, (((1,), (1,)), ((), ()))), jnp.zeros_like(m)))


def f_merge(ga, gb, gm, ya, yb, ym):
    return jax.nn.sigmoid(ga) * ya + jax.nn.sigmoid(gb) * yb + jax.nn.sigmoid(gm) * ym


def f_post(p, cb, sb, cm, sm, qn, kn, mqn, mkvn, wuq, wukv, s_b, r_b, t_b, r_m, rep, dup):
    def hnorm(x, g, w):
        ms = dot_select(x * x, s_b[:w, :w])
        gw = dot_select(g, t_b[:, :w])
        return x * lax.rsqrt(ms + EPS) * gw

    def rope(x, cos, sin, rot):
        return x * cos + dot_select(x, rot) * sin

    o = 3 * NA_W
    a_q, a_k, a_v = p[:, 0:NA_W], p[:, NA_W:2 * NA_W], p[:, 2 * NA_W:o]
    b_q = rope(hnorm(p[:, o:o + GQ_W], qn, GQ_W), cb, sb, r_b)
    o += GQ_W
    b_k = rope(hnorm(p[:, o:o + GK_W], kn, GK_W), cb[:, :GK_W], sb[:, :GK_W], r_b[:GK_W, :GK_W])
    b_v = p[:, o + GK_W:o + 2 * GK_W]
    o += 2 * GK_W
    q_lat = jnp.dot(_rms(p[:, o:o + MLA_Q_RANK], mqn).astype(BF16), wuq.astype(BF16), preferred_element_type=F32)
    o += MLA_Q_RANK
    kv_lat = jnp.dot(_rms(p[:, o:o + MLA_KV_RANK], mkvn).astype(BF16), wukv.astype(BF16), preferred_element_type=F32)
    o += MLA_KV_RANK
    nw = MLA_HEADS * MLA_NOPE
    mq_nope, mq_rope = q_lat[:, :nw], rope(q_lat[:, nw:], cm, sm, r_m)
    mk_nope, m_v = kv_lat[:, :nw], kv_lat[:, nw:]
    mk_rope = dot_select(rope(p[:, o:o + LANE], cm, sm, r_m), rep)
    b_k2 = dot_select(b_k, dup)
    b_v2 = dot_select(b_v, dup)
    mq_cat = jnp.concatenate([mq_nope[:, :LANE], mq_rope, mq_nope[:, LANE:], mq_rope], axis=1) * MLA_Q_SCALE
    mk_cat = jnp.concatenate([mk_nope[:, :LANE], mk_rope, mk_nope[:, LANE:], mk_rope], axis=1)
    return (a_q * Q_SCALE, a_k, a_v, b_q * Q_SCALE, b_k2, b_v2, mq_cat, mk_cat, m_v)


POST_QK = (0, 1, 3, 4, 6, 7)


POST_WIDTHS = (NA_W, NA_W, NA_W, GQ_W, 2 * GK_W, 2 * GK_W, 4 * LANE, 4 * LANE, MLA_HEADS * MLA_V)
N_POST = len(POST_WIDTHS)


_NT = (((1,), (1,)), ((), ()))
_TN = (((0,), (0,)), ((), ()))


def _dot(a, b, dims=None):
    if dims is None:
        return jnp.dot(a, b, preferred_element_type=F32)
    return lax.dot_general(a, b, dims, preferred_element_type=F32)


def _lanes(lo, width):
    lane = lax.broadcasted_iota(jnp.int32, (1, LANE), 1)
    return (lane >= lo) & (lane < lo + width)


def _only(x, mask):
    return jnp.where(mask, x, jnp.zeros_like(x))


def _stack_pair(x, width, lo):
    return jnp.concatenate([_only(x, _lanes(lo, width)), _only(x, _lanes(lo + width, width))], axis=0)


def _pair_softmax(s):
    m = jnp.max(s, axis=-1, keepdims=True)
    p = jnp.exp2(s - m)
    l = jnp.sum(p, axis=-1, keepdims=True)
    return p, l, m + jnp.log2(l)


def gqa_fwd(q, k2, v2, *, lc, ctx_q, name):
    nb, t, qw = q.shape
    npair = qw // LANE
    per_kv = npair // GQA_KV_HEADS
    nctb = lc // TQ

    def body(q_ref, k_ref, v_ref, o_ref, lse_ref):
        i = pl.program_id(1)

        def run(rows):
            for pr in range(npair):
                lanes = slice(LANE * pr, LANE * (pr + 1))
                kv = slice(LANE * (pr // per_kv), LANE * (pr // per_kv + 1))
                kk, vv = k_ref[rows, kv], v_ref[rows, kv]
                outs = []
                for e in range(2):
                    p, l, lse = _pair_softmax(_dot(_only(q_ref[:, lanes], _lanes(HEAD_DIM * e, HEAD_DIM)), kk, _NT))
                    outs.append(_dot(p.astype(BF16), vv) / l)
                    lse_ref[2 * pr + e] = lse
                o_ref[:, lanes] = jnp.where(_lanes(0, HEAD_DIM), outs[0], outs[1]).astype(o_ref.dtype)

        @pl.when(i < nctb)
        def _():
            if ctx_q:
                run(pl.ds(0, lc))
            else:
                o_ref[...] = jnp.zeros_like(o_ref)
                lse_ref[...] = jnp.zeros_like(lse_ref)

        @pl.when(i >= nctb)
        def _():
            run(pl.ds(0, t))

    qmap = lambda b, i: (b, i, 0)
    kmap = lambda b, i: (b, 0, 0)
    kw = k2.shape[-1]
    return pl.pallas_call(
        body, name=name, grid=(nb, t // TQ),
        in_specs=[pl.BlockSpec((None, TQ, qw), qmap), pl.BlockSpec((None, t, kw), kmap), pl.BlockSpec((None, t, kw), kmap)],
        out_specs=[pl.BlockSpec((None, TQ, qw), qmap), pl.BlockSpec((None, 2 * npair, TQ, 1), lambda b, i: (b, 0, i, 0))],
        out_shape=[jax.ShapeDtypeStruct((nb, t, qw), BF16), jax.ShapeDtypeStruct((nb, 2 * npair, t, 1), F32)],
        compiler_params=_cp("parallel", "arbitrary"),
    )(q, k2, v2)


def gqa_bwd(q, k2, v2, lse, do, *, lc, ctx_q, name):
    nb, t, qw = q.shape
    npair = qw // LANE
    per_kv = npair // GQA_KV_HEADS
    nctb = lc // TQ

    def body(q_ref, k_ref, v_ref, lse_ref, do_ref, dq_ref, dk_ref, dv_ref):
        i = pl.program_id(1)

        @pl.when(i == 0)
        def _():
            dk_ref[...] = jnp.zeros_like(dk_ref)
            dv_ref[...] = jnp.zeros_like(dv_ref)

        def run(rows):
            for pr in range(npair):
                lanes = slice(LANE * pr, LANE * (pr + 1))
                kv = slice(LANE * (pr // per_kv), LANE * (pr // per_kv + 1))
                kk, vv = k_ref[rows, kv], v_ref[rows, kv]
                qq, dd = _stack_pair(q_ref[:, lanes], HEAD_DIM, 0), _stack_pair(do_ref[:, lanes], HEAD_DIM, 0)
                p = jnp.exp2(_dot(qq, kk, _NT) - jnp.concatenate([lse_ref[2 * pr], lse_ref[2 * pr + 1]], axis=0))
                dp = _dot(dd, vv, _NT)
                delta = jnp.sum(p * dp, axis=-1, keepdims=True)
                ds = (p * (dp - delta)).astype(BF16)
                dq = _dot(ds, kk)
                dq_ref[:, lanes] = jnp.where(_lanes(0, HEAD_DIM), dq[:TQ], dq[TQ:])
                dk_ref[rows, kv] += _dot(ds, qq, _TN)
                dv_ref[rows, kv] += _dot(p.astype(BF16), dd, _TN)

        @pl.when(i < nctb)
        def _():
            if ctx_q:
                run(pl.ds(0, lc))
            else:
                dq_ref[...] = jnp.zeros_like(dq_ref)

        @pl.when(i >= nctb)
        def _():
            run(pl.ds(0, t))

    qmap = lambda b, i: (b, i, 0)
    kmap = lambda b, i: (b, 0, 0)
    kw = k2.shape[-1]
    return pl.pallas_call(
        body, name=name, grid=(nb, t // TQ),
        in_specs=[pl.BlockSpec((None, TQ, qw), qmap), pl.BlockSpec((None, t, kw), kmap), pl.BlockSpec((None, t, kw), kmap),
                  pl.BlockSpec((None, 2 * npair, TQ, 1), lambda b, i: (b, 0, i, 0)), pl.BlockSpec((None, TQ, qw), qmap)],
        out_specs=[pl.BlockSpec((None, TQ, qw), qmap), pl.BlockSpec((None, t, kw), kmap), pl.BlockSpec((None, t, kw), kmap)],
        out_shape=[jax.ShapeDtypeStruct((nb, t, qw), F32), jax.ShapeDtypeStruct(k2.shape, F32), jax.ShapeDtypeStruct(v2.shape, F32)],
        compiler_params=_cp("arbitrary", "arbitrary"),
    )(q, k2, v2, lse, do)


def _mla_lanes(pr, e):
    lane = lax.broadcasted_iota(jnp.int32, (1, 2 * LANE), 1)
    lo = LANE + MLA_ROPE * (2 * pr + e)
    return ((lane >= MLA_NOPE * e) & (lane < MLA_NOPE * (e + 1))) | ((lane >= lo) & (lane < lo + MLA_ROPE))


def mla_fwd(q, k, v, *, lc, ctx_q, name):
    nb, t, w = v.shape
    npair = w // LANE
    nctb = lc // TQ

    def body(q_ref, k_ref, v_ref, o_ref, lse_ref):
        pr, i = pl.program_id(1), pl.program_id(2)

        def run(rows):
            kk, vv = k_ref[rows, :], v_ref[rows, :]
            outs = []
            for e in range(2):
                p, l, lse = _pair_softmax(_dot(_only(q_ref[...], _mla_lanes(pr, e)), kk, _NT))
                outs.append(_dot(p.astype(BF16), vv) / l)
                lse_ref[e] = lse
            o_ref[...] = jnp.where(_lanes(0, MLA_V), outs[0], outs[1]).astype(o_ref.dtype)

        @pl.when(i < nctb)
        def _():
            if ctx_q:
                run(pl.ds(0, lc))
            else:
                o_ref[...] = jnp.zeros_like(o_ref)
                lse_ref[...] = jnp.zeros_like(lse_ref)

        @pl.when(i >= nctb)
        def _():
            run(pl.ds(0, t))

    qmap = lambda b, p, i: (b, i, p)
    kmap = lambda b, p, i: (b, 0, p)
    return pl.pallas_call(
        body, name=name, grid=(nb, npair, t // TQ),
        in_specs=[pl.BlockSpec((None, TQ, 2 * LANE), qmap), pl.BlockSpec((None, t, 2 * LANE), kmap), pl.BlockSpec((None, t, LANE), kmap)],
        out_specs=[pl.BlockSpec((None, TQ, LANE), qmap), pl.BlockSpec((None, 2, TQ, 1), lambda b, p, i: (b, p, i, 0))],
        out_shape=[jax.ShapeDtypeStruct((nb, t, w), BF16), jax.ShapeDtypeStruct((nb, 2 * npair, t, 1), F32)],
        compiler_params=_cp("parallel", "parallel", "arbitrary"),
    )(q, k, v)


def mla_bwd(q, k, v, lse, do, *, lc, ctx_q, name):
    nb, t, w = v.shape
    npair = w // LANE
    nctb = lc // TQ

    def body(q_ref, k_ref, v_ref, lse_ref, do_ref, dq_ref, dk_ref, dv_ref):
        pr, i = pl.program_id(1), pl.program_id(2)

        @pl.when(i == 0)
        def _():
            dk_ref[...] = jnp.zeros_like(dk_ref)
            dv_ref[...] = jnp.zeros_like(dv_ref)

        def run(rows):
            kk, vv = k_ref[rows, :], v_ref[rows, :]
            m0, m1 = _mla_lanes(pr, 0), _mla_lanes(pr, 1)
            qq = jnp.concatenate([_only(q_ref[...], m0), _only(q_ref[...], m1)], axis=0)
            dd = _stack_pair(do_ref[...], MLA_V, 0)
            p = jnp.exp2(_dot(qq, kk, _NT) - jnp.concatenate([lse_ref[0], lse_ref[1]], axis=0))
            dp = _dot(dd, vv, _NT)
            delta = jnp.sum(p * dp, axis=-1, keepdims=True)
            ds = (p * (dp - delta)).astype(BF16)
            dq = _dot(ds, kk)
            dq_ref[...] = _only(dq[:TQ], m0) + _only(dq[TQ:], m1)
            dk_ref[rows, :] += _dot(ds, qq, _TN)
            dv_ref[rows, :] += _dot(p.astype(BF16), dd, _TN)

        @pl.when(i < nctb)
        def _():
            if ctx_q:
                run(pl.ds(0, lc))
            else:
                dq_ref[...] = jnp.zeros_like(dq_ref)

        @pl.when(i >= nctb)
        def _():
            run(pl.ds(0, t))

    qmap = lambda b, p, i: (b, i, p)
    kmap = lambda b, p, i: (b, 0, p)
    return pl.pallas_call(
        body, name=name, grid=(nb, npair, t // TQ),
        in_specs=[pl.BlockSpec((None, TQ, 2 * LANE), qmap), pl.BlockSpec((None, t, 2 * LANE), kmap), pl.BlockSpec((None, t, LANE), kmap),
                  pl.BlockSpec((None, 2, TQ, 1), lambda b, p, i: (b, p, i, 0)), pl.BlockSpec((None, TQ, LANE), qmap)],
        out_specs=[pl.BlockSpec((None, TQ, 2 * LANE), qmap), pl.BlockSpec((None, t, 2 * LANE), kmap), pl.BlockSpec((None, t, LANE), kmap)],
        out_shape=[jax.ShapeDtypeStruct(q.shape, F32), jax.ShapeDtypeStruct(k.shape, F32), jax.ShapeDtypeStruct(v.shape, F32)],
        compiler_params=_cp("arbitrary", "arbitrary", "arbitrary"),
    )(q, k, v, lse, do)


def _na_window(st, nc, rows):
    r = jnp.maximum(st - nc, 0)
    r0 = jnp.clip(r - NA_ROWS // 2, 0, rows - NA_ROWS)
    return r, r0, r - r0


def na_fwd(q, k, v, bias, *, lc, ctx_q, name):
    nb, t, w = q.shape
    npair = w // LANE
    nc, rows = lc // GRID_W, (t - lc) // GRID_W
    nwin = NA_ROWS * GRID_W

    def body(q_ref, k_ref, v_ref, bias_ref, o_ref, lse_ref):
        st = pl.program_id(1)
        ctx = pl.ds(0, lc)

        @pl.when(st < nc)
        def _():
            if not ctx_q:
                o_ref[...] = jnp.zeros_like(o_ref)
                lse_ref[...] = jnp.zeros_like(lse_ref)
                return
            for pr in range(npair):
                lanes = slice(LANE * pr, LANE * (pr + 1))
                kc, vc = k_ref[ctx, lanes], v_ref[ctx, lanes]
                outs = []
                for e in range(2):
                    p, l, lse = _pair_softmax(_dot(_only(q_ref[:, lanes], _lanes(HEAD_DIM * e, HEAD_DIM)), kc, _NT))
                    outs.append(_dot(p.astype(BF16), vc) / l)
                    lse_ref[2 * pr + e] = lse
                o_ref[:, lanes] = jnp.where(_lanes(0, HEAD_DIM), outs[0], outs[1]).astype(o_ref.dtype)

        @pl.when(st >= nc)
        def _():
            _, r0, _ = _na_window(st, nc, rows)
            win = pl.ds(pl.multiple_of(lc + r0 * GRID_W, GRID_W), nwin)
            for pr in range(npair):
                lanes = slice(LANE * pr, LANE * (pr + 1))
                kc, vc, kw, vw = k_ref[ctx, lanes], v_ref[ctx, lanes], k_ref[win, lanes], v_ref[win, lanes]
                qq = _stack_pair(q_ref[:, lanes], HEAD_DIM, 0)
                s_loc = _dot(qq, kw, _NT) + jnp.concatenate([bias_ref[2 * pr], bias_ref[2 * pr + 1]], axis=0) * LOG2E
                s_ctx = _dot(qq, kc, _NT)
                m = jnp.maximum(jnp.max(s_loc, axis=-1, keepdims=True), jnp.max(s_ctx, axis=-1, keepdims=True))
                p_loc, p_ctx = jnp.exp2(s_loc - m), jnp.exp2(s_ctx - m)
                l = jnp.sum(p_loc, axis=-1, keepdims=True) + jnp.sum(p_ctx, axis=-1, keepdims=True)
                o = (_dot(p_loc.astype(BF16), vw) + _dot(p_ctx.astype(BF16), vc)) / l
                lse = m + jnp.log2(l)
                lse_ref[2 * pr], lse_ref[2 * pr + 1] = lse[:GRID_W], lse[GRID_W:]
                o_ref[:, lanes] = jnp.where(_lanes(0, HEAD_DIM), o[:GRID_W], o[GRID_W:]).astype(o_ref.dtype)

    qmap = lambda b, st: (b, st, 0)
    kmap = lambda b, st: (b, 0, 0)
    return pl.pallas_call(
        body, name=name, grid=(nb, nc + rows),
        in_specs=[pl.BlockSpec((None, GRID_W, w), qmap), pl.BlockSpec((None, t, w), kmap), pl.BlockSpec((None, t, w), kmap),
                  pl.BlockSpec((2 * npair, None, GRID_W, nwin), lambda b, st: (0, _na_window(st, nc, rows)[2], 0, 0))],
        out_specs=[pl.BlockSpec((None, GRID_W, w), qmap), pl.BlockSpec((None, 2 * npair, GRID_W, 1), lambda b, st: (b, 0, st, 0))],
        out_shape=[jax.ShapeDtypeStruct((nb, t, w), BF16), jax.ShapeDtypeStruct((nb, 2 * npair, t, 1), F32)],
        compiler_params=_cp("parallel", "arbitrary"),
    )(q, k, v, bias)


def na_bwd(q, k, v, bias, lse, do, *, lc, ctx_q, name):
    nb, t, w = q.shape
    npair = w // LANE
    nc, rows = lc // GRID_W, (t - lc) // GRID_W
    nwin = NA_ROWS * GRID_W

    def body(q_ref, k_ref, v_ref, bias_ref, lse_ref, do_ref, dq_ref, dk_ref, dv_ref, db_ref):
        b, st = pl.program_id(0), pl.program_id(1)

        @pl.when(st == 0)
        def _():
            dk_ref[...] = jnp.zeros_like(dk_ref)
            dv_ref[...] = jnp.zeros_like(dv_ref)

        @pl.when((st == 0) & (b == 0))
        def _():
            db_ref[...] = jnp.zeros_like(db_ref)

        ctx = pl.ds(0, lc)

        @pl.when(st < nc)
        def _():
            if not ctx_q:
                dq_ref[...] = jnp.zeros_like(dq_ref)
                return
            for pr in range(npair):
                lanes = slice(LANE * pr, LANE * (pr + 1))
                kc, vc = k_ref[ctx, lanes], v_ref[ctx, lanes]
                dqs = []
                for e in range(2):
                    mine = _lanes(HEAD_DIM * e, HEAD_DIM)
                    qq, dd = _only(q_ref[:, lanes], mine), _only(do_ref[:, lanes], mine)
                    p = jnp.exp2(_dot(qq, kc, _NT) - lse_ref[2 * pr + e])
                    dp = _dot(dd, vc, _NT)
                    delta = jnp.sum(p * dp, axis=-1, keepdims=True)
                    ds = (p * (dp - delta)).astype(BF16)
                    dqs.append(_dot(ds, kc))
                    dk_ref[ctx, lanes] += _dot(ds, qq, _TN)
                    dv_ref[ctx, lanes] += _dot(p.astype(BF16), dd, _TN)
                dq_ref[:, lanes] = jnp.where(_lanes(0, HEAD_DIM), dqs[0], dqs[1])

        @pl.when(st >= nc)
        def _():
            _, r0, case = _na_window(st, nc, rows)
            win = pl.ds(pl.multiple_of(lc + r0 * GRID_W, GRID_W), nwin)
            for pr in range(npair):
                lanes = slice(LANE * pr, LANE * (pr + 1))
                kc, vc, kw, vw = k_ref[ctx, lanes], v_ref[ctx, lanes], k_ref[win, lanes], v_ref[win, lanes]
                qq, dd = _stack_pair(q_ref[:, lanes], HEAD_DIM, 0), _stack_pair(do_ref[:, lanes], HEAD_DIM, 0)
                lse = jnp.concatenate([lse_ref[2 * pr], lse_ref[2 * pr + 1]], axis=0)
                bias2 = jnp.concatenate([bias_ref[2 * pr], bias_ref[2 * pr + 1]], axis=0)
                p_loc = jnp.exp2(_dot(qq, kw, _NT) + bias2 * LOG2E - lse)
                p_ctx = jnp.exp2(_dot(qq, kc, _NT) - lse)
                dp_loc, dp_ctx = _dot(dd, vw, _NT), _dot(dd, vc, _NT)
                delta = jnp.sum(p_loc * dp_loc, axis=-1, keepdims=True) + jnp.sum(p_ctx * dp_ctx, axis=-1, keepdims=True)
                ds_loc = p_loc * (dp_loc - delta)
                db_ref[2 * pr, case] += ds_loc[:GRID_W]
                db_ref[2 * pr + 1, case] += ds_loc[GRID_W:]
                ds_loc = ds_loc.astype(BF16)
                ds_ctx = (p_ctx * (dp_ctx - delta)).astype(BF16)
                dq = _dot(ds_loc, kw) + _dot(ds_ctx, kc)
                dq_ref[:, lanes] = jnp.where(_lanes(0, HEAD_DIM), dq[:GRID_W], dq[GRID_W:])
                dk_ref[win, lanes] += _dot(ds_loc, qq, _TN)
                dk_ref[ctx, lanes] += _dot(ds_ctx, qq, _TN)
                dv_ref[win, lanes] += _dot(p_loc.astype(BF16), dd, _TN)
                dv_ref[ctx, lanes] += _dot(p_ctx.astype(BF16), dd, _TN)

    qmap = lambda b, st: (b, st, 0)
    kmap = lambda b, st: (b, 0, 0)
    nh = 2 * npair
    return pl.pallas_call(
        body, name=name, grid=(nb, nc + rows),
        in_specs=[pl.BlockSpec((None, GRID_W, w), qmap), pl.BlockSpec((None, t, w), kmap), pl.BlockSpec((None, t, w), kmap),
                  pl.BlockSpec((nh, None, GRID_W, nwin), lambda b, st: (0, _na_window(st, nc, rows)[2], 0, 0)),
                  pl.BlockSpec((None, nh, GRID_W, 1), lambda b, st: (b, 0, st, 0)), pl.BlockSpec((None, GRID_W, w), qmap)],
        out_specs=[pl.BlockSpec((None, GRID_W, w), qmap), pl.BlockSpec((None, t, w), kmap), pl.BlockSpec((None, t, w), kmap),
                   pl.BlockSpec((nh, NA_ROWS, GRID_W, nwin), lambda b, st: (0, 0, 0, 0))],
        out_shape=[jax.ShapeDtypeStruct((nb, t, w), F32), jax.ShapeDtypeStruct((nb, t, w), F32), jax.ShapeDtypeStruct((nb, t, w), F32),
                   jax.ShapeDtypeStruct((nh, NA_ROWS, GRID_W, nwin), F32)],
        compiler_params=_cp("arbitrary", "arbitrary"),
    )(q, k, v, bias, lse, do)


def _na_tables():
    cols = np.arange(GRID_W)
    c0 = np.clip(cols - NA_COLS // 2, 0, GRID_W - NA_COLS)
    col_in = (cols[None, :] >= c0[:, None]) & (cols[None, :] < c0[:, None] + NA_COLS)
    dc = np.clip(cols[None, :] - cols[:, None] + NA_COLS - 1, 0, 2 * NA_COLS - 2)
    dr = np.arange(NA_ROWS)[None, :] + (NA_ROWS - 1) - np.arange(NA_ROWS)[:, None]
    return col_in, dc, dr


def _na_onehots():
    col_in, dc, dr = _na_tables()
    e1 = np.zeros((GRID_W, GRID_W, LANE), np.float32)
    qi, ki = np.nonzero(col_in)
    e1[qi, ki, dc[qi, ki]] = 1.0
    e2 = np.zeros((2 * NA_ROWS, NA_ROWS, NA_ROWS), np.float32)
    ci, ji = np.meshgrid(np.arange(NA_ROWS), np.arange(NA_ROWS), indexing='ij')
    e2[dr[ci, ji], ci, ji] = 1.0
    return jnp.asarray(e1.reshape(GRID_W * GRID_W, LANE)), jnp.asarray(e2.reshape(2 * NA_ROWS, NA_ROWS * NA_ROWS)), col_in


def na_expand_bias(rel_bias, name):
    e1, e2, col_in = _na_onehots()
    nh = rel_bias.shape[0]
    nrow = NA_ROWS * NA_ROWS
    rel = jnp.pad(rel_bias, ((0, 0), (0, 1), (0, LANE - rel_bias.shape[2])))
    rel = rel.transpose(1, 0, 2).reshape(2 * NA_ROWS, nh * LANE)
    y = mm(e2, rel, ta=True, name=name + "_rows", precise=True)
    y = y.reshape(nrow, nh, LANE).transpose(1, 0, 2).reshape(nh * nrow, LANE)
    g = mm(y, e1, tb=True, name=name + "_cols", precise=True)
    g = g.reshape(nh, NA_ROWS, NA_ROWS, GRID_W, GRID_W).transpose(0, 1, 3, 2, 4)
    g = jnp.where(col_in[None, None, :, None, :], g, NEG_BIG)
    return g.reshape(nh, NA_ROWS, GRID_W, NA_ROWS * GRID_W)


def na_reduce_bias(dexp, name):
    e1, e2, _ = _na_onehots()
    nh = dexp.shape[0]
    x = dexp.reshape(nh, NA_ROWS, GRID_W, NA_ROWS, GRID_W).transpose(0, 1, 3, 2, 4).reshape(nh * NA_ROWS * NA_ROWS, GRID_W * GRID_W)
    y = mm(x, e1, name=name + "_cols", precise=True)
    y = y.reshape(nh, NA_ROWS * NA_ROWS, LANE).transpose(1, 0, 2).reshape(NA_ROWS * NA_ROWS, nh * LANE)
    z = mm(e2, y, name=name + "_rows", precise=True)
    return z.reshape(2 * NA_ROWS, nh, LANE).transpose(1, 0, 2)[:, :2 * NA_ROWS - 1, :2 * NA_COLS - 1]


def _rot_matrix(width, d_rot):
    f = d_rot // 4
    r = np.zeros((width, width), np.float32)
    for base in range(0, width, d_rot // 2):
        for j in range(f):
            r[base + f + j, base + j] = -1.0
            r[base + j, base + f + j] = 1.0
    return r


def _rope_tables(s_len, lc, d_rot, reps):
    half = d_rot // 2
    freqs = ROPE_THETA ** (-jnp.arange(0, half, 2, dtype=F32) / half)
    tpos = jnp.arange(s_len)
    row = (tpos // GRID_W).astype(F32)[:, None] * freqs
    col = (tpos % GRID_W).astype(F32)[:, None] * freqs
    ang = jnp.concatenate([row, row, col, col], axis=-1)
    cos = jnp.concatenate([jnp.ones((lc, d_rot), F32), jnp.cos(ang)], axis=0)
    sin = jnp.concatenate([jnp.zeros((lc, d_rot), F32), jnp.sin(ang)], axis=0)
    return jnp.tile(cos, (1, reps)), jnp.tile(sin, (1, reps))


def _post_consts():
    s_b = np.kron(np.eye(GQA_HEADS, dtype=np.float32), np.full((HEAD_DIM, HEAD_DIM), 1.0 / HEAD_DIM, np.float32))
    t_b = np.tile(np.eye(HEAD_DIM, dtype=np.float32), (1, GQA_HEADS))
    r_b = _rot_matrix(GQ_W, HEAD_DIM)
    r_m = _rot_matrix(LANE, MLA_ROPE)
    rep = np.zeros((LANE, LANE), np.float32)
    for h in range(MLA_HEADS):
        rep[np.arange(MLA_ROPE), h * MLA_ROPE + np.arange(MLA_ROPE)] = 1.0
    dup = np.zeros((GK_W, 2 * GK_W), np.float32)
    for j in range(GQA_KV_HEADS):
        for e in range(2):
            dup[HEAD_DIM * j + np.arange(HEAD_DIM), 2 * HEAD_DIM * j + HEAD_DIM * e + np.arange(HEAD_DIM)] = 1.0
    return tuple(jnp.asarray(a) for a in (s_b, r_b, t_b, r_m, rep, dup))


def _heads_to_parts(w, first):
    r = w.shape[0]
    w3 = w.reshape(r, MLA_HEADS, -1)
    return jnp.concatenate([w3[:, :, :first].reshape(r, -1), w3[:, :, first:].reshape(r, -1)], axis=1)


def _parts_to_heads(w, first):
    r = w.shape[0]
    nf = MLA_HEADS * first
    return jnp.concatenate([w[:, :nf].reshape(r, MLA_HEADS, first), w[:, nf:].reshape(r, MLA_HEADS, -1)], axis=2).reshape(r, -1)


def _place():
    return lax.axis_index("x"), lax.axis_index("y"), lax.axis_index("c")


def all_gather(v, *, name, with_c):
    flips = [(dx, dy, dc) for dx in (0, 1) for dy in (0, 1) for dc in ((0, 1) if with_c else (0,))][1:]
    n = len(flips) + 1

    def body(v_ref, out_ref, send_sems, recv_sems, local_sem):
        mx, my, mc = _place()

        def slot(px, py, pc):
            return 4 * px + 2 * py + pc if with_c else 2 * px + py

        mine = pltpu.make_async_copy(v_ref, out_ref.at[slot(mx, my, mc)], local_sem)
        mine.start()
        sends = []
        for j, (dx, dy, dc) in enumerate(flips):
            peer = (mx ^ dx, my ^ dy, mc ^ dc)
            cp = pltpu.make_async_remote_copy(src_ref=v_ref, dst_ref=out_ref.at[slot(mx, my, mc)], send_sem=send_sems.at[j],
                                              recv_sem=recv_sems.at[j], device_id=peer, device_id_type=MESH)
            cp.start()
            sends.append(cp)
        for j, (dx, dy, dc) in enumerate(flips):
            peer = (mx ^ dx, my ^ dy, mc ^ dc)
            pltpu.make_async_remote_copy(src_ref=v_ref, dst_ref=out_ref.at[slot(*peer)], send_sem=send_sems.at[j],
                                         recv_sem=recv_sems.at[j], device_id=peer, device_id_type=MESH).wait_recv()
        for cp in sends:
            cp.wait_send()
        mine.wait()

    return pl.pallas_call(
        body, name=name, in_specs=[ANY], out_specs=ANY, out_shape=jax.ShapeDtypeStruct((n,) + v.shape, v.dtype),
        scratch_shapes=[pltpu.SemaphoreType.DMA((n - 1,)), pltpu.SemaphoreType.DMA((n - 1,)), pltpu.SemaphoreType.DMA(())],
    )(v)


def gather_shards(v, *, name):
    _, h, w = v.shape
    flips = [(1, 0), (0, 1), (1, 1)]

    def body(v_ref, out_ref, send_sems, recv_sems):
        mx, my, mc = _place()
        me = 2 * mx + my
        sib = (mx, my, 1 - mc)

        def copy(k, src, dst, to):
            return pltpu.make_async_remote_copy(src_ref=src, dst_ref=dst, send_sem=send_sems.at[k], recv_sem=recv_sems.at[k],
                                                device_id=to, device_id_type=MESH)

        first = [copy(j, v_ref.at[mc], out_ref.at[me, mc], (mx ^ dx, my ^ dy, mc)) for j, (dx, dy) in enumerate(flips)]
        for cp in first:
            cp.start()
        passed = []
        for j, (dx, dy) in enumerate(flips):
            theirs = out_ref.at[2 * (mx ^ dx) + (my ^ dy), mc]
            copy(j, v_ref.at[mc], theirs, (mx ^ dx, my ^ dy, mc)).wait_recv()
            fw = copy(3 + j, theirs, theirs, sib)
            fw.start()
            passed.append(fw)
        for j, (dx, dy) in enumerate(flips):
            other = out_ref.at[2 * (mx ^ dx) + (my ^ dy), 1 - mc]
            copy(3 + j, other, other, sib).wait_recv()
        for cp in first + passed:
            cp.wait_send()

    out = pl.pallas_call(
        body, name=name, in_specs=[ANY], out_specs=ANY, out_shape=jax.ShapeDtypeStruct((4, 2, h, w), v.dtype),
        scratch_shapes=[pltpu.SemaphoreType.DMA((6,)), pltpu.SemaphoreType.DMA((6,))],
    )(v)
    mx, my, _ = _place()
    return lax.dynamic_update_slice(out, v[None], (2 * mx + my, 0, 0, 0))


def pair_exchange_halves(g, *, name):
    n, _, h, w = g.shape

    def body(g_ref, out_ref, send_sems, recv_sems):
        mx, my, mc = _place()
        sib = (mx, my, 1 - mc)
        cps = [pltpu.make_async_remote_copy(src_ref=g_ref.at[s, 1 - mc], dst_ref=out_ref.at[s], send_sem=send_sems.at[s],
                                            recv_sem=recv_sems.at[s], device_id=sib, device_id_type=MESH) for s in range(n)]
        for cp in cps:
            cp.start()
        for cp in cps:
            cp.wait_recv()
        for cp in cps:
            cp.wait_send()

    return pl.pallas_call(
        body, name=name, in_specs=[ANY], out_specs=ANY, out_shape=jax.ShapeDtypeStruct((n, h, w), g.dtype),
        scratch_shapes=[pltpu.SemaphoreType.DMA((n,)), pltpu.SemaphoreType.DMA((n,))],
    )(g)


def all_to_all_xy(v, *, name):
    def body(v_ref, out_ref, send_sems, recv_sems):
        mx, my, mc = _place()
        me = 2 * mx + my
        flips = [(1, 0), (0, 1), (1, 1)]
        sends = []
        for j, (dx, dy) in enumerate(flips):
            px, py = mx ^ dx, my ^ dy
            cp = pltpu.make_async_remote_copy(src_ref=v_ref.at[2 * px + py], dst_ref=out_ref.at[me], send_sem=send_sems.at[j],
                                              recv_sem=recv_sems.at[j], device_id=(px, py, mc), device_id_type=MESH)
            cp.start()
            sends.append(cp)
        for j, (dx, dy) in enumerate(flips):
            px, py = mx ^ dx, my ^ dy
            pltpu.make_async_remote_copy(src_ref=v_ref.at[me], dst_ref=out_ref.at[2 * px + py], send_sem=send_sems.at[j],
                                         recv_sem=recv_sems.at[j], device_id=(px, py, mc), device_id_type=MESH).wait_recv()
        for cp in sends:
            cp.wait_send()

    out = pl.pallas_call(
        body, name=name, in_specs=[ANY], out_specs=ANY, out_shape=jax.ShapeDtypeStruct(v.shape, v.dtype),
        scratch_shapes=[pltpu.SemaphoreType.DMA((3,)), pltpu.SemaphoreType.DMA((3,))],
    )(v)
    mx, my, _ = _place()
    me = 2 * mx + my
    return lax.dynamic_update_slice(out, lax.dynamic_slice_in_dim(v, me, 1, axis=0), (me, 0, 0))


def pair_all_gather(v, *, name):
    def body(v_ref, out_ref, send_sem, recv_sem):
        mx, my, mc = _place()
        cp = pltpu.make_async_remote_copy(src_ref=v_ref, dst_ref=out_ref.at[mc], send_sem=send_sem, recv_sem=recv_sem,
                                          device_id=(mx, my, 1 - mc), device_id_type=MESH)
        cp.start()
        pltpu.make_async_remote_copy(src_ref=v_ref, dst_ref=out_ref.at[1 - mc], send_sem=send_sem, recv_sem=recv_sem,
                                     device_id=(mx, my, 1 - mc), device_id_type=MESH).wait_recv()
        cp.wait_send()

    out = pl.pallas_call(
        body, name=name, in_specs=[ANY], out_specs=ANY, out_shape=jax.ShapeDtypeStruct((2,) + v.shape, v.dtype),
        scratch_shapes=[pltpu.SemaphoreType.DMA(()), pltpu.SemaphoreType.DMA(())],
    )(v)
    return lax.dynamic_update_slice(out, v[None], (_place()[2], 0, 0))


def gather_ffn(wl, *, name):
    nl, nblk, cs, d = wl.shape
    assert nl == 2
    flips = [(1, 0), (0, 1), (1, 1)]

    def body(v_ref, out_ref, send_sems, recv_sems):
        mx, my, mc = _place()
        me = 2 * mx + my
        sib = (mx, my, 1 - mc)

        def copy(k, src, dst, to):
            return pltpu.make_async_remote_copy(src_ref=src, dst_ref=dst, send_sem=send_sems.at[k], recv_sem=recv_sems.at[k],
                                                device_id=to, device_id_type=MESH)

        first = [copy(j, v_ref.at[mc], out_ref.at[mc, me], (mx ^ dx, my ^ dy, mc)) for j, (dx, dy) in enumerate(flips)]
        for cp in first:
            cp.start()
        passed = []
        for j, (dx, dy) in enumerate(flips):
            theirs = out_ref.at[mc, 2 * (mx ^ dx) + (my ^ dy)]
            copy(j, v_ref.at[mc], theirs, (mx ^ dx, my ^ dy, mc)).wait_recv()
            fw = copy(3 + j, theirs, theirs, sib)
            fw.start()
            passed.append(fw)
        for j, (dx, dy) in enumerate(flips):
            other = out_ref.at[1 - mc, 2 * (mx ^ dx) + (my ^ dy)]
            copy(3 + j, other, other, sib).wait_recv()
        for cp in first + passed:
            cp.wait_send()

    out = pl.pallas_call(
        body, name=name, in_specs=[ANY], out_specs=ANY, out_shape=jax.ShapeDtypeStruct((nl, 4, nblk, cs, d), wl.dtype),
        scratch_shapes=[pltpu.SemaphoreType.DMA((6,)), pltpu.SemaphoreType.DMA((6,))],
    )(wl)
    mx, my, _ = _place()
    return lax.dynamic_update_slice(out, wl[:, None], (0, 2 * mx + my, 0, 0, 0))


def reduce_ffn(g0, g1, *, name):
    nt = len(g0)
    nsh, cs, d = g0[0].shape
    flips = [(1, 0), (0, 1), (1, 1)]
    mx, my, mc = _place()
    me = 2 * mx + my
    c_idx = jnp.reshape(mc, (1,)).astype(jnp.int32)

    def pair_body(*refs):
        ins0, ins1, outs = refs[:nt], refs[nt:2 * nt], refs[2 * nt:3 * nt]
        send_sems, recv_sems = refs[3 * nt:]
        kx, ky, kc = _place()
        sib = (kx, ky, 1 - kc)
        for c in range(2):
            @pl.when(kc == c)
            def _(c=c):
                mine_out = (ins1, ins0)[c]
                cps = [pltpu.make_async_remote_copy(src_ref=mine_out[t], dst_ref=outs[t], send_sem=send_sems.at[t],
                                                    recv_sem=recv_sems.at[t], device_id=sib, device_id_type=MESH) for t in range(nt)]
                for cp in cps:
                    cp.start()
                for cp in cps:
                    cp.wait_recv()
                for cp in cps:
                    cp.wait_send()

    from_pair = pl.pallas_call(
        pair_body, name=name + "_pair", in_specs=[ANY] * (2 * nt), out_specs=[ANY] * nt,
        out_shape=[jax.ShapeDtypeStruct((nsh, cs, d), F32)] * nt,
        scratch_shapes=[pltpu.SemaphoreType.DMA((nt,)), pltpu.SemaphoreType.DMA((nt,))],
    )(*g0, *g1)

    tr = _row_tile(cs, 64)

    def add_body(c_ref, *refs):
        for t in range(nt):
            mine = jnp.where(c_ref[0] == 0, refs[t][...], refs[nt + t][...])
            refs[3 * nt + t][...] = (mine + refs[2 * nt + t][...]).astype(BF16)

    spec = pl.BlockSpec((None, tr, d), lambda s, i, c_ref: (s, i, 0))
    chip_sum = pl.pallas_call(
        add_body, name=name + "_pair_add",
        grid_spec=pltpu.PrefetchScalarGridSpec(num_scalar_prefetch=1, grid=(nsh, cs // tr), in_specs=[spec] * (3 * nt),
                                               out_specs=[spec] * nt),
        out_shape=[jax.ShapeDtypeStruct((nsh, cs, d), BF16)] * nt, compiler_params=_cp("parallel", "parallel"),
    )(c_idx, *g0, *g1, *from_pair)

    def xy_body(*refs):
        ins, outs = refs[:nt], refs[nt:2 * nt]
        send_sems, recv_sems = refs[2 * nt:]
        kx, ky, kc = _place()
        k_me = 2 * kx + ky
        sends = []
        for j, (dx, dy) in enumerate(flips):
            px, py = kx ^ dx, ky ^ dy
            for t in range(nt):
                cp = pltpu.make_async_remote_copy(src_ref=ins[t].at[2 * px + py], dst_ref=outs[t].at[k_me],
                                                  send_sem=send_sems.at[j * nt + t], recv_sem=recv_sems.at[j * nt + t],
                                                  device_id=(px, py, kc), device_id_type=MESH)
                cp.start()
                sends.append(cp)
        for j, (dx, dy) in enumerate(flips):
            px, py = kx ^ dx, ky ^ dy
            for t in range(nt):
                pltpu.make_async_remote_copy(src_ref=ins[t].at[k_me], dst_ref=outs[t].at[2 * px + py],
                                             send_sem=send_sems.at[j * nt + t], recv_sem=recv_sems.at[j * nt + t],
                                             device_id=(px, py, kc), device_id_type=MESH).wait_recv()
        for cp in sends:
            cp.wait_send()

    from_xy = pl.pallas_call(
        xy_body, name=name + "_xy", in_specs=[ANY] * nt, out_specs=[ANY] * nt,
        out_shape=[jax.ShapeDtypeStruct((nsh, cs, d), BF16)] * nt,
        scratch_shapes=[pltpu.SemaphoreType.DMA((3 * nt,)), pltpu.SemaphoreType.DMA((3 * nt,))],
    )(*chip_sum)
    from_xy = [lax.dynamic_update_slice(o, lax.dynamic_slice_in_dim(v, me, 1, axis=0), (me, 0, 0)) for o, v in zip(from_xy, chip_sum)]

    def sum_body(*refs):
        for t in range(nt):
            acc = refs[t][0].astype(F32)
            for s in range(1, nsh):
                acc = acc + refs[t][s].astype(F32)
            refs[nt + t][...] = acc

    reduced = pl.pallas_call(
        sum_body, name=name + "_xy_add", grid=(cs // tr,), in_specs=[pl.BlockSpec((nsh, tr, d), lambda i: (0, i, 0))] * nt,
        out_specs=[pl.BlockSpec((tr, d), lambda i: (i, 0))] * nt, out_shape=[jax.ShapeDtypeStruct((cs, d), F32)] * nt,
        compiler_params=_cp("parallel"),
    )(*from_xy)

    def share_body(*refs):
        ins, outs = refs[:nt], refs[nt:2 * nt]
        send_sems, recv_sems = refs[2 * nt:]
        kx, ky, kc = _place()
        sib = (kx, ky, 1 - kc)
        cps = [pltpu.make_async_remote_copy(src_ref=ins[t], dst_ref=outs[t].at[kc], send_sem=send_sems.at[t],
                                            recv_sem=recv_sems.at[t], device_id=sib, device_id_type=MESH) for t in range(nt)]
        for cp in cps:
            cp.start()
        for t in range(nt):
            pltpu.make_async_remote_copy(src_ref=ins[t], dst_ref=outs[t].at[1 - kc], send_sem=send_sems.at[t],
                                         recv_sem=recv_sems.at[t], device_id=sib, device_id_type=MESH).wait_recv()
        for cp in cps:
            cp.wait_send()

    both = pl.pallas_call(
        share_body, name=name + "_share", in_specs=[ANY] * nt, out_specs=[ANY] * nt,
        out_shape=[jax.ShapeDtypeStruct((2, cs, d), F32)] * nt,
        scratch_shapes=[pltpu.SemaphoreType.DMA((nt,)), pltpu.SemaphoreType.DMA((nt,))],
    )(*reduced)
    return [lax.dynamic_update_slice(o, v[None], (mc, 0, 0)) for o, v in zip(both, reduced)]


def add_kept_half(g, r, c_idx, *, name, out_dtype):
    n, _, h, w = g.shape
    th = _row_tile(h)

    def body(c_ref, g_ref, r_ref, o_ref):
        o_ref[...] = (g_ref[...] + r_ref[...]).astype(o_ref.dtype)

    return pl.pallas_call(
        body, name=name,
        grid_spec=pltpu.PrefetchScalarGridSpec(
            num_scalar_prefetch=1, grid=(n, h // th),
            in_specs=[pl.BlockSpec((None, None, th, w), lambda s, i, c_ref: (s, c_ref[0], i, 0)),
                      pl.BlockSpec((None, th, w), lambda s, i, c_ref: (s, i, 0))],
            out_specs=pl.BlockSpec((None, th, w), lambda s, i, c_ref: (s, i, 0))),
        out_shape=jax.ShapeDtypeStruct((n, h, w), out_dtype), compiler_params=_cp("parallel", "parallel"),
    )(c_idx, g, r)


def sum_slots(v, *, name):
    n, rows, w = v.shape
    tr = _row_tile(rows, 256)

    def body(v_ref, o_ref):
        acc = v_ref[0].astype(F32)
        for s in range(1, n):
            acc = acc + v_ref[s].astype(F32)
        o_ref[...] = acc

    return pl.pallas_call(body, name=name, grid=(rows // tr,), in_specs=[pl.BlockSpec((n, tr, w), lambda i: (0, i, 0))],
                          out_specs=pl.BlockSpec((tr, w), lambda i: (i, 0)), out_shape=jax.ShapeDtypeStruct((rows, w), F32),
                          compiler_params=_cp("parallel"))(v)


def ada_fwd(c_rows, w_ada, b_shard, *, name):
    nl, d, ncol = w_ada.shape
    rows = c_rows.shape[0]
    tn = _tile(ncol, (768, 512, 256, 128))

    def body(c_ref, w_ref, b_ref, o_ref):
        o_ref[...] = jnp.dot(jax.nn.silu(c_ref[...]), w_ref[...], precision=HI, preferred_element_type=F32) + b_ref[...]

    return pl.pallas_call(
        body, name=name, grid=(nl, ncol // tn),
        in_specs=[pl.BlockSpec((rows, d), lambda l, j: (0, 0)), pl.BlockSpec((None, d, tn), lambda l, j: (l, 0, j)),
                  pl.BlockSpec((None, 1, tn), lambda l, j: (l, 0, j))],
        out_specs=pl.BlockSpec((None, rows, tn), lambda l, j: (l, 0, j)),
        out_shape=jax.ShapeDtypeStruct((nl, rows, ncol), F32), compiler_params=_cp("parallel", "parallel"),
    )(c_rows, w_ada, b_shard)


def ada_bwd(c_rows, w_ada, dm_shard, dm_full, n_ex, *, name):
    nl, d, ncol = w_ada.shape
    rows = c_rows.shape[0]
    tn = _tile(ncol, (768, 512, 256, 128))
    nj = ncol // tn

    def body(c_ref, w_ref, dm_ref, dmf_ref, gw_ref, gb_ref, dc_ref, dact_ref):
        l, j = pl.program_id(0), pl.program_id(1)
        act, act_vjp = jax.vjp(jax.nn.silu, c_ref[...])
        gw_ref[...] = lax.dot_general(act, dm_ref[...], _TN, precision=HI, preferred_element_type=F32)
        gb_ref[...] = jnp.sum(dmf_ref[...], axis=0, keepdims=True)
        part = lax.dot_general(dm_ref[...], w_ref[...], _NT, precision=HI, preferred_element_type=F32)

        @pl.when((l == 0) & (j == 0))
        def _():
            dact_ref[...] = part

        @pl.when((l > 0) | (j > 0))
        def _():
            dact_ref[...] += part

        @pl.when((l == nl - 1) & (j == nj - 1))
        def _():
            dc, = act_vjp(dact_ref[...])
            dc_ref[...] = jnp.sum(dc[n_ex:, :], axis=0, keepdims=True)

    return pl.pallas_call(
        body, name=name, grid=(nl, nj),
        in_specs=[pl.BlockSpec((rows, d), lambda l, j: (0, 0)), pl.BlockSpec((None, d, tn), lambda l, j: (l, 0, j)),
                  pl.BlockSpec((None, rows, tn), lambda l, j: (l, 0, j)),
                  pl.BlockSpec((None, rows, dm_full.shape[-1]), lambda l, j: (l, 0, 0))],
        out_specs=[pl.BlockSpec((None, d, tn), lambda l, j: (l, 0, j)),
                   pl.BlockSpec((None, 1, dm_full.shape[-1]), lambda l, j: (l, 0, 0)),
                   pl.BlockSpec((1, d), lambda l, j: (0, 0))],
        out_shape=[jax.ShapeDtypeStruct((nl, d, ncol), F32), jax.ShapeDtypeStruct((nl, 1, dm_full.shape[-1]), F32),
                   jax.ShapeDtypeStruct((1, d), F32)],
        scratch_shapes=[pltpu.VMEM((rows, d), F32)], compiler_params=_cp("arbitrary", "arbitrary"),
    )(c_rows, w_ada, dm_shard, dm_full)


def adamw(w, g, m, v, *, name):
    shape = w.shape
    cols = shape[-1]
    rows = int(np.prod(shape[:-1])) if len(shape) > 1 else 1
    tr = _row_tile(rows, 256)

    def body(w_ref, g_ref, m_ref, v_ref, d_ref, nm_ref, nv_ref):
        gg = g_ref[...]
        nm = ADAM_B1 * m_ref[...] + (1.0 - ADAM_B1) * gg
        nv = ADAM_B2 * v_ref[...] + (1.0 - ADAM_B2) * jnp.square(gg)
        m_hat = nm / (1.0 - ADAM_B1 ** ADAM_STEP)
        v_hat = nv / (1.0 - ADAM_B2 ** ADAM_STEP)
        d_ref[...] = -ADAM_LR * (m_hat / (jnp.sqrt(v_hat) + ADAM_EPS) + ADAM_WD * w_ref[...])
        nm_ref[...] = nm
        nv_ref[...] = nv

    spec = pl.BlockSpec((tr, cols), lambda i: (i, 0))
    out = pl.pallas_call(body, name=name, grid=(rows // tr,), in_specs=[spec] * 4, out_specs=[spec] * 3,
                         out_shape=[jax.ShapeDtypeStruct((rows, cols), F32)] * 3, compiler_params=_cp("parallel"),
                         )(*[a.reshape(rows, cols) for a in (w, g, m, v)])
    return tuple(o.reshape(shape) for o in out)


def local_step(h0, target, mods, lw, wf, small, *, lc):
    nb, t, d = h0.shape
    nt, nct = t // TM, lc // TM
    s_len = t - lc
    nl = len(lw)
    nsh = wf.shape[1]
    consts = _post_consts()
    cos_b, sin_b = _rope_tables(s_len, lc, HEAD_DIM, GQA_HEADS)
    cos_m, sin_m = _rope_tables(s_len, lc, MLA_ROPE, MLA_HEADS)
    rc = functools.partial(rowcall, nb=nb, nt=nt, nct=nct)
    flat = lambda a: a.reshape(nb * t, a.shape[-1])
    unflat = lambda a: a.reshape(nb, t, a.shape[-1])
    vec = lambda a: a.reshape(1, -1)

    def norm_first(h, g, shift, scale, tag):
        n, = rc(tag + "_norm", lambda _, *a: (f_normmod(*a),), [(h, 'tok'), (vec(g), 'full'), (shift, 'mod'), (scale, 'mod')],
                [('tok', d, BF16)])
        return n

    def res_norm(h, y, gate, coef, g, shift, scale, tag):
        def fn(_, hh, yy, gt, gn, sh, sc):
            h2 = hh + coef * gt * yy
            return h2, f_normmod(h2, gn, sh, sc)

        return rc(tag + "_res_norm", fn, [(h, 'tok'), (y, 'tok'), (gate, 'mod'), (vec(g), 'full'), (shift, 'mod'), (scale, 'mod')],
                  [('tok', d, F32), ('tok', d, BF16)])

    def res_last(h, y, gate, coef, tag):
        h2, = rc(tag + "_res", lambda _, hh, yy, gt: (hh + coef * gt * yy,), [(h, 'tok'), (y, 'tok'), (gate, 'mod')], [('tok', d, F32)])
        return h2

    def res_bwd_last(dh2, y, gate, coef, tag):
        return rc(tag + "_res_bwd", lambda _, dd, yy, gt: (coef * gt * dd, jnp.sum(coef * yy * dd, axis=0, keepdims=True)),
                  [(dh2, 'tok'), (y, 'tok'), (gate, 'mod')], [('tok', d, BF16), ('mod', d)])

    def norm_bwd_first(h, g, shift, scale, dn, dres, tag):
        def fn(_, hh, gn, sh, sc, dnn, dr):
            dh, dg, dsh, dsc = jax.vjp(f_normmod, hh, gn, sh, sc)[1](dnn)
            return dh + dr, dg, dsh, dsc

        return rc(tag + "_norm_bwd", fn, [(h, 'tok'), (vec(g), 'full'), (shift, 'mod'), (scale, 'mod'), (dn, 'tok'), (dres, 'tok')],
                  [('tok', d, F32), ('full', (1, d)), ('mod', d), ('mod', d)])

    def norm_bwd_res_bwd(h, g, shift, scale, dn, dres, y_prev, gate_prev, coef_prev, tag):
        def fn(_, hh, gn, sh, sc, dnn, dr, yy, gt):
            dh, dg, dsh, dsc = jax.vjp(f_normmod, hh, gn, sh, sc)[1](dnn)
            dh = dh + dr
            return dh, dg, dsh, dsc, coef_prev * gt * dh, jnp.sum(coef_prev * yy * dh, axis=0, keepdims=True)

        return rc(tag + "_norm_bwd", fn,
                  [(h, 'tok'), (vec(g), 'full'), (shift, 'mod'), (scale, 'mod'), (dn, 'tok'), (dres, 'tok'), (y_prev, 'tok'), (gate_prev, 'mod')],
                  [('tok', d, F32), ('full', (1, d)), ('mod', d), ('mod', d), ('tok', d, BF16), ('mod', d)])

    def ffn_fwd(n, l, base, tag):
        gg, uu, act = ffn_up(flat(n), wf, l, base, name=tag + "_up")
        return unflat(ffn_down(act, wf, l, base, name=tag + "_down")), (n, gg, uu, act)

    def ffn_bwd(dy, saved, l, base, tag):
        n, gg, uu, act = saved
        dw_d = ffn_dw(act, flat(dy), nsh, name=tag + "_down_dw")
        dgg, duu = ffn_down_bwd(flat(dy), gg, uu, wf, l, base, name=tag + "_down_dx")
        dw_g = ffn_dw(dgg, flat(n), nsh, name=tag + "_gate_dw")
        dw_u = ffn_dw(duu, flat(n), nsh, name=tag + "_up_dw")
        return unflat(ffn_up_bwd(dgg, duu, wf, l, base, name=tag + "_up_dx")), [dw_g, dw_u, dw_d]

    def post_ins(p, sm, w):
        return [(p, ('tokc', MAIN_PAD, 0)), (cos_b, 'pos'), (sin_b, 'pos'), (cos_m, 'pos'), (sin_m, 'pos'),
                (vec(sm['gqa_q_norm']), 'full'), (vec(sm['gqa_k_norm']), 'full'), (vec(sm['mla_q_norm']), 'full'),
                (vec(sm['mla_kv_norm']), 'full'), (w['w_uq'], 'full'), (w['w_ukv'], 'full')] + [(c, 'full') for c in consts]

    def mix_fwd(n, sm, w, ctx_q, tag):
        p = unflat(mm_resident(flat(n), w['w_in'], out_dtype=BF16, name=tag + "_in"))
        parts = rc(tag + "_post", lambda _, pp, *a: f_post(pp.astype(F32), *a), post_ins(p, sm, w),
                   [('tok', wd, BF16) for wd in POST_WIDTHS])
        aq, ak, av, bq, bk, bv, mq, mk, mv = parts
        bias = na_expand_bias(sm['na_rel_bias'], tag + "_bias")
        o_a, lse_a = na_fwd(aq, ak, av, bias, lc=lc, ctx_q=ctx_q, name=tag + "_na")
        o_b, lse_b = gqa_fwd(bq, bk, bv, lc=lc, ctx_q=ctx_q, name=tag + "_gqa")
        o_m, lse_m = mla_fwd(mq, mk, mv, lc=lc, ctx_q=ctx_q, name=tag + "_mla")
        fo = [o_a, o_b, o_m]
        ys = [unflat(mm_resident(flat(o), w[k], out_dtype=BF16, name=tag + "_br" + k[-1])) for o, k in zip(fo, ('w_a', 'w_b', 'w_c'))]
        gcols = [(p, ('tokc', d, MAIN_PAD // d + j)) for j in range(3)]
        y, = rc(tag + "_merge", lambda _, *a: (f_merge(*[v.astype(F32) for v in a]),), gcols + [(v, 'tok') for v in ys],
                [('tok', d, BF16)])
        z = unflat(mm_resident(flat(y), w['w_o'], name=tag + "_out"))
        saved = (n, p, (aq, ak, av, lse_a, bias), (bq, bk, bv, lse_b), (mq, mk, mv, lse_m), fo, ys, y)
        return z, saved

    def mix_bwd(dz, saved, sm, w, ctx_q, tag):
        n, p, (aq, ak, av, lse_a, bias), (bq, bk, bv, lse_b), (mq, mk, mv, lse_m), fo, ys, y = saved
        dw_o = mm(flat(y), flat(dz), ta=True, name=tag + "_out_dw")
        dy = unflat(mm_resident(flat(dz), w['w_o'], tb=True, name=tag + "_out_dx"))
        gcols = [(p, ('tokc', d, MAIN_PAD // d + j)) for j in range(3)]

        def merge_bwd(_, ga, gb, gm, ya, yb, ym, dyy):
            dga, dgb, dgm, dya, dyb, dym = jax.vjp(f_merge, *[v.astype(F32) for v in (ga, gb, gm, ya, yb, ym)])[1](dyy)
            return dya, dyb, dym, jnp.concatenate([dga, dgb, dgm], axis=-1)

        dya, dyb, dym, dgl = rc(tag + "_merge_bwd", merge_bwd, gcols + [(v, 'tok') for v in ys] + [(dy, 'tok')],
                                [('tok', d, BF16)] * 3 + [('tok', 3 * d, BF16)])
        dws, dos = {}, []
        for o, dyk, k in zip(fo, (dya, dyb, dym), ('w_a', 'w_b', 'w_c')):
            dws[k] = mm(flat(o), flat(dyk), ta=True, name=tag + "_br" + k[-1] + "_dw")
            dos.append(unflat(mm_resident(flat(dyk), w[k], tb=True, out_dtype=BF16, name=tag + "_br" + k[-1] + "_dx")))
        do_a, do_b, do_m = dos
        daq, dak, dav, dbias = na_bwd(aq, ak, av, bias, lse_a, do_a, lc=lc, ctx_q=ctx_q, name=tag + "_na_bwd")
        dbq, dbk, dbv = gqa_bwd(bq, bk, bv, lse_b, do_b, lc=lc, ctx_q=ctx_q, name=tag + "_gqa_bwd")
        dmq, dmk, dmv = mla_bwd(mq, mk, mv, lse_m, do_m, lc=lc, ctx_q=ctx_q, name=tag + "_mla_bwd")
        d_rel = na_reduce_bias(dbias, tag + "_relb")
        cots = [daq, dak, dav, dbq, dbk, dbv, dmq, dmk, dmv]
        ins = post_ins(p, sm, w)
        n_in = len(ins)

        def post_bwd(_, *a):
            prim, cot, dgl_v = a[:11], list(a[n_in:n_in + N_POST]), a[-1]
            for j in POST_QK:
                cot[j] = cot[j] * LN2
            outs = jax.vjp(lambda pp, qn, kn, mqn, mkvn, wuq, wukv: f_post(pp, *prim[1:5], qn, kn, mqn, mkvn, wuq, wukv, *a[11:n_in]),
                           prim[0].astype(F32), *prim[5:11])[1](tuple(cot))
            return (jnp.concatenate([outs[0].astype(BF16), dgl_v], axis=-1),) + tuple(outs[1:])

        res = rc(tag + "_post_bwd", post_bwd, ins + [(cv, 'tok') for cv in cots] + [(dgl, 'tok')],
                 [('tok', MAIN_PAD + 3 * d, BF16), ('full', (1, HEAD_DIM)), ('full', (1, HEAD_DIM)), ('full', (1, MLA_Q_RANK)),
                  ('full', (1, MLA_KV_RANK)), ('full', w['w_uq'].shape), ('full', w['w_ukv'].shape)])
        dp, dqn, dkn, dmqn, dmkvn, dw_uq, dw_ukv = res
        dw_in = ffn_dw(flat(dp), flat(n), 4, name=tag + "_in_dw").reshape(-1, d)
        dn = unflat(mm_resident(flat(dp), w['w_in'], tb=True, name=tag + "_in_dx"))
        dsm = {'na_rel_bias': d_rel, 'gqa_q_norm': dqn.reshape(-1), 'gqa_k_norm': dkn.reshape(-1),
               'mla_q_norm': dmqn.reshape(-1), 'mla_kv_norm': dmkvn.reshape(-1)}
        dwl = {'w_in': dw_in, 'w_uq': dw_uq, 'w_ukv': dw_ukv, 'w_o': dw_o, **dws}
        return dn, dsm, dwl

    subs = [(l, kind, gain, coef) for l in range(nl)
            for kind, gain, coef in (('ffn1', 'ffn1_norm', 0.5), ('mix', 'mix_norm', 1.0), ('ffn2', 'ffn2_norm', 0.5))]
    ns = len(subs)
    sms = [{k: small[k][l] for k in SMALL_LAYER} for l in range(nl)]

    def params(k):
        l, _, gain, _ = subs[k]
        j = 3 * (k % 3)
        return small[gain][l], mods[l][j], mods[l][j + 1], mods[l][j + 2]

    def tag_of(k):
        return f"l{subs[k][0]}_{subs[k][1]}"

    h = h0
    g0, sh0, sc0, _ = params(0)
    n = norm_first(h, g0, sh0, sc0, tag_of(0))
    h_in, core_out, saved = [None] * ns, [None] * ns, [None] * ns
    for k, (l, kind, _, coef) in enumerate(subs):
        h_in[k] = h
        if kind == 'mix':
            core_out[k], saved[k] = mix_fwd(n, sms[l], lw[l], l + 1 < nl, tag_of(k))
        else:
            core_out[k], saved[k] = ffn_fwd(n, l, 0 if kind == 'ffn1' else 3, tag_of(k))
        gate = params(k)[3]
        if k + 1 < ns:
            gn, shn, scn, _ = params(k + 1)
            h, n = res_norm(h, core_out[k], gate, coef, gn, shn, scn, tag_of(k))
        else:
            h = res_last(h, core_out[k], gate, coef, tag_of(k))

    def final(is_ctx, hh, gg, tgt):
        def loss_fn(hv, gv):
            return 0.5 * jnp.sum(jnp.mean(jnp.square(_rms(hv, gv) - tgt), axis=-1))

        keep = jnp.where(is_ctx, 0.0, 1.0)
        loss, (dh, dg) = jax.value_and_grad(loss_fn, argnums=(0, 1))(hh, gg)
        return dh * keep, jnp.full((1, LANE), loss * keep, F32), dg * keep

    dh, loss, dg_final = rc("final_loss", final, [(h, 'tok'), (vec(small['final_norm']), 'full'), (target, 'lat')],
                            [('tok', d, F32), ('full', (1, LANE)), ('full', (1, d))])

    dsmall = {k: [None] * nl for k in SMALL_LAYER}
    dmods, dlw, dwf = [[None] * N_MOD for _ in range(nl)], [None] * nl, [[None] * 6 for _ in range(nl)]
    l_last, _, _, coef_last = subs[-1]
    dcore, dmods[l_last][8] = res_bwd_last(dh, core_out[-1], params(ns - 1)[3], coef_last, tag_of(ns - 1))
    for k in reversed(range(ns)):
        l, kind, gain, _ = subs[k]
        j = 3 * (k % 3)
        if kind == 'mix':
            dn, dsm, dlw[l] = mix_bwd(dcore, saved[k], sms[l], lw[l], l + 1 < nl, tag_of(k))
            for name, val in dsm.items():
                dsmall[name][l] = val
        else:
            base = 0 if kind == 'ffn1' else 3
            dn, dwf[l][base:base + 3] = ffn_bwd(dcore, saved[k], l, base, tag_of(k))
        g, shift, scale, _ = params(k)
        if k > 0:
            lp, _, _, coef_prev = subs[k - 1]
            dh, dg, dmods[l][j], dmods[l][j + 1], dcore, dmods[lp][3 * ((k - 1) % 3) + 2] = norm_bwd_res_bwd(
                h_in[k], g, shift, scale, dn, dh, core_out[k - 1], params(k - 1)[3], coef_prev, tag_of(k))
        else:
            dh, dg, dmods[l][j], dmods[l][j + 1] = norm_bwd_first(h_in[k], g, shift, scale, dn, dh, tag_of(k))
        dsmall[gain][l] = dg.reshape(d)
    dsmall = {k: jnp.stack(v) for k, v in dsmall.items()}
    dsmall['final_norm'] = dg_final.reshape(d)
    return loss, dh, dmods, dlw, dwf, dsmall


def _pack(parts, pad_rows):
    flat, where, off = [], [], 0
    for a in parts:
        n = _ceil_to(a.size, PACK_W)
        flat.append(jnp.pad(a.reshape(-1), (0, n - a.size)))
        where.append((off, n // PACK_W))
        off += n // PACK_W
    total = _ceil_to(off, pad_rows)
    if total > off:
        flat.append(jnp.zeros(((total - off) * PACK_W,), flat[0].dtype))
    return jnp.concatenate(flat).reshape(total, PACK_W), where


def _unpack(buf, where, shape):
    off, rows = where
    return buf[off:off + rows].reshape(-1)[:int(np.prod(shape))].reshape(shape)


def layer_weights(full, l):
    wi = full['w_in'][l]
    d = wi.shape[0]
    return {
        'w_in': jnp.concatenate([wi[:, :MAIN_W], jnp.zeros((d, MAIN_PAD - MAIN_W), wi.dtype), wi[:, MAIN_W:]], axis=1),
        'w_uq': _heads_to_parts(full['mla_w_uq'][l], MLA_NOPE).astype(F32),
        'w_ukv': _heads_to_parts(full['mla_w_ukv'][l], MLA_NOPE).astype(F32),
        'w_a': full['w_branch_a'][l], 'w_b': full['w_branch_b'][l], 'w_c': full['w_branch_c'][l], 'w_o': full['w_out'][l]}


def layer_grads_by_name(dlw):
    per_name = {k: [] for k, _ in BIG}
    for g in dlw:
        per_name['w_in'].append(jnp.concatenate([g['w_in'][:MAIN_W], g['w_in'][MAIN_PAD:]], axis=0))
        per_name['mla_w_uq'].append(_parts_to_heads(g['w_uq'], MLA_NOPE))
        per_name['mla_w_ukv'].append(_parts_to_heads(g['w_ukv'], MLA_NOPE))
        per_name['w_branch_a'].append(g['w_a'])
        per_name['w_branch_b'].append(g['w_b'])
        per_name['w_branch_c'].append(g['w_c'])
        per_name['w_out'].append(g['w_o'])
    return per_name


def kernel(x, c, ctx, c_ctx, w_ada, b_ada, ffn1_norm, ffn1_w_gate, ffn1_w_up, ffn1_w_down, mix_norm, w_in, na_rel_bias, gqa_q_norm, gqa_k_norm, mla_q_norm, mla_kv_norm, mla_w_uq, mla_w_ukv, w_branch_a, w_branch_b, w_branch_c, w_out, ffn2_norm, ffn2_w_gate, ffn2_w_up, ffn2_w_down, final_norm, loss_target, m_c_ctx, m_w_ada, m_b_ada, m_ffn1_norm, m_ffn1_w_gate, m_ffn1_w_up, m_ffn1_w_down, m_mix_norm, m_w_in, m_na_rel_bias, m_gqa_q_norm, m_gqa_k_norm, m_mla_q_norm, m_mla_kv_norm, m_mla_w_uq, m_mla_w_ukv, m_w_branch_a, m_w_branch_b, m_w_branch_c, m_w_out, m_ffn2_norm, m_ffn2_w_gate, m_ffn2_w_up, m_ffn2_w_down, m_final_norm, v_c_ctx, v_w_ada, v_b_ada, v_ffn1_norm, v_ffn1_w_gate, v_ffn1_w_up, v_ffn1_w_down, v_mix_norm, v_w_in, v_na_rel_bias, v_gqa_q_norm, v_gqa_k_norm, v_mla_q_norm, v_mla_kv_norm, v_mla_w_uq, v_mla_w_ukv, v_w_branch_a, v_w_branch_b, v_w_branch_c, v_w_out, v_ffn2_norm, v_ffn2_w_gate, v_ffn2_w_up, v_ffn2_w_down, v_final_norm):
    args = locals()
    wts = {k: args[k] for k in WEIGHTS}
    mom = {k: args['m_' + k] for k in WEIGHTS}
    var = {k: args['v_' + k] for k in WEIGHTS}
    nb, s_len, d = x.shape
    lc = ctx.shape[1]
    nl = w_ada.shape[0]
    nsh, ndev = 4, 8
    mx, my, mc = _place()
    sidx = 2 * mx + my
    didx = 4 * mx + 2 * my + mc
    assert d % LANE == 0 and MAIN_PAD % d == 0 and lc % TQ == 0 and s_len % TQ == 0 and s_len // GRID_W >= NA_ROWS

    wpack, wwhere = _pack([wts[k].astype(BF16) for k, _ in BIG], 32)
    wall = gather_shards(wpack.reshape(2, -1, PACK_W), name="gather_weights").reshape(nsh, -1, PACK_W)
    full = {}
    for (k, ax), wh in zip(BIG, wwhere):
        shp = wts[k].shape
        parts = jnp.stack([_unpack(wall[s], wh, shp) for s in range(nsh)])
        if ax == 1:
            full[k] = parts.transpose(1, 2, 0, 3).reshape(nl, shp[1], nsh * shp[2])
        else:
            full[k] = parts.transpose(1, 0, 2, 3).reshape(nl, nsh * shp[1], shp[2])
    lw = [layer_weights(full, l) for l in range(nl)]
    wl = jnp.stack([(wts[k].transpose(0, 2, 1) if tr else wts[k]).astype(BF16) for k, tr in zip(FFN_NAMES, FFN_TRANSPOSED)], axis=1)
    wf = gather_ffn(wl, name="gather_ffn")

    n_ex = ndev * nb
    ncol = w_ada.shape[-1]
    c_all = all_gather(c, name="gather_cond", with_c=True).reshape(n_ex, d)
    c_rows = jnp.concatenate([c_all, jnp.broadcast_to(c_ctx[None], (n_ex, d))], axis=0)
    b_shard = lax.dynamic_slice_in_dim(b_ada, sidx * ncol, ncol, axis=1)[:, None, :]
    mod_sh = ada_fwd(c_rows, w_ada, b_shard, name="ada_fwd")
    mod_all = all_gather(mod_sh, name="gather_mod", with_c=False)
    mod_all = mod_all.transpose(1, 2, 0, 3).reshape(nl, 2 * n_ex, nsh * ncol)
    mod_x = lax.dynamic_slice_in_dim(mod_all, didx * nb, nb, axis=1)
    mod_c = jnp.broadcast_to(mod_all[:, n_ex:n_ex + 1], mod_x.shape)
    mods = [[jnp.stack([mod_c[l, :, j * d:(j + 1) * d], mod_x[l, :, j * d:(j + 1) * d]], axis=1)[:, :, None, :]
             for j in range(N_MOD)] for l in range(nl)]

    small = {k: wts[k] for k in SMALL_LAYER + ['final_norm']}
    h0 = jnp.concatenate([ctx, x], axis=1)
    loss_part, dh0, dmods, dlw, dwf, dsmall = local_step(h0, loss_target, mods, lw, wf, small, lc=lc)
    grad_x = dh0[:, lc:]

    dmod_mine = jnp.stack([jnp.concatenate([m[:, :, 0, :] for m in dmods[l]], axis=-1) for l in range(nl)])
    small_names = SMALL_LAYER + ['final_norm']
    spack, swhere = _pack([loss_part] + [dsmall[k] for k in small_names] + [dmod_mine], 8)
    sall = all_gather(spack, name="gather_small", with_c=True)
    ssum = sum_slots(sall, name="sum_small")
    loss = _unpack(ssum, swhere[0], (1, LANE))[0, 0]
    grads = {k: _unpack(ssum, wh, wts[k].shape) for k, wh in zip(small_names, swhere[1:])}
    off, rows = swhere[-1]
    dm_all = sall[:, off:off + rows].reshape(ndev, -1)[:, :dmod_mine.size].reshape((ndev,) + dmod_mine.shape)
    dm_all = dm_all.transpose(1, 3, 0, 2, 4).reshape(nl, 2, n_ex, N_MOD * d)
    dm_rows = jnp.concatenate([dm_all[:, 1], dm_all[:, 0]], axis=1)
    dm_shard = lax.dynamic_slice_in_dim(dm_rows, sidx * ncol, ncol, axis=2)
    grads['w_ada'], gb, dc_part = ada_bwd(c_rows, w_ada, dm_shard, dm_rows, n_ex, name="ada_bwd")
    grads['b_ada'] = gb.reshape(b_ada.shape)
    dc_all = all_gather(jnp.pad(dc_part, ((0, 7), (0, 0))), name="gather_dcond", with_c=False)
    grads['c_ctx'] = sum_slots(dc_all, name="sum_dcond")[0]

    per_name = layer_grads_by_name(dlw)
    pieces, gwhere, off = [], [], 0
    for k, ax in BIG:
        shp = wts[k].shape
        for g in per_name[k]:
            if ax == 1 and k not in GRAD_TRANSPOSED:
                pieces.append(g.reshape(shp[1], nsh, shp[2]).transpose(1, 0, 2).reshape(nsh, -1))
            else:
                pieces.append(g.reshape(nsh, -1))
        n = int(np.prod(shp))
        if n % PACK_W:
            pieces.append(jnp.zeros((nsh, _ceil_to(n, PACK_W) - n), F32))
        gwhere.append((off, _ceil_to(n, PACK_W) // PACK_W))
        off += _ceil_to(n, PACK_W) // PACK_W
    if off % 128:
        pieces.append(jnp.zeros((nsh, (_ceil_to(off, 128) - off) * PACK_W), F32))
    half = _ceil_to(off, 128) // 2
    gpack = jnp.concatenate(pieces, axis=1).reshape(nsh, 2, half, PACK_W)
    from_pair = pair_exchange_halves(gpack, name="reduce_pair")
    chip_sum = add_kept_half(gpack, from_pair, jnp.reshape(mc, (1,)).astype(jnp.int32), name="reduce_pair_add",
                             out_dtype=BF16)
    from_xy = all_to_all_xy(chip_sum, name="reduce_xy")
    reduced = sum_slots(from_xy, name="reduce_xy_add")
    gfull = pair_all_gather(reduced, name="reduce_share").reshape(2 * half, PACK_W)
    for (k, _), wh in zip(BIG, gwhere):
        shp = wts[k].shape
        grads[k] = (_unpack(gfull, wh, (shp[0], shp[2], shp[1])).transpose(0, 2, 1) if k in GRAD_TRANSPOSED
                    else _unpack(gfull, wh, shp))
    for k, tr, g in zip(FFN_NAMES, FFN_TRANSPOSED, reduce_ffn(dwf[0], dwf[1], name="reduce_ffn")):
        grads[k] = g.transpose(0, 2, 1) if tr else g

    outs = {k: adamw(wts[k], grads[k], mom[k], var[k], name="adamw_" + k) for k in WEIGHTS}
    return (loss, grad_x, *[grads[k] for k in WEIGHTS], *[outs[k][0] for k in WEIGHTS], *[outs[k][1] for k in WEIGHTS],
            *[outs[k][2] for k in WEIGHTS])
```

```python
import functools

import jax
import jax.numpy as jnp
import numpy as np
from jax import lax
from jax.experimental import pallas as pl
from jax.experimental.pallas import tpu as pltpu

F32 = jnp.float32
BF16 = jnp.bfloat16
HI = lax.Precision.HIGHEST
MESH = pl.DeviceIdType.MESH
ANY = pl.BlockSpec(memory_space=pl.ANY)

V7X_VMEM_BYTES = 64 * 1024 * 1024
VMEM_LIMIT = V7X_VMEM_BYTES - 8 * 1024 * 1024
LANE = 128
PACK_W = 1024

GRID_W = 64
HEAD_DIM = 64
NA_HEADS, NA_ROWS, NA_COLS = 4, 8, 16
GQA_HEADS, GQA_KV_HEADS = 8, 2
MLA_HEADS, MLA_Q_RANK, MLA_KV_RANK, MLA_NOPE, MLA_ROPE, MLA_V = 4, 256, 128, 64, 32, 64
N_MOD = 9
ROPE_THETA = 10000.0
EPS = 1e-6
NEG_BIG = -1e30
NA_W = NA_HEADS * HEAD_DIM
GQ_W = GQA_HEADS * HEAD_DIM
GK_W = GQA_KV_HEADS * HEAD_DIM
MAIN_W = 3 * NA_W + GQ_W + 2 * GK_W + MLA_Q_RANK + MLA_KV_RANK + MLA_ROPE
MAIN_PAD = 2048
LOG2E, LN2 = float(np.log2(np.e)), float(np.log(2.0))
Q_SCALE = HEAD_DIM ** -0.5 * LOG2E
MLA_Q_SCALE = (MLA_NOPE + MLA_ROPE) ** -0.5 * LOG2E
TQ = 256
TM = 256

ADAM_LR, ADAM_B1, ADAM_B2, ADAM_EPS, ADAM_WD, ADAM_STEP = 0.001, 0.9, 0.999, 1e-08, 0.01, 10

ARG_NAMES = ['x', 'c', 'ctx', 'c_ctx', 'w_ada', 'b_ada', 'ffn1_norm', 'ffn1_w_gate', 'ffn1_w_up', 'ffn1_w_down', 'mix_norm', 'w_in',
             'na_rel_bias', 'gqa_q_norm', 'gqa_k_norm', 'mla_q_norm', 'mla_kv_norm', 'mla_w_uq', 'mla_w_ukv', 'w_branch_a',
             'w_branch_b', 'w_branch_c', 'w_out', 'ffn2_norm', 'ffn2_w_gate', 'ffn2_w_up', 'ffn2_w_down', 'final_norm']
WEIGHTS = ARG_NAMES[3:]
BIG = [('w_in', 1), ('mla_w_uq', 1), ('mla_w_ukv', 1), ('w_branch_a', 1), ('w_branch_b', 1), ('w_branch_c', 1), ('w_out', 0)]
GRAD_TRANSPOSED = ('w_in',)
FFN_NAMES = ['ffn1_w_gate', 'ffn1_w_up', 'ffn1_w_down', 'ffn2_w_gate', 'ffn2_w_up', 'ffn2_w_down']
FFN_TRANSPOSED = [True, True, False, True, True, False]
SMALL_LAYER = ['ffn1_norm', 'mix_norm', 'na_rel_bias', 'gqa_q_norm', 'gqa_k_norm', 'mla_q_norm', 'mla_kv_norm', 'ffn2_norm']


def _cp(*sem):
    return pltpu.CompilerParams(dimension_semantics=sem, vmem_limit_bytes=VMEM_LIMIT)


def _tile(dim, cands):
    for t in cands:
        if dim % t == 0:
            return t
    return dim


def _row_tile(rows, cap=512, mult=16):
    best = None
    for t in range(mult, min(rows, cap) + 1, mult):
        if rows % t == 0:
            best = t
    return best or rows


def _ceil_to(n, m):
    return -(-n // m) * m


def mm(a, b, *, name, ta=False, tb=False, out_dtype=F32, precise=False):
    m, k = (a.shape[1], a.shape[0]) if ta else a.shape
    n = b.shape[0] if tb else b.shape[1]
    tm = _tile(m, (512, 256, 128))
    tn = _tile(n, (1024, 1408, 512, 256, 128))
    tk = _tile(k, (1024, 1408, 512, 256, 128))
    nk = k // tk
    dims = (((0 if ta else 1,), (1 if tb else 0,)), ((), ()))

    def body(a_ref, b_ref, o_ref, *acc):
        if precise:
            part = lax.dot_general(a_ref[...].astype(F32), b_ref[...].astype(F32), dims, precision=HI, preferred_element_type=F32)
        else:
            part = lax.dot_general(a_ref[...].astype(BF16), b_ref[...].astype(BF16), dims, preferred_element_type=F32)
        if nk == 1:
            o_ref[...] = part.astype(o_ref.dtype)
        else:
            acc_ref, = acc
            kk = pl.program_id(2)

            @pl.when(kk == 0)
            def _():
                acc_ref[...] = part

            @pl.when(kk > 0)
            def _():
                acc_ref[...] += part

            @pl.when(kk == nk - 1)
            def _():
                o_ref[...] = acc_ref[...].astype(o_ref.dtype)

    a_spec = pl.BlockSpec((tk, tm), lambda i, j, kk: (kk, i)) if ta else pl.BlockSpec((tm, tk), lambda i, j, kk: (i, kk))
    b_spec = pl.BlockSpec((tn, tk), lambda i, j, kk: (j, kk)) if tb else pl.BlockSpec((tk, tn), lambda i, j, kk: (kk, j))
    return pl.pallas_call(
        body, name=name, grid=(m // tm, n // tn, nk), in_specs=[a_spec, b_spec],
        out_specs=pl.BlockSpec((tm, tn), lambda i, j, kk: (i, j)),
        out_shape=jax.ShapeDtypeStruct((m, n), out_dtype),
        scratch_shapes=[pltpu.VMEM((tm, tn), F32)] if nk > 1 else [],
        compiler_params=_cp("parallel", "parallel", "arbitrary"),
    )(a, b)


def mm_resident(a, w, *, name, tb=False, out_dtype=F32):
    m, k = a.shape
    n = w.shape[0] if tb else w.shape[1]
    tm = _tile(m, (512, 256, 128))
    cn = n if tb else _tile(n, (1024, 512, 256, 128))

    def body(a_ref, w_ref, o_ref):
        aa = a_ref[...].astype(BF16)
        if tb:
            o_ref[...] = _dot(aa, w_ref[...], _NT).astype(o_ref.dtype)
        else:
            for c in range(n // cn):
                cols = slice(cn * c, cn * (c + 1))
                o_ref[:, cols] = _dot(aa, w_ref[:, cols]).astype(o_ref.dtype)

    return pl.pallas_call(
        body, name=name, grid=(m // tm,),
        in_specs=[pl.BlockSpec((tm, k), lambda i: (i, 0)), pl.BlockSpec(w.shape, lambda i: (0, 0), pipeline_mode=pl.Buffered(1))],
        out_specs=pl.BlockSpec((tm, n), lambda i: (i, 0)), out_shape=jax.ShapeDtypeStruct((m, n), out_dtype),
        compiler_params=_cp("parallel"),
    )(a, w)


FFN_GATE, FFN_UP, FFN_DOWN = 0, 1, 2


def _ffn_wspec(wf, l, which):
    _, nsh, _, cs, d = wf.shape
    return pl.BlockSpec((None, nsh, None, cs, d), lambda *_: (l, 0, which, 0, 0), pipeline_mode=pl.Buffered(1))


def _ffn_group(cs):
    for g in (1, 2, 4):
        if (g * cs) % LANE == 0:
            return g
    raise ValueError(cs)


def ffn_up(n, wf, l, base, *, name):
    m, d = n.shape
    nsh, cs = wf.shape[1], wf.shape[3]
    f = nsh * cs
    grp = _ffn_group(cs)
    tm = _tile(m, (512, 256, 128))

    def body(n_ref, wg_ref, wu_ref, g_ref, u_ref, a_ref):
        nn = n_ref[...]
        for c in range(nsh // grp):
            cols = slice(grp * cs * c, grp * cs * (c + 1))
            g = _dot(nn, wg_ref[grp * c:grp * (c + 1)].reshape(grp * cs, d), _NT)
            u = _dot(nn, wu_ref[grp * c:grp * (c + 1)].reshape(grp * cs, d), _NT)
            g_ref[:, cols] = g.astype(BF16)
            u_ref[:, cols] = u.astype(BF16)
            a_ref[:, cols] = f_act_gu(g, u).astype(BF16)

    ospec = pl.BlockSpec((tm, f), lambda i: (i, 0))
    return pl.pallas_call(
        body, name=name, grid=(m // tm,),
        in_specs=[pl.BlockSpec((tm, d), lambda i: (i, 0)), _ffn_wspec(wf, l, base + FFN_GATE), _ffn_wspec(wf, l, base + FFN_UP)],
        out_specs=[ospec] * 3, out_shape=[jax.ShapeDtypeStruct((m, f), BF16)] * 3, compiler_params=_cp("parallel"),
    )(n, wf, wf)


def ffn_down(act, wf, l, base, *, name):
    m, f = act.shape
    nsh, cs, d = wf.shape[1], wf.shape[3], wf.shape[4]
    tm = _tile(m, (512, 256, 128))

    def body(a_ref, wd_ref, y_ref):
        y_ref[...] = _dot(a_ref[...], wd_ref[...].reshape(f, d))

    return pl.pallas_call(
        body, name=name, grid=(m // tm,),
        in_specs=[pl.BlockSpec((tm, f), lambda i: (i, 0)), _ffn_wspec(wf, l, base + FFN_DOWN)],
        out_specs=pl.BlockSpec((tm, d), lambda i: (i, 0)), out_shape=jax.ShapeDtypeStruct((m, d), F32), compiler_params=_cp("parallel"),
    )(act, wf)


def ffn_down_bwd(dy, g, u, wf, l, base, *, name):
    m, d = dy.shape
    nsh, cs = wf.shape[1], wf.shape[3]
    f = nsh * cs
    grp = _ffn_group(cs)
    tm = _tile(m, (512, 256, 128))

    def body(dy_ref, g_ref, u_ref, wd_ref, dg_ref, du_ref):
        dd = dy_ref[...]
        for c in range(nsh // grp):
            cols = slice(grp * cs * c, grp * cs * (c + 1))
            dact = _dot(dd, wd_ref[grp * c:grp * (c + 1)].reshape(grp * cs, d), _NT)
            dg, du = jax.vjp(f_act_gu, g_ref[:, cols].astype(F32), u_ref[:, cols].astype(F32))[1](dact)
            dg_ref[:, cols] = dg.astype(BF16)
            du_ref[:, cols] = du.astype(BF16)

    fspec = pl.BlockSpec((tm, f), lambda i: (i, 0))
    return pl.pallas_call(
        body, name=name, grid=(m // tm,),
        in_specs=[pl.BlockSpec((tm, d), lambda i: (i, 0)), fspec, fspec, _ffn_wspec(wf, l, base + FFN_DOWN)],
        out_specs=[fspec] * 2, out_shape=[jax.ShapeDtypeStruct((m, f), BF16)] * 2, compiler_params=_cp("parallel"),
    )(dy, g, u, wf)


def ffn_up_bwd(dg, du, wf, l, base, *, name):
    m, f = dg.shape
    nsh, cs, d = wf.shape[1], wf.shape[3], wf.shape[4]
    tm = _tile(m, (512, 256, 128))

    def body(dg_ref, du_ref, wg_ref, wu_ref, dn_ref):
        dn_ref[...] = _dot(dg_ref[...], wg_ref[...].reshape(f, d)) + _dot(du_ref[...], wu_ref[...].reshape(f, d))

    fspec = pl.BlockSpec((tm, f), lambda i: (i, 0))
    return pl.pallas_call(
        body, name=name, grid=(m // tm,),
        in_specs=[fspec, fspec, _ffn_wspec(wf, l, base + FFN_GATE), _ffn_wspec(wf, l, base + FFN_UP)],
        out_specs=pl.BlockSpec((tm, d), lambda i: (i, 0)), out_shape=jax.ShapeDtypeStruct((m, d), F32), compiler_params=_cp("parallel"),
    )(dg, du, wf, wf)


def ffn_dw(a, b, nsh, *, name):
    m, f = a.shape
    d = b.shape[1]
    cs = f // nsh
    grp = _ffn_group(cs)
    tm = _tile(m, (1024, 512, 256, 128))

    def body(a_ref, b_ref, o_ref):
        part = _dot(a_ref[...], b_ref[...], _TN).reshape(grp, cs, d)
        i = pl.program_id(1)

        @pl.when(i == 0)
        def _():
            o_ref[...] = part

        @pl.when(i > 0)
        def _():
            o_ref[...] += part

    return pl.pallas_call(
        body, name=name, grid=(nsh // grp, m // tm),
        in_specs=[pl.BlockSpec((tm, grp * cs), lambda j, i: (i, j)), pl.BlockSpec((tm, d), lambda j, i: (i, 0))],
        out_specs=pl.BlockSpec((grp, cs, d), lambda j, i: (j, 0, 0)), out_shape=jax.ShapeDtypeStruct((nsh, cs, d), F32),
        compiler_params=_cp("parallel", "arbitrary"),
    )(a, b)


def rowcall(name, fn, ins, outs, *, nb, nt, nct):
    in_specs, arrays = [], []
    for arr, kind in ins:
        arrays.append(arr)
        if kind == 'tok':
            in_specs.append(pl.BlockSpec((None, TM, arr.shape[-1]), lambda b, t: (b, t, 0)))
        elif kind == 'lat':
            in_specs.append(pl.BlockSpec((None, TM, arr.shape[-1]), lambda b, t: (b, jnp.maximum(t - nct, 0), 0)))
        elif kind == 'pos':
            in_specs.append(pl.BlockSpec((TM, arr.shape[-1]), lambda b, t: (t, 0)))
        elif kind == 'mod':
            in_specs.append(pl.BlockSpec((None, None, 1, arr.shape[-1]), lambda b, t: (b, jnp.where(t >= nct, 1, 0), 0, 0)))
        elif kind == 'full':
            in_specs.append(pl.BlockSpec(arr.shape, lambda b, t, nd=arr.ndim: (0,) * nd))
        else:
            _, w, j = kind
            in_specs.append(pl.BlockSpec((None, TM, w), lambda b, t, j=j: (b, t, j)))
    out_specs, out_shape = [], []
    for o in outs:
        if o[0] == 'tok':
            out_specs.append(pl.BlockSpec((None, TM, o[1]), lambda b, t: (b, t, 0)))
            out_shape.append(jax.ShapeDtypeStruct((nb, nt * TM, o[1]), o[2]))
        elif o[0] == 'mod':
            out_specs.append(pl.BlockSpec((None, None, 1, o[1]), lambda b, t: (b, jnp.where(t >= nct, 1, 0), 0, 0)))
            out_shape.append(jax.ShapeDtypeStruct((nb, 2, 1, o[1]), F32))
        else:
            out_specs.append(pl.BlockSpec(o[1], lambda b, t, nd=len(o[1]): (0,) * nd))
            out_shape.append(jax.ShapeDtypeStruct(o[1], F32))
    n_in = len(ins)

    def body(*refs):
        b, t = pl.program_id(0), pl.program_id(1)
        res = fn(t < nct, *[r[...] for r in refs[:n_in]])
        for ref, o, val in zip(refs[n_in:], outs, res, strict=True):
            if o[0] == 'tok':
                ref[...] = val.astype(ref.dtype)
                continue
            first = ((t == 0) | (t == nct)) if o[0] == 'mod' else ((b == 0) & (t == 0))

            @pl.when(first)
            def _(ref=ref, val=val):
                ref[...] = val

            @pl.when(jnp.logical_not(first))
            def _(ref=ref, val=val):
                ref[...] += val

    return pl.pallas_call(body, name=name, grid=(nb, nt), in_specs=in_specs, out_specs=out_specs, out_shape=out_shape,
                          compiler_params=_cp("arbitrary", "arbitrary"))(*arrays)


def _rms(x, g):
    return x * lax.rsqrt(jnp.mean(x * x, axis=-1, keepdims=True) + EPS) * g


def f_normmod(h, g, shift, scale):
    return _rms(h, g) * (1.0 + scale) + shift


def f_act_gu(g, u):
    return jax.nn.silu(g) * u


def _dot_split(x, m, dims):
    hi = x.astype(BF16)
    lo = (x - hi.astype(F32)).astype(BF16)
    mb = m.astype(BF16)
    return (lax.dot_general(hi, mb, dims, preferred_element_type=F32) + lax.dot_general(lo, mb, dims, preferred_element_type=F32))


def dot_select(x, m):
    return _dot_select(x, m)


@jax.custom_vjp
def _dot_select(x, m):
    return _dot_split(x, m, (((1,), (0,)), ((), ())))


_dot_select.defvjp(lambda x, m: (_dot_split(x, m, (((1,), (0,)), ((), ()))), m),
                   lambda m, ct: (_dot_split(ct, m, (((1,), (1,)), ((), ()))), jnp.zeros_like(m)))


def f_merge(ga, gb, gm, ya, yb, ym):
    return jax.nn.sigmoid(ga) * ya + jax.nn.sigmoid(gb) * yb + jax.nn.sigmoid(gm) * ym


def f_post(p, cb, sb, cm, sm, qn, kn, mqn, mkvn, wuq, wukv, s_b, r_b, t_b, r_m, rep, dup):
    def hnorm(x, g, w):
        ms = dot_select(x * x, s_b[:w, :w])
        gw = dot_select(g, t_b[:, :w])
        return x * lax.rsqrt(ms + EPS) * gw

    def rope(x, cos, sin, rot):
        return x * cos + dot_select(x, rot) * sin

    o = 3 * NA_W
    a_q, a_k, a_v = p[:, 0:NA_W], p[:, NA_W:2 * NA_W], p[:, 2 * NA_W:o]
    b_q = rope(hnorm(p[:, o:o + GQ_W], qn, GQ_W), cb, sb, r_b)
    o += GQ_W
    b_k = rope(hnorm(p[:, o:o + GK_W], kn, GK_W), cb[:, :GK_W], sb[:, :GK_W], r_b[:GK_W, :GK_W])
    b_v = p[:, o + GK_W:o + 2 * GK_W]
    o += 2 * GK_W
    q_lat = jnp.dot(_rms(p[:, o:o + MLA_Q_RANK], mqn).astype(BF16), wuq.astype(BF16), preferred_element_type=F32)
    o += MLA_Q_RANK
    kv_lat = jnp.dot(_rms(p[:, o:o + MLA_KV_RANK], mkvn).astype(BF16), wukv.astype(BF16), preferred_element_type=F32)
    o += MLA_KV_RANK
    nw = MLA_HEADS * MLA_NOPE
    mq_nope, mq_rope = q_lat[:, :nw], rope(q_lat[:, nw:], cm, sm, r_m)
    mk_nope, m_v = kv_lat[:, :nw], kv_lat[:, nw:]
    mk_rope = dot_select(rope(p[:, o:o + LANE], cm, sm, r_m), rep)
    b_k2 = dot_select(b_k, dup)
    b_v2 = dot_select(b_v, dup)
    mq_cat = jnp.concatenate([mq_nope[:, :LANE], mq_rope, mq_nope[:, LANE:], mq_rope], axis=1) * MLA_Q_SCALE
    mk_cat = jnp.concatenate([mk_nope[:, :LANE], mk_rope, mk_nope[:, LANE:], mk_rope], axis=1)
    return (a_q * Q_SCALE, a_k, a_v, b_q * Q_SCALE, b_k2, b_v2, mq_cat, mk_cat, m_v)


POST_QK = (0, 1, 3, 4, 6, 7)


POST_WIDTHS = (NA_W, NA_W, NA_W, GQ_W, 2 * GK_W, 2 * GK_W, 4 * LANE, 4 * LANE, MLA_HEADS * MLA_V)
N_POST = len(POST_WIDTHS)


_NT = (((1,), (1,)), ((), ()))
_TN = (((0,), (0,)), ((), ()))


def _dot(a, b, dims=None):
    if dims is None:
        return jnp.dot(a, b, preferred_element_type=F32)
    return lax.dot_general(a, b, dims, preferred_element_type=F32)


def _lanes(lo, width):
    lane = lax.broadcasted_iota(jnp.int32, (1, LANE), 1)
    return (lane >= lo) & (lane < lo + width)


def _only(x, mask):
    return jnp.where(mask, x, jnp.zeros_like(x))


def _stack_pair(x, width, lo):
    return jnp.concatenate([_only(x, _lanes(lo, width)), _only(x, _lanes(lo + width, width))], axis=0)


def _pair_softmax(s):
    m = jnp.max(s, axis=-1, keepdims=True)
    p = jnp.exp2(s - m)
    l = jnp.sum(p, axis=-1, keepdims=True)
    return p, l, m + jnp.log2(l)


def gqa_fwd(q, k2, v2, *, lc, ctx_q, name):
    nb, t, qw = q.shape
    npair = qw // LANE
    per_kv = npair // GQA_KV_HEADS
    nctb = lc // TQ

    def body(q_ref, k_ref, v_ref, o_ref, lse_ref):
        i = pl.program_id(1)

        def run(rows):
            for pr in range(npair):
                lanes = slice(LANE * pr, LANE * (pr + 1))
                kv = slice(LANE * (pr // per_kv), LANE * (pr // per_kv + 1))
                kk, vv = k_ref[rows, kv], v_ref[rows, kv]
                outs = []
                for e in range(2):
                    p, l, lse = _pair_softmax(_dot(_only(q_ref[:, lanes], _lanes(HEAD_DIM * e, HEAD_DIM)), kk, _NT))
                    outs.append(_dot(p.astype(BF16), vv) / l)
                    lse_ref[2 * pr + e] = lse
                o_ref[:, lanes] = jnp.where(_lanes(0, HEAD_DIM), outs[0], outs[1]).astype(o_ref.dtype)

        @pl.when(i < nctb)
        def _():
            if ctx_q:
                run(pl.ds(0, lc))
            else:
                o_ref[...] = jnp.zeros_like(o_ref)
                lse_ref[...] = jnp.zeros_like(lse_ref)

        @pl.when(i >= nctb)
        def _():
            run(pl.ds(0, t))

    qmap = lambda b, i: (b, i, 0)
    kmap = lambda b, i: (b, 0, 0)
    kw = k2.shape[-1]
    return pl.pallas_call(
        body, name=name, grid=(nb, t // TQ),
        in_specs=[pl.BlockSpec((None, TQ, qw), qmap), pl.BlockSpec((None, t, kw), kmap), pl.BlockSpec((None, t, kw), kmap)],
        out_specs=[pl.BlockSpec((None, TQ, qw), qmap), pl.BlockSpec((None, 2 * npair, TQ, 1), lambda b, i: (b, 0, i, 0))],
        out_shape=[jax.ShapeDtypeStruct((nb, t, qw), BF16), jax.ShapeDtypeStruct((nb, 2 * npair, t, 1), F32)],
        compiler_params=_cp("parallel", "arbitrary"),
    )(q, k2, v2)


def gqa_bwd(q, k2, v2, lse, do, *, lc, ctx_q, name):
    nb, t, qw = q.shape
    npair = qw // LANE
    per_kv = npair // GQA_KV_HEADS
    nctb = lc // TQ

    def body(q_ref, k_ref, v_ref, lse_ref, do_ref, dq_ref, dk_ref, dv_ref):
        i = pl.program_id(1)

        @pl.when(i == 0)
        def _():
            dk_ref[...] = jnp.zeros_like(dk_ref)
            dv_ref[...] = jnp.zeros_like(dv_ref)

        def run(rows):
            for pr in range(npair):
                lanes = slice(LANE * pr, LANE * (pr + 1))
                kv = slice(LANE * (pr // per_kv), LANE * (pr // per_kv + 1))
                kk, vv = k_ref[rows, kv], v_ref[rows, kv]
                qq, dd = _stack_pair(q_ref[:, lanes], HEAD_DIM, 0), _stack_pair(do_ref[:, lanes], HEAD_DIM, 0)
                p = jnp.exp2(_dot(qq, kk, _NT) - jnp.concatenate([lse_ref[2 * pr], lse_ref[2 * pr + 1]], axis=0))
                dp = _dot(dd, vv, _NT)
                delta = jnp.sum(p * dp, axis=-1, keepdims=True)
                ds = (p * (dp - delta)).astype(BF16)
                dq = _dot(ds, kk)
                dq_ref[:, lanes] = jnp.where(_lanes(0, HEAD_DIM), dq[:TQ], dq[TQ:])
                dk_ref[rows, kv] += _dot(ds, qq, _TN)
                dv_ref[rows, kv] += _dot(p.astype(BF16), dd, _TN)

        @pl.when(i < nctb)
        def _():
            if ctx_q:
                run(pl.ds(0, lc))
            else:
                dq_ref[...] = jnp.zeros_like(dq_ref)

        @pl.when(i >= nctb)
        def _():
            run(pl.ds(0, t))

    qmap = lambda b, i: (b, i, 0)
    kmap = lambda b, i: (b, 0, 0)
    kw = k2.shape[-1]
    return pl.pallas_call(
        body, name=name, grid=(nb, t // TQ),
        in_specs=[pl.BlockSpec((None, TQ, qw), qmap), pl.BlockSpec((None, t, kw), kmap), pl.BlockSpec((None, t, kw), kmap),
                  pl.BlockSpec((None, 2 * npair, TQ, 1), lambda b, i: (b, 0, i, 0)), pl.BlockSpec((None, TQ, qw), qmap)],
        out_specs=[pl.BlockSpec((None, TQ, qw), qmap), pl.BlockSpec((None, t, kw), kmap), pl.BlockSpec((None, t, kw), kmap)],
        out_shape=[jax.ShapeDtypeStruct((nb, t, qw), F32), jax.ShapeDtypeStruct(k2.shape, F32), jax.ShapeDtypeStruct(v2.shape, F32)],
        compiler_params=_cp("arbitrary", "arbitrary"),
    )(q, k2, v2, lse, do)


def _mla_lanes(pr, e):
    lane = lax.broadcasted_iota(jnp.int32, (1, 2 * LANE), 1)
    lo = LANE + MLA_ROPE * (2 * pr + e)
    return ((lane >= MLA_NOPE * e) & (lane < MLA_NOPE * (e + 1))) | ((lane >= lo) & (lane < lo + MLA_ROPE))


def mla_fwd(q, k, v, *, lc, ctx_q, name):
    nb, t, w = v.shape
    npair = w // LANE
    nctb = lc // TQ

    def body(q_ref, k_ref, v_ref, o_ref, lse_ref):
        i = pl.program_id(1)

        def run(rows):
            for pr in range(npair):
                wide, lanes = slice(2 * LANE * pr, 2 * LANE * (pr + 1)), slice(LANE * pr, LANE * (pr + 1))
                kk, vv = k_ref[rows, wide], v_ref[rows, lanes]
                outs = []
                for e in range(2):
                    p, l, lse = _pair_softmax(_dot(_only(q_ref[:, wide], _mla_lanes(pr, e)), kk, _NT))
                    outs.append(_dot(p.astype(BF16), vv) / l)
                    lse_ref[2 * pr + e] = lse
                o_ref[:, lanes] = jnp.where(_lanes(0, MLA_V), outs[0], outs[1]).astype(o_ref.dtype)

        @pl.when(i < nctb)
        def _():
            if ctx_q:
                run(pl.ds(0, lc))
            else:
                o_ref[...] = jnp.zeros_like(o_ref)
                lse_ref[...] = jnp.zeros_like(lse_ref)

        @pl.when(i >= nctb)
        def _():
            run(pl.ds(0, t))

    qmap = lambda b, i: (b, i, 0)
    kmap = lambda b, i: (b, 0, 0)
    return pl.pallas_call(
        body, name=name, grid=(nb, t // TQ),
        in_specs=[pl.BlockSpec((None, TQ, 2 * w), qmap), pl.BlockSpec((None, t, 2 * w), kmap), pl.BlockSpec((None, t, w), kmap)],
        out_specs=[pl.BlockSpec((None, TQ, w), qmap), pl.BlockSpec((None, 2 * npair, TQ, 1), lambda b, i: (b, 0, i, 0))],
        out_shape=[jax.ShapeDtypeStruct((nb, t, w), BF16), jax.ShapeDtypeStruct((nb, 2 * npair, t, 1), F32)],
        compiler_params=_cp("parallel", "arbitrary"),
    )(q, k, v)


def mla_bwd(q, k, v, lse, do, *, lc, ctx_q, name):
    nb, t, w = v.shape
    npair = w // LANE
    nctb = lc // TQ

    def body(q_ref, k_ref, v_ref, lse_ref, do_ref, dq_ref, dk_ref, dv_ref):
        i = pl.program_id(1)

        @pl.when(i == 0)
        def _():
            dk_ref[...] = jnp.zeros_like(dk_ref)
            dv_ref[...] = jnp.zeros_like(dv_ref)

        def run(rows):
            for pr in range(npair):
                wide, lanes = slice(2 * LANE * pr, 2 * LANE * (pr + 1)), slice(LANE * pr, LANE * (pr + 1))
                kk, vv = k_ref[rows, wide], v_ref[rows, lanes]
                m0, m1 = _mla_lanes(pr, 0), _mla_lanes(pr, 1)
                qq = jnp.concatenate([_only(q_ref[:, wide], m0), _only(q_ref[:, wide], m1)], axis=0)
                dd = _stack_pair(do_ref[:, lanes], MLA_V, 0)
                p = jnp.exp2(_dot(qq, kk, _NT) - jnp.concatenate([lse_ref[2 * pr], lse_ref[2 * pr + 1]], axis=0))
                dp = _dot(dd, vv, _NT)
                delta = jnp.sum(p * dp, axis=-1, keepdims=True)
                ds = (p * (dp - delta)).astype(BF16)
                dq = _dot(ds, kk)
                dq_ref[:, wide] = _only(dq[:TQ], m0) + _only(dq[TQ:], m1)
                dk_ref[rows, wide] += _dot(ds, qq, _TN)
                dv_ref[rows, lanes] += _dot(p.astype(BF16), dd, _TN)

        @pl.when(i < nctb)
        def _():
            if ctx_q:
                run(pl.ds(0, lc))
            else:
                dq_ref[...] = jnp.zeros_like(dq_ref)

        @pl.when(i >= nctb)
        def _():
            run(pl.ds(0, t))

    qmap = lambda b, i: (b, i, 0)
    kmap = lambda b, i: (b, 0, 0)
    return pl.pallas_call(
        body, name=name, grid=(nb, t // TQ),
        in_specs=[pl.BlockSpec((None, TQ, 2 * w), qmap), pl.BlockSpec((None, t, 2 * w), kmap), pl.BlockSpec((None, t, w), kmap),
                  pl.BlockSpec((None, 2 * npair, TQ, 1), lambda b, i: (b, 0, i, 0)), pl.BlockSpec((None, TQ, w), qmap)],
        out_specs=[pl.BlockSpec((None, TQ, 2 * w), qmap), pl.BlockSpec((None, t, 2 * w), kmap), pl.BlockSpec((None, t, w), kmap)],
        out_shape=[jax.ShapeDtypeStruct(q.shape, F32), jax.ShapeDtypeStruct(k.shape, F32), jax.ShapeDtypeStruct(v.shape, F32)],
        compiler_params=_cp("arbitrary", "arbitrary"),
    )(q, k, v, lse, do)


def _na_window(st, nc, rows):
    r = jnp.maximum(st - nc, 0)
    r0 = jnp.clip(r - NA_ROWS // 2, 0, rows - NA_ROWS)
    return r, r0, r - r0


def na_fwd(q, k, v, bias, *, lc, ctx_q, name):
    nb, t, w = q.shape
    npair = w // LANE
    nc, rows = lc // GRID_W, (t - lc) // GRID_W
    nwin = NA_ROWS * GRID_W

    def body(q_ref, k_ref, v_ref, bias_ref, o_ref, lse_ref):
        st = pl.program_id(1)
        ctx = pl.ds(0, lc)

        @pl.when(st < nc)
        def _():
            if not ctx_q:
                o_ref[...] = jnp.zeros_like(o_ref)
                lse_ref[...] = jnp.zeros_like(lse_ref)
                return
            for pr in range(npair):
                lanes = slice(LANE * pr, LANE * (pr + 1))
                kc, vc = k_ref[ctx, lanes], v_ref[ctx, lanes]
                outs = []
                for e in range(2):
                    p, l, lse = _pair_softmax(_dot(_only(q_ref[:, lanes], _lanes(HEAD_DIM * e, HEAD_DIM)), kc, _NT))
                    outs.append(_dot(p.astype(BF16), vc) / l)
                    lse_ref[2 * pr + e] = lse
                o_ref[:, lanes] = jnp.where(_lanes(0, HEAD_DIM), outs[0], outs[1]).astype(o_ref.dtype)

        @pl.when(st >= nc)
        def _():
            _, r0, _ = _na_window(st, nc, rows)
            win = pl.ds(pl.multiple_of(lc + r0 * GRID_W, GRID_W), nwin)
            for pr in range(npair):
                lanes = slice(LANE * pr, LANE * (pr + 1))
                kc, vc, kw, vw = k_ref[ctx, lanes], v_ref[ctx, lanes], k_ref[win, lanes], v_ref[win, lanes]
                qq = _stack_pair(q_ref[:, lanes], HEAD_DIM, 0)
                s_loc = _dot(qq, kw, _NT) + jnp.concatenate([bias_ref[2 * pr], bias_ref[2 * pr + 1]], axis=0) * LOG2E
                s_ctx = _dot(qq, kc, _NT)
                m = jnp.maximum(jnp.max(s_loc, axis=-1, keepdims=True), jnp.max(s_ctx, axis=-1, keepdims=True))
                p_loc, p_ctx = jnp.exp2(s_loc - m), jnp.exp2(s_ctx - m)
                l = jnp.sum(p_loc, axis=-1, keepdims=True) + jnp.sum(p_ctx, axis=-1, keepdims=True)
                o = (_dot(p_loc.astype(BF16), vw) + _dot(p_ctx.astype(BF16), vc)) / l
                lse = m + jnp.log2(l)
                lse_ref[2 * pr], lse_ref[2 * pr + 1] = lse[:GRID_W], lse[GRID_W:]
                o_ref[:, lanes] = jnp.where(_lanes(0, HEAD_DIM), o[:GRID_W], o[GRID_W:]).astype(o_ref.dtype)

    qmap = lambda b, st: (b, st, 0)
    kmap = lambda b, st: (b, 0, 0)
    return pl.pallas_call(
        body, name=name, grid=(nb, nc + rows),
        in_specs=[pl.BlockSpec((None, GRID_W, w), qmap), pl.BlockSpec((None, t, w), kmap), pl.BlockSpec((None, t, w), kmap),
                  pl.BlockSpec((2 * npair, None, GRID_W, nwin), lambda b, st: (0, _na_window(st, nc, rows)[2], 0, 0))],
        out_specs=[pl.BlockSpec((None, GRID_W, w), qmap), pl.BlockSpec((None, 2 * npair, GRID_W, 1), lambda b, st: (b, 0, st, 0))],
        out_shape=[jax.ShapeDtypeStruct((nb, t, w), BF16), jax.ShapeDtypeStruct((nb, 2 * npair, t, 1), F32)],
        compiler_params=_cp("parallel", "arbitrary"),
    )(q, k, v, bias)


def na_bwd(q, k, v, bias, lse, do, *, lc, ctx_q, name):
    nb, t, w = q.shape
    npair = w // LANE
    nc, rows = lc // GRID_W, (t - lc) // GRID_W
    nwin = NA_ROWS * GRID_W

    def body(q_ref, k_ref, v_ref, bias_ref, lse_ref, do_ref, dq_ref, dk_ref, dv_ref, db_ref):
        b, st = pl.program_id(0), pl.program_id(1)

        @pl.when(st == 0)
        def _():
            dk_ref[...] = jnp.zeros_like(dk_ref)
            dv_ref[...] = jnp.zeros_like(dv_ref)

        @pl.when((st == 0) & (b == 0))
        def _():
            db_ref[...] = jnp.zeros_like(db_ref)

        ctx = pl.ds(0, lc)

        @pl.when(st < nc)
        def _():
            if not ctx_q:
                dq_ref[...] = jnp.zeros_like(dq_ref)
                return
            for pr in range(npair):
                lanes = slice(LANE * pr, LANE * (pr + 1))
                kc, vc = k_ref[ctx, lanes], v_ref[ctx, lanes]
                dqs = []
                for e in range(2):
                    mine = _lanes(HEAD_DIM * e, HEAD_DIM)
                    qq, dd = _only(q_ref[:, lanes], mine), _only(do_ref[:, lanes], mine)
                    p = jnp.exp2(_dot(qq, kc, _NT) - lse_ref[2 * pr + e])
                    dp = _dot(dd, vc, _NT)
                    delta = jnp.sum(p * dp, axis=-1, keepdims=True)
                    ds = (p * (dp - delta)).astype(BF16)
                    dqs.append(_dot(ds, kc))
                    dk_ref[ctx, lanes] += _dot(ds, qq, _TN)
                    dv_ref[ctx, lanes] += _dot(p.astype(BF16), dd, _TN)
                dq_ref[:, lanes] = jnp.where(_lanes(0, HEAD_DIM), dqs[0], dqs[1])

        @pl.when(st >= nc)
        def _():
            _, r0, case = _na_window(st, nc, rows)
            win = pl.ds(pl.multiple_of(lc + r0 * GRID_W, GRID_W), nwin)
            for pr in range(npair):
                lanes = slice(LANE * pr, LANE * (pr + 1))
                kc, vc, kw, vw = k_ref[ctx, lanes], v_ref[ctx, lanes], k_ref[win, lanes], v_ref[win, lanes]
                qq, dd = _stack_pair(q_ref[:, lanes], HEAD_DIM, 0), _stack_pair(do_ref[:, lanes], HEAD_DIM, 0)
                lse = jnp.concatenate([lse_ref[2 * pr], lse_ref[2 * pr + 1]], axis=0)
                bias2 = jnp.concatenate([bias_ref[2 * pr], bias_ref[2 * pr + 1]], axis=0)
                p_loc = jnp.exp2(_dot(qq, kw, _NT) + bias2 * LOG2E - lse)
                p_ctx = jnp.exp2(_dot(qq, kc, _NT) - lse)
                dp_loc, dp_ctx = _dot(dd, vw, _NT), _dot(dd, vc, _NT)
                delta = jnp.sum(p_loc * dp_loc, axis=-1, keepdims=True) + jnp.sum(p_ctx * dp_ctx, axis=-1, keepdims=True)
                ds_loc = p_loc * (dp_loc - delta)
                db_ref[2 * pr, case] += ds_loc[:GRID_W]
                db_ref[2 * pr + 1, case] += ds_loc[GRID_W:]
                ds_loc = ds_loc.astype(BF16)
                ds_ctx = (p_ctx * (dp_ctx - delta)).astype(BF16)
                dq = _dot(ds_loc, kw) + _dot(ds_ctx, kc)
                dq_ref[:, lanes] = jnp.where(_lanes(0, HEAD_DIM), dq[:GRID_W], dq[GRID_W:])
                dk_ref[win, lanes] += _dot(ds_loc, qq, _TN)
                dk_ref[ctx, lanes] += _dot(ds_ctx, qq, _TN)
                dv_ref[win, lanes] += _dot(p_loc.astype(BF16), dd, _TN)
                dv_ref[ctx, lanes] += _dot(p_ctx.astype(BF16), dd, _TN)

    qmap = lambda b, st: (b, st, 0)
    kmap = lambda b, st: (b, 0, 0)
    nh = 2 * npair
    return pl.pallas_call(
        body, name=name, grid=(nb, nc + rows),
        in_specs=[pl.BlockSpec((None, GRID_W, w), qmap), pl.BlockSpec((None, t, w), kmap), pl.BlockSpec((None, t, w), kmap),
                  pl.BlockSpec((nh, None, GRID_W, nwin), lambda b, st: (0, _na_window(st, nc, rows)[2], 0, 0)),
                  pl.BlockSpec((None, nh, GRID_W, 1), lambda b, st: (b, 0, st, 0)), pl.BlockSpec((None, GRID_W, w), qmap)],
        out_specs=[pl.BlockSpec((None, GRID_W, w), qmap), pl.BlockSpec((None, t, w), kmap), pl.BlockSpec((None, t, w), kmap),
                   pl.BlockSpec((nh, NA_ROWS, GRID_W, nwin), lambda b, st: (0, 0, 0, 0))],
        out_shape=[jax.ShapeDtypeStruct((nb, t, w), F32), jax.ShapeDtypeStruct((nb, t, w), F32), jax.ShapeDtypeStruct((nb, t, w), F32),
                   jax.ShapeDtypeStruct((nh, NA_ROWS, GRID_W, nwin), F32)],
        compiler_params=_cp("arbitrary", "arbitrary"),
    )(q, k, v, bias, lse, do)


def _na_tables():
    cols = np.arange(GRID_W)
    c0 = np.clip(cols - NA_COLS // 2, 0, GRID_W - NA_COLS)
    col_in = (cols[None, :] >= c0[:, None]) & (cols[None, :] < c0[:, None] + NA_COLS)
    dc = np.clip(cols[None, :] - cols[:, None] + NA_COLS - 1, 0, 2 * NA_COLS - 2)
    dr = np.arange(NA_ROWS)[None, :] + (NA_ROWS - 1) - np.arange(NA_ROWS)[:, None]
    return col_in, dc, dr


def _na_onehots():
    col_in, dc, dr = _na_tables()
    e1 = np.zeros((GRID_W, GRID_W, LANE), np.float32)
    qi, ki = np.nonzero(col_in)
    e1[qi, ki, dc[qi, ki]] = 1.0
    e2 = np.zeros((2 * NA_ROWS, NA_ROWS, NA_ROWS), np.float32)
    ci, ji = np.meshgrid(np.arange(NA_ROWS), np.arange(NA_ROWS), indexing='ij')
    e2[dr[ci, ji], ci, ji] = 1.0
    return jnp.asarray(e1.reshape(GRID_W * GRID_W, LANE)), jnp.asarray(e2.reshape(2 * NA_ROWS, NA_ROWS * NA_ROWS)), col_in


def na_expand_bias(rel_bias, name):
    e1, e2, col_in = _na_onehots()
    nh = rel_bias.shape[0]
    nrow = NA_ROWS * NA_ROWS
    rel = jnp.pad(rel_bias, ((0, 0), (0, 1), (0, LANE - rel_bias.shape[2])))
    rel = rel.transpose(1, 0, 2).reshape(2 * NA_ROWS, nh * LANE)
    y = mm(e2, rel, ta=True, name=name + "_rows", precise=True)
    y = y.reshape(nrow, nh, LANE).transpose(1, 0, 2).reshape(nh * nrow, LANE)
    g = mm(y, e1, tb=True, name=name + "_cols", precise=True)
    g = g.reshape(nh, NA_ROWS, NA_ROWS, GRID_W, GRID_W).transpose(0, 1, 3, 2, 4)
    g = jnp.where(col_in[None, None, :, None, :], g, NEG_BIG)
    return g.reshape(nh, NA_ROWS, GRID_W, NA_ROWS * GRID_W)


def na_reduce_bias(dexp, name):
    e1, e2, _ = _na_onehots()
    nh = dexp.shape[0]
    x = dexp.reshape(nh, NA_ROWS, GRID_W, NA_ROWS, GRID_W).transpose(0, 1, 3, 2, 4).reshape(nh * NA_ROWS * NA_ROWS, GRID_W * GRID_W)
    y = mm(x, e1, name=name + "_cols", precise=True)
    y = y.reshape(nh, NA_ROWS * NA_ROWS, LANE).transpose(1, 0, 2).reshape(NA_ROWS * NA_ROWS, nh * LANE)
    z = mm(e2, y, name=name + "_rows", precise=True)
    return z.reshape(2 * NA_ROWS, nh, LANE).transpose(1, 0, 2)[:, :2 * NA_ROWS - 1, :2 * NA_COLS - 1]


def _rot_matrix(width, d_rot):
    f = d_rot // 4
    r = np.zeros((width, width), np.float32)
    for base in range(0, width, d_rot // 2):
        for j in range(f):
            r[base + f + j, base + j] = -1.0
            r[base + j, base + f + j] = 1.0
    return r


def _rope_tables(s_len, lc, d_rot, reps):
    half = d_rot // 2
    freqs = ROPE_THETA ** (-jnp.arange(0, half, 2, dtype=F32) / half)
    tpos = jnp.arange(s_len)
    row = (tpos // GRID_W).astype(F32)[:, None] * freqs
    col = (tpos % GRID_W).astype(F32)[:, None] * freqs
    ang = jnp.concatenate([row, row, col, col], axis=-1)
    cos = jnp.concatenate([jnp.ones((lc, d_rot), F32), jnp.cos(ang)], axis=0)
    sin = jnp.concatenate([jnp.zeros((lc, d_rot), F32), jnp.sin(ang)], axis=0)
    return jnp.tile(cos, (1, reps)), jnp.tile(sin, (1, reps))


def _post_consts():
    s_b = np.kron(np.eye(GQA_HEADS, dtype=np.float32), np.full((HEAD_DIM, HEAD_DIM), 1.0 / HEAD_DIM, np.float32))
    t_b = np.tile(np.eye(HEAD_DIM, dtype=np.float32), (1, GQA_HEADS))
    r_b = _rot_matrix(GQ_W, HEAD_DIM)
    r_m = _rot_matrix(LANE, MLA_ROPE)
    rep = np.zeros((LANE, LANE), np.float32)
    for h in range(MLA_HEADS):
        rep[np.arange(MLA_ROPE), h * MLA_ROPE + np.arange(MLA_ROPE)] = 1.0
    dup = np.zeros((GK_W, 2 * GK_W), np.float32)
    for j in range(GQA_KV_HEADS):
        for e in range(2):
            dup[HEAD_DIM * j + np.arange(HEAD_DIM), 2 * HEAD_DIM * j + HEAD_DIM * e + np.arange(HEAD_DIM)] = 1.0
    return tuple(jnp.asarray(a) for a in (s_b, r_b, t_b, r_m, rep, dup))


def _heads_to_parts(w, first):
    r = w.shape[0]
    w3 = w.reshape(r, MLA_HEADS, -1)
    return jnp.concatenate([w3[:, :, :first].reshape(r, -1), w3[:, :, first:].reshape(r, -1)], axis=1)


def _parts_to_heads(w, first):
    r = w.shape[0]
    nf = MLA_HEADS * first
    return jnp.concatenate([w[:, :nf].reshape(r, MLA_HEADS, first), w[:, nf:].reshape(r, MLA_HEADS, -1)], axis=2).reshape(r, -1)


def _place():
    return lax.axis_index("x"), lax.axis_index("y"), lax.axis_index("c")


def all_gather(v, *, name, with_c):
    flips = [(dx, dy, dc) for dx in (0, 1) for dy in (0, 1) for dc in ((0, 1) if with_c else (0,))][1:]
    n = len(flips) + 1

    def body(v_ref, out_ref, send_sems, recv_sems, local_sem):
        mx, my, mc = _place()

        def slot(px, py, pc):
            return 4 * px + 2 * py + pc if with_c else 2 * px + py

        mine = pltpu.make_async_copy(v_ref, out_ref.at[slot(mx, my, mc)], local_sem)
        mine.start()
        sends = []
        for j, (dx, dy, dc) in enumerate(flips):
            peer = (mx ^ dx, my ^ dy, mc ^ dc)
            cp = pltpu.make_async_remote_copy(src_ref=v_ref, dst_ref=out_ref.at[slot(mx, my, mc)], send_sem=send_sems.at[j],
                                              recv_sem=recv_sems.at[j], device_id=peer, device_id_type=MESH)
            cp.start()
            sends.append(cp)
        for j, (dx, dy, dc) in enumerate(flips):
            peer = (mx ^ dx, my ^ dy, mc ^ dc)
            pltpu.make_async_remote_copy(src_ref=v_ref, dst_ref=out_ref.at[slot(*peer)], send_sem=send_sems.at[j],
                                         recv_sem=recv_sems.at[j], device_id=peer, device_id_type=MESH).wait_recv()
        for cp in sends:
            cp.wait_send()
        mine.wait()

    return pl.pallas_call(
        body, name=name, in_specs=[ANY], out_specs=ANY, out_shape=jax.ShapeDtypeStruct((n,) + v.shape, v.dtype),
        scratch_shapes=[pltpu.SemaphoreType.DMA((n - 1,)), pltpu.SemaphoreType.DMA((n - 1,)), pltpu.SemaphoreType.DMA(())],
    )(v)


def gather_shards(v, *, name):
    _, h, w = v.shape
    flips = [(1, 0), (0, 1), (1, 1)]

    def body(v_ref, out_ref, send_sems, recv_sems):
        mx, my, mc = _place()
        me = 2 * mx + my
        sib = (mx, my, 1 - mc)

        def copy(k, src, dst, to):
            return pltpu.make_async_remote_copy(src_ref=src, dst_ref=dst, send_sem=send_sems.at[k], recv_sem=recv_sems.at[k],
                                                device_id=to, device_id_type=MESH)

        first = [copy(j, v_ref.at[mc], out_ref.at[me, mc], (mx ^ dx, my ^ dy, mc)) for j, (dx, dy) in enumerate(flips)]
        for cp in first:
            cp.start()
        passed = []
        for j, (dx, dy) in enumerate(flips):
            theirs = out_ref.at[2 * (mx ^ dx) + (my ^ dy), mc]
            copy(j, v_ref.at[mc], theirs, (mx ^ dx, my ^ dy, mc)).wait_recv()
            fw = copy(3 + j, theirs, theirs, sib)
            fw.start()
            passed.append(fw)
        for j, (dx, dy) in enumerate(flips):
            other = out_ref.at[2 * (mx ^ dx) + (my ^ dy), 1 - mc]
            copy(3 + j, other, other, sib).wait_recv()
        for cp in first + passed:
            cp.wait_send()

    out = pl.pallas_call(
        body, name=name, in_specs=[ANY], out_specs=ANY, out_shape=jax.ShapeDtypeStruct((4, 2, h, w), v.dtype),
        scratch_shapes=[pltpu.SemaphoreType.DMA((6,)), pltpu.SemaphoreType.DMA((6,))],
    )(v)
    mx, my, _ = _place()
    return lax.dynamic_update_slice(out, v[None], (2 * mx + my, 0, 0, 0))


def pair_exchange_halves(g, *, name):
    n, _, h, w = g.shape

    def body(g_ref, out_ref, send_sems, recv_sems):
        mx, my, mc = _place()
        sib = (mx, my, 1 - mc)
        cps = [pltpu.make_async_remote_copy(src_ref=g_ref.at[s, 1 - mc], dst_ref=out_ref.at[s], send_sem=send_sems.at[s],
                                            recv_sem=recv_sems.at[s], device_id=sib, device_id_type=MESH) for s in range(n)]
        for cp in cps:
            cp.start()
        for cp in cps:
            cp.wait_recv()
        for cp in cps:
            cp.wait_send()

    return pl.pallas_call(
        body, name=name, in_specs=[ANY], out_specs=ANY, out_shape=jax.ShapeDtypeStruct((n, h, w), g.dtype),
        scratch_shapes=[pltpu.SemaphoreType.DMA((n,)), pltpu.SemaphoreType.DMA((n,))],
    )(g)


def all_to_all_xy(v, *, name):
    def body(v_ref, out_ref, send_sems, recv_sems):
        mx, my, mc = _place()
        me = 2 * mx + my
        flips = [(1, 0), (0, 1), (1, 1)]
        sends = []
        for j, (dx, dy) in enumerate(flips):
            px, py = mx ^ dx, my ^ dy
            cp = pltpu.make_async_remote_copy(src_ref=v_ref.at[2 * px + py], dst_ref=out_ref.at[me], send_sem=send_sems.at[j],
                                              recv_sem=recv_sems.at[j], device_id=(px, py, mc), device_id_type=MESH)
            cp.start()
            sends.append(cp)
        for j, (dx, dy) in enumerate(flips):
            px, py = mx ^ dx, my ^ dy
            pltpu.make_async_remote_copy(src_ref=v_ref.at[me], dst_ref=out_ref.at[2 * px + py], send_sem=send_sems.at[j],
                                         recv_sem=recv_sems.at[j], device_id=(px, py, mc), device_id_type=MESH).wait_recv()
        for cp in sends:
            cp.wait_send()

    out = pl.pallas_call(
        body, name=name, in_specs=[ANY], out_specs=ANY, out_shape=jax.ShapeDtypeStruct(v.shape, v.dtype),
        scratch_shapes=[pltpu.SemaphoreType.DMA((3,)), pltpu.SemaphoreType.DMA((3,))],
    )(v)
    mx, my, _ = _place()
    me = 2 * mx + my
    return lax.dynamic_update_slice(out, lax.dynamic_slice_in_dim(v, me, 1, axis=0), (me, 0, 0))


def pair_all_gather(v, *, name):
    def body(v_ref, out_ref, send_sem, recv_sem):
        mx, my, mc = _place()
        cp = pltpu.make_async_remote_copy(src_ref=v_ref, dst_ref=out_ref.at[mc], send_sem=send_sem, recv_sem=recv_sem,
                                          device_id=(mx, my, 1 - mc), device_id_type=MESH)
        cp.start()
        pltpu.make_async_remote_copy(src_ref=v_ref, dst_ref=out_ref.at[1 - mc], send_sem=send_sem, recv_sem=recv_sem,
                                     device_id=(mx, my, 1 - mc), device_id_type=MESH).wait_recv()
        cp.wait_send()

    out = pl.pallas_call(
        body, name=name, in_specs=[ANY], out_specs=ANY, out_shape=jax.ShapeDtypeStruct((2,) + v.shape, v.dtype),
        scratch_shapes=[pltpu.SemaphoreType.DMA(()), pltpu.SemaphoreType.DMA(())],
    )(v)
    return lax.dynamic_update_slice(out, v[None], (_place()[2], 0, 0))


def gather_ffn(wl, *, name):
    nl, nblk, cs, d = wl.shape
    assert nl == 2
    flips = [(1, 0), (0, 1), (1, 1)]

    def body(v_ref, out_ref, send_sems, recv_sems):
        mx, my, mc = _place()
        me = 2 * mx + my
        sib = (mx, my, 1 - mc)

        def copy(k, src, dst, to):
            return pltpu.make_async_remote_copy(src_ref=src, dst_ref=dst, send_sem=send_sems.at[k], recv_sem=recv_sems.at[k],
                                                device_id=to, device_id_type=MESH)

        first = [copy(j, v_ref.at[mc], out_ref.at[mc, me], (mx ^ dx, my ^ dy, mc)) for j, (dx, dy) in enumerate(flips)]
        for cp in first:
            cp.start()
        passed = []
        for j, (dx, dy) in enumerate(flips):
            theirs = out_ref.at[mc, 2 * (mx ^ dx) + (my ^ dy)]
            copy(j, v_ref.at[mc], theirs, (mx ^ dx, my ^ dy, mc)).wait_recv()
            fw = copy(3 + j, theirs, theirs, sib)
            fw.start()
            passed.append(fw)
        for j, (dx, dy) in enumerate(flips):
            other = out_ref.at[1 - mc, 2 * (mx ^ dx) + (my ^ dy)]
            copy(3 + j, other, other, sib).wait_recv()
        for cp in first + passed:
            cp.wait_send()

    out = pl.pallas_call(
        body, name=name, in_specs=[ANY], out_specs=ANY, out_shape=jax.ShapeDtypeStruct((nl, 4, nblk, cs, d), wl.dtype),
        scratch_shapes=[pltpu.SemaphoreType.DMA((6,)), pltpu.SemaphoreType.DMA((6,))],
    )(wl)
    mx, my, _ = _place()
    return lax.dynamic_update_slice(out, wl[:, None], (0, 2 * mx + my, 0, 0, 0))


def reduce_ffn(g0, g1, *, name):
    nt = len(g0)
    nsh, cs, d = g0[0].shape
    flips = [(1, 0), (0, 1), (1, 1)]
    mx, my, mc = _place()
    me = 2 * mx + my
    c_idx = jnp.reshape(mc, (1,)).astype(jnp.int32)

    def pair_body(*refs):
        ins0, ins1, outs = refs[:nt], refs[nt:2 * nt], refs[2 * nt:3 * nt]
        send_sems, recv_sems = refs[3 * nt:]
        kx, ky, kc = _place()
        sib = (kx, ky, 1 - kc)
        for c in range(2):
            @pl.when(kc == c)
            def _(c=c):
                mine_out = (ins1, ins0)[c]
                cps = [pltpu.make_async_remote_copy(src_ref=mine_out[t], dst_ref=outs[t], send_sem=send_sems.at[t],
                                                    recv_sem=recv_sems.at[t], device_id=sib, device_id_type=MESH) for t in range(nt)]
                for cp in cps:
                    cp.start()
                for cp in cps:
                    cp.wait_recv()
                for cp in cps:
                    cp.wait_send()

    from_pair = pl.pallas_call(
        pair_body, name=name + "_pair", in_specs=[ANY] * (2 * nt), out_specs=[ANY] * nt,
        out_shape=[jax.ShapeDtypeStruct((nsh, cs, d), F32)] * nt,
        scratch_shapes=[pltpu.SemaphoreType.DMA((nt,)), pltpu.SemaphoreType.DMA((nt,))],
    )(*g0, *g1)

    tr = _row_tile(cs, 64)

    def add_body(c_ref, *refs):
        for t in range(nt):
            mine = jnp.where(c_ref[0] == 0, refs[t][...], refs[nt + t][...])
            refs[3 * nt + t][...] = (mine + refs[2 * nt + t][...]).astype(BF16)

    spec = pl.BlockSpec((None, tr, d), lambda s, i, c_ref: (s, i, 0))
    chip_sum = pl.pallas_call(
        add_body, name=name + "_pair_add",
        grid_spec=pltpu.PrefetchScalarGridSpec(num_scalar_prefetch=1, grid=(nsh, cs // tr), in_specs=[spec] * (3 * nt),
                                               out_specs=[spec] * nt),
        out_shape=[jax.ShapeDtypeStruct((nsh, cs, d), BF16)] * nt, compiler_params=_cp("parallel", "parallel"),
    )(c_idx, *g0, *g1, *from_pair)

    def xy_body(*refs):
        ins, outs = refs[:nt], refs[nt:2 * nt]
        send_sems, recv_sems = refs[2 * nt:]
        kx, ky, kc = _place()
        k_me = 2 * kx + ky
        sends = []
        for j, (dx, dy) in enumerate(flips):
            px, py = kx ^ dx, ky ^ dy
            for t in range(nt):
                cp = pltpu.make_async_remote_copy(src_ref=ins[t].at[2 * px + py], dst_ref=outs[t].at[k_me],
                                                  send_sem=send_sems.at[j * nt + t], recv_sem=recv_sems.at[j * nt + t],
                                                  device_id=(px, py, kc), device_id_type=MESH)
                cp.start()
                sends.append(cp)
        for j, (dx, dy) in enumerate(flips):
            px, py = kx ^ dx, ky ^ dy
            for t in range(nt):
                pltpu.make_async_remote_copy(src_ref=ins[t].at[k_me], dst_ref=outs[t].at[2 * px + py],
                                             send_sem=send_sems.at[j * nt + t], recv_sem=recv_sems.at[j * nt + t],
                                             device_id=(px, py, kc), device_id_type=MESH).wait_recv()
        for cp in sends:
            cp.wait_send()

    from_xy = pl.pallas_call(
        xy_body, name=name + "_xy", in_specs=[ANY] * nt, out_specs=[ANY] * nt,
        out_shape=[jax.ShapeDtypeStruct((nsh, cs, d), BF16)] * nt,
        scratch_shapes=[pltpu.SemaphoreType.DMA((3 * nt,)), pltpu.SemaphoreType.DMA((3 * nt,))],
    )(*chip_sum)
    from_xy = [lax.dynamic_update_slice(o, lax.dynamic_slice_in_dim(v, me, 1, axis=0), (me, 0, 0)) for o, v in zip(from_xy, chip_sum)]

    def sum_body(*refs):
        for t in range(nt):
            acc = refs[t][0].astype(F32)
            for s in range(1, nsh):
                acc = acc + refs[t][s].astype(F32)
            refs[nt + t][...] = acc

    reduced = pl.pallas_call(
        sum_body, name=name + "_xy_add", grid=(cs // tr,), in_specs=[pl.BlockSpec((nsh, tr, d), lambda i: (0, i, 0))] * nt,
        out_specs=[pl.BlockSpec((tr, d), lambda i: (i, 0))] * nt, out_shape=[jax.ShapeDtypeStruct((cs, d), F32)] * nt,
        compiler_params=_cp("parallel"),
    )(*from_xy)

    def share_body(*refs):
        ins, outs = refs[:nt], refs[nt:2 * nt]
        send_sems, recv_sems = refs[2 * nt:]
        kx, ky, kc = _place()
        sib = (kx, ky, 1 - kc)
        cps = [pltpu.make_async_remote_copy(src_ref=ins[t], dst_ref=outs[t].at[kc], send_sem=send_sems.at[t],
                                            recv_sem=recv_sems.at[t], device_id=sib, device_id_type=MESH) for t in range(nt)]
        for cp in cps:
            cp.start()
        for t in range(nt):
            pltpu.make_async_remote_copy(src_ref=ins[t], dst_ref=outs[t].at[1 - kc], send_sem=send_sems.at[t],
                                         recv_sem=recv_sems.at[t], device_id=sib, device_id_type=MESH).wait_recv()
        for cp in cps:
            cp.wait_send()

    both = pl.pallas_call(
        share_body, name=name + "_share", in_specs=[ANY] * nt, out_specs=[ANY] * nt,
        out_shape=[jax.ShapeDtypeStruct((2, cs, d), F32)] * nt,
        scratch_shapes=[pltpu.SemaphoreType.DMA((nt,)), pltpu.SemaphoreType.DMA((nt,))],
    )(*reduced)
    return [lax.dynamic_update_slice(o, v[None], (mc, 0, 0)) for o, v in zip(both, reduced)]


def add_kept_half(g, r, c_idx, *, name, out_dtype):
    n, _, h, w = g.shape
    th = _row_tile(h)

    def body(c_ref, g_ref, r_ref, o_ref):
        o_ref[...] = (g_ref[...] + r_ref[...]).astype(o_ref.dtype)

    return pl.pallas_call(
        body, name=name,
        grid_spec=pltpu.PrefetchScalarGridSpec(
            num_scalar_prefetch=1, grid=(n, h // th),
            in_specs=[pl.BlockSpec((None, None, th, w), lambda s, i, c_ref: (s, c_ref[0], i, 0)),
                      pl.BlockSpec((None, th, w), lambda s, i, c_ref: (s, i, 0))],
            out_specs=pl.BlockSpec((None, th, w), lambda s, i, c_ref: (s, i, 0))),
        out_shape=jax.ShapeDtypeStruct((n, h, w), out_dtype), compiler_params=_cp("parallel", "parallel"),
    )(c_idx, g, r)


def sum_slots(v, *, name):
    n, rows, w = v.shape
    tr = _row_tile(rows, 256)

    def body(v_ref, o_ref):
        acc = v_ref[0].astype(F32)
        for s in range(1, n):
            acc = acc + v_ref[s].astype(F32)
        o_ref[...] = acc

    return pl.pallas_call(body, name=name, grid=(rows // tr,), in_specs=[pl.BlockSpec((n, tr, w), lambda i: (0, i, 0))],
                          out_specs=pl.BlockSpec((tr, w), lambda i: (i, 0)), out_shape=jax.ShapeDtypeStruct((rows, w), F32),
                          compiler_params=_cp("parallel"))(v)


def ada_fwd(c_rows, w_ada, b_shard, *, name):
    nl, d, ncol = w_ada.shape
    rows = c_rows.shape[0]
    tn = _tile(ncol, (768, 512, 256, 128))

    def body(c_ref, w_ref, b_ref, o_ref):
        o_ref[...] = jnp.dot(jax.nn.silu(c_ref[...]), w_ref[...], precision=HI, preferred_element_type=F32) + b_ref[...]

    return pl.pallas_call(
        body, name=name, grid=(nl, ncol // tn),
        in_specs=[pl.BlockSpec((rows, d), lambda l, j: (0, 0)), pl.BlockSpec((None, d, tn), lambda l, j: (l, 0, j)),
                  pl.BlockSpec((None, 1, tn), lambda l, j: (l, 0, j))],
        out_specs=pl.BlockSpec((None, rows, tn), lambda l, j: (l, 0, j)),
        out_shape=jax.ShapeDtypeStruct((nl, rows, ncol), F32), compiler_params=_cp("parallel", "parallel"),
    )(c_rows, w_ada, b_shard)


def ada_bwd(c_rows, w_ada, dm_shard, dm_full, n_ex, *, name):
    nl, d, ncol = w_ada.shape
    rows = c_rows.shape[0]
    tn = _tile(ncol, (768, 512, 256, 128))
    nj = ncol // tn

    def body(c_ref, w_ref, dm_ref, dmf_ref, gw_ref, gb_ref, dc_ref, dact_ref):
        l, j = pl.program_id(0), pl.program_id(1)
        act, act_vjp = jax.vjp(jax.nn.silu, c_ref[...])
        gw_ref[...] = lax.dot_general(act, dm_ref[...], _TN, precision=HI, preferred_element_type=F32)
        gb_ref[...] = jnp.sum(dmf_ref[...], axis=0, keepdims=True)
        part = lax.dot_general(dm_ref[...], w_ref[...], _NT, precision=HI, preferred_element_type=F32)

        @pl.when((l == 0) & (j == 0))
        def _():
            dact_ref[...] = part

        @pl.when((l > 0) | (j > 0))
        def _():
            dact_ref[...] += part

        @pl.when((l == nl - 1) & (j == nj - 1))
        def _():
            dc, = act_vjp(dact_ref[...])
            dc_ref[...] = jnp.sum(dc[n_ex:, :], axis=0, keepdims=True)

    return pl.pallas_call(
        body, name=name, grid=(nl, nj),
        in_specs=[pl.BlockSpec((rows, d), lambda l, j: (0, 0)), pl.BlockSpec((None, d, tn), lambda l, j: (l, 0, j)),
                  pl.BlockSpec((None, rows, tn), lambda l, j: (l, 0, j)),
                  pl.BlockSpec((None, rows, dm_full.shape[-1]), lambda l, j: (l, 0, 0))],
        out_specs=[pl.BlockSpec((None, d, tn), lambda l, j: (l, 0, j)),
                   pl.BlockSpec((None, 1, dm_full.shape[-1]), lambda l, j: (l, 0, 0)),
                   pl.BlockSpec((1, d), lambda l, j: (0, 0))],
        out_shape=[jax.ShapeDtypeStruct((nl, d, ncol), F32), jax.ShapeDtypeStruct((nl, 1, dm_full.shape[-1]), F32),
                   jax.ShapeDtypeStruct((1, d), F32)],
        scratch_shapes=[pltpu.VMEM((rows, d), F32)], compiler_params=_cp("arbitrary", "arbitrary"),
    )(c_rows, w_ada, dm_shard, dm_full)


def adamw(w, g, m, v, *, name):
    shape = w.shape
    cols = shape[-1]
    rows = int(np.prod(shape[:-1])) if len(shape) > 1 else 1
    tr = _row_tile(rows, 256)

    def body(w_ref, g_ref, m_ref, v_ref, d_ref, nm_ref, nv_ref):
        gg = g_ref[...]
        nm = ADAM_B1 * m_ref[...] + (1.0 - ADAM_B1) * gg
        nv = ADAM_B2 * v_ref[...] + (1.0 - ADAM_B2) * jnp.square(gg)
        m_hat = nm / (1.0 - ADAM_B1 ** ADAM_STEP)
        v_hat = nv / (1.0 - ADAM_B2 ** ADAM_STEP)
        d_ref[...] = -ADAM_LR * (m_hat / (jnp.sqrt(v_hat) + ADAM_EPS) + ADAM_WD * w_ref[...])
        nm_ref[...] = nm
        nv_ref[...] = nv

    spec = pl.BlockSpec((tr, cols), lambda i: (i, 0))
    out = pl.pallas_call(body, name=name, grid=(rows // tr,), in_specs=[spec] * 4, out_specs=[spec] * 3,
                         out_shape=[jax.ShapeDtypeStruct((rows, cols), F32)] * 3, compiler_params=_cp("parallel"),
                         )(*[a.reshape(rows, cols) for a in (w, g, m, v)])
    return tuple(o.reshape(shape) for o in out)


def local_step(h0, target, mods, lw, wf, small, *, lc):
    nb, t, d = h0.shape
    nt, nct = t // TM, lc // TM
    s_len = t - lc
    nl = len(lw)
    nsh = wf.shape[1]
    consts = _post_consts()
    cos_b, sin_b = _rope_tables(s_len, lc, HEAD_DIM, GQA_HEADS)
    cos_m, sin_m = _rope_tables(s_len, lc, MLA_ROPE, MLA_HEADS)
    rc = functools.partial(rowcall, nb=nb, nt=nt, nct=nct)
    flat = lambda a: a.reshape(nb * t, a.shape[-1])
    unflat = lambda a: a.reshape(nb, t, a.shape[-1])
    vec = lambda a: a.reshape(1, -1)

    def norm_first(h, g, shift, scale, tag):
        n, = rc(tag + "_norm", lambda _, *a: (f_normmod(*a),), [(h, 'tok'), (vec(g), 'full'), (shift, 'mod'), (scale, 'mod')],
                [('tok', d, BF16)])
        return n

    def res_norm(h, y, gate, coef, g, shift, scale, tag):
        def fn(_, hh, yy, gt, gn, sh, sc):
            h2 = hh + coef * gt * yy
            return h2, f_normmod(h2, gn, sh, sc)

        return rc(tag + "_res_norm", fn, [(h, 'tok'), (y, 'tok'), (gate, 'mod'), (vec(g), 'full'), (shift, 'mod'), (scale, 'mod')],
                  [('tok', d, F32), ('tok', d, BF16)])

    def res_last(h, y, gate, coef, tag):
        h2, = rc(tag + "_res", lambda _, hh, yy, gt: (hh + coef * gt * yy,), [(h, 'tok'), (y, 'tok'), (gate, 'mod')], [('tok', d, F32)])
        return h2

    def res_bwd_last(dh2, y, gate, coef, tag):
        return rc(tag + "_res_bwd", lambda _, dd, yy, gt: (coef * gt * dd, jnp.sum(coef * yy * dd, axis=0, keepdims=True)),
                  [(dh2, 'tok'), (y, 'tok'), (gate, 'mod')], [('tok', d, BF16), ('mod', d)])

    def norm_bwd_first(h, g, shift, scale, dn, dres, tag):
        def fn(_, hh, gn, sh, sc, dnn, dr):
            dh, dg, dsh, dsc = jax.vjp(f_normmod, hh, gn, sh, sc)[1](dnn)
            return dh + dr, dg, dsh, dsc

        return rc(tag + "_norm_bwd", fn, [(h, 'tok'), (vec(g), 'full'), (shift, 'mod'), (scale, 'mod'), (dn, 'tok'), (dres, 'tok')],
                  [('tok', d, F32), ('full', (1, d)), ('mod', d), ('mod', d)])

    def norm_bwd_res_bwd(h, g, shift, scale, dn, dres, y_prev, gate_prev, coef_prev, tag):
        def fn(_, hh, gn, sh, sc, dnn, dr, yy, gt):
            dh, dg, dsh, dsc = jax.vjp(f_normmod, hh, gn, sh, sc)[1](dnn)
            dh = dh + dr
            return dh, dg, dsh, dsc, coef_prev * gt * dh, jnp.sum(coef_prev * yy * dh, axis=0, keepdims=True)

        return rc(tag + "_norm_bwd", fn,
                  [(h, 'tok'), (vec(g), 'full'), (shift, 'mod'), (scale, 'mod'), (dn, 'tok'), (dres, 'tok'), (y_prev, 'tok'), (gate_prev, 'mod')],
                  [('tok', d, F32), ('full', (1, d)), ('mod', d), ('mod', d), ('tok', d, BF16), ('mod', d)])

    def ffn_fwd(n, l, base, tag):
        gg, uu, act = ffn_up(flat(n), wf, l, base, name=tag + "_up")
        return unflat(ffn_down(act, wf, l, base, name=tag + "_down")), (n, gg, uu, act)

    def ffn_bwd(dy, saved, l, base, tag):
        n, gg, uu, act = saved
        dw_d = ffn_dw(act, flat(dy), nsh, name=tag + "_down_dw")
        dgg, duu = ffn_down_bwd(flat(dy), gg, uu, wf, l, base, name=tag + "_down_dx")
        dw_g = ffn_dw(dgg, flat(n), nsh, name=tag + "_gate_dw")
        dw_u = ffn_dw(duu, flat(n), nsh, name=tag + "_up_dw")
        return unflat(ffn_up_bwd(dgg, duu, wf, l, base, name=tag + "_up_dx")), [dw_g, dw_u, dw_d]

    def post_ins(p, sm, w):
        return [(p, ('tokc', MAIN_PAD, 0)), (cos_b, 'pos'), (sin_b, 'pos'), (cos_m, 'pos'), (sin_m, 'pos'),
                (vec(sm['gqa_q_norm']), 'full'), (vec(sm['gqa_k_norm']), 'full'), (vec(sm['mla_q_norm']), 'full'),
                (vec(sm['mla_kv_norm']), 'full'), (w['w_uq'], 'full'), (w['w_ukv'], 'full')] + [(c, 'full') for c in consts]

    def mix_fwd(n, sm, w, ctx_q, tag):
        p = unflat(mm_resident(flat(n), w['w_in'], out_dtype=BF16, name=tag + "_in"))
        parts = rc(tag + "_post", lambda _, pp, *a: f_post(pp.astype(F32), *a), post_ins(p, sm, w),
                   [('tok', wd, BF16) for wd in POST_WIDTHS])
        aq, ak, av, bq, bk, bv, mq, mk, mv = parts
        bias = na_expand_bias(sm['na_rel_bias'], tag + "_bias")
        o_a, lse_a = na_fwd(aq, ak, av, bias, lc=lc, ctx_q=ctx_q, name=tag + "_na")
        o_b, lse_b = gqa_fwd(bq, bk, bv, lc=lc, ctx_q=ctx_q, name=tag + "_gqa")
        o_m, lse_m = mla_fwd(mq, mk, mv, lc=lc, ctx_q=ctx_q, name=tag + "_mla")
        fo = [o_a, o_b, o_m]
        ys = [unflat(mm_resident(flat(o), w[k], out_dtype=BF16, name=tag + "_br" + k[-1])) for o, k in zip(fo, ('w_a', 'w_b', 'w_c'))]
        gcols = [(p, ('tokc', d, MAIN_PAD // d + j)) for j in range(3)]
        y, = rc(tag + "_merge", lambda _, *a: (f_merge(*[v.astype(F32) for v in a]),), gcols + [(v, 'tok') for v in ys],
                [('tok', d, BF16)])
        z = unflat(mm_resident(flat(y), w['w_o'], name=tag + "_out"))
        saved = (n, p, (aq, ak, av, lse_a, bias), (bq, bk, bv, lse_b), (mq, mk, mv, lse_m), fo, ys, y)
        return z, saved

    def mix_bwd(dz, saved, sm, w, ctx_q, tag):
        n, p, (aq, ak, av, lse_a, bias), (bq, bk, bv, lse_b), (mq, mk, mv, lse_m), fo, ys, y = saved
        dw_o = mm(flat(y), flat(dz), ta=True, name=tag + "_out_dw")
        dy = unflat(mm_resident(flat(dz), w['w_o'], tb=True, name=tag + "_out_dx"))
        gcols = [(p, ('tokc', d, MAIN_PAD // d + j)) for j in range(3)]

        def merge_bwd(_, ga, gb, gm, ya, yb, ym, dyy):
            dga, dgb, dgm, dya, dyb, dym = jax.vjp(f_merge, *[v.astype(F32) for v in (ga, gb, gm, ya, yb, ym)])[1](dyy)
            return dya, dyb, dym, jnp.concatenate([dga, dgb, dgm], axis=-1)

        dya, dyb, dym, dgl = rc(tag + "_merge_bwd", merge_bwd, gcols + [(v, 'tok') for v in ys] + [(dy, 'tok')],
                                [('tok', d, BF16)] * 3 + [('tok', 3 * d, BF16)])
        dws, dos = {}, []
        for o, dyk, k in zip(fo, (dya, dyb, dym), ('w_a', 'w_b', 'w_c')):
            dws[k] = mm(flat(o), flat(dyk), ta=True, name=tag + "_br" + k[-1] + "_dw")
            dos.append(unflat(mm_resident(flat(dyk), w[k], tb=True, out_dtype=BF16, name=tag + "_br" + k[-1] + "_dx")))
        do_a, do_b, do_m = dos
        daq, dak, dav, dbias = na_bwd(aq, ak, av, bias, lse_a, do_a, lc=lc, ctx_q=ctx_q, name=tag + "_na_bwd")
        dbq, dbk, dbv = gqa_bwd(bq, bk, bv, lse_b, do_b, lc=lc, ctx_q=ctx_q, name=tag + "_gqa_bwd")
        dmq, dmk, dmv = mla_bwd(mq, mk, mv, lse_m, do_m, lc=lc, ctx_q=ctx_q, name=tag + "_mla_bwd")
        d_rel = na_reduce_bias(dbias, tag + "_relb")
        cots = [daq, dak, dav, dbq, dbk, dbv, dmq, dmk, dmv]
        ins = post_ins(p, sm, w)
        n_in = len(ins)

        def post_bwd(_, *a):
            prim, cot, dgl_v = a[:11], list(a[n_in:n_in + N_POST]), a[-1]
            for j in POST_QK:
                cot[j] = cot[j] * LN2
            outs = jax.vjp(lambda pp, qn, kn, mqn, mkvn, wuq, wukv: f_post(pp, *prim[1:5], qn, kn, mqn, mkvn, wuq, wukv, *a[11:n_in]),
                           prim[0].astype(F32), *prim[5:11])[1](tuple(cot))
            return (jnp.concatenate([outs[0].astype(BF16), dgl_v], axis=-1),) + tuple(outs[1:])

        res = rc(tag + "_post_bwd", post_bwd, ins + [(cv, 'tok') for cv in cots] + [(dgl, 'tok')],
                 [('tok', MAIN_PAD + 3 * d, BF16), ('full', (1, HEAD_DIM)), ('full', (1, HEAD_DIM)), ('full', (1, MLA_Q_RANK)),
                  ('full', (1, MLA_KV_RANK)), ('full', w['w_uq'].shape), ('full', w['w_ukv'].shape)])
        dp, dqn, dkn, dmqn, dmkvn, dw_uq, dw_ukv = res
        dw_in = ffn_dw(flat(dp), flat(n), 4, name=tag + "_in_dw").reshape(-1, d)
        dn = unflat(mm_resident(flat(dp), w['w_in'], tb=True, name=tag + "_in_dx"))
        dsm = {'na_rel_bias': d_rel, 'gqa_q_norm': dqn.reshape(-1), 'gqa_k_norm': dkn.reshape(-1),
               'mla_q_norm': dmqn.reshape(-1), 'mla_kv_norm': dmkvn.reshape(-1)}
        dwl = {'w_in': dw_in, 'w_uq': dw_uq, 'w_ukv': dw_ukv, 'w_o': dw_o, **dws}
        return dn, dsm, dwl

    subs = [(l, kind, gain, coef) for l in range(nl)
            for kind, gain, coef in (('ffn1', 'ffn1_norm', 0.5), ('mix', 'mix_norm', 1.0), ('ffn2', 'ffn2_norm', 0.5))]
    ns = len(subs)
    sms = [{k: small[k][l] for k in SMALL_LAYER} for l in range(nl)]

    def params(k):
        l, _, gain, _ = subs[k]
        j = 3 * (k % 3)
        return small[gain][l], mods[l][j], mods[l][j + 1], mods[l][j + 2]

    def tag_of(k):
        return f"l{subs[k][0]}_{subs[k][1]}"

    h = h0
    g0, sh0, sc0, _ = params(0)
    n = norm_first(h, g0, sh0, sc0, tag_of(0))
    h_in, core_out, saved = [None] * ns, [None] * ns, [None] * ns
    for k, (l, kind, _, coef) in enumerate(subs):
        h_in[k] = h
        if kind == 'mix':
            core_out[k], saved[k] = mix_fwd(n, sms[l], lw[l], l + 1 < nl, tag_of(k))
        else:
            core_out[k], saved[k] = ffn_fwd(n, l, 0 if kind == 'ffn1' else 3, tag_of(k))
        gate = params(k)[3]
        if k + 1 < ns:
            gn, shn, scn, _ = params(k + 1)
            h, n = res_norm(h, core_out[k], gate, coef, gn, shn, scn, tag_of(k))
        else:
            h = res_last(h, core_out[k], gate, coef, tag_of(k))

    def final(is_ctx, hh, gg, tgt):
        def loss_fn(hv, gv):
            return 0.5 * jnp.sum(jnp.mean(jnp.square(_rms(hv, gv) - tgt), axis=-1))

        keep = jnp.where(is_ctx, 0.0, 1.0)
        loss, (dh, dg) = jax.value_and_grad(loss_fn, argnums=(0, 1))(hh, gg)
        return dh * keep, jnp.full((1, LANE), loss * keep, F32), dg * keep

    dh, loss, dg_final = rc("final_loss", final, [(h, 'tok'), (vec(small['final_norm']), 'full'), (target, 'lat')],
                            [('tok', d, F32), ('full', (1, LANE)), ('full', (1, d))])

    dsmall = {k: [None] * nl for k in SMALL_LAYER}
    dmods, dlw, dwf = [[None] * N_MOD for _ in range(nl)], [None] * nl, [[None] * 6 for _ in range(nl)]
    l_last, _, _, coef_last = subs[-1]
    dcore, dmods[l_last][8] = res_bwd_last(dh, core_out[-1], params(ns - 1)[3], coef_last, tag_of(ns - 1))
    for k in reversed(range(ns)):
        l, kind, gain, _ = subs[k]
        j = 3 * (k % 3)
        if kind == 'mix':
            dn, dsm, dlw[l] = mix_bwd(dcore, saved[k], sms[l], lw[l], l + 1 < nl, tag_of(k))
            for name, val in dsm.items():
                dsmall[name][l] = val
        else:
            base = 0 if kind == 'ffn1' else 3
            dn, dwf[l][base:base + 3] = ffn_bwd(dcore, saved[k], l, base, tag_of(k))
        g, shift, scale, _ = params(k)
        if k > 0:
            lp, _, _, coef_prev = subs[k - 1]
            dh, dg, dmods[l][j], dmods[l][j + 1], dcore, dmods[lp][3 * ((k - 1) % 3) + 2] = norm_bwd_res_bwd(
                h_in[k], g, shift, scale, dn, dh, core_out[k - 1], params(k - 1)[3], coef_prev, tag_of(k))
        else:
            dh, dg, dmods[l][j], dmods[l][j + 1] = norm_bwd_first(h_in[k], g, shift, scale, dn, dh, tag_of(k))
        dsmall[gain][l] = dg.reshape(d)
    dsmall = {k: jnp.stack(v) for k, v in dsmall.items()}
    dsmall['final_norm'] = dg_final.reshape(d)
    return loss, dh, dmods, dlw, dwf, dsmall


def _pack(parts, pad_rows):
    flat, where, off = [], [], 0
    for a in parts:
        n = _ceil_to(a.size, PACK_W)
        flat.append(jnp.pad(a.reshape(-1), (0, n - a.size)))
        where.append((off, n // PACK_W))
        off += n // PACK_W
    total = _ceil_to(off, pad_rows)
    if total > off:
        flat.append(jnp.zeros(((total - off) * PACK_W,), flat[0].dtype))
    return jnp.concatenate(flat).reshape(total, PACK_W), where


def _unpack(buf, where, shape):
    off, rows = where
    return buf[off:off + rows].reshape(-1)[:int(np.prod(shape))].reshape(shape)


def layer_weights(full, l):
    wi = full['w_in'][l]
    d = wi.shape[0]
    return {
        'w_in': jnp.concatenate([wi[:, :MAIN_W], jnp.zeros((d, MAIN_PAD - MAIN_W), wi.dtype), wi[:, MAIN_W:]], axis=1),
        'w_uq': _heads_to_parts(full['mla_w_uq'][l], MLA_NOPE).astype(F32),
        'w_ukv': _heads_to_parts(full['mla_w_ukv'][l], MLA_NOPE).astype(F32),
        'w_a': full['w_branch_a'][l], 'w_b': full['w_branch_b'][l], 'w_c': full['w_branch_c'][l], 'w_o': full['w_out'][l]}


def layer_grads_by_name(dlw):
    per_name = {k: [] for k, _ in BIG}
    for g in dlw:
        per_name['w_in'].append(jnp.concatenate([g['w_in'][:MAIN_W], g['w_in'][MAIN_PAD:]], axis=0))
        per_name['mla_w_uq'].append(_parts_to_heads(g['w_uq'], MLA_NOPE))
        per_name['mla_w_ukv'].append(_parts_to_heads(g['w_ukv'], MLA_NOPE))
        per_name['w_branch_a'].append(g['w_a'])
        per_name['w_branch_b'].append(g['w_b'])
        per_name['w_branch_c'].append(g['w_c'])
        per_name['w_out'].append(g['w_o'])
    return per_name


def kernel(x, c, ctx, c_ctx, w_ada, b_ada, ffn1_norm, ffn1_w_gate, ffn1_w_up, ffn1_w_down, mix_norm, w_in, na_rel_bias, gqa_q_norm, gqa_k_norm, mla_q_norm, mla_kv_norm, mla_w_uq, mla_w_ukv, w_branch_a, w_branch_b, w_branch_c, w_out, ffn2_norm, ffn2_w_gate, ffn2_w_up, ffn2_w_down, final_norm, loss_target, m_c_ctx, m_w_ada, m_b_ada, m_ffn1_norm, m_ffn1_w_gate, m_ffn1_w_up, m_ffn1_w_down, m_mix_norm, m_w_in, m_na_rel_bias, m_gqa_q_norm, m_gqa_k_norm, m_mla_q_norm, m_mla_kv_norm, m_mla_w_uq, m_mla_w_ukv, m_w_branch_a, m_w_branch_b, m_w_branch_c, m_w_out, m_ffn2_norm, m_ffn2_w_gate, m_ffn2_w_up, m_ffn2_w_down, m_final_norm, v_c_ctx, v_w_ada, v_b_ada, v_ffn1_norm, v_ffn1_w_gate, v_ffn1_w_up, v_ffn1_w_down, v_mix_norm, v_w_in, v_na_rel_bias, v_gqa_q_norm, v_gqa_k_norm, v_mla_q_norm, v_mla_kv_norm, v_mla_w_uq, v_mla_w_ukv, v_w_branch_a, v_w_branch_b, v_w_branch_c, v_w_out, v_ffn2_norm, v_ffn2_w_gate, v_ffn2_w_up, v_ffn2_w_down, v_final_norm):
    args = locals()
    wts = {k: args[k] for k in WEIGHTS}
    mom = {k: args['m_' + k] for k in WEIGHTS}
    var = {k: args['v_' + k] for k in WEIGHTS}
    nb, s_len, d = x.shape
    lc = ctx.shape[1]
    nl = w_ada.shape[0]
    nsh, ndev = 4, 8
    mx, my, mc = _place()
    sidx = 2 * mx + my
    didx = 4 * mx + 2 * my + mc
    assert d % LANE == 0 and MAIN_PAD % d == 0 and lc % TQ == 0 and s_len % TQ == 0 and s_len // GRID_W >= NA_ROWS

    wpack, wwhere = _pack([wts[k].astype(BF16) for k, _ in BIG], 32)
    wall = gather_shards(wpack.reshape(2, -1, PACK_W), name="gather_weights").reshape(nsh, -1, PACK_W)
    full = {}
    for (k, ax), wh in zip(BIG, wwhere):
        shp = wts[k].shape
        parts = jnp.stack([_unpack(wall[s], wh, shp) for s in range(nsh)])
        if ax == 1:
            full[k] = parts.transpose(1, 2, 0, 3).reshape(nl, shp[1], nsh * shp[2])
        else:
            full[k] = parts.transpose(1, 0, 2, 3).reshape(nl, nsh * shp[1], shp[2])
    lw = [layer_weights(full, l) for l in range(nl)]
    wl = jnp.stack([(wts[k].transpose(0, 2, 1) if tr else wts[k]).astype(BF16) for k, tr in zip(FFN_NAMES, FFN_TRANSPOSED)], axis=1)
    wf = gather_ffn(wl, name="gather_ffn")

    n_ex = ndev * nb
    ncol = w_ada.shape[-1]
    c_all = all_gather(c, name="gather_cond", with_c=True).reshape(n_ex, d)
    c_rows = jnp.concatenate([c_all, jnp.broadcast_to(c_ctx[None], (n_ex, d))], axis=0)
    b_shard = lax.dynamic_slice_in_dim(b_ada, sidx * ncol, ncol, axis=1)[:, None, :]
    mod_sh = ada_fwd(c_rows, w_ada, b_shard, name="ada_fwd")
    mod_all = all_gather(mod_sh, name="gather_mod", with_c=False)
    mod_all = mod_all.transpose(1, 2, 0, 3).reshape(nl, 2 * n_ex, nsh * ncol)
    mod_x = lax.dynamic_slice_in_dim(mod_all, didx * nb, nb, axis=1)
    mod_c = jnp.broadcast_to(mod_all[:, n_ex:n_ex + 1], mod_x.shape)
    mods = [[jnp.stack([mod_c[l, :, j * d:(j + 1) * d], mod_x[l, :, j * d:(j + 1) * d]], axis=1)[:, :, None, :]
             for j in range(N_MOD)] for l in range(nl)]

    small = {k: wts[k] for k in SMALL_LAYER + ['final_norm']}
    h0 = jnp.concatenate([ctx, x], axis=1)
    loss_part, dh0, dmods, dlw, dwf, dsmall = local_step(h0, loss_target, mods, lw, wf, small, lc=lc)
    grad_x = dh0[:, lc:]

    dmod_mine = jnp.stack([jnp.concatenate([m[:, :, 0, :] for m in dmods[l]], axis=-1) for l in range(nl)])
    small_names = SMALL_LAYER + ['final_norm']
    spack, swhere = _pack([loss_part] + [dsmall[k] for k in small_names] + [dmod_mine], 8)
    sall = all_gather(spack, name="gather_small", with_c=True)
    ssum = sum_slots(sall, name="sum_small")
    loss = _unpack(ssum, swhere[0], (1, LANE))[0, 0]
    grads = {k: _unpack(ssum, wh, wts[k].shape) for k, wh in zip(small_names, swhere[1:])}
    off, rows = swhere[-1]
    dm_all = sall[:, off:off + rows].reshape(ndev, -1)[:, :dmod_mine.size].reshape((ndev,) + dmod_mine.shape)
    dm_all = dm_all.transpose(1, 3, 0, 2, 4).reshape(nl, 2, n_ex, N_MOD * d)
    dm_rows = jnp.concatenate([dm_all[:, 1], dm_all[:, 0]], axis=1)
    dm_shard = lax.dynamic_slice_in_dim(dm_rows, sidx * ncol, ncol, axis=2)
    grads['w_ada'], gb, dc_part = ada_bwd(c_rows, w_ada, dm_shard, dm_rows, n_ex, name="ada_bwd")
    grads['b_ada'] = gb.reshape(b_ada.shape)
    dc_all = all_gather(jnp.pad(dc_part, ((0, 7), (0, 0))), name="gather_dcond", with_c=False)
    grads['c_ctx'] = sum_slots(dc_all, name="sum_dcond")[0]

    per_name = layer_grads_by_name(dlw)
    pieces, gwhere, off = [], [], 0
    for k, ax in BIG:
        shp = wts[k].shape
        for g in per_name[k]:
            if ax == 1 and k not in GRAD_TRANSPOSED:
                pieces.append(g.reshape(shp[1], nsh, shp[2]).transpose(1, 0, 2).reshape(nsh, -1))
            else:
                pieces.append(g.reshape(nsh, -1))
        n = int(np.prod(shp))
        if n % PACK_W:
            pieces.append(jnp.zeros((nsh, _ceil_to(n, PACK_W) - n), F32))
        gwhere.append((off, _ceil_to(n, PACK_W) // PACK_W))
        off += _ceil_to(n, PACK_W) // PACK_W
    if off % 128:
        pieces.append(jnp.zeros((nsh, (_ceil_to(off, 128) - off) * PACK_W), F32))
    half = _ceil_to(off, 128) // 2
    gpack = jnp.concatenate(pieces, axis=1).reshape(nsh, 2, half, PACK_W)
    from_pair = pair_exchange_halves(gpack, name="reduce_pair")
    chip_sum = add_kept_half(gpack, from_pair, jnp.reshape(mc, (1,)).astype(jnp.int32), name="reduce_pair_add",
                             out_dtype=BF16)
    from_xy = all_to_all_xy(chip_sum, name="reduce_xy")
    reduced = sum_slots(from_xy, name="reduce_xy_add")
    gfull = pair_all_gather(reduced, name="reduce_share").reshape(2 * half, PACK_W)
    for (k, _), wh in zip(BIG, gwhere):
        shp = wts[k].shape
        grads[k] = (_unpack(gfull, wh, (shp[0], shp[2], shp[1])).transpose(0, 2, 1) if k in GRAD_TRANSPOSED
                    else _unpack(gfull, wh, shp))
    for k, tr, g in zip(FFN_NAMES, FFN_TRANSPOSED, reduce_ffn(dwf[0], dwf[1], name="reduce_ffn")):
        grads[k] = g.transpose(0, 2, 1) if tr else g

    outs = {k: adamw(wts[k], grads[k], mom[k], var[k], name="adamw_" + k) for k in WEIGHTS}
    return (loss, grad_x, *[grads[k] for k in WEIGHTS], *[outs[k][0] for k in WEIGHTS], *[outs[k][1] for k in WEIGHTS],
            *[outs[k][2] for k in WEIGHTS])
```

```python
import functools

import jax
import jax.numpy as jnp
import numpy as np
from jax import lax
from jax.experimental import pallas as pl
from jax.experimental.pallas import tpu as pltpu

F32 = jnp.float32
BF16 = jnp.bfloat16
HI = lax.Precision.HIGHEST
MESH = pl.DeviceIdType.MESH
ANY = pl.BlockSpec(memory_space=pl.ANY)

V7X_VMEM_BYTES = 64 * 1024 * 1024
VMEM_LIMIT = V7X_VMEM_BYTES - 8 * 1024 * 1024
LANE = 128
PACK_W = 1024

GRID_W = 64
HEAD_DIM = 64
NA_HEADS, NA_ROWS, NA_COLS = 4, 8, 16
GQA_HEADS, GQA_KV_HEADS = 8, 2
MLA_HEADS, MLA_Q_RANK, MLA_KV_RANK, MLA_NOPE, MLA_ROPE, MLA_V = 4, 256, 128, 64, 32, 64
N_MOD = 9
ROPE_THETA = 10000.0
EPS = 1e-6
NEG_BIG = -1e30
NA_W = NA_HEADS * HEAD_DIM
GQ_W = GQA_HEADS * HEAD_DIM
GK_W = GQA_KV_HEADS * HEAD_DIM
MAIN_W = 3 * NA_W + GQ_W + 2 * GK_W + MLA_Q_RANK + MLA_KV_RANK + MLA_ROPE
MAIN_PAD = 2048
LOG2E, LN2 = float(np.log2(np.e)), float(np.log(2.0))
Q_SCALE = HEAD_DIM ** -0.5 * LOG2E
MLA_Q_SCALE = (MLA_NOPE + MLA_ROPE) ** -0.5 * LOG2E
TQ = 256
TM = 256

ADAM_LR, ADAM_B1, ADAM_B2, ADAM_EPS, ADAM_WD, ADAM_STEP = 0.001, 0.9, 0.999, 1e-08, 0.01, 10

ARG_NAMES = ['x', 'c', 'ctx', 'c_ctx', 'w_ada', 'b_ada', 'ffn1_norm', 'ffn1_w_gate', 'ffn1_w_up', 'ffn1_w_down', 'mix_norm', 'w_in',
             'na_rel_bias', 'gqa_q_norm', 'gqa_k_norm', 'mla_q_norm', 'mla_kv_norm', 'mla_w_uq', 'mla_w_ukv', 'w_branch_a',
             'w_branch_b', 'w_branch_c', 'w_out', 'ffn2_norm', 'ffn2_w_gate', 'ffn2_w_up', 'ffn2_w_down', 'final_norm']
WEIGHTS = ARG_NAMES[3:]
BIG = [('w_in', 1), ('mla_w_uq', 1), ('mla_w_ukv', 1), ('w_branch_a', 1), ('w_branch_b', 1), ('w_branch_c', 1), ('w_out', 0)]
GRAD_TRANSPOSED = ('w_in',)
FFN_NAMES = ['ffn1_w_gate', 'ffn1_w_up', 'ffn1_w_down', 'ffn2_w_gate', 'ffn2_w_up', 'ffn2_w_down']
FFN_TRANSPOSED = [True, True, False, True, True, False]
SMALL_LAYER = ['ffn1_norm', 'mix_norm', 'na_rel_bias', 'gqa_q_norm', 'gqa_k_norm', 'mla_q_norm', 'mla_kv_norm', 'ffn2_norm']


def _cp(*sem):
    return pltpu.CompilerParams(dimension_semantics=sem, vmem_limit_bytes=VMEM_LIMIT)


def _tile(dim, cands):
    for t in cands:
        if dim % t == 0:
            return t
    return dim


def _row_tile(rows, cap=512, mult=16):
    best = None
    for t in range(mult, min(rows, cap) + 1, mult):
        if rows % t == 0:
            best = t
    return best or rows


def _ceil_to(n, m):
    return -(-n // m) * m


def mm(a, b, *, name, ta=False, tb=False, out_dtype=F32, precise=False):
    m, k = (a.shape[1], a.shape[0]) if ta else a.shape
    n = b.shape[0] if tb else b.shape[1]
    tm = _tile(m, (512, 256, 128))
    tn = _tile(n, (1024, 1408, 512, 256, 128))
    tk = _tile(k, (1024, 1408, 512, 256, 128))
    nk = k // tk
    dims = (((0 if ta else 1,), (1 if tb else 0,)), ((), ()))

    def body(a_ref, b_ref, o_ref, *acc):
        if precise:
            part = lax.dot_general(a_ref[...].astype(F32), b_ref[...].astype(F32), dims, precision=HI, preferred_element_type=F32)
        else:
            part = lax.dot_general(a_ref[...].astype(BF16), b_ref[...].astype(BF16), dims, preferred_element_type=F32)
        if nk == 1:
            o_ref[...] = part.astype(o_ref.dtype)
        else:
            acc_ref, = acc
            kk = pl.program_id(2)

            @pl.when(kk == 0)
            def _():
                acc_ref[...] = part

            @pl.when(kk > 0)
            def _():
                acc_ref[...] += part

            @pl.when(kk == nk - 1)
            def _():
                o_ref[...] = acc_ref[...].astype(o_ref.dtype)

    a_spec = pl.BlockSpec((tk, tm), lambda i, j, kk: (kk, i)) if ta else pl.BlockSpec((tm, tk), lambda i, j, kk: (i, kk))
    b_spec = pl.BlockSpec((tn, tk), lambda i, j, kk: (j, kk)) if tb else pl.BlockSpec((tk, tn), lambda i, j, kk: (kk, j))
    return pl.pallas_call(
        body, name=name, grid=(m // tm, n // tn, nk), in_specs=[a_spec, b_spec],
        out_specs=pl.BlockSpec((tm, tn), lambda i, j, kk: (i, j)),
        out_shape=jax.ShapeDtypeStruct((m, n), out_dtype),
        scratch_shapes=[pltpu.VMEM((tm, tn), F32)] if nk > 1 else [],
        compiler_params=_cp("parallel", "parallel", "arbitrary"),
    )(a, b)


def mm_resident(a, w, *, name, tb=False, out_dtype=F32):
    m, k = a.shape
    n = w.shape[0] if tb else w.shape[1]
    tm = _tile(m, (512, 256, 128))
    cn = n if tb else _tile(n, (1024, 512, 256, 128))

    def body(a_ref, w_ref, o_ref):
        aa = a_ref[...].astype(BF16)
        if tb:
            o_ref[...] = _dot(aa, w_ref[...], _NT).astype(o_ref.dtype)
        else:
            for c in range(n // cn):
                cols = slice(cn * c, cn * (c + 1))
                o_ref[:, cols] = _dot(aa, w_ref[:, cols]).astype(o_ref.dtype)

    return pl.pallas_call(
        body, name=name, grid=(m // tm,),
        in_specs=[pl.BlockSpec((tm, k), lambda i: (i, 0)), pl.BlockSpec(w.shape, lambda i: (0, 0), pipeline_mode=pl.Buffered(1))],
        out_specs=pl.BlockSpec((tm, n), lambda i: (i, 0)), out_shape=jax.ShapeDtypeStruct((m, n), out_dtype),
        compiler_params=_cp("parallel"),
    )(a, w)


FFN_GATE, FFN_UP, FFN_DOWN = 0, 1, 2


def _ffn_wspec(wf, l, which):
    _, nsh, _, cs, d = wf.shape
    return pl.BlockSpec((None, nsh, None, cs, d), lambda *_: (l, 0, which, 0, 0), pipeline_mode=pl.Buffered(1))


def _ffn_group(cs):
    for g in (1, 2, 4):
        if (g * cs) % LANE == 0:
            return g
    raise ValueError(cs)


def ffn_up(n, wf, l, base, *, name):
    m, d = n.shape
    nsh, cs = wf.shape[1], wf.shape[3]
    f = nsh * cs
    grp = _ffn_group(cs)
    tm = _tile(m, (512, 256, 128))

    def body(n_ref, wg_ref, wu_ref, g_ref, u_ref, a_ref):
        nn = n_ref[...]
        for c in range(nsh // grp):
            cols = slice(grp * cs * c, grp * cs * (c + 1))
            g = _dot(nn, wg_ref[grp * c:grp * (c + 1)].reshape(grp * cs, d), _NT)
            u = _dot(nn, wu_ref[grp * c:grp * (c + 1)].reshape(grp * cs, d), _NT)
            g_ref[:, cols] = g.astype(BF16)
            u_ref[:, cols] = u.astype(BF16)
            a_ref[:, cols] = f_act_gu(g, u).astype(BF16)

    ospec = pl.BlockSpec((tm, f), lambda i: (i, 0))
    return pl.pallas_call(
        body, name=name, grid=(m // tm,),
        in_specs=[pl.BlockSpec((tm, d), lambda i: (i, 0)), _ffn_wspec(wf, l, base + FFN_GATE), _ffn_wspec(wf, l, base + FFN_UP)],
        out_specs=[ospec] * 3, out_shape=[jax.ShapeDtypeStruct((m, f), BF16)] * 3, compiler_params=_cp("parallel"),
    )(n, wf, wf)


def ffn_down(act, wf, l, base, *, name):
    m, f = act.shape
    nsh, cs, d = wf.shape[1], wf.shape[3], wf.shape[4]
    tm = _tile(m, (512, 256, 128))

    def body(a_ref, wd_ref, y_ref):
        y_ref[...] = _dot(a_ref[...], wd_ref[...].reshape(f, d))

    return pl.pallas_call(
        body, name=name, grid=(m // tm,),
        in_specs=[pl.BlockSpec((tm, f), lambda i: (i, 0)), _ffn_wspec(wf, l, base + FFN_DOWN)],
        out_specs=pl.BlockSpec((tm, d), lambda i: (i, 0)), out_shape=jax.ShapeDtypeStruct((m, d), F32), compiler_params=_cp("parallel"),
    )(act, wf)


def ffn_down_bwd(dy, g, u, wf, l, base, *, name):
    m, d = dy.shape
    nsh, cs = wf.shape[1], wf.shape[3]
    f = nsh * cs
    grp = _ffn_group(cs)
    tm = _tile(m, (512, 256, 128))

    def body(dy_ref, g_ref, u_ref, wd_ref, dg_ref, du_ref):
        dd = dy_ref[...]
        for c in range(nsh // grp):
            cols = slice(grp * cs * c, grp * cs * (c + 1))
            dact = _dot(dd, wd_ref[grp * c:grp * (c + 1)].reshape(grp * cs, d), _NT)
            g, u = g_ref[:, cols].astype(F32), u_ref[:, cols].astype(F32)
            sg = jax.nn.sigmoid(g)
            du_ref[:, cols] = (dact * (g * sg)).astype(BF16)
            dg_ref[:, cols] = (dact * u * (sg * (1.0 + g * (1.0 - sg)))).astype(BF16)

    fspec = pl.BlockSpec((tm, f), lambda i: (i, 0))
    return pl.pallas_call(
        body, name=name, grid=(m // tm,),
        in_specs=[pl.BlockSpec((tm, d), lambda i: (i, 0)), fspec, fspec, _ffn_wspec(wf, l, base + FFN_DOWN)],
        out_specs=[fspec] * 2, out_shape=[jax.ShapeDtypeStruct((m, f), BF16)] * 2, compiler_params=_cp("parallel"),
    )(dy, g, u, wf)


def ffn_up_bwd(dg, du, wf, l, base, *, name):
    m, f = dg.shape
    nsh, cs, d = wf.shape[1], wf.shape[3], wf.shape[4]
    tm = _tile(m, (512, 256, 128))

    def body(dg_ref, du_ref, wg_ref, wu_ref, dn_ref):
        dn_ref[...] = _dot(dg_ref[...], wg_ref[...].reshape(f, d)) + _dot(du_ref[...], wu_ref[...].reshape(f, d))

    fspec = pl.BlockSpec((tm, f), lambda i: (i, 0))
    return pl.pallas_call(
        body, name=name, grid=(m // tm,),
        in_specs=[fspec, fspec, _ffn_wspec(wf, l, base + FFN_GATE), _ffn_wspec(wf, l, base + FFN_UP)],
        out_specs=pl.BlockSpec((tm, d), lambda i: (i, 0)), out_shape=jax.ShapeDtypeStruct((m, d), F32), compiler_params=_cp("parallel"),
    )(dg, du, wf, wf)


def ffn_dw(a, b, nsh, *, name):
    m, f = a.shape
    d = b.shape[1]
    cs = f // nsh
    grp = _ffn_group(cs)
    tm = _tile(m, (1024, 512, 256, 128))

    def body(a_ref, b_ref, o_ref):
        part = _dot(a_ref[...], b_ref[...], _TN).reshape(grp, cs, d)
        i = pl.program_id(1)

        @pl.when(i == 0)
        def _():
            o_ref[...] = part

        @pl.when(i > 0)
        def _():
            o_ref[...] += part

    return pl.pallas_call(
        body, name=name, grid=(nsh // grp, m // tm),
        in_specs=[pl.BlockSpec((tm, grp * cs), lambda j, i: (i, j)), pl.BlockSpec((tm, d), lambda j, i: (i, 0))],
        out_specs=pl.BlockSpec((grp, cs, d), lambda j, i: (j, 0, 0)), out_shape=jax.ShapeDtypeStruct((nsh, cs, d), F32),
        compiler_params=_cp("parallel", "arbitrary"),
    )(a, b)


def rowcall(name, fn, ins, outs, *, nb, nt, nct):
    in_specs, arrays = [], []
    for arr, kind in ins:
        arrays.append(arr)
        if kind == 'tok':
            in_specs.append(pl.BlockSpec((None, TM, arr.shape[-1]), lambda b, t: (b, t, 0)))
        elif kind == 'lat':
            in_specs.append(pl.BlockSpec((None, TM, arr.shape[-1]), lambda b, t: (b, jnp.maximum(t - nct, 0), 0)))
        elif kind == 'pos':
            in_specs.append(pl.BlockSpec((TM, arr.shape[-1]), lambda b, t: (t, 0)))
        elif kind == 'mod':
            in_specs.append(pl.BlockSpec((None, None, 1, arr.shape[-1]), lambda b, t: (b, jnp.where(t >= nct, 1, 0), 0, 0)))
        elif kind == 'full':
            in_specs.append(pl.BlockSpec(arr.shape, lambda b, t, nd=arr.ndim: (0,) * nd))
        else:
            _, w, j = kind
            in_specs.append(pl.BlockSpec((None, TM, w), lambda b, t, j=j: (b, t, j)))
    out_specs, out_shape = [], []
    for o in outs:
        if o[0] == 'tok':
            out_specs.append(pl.BlockSpec((None, TM, o[1]), lambda b, t: (b, t, 0)))
            out_shape.append(jax.ShapeDtypeStruct((nb, nt * TM, o[1]), o[2]))
        elif o[0] == 'mod':
            out_specs.append(pl.BlockSpec((None, None, 1, o[1]), lambda b, t: (b, jnp.where(t >= nct, 1, 0), 0, 0)))
            out_shape.append(jax.ShapeDtypeStruct((nb, 2, 1, o[1]), F32))
        else:
            out_specs.append(pl.BlockSpec(o[1], lambda b, t, nd=len(o[1]): (0,) * nd))
            out_shape.append(jax.ShapeDtypeStruct(o[1], F32))
    n_in = len(ins)

    def body(*refs):
        b, t = pl.program_id(0), pl.program_id(1)
        res = fn(t < nct, *[r[...] for r in refs[:n_in]])
        for ref, o, val in zip(refs[n_in:], outs, res, strict=True):
            if o[0] == 'tok':
                ref[...] = val.astype(ref.dtype)
                continue
            first = ((t == 0) | (t == nct)) if o[0] == 'mod' else ((b == 0) & (t == 0))

            @pl.when(first)
            def _(ref=ref, val=val):
                ref[...] = val

            @pl.when(jnp.logical_not(first))
            def _(ref=ref, val=val):
                ref[...] += val

    return pl.pallas_call(body, name=name, grid=(nb, nt), in_specs=in_specs, out_specs=out_specs, out_shape=out_shape,
                          compiler_params=_cp("arbitrary", "arbitrary"))(*arrays)


def _rms(x, g):
    return x * lax.rsqrt(jnp.mean(x * x, axis=-1, keepdims=True) + EPS) * g


def f_normmod(h, g, shift, scale):
    return _rms(h, g) * (1.0 + scale) + shift


def f_act_gu(g, u):
    return jax.nn.silu(g) * u


def _dot_split(x, m, dims):
    hi = x.astype(BF16)
    lo = (x - hi.astype(F32)).astype(BF16)
    mb = m.astype(BF16)
    return (lax.dot_general(hi, mb, dims, preferred_element_type=F32) + lax.dot_general(lo, mb, dims, preferred_element_type=F32))


def dot_select(x, m):
    return _dot_select(x, m)


@jax.custom_vjp
def _dot_select(x, m):
    return _dot_split(x, m, (((1,), (0,)), ((), ())))


_dot_select.defvjp(lambda x, m: (_dot_split(x, m, (((1,), (0,)), ((), ()))), m),
                   lambda m, ct: (_dot_split(ct, m, (((1,), (1,)), ((), ()))), jnp.zeros_like(m)))


def f_merge(ga, gb, gm, ya, yb, ym):
    return jax.nn.sigmoid(ga) * ya + jax.nn.sigmoid(gb) * yb + jax.nn.sigmoid(gm) * ym


def f_post(p, cb, sb, cm, sm, qn, kn, mqn, mkvn, wuq, wukv, s_b, r_b, t_b, r_m, rep, dup):
    def hnorm(x, g, w):
        ms = dot_select(x * x, s_b[:w, :w])
        gw = dot_select(g, t_b[:, :w])
        return x * lax.rsqrt(ms + EPS) * gw

    def rope(x, cos, sin, rot):
        return x * cos + dot_select(x, rot) * sin

    o = 3 * NA_W
    a_q, a_k, a_v = p[:, 0:NA_W], p[:, NA_W:2 * NA_W], p[:, 2 * NA_W:o]
    b_q = rope(hnorm(p[:, o:o + GQ_W], qn, GQ_W), cb, sb, r_b)
    o += GQ_W
    b_k = rope(hnorm(p[:, o:o + GK_W], kn, GK_W), cb[:, :GK_W], sb[:, :GK_W], r_b[:GK_W, :GK_W])
    b_v = p[:, o + GK_W:o + 2 * GK_W]
    o += 2 * GK_W
    q_lat = jnp.dot(_rms(p[:, o:o + MLA_Q_RANK], mqn).astype(BF16), wuq.astype(BF16), preferred_element_type=F32)
    o += MLA_Q_RANK
    kv_lat = jnp.dot(_rms(p[:, o:o + MLA_KV_RANK], mkvn).astype(BF16), wukv.astype(BF16), preferred_element_type=F32)
    o += MLA_KV_RANK
    nw = MLA_HEADS * MLA_NOPE
    mq_nope, mq_rope = q_lat[:, :nw], rope(q_lat[:, nw:], cm, sm, r_m)
    mk_nope, m_v = kv_lat[:, :nw], kv_lat[:, nw:]
    mk_rope = dot_select(rope(p[:, o:o + LANE], cm, sm, r_m), rep)
    b_k2 = dot_select(b_k, dup)
    b_v2 = dot_select(b_v, dup)
    mq_cat = jnp.concatenate([mq_nope[:, :LANE], mq_rope, mq_nope[:, LANE:], mq_rope], axis=1) * MLA_Q_SCALE
    mk_cat = jnp.concatenate([mk_nope[:, :LANE], mk_rope, mk_nope[:, LANE:], mk_rope], axis=1)
    return (a_q * Q_SCALE, a_k, a_v, b_q * Q_SCALE, b_k2, b_v2, mq_cat, mk_cat, m_v)


POST_QK = (0, 1, 3, 4, 6, 7)


POST_WIDTHS = (NA_W, NA_W, NA_W, GQ_W, 2 * GK_W, 2 * GK_W, 4 * LANE, 4 * LANE, MLA_HEADS * MLA_V)
N_POST = len(POST_WIDTHS)


_NT = (((1,), (1,)), ((), ()))
_TN = (((0,), (0,)), ((), ()))


def _dot(a, b, dims=None):
    if dims is None:
        return jnp.dot(a, b, preferred_element_type=F32)
    return lax.dot_general(a, b, dims, preferred_element_type=F32)


def _lanes(lo, width):
    lane = lax.broadcasted_iota(jnp.int32, (1, LANE), 1)
    return (lane >= lo) & (lane < lo + width)


def _only(x, mask):
    return jnp.where(mask, x, jnp.zeros_like(x))


def _stack_pair(x, width, lo):
    return jnp.concatenate([_only(x, _lanes(lo, width)), _only(x, _lanes(lo + width, width))], axis=0)


def _pair_softmax(s):
    m = jnp.max(s, axis=-1, keepdims=True)
    p = jnp.exp2(s - m)
    l = jnp.sum(p, axis=-1, keepdims=True)
    return p, l, m + jnp.log2(l)


def gqa_fwd(q, k2, v2, *, lc, ctx_q, name):
    nb, t, qw = q.shape
    npair = qw // LANE
    per_kv = npair // GQA_KV_HEADS
    nctb = lc // TQ

    def body(q_ref, k_ref, v_ref, o_ref, lse_ref):
        i = pl.program_id(1)

        def run(rows):
            for pr in range(npair):
                lanes = slice(LANE * pr, LANE * (pr + 1))
                kv = slice(LANE * (pr // per_kv), LANE * (pr // per_kv + 1))
                kk, vv = k_ref[rows, kv], v_ref[rows, kv]
                outs = []
                for e in range(2):
                    p, l, lse = _pair_softmax(_dot(_only(q_ref[:, lanes], _lanes(HEAD_DIM * e, HEAD_DIM)), kk, _NT))
                    outs.append(_dot(p.astype(BF16), vv) / l)
                    lse_ref[2 * pr + e] = lse
                o_ref[:, lanes] = jnp.where(_lanes(0, HEAD_DIM), outs[0], outs[1]).astype(o_ref.dtype)

        @pl.when(i < nctb)
        def _():
            if ctx_q:
                run(pl.ds(0, lc))
            else:
                o_ref[...] = jnp.zeros_like(o_ref)
                lse_ref[...] = jnp.zeros_like(lse_ref)

        @pl.when(i >= nctb)
        def _():
            run(pl.ds(0, t))

    qmap = lambda b, i: (b, i, 0)
    kmap = lambda b, i: (b, 0, 0)
    kw = k2.shape[-1]
    return pl.pallas_call(
        body, name=name, grid=(nb, t // TQ),
        in_specs=[pl.BlockSpec((None, TQ, qw), qmap), pl.BlockSpec((None, t, kw), kmap), pl.BlockSpec((None, t, kw), kmap)],
        out_specs=[pl.BlockSpec((None, TQ, qw), qmap), pl.BlockSpec((None, 2 * npair, TQ, 1), lambda b, i: (b, 0, i, 0))],
        out_shape=[jax.ShapeDtypeStruct((nb, t, qw), BF16), jax.ShapeDtypeStruct((nb, 2 * npair, t, 1), F32)],
        compiler_params=_cp("parallel", "arbitrary"),
    )(q, k2, v2)


def gqa_bwd(q, k2, v2, lse, do, *, lc, ctx_q, name):
    nb, t, qw = q.shape
    npair = qw // LANE
    per_kv = npair // GQA_KV_HEADS
    nctb = lc // TQ

    def body(q_ref, k_ref, v_ref, lse_ref, do_ref, dq_ref, dk_ref, dv_ref):
        i = pl.program_id(1)

        @pl.when(i == 0)
        def _():
            dk_ref[...] = jnp.zeros_like(dk_ref)
            dv_ref[...] = jnp.zeros_like(dv_ref)

        def run(rows):
            for pr in range(npair):
                lanes = slice(LANE * pr, LANE * (pr + 1))
                kv = slice(LANE * (pr // per_kv), LANE * (pr // per_kv + 1))
                kk, vv = k_ref[rows, kv], v_ref[rows, kv]
                qq, dd = _stack_pair(q_ref[:, lanes], HEAD_DIM, 0), _stack_pair(do_ref[:, lanes], HEAD_DIM, 0)
                p = jnp.exp2(_dot(qq, kk, _NT) - jnp.concatenate([lse_ref[2 * pr], lse_ref[2 * pr + 1]], axis=0))
                dp = _dot(dd, vv, _NT)
                delta = jnp.sum(p * dp, axis=-1, keepdims=True)
                ds = (p * (dp - delta)).astype(BF16)
                dq = _dot(ds, kk)
                dq_ref[:, lanes] = jnp.where(_lanes(0, HEAD_DIM), dq[:TQ], dq[TQ:])
                dk_ref[rows, kv] += _dot(ds, qq, _TN)
                dv_ref[rows, kv] += _dot(p.astype(BF16), dd, _TN)

        @pl.when(i < nctb)
        def _():
            if ctx_q:
                run(pl.ds(0, lc))
            else:
                dq_ref[...] = jnp.zeros_like(dq_ref)

        @pl.when(i >= nctb)
        def _():
            run(pl.ds(0, t))

    qmap = lambda b, i: (b, i, 0)
    kmap = lambda b, i: (b, 0, 0)
    kw = k2.shape[-1]
    return pl.pallas_call(
        body, name=name, grid=(nb, t // TQ),
        in_specs=[pl.BlockSpec((None, TQ, qw), qmap), pl.BlockSpec((None, t, kw), kmap), pl.BlockSpec((None, t, kw), kmap),
                  pl.BlockSpec((None, 2 * npair, TQ, 1), lambda b, i: (b, 0, i, 0)), pl.BlockSpec((None, TQ, qw), qmap)],
        out_specs=[pl.BlockSpec((None, TQ, qw), qmap), pl.BlockSpec((None, t, kw), kmap), pl.BlockSpec((None, t, kw), kmap)],
        out_shape=[jax.ShapeDtypeStruct((nb, t, qw), F32), jax.ShapeDtypeStruct(k2.shape, F32), jax.ShapeDtypeStruct(v2.shape, F32)],
        compiler_params=_cp("arbitrary", "arbitrary"),
    )(q, k2, v2, lse, do)


def _mla_lanes(pr, e):
    lane = lax.broadcasted_iota(jnp.int32, (1, 2 * LANE), 1)
    lo = LANE + MLA_ROPE * (2 * pr + e)
    return ((lane >= MLA_NOPE * e) & (lane < MLA_NOPE * (e + 1))) | ((lane >= lo) & (lane < lo + MLA_ROPE))


def mla_fwd(q, k, v, *, lc, ctx_q, name):
    nb, t, w = v.shape
    npair = w // LANE
    nctb = lc // TQ

    def body(q_ref, k_ref, v_ref, o_ref, lse_ref):
        i = pl.program_id(1)

        def run(rows):
            for pr in range(npair):
                wide, lanes = slice(2 * LANE * pr, 2 * LANE * (pr + 1)), slice(LANE * pr, LANE * (pr + 1))
                kk, vv = k_ref[rows, wide], v_ref[rows, lanes]
                outs = []
                for e in range(2):
                    p, l, lse = _pair_softmax(_dot(_only(q_ref[:, wide], _mla_lanes(pr, e)), kk, _NT))
                    outs.append(_dot(p.astype(BF16), vv) / l)
                    lse_ref[2 * pr + e] = lse
                o_ref[:, lanes] = jnp.where(_lanes(0, MLA_V), outs[0], outs[1]).astype(o_ref.dtype)

        @pl.when(i < nctb)
        def _():
            if ctx_q:
                run(pl.ds(0, lc))
            else:
                o_ref[...] = jnp.zeros_like(o_ref)
                lse_ref[...] = jnp.zeros_like(lse_ref)

        @pl.when(i >= nctb)
        def _():
            run(pl.ds(0, t))

    qmap = lambda b, i: (b, i, 0)
    kmap = lambda b, i: (b, 0, 0)
    return pl.pallas_call(
        body, name=name, grid=(nb, t // TQ),
        in_specs=[pl.BlockSpec((None, TQ, 2 * w), qmap), pl.BlockSpec((None, t, 2 * w), kmap), pl.BlockSpec((None, t, w), kmap)],
        out_specs=[pl.BlockSpec((None, TQ, w), qmap), pl.BlockSpec((None, 2 * npair, TQ, 1), lambda b, i: (b, 0, i, 0))],
        out_shape=[jax.ShapeDtypeStruct((nb, t, w), BF16), jax.ShapeDtypeStruct((nb, 2 * npair, t, 1), F32)],
        compiler_params=_cp("parallel", "arbitrary"),
    )(q, k, v)


def mla_bwd(q, k, v, lse, do, *, lc, ctx_q, name):
    nb, t, w = v.shape
    npair = w // LANE
    nctb = lc // TQ

    def body(q_ref, k_ref, v_ref, lse_ref, do_ref, dq_ref, dk_ref, dv_ref):
        i = pl.program_id(1)

        @pl.when(i == 0)
        def _():
            dk_ref[...] = jnp.zeros_like(dk_ref)
            dv_ref[...] = jnp.zeros_like(dv_ref)

        def run(rows):
            for pr in range(npair):
                wide, lanes = slice(2 * LANE * pr, 2 * LANE * (pr + 1)), slice(LANE * pr, LANE * (pr + 1))
                kk, vv = k_ref[rows, wide], v_ref[rows, lanes]
                m0, m1 = _mla_lanes(pr, 0), _mla_lanes(pr, 1)
                qq = jnp.concatenate([_only(q_ref[:, wide], m0), _only(q_ref[:, wide], m1)], axis=0)
                dd = _stack_pair(do_ref[:, lanes], MLA_V, 0)
                p = jnp.exp2(_dot(qq, kk, _NT) - jnp.concatenate([lse_ref[2 * pr], lse_ref[2 * pr + 1]], axis=0))
                dp = _dot(dd, vv, _NT)
                delta = jnp.sum(p * dp, axis=-1, keepdims=True)
                ds = (p * (dp - delta)).astype(BF16)
                dq = _dot(ds, kk)
                dq_ref[:, wide] = _only(dq[:TQ], m0) + _only(dq[TQ:], m1)
                dk_ref[rows, wide] += _dot(ds, qq, _TN)
                dv_ref[rows, lanes] += _dot(p.astype(BF16), dd, _TN)

        @pl.when(i < nctb)
        def _():
            if ctx_q:
                run(pl.ds(0, lc))
            else:
                dq_ref[...] = jnp.zeros_like(dq_ref)

        @pl.when(i >= nctb)
        def _():
            run(pl.ds(0, t))

    qmap = lambda b, i: (b, i, 0)
    kmap = lambda b, i: (b, 0, 0)
    return pl.pallas_call(
        body, name=name, grid=(nb, t // TQ),
        in_specs=[pl.BlockSpec((None, TQ, 2 * w), qmap), pl.BlockSpec((None, t, 2 * w), kmap), pl.BlockSpec((None, t, w), kmap),
                  pl.BlockSpec((None, 2 * npair, TQ, 1), lambda b, i: (b, 0, i, 0)), pl.BlockSpec((None, TQ, w), qmap)],
        out_specs=[pl.BlockSpec((None, TQ, 2 * w), qmap), pl.BlockSpec((None, t, 2 * w), kmap), pl.BlockSpec((None, t, w), kmap)],
        out_shape=[jax.ShapeDtypeStruct(q.shape, F32), jax.ShapeDtypeStruct(k.shape, F32), jax.ShapeDtypeStruct(v.shape, F32)],
        compiler_params=_cp("arbitrary", "arbitrary"),
    )(q, k, v, lse, do)


def _na_window(st, nc, rows):
    r = jnp.maximum(st - nc, 0)
    r0 = jnp.clip(r - NA_ROWS // 2, 0, rows - NA_ROWS)
    return r, r0, r - r0


def na_fwd(q, k, v, bias, *, lc, ctx_q, name):
    nb, t, w = q.shape
    npair = w // LANE
    nc, rows = lc // GRID_W, (t - lc) // GRID_W
    nwin = NA_ROWS * GRID_W

    def body(q_ref, k_ref, v_ref, bias_ref, o_ref, lse_ref):
        st = pl.program_id(1)
        ctx = pl.ds(0, lc)

        @pl.when(st < nc)
        def _():
            if not ctx_q:
                o_ref[...] = jnp.zeros_like(o_ref)
                lse_ref[...] = jnp.zeros_like(lse_ref)
                return
            for pr in range(npair):
                lanes = slice(LANE * pr, LANE * (pr + 1))
                kc, vc = k_ref[ctx, lanes], v_ref[ctx, lanes]
                outs = []
                for e in range(2):
                    p, l, lse = _pair_softmax(_dot(_only(q_ref[:, lanes], _lanes(HEAD_DIM * e, HEAD_DIM)), kc, _NT))
                    outs.append(_dot(p.astype(BF16), vc) / l)
                    lse_ref[2 * pr + e] = lse
                o_ref[:, lanes] = jnp.where(_lanes(0, HEAD_DIM), outs[0], outs[1]).astype(o_ref.dtype)

        @pl.when(st >= nc)
        def _():
            _, r0, _ = _na_window(st, nc, rows)
            win = pl.ds(pl.multiple_of(lc + r0 * GRID_W, GRID_W), nwin)
            for pr in range(npair):
                lanes = slice(LANE * pr, LANE * (pr + 1))
                kc, vc, kw, vw = k_ref[ctx, lanes], v_ref[ctx, lanes], k_ref[win, lanes], v_ref[win, lanes]
                qq = _stack_pair(q_ref[:, lanes], HEAD_DIM, 0)
                s_loc = _dot(qq, kw, _NT) + jnp.concatenate([bias_ref[2 * pr], bias_ref[2 * pr + 1]], axis=0) * LOG2E
                s_ctx = _dot(qq, kc, _NT)
                m = jnp.maximum(jnp.max(s_loc, axis=-1, keepdims=True), jnp.max(s_ctx, axis=-1, keepdims=True))
                p_loc, p_ctx = jnp.exp2(s_loc - m), jnp.exp2(s_ctx - m)
                l = jnp.sum(p_loc, axis=-1, keepdims=True) + jnp.sum(p_ctx, axis=-1, keepdims=True)
                o = (_dot(p_loc.astype(BF16), vw) + _dot(p_ctx.astype(BF16), vc)) / l
                lse = m + jnp.log2(l)
                lse_ref[2 * pr], lse_ref[2 * pr + 1] = lse[:GRID_W], lse[GRID_W:]
                o_ref[:, lanes] = jnp.where(_lanes(0, HEAD_DIM), o[:GRID_W], o[GRID_W:]).astype(o_ref.dtype)

    qmap = lambda b, st: (b, st, 0)
    kmap = lambda b, st: (b, 0, 0)
    return pl.pallas_call(
        body, name=name, grid=(nb, nc + rows),
        in_specs=[pl.BlockSpec((None, GRID_W, w), qmap), pl.BlockSpec((None, t, w), kmap), pl.BlockSpec((None, t, w), kmap),
                  pl.BlockSpec((2 * npair, None, GRID_W, nwin), lambda b, st: (0, _na_window(st, nc, rows)[2], 0, 0))],
        out_specs=[pl.BlockSpec((None, GRID_W, w), qmap), pl.BlockSpec((None, 2 * npair, GRID_W, 1), lambda b, st: (b, 0, st, 0))],
        out_shape=[jax.ShapeDtypeStruct((nb, t, w), BF16), jax.ShapeDtypeStruct((nb, 2 * npair, t, 1), F32)],
        compiler_params=_cp("parallel", "arbitrary"),
    )(q, k, v, bias)


def na_bwd(q, k, v, bias, lse, do, *, lc, ctx_q, name):
    nb, t, w = q.shape
    npair = w // LANE
    nc, rows = lc // GRID_W, (t - lc) // GRID_W
    nwin = NA_ROWS * GRID_W

    def body(q_ref, k_ref, v_ref, bias_ref, lse_ref, do_ref, dq_ref, dk_ref, dv_ref, db_ref):
        b, st = pl.program_id(0), pl.program_id(1)

        @pl.when(st == 0)
        def _():
            dk_ref[...] = jnp.zeros_like(dk_ref)
            dv_ref[...] = jnp.zeros_like(dv_ref)

        @pl.when((st == 0) & (b == 0))
        def _():
            db_ref[...] = jnp.zeros_like(db_ref)

        ctx = pl.ds(0, lc)

        @pl.when(st < nc)
        def _():
            if not ctx_q:
                dq_ref[...] = jnp.zeros_like(dq_ref)
                return
            for pr in range(npair):
                lanes = slice(LANE * pr, LANE * (pr + 1))
                kc, vc = k_ref[ctx, lanes], v_ref[ctx, lanes]
                dqs = []
                for e in range(2):
                    mine = _lanes(HEAD_DIM * e, HEAD_DIM)
                    qq, dd = _only(q_ref[:, lanes], mine), _only(do_ref[:, lanes], mine)
                    p = jnp.exp2(_dot(qq, kc, _NT) - lse_ref[2 * pr + e])
                    dp = _dot(dd, vc, _NT)
                    delta = jnp.sum(p * dp, axis=-1, keepdims=True)
                    ds = (p * (dp - delta)).astype(BF16)
                    dqs.append(_dot(ds, kc))
                    dk_ref[ctx, lanes] += _dot(ds, qq, _TN)
                    dv_ref[ctx, lanes] += _dot(p.astype(BF16), dd, _TN)
                dq_ref[:, lanes] = jnp.where(_lanes(0, HEAD_DIM), dqs[0], dqs[1])

        @pl.when(st >= nc)
        def _():
            _, r0, case = _na_window(st, nc, rows)
            win = pl.ds(pl.multiple_of(lc + r0 * GRID_W, GRID_W), nwin)
            for pr in range(npair):
                lanes = slice(LANE * pr, LANE * (pr + 1))
                kc, vc, kw, vw = k_ref[ctx, lanes], v_ref[ctx, lanes], k_ref[win, lanes], v_ref[win, lanes]
                qq, dd = _stack_pair(q_ref[:, lanes], HEAD_DIM, 0), _stack_pair(do_ref[:, lanes], HEAD_DIM, 0)
                lse = jnp.concatenate([lse_ref[2 * pr], lse_ref[2 * pr + 1]], axis=0)
                bias2 = jnp.concatenate([bias_ref[2 * pr], bias_ref[2 * pr + 1]], axis=0)
                p_loc = jnp.exp2(_dot(qq, kw, _NT) + bias2 * LOG2E - lse)
                p_ctx = jnp.exp2(_dot(qq, kc, _NT) - lse)
                dp_loc, dp_ctx = _dot(dd, vw, _NT), _dot(dd, vc, _NT)
                delta = jnp.sum(p_loc * dp_loc, axis=-1, keepdims=True) + jnp.sum(p_ctx * dp_ctx, axis=-1, keepdims=True)
                ds_loc = p_loc * (dp_loc - delta)
                db_ref[2 * pr, case] += ds_loc[:GRID_W]
                db_ref[2 * pr + 1, case] += ds_loc[GRID_W:]
                ds_loc = ds_loc.astype(BF16)
                ds_ctx = (p_ctx * (dp_ctx - delta)).astype(BF16)
                dq = _dot(ds_loc, kw) + _dot(ds_ctx, kc)
                dq_ref[:, lanes] = jnp.where(_lanes(0, HEAD_DIM), dq[:GRID_W], dq[GRID_W:])
                dk_ref[win, lanes] += _dot(ds_loc, qq, _TN)
                dk_ref[ctx, lanes] += _dot(ds_ctx, qq, _TN)
                dv_ref[win, lanes] += _dot(p_loc.astype(BF16), dd, _TN)
                dv_ref[ctx, lanes] += _dot(p_ctx.astype(BF16), dd, _TN)

    qmap = lambda b, st: (b, st, 0)
    kmap = lambda b, st: (b, 0, 0)
    nh = 2 * npair
    return pl.pallas_call(
        body, name=name, grid=(nb, nc + rows),
        in_specs=[pl.BlockSpec((None, GRID_W, w), qmap), pl.BlockSpec((None, t, w), kmap), pl.BlockSpec((None, t, w), kmap),
                  pl.BlockSpec((nh, None, GRID_W, nwin), lambda b, st: (0, _na_window(st, nc, rows)[2], 0, 0)),
                  pl.BlockSpec((None, nh, GRID_W, 1), lambda b, st: (b, 0, st, 0)), pl.BlockSpec((None, GRID_W, w), qmap)],
        out_specs=[pl.BlockSpec((None, GRID_W, w), qmap), pl.BlockSpec((None, t, w), kmap), pl.BlockSpec((None, t, w), kmap),
                   pl.BlockSpec((nh, NA_ROWS, GRID_W, nwin), lambda b, st: (0, 0, 0, 0))],
        out_shape=[jax.ShapeDtypeStruct((nb, t, w), F32), jax.ShapeDtypeStruct((nb, t, w), F32), jax.ShapeDtypeStruct((nb, t, w), F32),
                   jax.ShapeDtypeStruct((nh, NA_ROWS, GRID_W, nwin), F32)],
        compiler_params=_cp("arbitrary", "arbitrary"),
    )(q, k, v, bias, lse, do)


def _na_tables():
    cols = np.arange(GRID_W)
    c0 = np.clip(cols - NA_COLS // 2, 0, GRID_W - NA_COLS)
    col_in = (cols[None, :] >= c0[:, None]) & (cols[None, :] < c0[:, None] + NA_COLS)
    dc = np.clip(cols[None, :] - cols[:, None] + NA_COLS - 1, 0, 2 * NA_COLS - 2)
    dr = np.arange(NA_ROWS)[None, :] + (NA_ROWS - 1) - np.arange(NA_ROWS)[:, None]
    return col_in, dc, dr


def _na_onehots():
    col_in, dc, dr = _na_tables()
    e1 = np.zeros((GRID_W, GRID_W, LANE), np.float32)
    qi, ki = np.nonzero(col_in)
    e1[qi, ki, dc[qi, ki]] = 1.0
    e2 = np.zeros((2 * NA_ROWS, NA_ROWS, NA_ROWS), np.float32)
    ci, ji = np.meshgrid(np.arange(NA_ROWS), np.arange(NA_ROWS), indexing='ij')
    e2[dr[ci, ji], ci, ji] = 1.0
    return jnp.asarray(e1.reshape(GRID_W * GRID_W, LANE)), jnp.asarray(e2.reshape(2 * NA_ROWS, NA_ROWS * NA_ROWS)), col_in


def na_expand_bias(rel_bias, name):
    e1, e2, col_in = _na_onehots()
    nh = rel_bias.shape[0]
    nrow = NA_ROWS * NA_ROWS
    rel = jnp.pad(rel_bias, ((0, 0), (0, 1), (0, LANE - rel_bias.shape[2])))
    rel = rel.transpose(1, 0, 2).reshape(2 * NA_ROWS, nh * LANE)
    y = mm(e2, rel, ta=True, name=name + "_rows", precise=True)
    y = y.reshape(nrow, nh, LANE).transpose(1, 0, 2).reshape(nh * nrow, LANE)
    g = mm(y, e1, tb=True, name=name + "_cols", precise=True)
    g = g.reshape(nh, NA_ROWS, NA_ROWS, GRID_W, GRID_W).transpose(0, 1, 3, 2, 4)
    g = jnp.where(col_in[None, None, :, None, :], g, NEG_BIG)
    return g.reshape(nh, NA_ROWS, GRID_W, NA_ROWS * GRID_W)


def na_reduce_bias(dexp, name):
    e1, e2, _ = _na_onehots()
    nh = dexp.shape[0]
    x = dexp.reshape(nh, NA_ROWS, GRID_W, NA_ROWS, GRID_W).transpose(0, 1, 3, 2, 4).reshape(nh * NA_ROWS * NA_ROWS, GRID_W * GRID_W)
    y = mm(x, e1, name=name + "_cols", precise=True)
    y = y.reshape(nh, NA_ROWS * NA_ROWS, LANE).transpose(1, 0, 2).reshape(NA_ROWS * NA_ROWS, nh * LANE)
    z = mm(e2, y, name=name + "_rows", precise=True)
    return z.reshape(2 * NA_ROWS, nh, LANE).transpose(1, 0, 2)[:, :2 * NA_ROWS - 1, :2 * NA_COLS - 1]


def _rot_matrix(width, d_rot):
    f = d_rot // 4
    r = np.zeros((width, width), np.float32)
    for base in range(0, width, d_rot // 2):
        for j in range(f):
            r[base + f + j, base + j] = -1.0
            r[base + j, base + f + j] = 1.0
    return r


def _rope_tables(s_len, lc, d_rot, reps):
    half = d_rot // 2
    freqs = ROPE_THETA ** (-jnp.arange(0, half, 2, dtype=F32) / half)
    tpos = jnp.arange(s_len)
    row = (tpos // GRID_W).astype(F32)[:, None] * freqs
    col = (tpos % GRID_W).astype(F32)[:, None] * freqs
    ang = jnp.concatenate([row, row, col, col], axis=-1)
    cos = jnp.concatenate([jnp.ones((lc, d_rot), F32), jnp.cos(ang)], axis=0)
    sin = jnp.concatenate([jnp.zeros((lc, d_rot), F32), jnp.sin(ang)], axis=0)
    return jnp.tile(cos, (1, reps)), jnp.tile(sin, (1, reps))


def _post_consts():
    s_b = np.kron(np.eye(GQA_HEADS, dtype=np.float32), np.full((HEAD_DIM, HEAD_DIM), 1.0 / HEAD_DIM, np.float32))
    t_b = np.tile(np.eye(HEAD_DIM, dtype=np.float32), (1, GQA_HEADS))
    r_b = _rot_matrix(GQ_W, HEAD_DIM)
    r_m = _rot_matrix(LANE, MLA_ROPE)
    rep = np.zeros((LANE, LANE), np.float32)
    for h in range(MLA_HEADS):
        rep[np.arange(MLA_ROPE), h * MLA_ROPE + np.arange(MLA_ROPE)] = 1.0
    dup = np.zeros((GK_W, 2 * GK_W), np.float32)
    for j in range(GQA_KV_HEADS):
        for e in range(2):
            dup[HEAD_DIM * j + np.arange(HEAD_DIM), 2 * HEAD_DIM * j + HEAD_DIM * e + np.arange(HEAD_DIM)] = 1.0
    return tuple(jnp.asarray(a) for a in (s_b, r_b, t_b, r_m, rep, dup))


def _heads_to_parts(w, first):
    r = w.shape[0]
    w3 = w.reshape(r, MLA_HEADS, -1)
    return jnp.concatenate([w3[:, :, :first].reshape(r, -1), w3[:, :, first:].reshape(r, -1)], axis=1)


def _parts_to_heads(w, first):
    r = w.shape[0]
    nf = MLA_HEADS * first
    return jnp.concatenate([w[:, :nf].reshape(r, MLA_HEADS, first), w[:, nf:].reshape(r, MLA_HEADS, -1)], axis=2).reshape(r, -1)


def _place():
    return lax.axis_index("x"), lax.axis_index("y"), lax.axis_index("c")


def all_gather(v, *, name, with_c):
    flips = [(dx, dy, dc) for dx in (0, 1) for dy in (0, 1) for dc in ((0, 1) if with_c else (0,))][1:]
    n = len(flips) + 1

    def body(v_ref, out_ref, send_sems, recv_sems, local_sem):
        mx, my, mc = _place()

        def slot(px, py, pc):
            return 4 * px + 2 * py + pc if with_c else 2 * px + py

        mine = pltpu.make_async_copy(v_ref, out_ref.at[slot(mx, my, mc)], local_sem)
        mine.start()
        sends = []
        for j, (dx, dy, dc) in enumerate(flips):
            peer = (mx ^ dx, my ^ dy, mc ^ dc)
            cp = pltpu.make_async_remote_copy(src_ref=v_ref, dst_ref=out_ref.at[slot(mx, my, mc)], send_sem=send_sems.at[j],
                                              recv_sem=recv_sems.at[j], device_id=peer, device_id_type=MESH)
            cp.start()
            sends.append(cp)
        for j, (dx, dy, dc) in enumerate(flips):
            peer = (mx ^ dx, my ^ dy, mc ^ dc)
            pltpu.make_async_remote_copy(src_ref=v_ref, dst_ref=out_ref.at[slot(*peer)], send_sem=send_sems.at[j],
                                         recv_sem=recv_sems.at[j], device_id=peer, device_id_type=MESH).wait_recv()
        for cp in sends:
            cp.wait_send()
        mine.wait()

    return pl.pallas_call(
        body, name=name, in_specs=[ANY], out_specs=ANY, out_shape=jax.ShapeDtypeStruct((n,) + v.shape, v.dtype),
        scratch_shapes=[pltpu.SemaphoreType.DMA((n - 1,)), pltpu.SemaphoreType.DMA((n - 1,)), pltpu.SemaphoreType.DMA(())],
    )(v)


def gather_shards(v, *, name):
    _, h, w = v.shape
    flips = [(1, 0), (0, 1), (1, 1)]

    def body(v_ref, out_ref, send_sems, recv_sems):
        mx, my, mc = _place()
        me = 2 * mx + my
        sib = (mx, my, 1 - mc)

        def copy(k, src, dst, to):
            return pltpu.make_async_remote_copy(src_ref=src, dst_ref=dst, send_sem=send_sems.at[k], recv_sem=recv_sems.at[k],
                                                device_id=to, device_id_type=MESH)

        first = [copy(j, v_ref.at[mc], out_ref.at[me, mc], (mx ^ dx, my ^ dy, mc)) for j, (dx, dy) in enumerate(flips)]
        for cp in first:
            cp.start()
        passed = []
        for j, (dx, dy) in enumerate(flips):
            theirs = out_ref.at[2 * (mx ^ dx) + (my ^ dy), mc]
            copy(j, v_ref.at[mc], theirs, (mx ^ dx, my ^ dy, mc)).wait_recv()
            fw = copy(3 + j, theirs, theirs, sib)
            fw.start()
            passed.append(fw)
        for j, (dx, dy) in enumerate(flips):
            other = out_ref.at[2 * (mx ^ dx) + (my ^ dy), 1 - mc]
            copy(3 + j, other, other, sib).wait_recv()
        for cp in first + passed:
            cp.wait_send()

    out = pl.pallas_call(
        body, name=name, in_specs=[ANY], out_specs=ANY, out_shape=jax.ShapeDtypeStruct((4, 2, h, w), v.dtype),
        scratch_shapes=[pltpu.SemaphoreType.DMA((6,)), pltpu.SemaphoreType.DMA((6,))],
    )(v)
    mx, my, _ = _place()
    return lax.dynamic_update_slice(out, v[None], (2 * mx + my, 0, 0, 0))


def pair_exchange_halves(g, *, name):
    n, _, h, w = g.shape

    def body(g_ref, out_ref, send_sems, recv_sems):
        mx, my, mc = _place()
        sib = (mx, my, 1 - mc)
        cps = [pltpu.make_async_remote_copy(src_ref=g_ref.at[s, 1 - mc], dst_ref=out_ref.at[s], send_sem=send_sems.at[s],
                                            recv_sem=recv_sems.at[s], device_id=sib, device_id_type=MESH) for s in range(n)]
        for cp in cps:
            cp.start()
        for cp in cps:
            cp.wait_recv()
        for cp in cps:
            cp.wait_send()

    return pl.pallas_call(
        body, name=name, in_specs=[ANY], out_specs=ANY, out_shape=jax.ShapeDtypeStruct((n, h, w), g.dtype),
        scratch_shapes=[pltpu.SemaphoreType.DMA((n,)), pltpu.SemaphoreType.DMA((n,))],
    )(g)


def all_to_all_xy(v, *, name):
    def body(v_ref, out_ref, send_sems, recv_sems):
        mx, my, mc = _place()
        me = 2 * mx + my
        flips = [(1, 0), (0, 1), (1, 1)]
        sends = []
        for j, (dx, dy) in enumerate(flips):
            px, py = mx ^ dx, my ^ dy
            cp = pltpu.make_async_remote_copy(src_ref=v_ref.at[2 * px + py], dst_ref=out_ref.at[me], send_sem=send_sems.at[j],
                                              recv_sem=recv_sems.at[j], device_id=(px, py, mc), device_id_type=MESH)
            cp.start()
            sends.append(cp)
        for j, (dx, dy) in enumerate(flips):
            px, py = mx ^ dx, my ^ dy
            pltpu.make_async_remote_copy(src_ref=v_ref.at[me], dst_ref=out_ref.at[2 * px + py], send_sem=send_sems.at[j],
                                         recv_sem=recv_sems.at[j], device_id=(px, py, mc), device_id_type=MESH).wait_recv()
        for cp in sends:
            cp.wait_send()

    out = pl.pallas_call(
        body, name=name, in_specs=[ANY], out_specs=ANY, out_shape=jax.ShapeDtypeStruct(v.shape, v.dtype),
        scratch_shapes=[pltpu.SemaphoreType.DMA((3,)), pltpu.SemaphoreType.DMA((3,))],
    )(v)
    mx, my, _ = _place()
    me = 2 * mx + my
    return lax.dynamic_update_slice(out, lax.dynamic_slice_in_dim(v, me, 1, axis=0), (me, 0, 0))


def pair_all_gather(v, *, name):
    def body(v_ref, out_ref, send_sem, recv_sem):
        mx, my, mc = _place()
        cp = pltpu.make_async_remote_copy(src_ref=v_ref, dst_ref=out_ref.at[mc], send_sem=send_sem, recv_sem=recv_sem,
                                          device_id=(mx, my, 1 - mc), device_id_type=MESH)
        cp.start()
        pltpu.make_async_remote_copy(src_ref=v_ref, dst_ref=out_ref.at[1 - mc], send_sem=send_sem, recv_sem=recv_sem,
                                     device_id=(mx, my, 1 - mc), device_id_type=MESH).wait_recv()
        cp.wait_send()

    out = pl.pallas_call(
        body, name=name, in_specs=[ANY], out_specs=ANY, out_shape=jax.ShapeDtypeStruct((2,) + v.shape, v.dtype),
        scratch_shapes=[pltpu.SemaphoreType.DMA(()), pltpu.SemaphoreType.DMA(())],
    )(v)
    return lax.dynamic_update_slice(out, v[None], (_place()[2], 0, 0))


def gather_ffn(wl, *, name):
    nl, nblk, cs, d = wl.shape
    assert nl == 2
    flips = [(1, 0), (0, 1), (1, 1)]

    def body(v_ref, out_ref, send_sems, recv_sems):
        mx, my, mc = _place()
        me = 2 * mx + my
        sib = (mx, my, 1 - mc)

        def copy(k, src, dst, to):
            return pltpu.make_async_remote_copy(src_ref=src, dst_ref=dst, send_sem=send_sems.at[k], recv_sem=recv_sems.at[k],
                                                device_id=to, device_id_type=MESH)

        first = [copy(j, v_ref.at[mc], out_ref.at[mc, me], (mx ^ dx, my ^ dy, mc)) for j, (dx, dy) in enumerate(flips)]
        for cp in first:
            cp.start()
        passed = []
        for j, (dx, dy) in enumerate(flips):
            theirs = out_ref.at[mc, 2 * (mx ^ dx) + (my ^ dy)]
            copy(j, v_ref.at[mc], theirs, (mx ^ dx, my ^ dy, mc)).wait_recv()
            fw = copy(3 + j, theirs, theirs, sib)
            fw.start()
            passed.append(fw)
        for j, (dx, dy) in enumerate(flips):
            other = out_ref.at[1 - mc, 2 * (mx ^ dx) + (my ^ dy)]
            copy(3 + j, other, other, sib).wait_recv()
        for cp in first + passed:
            cp.wait_send()

    out = pl.pallas_call(
        body, name=name, in_specs=[ANY], out_specs=ANY, out_shape=jax.ShapeDtypeStruct((nl, 4, nblk, cs, d), wl.dtype),
        scratch_shapes=[pltpu.SemaphoreType.DMA((6,)), pltpu.SemaphoreType.DMA((6,))],
    )(wl)
    mx, my, _ = _place()
    return lax.dynamic_update_slice(out, wl[:, None], (0, 2 * mx + my, 0, 0, 0))


def reduce_ffn(g0, g1, *, name):
    nt = len(g0)
    nsh, cs, d = g0[0].shape
    flips = [(1, 0), (0, 1), (1, 1)]
    mx, my, mc = _place()
    me = 2 * mx + my
    c_idx = jnp.reshape(mc, (1,)).astype(jnp.int32)

    def pair_body(*refs):
        ins0, ins1, outs = refs[:nt], refs[nt:2 * nt], refs[2 * nt:3 * nt]
        send_sems, recv_sems = refs[3 * nt:]
        kx, ky, kc = _place()
        sib = (kx, ky, 1 - kc)
        for c in range(2):
            @pl.when(kc == c)
            def _(c=c):
                mine_out = (ins1, ins0)[c]
                cps = [pltpu.make_async_remote_copy(src_ref=mine_out[t], dst_ref=outs[t], send_sem=send_sems.at[t],
                                                    recv_sem=recv_sems.at[t], device_id=sib, device_id_type=MESH) for t in range(nt)]
                for cp in cps:
                    cp.start()
                for cp in cps:
                    cp.wait_recv()
                for cp in cps:
                    cp.wait_send()

    from_pair = pl.pallas_call(
        pair_body, name=name + "_pair", in_specs=[ANY] * (2 * nt), out_specs=[ANY] * nt,
        out_shape=[jax.ShapeDtypeStruct((nsh, cs, d), F32)] * nt,
        scratch_shapes=[pltpu.SemaphoreType.DMA((nt,)), pltpu.SemaphoreType.DMA((nt,))],
    )(*g0, *g1)

    tr = _row_tile(cs, 64)

    def add_body(c_ref, *refs):
        for t in range(nt):
            mine = jnp.where(c_ref[0] == 0, refs[t][...], refs[nt + t][...])
            refs[3 * nt + t][...] = (mine + refs[2 * nt + t][...]).astype(BF16)

    spec = pl.BlockSpec((None, tr, d), lambda s, i, c_ref: (s, i, 0))
    chip_sum = pl.pallas_call(
        add_body, name=name + "_pair_add",
        grid_spec=pltpu.PrefetchScalarGridSpec(num_scalar_prefetch=1, grid=(nsh, cs // tr), in_specs=[spec] * (3 * nt),
                                               out_specs=[spec] * nt),
        out_shape=[jax.ShapeDtypeStruct((nsh, cs, d), BF16)] * nt, compiler_params=_cp("parallel", "parallel"),
    )(c_idx, *g0, *g1, *from_pair)

    def xy_body(*refs):
        ins, outs = refs[:nt], refs[nt:2 * nt]
        send_sems, recv_sems = refs[2 * nt:]
        kx, ky, kc = _place()
        k_me = 2 * kx + ky
        sends = []
        for j, (dx, dy) in enumerate(flips):
            px, py = kx ^ dx, ky ^ dy
            for t in range(nt):
                cp = pltpu.make_async_remote_copy(src_ref=ins[t].at[2 * px + py], dst_ref=outs[t].at[k_me],
                                                  send_sem=send_sems.at[j * nt + t], recv_sem=recv_sems.at[j * nt + t],
                                                  device_id=(px, py, kc), device_id_type=MESH)
                cp.start()
                sends.append(cp)
        for j, (dx, dy) in enumerate(flips):
            px, py = kx ^ dx, ky ^ dy
            for t in range(nt):
                pltpu.make_async_remote_copy(src_ref=ins[t].at[k_me], dst_ref=outs[t].at[2 * px + py],
                                             send_sem=send_sems.at[j * nt + t], recv_sem=recv_sems.at[j * nt + t],
                                             device_id=(px, py, kc), device_id_type=MESH).wait_recv()
        for cp in sends:
            cp.wait_send()

    from_xy = pl.pallas_call(
        xy_body, name=name + "_xy", in_specs=[ANY] * nt, out_specs=[ANY] * nt,
        out_shape=[jax.ShapeDtypeStruct((nsh, cs, d), BF16)] * nt,
        scratch_shapes=[pltpu.SemaphoreType.DMA((3 * nt,)), pltpu.SemaphoreType.DMA((3 * nt,))],
    )(*chip_sum)
    from_xy = [lax.dynamic_update_slice(o, lax.dynamic_slice_in_dim(v, me, 1, axis=0), (me, 0, 0)) for o, v in zip(from_xy, chip_sum)]

    def sum_body(*refs):
        for t in range(nt):
            acc = refs[t][0].astype(F32)
            for s in range(1, nsh):
                acc = acc + refs[t][s].astype(F32)
            refs[nt + t][...] = acc

    reduced = pl.pallas_call(
        sum_body, name=name + "_xy_add", grid=(cs // tr,), in_specs=[pl.BlockSpec((nsh, tr, d), lambda i: (0, i, 0))] * nt,
        out_specs=[pl.BlockSpec((tr, d), lambda i: (i, 0))] * nt, out_shape=[jax.ShapeDtypeStruct((cs, d), F32)] * nt,
        compiler_params=_cp("parallel"),
    )(*from_xy)

    def share_body(*refs):
        ins, outs = refs[:nt], refs[nt:2 * nt]
        send_sems, recv_sems = refs[2 * nt:]
        kx, ky, kc = _place()
        sib = (kx, ky, 1 - kc)
        cps = [pltpu.make_async_remote_copy(src_ref=ins[t], dst_ref=outs[t].at[kc], send_sem=send_sems.at[t],
                                            recv_sem=recv_sems.at[t], device_id=sib, device_id_type=MESH) for t in range(nt)]
        for cp in cps:
            cp.start()
        for t in range(nt):
            pltpu.make_async_remote_copy(src_ref=ins[t], dst_ref=outs[t].at[1 - kc], send_sem=send_sems.at[t],
                                         recv_sem=recv_sems.at[t], device_id=sib, device_id_type=MESH).wait_recv()
        for cp in cps:
            cp.wait_send()

    both = pl.pallas_call(
        share_body, name=name + "_share", in_specs=[ANY] * nt, out_specs=[ANY] * nt,
        out_shape=[jax.ShapeDtypeStruct((2, cs, d), F32)] * nt,
        scratch_shapes=[pltpu.SemaphoreType.DMA((nt,)), pltpu.SemaphoreType.DMA((nt,))],
    )(*reduced)
    return [lax.dynamic_update_slice(o, v[None], (mc, 0, 0)) for o, v in zip(both, reduced)]


def add_kept_half(g, r, c_idx, *, name, out_dtype):
    n, _, h, w = g.shape
    th = _row_tile(h)

    def body(c_ref, g_ref, r_ref, o_ref):
        o_ref[...] = (g_ref[...] + r_ref[...]).astype(o_ref.dtype)

    return pl.pallas_call(
        body, name=name,
        grid_spec=pltpu.PrefetchScalarGridSpec(
            num_scalar_prefetch=1, grid=(n, h // th),
            in_specs=[pl.BlockSpec((None, None, th, w), lambda s, i, c_ref: (s, c_ref[0], i, 0)),
                      pl.BlockSpec((None, th, w), lambda s, i, c_ref: (s, i, 0))],
            out_specs=pl.BlockSpec((None, th, w), lambda s, i, c_ref: (s, i, 0))),
        out_shape=jax.ShapeDtypeStruct((n, h, w), out_dtype), compiler_params=_cp("parallel", "parallel"),
    )(c_idx, g, r)


def sum_slots(v, *, name):
    n, rows, w = v.shape
    tr = _row_tile(rows, 256)

    def body(v_ref, o_ref):
        acc = v_ref[0].astype(F32)
        for s in range(1, n):
            acc = acc + v_ref[s].astype(F32)
        o_ref[...] = acc

    return pl.pallas_call(body, name=name, grid=(rows // tr,), in_specs=[pl.BlockSpec((n, tr, w), lambda i: (0, i, 0))],
                          out_specs=pl.BlockSpec((tr, w), lambda i: (i, 0)), out_shape=jax.ShapeDtypeStruct((rows, w), F32),
                          compiler_params=_cp("parallel"))(v)


def ada_fwd(c_rows, w_ada, b_shard, *, name):
    nl, d, ncol = w_ada.shape
    rows = c_rows.shape[0]
    tn = _tile(ncol, (768, 512, 256, 128))

    def body(c_ref, w_ref, b_ref, o_ref):
        o_ref[...] = jnp.dot(jax.nn.silu(c_ref[...]), w_ref[...], precision=HI, preferred_element_type=F32) + b_ref[...]

    return pl.pallas_call(
        body, name=name, grid=(nl, ncol // tn),
        in_specs=[pl.BlockSpec((rows, d), lambda l, j: (0, 0)), pl.BlockSpec((None, d, tn), lambda l, j: (l, 0, j)),
                  pl.BlockSpec((None, 1, tn), lambda l, j: (l, 0, j))],
        out_specs=pl.BlockSpec((None, rows, tn), lambda l, j: (l, 0, j)),
        out_shape=jax.ShapeDtypeStruct((nl, rows, ncol), F32), compiler_params=_cp("parallel", "parallel"),
    )(c_rows, w_ada, b_shard)


def ada_bwd(c_rows, w_ada, dm_shard, dm_full, n_ex, *, name):
    nl, d, ncol = w_ada.shape
    rows = c_rows.shape[0]
    tn = _tile(ncol, (768, 512, 256, 128))
    nj = ncol // tn

    def body(c_ref, w_ref, dm_ref, dmf_ref, gw_ref, gb_ref, dc_ref, dact_ref):
        l, j = pl.program_id(0), pl.program_id(1)
        act, act_vjp = jax.vjp(jax.nn.silu, c_ref[...])
        gw_ref[...] = lax.dot_general(act, dm_ref[...], _TN, precision=HI, preferred_element_type=F32)
        gb_ref[...] = jnp.sum(dmf_ref[...], axis=0, keepdims=True)
        part = lax.dot_general(dm_ref[...], w_ref[...], _NT, precision=HI, preferred_element_type=F32)

        @pl.when((l == 0) & (j == 0))
        def _():
            dact_ref[...] = part

        @pl.when((l > 0) | (j > 0))
        def _():
            dact_ref[...] += part

        @pl.when((l == nl - 1) & (j == nj - 1))
        def _():
            dc, = act_vjp(dact_ref[...])
            dc_ref[...] = jnp.sum(dc[n_ex:, :], axis=0, keepdims=True)

    return pl.pallas_call(
        body, name=name, grid=(nl, nj),
        in_specs=[pl.BlockSpec((rows, d), lambda l, j: (0, 0)), pl.BlockSpec((None, d, tn), lambda l, j: (l, 0, j)),
                  pl.BlockSpec((None, rows, tn), lambda l, j: (l, 0, j)),
                  pl.BlockSpec((None, rows, dm_full.shape[-1]), lambda l, j: (l, 0, 0))],
        out_specs=[pl.BlockSpec((None, d, tn), lambda l, j: (l, 0, j)),
                   pl.BlockSpec((None, 1, dm_full.shape[-1]), lambda l, j: (l, 0, 0)),
                   pl.BlockSpec((1, d), lambda l, j: (0, 0))],
        out_shape=[jax.ShapeDtypeStruct((nl, d, ncol), F32), jax.ShapeDtypeStruct((nl, 1, dm_full.shape[-1]), F32),
                   jax.ShapeDtypeStruct((1, d), F32)],
        scratch_shapes=[pltpu.VMEM((rows, d), F32)], compiler_params=_cp("arbitrary", "arbitrary"),
    )(c_rows, w_ada, dm_shard, dm_full)


def adamw(w, g, m, v, *, name):
    shape = w.shape
    cols = shape[-1]
    rows = int(np.prod(shape[:-1])) if len(shape) > 1 else 1
    tr = _row_tile(rows, 256)

    def body(w_ref, g_ref, m_ref, v_ref, d_ref, nm_ref, nv_ref):
        gg = g_ref[...]
        nm = ADAM_B1 * m_ref[...] + (1.0 - ADAM_B1) * gg
        nv = ADAM_B2 * v_ref[...] + (1.0 - ADAM_B2) * jnp.square(gg)
        m_hat = nm / (1.0 - ADAM_B1 ** ADAM_STEP)
        v_hat = nv / (1.0 - ADAM_B2 ** ADAM_STEP)
        d_ref[...] = -ADAM_LR * (m_hat / (jnp.sqrt(v_hat) + ADAM_EPS) + ADAM_WD * w_ref[...])
        nm_ref[...] = nm
        nv_ref[...] = nv

    spec = pl.BlockSpec((tr, cols), lambda i: (i, 0))
    out = pl.pallas_call(body, name=name, grid=(rows // tr,), in_specs=[spec] * 4, out_specs=[spec] * 3,
                         out_shape=[jax.ShapeDtypeStruct((rows, cols), F32)] * 3, compiler_params=_cp("parallel"),
                         )(*[a.reshape(rows, cols) for a in (w, g, m, v)])
    return tuple(o.reshape(shape) for o in out)


def local_step(h0, target, mods, lw, wf, small, *, lc):
    nb, t, d = h0.shape
    nt, nct = t // TM, lc // TM
    s_len = t - lc
    nl = len(lw)
    nsh = wf.shape[1]
    consts = _post_consts()
    cos_b, sin_b = _rope_tables(s_len, lc, HEAD_DIM, GQA_HEADS)
    cos_m, sin_m = _rope_tables(s_len, lc, MLA_ROPE, MLA_HEADS)
    rc = functools.partial(rowcall, nb=nb, nt=nt, nct=nct)
    flat = lambda a: a.reshape(nb * t, a.shape[-1])
    unflat = lambda a: a.reshape(nb, t, a.shape[-1])
    vec = lambda a: a.reshape(1, -1)

    def norm_first(h, g, shift, scale, tag):
        n, = rc(tag + "_norm", lambda _, *a: (f_normmod(*a),), [(h, 'tok'), (vec(g), 'full'), (shift, 'mod'), (scale, 'mod')],
                [('tok', d, BF16)])
        return n

    def res_norm(h, y, gate, coef, g, shift, scale, tag):
        def fn(_, hh, yy, gt, gn, sh, sc):
            h2 = hh + coef * gt * yy
            return h2, f_normmod(h2, gn, sh, sc)

        return rc(tag + "_res_norm", fn, [(h, 'tok'), (y, 'tok'), (gate, 'mod'), (vec(g), 'full'), (shift, 'mod'), (scale, 'mod')],
                  [('tok', d, F32), ('tok', d, BF16)])

    def res_last(h, y, gate, coef, tag):
        h2, = rc(tag + "_res", lambda _, hh, yy, gt: (hh + coef * gt * yy,), [(h, 'tok'), (y, 'tok'), (gate, 'mod')], [('tok', d, F32)])
        return h2

    def res_bwd_last(dh2, y, gate, coef, tag):
        return rc(tag + "_res_bwd", lambda _, dd, yy, gt: (coef * gt * dd, jnp.sum(coef * yy * dd, axis=0, keepdims=True)),
                  [(dh2, 'tok'), (y, 'tok'), (gate, 'mod')], [('tok', d, BF16), ('mod', d)])

    def norm_bwd_first(h, g, shift, scale, dn, dres, tag):
        def fn(_, hh, gn, sh, sc, dnn, dr):
            dh, dg, dsh, dsc = jax.vjp(f_normmod, hh, gn, sh, sc)[1](dnn)
            return dh + dr, dg, dsh, dsc

        return rc(tag + "_norm_bwd", fn, [(h, 'tok'), (vec(g), 'full'), (shift, 'mod'), (scale, 'mod'), (dn, 'tok'), (dres, 'tok')],
                  [('tok', d, F32), ('full', (1, d)), ('mod', d), ('mod', d)])

    def norm_bwd_res_bwd(h, g, shift, scale, dn, dres, y_prev, gate_prev, coef_prev, tag):
        def fn(_, hh, gn, sh, sc, dnn, dr, yy, gt):
            dh, dg, dsh, dsc = jax.vjp(f_normmod, hh, gn, sh, sc)[1](dnn)
            dh = dh + dr
            return dh, dg, dsh, dsc, coef_prev * gt * dh, jnp.sum(coef_prev * yy * dh, axis=0, keepdims=True)

        return rc(tag + "_norm_bwd", fn,
                  [(h, 'tok'), (vec(g), 'full'), (shift, 'mod'), (scale, 'mod'), (dn, 'tok'), (dres, 'tok'), (y_prev, 'tok'), (gate_prev, 'mod')],
                  [('tok', d, F32), ('full', (1, d)), ('mod', d), ('mod', d), ('tok', d, BF16), ('mod', d)])

    def ffn_fwd(n, l, base, tag):
        gg, uu, act = ffn_up(flat(n), wf, l, base, name=tag + "_up")
        return unflat(ffn_down(act, wf, l, base, name=tag + "_down")), (n, gg, uu, act)

    def ffn_bwd(dy, saved, l, base, tag):
        n, gg, uu, act = saved
        dw_d = ffn_dw(act, flat(dy), nsh, name=tag + "_down_dw")
        dgg, duu = ffn_down_bwd(flat(dy), gg, uu, wf, l, base, name=tag + "_down_dx")
        dw_g = ffn_dw(dgg, flat(n), nsh, name=tag + "_gate_dw")
        dw_u = ffn_dw(duu, flat(n), nsh, name=tag + "_up_dw")
        return unflat(ffn_up_bwd(dgg, duu, wf, l, base, name=tag + "_up_dx")), [dw_g, dw_u, dw_d]

    def post_ins(p, sm, w):
        return [(p, ('tokc', MAIN_PAD, 0)), (cos_b, 'pos'), (sin_b, 'pos'), (cos_m, 'pos'), (sin_m, 'pos'),
                (vec(sm['gqa_q_norm']), 'full'), (vec(sm['gqa_k_norm']), 'full'), (vec(sm['mla_q_norm']), 'full'),
                (vec(sm['mla_kv_norm']), 'full'), (w['w_uq'], 'full'), (w['w_ukv'], 'full')] + [(c, 'full') for c in consts]

    def mix_fwd(n, sm, w, ctx_q, tag):
        p = unflat(mm_resident(flat(n), w['w_in'], out_dtype=BF16, name=tag + "_in"))
        parts = rc(tag + "_post", lambda _, pp, *a: f_post(pp.astype(F32), *a), post_ins(p, sm, w),
                   [('tok', wd, BF16) for wd in POST_WIDTHS])
        aq, ak, av, bq, bk, bv, mq, mk, mv = parts
        bias = na_expand_bias(sm['na_rel_bias'], tag + "_bias")
        o_a, lse_a = na_fwd(aq, ak, av, bias, lc=lc, ctx_q=ctx_q, name=tag + "_na")
        o_b, lse_b = gqa_fwd(bq, bk, bv, lc=lc, ctx_q=ctx_q, name=tag + "_gqa")
        o_m, lse_m = mla_fwd(mq, mk, mv, lc=lc, ctx_q=ctx_q, name=tag + "_mla")
        fo = [o_a, o_b, o_m]
        ys = [unflat(mm_resident(flat(o), w[k], out_dtype=BF16, name=tag + "_br" + k[-1])) for o, k in zip(fo, ('w_a', 'w_b', 'w_c'))]
        gcols = [(p, ('tokc', d, MAIN_PAD // d + j)) for j in range(3)]
        y, = rc(tag + "_merge", lambda _, *a: (f_merge(*[v.astype(F32) for v in a]),), gcols + [(v, 'tok') for v in ys],
                [('tok', d, BF16)])
        z = unflat(mm_resident(flat(y), w['w_o'], name=tag + "_out"))
        saved = (n, p, (aq, ak, av, lse_a, bias), (bq, bk, bv, lse_b), (mq, mk, mv, lse_m), fo, ys, y)
        return z, saved

    def mix_bwd(dz, saved, sm, w, ctx_q, tag):
        n, p, (aq, ak, av, lse_a, bias), (bq, bk, bv, lse_b), (mq, mk, mv, lse_m), fo, ys, y = saved
        dw_o = mm(flat(y), flat(dz), ta=True, name=tag + "_out_dw")
        dy = unflat(mm_resident(flat(dz), w['w_o'], tb=True, name=tag + "_out_dx"))
        gcols = [(p, ('tokc', d, MAIN_PAD // d + j)) for j in range(3)]

        def merge_bwd(_, ga, gb, gm, ya, yb, ym, dyy):
            dga, dgb, dgm, dya, dyb, dym = jax.vjp(f_merge, *[v.astype(F32) for v in (ga, gb, gm, ya, yb, ym)])[1](dyy)
            return dya, dyb, dym, jnp.concatenate([dga, dgb, dgm], axis=-1)

        dya, dyb, dym, dgl = rc(tag + "_merge_bwd", merge_bwd, gcols + [(v, 'tok') for v in ys] + [(dy, 'tok')],
                                [('tok', d, BF16)] * 3 + [('tok', 3 * d, BF16)])
        dws, dos = {}, []
        for o, dyk, k in zip(fo, (dya, dyb, dym), ('w_a', 'w_b', 'w_c')):
            dws[k] = mm(flat(o), flat(dyk), ta=True, name=tag + "_br" + k[-1] + "_dw")
            dos.append(unflat(mm_resident(flat(dyk), w[k], tb=True, out_dtype=BF16, name=tag + "_br" + k[-1] + "_dx")))
        do_a, do_b, do_m = dos
        daq, dak, dav, dbias = na_bwd(aq, ak, av, bias, lse_a, do_a, lc=lc, ctx_q=ctx_q, name=tag + "_na_bwd")
        dbq, dbk, dbv = gqa_bwd(bq, bk, bv, lse_b, do_b, lc=lc, ctx_q=ctx_q, name=tag + "_gqa_bwd")
        dmq, dmk, dmv = mla_bwd(mq, mk, mv, lse_m, do_m, lc=lc, ctx_q=ctx_q, name=tag + "_mla_bwd")
        d_rel = na_reduce_bias(dbias, tag + "_relb")
        cots = [daq, dak, dav, dbq, dbk, dbv, dmq, dmk, dmv]
        ins = post_ins(p, sm, w)
        n_in = len(ins)

        def post_bwd(_, *a):
            prim, cot, dgl_v = a[:11], list(a[n_in:n_in + N_POST]), a[-1]
            for j in POST_QK:
                cot[j] = cot[j] * LN2
            outs = jax.vjp(lambda pp, qn, kn, mqn, mkvn, wuq, wukv: f_post(pp, *prim[1:5], qn, kn, mqn, mkvn, wuq, wukv, *a[11:n_in]),
                           prim[0].astype(F32), *prim[5:11])[1](tuple(cot))
            return (jnp.concatenate([outs[0].astype(BF16), dgl_v], axis=-1),) + tuple(outs[1:])

        res = rc(tag + "_post_bwd", post_bwd, ins + [(cv, 'tok') for cv in cots] + [(dgl, 'tok')],
                 [('tok', MAIN_PAD + 3 * d, BF16), ('full', (1, HEAD_DIM)), ('full', (1, HEAD_DIM)), ('full', (1, MLA_Q_RANK)),
                  ('full', (1, MLA_KV_RANK)), ('full', w['w_uq'].shape), ('full', w['w_ukv'].shape)])
        dp, dqn, dkn, dmqn, dmkvn, dw_uq, dw_ukv = res
        dw_in = ffn_dw(flat(dp), flat(n), 4, name=tag + "_in_dw").reshape(-1, d)
        dn = unflat(mm_resident(flat(dp), w['w_in'], tb=True, name=tag + "_in_dx"))
        dsm = {'na_rel_bias': d_rel, 'gqa_q_norm': dqn.reshape(-1), 'gqa_k_norm': dkn.reshape(-1),
               'mla_q_norm': dmqn.reshape(-1), 'mla_kv_norm': dmkvn.reshape(-1)}
        dwl = {'w_in': dw_in, 'w_uq': dw_uq, 'w_ukv': dw_ukv, 'w_o': dw_o, **dws}
        return dn, dsm, dwl

    subs = [(l, kind, gain, coef) for l in range(nl)
            for kind, gain, coef in (('ffn1', 'ffn1_norm', 0.5), ('mix', 'mix_norm', 1.0), ('ffn2', 'ffn2_norm', 0.5))]
    ns = len(subs)
    sms = [{k: small[k][l] for k in SMALL_LAYER} for l in range(nl)]

    def params(k):
        l, _, gain, _ = subs[k]
        j = 3 * (k % 3)
        return small[gain][l], mods[l][j], mods[l][j + 1], mods[l][j + 2]

    def tag_of(k):
        return f"l{subs[k][0]}_{subs[k][1]}"

    h = h0
    g0, sh0, sc0, _ = params(0)
    n = norm_first(h, g0, sh0, sc0, tag_of(0))
    h_in, core_out, saved = [None] * ns, [None] * ns, [None] * ns
    for k, (l, kind, _, coef) in enumerate(subs):
        h_in[k] = h
        if kind == 'mix':
            core_out[k], saved[k] = mix_fwd(n, sms[l], lw[l], l + 1 < nl, tag_of(k))
        else:
            core_out[k], saved[k] = ffn_fwd(n, l, 0 if kind == 'ffn1' else 3, tag_of(k))
        gate = params(k)[3]
        if k + 1 < ns:
            gn, shn, scn, _ = params(k + 1)
            h, n = res_norm(h, core_out[k], gate, coef, gn, shn, scn, tag_of(k))
        else:
            h = res_last(h, core_out[k], gate, coef, tag_of(k))

    def final(is_ctx, hh, gg, tgt):
        def loss_fn(hv, gv):
            return 0.5 * jnp.sum(jnp.mean(jnp.square(_rms(hv, gv) - tgt), axis=-1))

        keep = jnp.where(is_ctx, 0.0, 1.0)
        loss, (dh, dg) = jax.value_and_grad(loss_fn, argnums=(0, 1))(hh, gg)
        return dh * keep, jnp.full((1, LANE), loss * keep, F32), dg * keep

    dh, loss, dg_final = rc("final_loss", final, [(h, 'tok'), (vec(small['final_norm']), 'full'), (target, 'lat')],
                            [('tok', d, F32), ('full', (1, LANE)), ('full', (1, d))])

    dsmall = {k: [None] * nl for k in SMALL_LAYER}
    dmods, dlw, dwf = [[None] * N_MOD for _ in range(nl)], [None] * nl, [[None] * 6 for _ in range(nl)]
    l_last, _, _, coef_last = subs[-1]
    dcore, dmods[l_last][8] = res_bwd_last(dh, core_out[-1], params(ns - 1)[3], coef_last, tag_of(ns - 1))
    for k in reversed(range(ns)):
        l, kind, gain, _ = subs[k]
        j = 3 * (k % 3)
        if kind == 'mix':
            dn, dsm, dlw[l] = mix_bwd(dcore, saved[k], sms[l], lw[l], l + 1 < nl, tag_of(k))
            for name, val in dsm.items():
                dsmall[name][l] = val
        else:
            base = 0 if kind == 'ffn1' else 3
            dn, dwf[l][base:base + 3] = ffn_bwd(dcore, saved[k], l, base, tag_of(k))
        g, shift, scale, _ = params(k)
        if k > 0:
            lp, _, _, coef_prev = subs[k - 1]
            dh, dg, dmods[l][j], dmods[l][j + 1], dcore, dmods[lp][3 * ((k - 1) % 3) + 2] = norm_bwd_res_bwd(
                h_in[k], g, shift, scale, dn, dh, core_out[k - 1], params(k - 1)[3], coef_prev, tag_of(k))
        else:
            dh, dg, dmods[l][j], dmods[l][j + 1] = norm_bwd_first(h_in[k], g, shift, scale, dn, dh, tag_of(k))
        dsmall[gain][l] = dg.reshape(d)
    dsmall = {k: jnp.stack(v) for k, v in dsmall.items()}
    dsmall['final_norm'] = dg_final.reshape(d)
    return loss, dh, dmods, dlw, dwf, dsmall


def _pack(parts, pad_rows):
    flat, where, off = [], [], 0
    for a in parts:
        n = _ceil_to(a.size, PACK_W)
        flat.append(jnp.pad(a.reshape(-1), (0, n - a.size)))
        where.append((off, n // PACK_W))
        off += n // PACK_W
    total = _ceil_to(off, pad_rows)
    if total > off:
        flat.append(jnp.zeros(((total - off) * PACK_W,), flat[0].dtype))
    return jnp.concatenate(flat).reshape(total, PACK_W), where


def _unpack(buf, where, shape):
    off, rows = where
    return buf[off:off + rows].reshape(-1)[:int(np.prod(shape))].reshape(shape)


def layer_weights(full, l):
    wi = full['w_in'][l]
    d = wi.shape[0]
    return {
        'w_in': jnp.concatenate([wi[:, :MAIN_W], jnp.zeros((d, MAIN_PAD - MAIN_W), wi.dtype), wi[:, MAIN_W:]], axis=1),
        'w_uq': _heads_to_parts(full['mla_w_uq'][l], MLA_NOPE).astype(F32),
        'w_ukv': _heads_to_parts(full['mla_w_ukv'][l], MLA_NOPE).astype(F32),
        'w_a': full['w_branch_a'][l], 'w_b': full['w_branch_b'][l], 'w_c': full['w_branch_c'][l], 'w_o': full['w_out'][l]}


def layer_grads_by_name(dlw):
    per_name = {k: [] for k, _ in BIG}
    for g in dlw:
        per_name['w_in'].append(jnp.concatenate([g['w_in'][:MAIN_W], g['w_in'][MAIN_PAD:]], axis=0))
        per_name['mla_w_uq'].append(_parts_to_heads(g['w_uq'], MLA_NOPE))
        per_name['mla_w_ukv'].append(_parts_to_heads(g['w_ukv'], MLA_NOPE))
        per_name['w_branch_a'].append(g['w_a'])
        per_name['w_branch_b'].append(g['w_b'])
        per_name['w_branch_c'].append(g['w_c'])
        per_name['w_out'].append(g['w_o'])
    return per_name


def kernel(x, c, ctx, c_ctx, w_ada, b_ada, ffn1_norm, ffn1_w_gate, ffn1_w_up, ffn1_w_down, mix_norm, w_in, na_rel_bias, gqa_q_norm, gqa_k_norm, mla_q_norm, mla_kv_norm, mla_w_uq, mla_w_ukv, w_branch_a, w_branch_b, w_branch_c, w_out, ffn2_norm, ffn2_w_gate, ffn2_w_up, ffn2_w_down, final_norm, loss_target, m_c_ctx, m_w_ada, m_b_ada, m_ffn1_norm, m_ffn1_w_gate, m_ffn1_w_up, m_ffn1_w_down, m_mix_norm, m_w_in, m_na_rel_bias, m_gqa_q_norm, m_gqa_k_norm, m_mla_q_norm, m_mla_kv_norm, m_mla_w_uq, m_mla_w_ukv, m_w_branch_a, m_w_branch_b, m_w_branch_c, m_w_out, m_ffn2_norm, m_ffn2_w_gate, m_ffn2_w_up, m_ffn2_w_down, m_final_norm, v_c_ctx, v_w_ada, v_b_ada, v_ffn1_norm, v_ffn1_w_gate, v_ffn1_w_up, v_ffn1_w_down, v_mix_norm, v_w_in, v_na_rel_bias, v_gqa_q_norm, v_gqa_k_norm, v_mla_q_norm, v_mla_kv_norm, v_mla_w_uq, v_mla_w_ukv, v_w_branch_a, v_w_branch_b, v_w_branch_c, v_w_out, v_ffn2_norm, v_ffn2_w_gate, v_ffn2_w_up, v_ffn2_w_down, v_final_norm):
    args = locals()
    wts = {k: args[k] for k in WEIGHTS}
    mom = {k: args['m_' + k] for k in WEIGHTS}
    var = {k: args['v_' + k] for k in WEIGHTS}
    nb, s_len, d = x.shape
    lc = ctx.shape[1]
    nl = w_ada.shape[0]
    nsh, ndev = 4, 8
    mx, my, mc = _place()
    sidx = 2 * mx + my
    didx = 4 * mx + 2 * my + mc
    assert d % LANE == 0 and MAIN_PAD % d == 0 and lc % TQ == 0 and s_len % TQ == 0 and s_len // GRID_W >= NA_ROWS

    wpack, wwhere = _pack([wts[k].astype(BF16) for k, _ in BIG], 32)
    wall = gather_shards(wpack.reshape(2, -1, PACK_W), name="gather_weights").reshape(nsh, -1, PACK_W)
    full = {}
    for (k, ax), wh in zip(BIG, wwhere):
        shp = wts[k].shape
        parts = jnp.stack([_unpack(wall[s], wh, shp) for s in range(nsh)])
        if ax == 1:
            full[k] = parts.transpose(1, 2, 0, 3).reshape(nl, shp[1], nsh * shp[2])
        else:
            full[k] = parts.transpose(1, 0, 2, 3).reshape(nl, nsh * shp[1], shp[2])
    lw = [layer_weights(full, l) for l in range(nl)]
    wl = jnp.stack([(wts[k].transpose(0, 2, 1) if tr else wts[k]).astype(BF16) for k, tr in zip(FFN_NAMES, FFN_TRANSPOSED)], axis=1)
    wf = gather_ffn(wl, name="gather_ffn")

    n_ex = ndev * nb
    ncol = w_ada.shape[-1]
    c_all = all_gather(c, name="gather_cond", with_c=True).reshape(n_ex, d)
    c_rows = jnp.concatenate([c_all, jnp.broadcast_to(c_ctx[None], (n_ex, d))], axis=0)
    b_shard = lax.dynamic_slice_in_dim(b_ada, sidx * ncol, ncol, axis=1)[:, None, :]
    mod_sh = ada_fwd(c_rows, w_ada, b_shard, name="ada_fwd")
    mod_all = all_gather(mod_sh, name="gather_mod", with_c=False)
    mod_all = mod_all.transpose(1, 2, 0, 3).reshape(nl, 2 * n_ex, nsh * ncol)
    mod_x = lax.dynamic_slice_in_dim(mod_all, didx * nb, nb, axis=1)
    mod_c = jnp.broadcast_to(mod_all[:, n_ex:n_ex + 1], mod_x.shape)
    mods = [[jnp.stack([mod_c[l, :, j * d:(j + 1) * d], mod_x[l, :, j * d:(j + 1) * d]], axis=1)[:, :, None, :]
             for j in range(N_MOD)] for l in range(nl)]

    small = {k: wts[k] for k in SMALL_LAYER + ['final_norm']}
    h0 = jnp.concatenate([ctx, x], axis=1)
    loss_part, dh0, dmods, dlw, dwf, dsmall = local_step(h0, loss_target, mods, lw, wf, small, lc=lc)
    grad_x = dh0[:, lc:]

    dmod_mine = jnp.stack([jnp.concatenate([m[:, :, 0, :] for m in dmods[l]], axis=-1) for l in range(nl)])
    small_names = SMALL_LAYER + ['final_norm']
    spack, swhere = _pack([loss_part] + [dsmall[k] for k in small_names] + [dmod_mine], 8)
    sall = all_gather(spack, name="gather_small", with_c=True)
    ssum = sum_slots(sall, name="sum_small")
    loss = _unpack(ssum, swhere[0], (1, LANE))[0, 0]
    grads = {k: _unpack(ssum, wh, wts[k].shape) for k, wh in zip(small_names, swhere[1:])}
    off, rows = swhere[-1]
    dm_all = sall[:, off:off + rows].reshape(ndev, -1)[:, :dmod_mine.size].reshape((ndev,) + dmod_mine.shape)
    dm_all = dm_all.transpose(1, 3, 0, 2, 4).reshape(nl, 2, n_ex, N_MOD * d)
    dm_rows = jnp.concatenate([dm_all[:, 1], dm_all[:, 0]], axis=1)
    dm_shard = lax.dynamic_slice_in_dim(dm_rows, sidx * ncol, ncol, axis=2)
    grads['w_ada'], gb, dc_part = ada_bwd(c_rows, w_ada, dm_shard, dm_rows, n_ex, name="ada_bwd")
    grads['b_ada'] = gb.reshape(b_ada.shape)
    dc_all = all_gather(jnp.pad(dc_part, ((0, 7), (0, 0))), name="gather_dcond", with_c=False)
    grads['c_ctx'] = sum_slots(dc_all, name="sum_dcond")[0]

    per_name = layer_grads_by_name(dlw)
    pieces, gwhere, off = [], [], 0
    for k, ax in BIG:
        shp = wts[k].shape
        for g in per_name[k]:
            if ax == 1 and k not in GRAD_TRANSPOSED:
                pieces.append(g.reshape(shp[1], nsh, shp[2]).transpose(1, 0, 2).reshape(nsh, -1))
            else:
                pieces.append(g.reshape(nsh, -1))
        n = int(np.prod(shp))
        if n % PACK_W:
            pieces.append(jnp.zeros((nsh, _ceil_to(n, PACK_W) - n), F32))
        gwhere.append((off, _ceil_to(n, PACK_W) // PACK_W))
        off += _ceil_to(n, PACK_W) // PACK_W
    if off % 128:
        pieces.append(jnp.zeros((nsh, (_ceil_to(off, 128) - off) * PACK_W), F32))
    half = _ceil_to(off, 128) // 2
    gpack = jnp.concatenate(pieces, axis=1).reshape(nsh, 2, half, PACK_W)
    from_pair = pair_exchange_halves(gpack, name="reduce_pair")
    chip_sum = add_kept_half(gpack, from_pair, jnp.reshape(mc, (1,)).astype(jnp.int32), name="reduce_pair_add",
                             out_dtype=BF16)
    from_xy = all_to_all_xy(chip_sum, name="reduce_xy")
    reduced = sum_slots(from_xy, name="reduce_xy_add")
    gfull = pair_all_gather(reduced, name="reduce_share").reshape(2 * half, PACK_W)
    for (k, _), wh in zip(BIG, gwhere):
        shp = wts[k].shape
        grads[k] = (_unpack(gfull, wh, (shp[0], shp[2], shp[1])).transpose(0, 2, 1) if k in GRAD_TRANSPOSED
                    else _unpack(gfull, wh, shp))
    for k, tr, g in zip(FFN_NAMES, FFN_TRANSPOSED, reduce_ffn(dwf[0], dwf[1], name="reduce_ffn")):
        grads[k] = g.transpose(0, 2, 1) if tr else g

    outs = {k: adamw(wts[k], grads[k], mom[k], var[k], name="adamw_" + k) for k in WEIGHTS}
    return (loss, grad_x, *[grads[k] for k in WEIGHTS], *[outs[k][0] for k in WEIGHTS], *[outs[k][1] for k in WEIGHTS],
            *[outs[k][2] for k in WEIGHTS])
```
